```python
import math
import jax, jax.numpy as jnp
from jax import lax
import numpy as np

D_MODEL = 2048
BATCH = 4
SEQ = 2048
DEPTH = 1
DEC_BATCH = 128
DEC_SEQ = 1
PAST_LEN = 16384
PAGE_SIZE = 128

D_MIX = D_MODEL
D_S5 = D_MIX // 2
D_LRU = D_MIX - D_S5
S5_CH = 16
S5_GROUPS = D_S5 // S5_CH
S5_STATE = 64
LRU_HEADS = 4
LRU_BLOCK = D_LRU // LRU_HEADS
CONV_W = 4
LRU_C = 8.0
D_FF = 4 * D_MODEL
D_PLE = 256
EPS = 1e-6
F32 = jnp.float32

kernel_name = "hymba_s5_rglru_hybrid_step"


def rmsnorm(x, g):
    xf = x.astype(F32)
    out = xf * lax.rsqrt(jnp.mean(xf * xf, axis=-1, keepdims=True) + EPS) * g.astype(F32)
    return out.astype(x.dtype)


def _complex_affine_combine(e1, e2):
    a1r, a1i, b1r, b1i = e1
    a2r, a2i, b2r, b2i = e2
    return (a2r * a1r - a2i * a1i,
            a2r * a1i + a2i * a1r,
            a2r * b1r - a2i * b1i + b2r,
            a2r * b1i + a2i * b1r + b2i)


def _real_affine_combine(e1, e2):
    a1, b1 = e1
    a2, b2 = e2
    return (a1 * a2, a2 * b1 + b2)


def s5_mixer(u, h0_re, h0_im, lam_re, lam_im, log_step, b_re, b_im, c_re, c_im, d_skip, w_glu, b_glu):
    n, t, _ = u.shape
    uf = u.astype(F32).reshape(n, t, S5_GROUPS, S5_CH)
    step = jnp.exp(log_step.astype(F32))[:, None]
    lr = lam_re.astype(F32)
    li = lam_im.astype(F32)
    mag = jnp.exp(lr * step)
    ar = mag * jnp.cos(li * step)
    ai = mag * jnp.sin(li * step)
    nr, ni = ar - 1.0, ai
    den = lr * lr + li * li
    cr = (nr * lr + ni * li) / den
    ci = (ni * lr - nr * li) / den
    br = b_re.astype(F32)
    bi = b_im.astype(F32)
    bbr = cr[..., None] * br - ci[..., None] * bi
    bbi = cr[..., None] * bi + ci[..., None] * br
    bu_re = jnp.einsum('ntgc,gpc->ntgp', uf, bbr)
    bu_im = jnp.einsum('ntgc,gpc->ntgp', uf, bbi)
    h0r = h0_re.astype(F32)
    h0i = h0_im.astype(F32)
    bu_re = bu_re.at[:, 0].add(ar * h0r - ai * h0i)
    bu_im = bu_im.at[:, 0].add(ar * h0i + ai * h0r)
    a_re = jnp.broadcast_to(ar, bu_re.shape)
    a_im = jnp.broadcast_to(ai, bu_im.shape)
    _, _, h_re, h_im = lax.associative_scan(_complex_affine_combine, (a_re, a_im, bu_re, bu_im), axis=1)
    y = (jnp.einsum('ntgp,gcp->ntgc', h_re, c_re.astype(F32))
         - jnp.einsum('ntgp,gcp->ntgc', h_im, c_im.astype(F32))
         + d_skip.astype(F32).reshape(S5_GROUPS, S5_CH) * uf)
    y = jax.nn.gelu(y.reshape(n, t, D_S5))
    y = y * jax.nn.sigmoid(y @ w_glu.astype(F32) + b_glu.astype(F32))
    return y.astype(u.dtype), h_re[:, -1], h_im[:, -1]


def rglru_mixer(xb, gb, h0, conv0, conv_w, conv_b, w_a, b_a, w_i, b_i, lam):
    n, t, _ = xb.shape
    xpad = jnp.concatenate([conv0.astype(F32), xb.astype(F32)], axis=1)
    cw = conv_w.astype(F32)
    xc = conv_b.astype(F32) + sum(xpad[:, k:k + t] * cw[k] for k in range(CONV_W))
    new_conv = xpad[:, -(CONV_W - 1):]
    xh = xc.reshape(n, t, LRU_HEADS, LRU_BLOCK)
    r = jax.nn.sigmoid(jnp.einsum('nthi,hij->nthj', xh, w_a.astype(F32)).reshape(n, t, D_LRU) + b_a.astype(F32))
    ig = jax.nn.sigmoid(jnp.einsum('nthi,hij->nthj', xh, w_i.astype(F32)).reshape(n, t, D_LRU) + b_i.astype(F32))
    log_a = -LRU_C * r * jax.nn.softplus(-lam.astype(F32))
    a = jnp.exp(log_a)
    mult = jnp.sqrt(-jnp.expm1(2.0 * log_a))
    bterm = mult * (ig * xc)
    bterm = bterm.at[:, 0].add(a[:, 0] * h0.astype(F32))
    _, h = lax.associative_scan(_real_affine_combine, (a, bterm), axis=1)
    out = h * jax.nn.gelu(gb.astype(F32))
    return out.astype(xb.dtype), h[:, -1], new_conv


def decoder_layer(x, p, s5_re, s5_im, lru_h, conv_buf, lw):
    h = rmsnorm(x, lw['g_mix'])
    z = h @ lw['w_in']
    u = z[..., :D_S5]
    xb = z[..., D_S5:D_S5 + D_LRU]
    gb = z[..., D_S5 + D_LRU:]
    ya, s5r, s5i = s5_mixer(u, s5_re, s5_im, lw['s5_lam_re'], lw['s5_lam_im'], lw['s5_log_step'],
                            lw['s5_b_re'], lw['s5_b_im'], lw['s5_c_re'], lw['s5_c_im'], lw['s5_d'],
                            lw['s5_w_glu'], lw['s5_b_glu'])
    yb, lh, cb = rglru_mixer(xb, gb, lru_h, conv_buf, lw['conv_w'], lw['conv_b'], lw['lru_w_a'],
                             lw['lru_b_a'], lw['lru_w_i'], lw['lru_b_i'], lw['lru_lam'])
    merged = jnp.concatenate([rmsnorm(ya, lw['g_merge_a']), rmsnorm(yb, lw['g_merge_b'])], axis=-1)
    x = x + merged @ lw['w_out']
    h2 = rmsnorm(x, lw['g_mlp'])
    x = x + jnp.square(jax.nn.relu(h2 @ lw['w_up'])) @ lw['w_down']
    gate = jax.nn.sigmoid(rmsnorm(x, lw['g_ple']) @ lw['w_ple_gate'])
    x = x + (p.astype(x.dtype) @ lw['w_ple']) * gate
    return x, s5r, s5i, lh, cb


def setup_inputs(seed: int = 0) -> dict:
    key = jax.random.key(seed)
    ks = iter(jax.random.split(key, 48))
    nrm = lambda shape, s: jax.random.normal(next(ks), shape, F32) * s
    uni = lambda shape, lo, hi: jax.random.uniform(next(ks), shape, F32, lo, hi)
    L = DEPTH
    n_idx = jnp.arange(S5_STATE, dtype=F32)
    lam_re = -0.5 + nrm((L, S5_GROUPS, S5_STATE), 0.01)
    lam_im = math.pi * n_idx + nrm((L, S5_GROUPS, S5_STATE), 0.01)
    a0 = uni((L, D_LRU), 0.9, 0.999) ** (1.0 / LRU_C)
    lru_lam = jnp.log(a0) - jnp.log1p(-a0)
    return {
        'x_prompt': nrm((BATCH, SEQ, D_MODEL), 1.0),
        'x_sample': nrm((DEC_BATCH, DEC_SEQ, D_MODEL), 1.0),
        'state_s5_re': nrm((L, DEC_BATCH, S5_GROUPS, S5_STATE), 0.5),
        'state_s5_im': nrm((L, DEC_BATCH, S5_GROUPS, S5_STATE), 0.5),
        'state_lru': nrm((L, DEC_BATCH, D_LRU), 0.5),
        'state_conv': nrm((L, DEC_BATCH, CONV_W - 1, D_LRU), 1.0),
        'p_prompt': nrm((L, BATCH, SEQ, D_PLE), 1.0),
        'p_sample': nrm((L, DEC_BATCH, DEC_SEQ, D_PLE), 1.0),
        'g_mix': 1.0 + nrm((L, D_MODEL), 0.02),
        'w_in': nrm((L, D_MODEL, D_S5 + 2 * D_LRU), D_MODEL ** -0.5),
        's5_lam_re': lam_re,
        's5_lam_im': lam_im,
        's5_log_step': uni((L, S5_GROUPS), math.log(1e-3), math.log(1e-1)),
        's5_b_re': nrm((L, S5_GROUPS, S5_STATE, S5_CH), (2 * S5_CH) ** -0.5),
        's5_b_im': nrm((L, S5_GROUPS, S5_STATE, S5_CH), (2 * S5_CH) ** -0.5),
        's5_c_re': nrm((L, S5_GROUPS, S5_CH, S5_STATE), (2 * S5_STATE) ** -0.5),
        's5_c_im': nrm((L, S5_GROUPS, S5_CH, S5_STATE), (2 * S5_STATE) ** -0.5),
        's5_d': nrm((L, D_S5), 1.0),
        's5_w_glu': nrm((L, D_S5, D_S5), D_S5 ** -0.5),
        's5_b_glu': nrm((L, D_S5), 0.01),
        'conv_w': nrm((L, CONV_W, D_LRU), CONV_W ** -0.5),
        'conv_b': nrm((L, D_LRU), 0.01),
        'lru_w_a': nrm((L, LRU_HEADS, LRU_BLOCK, LRU_BLOCK), LRU_BLOCK ** -0.5),
        'lru_b_a': nrm((L, D_LRU), 0.01),
        'lru_w_i': nrm((L, LRU_HEADS, LRU_BLOCK, LRU_BLOCK), LRU_BLOCK ** -0.5),
        'lru_b_i': nrm((L, D_LRU), 0.01),
        'lru_lam': lru_lam,
        'g_merge_a': 1.0 + nrm((L, D_S5), 0.02),
        'g_merge_b': 1.0 + nrm((L, D_LRU), 0.02),
        'w_out': nrm((L, D_MIX, D_MODEL), D_MIX ** -0.5),
        'g_mlp': 1.0 + nrm((L, D_MODEL), 0.02),
        'w_up': nrm((L, D_MODEL, D_FF), D_MODEL ** -0.5),
        'w_down': nrm((L, D_FF, D_MODEL), D_FF ** -0.5),
        'g_ple': 1.0 + nrm((L, D_MODEL), 0.02),
        'w_ple_gate': nrm((L, D_MODEL, D_MODEL), D_MODEL ** -0.5),
        'w_ple': nrm((L, D_PLE, D_MODEL), D_PLE ** -0.5),
        'g_final': 1.0 + nrm((D_MODEL,), 0.02),
    }


def reference(x_prompt, x_sample, state_s5_re, state_s5_im, state_lru, state_conv, p_prompt, p_sample,
              g_mix, w_in, s5_lam_re, s5_lam_im, s5_log_step, s5_b_re, s5_b_im, s5_c_re, s5_c_im, s5_d,
              s5_w_glu, s5_b_glu, conv_w, conv_b, lru_w_a, lru_b_a, lru_w_i, lru_b_i, lru_lam,
              g_merge_a, g_merge_b, w_out, g_mlp, w_up, w_down, g_ple, w_ple_gate, w_ple, g_final):
    nb = x_prompt.shape[0]
    xp, xs = x_prompt, x_sample
    s5r_p, s5i_p, lru_p, conv_p = [], [], [], []
    s5r_s, s5i_s, lru_s, conv_s = [], [], [], []
    for i in range(DEPTH):
        lw = dict(g_mix=g_mix[i], w_in=w_in[i], s5_lam_re=s5_lam_re[i], s5_lam_im=s5_lam_im[i],
                  s5_log_step=s5_log_step[i], s5_b_re=s5_b_re[i], s5_b_im=s5_b_im[i],
                  s5_c_re=s5_c_re[i], s5_c_im=s5_c_im[i], s5_d=s5_d[i], s5_w_glu=s5_w_glu[i],
                  s5_b_glu=s5_b_glu[i], conv_w=conv_w[i], conv_b=conv_b[i], lru_w_a=lru_w_a[i],
                  lru_b_a=lru_b_a[i], lru_w_i=lru_w_i[i], lru_b_i=lru_b_i[i], lru_lam=lru_lam[i],
                  g_merge_a=g_merge_a[i], g_merge_b=g_merge_b[i], w_out=w_out[i], g_mlp=g_mlp[i],
                  w_up=w_up[i], w_down=w_down[i], g_ple=g_ple[i], w_ple_gate=w_ple_gate[i],
                  w_ple=w_ple[i])
        z_s5 = jnp.zeros((nb, S5_GROUPS, S5_STATE), F32)
        xp, a, b, c, d = decoder_layer(xp, p_prompt[i], z_s5, z_s5, jnp.zeros((nb, D_LRU), F32),
                                       jnp.zeros((nb, CONV_W - 1, D_LRU), F32), lw)
        s5r_p.append(a); s5i_p.append(b); lru_p.append(c); conv_p.append(d)
        xs, a, b, c, d = decoder_layer(xs, p_sample[i], state_s5_re[i], state_s5_im[i], state_lru[i],
                                       state_conv[i], lw)
        s5r_s.append(a); s5i_s.append(b); lru_s.append(c); conv_s.append(d)
    y_prompt = rmsnorm(xp, g_final)
    y_sample = rmsnorm(xs, g_final)
    return (y_prompt, y_sample,
            jnp.stack(s5r_p), jnp.stack(s5i_p), jnp.stack(lru_p), jnp.stack(conv_p),
            jnp.stack(s5r_s), jnp.stack(s5i_s), jnp.stack(lru_s), jnp.stack(conv_s))
```

```python
import functools

import jax
import jax.numpy as jnp
from jax import lax
from jax.experimental import pallas as pl
from jax.experimental.pallas import tpu as pltpu

F32 = jnp.float32
BF16 = jnp.bfloat16
HIGHEST = lax.Precision.HIGHEST

EPS = 1e-6
LRU_C = 8.0
S5_CH = 16
S5_STATE = 64
LRU_HEADS = 4
CONV_W = 4
CHUNK = 16
SUBLANES = 8
VMEM_LIMIT = 56 * 1024 * 1024


def _params(*sem):
    return pltpu.CompilerParams(dimension_semantics=sem, vmem_limit_bytes=VMEM_LIMIT)


def _rms(x, g):
    return x * lax.rsqrt(jnp.mean(x * x, axis=-1, keepdims=True) + EPS) * g


def _dot(a, b):
    return jnp.dot(a, b, preferred_element_type=F32)


def _norm_matmul_kernel(x_ref, g_ref, w_ref, o_ref, h_ref):
    @pl.when(pl.program_id(1) == 0)
    def _():
        h_ref[...] = _rms(x_ref[...], g_ref[...]).astype(BF16)

    o_ref[...] = _dot(h_ref[...], w_ref[...])


def _norm_matmul(x, g, w, tm, tn):
    m, k = x.shape
    n = w.shape[1]
    return pl.pallas_call(
        _norm_matmul_kernel,
        out_shape=jax.ShapeDtypeStruct((m, n), F32),
        grid=(m // tm, n // tn),
        in_specs=[
            pl.BlockSpec((tm, k), lambda i, j: (i, 0)),
            pl.BlockSpec((1, k), lambda i, j: (0, 0)),
            pl.BlockSpec((k, tn), lambda i, j: (0, j)),
        ],
        out_specs=pl.BlockSpec((tm, tn), lambda i, j: (i, j)),
        scratch_shapes=[pltpu.VMEM((tm, k), BF16)],
        compiler_params=_params("parallel", "arbitrary"),
        name="norm_w_in",
    )(x, g.reshape(1, k), w)


def _s5_discretize(lam_re, lam_im, log_step, b_re, b_im):
    step = jnp.exp(log_step)[:, None]
    mag = jnp.exp(lam_re * step)
    ar = mag * jnp.cos(lam_im * step)
    ai = mag * jnp.sin(lam_im * step)
    nr, ni = ar - 1.0, ai
    den = lam_re * lam_re + lam_im * lam_im
    cr = (nr * lam_re + ni * lam_im) / den
    ci = (ni * lam_re - nr * lam_im) / den
    bbr = cr[..., None] * b_re - ci[..., None] * b_im
    bbi = cr[..., None] * b_im + ci[..., None] * b_re
    return ar, ai, bbr, bbi


def _blockdiag(w, nb):
    qn, k, n = w.shape
    w = w.reshape(qn // nb, nb, k, n)
    eye = jnp.eye(nb, dtype=w.dtype)
    out = w[:, :, :, None, :] * eye[None, :, None, :, None]
    return out.reshape(qn // nb, nb * k, nb * n)


def _s5_chunk_weights(ar, ai, bbr, bbi, c_re, c_im, d):
    g = ar.shape[0]
    pr, pi = [jnp.ones_like(ar)], [jnp.zeros_like(ar)]
    for _ in range(CHUNK):
        pr.append(pr[-1] * ar - pi[-1] * ai)
        pi.append(pr[-2] * ai + pi[-1] * ar)
    pr = jnp.stack(pr)
    pi = jnp.stack(pi)
    bbr_t = jnp.transpose(bbr, (0, 2, 1))
    bbi_t = jnp.transpose(bbi, (0, 2, 1))
    pk_r = pr[:CHUNK, :, None, :]
    pk_i = pi[:CHUNK, :, None, :]
    gr = bbr_t[None] * pk_r - bbi_t[None] * pk_i
    gi = bbr_t[None] * pk_i + bbi_t[None] * pk_r
    ws_r = jnp.transpose(gr[::-1], (1, 0, 2, 3)).reshape(g, CHUNK * S5_CH, S5_STATE)
    ws_i = jnp.transpose(gi[::-1], (1, 0, 2, 3)).reshape(g, CHUNK * S5_CH, S5_STATE)
    w_s = jnp.concatenate([_blockdiag(ws_r, 2), _blockdiag(ws_i, 2)], axis=-1)
    kt = (jnp.einsum('kgcp,gdp->kgcd', gr, c_re, precision=HIGHEST)
          - jnp.einsum('kgcp,gdp->kgcd', gi, c_im, precision=HIGHEST))
    kt = kt.at[0].add(d.reshape(g, S5_CH)[:, :, None] * jnp.eye(S5_CH, dtype=F32))
    lag = jnp.arange(CHUNK)[None, :] - jnp.arange(CHUNK)[:, None]
    toe = jnp.where((lag >= 0)[:, :, None, None, None], kt[jnp.maximum(lag, 0)], 0.0)
    w_t = jnp.transpose(toe, (2, 0, 3, 1, 4)).reshape(g, CHUNK * S5_CH, CHUNK * S5_CH)
    qr = pr[1:, :, None, :]
    qi = pi[1:, :, None, :]
    ca_r = c_re[None] * qr - c_im[None] * qi
    ca_i = c_re[None] * qi + c_im[None] * qr
    wc_r = jnp.transpose(ca_r, (1, 3, 0, 2)).reshape(g, S5_STATE, CHUNK * S5_CH)
    wc_i = -jnp.transpose(ca_i, (1, 3, 0, 2)).reshape(g, S5_STATE, CHUNK * S5_CH)
    w_c = jnp.concatenate([_blockdiag(wc_r, 2), _blockdiag(wc_i, 2)], axis=1)
    a_r = pr[CHUNK].reshape(g // 2, 1, 2 * S5_STATE)
    a_i = pi[CHUNK].reshape(g // 2, 1, 2 * S5_STATE)
    return w_t.astype(BF16), w_s.astype(BF16), w_c.astype(BF16), a_r, a_i


def _s5_chunk_kernel(u_ref, wt_ref, ws_ref, wc_ref, ar_ref, ai_ref,
                     y_ref, hr_ref, hi_ref, sr_ref, si_ref, *, pb, nb, n_chunks):
    half = CHUNK * S5_CH
    for p in range(pb):
        s = _dot(u_ref[p], ws_ref[p])
        sr_ref[p] = s[:, :2 * S5_STATE]
        si_ref[p] = s[:, 2 * S5_STATE:]

    ar = jnp.broadcast_to(ar_ref[...], (pb, nb, 2 * S5_STATE))
    ai = jnp.broadcast_to(ai_ref[...], (pb, nb, 2 * S5_STATE))

    def step(c, carry):
        hr, hi = carry
        r0 = pl.multiple_of(c * nb, nb)
        xr = sr_ref[:, pl.ds(r0, nb), :]
        xi = si_ref[:, pl.ds(r0, nb), :]
        sr_ref[:, pl.ds(r0, nb), :] = hr
        si_ref[:, pl.ds(r0, nb), :] = hi
        return ar * hr - ai * hi + xr, ar * hi + ai * hr + xi

    zero = jnp.zeros((pb, nb, 2 * S5_STATE), F32)
    hr, hi = lax.fori_loop(0, n_chunks, step, (zero, zero))
    hr_ref[...] = hr
    hi_ref[...] = hi

    for p in range(pb):
        u = u_ref[p]
        hs = jnp.concatenate([sr_ref[p], si_ref[p]], axis=-1).astype(BF16)
        y = _dot(hs, wc_ref[p])
        y = y + jnp.concatenate(
            [_dot(u[:, :half], wt_ref[2 * p]), _dot(u[:, half:], wt_ref[2 * p + 1])], axis=-1)
        y_ref[p] = jax.nn.gelu(y).astype(BF16)


def _s5_chunked(u, w_t, w_s, w_c, a_r, a_i, nb, pb=4):
    pairs, rows, width = u.shape
    n_chunks = rows // nb
    kern = functools.partial(_s5_chunk_kernel, pb=pb, nb=nb, n_chunks=n_chunks)
    lanes = 2 * S5_STATE
    return pl.pallas_call(
        kern,
        out_shape=(jax.ShapeDtypeStruct((pairs, rows, width), BF16),
                   jax.ShapeDtypeStruct((pairs, nb, lanes), F32),
                   jax.ShapeDtypeStruct((pairs, nb, lanes), F32)),
        grid=(pairs // pb,),
        in_specs=[
            pl.BlockSpec((pb, rows, width), lambda i: (i, 0, 0)),
            pl.BlockSpec((2 * pb, width // 2, width // 2), lambda i: (i, 0, 0)),
            pl.BlockSpec((pb, width, 2 * lanes), lambda i: (i, 0, 0)),
            pl.BlockSpec((pb, 2 * lanes, width), lambda i: (i, 0, 0)),
            pl.BlockSpec((pb, 1, lanes), lambda i: (i, 0, 0)),
            pl.BlockSpec((pb, 1, lanes), lambda i: (i, 0, 0)),
        ],
        out_specs=(pl.BlockSpec((pb, rows, width), lambda i: (i, 0, 0)),
                   pl.BlockSpec((pb, nb, lanes), lambda i: (i, 0, 0)),
                   pl.BlockSpec((pb, nb, lanes), lambda i: (i, 0, 0))),
        scratch_shapes=[pltpu.VMEM((pb, rows, lanes), F32),
                        pltpu.VMEM((pb, rows, lanes), F32)],
        compiler_params=_params("parallel"),
        name="s5_chunked",
    )(u, w_t, w_s, w_c, a_r, a_i)


S5_STEP_GROUPS = 16


def _s5_step_kernel(u_ref, h0r_ref, h0i_ref, wbr_ref, wbi_ref, ar_ref, ai_ref,
                    wcr_ref, wci_ref, d_ref, y_ref, hr_ref, hi_ref):
    u = u_ref[...]
    ub = u.astype(BF16)
    ar, ai = ar_ref[...], ai_ref[...]
    h0r, h0i = h0r_ref[...], h0i_ref[...]
    hr = ar * h0r - ai * h0i + _dot(ub, wbr_ref[...])
    hi = ar * h0i + ai * h0r + _dot(ub, wbi_ref[...])
    hr_ref[...] = hr
    hi_ref[...] = hi
    y = _dot(hr.astype(BF16), wcr_ref[...]) - _dot(hi.astype(BF16), wci_ref[...]) + d_ref[...] * u
    y_ref[...] = jax.nn.gelu(y).astype(BF16)


def _s5_step(u, h0r, h0i, wbr, wbi, a_r, a_i, wcr, wci, d):
    n = u.shape[0]
    nblk = wbr.shape[0]
    cw = S5_STEP_GROUPS * S5_CH
    sw = S5_STEP_GROUPS * S5_STATE
    col = lambda i: (0, i)
    blk = lambda i: (i, 0, 0)
    return pl.pallas_call(
        _s5_step_kernel,
        out_shape=(jax.ShapeDtypeStruct((n, nblk * cw), BF16),
                   jax.ShapeDtypeStruct((n, nblk * sw), F32),
                   jax.ShapeDtypeStruct((n, nblk * sw), F32)),
        grid=(nblk,),
        in_specs=[
            pl.BlockSpec((n, cw), col),
            pl.BlockSpec((n, sw), col),
            pl.BlockSpec((n, sw), col),
            pl.BlockSpec((None, cw, sw), blk),
            pl.BlockSpec((None, cw, sw), blk),
            pl.BlockSpec((1, sw), col),
            pl.BlockSpec((1, sw), col),
            pl.BlockSpec((None, sw, cw), blk),
            pl.BlockSpec((None, sw, cw), blk),
            pl.BlockSpec((1, cw), col),
        ],
        out_specs=(pl.BlockSpec((n, cw), col),
                   pl.BlockSpec((n, sw), col),
                   pl.BlockSpec((n, sw), col)),
        compiler_params=_params("parallel"),
        name="s5_step",
    )(u, h0r, h0i, wbr, wbi, a_r, a_i, wcr, wci, d)


def _lru_gates(xc, wa_ref, ba, wi_ref, bi, lam):
    blk = xc.shape[1] // LRU_HEADS
    xb16 = xc.astype(BF16)
    ra = jnp.concatenate([_dot(xb16[:, h * blk:(h + 1) * blk], wa_ref[h])
                          for h in range(LRU_HEADS)], axis=-1)
    ri = jnp.concatenate([_dot(xb16[:, h * blk:(h + 1) * blk], wi_ref[h])
                          for h in range(LRU_HEADS)], axis=-1)
    r = jax.nn.sigmoid(ra + ba)
    ig = jax.nn.sigmoid(ri + bi)
    log_a = -LRU_C * r * jax.nn.softplus(-lam)
    a = jnp.exp(log_a)
    mult = jnp.sqrt(1.0 - a * a)
    return a, mult * (ig * xc)


def _lru_seq_kernel(xb_ref, gb_ref, cw_ref, cb_ref, wa_ref, ba_ref, wi_ref, bi_ref, lam_ref,
                    gm_ref, o_ref, hl_ref, xe_ref, a_ref, b_ref, hc_ref, *, tt):
    halo = SUBLANES

    @pl.when(pl.program_id(1) == 0)
    def _():
        xe_ref[0:halo, :] = jnp.zeros((halo, xe_ref.shape[1]), F32)
        hc_ref[...] = jnp.zeros_like(hc_ref)

    xb = xb_ref[...]
    xe_ref[halo:halo + tt, :] = xb
    cw = cw_ref[...]
    xc = cb_ref[...] + xb * cw[CONV_W - 1:CONV_W, :]
    for k in range(1, CONV_W):
        xc = xc + xe_ref[halo - k:halo - k + tt, :] * cw[CONV_W - 1 - k:CONV_W - k, :]
    xe_ref[0:halo, :] = xb[tt - halo:tt, :]

    a, b = _lru_gates(xc, wa_ref, ba_ref[...], wi_ref, bi_ref[...], lam_ref[...])
    a_ref[...] = a
    b_ref[...] = b

    def block(i, h):
        base = pl.multiple_of(i * SUBLANES, SUBLANES)
        for j in range(SUBLANES):
            row = pl.ds(base + j, 1)
            h = a_ref[row, :] * h + b_ref[row, :]
            b_ref[row, :] = h
        return h

    h = lax.fori_loop(0, tt // SUBLANES, block, hc_ref[...])
    hc_ref[...] = h
    hl_ref[...] = h

    out = b_ref[...] * jax.nn.gelu(gb_ref[...])
    o_ref[...] = _rms(out, gm_ref[...]).astype(BF16)


def _lru_seq(z, nseq, t_len, conv_w, conv_b, w_a, b_a, w_i, b_i, lam, g_merge, tt=256):
    d = conv_w.shape[1]
    nt = t_len // tt
    row = lambda v: v.reshape(1, d)
    const2 = lambda n, t: (0, 0)
    const3 = lambda n, t: (0, 0, 0)
    kern = functools.partial(_lru_seq_kernel, tt=tt)
    return pl.pallas_call(
        kern,
        out_shape=(jax.ShapeDtypeStruct((nseq * t_len, d), BF16),
                   jax.ShapeDtypeStruct((nseq, 1, d), F32)),
        grid=(nseq, nt),
        in_specs=[
            pl.BlockSpec((tt, d), lambda n, t: (n * nt + t, 1)),
            pl.BlockSpec((tt, d), lambda n, t: (n * nt + t, 2)),
            pl.BlockSpec((CONV_W, d), const2),
            pl.BlockSpec((1, d), const2),
            pl.BlockSpec(w_a.shape, const3),
            pl.BlockSpec((1, d), const2),
            pl.BlockSpec(w_i.shape, const3),
            pl.BlockSpec((1, d), const2),
            pl.BlockSpec((1, d), const2),
            pl.BlockSpec((1, d), const2),
        ],
        out_specs=(pl.BlockSpec((tt, d), lambda n, t: (n * nt + t, 0)),
                   pl.BlockSpec((None, 1, d), lambda n, t: (n, 0, 0))),
        scratch_shapes=[pltpu.VMEM((tt + SUBLANES, d), F32),
                        pltpu.VMEM((tt, d), F32),
                        pltpu.VMEM((tt, d), F32),
                        pltpu.VMEM((1, d), F32)],
        compiler_params=_params("parallel", "arbitrary"),
        name="rglru_seq",
    )(z, z, conv_w, row(conv_b), w_a, row(b_a), w_i, row(b_i), row(lam), row(g_merge))


def _lru_step_kernel(xb_ref, gb_ref, c0_ref, c1_ref, c2_ref, h0_ref, cw_ref, cb_ref,
                     wa_ref, ba_ref, wi_ref, bi_ref, lam_ref, gm_ref, o_ref, h_ref):
    xb = xb_ref[...]
    cw = cw_ref[...]
    xc = (cb_ref[...] + c0_ref[...] * cw[0:1, :] + c1_ref[...] * cw[1:2, :]
          + c2_ref[...] * cw[2:3, :] + xb * cw[3:4, :])
    a, b = _lru_gates(xc, wa_ref, ba_ref[...], wi_ref, bi_ref[...], lam_ref[...])
    h = a * h0_ref[...] + b
    h_ref[...] = h
    out = h * jax.nn.gelu(gb_ref[...])
    o_ref[...] = _rms(out, gm_ref[...]).astype(BF16)


def _lru_step(z, conv0, h0, conv_w, conv_b, w_a, b_a, w_i, b_i, lam, g_merge):
    n = z.shape[0]
    d = conv_w.shape[1]
    row = lambda v: v.reshape(1, d)
    full = lambda shape: pl.BlockSpec(shape, lambda i: (0,) * len(shape))
    nd = full((n, d))
    rd = full((1, d))
    return pl.pallas_call(
        _lru_step_kernel,
        out_shape=(jax.ShapeDtypeStruct((n, d), BF16), jax.ShapeDtypeStruct((n, d), F32)),
        grid=(1,),
        in_specs=[
            pl.BlockSpec((n, d), lambda i: (0, 1)),
            pl.BlockSpec((n, d), lambda i: (0, 2)),
            nd, nd, nd, nd,
            full((CONV_W, d)), rd,
            full(w_a.shape), rd, full(w_i.shape), rd, rd, rd,
        ],
        out_specs=(nd, nd),
        compiler_params=_params("arbitrary"),
        name="rglru_step",
    )(z, z, conv0[:, 0], conv0[:, 1], conv0[:, 2], h0, conv_w, row(conv_b),
      w_a, row(b_a), w_i, row(b_i), row(lam), row(g_merge))


def _glu_kernel(y_ref, w_ref, b_ref, g_ref, o_ref):
    y = y_ref[...]
    gate = jax.nn.sigmoid(_dot(y, w_ref[...]) + b_ref[...])
    o_ref[...] = _rms(y.astype(F32) * gate, g_ref[...]).astype(BF16)


def _glu(y, w, b, g, tm):
    m, d = y.shape
    return pl.pallas_call(
        _glu_kernel,
        out_shape=jax.ShapeDtypeStruct((m, d), BF16),
        grid=(m // tm,),
        in_specs=[
            pl.BlockSpec((tm, d), lambda i: (i, 0)),
            pl.BlockSpec((d, d), lambda i: (0, 0)),
            pl.BlockSpec((1, d), lambda i: (0, 0)),
            pl.BlockSpec((1, d), lambda i: (0, 0)),
        ],
        out_specs=pl.BlockSpec((tm, d), lambda i: (i, 0)),
        compiler_params=_params("parallel"),
        name="s5_glu",
    )(y, w, b.reshape(1, d), g.reshape(1, d))


def _out_proj_kernel(x_ref, ma_ref, mb_ref, wa_ref, wb_ref, o_ref):
    o_ref[...] = x_ref[...] + _dot(ma_ref[...], wa_ref[...]) + _dot(mb_ref[...], wb_ref[...])


def _out_proj(x, ma, mb, w, tm, tn):
    m, n = x.shape
    ka = ma.shape[1]
    kb = mb.shape[1]
    nj = n // tn
    return pl.pallas_call(
        _out_proj_kernel,
        out_shape=jax.ShapeDtypeStruct((m, n), F32),
        grid=(m // tm, nj),
        in_specs=[
            pl.BlockSpec((tm, tn), lambda i, j: (i, j)),
            pl.BlockSpec((tm, ka), lambda i, j: (i, 0)),
            pl.BlockSpec((tm, kb), lambda i, j: (i, 0)),
            pl.BlockSpec((ka, tn), lambda i, j: (0, j)),
            pl.BlockSpec((kb, tn), lambda i, j: (1, j)),
        ],
        out_specs=pl.BlockSpec((tm, tn), lambda i, j: (i, j)),
        compiler_params=_params("parallel", "arbitrary"),
        name="out_proj",
    )(x, ma, mb, w, w)


def _mlp_kernel(x_ref, g_ref, wu_ref, wd_ref, o_ref, h_ref):
    @pl.when(pl.program_id(1) == 0)
    def _():
        x = x_ref[...]
        h_ref[...] = _rms(x, g_ref[...]).astype(BF16)
        o_ref[...] = x

    up = _dot(h_ref[...], wu_ref[...])
    act = jnp.square(jnp.maximum(up, 0.0)).astype(BF16)
    o_ref[...] += _dot(act, wd_ref[...])


def _mlp(x, g, w_up, w_down, tm, tf):
    m, d = x.shape
    f = w_up.shape[1]
    return pl.pallas_call(
        _mlp_kernel,
        out_shape=jax.ShapeDtypeStruct((m, d), F32),
        grid=(m // tm, f // tf),
        in_specs=[
            pl.BlockSpec((tm, d), lambda i, j: (i, 0)),
            pl.BlockSpec((1, d), lambda i, j: (0, 0)),
            pl.BlockSpec((d, tf), lambda i, j: (0, j)),
            pl.BlockSpec((tf, d), lambda i, j: (j, 0)),
        ],
        out_specs=pl.BlockSpec((tm, d), lambda i, j: (i, 0)),
        scratch_shapes=[pltpu.VMEM((tm, d), BF16)],
        compiler_params=_params("parallel", "arbitrary"),
        name="mlp",
    )(x, g.reshape(1, d), w_up, w_down)


def _ple_kernel(x_ref, p_ref, g_ref, wg_ref, wp_ref, gf_ref, o_ref, *, final):
    x = x_ref[...]
    gate = jax.nn.sigmoid(_dot(_rms(x, g_ref[...]).astype(BF16), wg_ref[...]))
    pe = _dot(p_ref[...].astype(BF16), wp_ref[...])
    x = x + pe * gate
    o_ref[...] = _rms(x, gf_ref[...]) if final else x


def _ple(x, p, g, w_gate, w_ple, g_final, tm, final):
    m, d = x.shape
    dp = p.shape[1]
    return pl.pallas_call(
        functools.partial(_ple_kernel, final=final),
        out_shape=jax.ShapeDtypeStruct((m, d), F32),
        grid=(m // tm,),
        in_specs=[
            pl.BlockSpec((tm, d), lambda i: (i, 0)),
            pl.BlockSpec((tm, dp), lambda i: (i, 0)),
            pl.BlockSpec((1, d), lambda i: (0, 0)),
            pl.BlockSpec((d, d), lambda i: (0, 0)),
            pl.BlockSpec((dp, d), lambda i: (0, 0)),
            pl.BlockSpec((1, d), lambda i: (0, 0)),
        ],
        out_specs=pl.BlockSpec((tm, d), lambda i: (i, 0)),
        compiler_params=_params("parallel"),
        name="ple_final",
    )(x, p, g.reshape(1, d), w_gate, w_ple, g_final.reshape(1, d))


def _tail(x, ma, mb, p, w, g_final, final, tm, tm_mlp, tm_ple):
    x = _out_proj(x, ma, mb, w['w_out'], tm, 512)
    x = _mlp(x, w['g_mlp'], w['w_up'], w['w_down'], tm_mlp, 1024)
    return _ple(x, p, w['g_ple'], w['w_ple_gate'], w['w_ple'], g_final, tm_ple, final)


def kernel(x_prompt, x_sample, state_s5_re, state_s5_im, state_lru, state_conv, p_prompt, p_sample,
           g_mix, w_in, s5_lam_re, s5_lam_im, s5_log_step, s5_b_re, s5_b_im, s5_c_re, s5_c_im, s5_d,
           s5_w_glu, s5_b_glu, conv_w, conv_b, lru_w_a, lru_b_a, lru_w_i, lru_b_i, lru_lam,
           g_merge_a, g_merge_b, w_out, g_mlp, w_up, w_down, g_ple, w_ple_gate, w_ple, g_final):
    depth = g_mix.shape[0]
    nb, t_len, d_model = x_prompt.shape
    ns = x_sample.shape[0]
    d_s5 = s5_d.shape[1]
    d_lru = conv_w.shape[2]
    groups = d_s5 // S5_CH
    pairs = groups // 2
    n_chunks = t_len // CHUNK

    xp = x_prompt.reshape(nb * t_len, d_model)
    xs = x_sample.reshape(ns, d_model)
    outs = [[] for _ in range(8)]
    for l in range(depth):
        w = dict(w_out=w_out[l].astype(BF16), g_mlp=g_mlp[l], w_up=w_up[l].astype(BF16),
                 w_down=w_down[l].astype(BF16), g_ple=g_ple[l],
                 w_ple_gate=w_ple_gate[l].astype(BF16), w_ple=w_ple[l].astype(BF16))
        w_in_b = w_in[l].astype(BF16)
        w_glu_b = s5_w_glu[l].astype(BF16)
        w_a_b = lru_w_a[l].astype(BF16)
        w_i_b = lru_w_i[l].astype(BF16)
        lru_args = (conv_w[l], conv_b[l], w_a_b, lru_b_a[l], w_i_b, lru_b_i[l], lru_lam[l],
                    g_merge_b[l])

        ar, ai, bbr, bbi = _s5_discretize(s5_lam_re[l], s5_lam_im[l], s5_log_step[l],
                                          s5_b_re[l], s5_b_im[l])
        w_t, w_s, w_c, a16_r, a16_i = _s5_chunk_weights(ar, ai, bbr, bbi, s5_c_re[l], s5_c_im[l],
                                                        s5_d[l])

        z = _norm_matmul(xp, g_mix[l], w_in_b, 1024, 512)
        z4 = z.reshape(nb, t_len, 3 * d_s5)
        u = z4[:, :, :d_s5].astype(BF16).reshape(nb, n_chunks, CHUNK, pairs, 2, S5_CH)
        u = jnp.transpose(u, (3, 1, 0, 4, 2, 5)).reshape(pairs, n_chunks * nb, 2 * CHUNK * S5_CH)
        y, hr, hi = _s5_chunked(u, w_t, w_s, w_c, a16_r, a16_i, nb)
        y = y.reshape(pairs, n_chunks, nb, 2, CHUNK, S5_CH)
        y = jnp.transpose(y, (2, 1, 4, 0, 3, 5)).reshape(nb * t_len, d_s5)
        ma = _glu(y, w_glu_b, s5_b_glu[l], g_merge_a[l], 1024)
        mb, lru_h = _lru_seq(z, nb, t_len, *lru_args)
        final = l == depth - 1
        xp = _tail(xp, ma, mb, p_prompt[l].reshape(nb * t_len, -1), w, g_final, final, 1024, 512, 256)
        to_state = lambda h: jnp.transpose(h.reshape(pairs, nb, 2, S5_STATE), (1, 0, 2, 3)).reshape(
            nb, groups, S5_STATE)
        outs[0].append(to_state(hr))
        outs[1].append(to_state(hi))
        outs[2].append(lru_h.reshape(nb, d_lru))
        outs[3].append(z4[:, t_len - (CONV_W - 1):, d_s5:d_s5 + d_lru])

        zs = _norm_matmul(xs, g_mix[l], w_in_b, ns, 512)
        wbr = _blockdiag(jnp.transpose(bbr, (0, 2, 1)), S5_STEP_GROUPS).astype(BF16)
        wbi = _blockdiag(jnp.transpose(bbi, (0, 2, 1)), S5_STEP_GROUPS).astype(BF16)
        wcr = _blockdiag(jnp.transpose(s5_c_re[l], (0, 2, 1)), S5_STEP_GROUPS).astype(BF16)
        wci = _blockdiag(jnp.transpose(s5_c_im[l], (0, 2, 1)), S5_STEP_GROUPS).astype(BF16)
        ys, hsr, hsi = _s5_step(zs, state_s5_re[l].reshape(ns, groups * S5_STATE),
                                state_s5_im[l].reshape(ns, groups * S5_STATE),
                                wbr, wbi, ar.reshape(1, -1), ai.reshape(1, -1), wcr, wci,
                                s5_d[l].reshape(1, d_s5))
        mas = _glu(ys, w_glu_b, s5_b_glu[l], g_merge_a[l], ns)
        mbs, lru_hs = _lru_step(zs, state_conv[l], state_lru[l], *lru_args)
        xs = _tail(xs, mas, mbs, p_sample[l].reshape(ns, -1), w, g_final, final, ns, ns, ns)
        outs[4].append(hsr.reshape(ns, groups, S5_STATE))
        outs[5].append(hsi.reshape(ns, groups, S5_STATE))
        outs[6].append(lru_hs)
        outs[7].append(jnp.concatenate(
            [state_conv[l][:, 1:], zs[:, None, d_s5:d_s5 + d_lru]], axis=1))
    return (xp.reshape(nb, t_len, d_model), xs.reshape(ns, 1, d_model),
            *(jnp.stack(o) for o in outs))
```

```python
import functools

import jax
import jax.numpy as jnp
from jax import lax
from jax.experimental import pallas as pl
from jax.experimental.pallas import tpu as pltpu

F32 = jnp.float32
BF16 = jnp.bfloat16
HIGHEST = lax.Precision.HIGHEST

EPS = 1e-6
LRU_C = 8.0
S5_CH = 16
S5_STATE = 64
LRU_HEADS = 4
CONV_W = 4
CHUNK = 16
SUBLANES = 8
VMEM_LIMIT = 56 * 1024 * 1024


def _params(*sem):
    return pltpu.CompilerParams(dimension_semantics=sem, vmem_limit_bytes=VMEM_LIMIT)


def _rms(x, g):
    return x * lax.rsqrt(jnp.mean(x * x, axis=-1, keepdims=True) + EPS) * g


def _dot(a, b):
    return jnp.dot(a, b, preferred_element_type=F32)


def _norm_matmul_kernel(x_ref, g_ref, w_ref, o_ref, h_ref):
    @pl.when(pl.program_id(1) == 0)
    def _():
        h_ref[...] = _rms(x_ref[...], g_ref[...]).astype(BF16)

    o_ref[...] = _dot(h_ref[...], w_ref[...])


def _norm_matmul(x, g, w, tm, tn):
    m, k = x.shape
    n = w.shape[1]
    return pl.pallas_call(
        _norm_matmul_kernel,
        out_shape=jax.ShapeDtypeStruct((m, n), F32),
        grid=(m // tm, n // tn),
        in_specs=[
            pl.BlockSpec((tm, k), lambda i, j: (i, 0)),
            pl.BlockSpec((1, k), lambda i, j: (0, 0)),
            pl.BlockSpec((k, tn), lambda i, j: (0, j)),
        ],
        out_specs=pl.BlockSpec((tm, tn), lambda i, j: (i, j)),
        scratch_shapes=[pltpu.VMEM((tm, k), BF16)],
        compiler_params=_params("parallel", "arbitrary"),
        name="norm_w_in",
    )(x, g.reshape(1, k), w)


def _s5_discretize(lam_re, lam_im, log_step, b_re, b_im):
    step = jnp.exp(log_step)[:, None]
    mag = jnp.exp(lam_re * step)
    ar = mag * jnp.cos(lam_im * step)
    ai = mag * jnp.sin(lam_im * step)
    nr, ni = ar - 1.0, ai
    den = lam_re * lam_re + lam_im * lam_im
    cr = (nr * lam_re + ni * lam_im) / den
    ci = (ni * lam_re - nr * lam_im) / den
    bbr = cr[..., None] * b_re - ci[..., None] * b_im
    bbi = cr[..., None] * b_im + ci[..., None] * b_re
    return ar, ai, bbr, bbi


def _blockdiag(w, nb):
    qn, k, n = w.shape
    w = w.reshape(qn // nb, nb, k, n)
    eye = jnp.eye(nb, dtype=w.dtype)
    out = w[:, :, :, None, :] * eye[None, :, None, :, None]
    return out.reshape(qn // nb, nb * k, nb * n)


def _s5_chunk_weights(ar, ai, bbr, bbi, c_re, c_im, d):
    g = ar.shape[0]
    pr, pi = [jnp.ones_like(ar)], [jnp.zeros_like(ar)]
    for _ in range(CHUNK):
        pr.append(pr[-1] * ar - pi[-1] * ai)
        pi.append(pr[-2] * ai + pi[-1] * ar)
    pr = jnp.stack(pr)
    pi = jnp.stack(pi)
    bbr_t = jnp.transpose(bbr, (0, 2, 1))
    bbi_t = jnp.transpose(bbi, (0, 2, 1))
    pk_r = pr[:CHUNK, :, None, :]
    pk_i = pi[:CHUNK, :, None, :]
    gr = bbr_t[None] * pk_r - bbi_t[None] * pk_i
    gi = bbr_t[None] * pk_i + bbi_t[None] * pk_r
    ws_r = jnp.transpose(gr[::-1], (1, 3, 0, 2)).reshape(g, S5_STATE, CHUNK * S5_CH)
    ws_i = jnp.transpose(gi[::-1], (1, 3, 0, 2)).reshape(g, S5_STATE, CHUNK * S5_CH)
    w_s = jnp.concatenate([ws_r, ws_i], axis=1)
    kt = (jnp.einsum('kgcp,gdp->kgcd', gr, c_re, precision=HIGHEST)
          - jnp.einsum('kgcp,gdp->kgcd', gi, c_im, precision=HIGHEST))
    kt = kt.at[0].add(d.reshape(g, S5_CH)[:, :, None] * jnp.eye(S5_CH, dtype=F32))
    lag = jnp.arange(CHUNK)[None, :] - jnp.arange(CHUNK)[:, None]
    toe = jnp.where((lag >= 0)[:, :, None, None, None], kt[jnp.maximum(lag, 0)], 0.0)
    w_t = jnp.transpose(toe, (2, 1, 4, 0, 3)).reshape(g, CHUNK * S5_CH, CHUNK * S5_CH)
    qr = pr[1:, :, None, :]
    qi = pi[1:, :, None, :]
    ca_r = c_re[None] * qr - c_im[None] * qi
    ca_i = c_re[None] * qi + c_im[None] * qr
    wc_r = jnp.transpose(ca_r, (1, 0, 2, 3)).reshape(g, CHUNK * S5_CH, S5_STATE)
    wc_i = -jnp.transpose(ca_i, (1, 0, 2, 3)).reshape(g, CHUNK * S5_CH, S5_STATE)
    w_c = jnp.concatenate([wc_r, wc_i], axis=2)
    qr, qi = pr[CHUNK], pi[CHUNK]
    pw_r, pw_i = [qr], [qi]
    for _ in range(SUBLANES - 1):
        qr, qi = qr * qr - qi * qi, 2.0 * qr * qi
        pw_r.append(qr)
        pw_i.append(qi)
    pw = jnp.stack([jnp.stack(pw_r, axis=-1), jnp.stack(pw_i, axis=-1)], axis=1)
    return w_t.astype(BF16), w_s.astype(BF16), w_c.astype(BF16), pw


LANES = 128
LANE_GROUPS = LANES // S5_CH


def _s5_seq_kernel(z_ref, wt_ref, ws_ref, wc_ref, pw_ref, y_ref, hf_ref, ut_ref, yt_ref,
                   *, nb, n_chunks):
    scan_steps = n_chunks.bit_length() - 1
    for n in range(nb):
        for s in range(CHUNK):
            xs = z_ref[n, pl.ds(s, n_chunks, stride=CHUNK), :]
            ut_ref[:, s * S5_CH:(s + 1) * S5_CH, n * n_chunks:(n + 1) * n_chunks] = (
                xs.T.astype(BF16).reshape(LANE_GROUPS, S5_CH, n_chunks))

    lane = lax.broadcasted_iota(jnp.int32, (S5_STATE, n_chunks), 1)
    for g in range(LANE_GROUPS):
        ut = ut_ref[g]
        st = _dot(ws_ref[g], ut)
        pw_r = pw_ref[g, 0]
        pw_i = pw_ref[g, 1]
        fin_r = jnp.zeros((S5_STATE, n_chunks), F32)
        fin_i = jnp.zeros((S5_STATE, n_chunks), F32)
        starts = []
        for n in range(nb):
            cols = slice(n * n_chunks, (n + 1) * n_chunks)
            xr = st[:S5_STATE, cols]
            xi = st[S5_STATE:, cols]
            for k in range(scan_steps):
                d = 1 << k
                keep = lane >= d
                sr = jnp.where(keep, pltpu.roll(xr, d, 1), 0.0)
                si = jnp.where(keep, pltpu.roll(xi, d, 1), 0.0)
                pr = pw_r[:, k:k + 1]
                pi = pw_i[:, k:k + 1]
                xr, xi = xr + (pr * sr - pi * si), xi + (pr * si + pi * sr)
            last = lane == n_chunks - 1
            fin_r = fin_r + jnp.where(
                lane == n, jnp.sum(jnp.where(last, xr, 0.0), axis=1, keepdims=True), 0.0)
            fin_i = fin_i + jnp.where(
                lane == n, jnp.sum(jnp.where(last, xi, 0.0), axis=1, keepdims=True), 0.0)
            keep = lane >= 1
            starts.append(jnp.concatenate([jnp.where(keep, pltpu.roll(xr, 1, 1), 0.0),
                                           jnp.where(keep, pltpu.roll(xi, 1, 1), 0.0)], axis=0))
        hf_ref[g, 0] = fin_r
        hf_ref[g, 1] = fin_i
        hs = jnp.concatenate(starts, axis=1).astype(BF16)
        yt_ref[g] = jax.nn.gelu(_dot(wc_ref[g], hs) + _dot(wt_ref[g], ut))

    for n in range(nb):
        for s in range(CHUNK):
            blk = yt_ref[:, s * S5_CH:(s + 1) * S5_CH, n * n_chunks:(n + 1) * n_chunks]
            y_ref[n, pl.ds(s, n_chunks, stride=CHUNK), :] = blk.reshape(LANES, n_chunks).T


def _s5_seq(z, w_t, w_s, w_c, pw, d_s5):
    nb, t_len, _ = z.shape
    n_chunks = t_len // CHUNK
    assert n_chunks == LANES, "the chunk axis must fill one 128-lane tile"
    groups = d_s5 // S5_CH
    rows = CHUNK * S5_CH
    kern = functools.partial(_s5_seq_kernel, nb=nb, n_chunks=n_chunks)
    blk3 = lambda i: (i, 0, 0)
    return pl.pallas_call(
        kern,
        out_shape=(jax.ShapeDtypeStruct((nb, t_len, d_s5), F32),
                   jax.ShapeDtypeStruct((groups, 2, S5_STATE, LANES), F32)),
        grid=(groups // LANE_GROUPS,),
        in_specs=[
            pl.BlockSpec((nb, t_len, LANES), lambda i: (0, 0, i)),
            pl.BlockSpec((LANE_GROUPS, rows, rows), blk3),
            pl.BlockSpec((LANE_GROUPS, 2 * S5_STATE, rows), blk3),
            pl.BlockSpec((LANE_GROUPS, rows, 2 * S5_STATE), blk3),
            pl.BlockSpec((LANE_GROUPS, 2, S5_STATE, SUBLANES), lambda i: (i, 0, 0, 0)),
        ],
        out_specs=(pl.BlockSpec((nb, t_len, LANES), lambda i: (0, 0, i)),
                   pl.BlockSpec((LANE_GROUPS, 2, S5_STATE, LANES), lambda i: (i, 0, 0, 0))),
        scratch_shapes=[pltpu.VMEM((LANE_GROUPS, rows, nb * n_chunks), BF16),
                        pltpu.VMEM((LANE_GROUPS, rows, nb * n_chunks), F32)],
        compiler_params=_params("parallel"),
        name="s5_seq",
    )(z, w_t, w_s, w_c, pw)


S5_STEP_GROUPS = 16


def _s5_step_kernel(u_ref, h0r_ref, h0i_ref, wbr_ref, wbi_ref, ar_ref, ai_ref,
                    wcr_ref, wci_ref, d_ref, y_ref, hr_ref, hi_ref):
    u = u_ref[...]
    ub = u.astype(BF16)
    ar, ai = ar_ref[...], ai_ref[...]
    h0r, h0i = h0r_ref[...], h0i_ref[...]
    hr = ar * h0r - ai * h0i + _dot(ub, wbr_ref[...])
    hi = ar * h0i + ai * h0r + _dot(ub, wbi_ref[...])
    hr_ref[...] = hr
    hi_ref[...] = hi
    y = _dot(hr.astype(BF16), wcr_ref[...]) - _dot(hi.astype(BF16), wci_ref[...]) + d_ref[...] * u
    y_ref[...] = jax.nn.gelu(y).astype(BF16)


def _s5_step(u, h0r, h0i, wbr, wbi, a_r, a_i, wcr, wci, d):
    n = u.shape[0]
    nblk = wbr.shape[0]
    cw = S5_STEP_GROUPS * S5_CH
    sw = S5_STEP_GROUPS * S5_STATE
    col = lambda i: (0, i)
    blk = lambda i: (i, 0, 0)
    return pl.pallas_call(
        _s5_step_kernel,
        out_shape=(jax.ShapeDtypeStruct((n, nblk * cw), BF16),
                   jax.ShapeDtypeStruct((n, nblk * sw), F32),
                   jax.ShapeDtypeStruct((n, nblk * sw), F32)),
        grid=(nblk,),
        in_specs=[
            pl.BlockSpec((n, cw), col),
            pl.BlockSpec((n, sw), col),
            pl.BlockSpec((n, sw), col),
            pl.BlockSpec((None, cw, sw), blk),
            pl.BlockSpec((None, cw, sw), blk),
            pl.BlockSpec((1, sw), col),
            pl.BlockSpec((1, sw), col),
            pl.BlockSpec((None, sw, cw), blk),
            pl.BlockSpec((None, sw, cw), blk),
            pl.BlockSpec((1, cw), col),
        ],
        out_specs=(pl.BlockSpec((n, cw), col),
                   pl.BlockSpec((n, sw), col),
                   pl.BlockSpec((n, sw), col)),
        compiler_params=_params("parallel"),
        name="s5_step",
    )(u, h0r, h0i, wbr, wbi, a_r, a_i, wcr, wci, d)


def _lru_gates(xc, wa_ref, ba, wi_ref, bi, lam):
    blk = xc.shape[1] // LRU_HEADS
    xb16 = xc.astype(BF16)
    ra = jnp.concatenate([_dot(xb16[:, h * blk:(h + 1) * blk], wa_ref[h])
                          for h in range(LRU_HEADS)], axis=-1)
    ri = jnp.concatenate([_dot(xb16[:, h * blk:(h + 1) * blk], wi_ref[h])
                          for h in range(LRU_HEADS)], axis=-1)
    r = jax.nn.sigmoid(ra + ba)
    ig = jax.nn.sigmoid(ri + bi)
    log_a = -LRU_C * r * jax.nn.softplus(-lam)
    a = jnp.exp(log_a)
    mult = jnp.sqrt(1.0 - a * a)
    return a, mult * (ig * xc)


def _lru_seq_kernel(xb_ref, gb_ref, cw_ref, cb_ref, wa_ref, ba_ref, wi_ref, bi_ref, lam_ref,
                    gm_ref, o_ref, hl_ref, xe_ref, a_ref, b_ref, hc_ref, *, tt):
    halo = SUBLANES

    @pl.when(pl.program_id(1) == 0)
    def _():
        xe_ref[0:halo, :] = jnp.zeros((halo, xe_ref.shape[1]), F32)
        hc_ref[...] = jnp.zeros_like(hc_ref)

    xb = xb_ref[...]
    xe_ref[halo:halo + tt, :] = xb
    cw = cw_ref[...]
    xc = cb_ref[...] + xb * cw[CONV_W - 1:CONV_W, :]
    for k in range(1, CONV_W):
        xc = xc + xe_ref[halo - k:halo - k + tt, :] * cw[CONV_W - 1 - k:CONV_W - k, :]
    xe_ref[0:halo, :] = xb[tt - halo:tt, :]

    a, b = _lru_gates(xc, wa_ref, ba_ref[...], wi_ref, bi_ref[...], lam_ref[...])
    a_ref[...] = a
    b_ref[...] = b

    def block(i, h):
        base = pl.multiple_of(i * SUBLANES, SUBLANES)
        for j in range(SUBLANES):
            row = pl.ds(base + j, 1)
            h = a_ref[row, :] * h + b_ref[row, :]
            b_ref[row, :] = h
        return h

    h = lax.fori_loop(0, tt // SUBLANES, block, hc_ref[...])
    hc_ref[...] = h
    hl_ref[...] = h

    out = b_ref[...] * jax.nn.gelu(gb_ref[...])
    o_ref[...] = _rms(out, gm_ref[...]).astype(BF16)


def _lru_seq(z, nseq, t_len, conv_w, conv_b, w_a, b_a, w_i, b_i, lam, g_merge, tt=256):
    d = conv_w.shape[1]
    nt = t_len // tt
    row = lambda v: v.reshape(1, d)
    const2 = lambda n, t: (0, 0)
    const3 = lambda n, t: (0, 0, 0)
    kern = functools.partial(_lru_seq_kernel, tt=tt)
    return pl.pallas_call(
        kern,
        out_shape=(jax.ShapeDtypeStruct((nseq * t_len, d), BF16),
                   jax.ShapeDtypeStruct((nseq, 1, d), F32)),
        grid=(nseq, nt),
        in_specs=[
            pl.BlockSpec((tt, d), lambda n, t: (n * nt + t, 1)),
            pl.BlockSpec((tt, d), lambda n, t: (n * nt + t, 2)),
            pl.BlockSpec((CONV_W, d), const2),
            pl.BlockSpec((1, d), const2),
            pl.BlockSpec(w_a.shape, const3),
            pl.BlockSpec((1, d), const2),
            pl.BlockSpec(w_i.shape, const3),
            pl.BlockSpec((1, d), const2),
            pl.BlockSpec((1, d), const2),
            pl.BlockSpec((1, d), const2),
        ],
        out_specs=(pl.BlockSpec((tt, d), lambda n, t: (n * nt + t, 0)),
                   pl.BlockSpec((None, 1, d), lambda n, t: (n, 0, 0))),
        scratch_shapes=[pltpu.VMEM((tt + SUBLANES, d), F32),
                        pltpu.VMEM((tt, d), F32),
                        pltpu.VMEM((tt, d), F32),
                        pltpu.VMEM((1, d), F32)],
        compiler_params=_params("parallel", "arbitrary"),
        name="rglru_seq",
    )(z, z, conv_w, row(conv_b), w_a, row(b_a), w_i, row(b_i), row(lam), row(g_merge))


def _lru_step_kernel(xb_ref, gb_ref, c0_ref, c1_ref, c2_ref, h0_ref, cw_ref, cb_ref,
                     wa_ref, ba_ref, wi_ref, bi_ref, lam_ref, gm_ref, o_ref, h_ref):
    xb = xb_ref[...]
    cw = cw_ref[...]
    xc = (cb_ref[...] + c0_ref[...] * cw[0:1, :] + c1_ref[...] * cw[1:2, :]
          + c2_ref[...] * cw[2:3, :] + xb * cw[3:4, :])
    a, b = _lru_gates(xc, wa_ref, ba_ref[...], wi_ref, bi_ref[...], lam_ref[...])
    h = a * h0_ref[...] + b
    h_ref[...] = h
    out = h * jax.nn.gelu(gb_ref[...])
    o_ref[...] = _rms(out, gm_ref[...]).astype(BF16)


def _lru_step(z, conv0, h0, conv_w, conv_b, w_a, b_a, w_i, b_i, lam, g_merge):
    n = z.shape[0]
    d = conv_w.shape[1]
    row = lambda v: v.reshape(1, d)
    full = lambda shape: pl.BlockSpec(shape, lambda i: (0,) * len(shape))
    nd = full((n, d))
    rd = full((1, d))
    return pl.pallas_call(
        _lru_step_kernel,
        out_shape=(jax.ShapeDtypeStruct((n, d), BF16), jax.ShapeDtypeStruct((n, d), F32)),
        grid=(1,),
        in_specs=[
            pl.BlockSpec((n, d), lambda i: (0, 1)),
            pl.BlockSpec((n, d), lambda i: (0, 2)),
            nd, nd, nd, nd,
            full((CONV_W, d)), rd,
            full(w_a.shape), rd, full(w_i.shape), rd, rd, rd,
        ],
        out_specs=(nd, nd),
        compiler_params=_params("arbitrary"),
        name="rglru_step",
    )(z, z, conv0[:, 0], conv0[:, 1], conv0[:, 2], h0, conv_w, row(conv_b),
      w_a, row(b_a), w_i, row(b_i), row(lam), row(g_merge))


def _glu_kernel(y_ref, w_ref, b_ref, g_ref, o_ref):
    y = y_ref[...]
    gate = jax.nn.sigmoid(_dot(y.astype(BF16), w_ref[...]) + b_ref[...])
    o_ref[...] = _rms(y.astype(F32) * gate, g_ref[...]).astype(BF16)


def _glu(y, w, b, g, tm):
    m, d = y.shape
    return pl.pallas_call(
        _glu_kernel,
        out_shape=jax.ShapeDtypeStruct((m, d), BF16),
        grid=(m // tm,),
        in_specs=[
            pl.BlockSpec((tm, d), lambda i: (i, 0)),
            pl.BlockSpec((d, d), lambda i: (0, 0)),
            pl.BlockSpec((1, d), lambda i: (0, 0)),
            pl.BlockSpec((1, d), lambda i: (0, 0)),
        ],
        out_specs=pl.BlockSpec((tm, d), lambda i: (i, 0)),
        compiler_params=_params("parallel"),
        name="s5_glu",
    )(y, w, b.reshape(1, d), g.reshape(1, d))


def _out_proj_kernel(x_ref, ma_ref, mb_ref, wa_ref, wb_ref, o_ref):
    o_ref[...] = x_ref[...] + _dot(ma_ref[...], wa_ref[...]) + _dot(mb_ref[...], wb_ref[...])


def _out_proj(x, ma, mb, w, tm, tn):
    m, n = x.shape
    ka = ma.shape[1]
    kb = mb.shape[1]
    nj = n // tn
    return pl.pallas_call(
        _out_proj_kernel,
        out_shape=jax.ShapeDtypeStruct((m, n), F32),
        grid=(m // tm, nj),
        in_specs=[
            pl.BlockSpec((tm, tn), lambda i, j: (i, j)),
            pl.BlockSpec((tm, ka), lambda i, j: (i, 0)),
            pl.BlockSpec((tm, kb), lambda i, j: (i, 0)),
            pl.BlockSpec((ka, tn), lambda i, j: (0, j)),
            pl.BlockSpec((kb, tn), lambda i, j: (1, j)),
        ],
        out_specs=pl.BlockSpec((tm, tn), lambda i, j: (i, j)),
        compiler_params=_params("parallel", "arbitrary"),
        name="out_proj",
    )(x, ma, mb, w, w)


def _mlp_kernel(x_ref, g_ref, wu_ref, wd_ref, o_ref, h_ref):
    @pl.when(pl.program_id(1) == 0)
    def _():
        x = x_ref[...]
        h_ref[...] = _rms(x, g_ref[...]).astype(BF16)
        o_ref[...] = x

    up = _dot(h_ref[...], wu_ref[...])
    act = jnp.square(jnp.maximum(up, 0.0)).astype(BF16)
    o_ref[...] += _dot(act, wd_ref[...])


def _mlp(x, g, w_up, w_down, tm, tf):
    m, d = x.shape
    f = w_up.shape[1]
    return pl.pallas_call(
        _mlp_kernel,
        out_shape=jax.ShapeDtypeStruct((m, d), F32),
        grid=(m // tm, f // tf),
        in_specs=[
            pl.BlockSpec((tm, d), lambda i, j: (i, 0)),
            pl.BlockSpec((1, d), lambda i, j: (0, 0)),
            pl.BlockSpec((d, tf), lambda i, j: (0, j)),
            pl.BlockSpec((tf, d), lambda i, j: (j, 0)),
        ],
        out_specs=pl.BlockSpec((tm, d), lambda i, j: (i, 0)),
        scratch_shapes=[pltpu.VMEM((tm, d), BF16)],
        compiler_params=_params("parallel", "arbitrary"),
        name="mlp",
    )(x, g.reshape(1, d), w_up, w_down)


def _ple_kernel(x_ref, p_ref, g_ref, wg_ref, wp_ref, gf_ref, o_ref, *, final):
    x = x_ref[...]
    gate = jax.nn.sigmoid(_dot(_rms(x, g_ref[...]).astype(BF16), wg_ref[...]))
    pe = _dot(p_ref[...].astype(BF16), wp_ref[...])
    x = x + pe * gate
    o_ref[...] = _rms(x, gf_ref[...]) if final else x


def _ple(x, p, g, w_gate, w_ple, g_final, tm, final):
    m, d = x.shape
    dp = p.shape[1]
    return pl.pallas_call(
        functools.partial(_ple_kernel, final=final),
        out_shape=jax.ShapeDtypeStruct((m, d), F32),
        grid=(m // tm,),
        in_specs=[
            pl.BlockSpec((tm, d), lambda i: (i, 0)),
            pl.BlockSpec((tm, dp), lambda i: (i, 0)),
            pl.BlockSpec((1, d), lambda i: (0, 0)),
            pl.BlockSpec((d, d), lambda i: (0, 0)),
            pl.BlockSpec((dp, d), lambda i: (0, 0)),
            pl.BlockSpec((1, d), lambda i: (0, 0)),
        ],
        out_specs=pl.BlockSpec((tm, d), lambda i: (i, 0)),
        compiler_params=_params("parallel"),
        name="ple_final",
    )(x, p, g.reshape(1, d), w_gate, w_ple, g_final.reshape(1, d))


def _tail(x, ma, mb, p, w, g_final, final, tm, tm_mlp, tm_ple):
    x = _out_proj(x, ma, mb, w['w_out'], tm, 512)
    x = _mlp(x, w['g_mlp'], w['w_up'], w['w_down'], tm_mlp, 1024)
    return _ple(x, p, w['g_ple'], w['w_ple_gate'], w['w_ple'], g_final, tm_ple, final)


def kernel(x_prompt, x_sample, state_s5_re, state_s5_im, state_lru, state_conv, p_prompt, p_sample,
           g_mix, w_in, s5_lam_re, s5_lam_im, s5_log_step, s5_b_re, s5_b_im, s5_c_re, s5_c_im, s5_d,
           s5_w_glu, s5_b_glu, conv_w, conv_b, lru_w_a, lru_b_a, lru_w_i, lru_b_i, lru_lam,
           g_merge_a, g_merge_b, w_out, g_mlp, w_up, w_down, g_ple, w_ple_gate, w_ple, g_final):
    depth = g_mix.shape[0]
    nb, t_len, d_model = x_prompt.shape
    ns = x_sample.shape[0]
    d_s5 = s5_d.shape[1]
    d_lru = conv_w.shape[2]
    groups = d_s5 // S5_CH
    pairs = groups // 2
    n_chunks = t_len // CHUNK

    xp = x_prompt.reshape(nb * t_len, d_model)
    xs = x_sample.reshape(ns, d_model)
    outs = [[] for _ in range(8)]
    for l in range(depth):
        w = dict(w_out=w_out[l].astype(BF16), g_mlp=g_mlp[l], w_up=w_up[l].astype(BF16),
                 w_down=w_down[l].astype(BF16), g_ple=g_ple[l],
                 w_ple_gate=w_ple_gate[l].astype(BF16), w_ple=w_ple[l].astype(BF16))
        w_in_b = w_in[l].astype(BF16)
        w_glu_b = s5_w_glu[l].astype(BF16)
        w_a_b = lru_w_a[l].astype(BF16)
        w_i_b = lru_w_i[l].astype(BF16)
        lru_args = (conv_w[l], conv_b[l], w_a_b, lru_b_a[l], w_i_b, lru_b_i[l], lru_lam[l],
                    g_merge_b[l])

        ar, ai, bbr, bbi = _s5_discretize(s5_lam_re[l], s5_lam_im[l], s5_log_step[l],
                                          s5_b_re[l], s5_b_im[l])
        w_t, w_s, w_c, pw = _s5_chunk_weights(ar, ai, bbr, bbi, s5_c_re[l], s5_c_im[l], s5_d[l])

        z = _norm_matmul(xp, g_mix[l], w_in_b, 1024, 512)
        z4 = z.reshape(nb, t_len, 3 * d_s5)
        y, hf = _s5_seq(z4, w_t, w_s, w_c, pw, d_s5)
        ma = _glu(y.reshape(nb * t_len, d_s5), w_glu_b, s5_b_glu[l], g_merge_a[l], 1024)
        mb, lru_h = _lru_seq(z, nb, t_len, *lru_args)
        final = l == depth - 1
        xp = _tail(xp, ma, mb, p_prompt[l].reshape(nb * t_len, -1), w, g_final, final, 1024, 512, 256)
        hf = jnp.transpose(hf[:, :, :, :nb], (1, 3, 0, 2))
        outs[0].append(hf[0])
        outs[1].append(hf[1])
        outs[2].append(lru_h.reshape(nb, d_lru))
        outs[3].append(z4[:, t_len - (CONV_W - 1):, d_s5:d_s5 + d_lru])

        zs = _norm_matmul(xs, g_mix[l], w_in_b, ns, 512)
        wbr = _blockdiag(jnp.transpose(bbr, (0, 2, 1)), S5_STEP_GROUPS).astype(BF16)
        wbi = _blockdiag(jnp.transpose(bbi, (0, 2, 1)), S5_STEP_GROUPS).astype(BF16)
        wcr = _blockdiag(jnp.transpose(s5_c_re[l], (0, 2, 1)), S5_STEP_GROUPS).astype(BF16)
        wci = _blockdiag(jnp.transpose(s5_c_im[l], (0, 2, 1)), S5_STEP_GROUPS).astype(BF16)
        ys, hsr, hsi = _s5_step(zs, state_s5_re[l].reshape(ns, groups * S5_STATE),
                                state_s5_im[l].reshape(ns, groups * S5_STATE),
                                wbr, wbi, ar.reshape(1, -1), ai.reshape(1, -1), wcr, wci,
                                s5_d[l].reshape(1, d_s5))
        mas = _glu(ys, w_glu_b, s5_b_glu[l], g_merge_a[l], ns)
        mbs, lru_hs = _lru_step(zs, state_conv[l], state_lru[l], *lru_args)
        xs = _tail(xs, mas, mbs, p_sample[l].reshape(ns, -1), w, g_final, final, ns, ns, ns)
        outs[4].append(hsr.reshape(ns, groups, S5_STATE))
        outs[5].append(hsi.reshape(ns, groups, S5_STATE))
        outs[6].append(lru_hs)
        outs[7].append(jnp.concatenate(
            [state_conv[l][:, 1:], zs[:, None, d_s5:d_s5 + d_lru]], axis=1))
    return (xp.reshape(nb, t_len, d_model), xs.reshape(ns, 1, d_model),
            *(jnp.stack(o) for o in outs))
```

```python
import functools

import jax
import jax.numpy as jnp
from jax import lax
from jax.experimental import pallas as pl
from jax.experimental.pallas import tpu as pltpu

F32 = jnp.float32
BF16 = jnp.bfloat16
HIGHEST = lax.Precision.HIGHEST

EPS = 1e-6
LRU_C = 8.0
S5_CH = 16
S5_STATE = 64
LRU_HEADS = 4
CONV_W = 4
CHUNK = 16
SUBLANES = 8
LANES = 128
LANE_GROUPS = LANES // S5_CH
LANE_PAIRS = LANE_GROUPS // 2
ROWS = CHUNK * S5_CH
POW_ROWS = 32
SCAN_ROW0 = POW_ROWS
VMEM_LIMIT = 56 * 1024 * 1024


def _params(*sem):
    return pltpu.CompilerParams(dimension_semantics=sem, vmem_limit_bytes=VMEM_LIMIT)


def _rms(x, g):
    return x * lax.rsqrt(jnp.mean(x * x, axis=-1, keepdims=True) + EPS) * g


def _dot(a, b):
    return jnp.dot(a, b, preferred_element_type=F32)


def _dot_nt(a, b, precision=None):
    return lax.dot_general(a, b, (((1,), (1,)), ((), ())), precision=precision,
                           preferred_element_type=F32)


def _transpose_tiles(x):
    r, c = x.shape
    return jnp.concatenate(
        [jnp.concatenate([x[i:i + LANES, j:j + LANES].T for i in range(0, r, LANES)], axis=1)
         for j in range(0, c, LANES)], axis=0)


def _norm_matmul_kernel(x_ref, g_ref, w_ref, o_ref, h_ref):
    @pl.when(pl.program_id(1) == 0)
    def _():
        h_ref[...] = _rms(x_ref[...], g_ref[...]).astype(BF16)

    o_ref[...] = _dot(h_ref[...], w_ref[...])


def _norm_matmul(x, g, w, tm, tn):
    m, k = x.shape
    n = w.shape[1]
    return pl.pallas_call(
        _norm_matmul_kernel,
        out_shape=jax.ShapeDtypeStruct((m, n), F32),
        grid=(m // tm, n // tn),
        in_specs=[
            pl.BlockSpec((tm, k), lambda i, j: (i, 0)),
            pl.BlockSpec((1, k), lambda i, j: (0, 0)),
            pl.BlockSpec((k, tn), lambda i, j: (0, j)),
        ],
        out_specs=pl.BlockSpec((tm, tn), lambda i, j: (i, j)),
        scratch_shapes=[pltpu.VMEM((tm, k), BF16)],
        compiler_params=_params("parallel", "arbitrary"),
        name="norm_w_in",
    )(x, g.reshape(1, k), w)


def _s5_param_kernel(lr_ref, li_ref, ls_ref, bt_r_ref, bt_i_ref, c_r_ref, c_i_ref, d_ref,
                     wt_ref, ws_ref, wc_ref, tab_ref, wb_ref, wct_ref, arow_ref):
    row_p = lax.broadcasted_iota(jnp.int32, (POW_ROWS, LANES), 0)
    lane = lax.broadcasted_iota(jnp.int32, (ROWS, LANES), 1)
    lane16 = lax.broadcasted_iota(jnp.int32, (S5_CH, LANES), 1)
    row16 = lax.broadcasted_iota(jnp.int32, (S5_CH, LANES), 0)
    lo = lane < S5_STATE

    wb_ref[...] = jnp.zeros_like(wb_ref)
    wct_ref[...] = jnp.zeros_like(wct_ref)

    for q in range(LANE_PAIRS):
        lr, li = lr_ref[q], li_ref[q]
        step = jnp.exp(ls_ref[q])
        mag = jnp.exp(lr * step)
        ar = mag * jnp.cos(li * step)
        ai = mag * jnp.sin(li * step)
        nr, ni = ar - 1.0, ai
        den = lr * lr + li * li
        cr = (nr * lr + ni * li) / den
        ci = (ni * lr - nr * li) / den
        bt_r, bt_i = bt_r_ref[q], bt_i_ref[q]
        bb_r = cr * bt_r - ci * bt_i
        bb_i = cr * bt_i + ci * bt_r
        c_r, c_i = c_r_ref[q], c_i_ref[q]

        pr = jnp.ones((POW_ROWS, LANES), F32)
        pi = jnp.zeros((POW_ROWS, LANES), F32)
        sr, si = ar, ai
        sq = []
        for m in range(POW_ROWS.bit_length() - 1 + SUBLANES):
            sq.append((sr, si))
            if (1 << m) < POW_ROWS:
                bit = ((row_p >> m) & 1) == 1
                pr, pi = (jnp.where(bit, pr * sr - pi * si, pr),
                          jnp.where(bit, pr * si + pi * sr, pi))
            sr, si = sr * sr - si * si, 2.0 * sr * si
        m0 = CHUNK.bit_length() - 1
        scan_r = jnp.concatenate([sq[m0 + j][0] for j in range(SUBLANES)], axis=0)
        scan_i = jnp.concatenate([sq[m0 + j][1] for j in range(SUBLANES)], axis=0)
        pad = jnp.zeros((LANES - POW_ROWS - SUBLANES, LANES), F32)
        tab_ref[q, 0] = jnp.concatenate([pr, scan_r, pad], axis=0).T
        tab_ref[q, 1] = jnp.concatenate([pi, scan_i, pad], axis=0).T
        arow_ref[q, 0] = pr[:SUBLANES]
        arow_ref[q, 1] = pi[:SUBLANES]

        def expand(tab, k0, sign):
            return jnp.concatenate(
                [jnp.broadcast_to(tab[k0 + sign * s:k0 + sign * s + 1, :], (S5_CH, LANES))
                 for s in range(CHUNK)], axis=0)

        tile = lambda v: jnp.concatenate([v] * CHUNK, axis=0)

        e_r, e_i = expand(pr, CHUNK - 1, -1), expand(pi, CHUNK - 1, -1)
        t_r, t_i = tile(bb_r), tile(bb_i)
        ws_r = _transpose_tiles(t_r * e_r - t_i * e_i)
        ws_i = _transpose_tiles(t_r * e_i + t_i * e_r)
        for h in range(2):
            rows = slice(h * S5_STATE, (h + 1) * S5_STATE)
            ws_ref[2 * q + h] = jnp.concatenate([ws_r[rows], ws_i[rows]], axis=0).astype(BF16)

        e_r, e_i = expand(pr, 1, 1), expand(pi, 1, 1)
        t_r, t_i = tile(c_r), tile(c_i)
        ca_r = t_r * e_r - t_i * e_i
        ca_n = -(t_r * e_i + t_i * e_r)
        wc_ref[2 * q] = jnp.where(lo, ca_r, pltpu.roll(ca_n, S5_STATE, 1)).astype(BF16)
        wc_ref[2 * q + 1] = jnp.where(lo, pltpu.roll(ca_r, S5_STATE, 1), ca_n).astype(BF16)

        e_r, e_i = expand(pr, 0, 1), expand(pi, 0, 1)
        c0_r = t_r * e_r - t_i * e_i
        c0_i = t_r * e_i + t_i * e_r

        for h in range(2):
            g = 2 * q + h
            mine = (lane16 < S5_STATE) == (h == 0)
            bm_r = jnp.where(mine, bb_r, 0.0)
            bm_i = jnp.where(mine, bb_i, 0.0)
            kt = _dot_nt(bm_r, c0_r, HIGHEST) - _dot_nt(bm_i, c0_i, HIGHEST)
            k_lo = kt[:, :LANES] + jnp.where(row16 == lane16, d_ref[g], 0.0)
            k_hi = kt[:, LANES:]
            blocks = []
            half = LANES // S5_CH
            for s in range(CHUNK):
                sh = (s % half) * S5_CH
                keep = lane16 >= sh
                r_lo = pltpu.roll(k_lo, sh, 1) if sh else k_lo
                r_hi = pltpu.roll(k_hi, sh, 1) if sh else k_hi
                if s < half:
                    blk = jnp.concatenate([jnp.where(keep, r_lo, 0.0),
                                           jnp.where(keep, r_hi, r_lo)], axis=1)
                else:
                    blk = jnp.concatenate([jnp.zeros_like(r_lo), jnp.where(keep, r_lo, 0.0)], axis=1)
                blocks.append(blk)
            wt_ref[g] = _transpose_tiles(jnp.concatenate(blocks, axis=0)).astype(BF16)

            r0 = (2 * q + h) * S5_CH
            c0 = q * 2 * LANES
            wb_ref[r0:r0 + S5_CH, c0:c0 + LANES] = bm_r.astype(BF16)
            wb_ref[r0:r0 + S5_CH, c0 + LANES:c0 + 2 * LANES] = bm_i.astype(BF16)
            wct_ref[q, r0:r0 + S5_CH, :LANES] = jnp.where(mine, c_r, 0.0).astype(BF16)
            wct_ref[q, r0:r0 + S5_CH, LANES:] = jnp.where(mine, -c_i, 0.0).astype(BF16)


def _s5_params(lam_re, lam_im, log_step, b_re, b_im, c_re, c_im, d):
    groups = lam_re.shape[0]
    pairs = groups // 2
    slabs = groups // LANE_GROUPS
    pair_row = lambda v: v.reshape(pairs, 1, LANES)
    pair_ch = lambda v: jnp.transpose(v.reshape(pairs, 2, S5_CH, S5_STATE), (0, 2, 1, 3)).reshape(
        pairs, S5_CH, LANES)
    ls = jnp.broadcast_to(log_step[:, None], (groups, S5_STATE))
    d_pad = jnp.pad(d.reshape(groups, 1, S5_CH), ((0, 0), (0, 0), (0, LANES - S5_CH)))
    blk3 = lambda i: (i, 0, 0)
    blk4 = lambda i: (i, 0, 0, 0)
    prow = pl.BlockSpec((LANE_PAIRS, 1, LANES), blk3)
    pch = pl.BlockSpec((LANE_PAIRS, S5_CH, LANES), blk3)
    return pl.pallas_call(
        _s5_param_kernel,
        out_shape=(jax.ShapeDtypeStruct((groups, ROWS, ROWS), BF16),
                   jax.ShapeDtypeStruct((groups, 2 * S5_STATE, ROWS), BF16),
                   jax.ShapeDtypeStruct((groups, ROWS, 2 * S5_STATE), BF16),
                   jax.ShapeDtypeStruct((pairs, 2, LANES, LANES), F32),
                   jax.ShapeDtypeStruct((slabs, LANES, LANE_PAIRS * 2 * LANES), BF16),
                   jax.ShapeDtypeStruct((slabs, LANE_PAIRS, LANES, 2 * LANES), BF16),
                   jax.ShapeDtypeStruct((pairs, 2, SUBLANES, LANES), F32)),
        grid=(slabs,),
        in_specs=[prow, prow, prow, pch, pch, pch, pch,
                  pl.BlockSpec((LANE_GROUPS, 1, LANES), blk3)],
        out_specs=(pl.BlockSpec((LANE_GROUPS, ROWS, ROWS), blk3),
                   pl.BlockSpec((LANE_GROUPS, 2 * S5_STATE, ROWS), blk3),
                   pl.BlockSpec((LANE_GROUPS, ROWS, 2 * S5_STATE), blk3),
                   pl.BlockSpec((LANE_PAIRS, 2, LANES, LANES), blk4),
                   pl.BlockSpec((None, LANES, LANE_PAIRS * 2 * LANES), blk3),
                   pl.BlockSpec((None, LANE_PAIRS, LANES, 2 * LANES), blk4),
                   pl.BlockSpec((LANE_PAIRS, 2, SUBLANES, LANES), blk4)),
        compiler_params=_params("parallel"),
        name="s5_params",
    )(pair_row(lam_re), pair_row(lam_im), pair_row(ls),
      pair_ch(jnp.transpose(b_re, (0, 2, 1))), pair_ch(jnp.transpose(b_im, (0, 2, 1))),
      pair_ch(c_re), pair_ch(c_im), d_pad)


def _s5_seq_kernel(z_ref, wt_ref, ws_ref, wc_ref, tab_ref, y_ref, hf_ref, ut_ref, yt_ref,
                   *, nb, n_chunks):
    scan_steps = n_chunks.bit_length() - 1
    for n in range(nb):
        for s in range(CHUNK):
            xs = z_ref[n, pl.ds(s, n_chunks, stride=CHUNK), :]
            ut_ref[:, s * S5_CH:(s + 1) * S5_CH, n * n_chunks:(n + 1) * n_chunks] = (
                xs.T.astype(BF16).reshape(LANE_GROUPS, S5_CH, n_chunks))

    lane = lax.broadcasted_iota(jnp.int32, (S5_STATE, n_chunks), 1)
    for g in range(LANE_GROUPS):
        ut = ut_ref[g]
        st = _dot(ws_ref[g], ut)
        rows = slice((g % 2) * S5_STATE, (g % 2 + 1) * S5_STATE)
        pw_r = tab_ref[g // 2, 0, rows, :]
        pw_i = tab_ref[g // 2, 1, rows, :]
        fin_r = jnp.zeros((S5_STATE, n_chunks), F32)
        fin_i = jnp.zeros((S5_STATE, n_chunks), F32)
        starts = []
        for n in range(nb):
            cols = slice(n * n_chunks, (n + 1) * n_chunks)
            xr = st[:S5_STATE, cols]
            xi = st[S5_STATE:, cols]
            for k in range(scan_steps):
                d = 1 << k
                keep = lane >= d
                sr = jnp.where(keep, pltpu.roll(xr, d, 1), 0.0)
                si = jnp.where(keep, pltpu.roll(xi, d, 1), 0.0)
                pr = pw_r[:, SCAN_ROW0 + k:SCAN_ROW0 + k + 1]
                pi = pw_i[:, SCAN_ROW0 + k:SCAN_ROW0 + k + 1]
                xr, xi = xr + (pr * sr - pi * si), xi + (pr * si + pi * sr)
            last = lane == n_chunks - 1
            fin_r = fin_r + jnp.where(
                lane == n, jnp.sum(jnp.where(last, xr, 0.0), axis=1, keepdims=True), 0.0)
            fin_i = fin_i + jnp.where(
                lane == n, jnp.sum(jnp.where(last, xi, 0.0), axis=1, keepdims=True), 0.0)
            keep = lane >= 1
            starts.append(jnp.concatenate([jnp.where(keep, pltpu.roll(xr, 1, 1), 0.0),
                                           jnp.where(keep, pltpu.roll(xi, 1, 1), 0.0)], axis=0))
        hf_ref[g, 0] = fin_r
        hf_ref[g, 1] = fin_i
        hs = jnp.concatenate(starts, axis=1).astype(BF16)
        yt_ref[g] = jax.nn.gelu(_dot(wc_ref[g], hs) + _dot(wt_ref[g], ut))

    for n in range(nb):
        for s in range(CHUNK):
            blk = yt_ref[:, s * S5_CH:(s + 1) * S5_CH, n * n_chunks:(n + 1) * n_chunks]
            y_ref[n, pl.ds(s, n_chunks, stride=CHUNK), :] = blk.reshape(LANES, n_chunks).T


def _s5_seq(z, w_t, w_s, w_c, tab, d_s5):
    nb, t_len, _ = z.shape
    n_chunks = t_len // CHUNK
    assert n_chunks == LANES, "the chunk axis must fill one 128-lane tile"
    groups = d_s5 // S5_CH
    kern = functools.partial(_s5_seq_kernel, nb=nb, n_chunks=n_chunks)
    blk3 = lambda i: (i, 0, 0)
    blk4 = lambda i: (i, 0, 0, 0)
    return pl.pallas_call(
        kern,
        out_shape=(jax.ShapeDtypeStruct((nb, t_len, d_s5), F32),
                   jax.ShapeDtypeStruct((groups, 2, S5_STATE, LANES), F32)),
        grid=(groups // LANE_GROUPS,),
        in_specs=[
            pl.BlockSpec((nb, t_len, LANES), lambda i: (0, 0, i)),
            pl.BlockSpec((LANE_GROUPS, ROWS, ROWS), blk3),
            pl.BlockSpec((LANE_GROUPS, 2 * S5_STATE, ROWS), blk3),
            pl.BlockSpec((LANE_GROUPS, ROWS, 2 * S5_STATE), blk3),
            pl.BlockSpec((LANE_PAIRS, 2, LANES, LANES), blk4),
        ],
        out_specs=(pl.BlockSpec((nb, t_len, LANES), lambda i: (0, 0, i)),
                   pl.BlockSpec((LANE_GROUPS, 2, S5_STATE, LANES), blk4)),
        scratch_shapes=[pltpu.VMEM((LANE_GROUPS, ROWS, nb * n_chunks), BF16),
                        pltpu.VMEM((LANE_GROUPS, ROWS, nb * n_chunks), F32)],
        compiler_params=_params("parallel"),
        name="s5_seq",
    )(z, w_t, w_s, w_c, tab)


def _s5_step_kernel(u_ref, h0r_ref, h0i_ref, wb_ref, wct_ref, arow_ref, d_ref,
                    y_ref, hr_ref, hi_ref):
    u = u_ref[...]
    bu = _dot(u.astype(BF16), wb_ref[...])
    y = d_ref[...] * u
    for q in range(LANE_PAIRS):
        cols = slice(q * LANES, (q + 1) * LANES)
        ar = arow_ref[q, 0, 1:2, :]
        ai = arow_ref[q, 1, 1:2, :]
        h0r, h0i = h0r_ref[:, cols], h0i_ref[:, cols]
        hr = ar * h0r - ai * h0i + bu[:, 2 * q * LANES:(2 * q + 1) * LANES]
        hi = ar * h0i + ai * h0r + bu[:, (2 * q + 1) * LANES:(2 * q + 2) * LANES]
        hr_ref[:, cols] = hr
        hi_ref[:, cols] = hi
        y = y + _dot_nt(jnp.concatenate([hr, hi], axis=1).astype(BF16), wct_ref[q])
    y_ref[...] = jax.nn.gelu(y).astype(BF16)


def _s5_step(z, h0r, h0i, wb, wct, arow, d):
    n = z.shape[0]
    slabs = wb.shape[0]
    sw = LANE_PAIRS * LANES
    col = lambda i: (0, i)
    blk3 = lambda i: (i, 0, 0)
    blk4 = lambda i: (i, 0, 0, 0)
    return pl.pallas_call(
        _s5_step_kernel,
        out_shape=(jax.ShapeDtypeStruct((n, slabs * LANES), BF16),
                   jax.ShapeDtypeStruct((n, slabs * sw), F32),
                   jax.ShapeDtypeStruct((n, slabs * sw), F32)),
        grid=(slabs,),
        in_specs=[
            pl.BlockSpec((n, LANES), col),
            pl.BlockSpec((n, sw), col),
            pl.BlockSpec((n, sw), col),
            pl.BlockSpec((None, LANES, 2 * sw), blk3),
            pl.BlockSpec((None, LANE_PAIRS, LANES, 2 * LANES), blk4),
            pl.BlockSpec((LANE_PAIRS, 2, SUBLANES, LANES), blk4),
            pl.BlockSpec((1, LANES), col),
        ],
        out_specs=(pl.BlockSpec((n, LANES), col),
                   pl.BlockSpec((n, sw), col),
                   pl.BlockSpec((n, sw), col)),
        compiler_params=_params("parallel"),
        name="s5_step",
    )(z, h0r, h0i, wb, wct, arow, d)


def _lru_gates(xc, wa_ref, ba, wi_ref, bi, lam):
    blk = xc.shape[1] // LRU_HEADS
    xb16 = xc.astype(BF16)
    ra = jnp.concatenate([_dot(xb16[:, h * blk:(h + 1) * blk], wa_ref[h])
                          for h in range(LRU_HEADS)], axis=-1)
    ri = jnp.concatenate([_dot(xb16[:, h * blk:(h + 1) * blk], wi_ref[h])
                          for h in range(LRU_HEADS)], axis=-1)
    r = jax.nn.sigmoid(ra + ba)
    ig = jax.nn.sigmoid(ri + bi)
    log_a = -LRU_C * r * jax.nn.softplus(-lam)
    a = jnp.exp(log_a)
    mult = jnp.sqrt(1.0 - a * a)
    return a, mult * (ig * xc)


def _lru_seq_kernel(xb_ref, gb_ref, cw_ref, cb_ref, wa_ref, ba_ref, wi_ref, bi_ref, lam_ref,
                    gm_ref, o_ref, hl_ref, xe_ref, a_ref, b_ref, hc_ref, *, tt):
    halo = SUBLANES

    @pl.when(pl.program_id(1) == 0)
    def _():
        xe_ref[0:halo, :] = jnp.zeros((halo, xe_ref.shape[1]), F32)
        hc_ref[...] = jnp.zeros_like(hc_ref)

    xb = xb_ref[...]
    xe_ref[halo:halo + tt, :] = xb
    cw = cw_ref[...]
    xc = cb_ref[...] + xb * cw[CONV_W - 1:CONV_W, :]
    for k in range(1, CONV_W):
        xc = xc + xe_ref[halo - k:halo - k + tt, :] * cw[CONV_W - 1 - k:CONV_W - k, :]
    xe_ref[0:halo, :] = xb[tt - halo:tt, :]

    a, b = _lru_gates(xc, wa_ref, ba_ref[...], wi_ref, bi_ref[...], lam_ref[...])
    a_ref[...] = a
    b_ref[...] = b

    def block(i, h):
        base = pl.multiple_of(i * SUBLANES, SUBLANES)
        for j in range(SUBLANES):
            row = pl.ds(base + j, 1)
            h = a_ref[row, :] * h + b_ref[row, :]
            b_ref[row, :] = h
        return h

    h = lax.fori_loop(0, tt // SUBLANES, block, hc_ref[...])
    hc_ref[...] = h
    hl_ref[...] = h

    out = b_ref[...] * jax.nn.gelu(gb_ref[...])
    o_ref[...] = _rms(out, gm_ref[...]).astype(BF16)


def _lru_seq(z, nseq, t_len, conv_w, conv_b, w_a, b_a, w_i, b_i, lam, g_merge, tt=256):
    d = conv_w.shape[1]
    nt = t_len // tt
    row = lambda v: v.reshape(1, d)
    const2 = lambda n, t: (0, 0)
    const3 = lambda n, t: (0, 0, 0)
    kern = functools.partial(_lru_seq_kernel, tt=tt)
    return pl.pallas_call(
        kern,
        out_shape=(jax.ShapeDtypeStruct((nseq * t_len, d), BF16),
                   jax.ShapeDtypeStruct((nseq, 1, d), F32)),
        grid=(nseq, nt),
        in_specs=[
            pl.BlockSpec((tt, d), lambda n, t: (n * nt + t, 1)),
            pl.BlockSpec((tt, d), lambda n, t: (n * nt + t, 2)),
            pl.BlockSpec((CONV_W, d), const2),
            pl.BlockSpec((1, d), const2),
            pl.BlockSpec(w_a.shape, const3),
            pl.BlockSpec((1, d), const2),
            pl.BlockSpec(w_i.shape, const3),
            pl.BlockSpec((1, d), const2),
            pl.BlockSpec((1, d), const2),
            pl.BlockSpec((1, d), const2),
        ],
        out_specs=(pl.BlockSpec((tt, d), lambda n, t: (n * nt + t, 0)),
                   pl.BlockSpec((None, 1, d), lambda n, t: (n, 0, 0))),
        scratch_shapes=[pltpu.VMEM((tt + SUBLANES, d), F32),
                        pltpu.VMEM((tt, d), F32),
                        pltpu.VMEM((tt, d), F32),
                        pltpu.VMEM((1, d), F32)],
        compiler_params=_params("parallel", "arbitrary"),
        name="rglru_seq",
    )(z, z, conv_w, row(conv_b), w_a, row(b_a), w_i, row(b_i), row(lam), row(g_merge))


def _lru_step_kernel(xb_ref, gb_ref, c0_ref, c1_ref, c2_ref, h0_ref, cw_ref, cb_ref,
                     wa_ref, ba_ref, wi_ref, bi_ref, lam_ref, gm_ref, o_ref, h_ref):
    xb = xb_ref[...]
    cw = cw_ref[...]
    xc = (cb_ref[...] + c0_ref[...] * cw[0:1, :] + c1_ref[...] * cw[1:2, :]
          + c2_ref[...] * cw[2:3, :] + xb * cw[3:4, :])
    a, b = _lru_gates(xc, wa_ref, ba_ref[...], wi_ref, bi_ref[...], lam_ref[...])
    h = a * h0_ref[...] + b
    h_ref[...] = h
    out = h * jax.nn.gelu(gb_ref[...])
    o_ref[...] = _rms(out, gm_ref[...]).astype(BF16)


def _lru_step(z, conv0, h0, conv_w, conv_b, w_a, b_a, w_i, b_i, lam, g_merge):
    n = z.shape[0]
    d = conv_w.shape[1]
    row = lambda v: v.reshape(1, d)
    full = lambda shape: pl.BlockSpec(shape, lambda i: (0,) * len(shape))
    nd = full((n, d))
    rd = full((1, d))
    return pl.pallas_call(
        _lru_step_kernel,
        out_shape=(jax.ShapeDtypeStruct((n, d), BF16), jax.ShapeDtypeStruct((n, d), F32)),
        grid=(1,),
        in_specs=[
            pl.BlockSpec((n, d), lambda i: (0, 1)),
            pl.BlockSpec((n, d), lambda i: (0, 2)),
            nd, nd, nd, nd,
            full((CONV_W, d)), rd,
            full(w_a.shape), rd, full(w_i.shape), rd, rd, rd,
        ],
        out_specs=(nd, nd),
        compiler_params=_params("arbitrary"),
        name="rglru_step",
    )(z, z, conv0[:, 0], conv0[:, 1], conv0[:, 2], h0, conv_w, row(conv_b),
      w_a, row(b_a), w_i, row(b_i), row(lam), row(g_merge))


def _glu_kernel(y_ref, w_ref, b_ref, g_ref, o_ref):
    y = y_ref[...]
    gate = jax.nn.sigmoid(_dot(y.astype(BF16), w_ref[...]) + b_ref[...])
    o_ref[...] = _rms(y.astype(F32) * gate, g_ref[...]).astype(BF16)


def _glu(y, w, b, g, tm):
    m, d = y.shape
    return pl.pallas_call(
        _glu_kernel,
        out_shape=jax.ShapeDtypeStruct((m, d), BF16),
        grid=(m // tm,),
        in_specs=[
            pl.BlockSpec((tm, d), lambda i: (i, 0)),
            pl.BlockSpec((d, d), lambda i: (0, 0)),
            pl.BlockSpec((1, d), lambda i: (0, 0)),
            pl.BlockSpec((1, d), lambda i: (0, 0)),
        ],
        out_specs=pl.BlockSpec((tm, d), lambda i: (i, 0)),
        compiler_params=_params("parallel"),
        name="s5_glu",
    )(y, w, b.reshape(1, d), g.reshape(1, d))


def _out_proj_kernel(x_ref, ma_ref, mb_ref, wa_ref, wb_ref, o_ref):
    o_ref[...] = x_ref[...] + _dot(ma_ref[...], wa_ref[...]) + _dot(mb_ref[...], wb_ref[...])


def _out_proj(x, ma, mb, w, tm, tn):
    m, n = x.shape
    ka = ma.shape[1]
    kb = mb.shape[1]
    nj = n // tn
    return pl.pallas_call(
        _out_proj_kernel,
        out_shape=jax.ShapeDtypeStruct((m, n), F32),
        grid=(m // tm, nj),
        in_specs=[
            pl.BlockSpec((tm, tn), lambda i, j: (i, j)),
            pl.BlockSpec((tm, ka), lambda i, j: (i, 0)),
            pl.BlockSpec((tm, kb), lambda i, j: (i, 0)),
            pl.BlockSpec((ka, tn), lambda i, j: (0, j)),
            pl.BlockSpec((kb, tn), lambda i, j: (1, j)),
        ],
        out_specs=pl.BlockSpec((tm, tn), lambda i, j: (i, j)),
        compiler_params=_params("parallel", "arbitrary"),
        name="out_proj",
    )(x, ma, mb, w, w)


def _mlp_kernel(x_ref, g_ref, wu_ref, wd_ref, o_ref, h_ref):
    @pl.when(pl.program_id(1) == 0)
    def _():
        x = x_ref[...]
        h_ref[...] = _rms(x, g_ref[...]).astype(BF16)
        o_ref[...] = x

    up = _dot(h_ref[...], wu_ref[...])
    act = jnp.square(jnp.maximum(up, 0.0)).astype(BF16)
    o_ref[...] += _dot(act, wd_ref[...])


def _mlp(x, g, w_up, w_down, tm, tf):
    m, d = x.shape
    f = w_up.shape[1]
    return pl.pallas_call(
        _mlp_kernel,
        out_shape=jax.ShapeDtypeStruct((m, d), F32),
        grid=(m // tm, f // tf),
        in_specs=[
            pl.BlockSpec((tm, d), lambda i, j: (i, 0)),
            pl.BlockSpec((1, d), lambda i, j: (0, 0)),
            pl.BlockSpec((d, tf), lambda i, j: (0, j)),
            pl.BlockSpec((tf, d), lambda i, j: (j, 0)),
        ],
        out_specs=pl.BlockSpec((tm, d), lambda i, j: (i, 0)),
        scratch_shapes=[pltpu.VMEM((tm, d), BF16)],
        compiler_params=_params("parallel", "arbitrary"),
        name="mlp",
    )(x, g.reshape(1, d), w_up, w_down)


def _ple_kernel(x_ref, p_ref, g_ref, wg_ref, wp_ref, gf_ref, o_ref, *, final):
    x = x_ref[...]
    gate = jax.nn.sigmoid(_dot(_rms(x, g_ref[...]).astype(BF16), wg_ref[...]))
    pe = _dot(p_ref[...].astype(BF16), wp_ref[...])
    x = x + pe * gate
    o_ref[...] = _rms(x, gf_ref[...]) if final else x


def _ple(x, p, g, w_gate, w_ple, g_final, tm, final):
    m, d = x.shape
    dp = p.shape[1]
    return pl.pallas_call(
        functools.partial(_ple_kernel, final=final),
        out_shape=jax.ShapeDtypeStruct((m, d), F32),
        grid=(m // tm,),
        in_specs=[
            pl.BlockSpec((tm, d), lambda i: (i, 0)),
            pl.BlockSpec((tm, dp), lambda i: (i, 0)),
            pl.BlockSpec((1, d), lambda i: (0, 0)),
            pl.BlockSpec((d, d), lambda i: (0, 0)),
            pl.BlockSpec((dp, d), lambda i: (0, 0)),
            pl.BlockSpec((1, d), lambda i: (0, 0)),
        ],
        out_specs=pl.BlockSpec((tm, d), lambda i: (i, 0)),
        compiler_params=_params("parallel"),
        name="ple_final",
    )(x, p, g.reshape(1, d), w_gate, w_ple, g_final.reshape(1, d))


def _tail(x, ma, mb, p, w, g_final, final, tm, tm_mlp, tm_ple):
    x = _out_proj(x, ma, mb, w['w_out'], tm, 512)
    x = _mlp(x, w['g_mlp'], w['w_up'], w['w_down'], tm_mlp, 1024)
    return _ple(x, p, w['g_ple'], w['w_ple_gate'], w['w_ple'], g_final, tm_ple, final)


def kernel(x_prompt, x_sample, state_s5_re, state_s5_im, state_lru, state_conv, p_prompt, p_sample,
           g_mix, w_in, s5_lam_re, s5_lam_im, s5_log_step, s5_b_re, s5_b_im, s5_c_re, s5_c_im, s5_d,
           s5_w_glu, s5_b_glu, conv_w, conv_b, lru_w_a, lru_b_a, lru_w_i, lru_b_i, lru_lam,
           g_merge_a, g_merge_b, w_out, g_mlp, w_up, w_down, g_ple, w_ple_gate, w_ple, g_final):
    depth = g_mix.shape[0]
    nb, t_len, d_model = x_prompt.shape
    ns = x_sample.shape[0]
    d_s5 = s5_d.shape[1]
    d_lru = conv_w.shape[2]
    groups = d_s5 // S5_CH

    xp = x_prompt.reshape(nb * t_len, d_model)
    xs = x_sample.reshape(ns, d_model)
    outs = [[] for _ in range(8)]
    for l in range(depth):
        final = l == depth - 1
        w = dict(w_out=w_out[l].astype(BF16), g_mlp=g_mlp[l], w_up=w_up[l].astype(BF16),
                 w_down=w_down[l].astype(BF16), g_ple=g_ple[l],
                 w_ple_gate=w_ple_gate[l].astype(BF16), w_ple=w_ple[l].astype(BF16))
        w_in_b = w_in[l].astype(BF16)
        w_glu_b = s5_w_glu[l].astype(BF16)
        lru_args = (conv_w[l], conv_b[l], lru_w_a[l].astype(BF16), lru_b_a[l],
                    lru_w_i[l].astype(BF16), lru_b_i[l], lru_lam[l], g_merge_b[l])
        w_t, w_s, w_c, tab, wb, wct, arow = _s5_params(
            s5_lam_re[l], s5_lam_im[l], s5_log_step[l], s5_b_re[l], s5_b_im[l],
            s5_c_re[l], s5_c_im[l], s5_d[l])

        z = _norm_matmul(xp, g_mix[l], w_in_b, 1024, 512)
        z4 = z.reshape(nb, t_len, 3 * d_s5)
        y, hf = _s5_seq(z4, w_t, w_s, w_c, tab, d_s5)
        ma = _glu(y.reshape(nb * t_len, d_s5), w_glu_b, s5_b_glu[l], g_merge_a[l], 1024)
        mb, lru_h = _lru_seq(z, nb, t_len, *lru_args)
        xp = _tail(xp, ma, mb, p_prompt[l].reshape(nb * t_len, -1), w, g_final, final, 1024, 512, 256)
        hf = jnp.transpose(hf[:, :, :, :nb], (1, 3, 0, 2))
        outs[0].append(hf[0])
        outs[1].append(hf[1])
        outs[2].append(lru_h.reshape(nb, d_lru))
        outs[3].append(z4[:, t_len - (CONV_W - 1):, d_s5:d_s5 + d_lru])

        zs = _norm_matmul(xs, g_mix[l], w_in_b, ns, 512)
        ys, hsr, hsi = _s5_step(zs, state_s5_re[l].reshape(ns, groups * S5_STATE),
                                state_s5_im[l].reshape(ns, groups * S5_STATE),
                                wb, wct, arow, s5_d[l].reshape(1, d_s5))
        mas = _glu(ys, w_glu_b, s5_b_glu[l], g_merge_a[l], ns)
        mbs, lru_hs = _lru_step(zs, state_conv[l], state_lru[l], *lru_args)
        xs = _tail(xs, mas, mbs, p_sample[l].reshape(ns, -1), w, g_final, final, ns, ns, ns)
        outs[4].append(hsr.reshape(ns, groups, S5_STATE))
        outs[5].append(hsi.reshape(ns, groups, S5_STATE))
        outs[6].append(lru_hs)
        outs[7].append(jnp.concatenate(
            [state_conv[l][:, 1:], zs[:, None, d_s5:d_s5 + d_lru]], axis=1))
    return (xp.reshape(nb, t_len, d_model), xs.reshape(ns, 1, d_model),
            *(jnp.stack(o) for o in outs))
```

```python
import functools

import jax
import jax.numpy as jnp
from jax import lax
from jax.experimental import pallas as pl
from jax.experimental.pallas import tpu as pltpu

F32 = jnp.float32
BF16 = jnp.bfloat16
HIGHEST = lax.Precision.HIGHEST

EPS = 1e-6
LRU_C = 8.0
S5_CH = 16
S5_STATE = 64
LRU_HEADS = 4
CONV_W = 4
CHUNK = 16
SUBLANES = 8
LANES = 128
LANE_GROUPS = LANES // S5_CH
LANE_PAIRS = LANE_GROUPS // 2
ROWS = CHUNK * S5_CH
POW_ROWS = 32
SCAN_ROW0 = POW_ROWS
VMEM_LIMIT = 56 * 1024 * 1024


def _params(*sem):
    return pltpu.CompilerParams(dimension_semantics=sem, vmem_limit_bytes=VMEM_LIMIT)


def _rms(x, g):
    return x * lax.rsqrt(jnp.mean(x * x, axis=-1, keepdims=True) + EPS) * g


def _dot(a, b):
    return jnp.dot(a, b, preferred_element_type=F32)


def _dot_nt(a, b, precision=None):
    return lax.dot_general(a, b, (((1,), (1,)), ((), ())), precision=precision,
                           preferred_element_type=F32)


def _transpose_tiles(x):
    r, c = x.shape
    return jnp.concatenate(
        [jnp.concatenate([x[i:i + LANES, j:j + LANES].T for i in range(0, r, LANES)], axis=1)
         for j in range(0, c, LANES)], axis=0)


def _norm_matmul_kernel(x_ref, g_ref, w_ref, o_ref, h_ref):
    @pl.when(pl.program_id(1) == 0)
    def _():
        h_ref[...] = _rms(x_ref[...], g_ref[...]).astype(BF16)

    o_ref[...] = _dot(h_ref[...], w_ref[...])


def _norm_matmul(x, g, w, tm, tn):
    m, k = x.shape
    n = w.shape[1]
    return pl.pallas_call(
        _norm_matmul_kernel,
        out_shape=jax.ShapeDtypeStruct((m, n), F32),
        grid=(m // tm, n // tn),
        in_specs=[
            pl.BlockSpec((tm, k), lambda i, j: (i, 0)),
            pl.BlockSpec((1, k), lambda i, j: (0, 0)),
            pl.BlockSpec((k, tn), lambda i, j: (0, j)),
        ],
        out_specs=pl.BlockSpec((tm, tn), lambda i, j: (i, j)),
        scratch_shapes=[pltpu.VMEM((tm, k), BF16)],
        compiler_params=_params("parallel", "arbitrary"),
        name="norm_w_in",
    )(x, g.reshape(1, k), w)


def _s5_param_kernel(lr_ref, li_ref, ls_ref, bt_r_ref, bt_i_ref, c_r_ref, c_i_ref, d_ref,
                     wt_ref, ws_ref, wc_ref, tab_ref, wb_ref, wct_ref, arow_ref):
    row_p = lax.broadcasted_iota(jnp.int32, (POW_ROWS, LANES), 0)
    lane = lax.broadcasted_iota(jnp.int32, (ROWS, LANES), 1)
    lane16 = lax.broadcasted_iota(jnp.int32, (S5_CH, LANES), 1)
    row16 = lax.broadcasted_iota(jnp.int32, (S5_CH, LANES), 0)
    lo = lane < S5_STATE

    wb_ref[...] = jnp.zeros_like(wb_ref)
    wct_ref[...] = jnp.zeros_like(wct_ref)

    for q in range(LANE_PAIRS):
        lr, li = lr_ref[q], li_ref[q]
        step = jnp.exp(ls_ref[q])
        mag = jnp.exp(lr * step)
        ar = mag * jnp.cos(li * step)
        ai = mag * jnp.sin(li * step)
        nr, ni = ar - 1.0, ai
        den = lr * lr + li * li
        cr = (nr * lr + ni * li) / den
        ci = (ni * lr - nr * li) / den
        bt_r, bt_i = bt_r_ref[q], bt_i_ref[q]
        bb_r = cr * bt_r - ci * bt_i
        bb_i = cr * bt_i + ci * bt_r
        c_r, c_i = c_r_ref[q], c_i_ref[q]

        pr = jnp.ones((POW_ROWS, LANES), F32)
        pi = jnp.zeros((POW_ROWS, LANES), F32)
        sr, si = ar, ai
        sq = []
        for m in range(POW_ROWS.bit_length() - 1 + SUBLANES):
            sq.append((sr, si))
            if (1 << m) < POW_ROWS:
                bit = ((row_p >> m) & 1) == 1
                pr, pi = (jnp.where(bit, pr * sr - pi * si, pr),
                          jnp.where(bit, pr * si + pi * sr, pi))
            sr, si = sr * sr - si * si, 2.0 * sr * si
        m0 = CHUNK.bit_length() - 1
        scan_r = jnp.concatenate([sq[m0 + j][0] for j in range(SUBLANES)], axis=0)
        scan_i = jnp.concatenate([sq[m0 + j][1] for j in range(SUBLANES)], axis=0)
        pad = jnp.zeros((LANES - POW_ROWS - SUBLANES, LANES), F32)
        tab_ref[q, 0] = jnp.concatenate([pr, scan_r, pad], axis=0).T
        tab_ref[q, 1] = jnp.concatenate([pi, scan_i, pad], axis=0).T
        arow_ref[q, 0] = pr[:SUBLANES]
        arow_ref[q, 1] = pi[:SUBLANES]

        def expand(tab, k0, sign):
            return jnp.concatenate(
                [jnp.broadcast_to(tab[k0 + sign * s:k0 + sign * s + 1, :], (S5_CH, LANES))
                 for s in range(CHUNK)], axis=0)

        tile = lambda v: jnp.concatenate([v] * CHUNK, axis=0)

        e_r, e_i = expand(pr, CHUNK - 1, -1), expand(pi, CHUNK - 1, -1)
        t_r, t_i = tile(bb_r), tile(bb_i)
        ws_r = _transpose_tiles(t_r * e_r - t_i * e_i)
        ws_i = _transpose_tiles(t_r * e_i + t_i * e_r)
        for h in range(2):
            rows = slice(h * S5_STATE, (h + 1) * S5_STATE)
            ws_ref[2 * q + h] = jnp.concatenate([ws_r[rows], ws_i[rows]], axis=0).astype(BF16)

        e_r, e_i = expand(pr, 1, 1), expand(pi, 1, 1)
        t_r, t_i = tile(c_r), tile(c_i)
        ca_r = t_r * e_r - t_i * e_i
        ca_n = -(t_r * e_i + t_i * e_r)
        wc_ref[2 * q] = jnp.where(lo, ca_r, pltpu.roll(ca_n, S5_STATE, 1)).astype(BF16)
        wc_ref[2 * q + 1] = jnp.where(lo, pltpu.roll(ca_r, S5_STATE, 1), ca_n).astype(BF16)

        e_r, e_i = expand(pr, 0, 1), expand(pi, 0, 1)
        c0_r = t_r * e_r - t_i * e_i
        c0_i = t_r * e_i + t_i * e_r

        for h in range(2):
            g = 2 * q + h
            mine = (lane16 < S5_STATE) == (h == 0)
            bm_r = jnp.where(mine, bb_r, 0.0)
            bm_i = jnp.where(mine, bb_i, 0.0)
            kt = _dot_nt(bm_r, c0_r, HIGHEST) - _dot_nt(bm_i, c0_i, HIGHEST)
            k_lo = kt[:, :LANES] + jnp.where(row16 == lane16, d_ref[g], 0.0)
            k_hi = kt[:, LANES:]
            blocks = []
            half = LANES // S5_CH
            for s in range(CHUNK):
                sh = (s % half) * S5_CH
                keep = lane16 >= sh
                r_lo = pltpu.roll(k_lo, sh, 1) if sh else k_lo
                r_hi = pltpu.roll(k_hi, sh, 1) if sh else k_hi
                if s < half:
                    blk = jnp.concatenate([jnp.where(keep, r_lo, 0.0),
                                           jnp.where(keep, r_hi, r_lo)], axis=1)
                else:
                    blk = jnp.concatenate([jnp.zeros_like(r_lo), jnp.where(keep, r_lo, 0.0)], axis=1)
                blocks.append(blk)
            wt_ref[g] = _transpose_tiles(jnp.concatenate(blocks, axis=0)).astype(BF16)

            r0 = (2 * q + h) * S5_CH
            c0 = q * 2 * LANES
            wb_ref[r0:r0 + S5_CH, c0:c0 + LANES] = bm_r.astype(BF16)
            wb_ref[r0:r0 + S5_CH, c0 + LANES:c0 + 2 * LANES] = bm_i.astype(BF16)
            wct_ref[q, r0:r0 + S5_CH, :LANES] = jnp.where(mine, c_r, 0.0).astype(BF16)
            wct_ref[q, r0:r0 + S5_CH, LANES:] = jnp.where(mine, -c_i, 0.0).astype(BF16)


def _s5_params(lam_re, lam_im, log_step, b_re, b_im, c_re, c_im, d):
    groups = lam_re.shape[0]
    pairs = groups // 2
    slabs = groups // LANE_GROUPS
    pair_row = lambda v: v.reshape(pairs, 1, LANES)
    pair_ch = lambda v: jnp.transpose(v.reshape(pairs, 2, S5_CH, S5_STATE), (0, 2, 1, 3)).reshape(
        pairs, S5_CH, LANES)
    ls = jnp.broadcast_to(log_step[:, None], (groups, S5_STATE))
    d_pad = jnp.pad(d.reshape(groups, 1, S5_CH), ((0, 0), (0, 0), (0, LANES - S5_CH)))
    blk3 = lambda i: (i, 0, 0)
    blk4 = lambda i: (i, 0, 0, 0)
    prow = pl.BlockSpec((LANE_PAIRS, 1, LANES), blk3)
    pch = pl.BlockSpec((LANE_PAIRS, S5_CH, LANES), blk3)
    return pl.pallas_call(
        _s5_param_kernel,
        out_shape=(jax.ShapeDtypeStruct((groups, ROWS, ROWS), BF16),
                   jax.ShapeDtypeStruct((groups, 2 * S5_STATE, ROWS), BF16),
                   jax.ShapeDtypeStruct((groups, ROWS, 2 * S5_STATE), BF16),
                   jax.ShapeDtypeStruct((pairs, 2, LANES, LANES), F32),
                   jax.ShapeDtypeStruct((slabs, LANES, LANE_PAIRS * 2 * LANES), BF16),
                   jax.ShapeDtypeStruct((slabs, LANE_PAIRS, LANES, 2 * LANES), BF16),
                   jax.ShapeDtypeStruct((pairs, 2, SUBLANES, LANES), F32)),
        grid=(slabs,),
        in_specs=[prow, prow, prow, pch, pch, pch, pch,
                  pl.BlockSpec((LANE_GROUPS, 1, LANES), blk3)],
        out_specs=(pl.BlockSpec((LANE_GROUPS, ROWS, ROWS), blk3),
                   pl.BlockSpec((LANE_GROUPS, 2 * S5_STATE, ROWS), blk3),
                   pl.BlockSpec((LANE_GROUPS, ROWS, 2 * S5_STATE), blk3),
                   pl.BlockSpec((LANE_PAIRS, 2, LANES, LANES), blk4),
                   pl.BlockSpec((None, LANES, LANE_PAIRS * 2 * LANES), blk3),
                   pl.BlockSpec((None, LANE_PAIRS, LANES, 2 * LANES), blk4),
                   pl.BlockSpec((LANE_PAIRS, 2, SUBLANES, LANES), blk4)),
        compiler_params=_params("parallel"),
        name="s5_params",
    )(pair_row(lam_re), pair_row(lam_im), pair_row(ls),
      pair_ch(jnp.transpose(b_re, (0, 2, 1))), pair_ch(jnp.transpose(b_im, (0, 2, 1))),
      pair_ch(c_re), pair_ch(c_im), d_pad)


def _s5_seq_kernel(z_ref, wt_ref, ws_ref, wc_ref, tab_ref, y_ref, hf_ref, ut_ref, yt_ref,
                   sr_ref, si_ref, *, nb, n_chunks):
    scan_steps = n_chunks.bit_length() - 1
    for n in range(nb):
        for s in range(CHUNK):
            xs = z_ref[n, pl.ds(s, n_chunks, stride=CHUNK), :]
            ut_ref[:, s * S5_CH:(s + 1) * S5_CH, n * n_chunks:(n + 1) * n_chunks] = (
                xs.T.astype(BF16).reshape(LANE_GROUPS, S5_CH, n_chunks))

    for g in range(LANE_GROUPS):
        st = _dot(ws_ref[g], ut_ref[g])
        sr_ref[g * S5_STATE:(g + 1) * S5_STATE, :] = st[:S5_STATE]
        si_ref[g * S5_STATE:(g + 1) * S5_STATE, :] = st[S5_STATE:]

    srows = LANE_GROUPS * S5_STATE
    lane = lax.broadcasted_iota(jnp.int32, (srows, n_chunks), 1)
    tab_r = jnp.concatenate([tab_ref[q, 0] for q in range(LANE_PAIRS)], axis=0)
    tab_i = jnp.concatenate([tab_ref[q, 1] for q in range(LANE_PAIRS)], axis=0)
    fin_r = jnp.zeros((srows, n_chunks), F32)
    fin_i = jnp.zeros((srows, n_chunks), F32)
    for n in range(nb):
        cols = slice(n * n_chunks, (n + 1) * n_chunks)
        xr = sr_ref[:, cols]
        xi = si_ref[:, cols]
        for k in range(scan_steps):
            d = 1 << k
            keep = lane >= d
            sr = jnp.where(keep, pltpu.roll(xr, d, 1), 0.0)
            si = jnp.where(keep, pltpu.roll(xi, d, 1), 0.0)
            pr = tab_r[:, SCAN_ROW0 + k:SCAN_ROW0 + k + 1]
            pi = tab_i[:, SCAN_ROW0 + k:SCAN_ROW0 + k + 1]
            xr, xi = xr + (pr * sr - pi * si), xi + (pr * si + pi * sr)
        last = lane == n_chunks - 1
        fin_r = fin_r + jnp.where(
            lane == n, jnp.sum(jnp.where(last, xr, 0.0), axis=1, keepdims=True), 0.0)
        fin_i = fin_i + jnp.where(
            lane == n, jnp.sum(jnp.where(last, xi, 0.0), axis=1, keepdims=True), 0.0)
        keep = lane >= 1
        sr_ref[:, cols] = jnp.where(keep, pltpu.roll(xr, 1, 1), 0.0)
        si_ref[:, cols] = jnp.where(keep, pltpu.roll(xi, 1, 1), 0.0)

    for g in range(LANE_GROUPS):
        rows = slice(g * S5_STATE, (g + 1) * S5_STATE)
        hf_ref[g, 0] = fin_r[rows]
        hf_ref[g, 1] = fin_i[rows]
        hs = jnp.concatenate([sr_ref[rows, :], si_ref[rows, :]], axis=0).astype(BF16)
        yt_ref[g] = jax.nn.gelu(_dot(wc_ref[g], hs) + _dot(wt_ref[g], ut_ref[g]))

    for n in range(nb):
        for s in range(CHUNK):
            blk = yt_ref[:, s * S5_CH:(s + 1) * S5_CH, n * n_chunks:(n + 1) * n_chunks]
            y_ref[n, pl.ds(s, n_chunks, stride=CHUNK), :] = blk.reshape(LANES, n_chunks).T


def _s5_seq(z, w_t, w_s, w_c, tab, d_s5):
    nb, t_len, _ = z.shape
    n_chunks = t_len // CHUNK
    assert n_chunks == LANES, "the chunk axis must fill one 128-lane tile"
    groups = d_s5 // S5_CH
    kern = functools.partial(_s5_seq_kernel, nb=nb, n_chunks=n_chunks)
    blk3 = lambda i: (i, 0, 0)
    blk4 = lambda i: (i, 0, 0, 0)
    return pl.pallas_call(
        kern,
        out_shape=(jax.ShapeDtypeStruct((nb, t_len, d_s5), F32),
                   jax.ShapeDtypeStruct((groups, 2, S5_STATE, LANES), F32)),
        grid=(groups // LANE_GROUPS,),
        in_specs=[
            pl.BlockSpec((nb, t_len, LANES), lambda i: (0, 0, i)),
            pl.BlockSpec((LANE_GROUPS, ROWS, ROWS), blk3),
            pl.BlockSpec((LANE_GROUPS, 2 * S5_STATE, ROWS), blk3),
            pl.BlockSpec((LANE_GROUPS, ROWS, 2 * S5_STATE), blk3),
            pl.BlockSpec((LANE_PAIRS, 2, LANES, LANES), blk4),
        ],
        out_specs=(pl.BlockSpec((nb, t_len, LANES), lambda i: (0, 0, i)),
                   pl.BlockSpec((LANE_GROUPS, 2, S5_STATE, LANES), blk4)),
        scratch_shapes=[pltpu.VMEM((LANE_GROUPS, ROWS, nb * n_chunks), BF16),
                        pltpu.VMEM((LANE_GROUPS, ROWS, nb * n_chunks), F32),
                        pltpu.VMEM((LANE_GROUPS * S5_STATE, nb * n_chunks), F32),
                        pltpu.VMEM((LANE_GROUPS * S5_STATE, nb * n_chunks), F32)],
        compiler_params=_params("parallel"),
        name="s5_seq",
    )(z, w_t, w_s, w_c, tab)


def _s5_step_kernel(u_ref, h0r_ref, h0i_ref, wb_ref, wct_ref, arow_ref, d_ref,
                    y_ref, hr_ref, hi_ref):
    u = u_ref[...]
    bu = _dot(u.astype(BF16), wb_ref[...])
    y = d_ref[...] * u
    for q in range(LANE_PAIRS):
        cols = slice(q * LANES, (q + 1) * LANES)
        ar = arow_ref[q, 0, 1:2, :]
        ai = arow_ref[q, 1, 1:2, :]
        h0r, h0i = h0r_ref[:, cols], h0i_ref[:, cols]
        hr = ar * h0r - ai * h0i + bu[:, 2 * q * LANES:(2 * q + 1) * LANES]
        hi = ar * h0i + ai * h0r + bu[:, (2 * q + 1) * LANES:(2 * q + 2) * LANES]
        hr_ref[:, cols] = hr
        hi_ref[:, cols] = hi
        y = y + _dot_nt(jnp.concatenate([hr, hi], axis=1).astype(BF16), wct_ref[q])
    y_ref[...] = jax.nn.gelu(y).astype(BF16)


def _s5_step(z, h0r, h0i, wb, wct, arow, d):
    n = z.shape[0]
    slabs = wb.shape[0]
    sw = LANE_PAIRS * LANES
    col = lambda i: (0, i)
    blk3 = lambda i: (i, 0, 0)
    blk4 = lambda i: (i, 0, 0, 0)
    return pl.pallas_call(
        _s5_step_kernel,
        out_shape=(jax.ShapeDtypeStruct((n, slabs * LANES), BF16),
                   jax.ShapeDtypeStruct((n, slabs * sw), F32),
                   jax.ShapeDtypeStruct((n, slabs * sw), F32)),
        grid=(slabs,),
        in_specs=[
            pl.BlockSpec((n, LANES), col),
            pl.BlockSpec((n, sw), col),
            pl.BlockSpec((n, sw), col),
            pl.BlockSpec((None, LANES, 2 * sw), blk3),
            pl.BlockSpec((None, LANE_PAIRS, LANES, 2 * LANES), blk4),
            pl.BlockSpec((LANE_PAIRS, 2, SUBLANES, LANES), blk4),
            pl.BlockSpec((1, LANES), col),
        ],
        out_specs=(pl.BlockSpec((n, LANES), col),
                   pl.BlockSpec((n, sw), col),
                   pl.BlockSpec((n, sw), col)),
        compiler_params=_params("parallel"),
        name="s5_step",
    )(z, h0r, h0i, wb, wct, arow, d)


def _lru_gate_block(xc, wa, ba, wi, bi, lam):
    xb16 = xc.astype(BF16)
    r = jax.nn.sigmoid(_dot(xb16, wa) + ba)
    ig = jax.nn.sigmoid(_dot(xb16, wi) + bi)
    log_a = -LRU_C * r * jax.nn.softplus(-lam)
    a = jnp.exp(log_a)
    mult = jnp.sqrt(1.0 - a * a)
    return a, mult * (ig * xc)


def _lru_gates(xc, wa_ref, ba, wi_ref, bi, lam):
    blk = xc.shape[1] // LRU_HEADS
    parts = [_lru_gate_block(xc[:, h * blk:(h + 1) * blk], wa_ref[h], ba[:, h * blk:(h + 1) * blk],
                             wi_ref[h], bi[:, h * blk:(h + 1) * blk],
                             lam[:, h * blk:(h + 1) * blk]) for h in range(LRU_HEADS)]
    return (jnp.concatenate([p[0] for p in parts], axis=-1),
            jnp.concatenate([p[1] for p in parts], axis=-1))


def _lru_seq_kernel(xb_ref, gb_ref, cw_ref, cb_ref, wa_ref, ba_ref, wi_ref, bi_ref, lam_ref,
                    gm_ref, o_ref, hl_ref, xe_ref, a_ref, b_ref, hc_ref, *, tt):
    halo = SUBLANES
    nseq, _, d = xb_ref.shape
    blk = d // LRU_HEADS

    @pl.when(pl.program_id(0) == 0)
    def _():
        xe_ref[:, 0:halo, :] = jnp.zeros((nseq, halo, d), F32)
        hc_ref[...] = jnp.zeros_like(hc_ref)

    xe_ref[:, halo:halo + tt, :] = xb_ref[...]
    for h in range(LRU_HEADS):
        cols = slice(h * blk, (h + 1) * blk)
        cw = cw_ref[:, cols]
        xc = cb_ref[:, cols] + xe_ref[:, halo:halo + tt, cols] * cw[CONV_W - 1:CONV_W, :]
        for k in range(1, CONV_W):
            xc = xc + xe_ref[:, halo - k:halo - k + tt, cols] * cw[CONV_W - 1 - k:CONV_W - k, :]
        a, b = _lru_gate_block(xc.reshape(nseq * tt, blk), wa_ref[h], ba_ref[:, cols],
                               wi_ref[h], bi_ref[:, cols], lam_ref[:, cols])
        a_ref[:, :, cols] = a.reshape(nseq, tt, blk)
        b_ref[:, :, cols] = b.reshape(nseq, tt, blk)
    xe_ref[:, 0:halo, :] = xb_ref[:, tt - halo:tt, :]

    def block(i, h):
        base = pl.multiple_of(i * SUBLANES, SUBLANES)
        for j in range(SUBLANES):
            row = pl.ds(base + j, 1)
            h = a_ref[:, row, :] * h + b_ref[:, row, :]
            b_ref[:, row, :] = h
        return h

    h = lax.fori_loop(0, tt // SUBLANES, block, hc_ref[...])
    hc_ref[...] = h
    hl_ref[...] = h

    out = b_ref[...] * jax.nn.gelu(gb_ref[...])
    o_ref[...] = _rms(out, gm_ref[...]).astype(BF16)


def _lru_seq(z, conv_w, conv_b, w_a, b_a, w_i, b_i, lam, g_merge, tt=64):
    nseq, t_len, _ = z.shape
    d = conv_w.shape[1]
    row = lambda v: v.reshape(1, d)
    const2 = lambda t: (0, 0)
    const3 = lambda t: (0, 0, 0)
    kern = functools.partial(_lru_seq_kernel, tt=tt)
    return pl.pallas_call(
        kern,
        out_shape=(jax.ShapeDtypeStruct((nseq, t_len, d), BF16),
                   jax.ShapeDtypeStruct((nseq, 1, d), F32)),
        grid=(t_len // tt,),
        in_specs=[
            pl.BlockSpec((nseq, tt, d), lambda t: (0, t, 1)),
            pl.BlockSpec((nseq, tt, d), lambda t: (0, t, 2)),
            pl.BlockSpec((CONV_W, d), const2),
            pl.BlockSpec((1, d), const2),
            pl.BlockSpec(w_a.shape, const3),
            pl.BlockSpec((1, d), const2),
            pl.BlockSpec(w_i.shape, const3),
            pl.BlockSpec((1, d), const2),
            pl.BlockSpec((1, d), const2),
            pl.BlockSpec((1, d), const2),
        ],
        out_specs=(pl.BlockSpec((nseq, tt, d), lambda t: (0, t, 0)),
                   pl.BlockSpec((nseq, 1, d), const3)),
        scratch_shapes=[pltpu.VMEM((nseq, tt + SUBLANES, d), F32),
                        pltpu.VMEM((nseq, tt, d), F32),
                        pltpu.VMEM((nseq, tt, d), F32),
                        pltpu.VMEM((nseq, 1, d), F32)],
        compiler_params=_params("arbitrary"),
        name="rglru_seq",
    )(z, z, conv_w, row(conv_b), w_a, row(b_a), w_i, row(b_i), row(lam), row(g_merge))


def _lru_step_kernel(xb_ref, gb_ref, c0_ref, c1_ref, c2_ref, h0_ref, cw_ref, cb_ref,
                     wa_ref, ba_ref, wi_ref, bi_ref, lam_ref, gm_ref, o_ref, h_ref):
    xb = xb_ref[...]
    cw = cw_ref[...]
    xc = (cb_ref[...] + c0_ref[...] * cw[0:1, :] + c1_ref[...] * cw[1:2, :]
          + c2_ref[...] * cw[2:3, :] + xb * cw[3:4, :])
    a, b = _lru_gates(xc, wa_ref, ba_ref[...], wi_ref, bi_ref[...], lam_ref[...])
    h = a * h0_ref[...] + b
    h_ref[...] = h
    out = h * jax.nn.gelu(gb_ref[...])
    o_ref[...] = _rms(out, gm_ref[...]).astype(BF16)


def _lru_step(z, conv0, h0, conv_w, conv_b, w_a, b_a, w_i, b_i, lam, g_merge):
    n = z.shape[0]
    d = conv_w.shape[1]
    row = lambda v: v.reshape(1, d)
    full = lambda shape: pl.BlockSpec(shape, lambda i: (0,) * len(shape))
    nd = full((n, d))
    rd = full((1, d))
    return pl.pallas_call(
        _lru_step_kernel,
        out_shape=(jax.ShapeDtypeStruct((n, d), BF16), jax.ShapeDtypeStruct((n, d), F32)),
        grid=(1,),
        in_specs=[
            pl.BlockSpec((n, d), lambda i: (0, 1)),
            pl.BlockSpec((n, d), lambda i: (0, 2)),
            nd, nd, nd, nd,
            full((CONV_W, d)), rd,
            full(w_a.shape), rd, full(w_i.shape), rd, rd, rd,
        ],
        out_specs=(nd, nd),
        compiler_params=_params("arbitrary"),
        name="rglru_step",
    )(z, z, conv0[:, 0], conv0[:, 1], conv0[:, 2], h0, conv_w, row(conv_b),
      w_a, row(b_a), w_i, row(b_i), row(lam), row(g_merge))


def _glu_kernel(y_ref, w_ref, b_ref, g_ref, o_ref):
    y = y_ref[...]
    gate = jax.nn.sigmoid(_dot(y.astype(BF16), w_ref[...]) + b_ref[...])
    o_ref[...] = _rms(y.astype(F32) * gate, g_ref[...]).astype(BF16)


def _glu(y, w, b, g, tm):
    m, d = y.shape
    return pl.pallas_call(
        _glu_kernel,
        out_shape=jax.ShapeDtypeStruct((m, d), BF16),
        grid=(m // tm,),
        in_specs=[
            pl.BlockSpec((tm, d), lambda i: (i, 0)),
            pl.BlockSpec((d, d), lambda i: (0, 0)),
            pl.BlockSpec((1, d), lambda i: (0, 0)),
            pl.BlockSpec((1, d), lambda i: (0, 0)),
        ],
        out_specs=pl.BlockSpec((tm, d), lambda i: (i, 0)),
        compiler_params=_params("parallel"),
        name="s5_glu",
    )(y, w, b.reshape(1, d), g.reshape(1, d))


def _out_proj_kernel(x_ref, ma_ref, mb_ref, wa_ref, wb_ref, o_ref):
    o_ref[...] = x_ref[...] + _dot(ma_ref[...], wa_ref[...]) + _dot(mb_ref[...], wb_ref[...])


def _out_proj(x, ma, mb, w, tm, tn):
    m, n = x.shape
    ka = ma.shape[1]
    kb = mb.shape[1]
    nj = n // tn
    return pl.pallas_call(
        _out_proj_kernel,
        out_shape=jax.ShapeDtypeStruct((m, n), F32),
        grid=(m // tm, nj),
        in_specs=[
            pl.BlockSpec((tm, tn), lambda i, j: (i, j)),
            pl.BlockSpec((tm, ka), lambda i, j: (i, 0)),
            pl.BlockSpec((tm, kb), lambda i, j: (i, 0)),
            pl.BlockSpec((ka, tn), lambda i, j: (0, j)),
            pl.BlockSpec((kb, tn), lambda i, j: (1, j)),
        ],
        out_specs=pl.BlockSpec((tm, tn), lambda i, j: (i, j)),
        compiler_params=_params("parallel", "arbitrary"),
        name="out_proj",
    )(x, ma, mb, w, w)


def _mlp_kernel(x_ref, g_ref, wu_ref, wd_ref, o_ref, h_ref):
    @pl.when(pl.program_id(1) == 0)
    def _():
        x = x_ref[...]
        h_ref[...] = _rms(x, g_ref[...]).astype(BF16)
        o_ref[...] = x

    up = _dot(h_ref[...], wu_ref[...])
    act = jnp.square(jnp.maximum(up, 0.0)).astype(BF16)
    o_ref[...] += _dot(act, wd_ref[...])


def _mlp(x, g, w_up, w_down, tm, tf):
    m, d = x.shape
    f = w_up.shape[1]
    return pl.pallas_call(
        _mlp_kernel,
        out_shape=jax.ShapeDtypeStruct((m, d), F32),
        grid=(m // tm, f // tf),
        in_specs=[
            pl.BlockSpec((tm, d), lambda i, j: (i, 0)),
            pl.BlockSpec((1, d), lambda i, j: (0, 0)),
            pl.BlockSpec((d, tf), lambda i, j: (0, j)),
            pl.BlockSpec((tf, d), lambda i, j: (j, 0)),
        ],
        out_specs=pl.BlockSpec((tm, d), lambda i, j: (i, 0)),
        scratch_shapes=[pltpu.VMEM((tm, d), BF16)],
        compiler_params=_params("parallel", "arbitrary"),
        name="mlp",
    )(x, g.reshape(1, d), w_up, w_down)


def _ple_kernel(x_ref, p_ref, g_ref, wg_ref, wp_ref, gf_ref, o_ref, *, final):
    x = x_ref[...]
    gate = jax.nn.sigmoid(_dot(_rms(x, g_ref[...]).astype(BF16), wg_ref[...]))
    pe = _dot(p_ref[...].astype(BF16), wp_ref[...])
    x = x + pe * gate
    o_ref[...] = _rms(x, gf_ref[...]) if final else x


def _ple(x, p, g, w_gate, w_ple, g_final, tm, final):
    m, d = x.shape
    dp = p.shape[1]
    return pl.pallas_call(
        functools.partial(_ple_kernel, final=final),
        out_shape=jax.ShapeDtypeStruct((m, d), F32),
        grid=(m // tm,),
        in_specs=[
            pl.BlockSpec((tm, d), lambda i: (i, 0)),
            pl.BlockSpec((tm, dp), lambda i: (i, 0)),
            pl.BlockSpec((1, d), lambda i: (0, 0)),
            pl.BlockSpec((d, d), lambda i: (0, 0)),
            pl.BlockSpec((dp, d), lambda i: (0, 0)),
            pl.BlockSpec((1, d), lambda i: (0, 0)),
        ],
        out_specs=pl.BlockSpec((tm, d), lambda i: (i, 0)),
        compiler_params=_params("parallel"),
        name="ple_final",
    )(x, p, g.reshape(1, d), w_gate, w_ple, g_final.reshape(1, d))


def _tail(x, ma, mb, p, w, g_final, final, tm, tm_mlp, tm_ple):
    x = _out_proj(x, ma, mb, w['w_out'], tm, 512)
    x = _mlp(x, w['g_mlp'], w['w_up'], w['w_down'], tm_mlp, 1024)
    return _ple(x, p, w['g_ple'], w['w_ple_gate'], w['w_ple'], g_final, tm_ple, final)


def kernel(x_prompt, x_sample, state_s5_re, state_s5_im, state_lru, state_conv, p_prompt, p_sample,
           g_mix, w_in, s5_lam_re, s5_lam_im, s5_log_step, s5_b_re, s5_b_im, s5_c_re, s5_c_im, s5_d,
           s5_w_glu, s5_b_glu, conv_w, conv_b, lru_w_a, lru_b_a, lru_w_i, lru_b_i, lru_lam,
           g_merge_a, g_merge_b, w_out, g_mlp, w_up, w_down, g_ple, w_ple_gate, w_ple, g_final):
    depth = g_mix.shape[0]
    nb, t_len, d_model = x_prompt.shape
    ns = x_sample.shape[0]
    d_s5 = s5_d.shape[1]
    d_lru = conv_w.shape[2]
    groups = d_s5 // S5_CH

    xp = x_prompt.reshape(nb * t_len, d_model)
    xs = x_sample.reshape(ns, d_model)
    outs = [[] for _ in range(8)]
    for l in range(depth):
        final = l == depth - 1
        w = dict(w_out=w_out[l].astype(BF16), g_mlp=g_mlp[l], w_up=w_up[l].astype(BF16),
                 w_down=w_down[l].astype(BF16), g_ple=g_ple[l],
                 w_ple_gate=w_ple_gate[l].astype(BF16), w_ple=w_ple[l].astype(BF16))
        w_in_b = w_in[l].astype(BF16)
        w_glu_b = s5_w_glu[l].astype(BF16)
        lru_args = (conv_w[l], conv_b[l], lru_w_a[l].astype(BF16), lru_b_a[l],
                    lru_w_i[l].astype(BF16), lru_b_i[l], lru_lam[l], g_merge_b[l])
        w_t, w_s, w_c, tab, wb, wct, arow = _s5_params(
            s5_lam_re[l], s5_lam_im[l], s5_log_step[l], s5_b_re[l], s5_b_im[l],
            s5_c_re[l], s5_c_im[l], s5_d[l])

        z = _norm_matmul(xp, g_mix[l], w_in_b, 1024, 1024)
        z4 = z.reshape(nb, t_len, 3 * d_s5)
        y, hf = _s5_seq(z4, w_t, w_s, w_c, tab, d_s5)
        ma = _glu(y.reshape(nb * t_len, d_s5), w_glu_b, s5_b_glu[l], g_merge_a[l], 1024)
        mb, lru_h = _lru_seq(z4, *lru_args)
        xp = _tail(xp, ma, mb.reshape(nb * t_len, d_lru), p_prompt[l].reshape(nb * t_len, -1), w,
                   g_final, final, 1024, 512, 256)
        hf = jnp.transpose(hf[:, :, :, :nb], (1, 3, 0, 2))
        outs[0].append(hf[0])
        outs[1].append(hf[1])
        outs[2].append(lru_h.reshape(nb, d_lru))
        outs[3].append(z4[:, t_len - (CONV_W - 1):, d_s5:d_s5 + d_lru])

        zs = _norm_matmul(xs, g_mix[l], w_in_b, ns, 512)
        ys, hsr, hsi = _s5_step(zs, state_s5_re[l].reshape(ns, groups * S5_STATE),
                                state_s5_im[l].reshape(ns, groups * S5_STATE),
                                wb, wct, arow, s5_d[l].reshape(1, d_s5))
        mas = _glu(ys, w_glu_b, s5_b_glu[l], g_merge_a[l], ns)
        mbs, lru_hs = _lru_step(zs, state_conv[l], state_lru[l], *lru_args)
        xs = _tail(xs, mas, mbs, p_sample[l].reshape(ns, -1), w, g_final, final, ns, ns, ns)
        outs[4].append(hsr.reshape(ns, groups, S5_STATE))
        outs[5].append(hsi.reshape(ns, groups, S5_STATE))
        outs[6].append(lru_hs)
        outs[7].append(jnp.concatenate(
            [state_conv[l][:, 1:], zs[:, None, d_s5:d_s5 + d_lru]], axis=1))
    return (xp.reshape(nb, t_len, d_model), xs.reshape(ns, 1, d_model),
            *(jnp.stack(o) for o in outs))
```

```python
import functools

import jax
import jax.numpy as jnp
from jax import lax
from jax.experimental import pallas as pl
from jax.experimental.pallas import tpu as pltpu

F32 = jnp.float32
BF16 = jnp.bfloat16
HIGHEST = lax.Precision.HIGHEST

EPS = 1e-6
LRU_C = 8.0
S5_CH = 16
S5_STATE = 64
LRU_HEADS = 4
CONV_W = 4
CHUNK = 16
SUBLANES = 8
LANES = 128
LANE_GROUPS = LANES // S5_CH
LANE_PAIRS = LANE_GROUPS // 2
ROWS = CHUNK * S5_CH
POW_ROWS = 32
SCAN_ROW0 = POW_ROWS
VMEM_LIMIT = 56 * 1024 * 1024


def _params(*sem):
    return pltpu.CompilerParams(dimension_semantics=sem, vmem_limit_bytes=VMEM_LIMIT)


def _rms(x, g):
    return x * lax.rsqrt(jnp.mean(x * x, axis=-1, keepdims=True) + EPS) * g


def _dot(a, b):
    return jnp.dot(a, b, preferred_element_type=F32)


def _dot_nt(a, b, precision=None):
    return lax.dot_general(a, b, (((1,), (1,)), ((), ())), precision=precision,
                           preferred_element_type=F32)


def _transpose_tiles(x):
    r, c = x.shape
    return jnp.concatenate(
        [jnp.concatenate([x[i:i + LANES, j:j + LANES].T for i in range(0, r, LANES)], axis=1)
         for j in range(0, c, LANES)], axis=0)


def _row_parts(tm, want):
    parts = want if tm % (want * 16) == 0 else 1
    step = tm // parts
    return [slice(r * step, (r + 1) * step) for r in range(parts)]


def _norm_matmul_kernel(x_ref, g_ref, w_ref, o_ref):
    for rows in _row_parts(x_ref.shape[0], 4):
        h = _rms(x_ref[rows, :], g_ref[...]).astype(BF16)
        o_ref[rows, :] = _dot(h, w_ref[...])


def _norm_matmul(x, g, w, tm, tn):
    m, k = x.shape
    n = w.shape[1]
    return pl.pallas_call(
        _norm_matmul_kernel,
        out_shape=jax.ShapeDtypeStruct((m, n), F32),
        grid=(m // tm, n // tn),
        in_specs=[
            pl.BlockSpec((tm, k), lambda i, j: (i, 0)),
            pl.BlockSpec((1, k), lambda i, j: (0, 0)),
            pl.BlockSpec((k, tn), lambda i, j: (0, j)),
        ],
        out_specs=pl.BlockSpec((tm, tn), lambda i, j: (i, j)),
        compiler_params=_params("parallel", "parallel"),
        name="norm_w_in",
    )(x, g.reshape(1, k), w)


def _s5_param_kernel(lr_ref, li_ref, ls_ref, bt_r_ref, bt_i_ref, c_r_ref, c_i_ref, d_ref,
                     wt_ref, ws_ref, wc_ref, tab_ref, wb_ref, wct_ref, arow_ref):
    row_p = lax.broadcasted_iota(jnp.int32, (POW_ROWS, LANES), 0)
    lane = lax.broadcasted_iota(jnp.int32, (ROWS, LANES), 1)
    lane16 = lax.broadcasted_iota(jnp.int32, (S5_CH, LANES), 1)
    row16 = lax.broadcasted_iota(jnp.int32, (S5_CH, LANES), 0)
    lo = lane < S5_STATE

    wb_ref[...] = jnp.zeros_like(wb_ref)
    wct_ref[...] = jnp.zeros_like(wct_ref)

    for q in range(LANE_PAIRS):
        lr, li = lr_ref[q], li_ref[q]
        step = jnp.exp(ls_ref[q])
        mag = jnp.exp(lr * step)
        ar = mag * jnp.cos(li * step)
        ai = mag * jnp.sin(li * step)
        nr, ni = ar - 1.0, ai
        den = lr * lr + li * li
        cr = (nr * lr + ni * li) / den
        ci = (ni * lr - nr * li) / den
        bt_r, bt_i = bt_r_ref[q], bt_i_ref[q]
        bb_r = cr * bt_r - ci * bt_i
        bb_i = cr * bt_i + ci * bt_r
        c_r, c_i = c_r_ref[q], c_i_ref[q]

        pr = jnp.ones((POW_ROWS, LANES), F32)
        pi = jnp.zeros((POW_ROWS, LANES), F32)
        sr, si = ar, ai
        sq = []
        for m in range(POW_ROWS.bit_length() - 1 + SUBLANES):
            sq.append((sr, si))
            if (1 << m) < POW_ROWS:
                bit = ((row_p >> m) & 1) == 1
                pr, pi = (jnp.where(bit, pr * sr - pi * si, pr),
                          jnp.where(bit, pr * si + pi * sr, pi))
            sr, si = sr * sr - si * si, 2.0 * sr * si
        m0 = CHUNK.bit_length() - 1
        scan_r = jnp.concatenate([sq[m0 + j][0] for j in range(SUBLANES)], axis=0)
        scan_i = jnp.concatenate([sq[m0 + j][1] for j in range(SUBLANES)], axis=0)
        pad = jnp.zeros((LANES - POW_ROWS - SUBLANES, LANES), F32)
        tab_ref[q, 0] = jnp.concatenate([pr, scan_r, pad], axis=0).T
        tab_ref[q, 1] = jnp.concatenate([pi, scan_i, pad], axis=0).T
        arow_ref[q, 0] = pr[:SUBLANES]
        arow_ref[q, 1] = pi[:SUBLANES]

        def expand(tab, k0, sign):
            return jnp.concatenate(
                [jnp.broadcast_to(tab[k0 + sign * s:k0 + sign * s + 1, :], (S5_CH, LANES))
                 for s in range(CHUNK)], axis=0)

        tile = lambda v: jnp.concatenate([v] * CHUNK, axis=0)

        e_r, e_i = expand(pr, CHUNK - 1, -1), expand(pi, CHUNK - 1, -1)
        t_r, t_i = tile(bb_r), tile(bb_i)
        ws_r = _transpose_tiles(t_r * e_r - t_i * e_i)
        ws_i = _transpose_tiles(t_r * e_i + t_i * e_r)
        for h in range(2):
            rows = slice(h * S5_STATE, (h + 1) * S5_STATE)
            ws_ref[2 * q + h] = jnp.concatenate([ws_r[rows], ws_i[rows]], axis=0).astype(BF16)

        e_r, e_i = expand(pr, 1, 1), expand(pi, 1, 1)
        t_r, t_i = tile(c_r), tile(c_i)
        ca_r = t_r * e_r - t_i * e_i
        ca_n = -(t_r * e_i + t_i * e_r)
        wc_ref[2 * q] = jnp.where(lo, ca_r, pltpu.roll(ca_n, S5_STATE, 1)).astype(BF16)
        wc_ref[2 * q + 1] = jnp.where(lo, pltpu.roll(ca_r, S5_STATE, 1), ca_n).astype(BF16)

        e_r, e_i = expand(pr, 0, 1), expand(pi, 0, 1)
        c0_r = t_r * e_r - t_i * e_i
        c0_i = t_r * e_i + t_i * e_r

        for h in range(2):
            g = 2 * q + h
            mine = (lane16 < S5_STATE) == (h == 0)
            bm_r = jnp.where(mine, bb_r, 0.0)
            bm_i = jnp.where(mine, bb_i, 0.0)
            kt = _dot_nt(bm_r, c0_r, HIGHEST) - _dot_nt(bm_i, c0_i, HIGHEST)
            k_lo = kt[:, :LANES] + jnp.where(row16 == lane16, d_ref[g], 0.0)
            k_hi = kt[:, LANES:]
            blocks = []
            half = LANES // S5_CH
            for s in range(CHUNK):
                sh = (s % half) * S5_CH
                keep = lane16 >= sh
                r_lo = pltpu.roll(k_lo, sh, 1) if sh else k_lo
                r_hi = pltpu.roll(k_hi, sh, 1) if sh else k_hi
                if s < half:
                    blk = jnp.concatenate([jnp.where(keep, r_lo, 0.0),
                                           jnp.where(keep, r_hi, r_lo)], axis=1)
                else:
                    blk = jnp.concatenate([jnp.zeros_like(r_lo), jnp.where(keep, r_lo, 0.0)], axis=1)
                blocks.append(blk)
            wt_ref[g] = _transpose_tiles(jnp.concatenate(blocks, axis=0)).astype(BF16)

            r0 = (2 * q + h) * S5_CH
            c0 = q * 2 * LANES
            wb_ref[r0:r0 + S5_CH, c0:c0 + LANES] = bm_r.astype(BF16)
            wb_ref[r0:r0 + S5_CH, c0 + LANES:c0 + 2 * LANES] = bm_i.astype(BF16)
            wct_ref[q, r0:r0 + S5_CH, :LANES] = jnp.where(mine, c_r, 0.0).astype(BF16)
            wct_ref[q, r0:r0 + S5_CH, LANES:] = jnp.where(mine, -c_i, 0.0).astype(BF16)


def _s5_params(lam_re, lam_im, log_step, b_re, b_im, c_re, c_im, d):
    groups = lam_re.shape[0]
    pairs = groups // 2
    slabs = groups // LANE_GROUPS
    pair_row = lambda v: v.reshape(pairs, 1, LANES)
    pair_ch = lambda v: jnp.transpose(v.reshape(pairs, 2, S5_CH, S5_STATE), (0, 2, 1, 3)).reshape(
        pairs, S5_CH, LANES)
    ls = jnp.broadcast_to(log_step[:, None], (groups, S5_STATE))
    d_pad = jnp.pad(d.reshape(groups, 1, S5_CH), ((0, 0), (0, 0), (0, LANES - S5_CH)))
    blk3 = lambda i: (i, 0, 0)
    blk4 = lambda i: (i, 0, 0, 0)
    prow = pl.BlockSpec((LANE_PAIRS, 1, LANES), blk3)
    pch = pl.BlockSpec((LANE_PAIRS, S5_CH, LANES), blk3)
    return pl.pallas_call(
        _s5_param_kernel,
        out_shape=(jax.ShapeDtypeStruct((groups, ROWS, ROWS), BF16),
                   jax.ShapeDtypeStruct((groups, 2 * S5_STATE, ROWS), BF16),
                   jax.ShapeDtypeStruct((groups, ROWS, 2 * S5_STATE), BF16),
                   jax.ShapeDtypeStruct((pairs, 2, LANES, LANES), F32),
                   jax.ShapeDtypeStruct((slabs, LANES, LANE_PAIRS * 2 * LANES), BF16),
                   jax.ShapeDtypeStruct((slabs, LANE_PAIRS, LANES, 2 * LANES), BF16),
                   jax.ShapeDtypeStruct((pairs, 2, SUBLANES, LANES), F32)),
        grid=(slabs,),
        in_specs=[prow, prow, prow, pch, pch, pch, pch,
                  pl.BlockSpec((LANE_GROUPS, 1, LANES), blk3)],
        out_specs=(pl.BlockSpec((LANE_GROUPS, ROWS, ROWS), blk3),
                   pl.BlockSpec((LANE_GROUPS, 2 * S5_STATE, ROWS), blk3),
                   pl.BlockSpec((LANE_GROUPS, ROWS, 2 * S5_STATE), blk3),
                   pl.BlockSpec((LANE_PAIRS, 2, LANES, LANES), blk4),
                   pl.BlockSpec((None, LANES, LANE_PAIRS * 2 * LANES), blk3),
                   pl.BlockSpec((None, LANE_PAIRS, LANES, 2 * LANES), blk4),
                   pl.BlockSpec((LANE_PAIRS, 2, SUBLANES, LANES), blk4)),
        compiler_params=_params("parallel"),
        name="s5_params",
    )(pair_row(lam_re), pair_row(lam_im), pair_row(ls),
      pair_ch(jnp.transpose(b_re, (0, 2, 1))), pair_ch(jnp.transpose(b_im, (0, 2, 1))),
      pair_ch(c_re), pair_ch(c_im), d_pad)


def _s5_seq_kernel(z_ref, wt_ref, ws_ref, wc_ref, tab_ref, y_ref, hf_ref, ut_ref, yt_ref,
                   sr_ref, si_ref, *, nb, n_chunks):
    scan_steps = n_chunks.bit_length() - 1
    for n in range(nb):
        for s in range(CHUNK):
            xs = z_ref[n, pl.ds(s, n_chunks, stride=CHUNK), :]
            ut_ref[:, s * S5_CH:(s + 1) * S5_CH, n * n_chunks:(n + 1) * n_chunks] = (
                xs.T.astype(BF16).reshape(LANE_GROUPS, S5_CH, n_chunks))

    for g in range(LANE_GROUPS):
        st = _dot(ws_ref[g], ut_ref[g])
        sr_ref[g * S5_STATE:(g + 1) * S5_STATE, :] = st[:S5_STATE]
        si_ref[g * S5_STATE:(g + 1) * S5_STATE, :] = st[S5_STATE:]

    srows = LANE_GROUPS * S5_STATE
    lane = lax.broadcasted_iota(jnp.int32, (srows, n_chunks), 1)
    tab_r = jnp.concatenate([tab_ref[q, 0] for q in range(LANE_PAIRS)], axis=0)
    tab_i = jnp.concatenate([tab_ref[q, 1] for q in range(LANE_PAIRS)], axis=0)
    fin_r = jnp.zeros((srows, n_chunks), F32)
    fin_i = jnp.zeros((srows, n_chunks), F32)
    for n in range(nb):
        cols = slice(n * n_chunks, (n + 1) * n_chunks)
        xr = sr_ref[:, cols]
        xi = si_ref[:, cols]
        for k in range(scan_steps):
            d = 1 << k
            keep = lane >= d
            sr = jnp.where(keep, pltpu.roll(xr, d, 1), 0.0)
            si = jnp.where(keep, pltpu.roll(xi, d, 1), 0.0)
            pr = tab_r[:, SCAN_ROW0 + k:SCAN_ROW0 + k + 1]
            pi = tab_i[:, SCAN_ROW0 + k:SCAN_ROW0 + k + 1]
            xr, xi = xr + (pr * sr - pi * si), xi + (pr * si + pi * sr)
        last = lane == n_chunks - 1
        fin_r = fin_r + jnp.where(
            lane == n, jnp.sum(jnp.where(last, xr, 0.0), axis=1, keepdims=True), 0.0)
        fin_i = fin_i + jnp.where(
            lane == n, jnp.sum(jnp.where(last, xi, 0.0), axis=1, keepdims=True), 0.0)
        keep = lane >= 1
        sr_ref[:, cols] = jnp.where(keep, pltpu.roll(xr, 1, 1), 0.0)
        si_ref[:, cols] = jnp.where(keep, pltpu.roll(xi, 1, 1), 0.0)

    for g in range(LANE_GROUPS):
        rows = slice(g * S5_STATE, (g + 1) * S5_STATE)
        hf_ref[g, 0] = fin_r[rows]
        hf_ref[g, 1] = fin_i[rows]
        hs = jnp.concatenate([sr_ref[rows, :], si_ref[rows, :]], axis=0).astype(BF16)
        yt_ref[g] = jax.nn.gelu(_dot(wc_ref[g], hs) + _dot(wt_ref[g], ut_ref[g]))

    for n in range(nb):
        for s in range(CHUNK):
            blk = yt_ref[:, s * S5_CH:(s + 1) * S5_CH, n * n_chunks:(n + 1) * n_chunks]
            y_ref[n, pl.ds(s, n_chunks, stride=CHUNK), :] = blk.reshape(LANES, n_chunks).T


def _s5_seq(z, w_t, w_s, w_c, tab, d_s5):
    nb, t_len, _ = z.shape
    n_chunks = t_len // CHUNK
    assert n_chunks == LANES, "the chunk axis must fill one 128-lane tile"
    groups = d_s5 // S5_CH
    kern = functools.partial(_s5_seq_kernel, nb=nb, n_chunks=n_chunks)
    blk3 = lambda i: (i, 0, 0)
    blk4 = lambda i: (i, 0, 0, 0)
    return pl.pallas_call(
        kern,
        out_shape=(jax.ShapeDtypeStruct((nb, t_len, d_s5), F32),
                   jax.ShapeDtypeStruct((groups, 2, S5_STATE, LANES), F32)),
        grid=(groups // LANE_GROUPS,),
        in_specs=[
            pl.BlockSpec((nb, t_len, LANES), lambda i: (0, 0, i)),
            pl.BlockSpec((LANE_GROUPS, ROWS, ROWS), blk3),
            pl.BlockSpec((LANE_GROUPS, 2 * S5_STATE, ROWS), blk3),
            pl.BlockSpec((LANE_GROUPS, ROWS, 2 * S5_STATE), blk3),
            pl.BlockSpec((LANE_PAIRS, 2, LANES, LANES), blk4),
        ],
        out_specs=(pl.BlockSpec((nb, t_len, LANES), lambda i: (0, 0, i)),
                   pl.BlockSpec((LANE_GROUPS, 2, S5_STATE, LANES), blk4)),
        scratch_shapes=[pltpu.VMEM((LANE_GROUPS, ROWS, nb * n_chunks), BF16),
                        pltpu.VMEM((LANE_GROUPS, ROWS, nb * n_chunks), F32),
                        pltpu.VMEM((LANE_GROUPS * S5_STATE, nb * n_chunks), F32),
                        pltpu.VMEM((LANE_GROUPS * S5_STATE, nb * n_chunks), F32)],
        compiler_params=_params("parallel"),
        name="s5_seq",
    )(z, w_t, w_s, w_c, tab)


def _s5_step_kernel(u_ref, h0r_ref, h0i_ref, wb_ref, wct_ref, arow_ref, d_ref,
                    y_ref, hr_ref, hi_ref):
    u = u_ref[...]
    bu = _dot(u.astype(BF16), wb_ref[...])
    y = d_ref[...] * u
    for q in range(LANE_PAIRS):
        cols = slice(q * LANES, (q + 1) * LANES)
        ar = arow_ref[q, 0, 1:2, :]
        ai = arow_ref[q, 1, 1:2, :]
        h0r, h0i = h0r_ref[:, cols], h0i_ref[:, cols]
        hr = ar * h0r - ai * h0i + bu[:, 2 * q * LANES:(2 * q + 1) * LANES]
        hi = ar * h0i + ai * h0r + bu[:, (2 * q + 1) * LANES:(2 * q + 2) * LANES]
        hr_ref[:, cols] = hr
        hi_ref[:, cols] = hi
        y = y + _dot_nt(jnp.concatenate([hr, hi], axis=1).astype(BF16), wct_ref[q])
    y_ref[...] = jax.nn.gelu(y).astype(BF16)


def _s5_step(z, h0r, h0i, wb, wct, arow, d):
    n = z.shape[0]
    slabs = wb.shape[0]
    sw = LANE_PAIRS * LANES
    col = lambda i: (0, i)
    blk3 = lambda i: (i, 0, 0)
    blk4 = lambda i: (i, 0, 0, 0)
    return pl.pallas_call(
        _s5_step_kernel,
        out_shape=(jax.ShapeDtypeStruct((n, slabs * LANES), BF16),
                   jax.ShapeDtypeStruct((n, slabs * sw), F32),
                   jax.ShapeDtypeStruct((n, slabs * sw), F32)),
        grid=(slabs,),
        in_specs=[
            pl.BlockSpec((n, LANES), col),
            pl.BlockSpec((n, sw), col),
            pl.BlockSpec((n, sw), col),
            pl.BlockSpec((None, LANES, 2 * sw), blk3),
            pl.BlockSpec((None, LANE_PAIRS, LANES, 2 * LANES), blk4),
            pl.BlockSpec((LANE_PAIRS, 2, SUBLANES, LANES), blk4),
            pl.BlockSpec((1, LANES), col),
        ],
        out_specs=(pl.BlockSpec((n, LANES), col),
                   pl.BlockSpec((n, sw), col),
                   pl.BlockSpec((n, sw), col)),
        compiler_params=_params("parallel"),
        name="s5_step",
    )(z, h0r, h0i, wb, wct, arow, d)


def _lru_gate_block(xc, wa, ba, wi, bi, lam):
    xb16 = xc.astype(BF16)
    r = jax.nn.sigmoid(_dot(xb16, wa) + ba)
    ig = jax.nn.sigmoid(_dot(xb16, wi) + bi)
    log_a = -LRU_C * r * jax.nn.softplus(-lam)
    a = jnp.exp(log_a)
    mult = jnp.sqrt(1.0 - a * a)
    return a, mult * (ig * xc)


def _lru_gates(xc, wa_ref, ba, wi_ref, bi, lam):
    blk = xc.shape[1] // LRU_HEADS
    parts = [_lru_gate_block(xc[:, h * blk:(h + 1) * blk], wa_ref[h], ba[:, h * blk:(h + 1) * blk],
                             wi_ref[h], bi[:, h * blk:(h + 1) * blk],
                             lam[:, h * blk:(h + 1) * blk]) for h in range(LRU_HEADS)]
    return (jnp.concatenate([p[0] for p in parts], axis=-1),
            jnp.concatenate([p[1] for p in parts], axis=-1))


def _lru_seq_kernel(xb_ref, gb_ref, cw_ref, cb_ref, wa_ref, ba_ref, wi_ref, bi_ref, lam_ref,
                    gm_ref, o_ref, hl_ref, xe_ref, a_ref, b_ref, hc_ref, *, tt):
    halo = SUBLANES
    nseq, _, d = xb_ref.shape
    blk = d // LRU_HEADS

    @pl.when(pl.program_id(0) == 0)
    def _():
        xe_ref[:, 0:halo, :] = jnp.zeros((nseq, halo, d), F32)
        hc_ref[...] = jnp.zeros_like(hc_ref)

    xe_ref[:, halo:halo + tt, :] = xb_ref[...]
    for h in range(LRU_HEADS):
        cols = slice(h * blk, (h + 1) * blk)
        cw = cw_ref[:, cols]
        xc = cb_ref[:, cols] + xe_ref[:, halo:halo + tt, cols] * cw[CONV_W - 1:CONV_W, :]
        for k in range(1, CONV_W):
            xc = xc + xe_ref[:, halo - k:halo - k + tt, cols] * cw[CONV_W - 1 - k:CONV_W - k, :]
        a, b = _lru_gate_block(xc.reshape(nseq * tt, blk), wa_ref[h], ba_ref[:, cols],
                               wi_ref[h], bi_ref[:, cols], lam_ref[:, cols])
        a_ref[:, :, cols] = a.reshape(nseq, tt, blk)
        b_ref[:, :, cols] = b.reshape(nseq, tt, blk)
    xe_ref[:, 0:halo, :] = xb_ref[:, tt - halo:tt, :]

    def block(i, h):
        base = pl.multiple_of(i * SUBLANES, SUBLANES)
        for j in range(SUBLANES):
            row = pl.ds(base + j, 1)
            h = a_ref[:, row, :] * h + b_ref[:, row, :]
            b_ref[:, row, :] = h
        return h

    h = lax.fori_loop(0, tt // SUBLANES, block, hc_ref[...])
    hc_ref[...] = h
    hl_ref[...] = h

    out = b_ref[...] * jax.nn.gelu(gb_ref[...])
    o_ref[...] = _rms(out, gm_ref[...]).astype(BF16)


def _lru_seq(z, conv_w, conv_b, w_a, b_a, w_i, b_i, lam, g_merge, tt=64):
    nseq, t_len, _ = z.shape
    d = conv_w.shape[1]
    row = lambda v: v.reshape(1, d)
    const2 = lambda t: (0, 0)
    const3 = lambda t: (0, 0, 0)
    kern = functools.partial(_lru_seq_kernel, tt=tt)
    return pl.pallas_call(
        kern,
        out_shape=(jax.ShapeDtypeStruct((nseq, t_len, d), BF16),
                   jax.ShapeDtypeStruct((nseq, 1, d), F32)),
        grid=(t_len // tt,),
        in_specs=[
            pl.BlockSpec((nseq, tt, d), lambda t: (0, t, 1)),
            pl.BlockSpec((nseq, tt, d), lambda t: (0, t, 2)),
            pl.BlockSpec((CONV_W, d), const2),
            pl.BlockSpec((1, d), const2),
            pl.BlockSpec(w_a.shape, const3),
            pl.BlockSpec((1, d), const2),
            pl.BlockSpec(w_i.shape, const3),
            pl.BlockSpec((1, d), const2),
            pl.BlockSpec((1, d), const2),
            pl.BlockSpec((1, d), const2),
        ],
        out_specs=(pl.BlockSpec((nseq, tt, d), lambda t: (0, t, 0)),
                   pl.BlockSpec((nseq, 1, d), const3)),
        scratch_shapes=[pltpu.VMEM((nseq, tt + SUBLANES, d), F32),
                        pltpu.VMEM((nseq, tt, d), F32),
                        pltpu.VMEM((nseq, tt, d), F32),
                        pltpu.VMEM((nseq, 1, d), F32)],
        compiler_params=_params("arbitrary"),
        name="rglru_seq",
    )(z, z, conv_w, row(conv_b), w_a, row(b_a), w_i, row(b_i), row(lam), row(g_merge))


def _lru_step_kernel(xb_ref, gb_ref, c0_ref, c1_ref, c2_ref, h0_ref, cw_ref, cb_ref,
                     wa_ref, ba_ref, wi_ref, bi_ref, lam_ref, gm_ref, o_ref, h_ref):
    xb = xb_ref[...]
    cw = cw_ref[...]
    xc = (cb_ref[...] + c0_ref[...] * cw[0:1, :] + c1_ref[...] * cw[1:2, :]
          + c2_ref[...] * cw[2:3, :] + xb * cw[3:4, :])
    a, b = _lru_gates(xc, wa_ref, ba_ref[...], wi_ref, bi_ref[...], lam_ref[...])
    h = a * h0_ref[...] + b
    h_ref[...] = h
    out = h * jax.nn.gelu(gb_ref[...])
    o_ref[...] = _rms(out, gm_ref[...]).astype(BF16)


def _lru_step(z, conv0, h0, conv_w, conv_b, w_a, b_a, w_i, b_i, lam, g_merge):
    n = z.shape[0]
    d = conv_w.shape[1]
    row = lambda v: v.reshape(1, d)
    full = lambda shape: pl.BlockSpec(shape, lambda i: (0,) * len(shape))
    nd = full((n, d))
    rd = full((1, d))
    return pl.pallas_call(
        _lru_step_kernel,
        out_shape=(jax.ShapeDtypeStruct((n, d), BF16), jax.ShapeDtypeStruct((n, d), F32)),
        grid=(1,),
        in_specs=[
            pl.BlockSpec((n, d), lambda i: (0, 1)),
            pl.BlockSpec((n, d), lambda i: (0, 2)),
            nd, nd, nd, nd,
            full((CONV_W, d)), rd,
            full(w_a.shape), rd, full(w_i.shape), rd, rd, rd,
        ],
        out_specs=(nd, nd),
        compiler_params=_params("arbitrary"),
        name="rglru_step",
    )(z, z, conv0[:, 0], conv0[:, 1], conv0[:, 2], h0, conv_w, row(conv_b),
      w_a, row(b_a), w_i, row(b_i), row(lam), row(g_merge))


def _glu_kernel(y_ref, w_ref, b_ref, g_ref, o_ref):
    for rows in _row_parts(y_ref.shape[0], 4):
        y = y_ref[rows, :]
        gate = jax.nn.sigmoid(_dot(y.astype(BF16), w_ref[...]) + b_ref[...])
        o_ref[rows, :] = _rms(y.astype(F32) * gate, g_ref[...]).astype(BF16)


def _glu(y, w, b, g, tm):
    m, d = y.shape
    return pl.pallas_call(
        _glu_kernel,
        out_shape=jax.ShapeDtypeStruct((m, d), BF16),
        grid=(m // tm,),
        in_specs=[
            pl.BlockSpec((tm, d), lambda i: (i, 0)),
            pl.BlockSpec((d, d), lambda i: (0, 0)),
            pl.BlockSpec((1, d), lambda i: (0, 0)),
            pl.BlockSpec((1, d), lambda i: (0, 0)),
        ],
        out_specs=pl.BlockSpec((tm, d), lambda i: (i, 0)),
        compiler_params=_params("parallel"),
        name="s5_glu",
    )(y, w, b.reshape(1, d), g.reshape(1, d))


def _out_proj_kernel(x_ref, ma_ref, mb_ref, w_ref, o_ref, wb_ref):
    @pl.when(pl.program_id(0) == 0)
    def _():
        wb_ref[...] = w_ref[...].astype(BF16)

    ka = ma_ref.shape[1]
    for rows in _row_parts(x_ref.shape[0], 2):
        o_ref[rows, :] = (x_ref[rows, :] + _dot(ma_ref[rows, :], wb_ref[:ka, :])
                          + _dot(mb_ref[rows, :], wb_ref[ka:, :]))


def _out_proj(x, ma, mb, w, tm):
    m, n = x.shape
    ka = ma.shape[1]
    kb = mb.shape[1]
    return pl.pallas_call(
        _out_proj_kernel,
        out_shape=jax.ShapeDtypeStruct((m, n), F32),
        grid=(m // tm,),
        in_specs=[
            pl.BlockSpec((tm, n), lambda i: (i, 0)),
            pl.BlockSpec((tm, ka), lambda i: (i, 0)),
            pl.BlockSpec((tm, kb), lambda i: (i, 0)),
            pl.BlockSpec((ka + kb, n), lambda i: (0, 0), pipeline_mode=pl.Buffered(1)),
        ],
        out_specs=pl.BlockSpec((tm, n), lambda i: (i, 0)),
        scratch_shapes=[pltpu.VMEM((ka + kb, n), BF16)],
        compiler_params=_params("arbitrary"),
        name="out_proj",
    )(x, ma, mb, w)


def _mlp_kernel(x_ref, g_ref, wu_ref, wd_ref, o_ref, h_ref):
    @pl.when(pl.program_id(1) == 0)
    def _():
        x = x_ref[...]
        h_ref[...] = _rms(x, g_ref[...]).astype(BF16)
        o_ref[...] = x

    up = _dot(h_ref[...], wu_ref[...])
    act = jnp.square(jnp.maximum(up, 0.0)).astype(BF16)
    o_ref[...] += _dot(act, wd_ref[...])


def _mlp(x, g, w_up, w_down, tm, tf):
    m, d = x.shape
    f = w_up.shape[1]
    return pl.pallas_call(
        _mlp_kernel,
        out_shape=jax.ShapeDtypeStruct((m, d), F32),
        grid=(m // tm, f // tf),
        in_specs=[
            pl.BlockSpec((tm, d), lambda i, j: (i, 0)),
            pl.BlockSpec((1, d), lambda i, j: (0, 0)),
            pl.BlockSpec((d, tf), lambda i, j: (0, j)),
            pl.BlockSpec((tf, d), lambda i, j: (j, 0)),
        ],
        out_specs=pl.BlockSpec((tm, d), lambda i, j: (i, 0)),
        scratch_shapes=[pltpu.VMEM((tm, d), BF16)],
        compiler_params=_params("parallel", "arbitrary"),
        name="mlp",
    )(x, g.reshape(1, d), w_up, w_down)


def _ple_kernel(x_ref, p_ref, g_ref, wg_ref, wp_ref, gf_ref, o_ref, wgb_ref, wpb_ref, *, final):
    @pl.when(pl.program_id(0) == 0)
    def _():
        wgb_ref[...] = wg_ref[...].astype(BF16)
        wpb_ref[...] = wp_ref[...].astype(BF16)

    for rows in _row_parts(x_ref.shape[0], 2):
        x = x_ref[rows, :]
        gate = jax.nn.sigmoid(_dot(_rms(x, g_ref[...]).astype(BF16), wgb_ref[...]))
        pe = _dot(p_ref[rows, :].astype(BF16), wpb_ref[...])
        x = x + pe * gate
        o_ref[rows, :] = _rms(x, gf_ref[...]) if final else x


def _ple(x, p, g, w_gate, w_ple, g_final, tm, final):
    m, d = x.shape
    dp = p.shape[1]
    once = pl.Buffered(1)
    return pl.pallas_call(
        functools.partial(_ple_kernel, final=final),
        out_shape=jax.ShapeDtypeStruct((m, d), F32),
        grid=(m // tm,),
        in_specs=[
            pl.BlockSpec((tm, d), lambda i: (i, 0)),
            pl.BlockSpec((tm, dp), lambda i: (i, 0)),
            pl.BlockSpec((1, d), lambda i: (0, 0)),
            pl.BlockSpec((d, d), lambda i: (0, 0), pipeline_mode=once),
            pl.BlockSpec((dp, d), lambda i: (0, 0), pipeline_mode=once),
            pl.BlockSpec((1, d), lambda i: (0, 0)),
        ],
        out_specs=pl.BlockSpec((tm, d), lambda i: (i, 0)),
        scratch_shapes=[pltpu.VMEM((d, d), BF16), pltpu.VMEM((dp, d), BF16)],
        compiler_params=_params("arbitrary"),
        name="ple_final",
    )(x, p, g.reshape(1, d), w_gate, w_ple, g_final.reshape(1, d))


def _tail(x, ma, mb, p, w, g_final, final, tm, tm_mlp, tm_ple):
    x = _out_proj(x, ma, mb, w['w_out'], tm)
    x = _mlp(x, w['g_mlp'], w['w_up'], w['w_down'], tm_mlp, 1024)
    return _ple(x, p, w['g_ple'], w['w_ple_gate'], w['w_ple'], g_final, tm_ple, final)


def kernel(x_prompt, x_sample, state_s5_re, state_s5_im, state_lru, state_conv, p_prompt, p_sample,
           g_mix, w_in, s5_lam_re, s5_lam_im, s5_log_step, s5_b_re, s5_b_im, s5_c_re, s5_c_im, s5_d,
           s5_w_glu, s5_b_glu, conv_w, conv_b, lru_w_a, lru_b_a, lru_w_i, lru_b_i, lru_lam,
           g_merge_a, g_merge_b, w_out, g_mlp, w_up, w_down, g_ple, w_ple_gate, w_ple, g_final):
    depth = g_mix.shape[0]
    nb, t_len, d_model = x_prompt.shape
    ns = x_sample.shape[0]
    d_s5 = s5_d.shape[1]
    d_lru = conv_w.shape[2]
    groups = d_s5 // S5_CH

    xp = x_prompt.reshape(nb * t_len, d_model)
    xs = x_sample.reshape(ns, d_model)
    outs = [[] for _ in range(8)]
    for l in range(depth):
        final = l == depth - 1
        w = dict(w_out=w_out[l], g_mlp=g_mlp[l], w_up=w_up[l].astype(BF16),
                 w_down=w_down[l].astype(BF16), g_ple=g_ple[l],
                 w_ple_gate=w_ple_gate[l], w_ple=w_ple[l])
        w_in_b = w_in[l].astype(BF16)
        w_glu_b = s5_w_glu[l].astype(BF16)
        lru_args = (conv_w[l], conv_b[l], lru_w_a[l].astype(BF16), lru_b_a[l],
                    lru_w_i[l].astype(BF16), lru_b_i[l], lru_lam[l], g_merge_b[l])
        w_t, w_s, w_c, tab, wb, wct, arow = _s5_params(
            s5_lam_re[l], s5_lam_im[l], s5_log_step[l], s5_b_re[l], s5_b_im[l],
            s5_c_re[l], s5_c_im[l], s5_d[l])

        z = _norm_matmul(xp, g_mix[l], w_in_b, 1024, 1024)
        z4 = z.reshape(nb, t_len, 3 * d_s5)
        y, hf = _s5_seq(z4, w_t, w_s, w_c, tab, d_s5)
        ma = _glu(y.reshape(nb * t_len, d_s5), w_glu_b, s5_b_glu[l], g_merge_a[l], 1024)
        mb, lru_h = _lru_seq(z4, *lru_args)
        xp = _tail(xp, ma, mb.reshape(nb * t_len, d_lru), p_prompt[l].reshape(nb * t_len, -1), w,
                   g_final, final, 512, 512, 256)
        hf = jnp.transpose(hf[:, :, :, :nb], (1, 3, 0, 2))
        outs[0].append(hf[0])
        outs[1].append(hf[1])
        outs[2].append(lru_h.reshape(nb, d_lru))
        outs[3].append(z4[:, t_len - (CONV_W - 1):, d_s5:d_s5 + d_lru])

        zs = _norm_matmul(xs, g_mix[l], w_in_b, ns, 512)
        ys, hsr, hsi = _s5_step(zs, state_s5_re[l].reshape(ns, groups * S5_STATE),
                                state_s5_im[l].reshape(ns, groups * S5_STATE),
                                wb, wct, arow, s5_d[l].reshape(1, d_s5))
        mas = _glu(ys, w_glu_b, s5_b_glu[l], g_merge_a[l], ns)
        mbs, lru_hs = _lru_step(zs, state_conv[l], state_lru[l], *lru_args)
        xs = _tail(xs, mas, mbs, p_sample[l].reshape(ns, -1), w, g_final, final, ns, ns, ns)
        outs[4].append(hsr.reshape(ns, groups, S5_STATE))
        outs[5].append(hsi.reshape(ns, groups, S5_STATE))
        outs[6].append(lru_hs)
        outs[7].append(jnp.concatenate(
            [state_conv[l][:, 1:], zs[:, None, d_s5:d_s5 + d_lru]], axis=1))
    return (xp.reshape(nb, t_len, d_model), xs.reshape(ns, 1, d_model),
            *(jnp.stack(o) for o in outs))
```

```python
import functools

import jax
import jax.numpy as jnp
from jax import lax
from jax.experimental import pallas as pl
from jax.experimental.pallas import tpu as pltpu

F32 = jnp.float32
BF16 = jnp.bfloat16
HIGHEST = lax.Precision.HIGHEST

EPS = 1e-6
LRU_C = 8.0
S5_CH = 16
S5_STATE = 64
LRU_HEADS = 4
CONV_W = 4
CHUNK = 16
SUBLANES = 8
LANES = 128
LANE_GROUPS = LANES // S5_CH
LANE_PAIRS = LANE_GROUPS // 2
ROWS = CHUNK * S5_CH
POW_ROWS = 32
SCAN_ROW0 = POW_ROWS
VMEM_LIMIT = 56 * 1024 * 1024


def _params(*sem):
    return pltpu.CompilerParams(dimension_semantics=sem, vmem_limit_bytes=VMEM_LIMIT)


def _rms(x, g):
    return x * lax.rsqrt(jnp.mean(x * x, axis=-1, keepdims=True) + EPS) * g


def _dot(a, b):
    return jnp.dot(a, b, preferred_element_type=F32)


def _dot_nt(a, b, precision=None):
    return lax.dot_general(a, b, (((1,), (1,)), ((), ())), precision=precision,
                           preferred_element_type=F32)


def _transpose_tiles(x):
    r, c = x.shape
    return jnp.concatenate(
        [jnp.concatenate([x[i:i + LANES, j:j + LANES].T for i in range(0, r, LANES)], axis=1)
         for j in range(0, c, LANES)], axis=0)


def _row_parts(tm, want):
    parts = want if tm % (want * 16) == 0 else 1
    step = tm // parts
    return [slice(r * step, (r + 1) * step) for r in range(parts)]


def _norm_matmul_kernel(x_ref, g_ref, w_ref, o_ref):
    for rows in _row_parts(x_ref.shape[0], 4):
        h = _rms(x_ref[rows, :], g_ref[...]).astype(BF16)
        o_ref[rows, :] = _dot(h, w_ref[...])


def _norm_matmul(x, g, w, tm, tn):
    m, k = x.shape
    n = w.shape[1]
    return pl.pallas_call(
        _norm_matmul_kernel,
        out_shape=jax.ShapeDtypeStruct((m, n), F32),
        grid=(m // tm, n // tn),
        in_specs=[
            pl.BlockSpec((tm, k), lambda i, j: (i, 0)),
            pl.BlockSpec((1, k), lambda i, j: (0, 0)),
            pl.BlockSpec((k, tn), lambda i, j: (0, j)),
        ],
        out_specs=pl.BlockSpec((tm, tn), lambda i, j: (i, j)),
        compiler_params=_params("parallel", "parallel"),
        name="norm_w_in",
    )(x, g.reshape(1, k), w)


def _s5_param_kernel(lr_ref, li_ref, ls_ref, bt_r_ref, bt_i_ref, c_r_ref, c_i_ref, d_ref,
                     wt_ref, ws_ref, wc_ref, tab_ref, wb_ref, wct_ref, arow_ref):
    row_p = lax.broadcasted_iota(jnp.int32, (POW_ROWS, LANES), 0)
    lane = lax.broadcasted_iota(jnp.int32, (ROWS, LANES), 1)
    lane16 = lax.broadcasted_iota(jnp.int32, (S5_CH, LANES), 1)
    row16 = lax.broadcasted_iota(jnp.int32, (S5_CH, LANES), 0)
    lo = lane < S5_STATE

    wb_ref[...] = jnp.zeros_like(wb_ref)
    wct_ref[...] = jnp.zeros_like(wct_ref)

    for q in range(LANE_PAIRS):
        lr, li = lr_ref[q], li_ref[q]
        step = jnp.exp(ls_ref[q])
        mag = jnp.exp(lr * step)
        ar = mag * jnp.cos(li * step)
        ai = mag * jnp.sin(li * step)
        nr, ni = ar - 1.0, ai
        den = lr * lr + li * li
        cr = (nr * lr + ni * li) / den
        ci = (ni * lr - nr * li) / den
        bt_r, bt_i = bt_r_ref[q], bt_i_ref[q]
        bb_r = cr * bt_r - ci * bt_i
        bb_i = cr * bt_i + ci * bt_r
        c_r, c_i = c_r_ref[q], c_i_ref[q]

        pr = jnp.ones((POW_ROWS, LANES), F32)
        pi = jnp.zeros((POW_ROWS, LANES), F32)
        sr, si = ar, ai
        sq = []
        for m in range(POW_ROWS.bit_length() - 1 + SUBLANES):
            sq.append((sr, si))
            if (1 << m) < POW_ROWS:
                bit = ((row_p >> m) & 1) == 1
                pr, pi = (jnp.where(bit, pr * sr - pi * si, pr),
                          jnp.where(bit, pr * si + pi * sr, pi))
            sr, si = sr * sr - si * si, 2.0 * sr * si
        m0 = CHUNK.bit_length() - 1
        scan_r = jnp.concatenate([sq[m0 + j][0] for j in range(SUBLANES)], axis=0)
        scan_i = jnp.concatenate([sq[m0 + j][1] for j in range(SUBLANES)], axis=0)
        pad = jnp.zeros((LANES - POW_ROWS - SUBLANES, LANES), F32)
        tab_ref[q, 0] = jnp.concatenate([pr, scan_r, pad], axis=0).T
        tab_ref[q, 1] = jnp.concatenate([pi, scan_i, pad], axis=0).T
        arow_ref[q, 0] = pr[:SUBLANES]
        arow_ref[q, 1] = pi[:SUBLANES]

        def expand(tab, k0, sign):
            return jnp.concatenate(
                [jnp.broadcast_to(tab[k0 + sign * s:k0 + sign * s + 1, :], (S5_CH, LANES))
                 for s in range(CHUNK)], axis=0)

        tile = lambda v: jnp.concatenate([v] * CHUNK, axis=0)

        e_r, e_i = expand(pr, CHUNK - 1, -1), expand(pi, CHUNK - 1, -1)
        t_r, t_i = tile(bb_r), tile(bb_i)
        ws_r = _transpose_tiles(t_r * e_r - t_i * e_i)
        ws_i = _transpose_tiles(t_r * e_i + t_i * e_r)
        for h in range(2):
            rows = slice(h * S5_STATE, (h + 1) * S5_STATE)
            ws_ref[2 * q + h] = jnp.concatenate([ws_r[rows], ws_i[rows]], axis=0).astype(BF16)

        e_r, e_i = expand(pr, 1, 1), expand(pi, 1, 1)
        t_r, t_i = tile(c_r), tile(c_i)
        ca_r = t_r * e_r - t_i * e_i
        ca_n = -(t_r * e_i + t_i * e_r)
        wc_ref[2 * q] = jnp.where(lo, ca_r, pltpu.roll(ca_n, S5_STATE, 1)).astype(BF16)
        wc_ref[2 * q + 1] = jnp.where(lo, pltpu.roll(ca_r, S5_STATE, 1), ca_n).astype(BF16)

        e_r, e_i = expand(pr, 0, 1), expand(pi, 0, 1)
        c0_r = t_r * e_r - t_i * e_i
        c0_i = t_r * e_i + t_i * e_r

        for h in range(2):
            g = 2 * q + h
            mine = (lane16 < S5_STATE) == (h == 0)
            bm_r = jnp.where(mine, bb_r, 0.0)
            bm_i = jnp.where(mine, bb_i, 0.0)
            kt = _dot_nt(bm_r, c0_r, HIGHEST) - _dot_nt(bm_i, c0_i, HIGHEST)
            k_lo = kt[:, :LANES] + jnp.where(row16 == lane16, d_ref[g], 0.0)
            k_hi = kt[:, LANES:]
            blocks = []
            half = LANES // S5_CH
            for s in range(CHUNK):
                sh = (s % half) * S5_CH
                keep = lane16 >= sh
                r_lo = pltpu.roll(k_lo, sh, 1) if sh else k_lo
                r_hi = pltpu.roll(k_hi, sh, 1) if sh else k_hi
                if s < half:
                    blk = jnp.concatenate([jnp.where(keep, r_lo, 0.0),
                                           jnp.where(keep, r_hi, r_lo)], axis=1)
                else:
                    blk = jnp.concatenate([jnp.zeros_like(r_lo), jnp.where(keep, r_lo, 0.0)], axis=1)
                blocks.append(blk)
            wt_ref[g] = _transpose_tiles(jnp.concatenate(blocks, axis=0)).astype(BF16)

            r0 = (2 * q + h) * S5_CH
            c0 = q * 2 * LANES
            wb_ref[r0:r0 + S5_CH, c0:c0 + LANES] = bm_r.astype(BF16)
            wb_ref[r0:r0 + S5_CH, c0 + LANES:c0 + 2 * LANES] = bm_i.astype(BF16)
            wct_ref[q, r0:r0 + S5_CH, :LANES] = jnp.where(mine, c_r, 0.0).astype(BF16)
            wct_ref[q, r0:r0 + S5_CH, LANES:] = jnp.where(mine, -c_i, 0.0).astype(BF16)


def _s5_params(lam_re, lam_im, log_step, b_re, b_im, c_re, c_im, d):
    groups = lam_re.shape[0]
    pairs = groups // 2
    slabs = groups // LANE_GROUPS
    pair_row = lambda v: v.reshape(pairs, 1, LANES)
    pair_ch = lambda v: jnp.transpose(v.reshape(pairs, 2, S5_CH, S5_STATE), (0, 2, 1, 3)).reshape(
        pairs, S5_CH, LANES)
    ls = jnp.broadcast_to(log_step[:, None], (groups, S5_STATE))
    d_pad = jnp.pad(d.reshape(groups, 1, S5_CH), ((0, 0), (0, 0), (0, LANES - S5_CH)))
    blk3 = lambda i: (i, 0, 0)
    blk4 = lambda i: (i, 0, 0, 0)
    prow = pl.BlockSpec((LANE_PAIRS, 1, LANES), blk3)
    pch = pl.BlockSpec((LANE_PAIRS, S5_CH, LANES), blk3)
    return pl.pallas_call(
        _s5_param_kernel,
        out_shape=(jax.ShapeDtypeStruct((groups, ROWS, ROWS), BF16),
                   jax.ShapeDtypeStruct((groups, 2 * S5_STATE, ROWS), BF16),
                   jax.ShapeDtypeStruct((groups, ROWS, 2 * S5_STATE), BF16),
                   jax.ShapeDtypeStruct((pairs, 2, LANES, LANES), F32),
                   jax.ShapeDtypeStruct((slabs, LANES, LANE_PAIRS * 2 * LANES), BF16),
                   jax.ShapeDtypeStruct((slabs, LANE_PAIRS, LANES, 2 * LANES), BF16),
                   jax.ShapeDtypeStruct((pairs, 2, SUBLANES, LANES), F32)),
        grid=(slabs,),
        in_specs=[prow, prow, prow, pch, pch, pch, pch,
                  pl.BlockSpec((LANE_GROUPS, 1, LANES), blk3)],
        out_specs=(pl.BlockSpec((LANE_GROUPS, ROWS, ROWS), blk3),
                   pl.BlockSpec((LANE_GROUPS, 2 * S5_STATE, ROWS), blk3),
                   pl.BlockSpec((LANE_GROUPS, ROWS, 2 * S5_STATE), blk3),
                   pl.BlockSpec((LANE_PAIRS, 2, LANES, LANES), blk4),
                   pl.BlockSpec((None, LANES, LANE_PAIRS * 2 * LANES), blk3),
                   pl.BlockSpec((None, LANE_PAIRS, LANES, 2 * LANES), blk4),
                   pl.BlockSpec((LANE_PAIRS, 2, SUBLANES, LANES), blk4)),
        compiler_params=_params("parallel"),
        name="s5_params",
    )(pair_row(lam_re), pair_row(lam_im), pair_row(ls),
      pair_ch(jnp.transpose(b_re, (0, 2, 1))), pair_ch(jnp.transpose(b_im, (0, 2, 1))),
      pair_ch(c_re), pair_ch(c_im), d_pad)


def _s5_seq_kernel(z_ref, wt_ref, ws_ref, wc_ref, tab_ref, cast_ref, y_ref, hf_ref, castb_ref,
                   ut_ref, yt_ref, sr_ref, si_ref, *, nb, n_chunks):
    castb_ref[...] = cast_ref[...].astype(BF16)
    scan_steps = n_chunks.bit_length() - 1
    for n in range(nb):
        for s in range(CHUNK):
            xs = z_ref[n, pl.ds(s, n_chunks, stride=CHUNK), :]
            ut_ref[:, s * S5_CH:(s + 1) * S5_CH, n * n_chunks:(n + 1) * n_chunks] = (
                xs.T.astype(BF16).reshape(LANE_GROUPS, S5_CH, n_chunks))

    for g in range(LANE_GROUPS):
        st = _dot(ws_ref[g], ut_ref[g])
        sr_ref[g * S5_STATE:(g + 1) * S5_STATE, :] = st[:S5_STATE]
        si_ref[g * S5_STATE:(g + 1) * S5_STATE, :] = st[S5_STATE:]

    srows = LANE_GROUPS * S5_STATE
    lane = lax.broadcasted_iota(jnp.int32, (srows, n_chunks), 1)
    tab_r = jnp.concatenate([tab_ref[q, 0] for q in range(LANE_PAIRS)], axis=0)
    tab_i = jnp.concatenate([tab_ref[q, 1] for q in range(LANE_PAIRS)], axis=0)
    fin_r = jnp.zeros((srows, n_chunks), F32)
    fin_i = jnp.zeros((srows, n_chunks), F32)
    for n in range(nb):
        cols = slice(n * n_chunks, (n + 1) * n_chunks)
        xr = sr_ref[:, cols]
        xi = si_ref[:, cols]
        for k in range(scan_steps):
            d = 1 << k
            keep = lane >= d
            sr = jnp.where(keep, pltpu.roll(xr, d, 1), 0.0)
            si = jnp.where(keep, pltpu.roll(xi, d, 1), 0.0)
            pr = tab_r[:, SCAN_ROW0 + k:SCAN_ROW0 + k + 1]
            pi = tab_i[:, SCAN_ROW0 + k:SCAN_ROW0 + k + 1]
            xr, xi = xr + (pr * sr - pi * si), xi + (pr * si + pi * sr)
        last = lane == n_chunks - 1
        fin_r = fin_r + jnp.where(
            lane == n, jnp.sum(jnp.where(last, xr, 0.0), axis=1, keepdims=True), 0.0)
        fin_i = fin_i + jnp.where(
            lane == n, jnp.sum(jnp.where(last, xi, 0.0), axis=1, keepdims=True), 0.0)
        keep = lane >= 1
        sr_ref[:, cols] = jnp.where(keep, pltpu.roll(xr, 1, 1), 0.0)
        si_ref[:, cols] = jnp.where(keep, pltpu.roll(xi, 1, 1), 0.0)

    for g in range(LANE_GROUPS):
        rows = slice(g * S5_STATE, (g + 1) * S5_STATE)
        hf_ref[g, 0] = fin_r[rows]
        hf_ref[g, 1] = fin_i[rows]
        hs = jnp.concatenate([sr_ref[rows, :], si_ref[rows, :]], axis=0).astype(BF16)
        yt_ref[g] = jax.nn.gelu(_dot(wc_ref[g], hs) + _dot(wt_ref[g], ut_ref[g]))

    for n in range(nb):
        for s in range(CHUNK):
            blk = yt_ref[:, s * S5_CH:(s + 1) * S5_CH, n * n_chunks:(n + 1) * n_chunks]
            y_ref[n, pl.ds(s, n_chunks, stride=CHUNK), :] = blk.reshape(LANES, n_chunks).T


def _s5_seq(z, w_t, w_s, w_c, tab, d_s5, cast_w):
    nb, t_len, _ = z.shape
    n_chunks = t_len // CHUNK
    assert n_chunks == LANES, "the chunk axis must fill one 128-lane tile"
    groups = d_s5 // S5_CH
    steps = groups // LANE_GROUPS
    cast_blk = pl.BlockSpec((cast_w.shape[0] // steps, cast_w.shape[1]), lambda i: (i, 0))
    kern = functools.partial(_s5_seq_kernel, nb=nb, n_chunks=n_chunks)
    blk3 = lambda i: (i, 0, 0)
    blk4 = lambda i: (i, 0, 0, 0)
    return pl.pallas_call(
        kern,
        out_shape=(jax.ShapeDtypeStruct((nb, t_len, d_s5), F32),
                   jax.ShapeDtypeStruct((groups, 2, S5_STATE, LANES), F32),
                   jax.ShapeDtypeStruct(cast_w.shape, BF16)),
        grid=(steps,),
        in_specs=[
            pl.BlockSpec((nb, t_len, LANES), lambda i: (0, 0, i)),
            pl.BlockSpec((LANE_GROUPS, ROWS, ROWS), blk3),
            pl.BlockSpec((LANE_GROUPS, 2 * S5_STATE, ROWS), blk3),
            pl.BlockSpec((LANE_GROUPS, ROWS, 2 * S5_STATE), blk3),
            pl.BlockSpec((LANE_PAIRS, 2, LANES, LANES), blk4),
            cast_blk,
        ],
        out_specs=(pl.BlockSpec((nb, t_len, LANES), lambda i: (0, 0, i)),
                   pl.BlockSpec((LANE_GROUPS, 2, S5_STATE, LANES), blk4),
                   cast_blk),
        scratch_shapes=[pltpu.VMEM((LANE_GROUPS, ROWS, nb * n_chunks), BF16),
                        pltpu.VMEM((LANE_GROUPS, ROWS, nb * n_chunks), F32),
                        pltpu.VMEM((LANE_GROUPS * S5_STATE, nb * n_chunks), F32),
                        pltpu.VMEM((LANE_GROUPS * S5_STATE, nb * n_chunks), F32)],
        compiler_params=_params("parallel"),
        name="s5_seq",
    )(z, w_t, w_s, w_c, tab, cast_w)


def _s5_step_kernel(u_ref, h0r_ref, h0i_ref, wb_ref, wct_ref, arow_ref, d_ref,
                    y_ref, hr_ref, hi_ref):
    u = u_ref[...]
    bu = _dot(u.astype(BF16), wb_ref[...])
    y = d_ref[...] * u
    for q in range(LANE_PAIRS):
        cols = slice(q * LANES, (q + 1) * LANES)
        ar = arow_ref[q, 0, 1:2, :]
        ai = arow_ref[q, 1, 1:2, :]
        h0r, h0i = h0r_ref[:, cols], h0i_ref[:, cols]
        hr = ar * h0r - ai * h0i + bu[:, 2 * q * LANES:(2 * q + 1) * LANES]
        hi = ar * h0i + ai * h0r + bu[:, (2 * q + 1) * LANES:(2 * q + 2) * LANES]
        hr_ref[:, cols] = hr
        hi_ref[:, cols] = hi
        y = y + _dot_nt(jnp.concatenate([hr, hi], axis=1).astype(BF16), wct_ref[q])
    y_ref[...] = jax.nn.gelu(y).astype(BF16)


def _s5_step(z, h0r, h0i, wb, wct, arow, d):
    n = z.shape[0]
    slabs = wb.shape[0]
    sw = LANE_PAIRS * LANES
    col = lambda i: (0, i)
    blk3 = lambda i: (i, 0, 0)
    blk4 = lambda i: (i, 0, 0, 0)
    return pl.pallas_call(
        _s5_step_kernel,
        out_shape=(jax.ShapeDtypeStruct((n, slabs * LANES), BF16),
                   jax.ShapeDtypeStruct((n, slabs * sw), F32),
                   jax.ShapeDtypeStruct((n, slabs * sw), F32)),
        grid=(slabs,),
        in_specs=[
            pl.BlockSpec((n, LANES), col),
            pl.BlockSpec((n, sw), col),
            pl.BlockSpec((n, sw), col),
            pl.BlockSpec((None, LANES, 2 * sw), blk3),
            pl.BlockSpec((None, LANE_PAIRS, LANES, 2 * LANES), blk4),
            pl.BlockSpec((LANE_PAIRS, 2, SUBLANES, LANES), blk4),
            pl.BlockSpec((1, LANES), col),
        ],
        out_specs=(pl.BlockSpec((n, LANES), col),
                   pl.BlockSpec((n, sw), col),
                   pl.BlockSpec((n, sw), col)),
        compiler_params=_params("parallel"),
        name="s5_step",
    )(z, h0r, h0i, wb, wct, arow, d)


def _lru_gate_block(xc, wa, ba, wi, bi, lam):
    xb16 = xc.astype(BF16)
    r = jax.nn.sigmoid(_dot(xb16, wa) + ba)
    ig = jax.nn.sigmoid(_dot(xb16, wi) + bi)
    log_a = -LRU_C * r * jax.nn.softplus(-lam)
    a = jnp.exp(log_a)
    mult = jnp.sqrt(1.0 - a * a)
    return a, mult * (ig * xc)


def _lru_gates(xc, wa_ref, ba, wi_ref, bi, lam):
    blk = xc.shape[1] // LRU_HEADS
    parts = [_lru_gate_block(xc[:, h * blk:(h + 1) * blk], wa_ref[h], ba[:, h * blk:(h + 1) * blk],
                             wi_ref[h], bi[:, h * blk:(h + 1) * blk],
                             lam[:, h * blk:(h + 1) * blk]) for h in range(LRU_HEADS)]
    return (jnp.concatenate([p[0] for p in parts], axis=-1),
            jnp.concatenate([p[1] for p in parts], axis=-1))


def _lru_seq_kernel(xb_ref, gb_ref, cw_ref, cb_ref, wa_ref, ba_ref, wi_ref, bi_ref, lam_ref,
                    gm_ref, cast_ref, o_ref, hl_ref, castb_ref, xe_ref, a_ref, b_ref, hc_ref,
                    *, tt):
    castb_ref[...] = cast_ref[...].astype(BF16)
    halo = SUBLANES
    nseq, _, d = xb_ref.shape
    blk = d // LRU_HEADS

    @pl.when(pl.program_id(0) == 0)
    def _():
        xe_ref[:, 0:halo, :] = jnp.zeros((nseq, halo, d), F32)
        hc_ref[...] = jnp.zeros_like(hc_ref)

    xe_ref[:, halo:halo + tt, :] = xb_ref[...]
    for h in range(LRU_HEADS):
        cols = slice(h * blk, (h + 1) * blk)
        cw = cw_ref[:, cols]
        xc = cb_ref[:, cols] + xe_ref[:, halo:halo + tt, cols] * cw[CONV_W - 1:CONV_W, :]
        for k in range(1, CONV_W):
            xc = xc + xe_ref[:, halo - k:halo - k + tt, cols] * cw[CONV_W - 1 - k:CONV_W - k, :]
        a, b = _lru_gate_block(xc.reshape(nseq * tt, blk), wa_ref[h], ba_ref[:, cols],
                               wi_ref[h], bi_ref[:, cols], lam_ref[:, cols])
        a_ref[:, :, cols] = a.reshape(nseq, tt, blk)
        b_ref[:, :, cols] = b.reshape(nseq, tt, blk)
    xe_ref[:, 0:halo, :] = xb_ref[:, tt - halo:tt, :]

    def block(i, h):
        base = pl.multiple_of(i * SUBLANES, SUBLANES)
        for j in range(SUBLANES):
            row = pl.ds(base + j, 1)
            h = a_ref[:, row, :] * h + b_ref[:, row, :]
            b_ref[:, row, :] = h
        return h

    h = lax.fori_loop(0, tt // SUBLANES, block, hc_ref[...])
    hc_ref[...] = h
    hl_ref[...] = h

    out = b_ref[...] * jax.nn.gelu(gb_ref[...])
    o_ref[...] = _rms(out, gm_ref[...]).astype(BF16)


def _lru_seq(z, conv_w, conv_b, w_a, b_a, w_i, b_i, lam, g_merge, cast_w, tt=64):
    nseq, t_len, _ = z.shape
    d = conv_w.shape[1]
    steps = t_len // tt
    cast_blk = pl.BlockSpec((cast_w.shape[0] // steps, cast_w.shape[1]), lambda t: (t, 0))
    row = lambda v: v.reshape(1, d)
    const2 = lambda t: (0, 0)
    const3 = lambda t: (0, 0, 0)
    kern = functools.partial(_lru_seq_kernel, tt=tt)
    return pl.pallas_call(
        kern,
        out_shape=(jax.ShapeDtypeStruct((nseq, t_len, d), BF16),
                   jax.ShapeDtypeStruct((nseq, 1, d), F32),
                   jax.ShapeDtypeStruct(cast_w.shape, BF16)),
        grid=(steps,),
        in_specs=[
            pl.BlockSpec((nseq, tt, d), lambda t: (0, t, 1)),
            pl.BlockSpec((nseq, tt, d), lambda t: (0, t, 2)),
            pl.BlockSpec((CONV_W, d), const2),
            pl.BlockSpec((1, d), const2),
            pl.BlockSpec(w_a.shape, const3),
            pl.BlockSpec((1, d), const2),
            pl.BlockSpec(w_i.shape, const3),
            pl.BlockSpec((1, d), const2),
            pl.BlockSpec((1, d), const2),
            pl.BlockSpec((1, d), const2),
            cast_blk,
        ],
        out_specs=(pl.BlockSpec((nseq, tt, d), lambda t: (0, t, 0)),
                   pl.BlockSpec((nseq, 1, d), const3),
                   cast_blk),
        scratch_shapes=[pltpu.VMEM((nseq, tt + SUBLANES, d), F32),
                        pltpu.VMEM((nseq, tt, d), F32),
                        pltpu.VMEM((nseq, tt, d), F32),
                        pltpu.VMEM((nseq, 1, d), F32)],
        compiler_params=_params("arbitrary"),
        name="rglru_seq",
    )(z, z, conv_w, row(conv_b), w_a, row(b_a), w_i, row(b_i), row(lam), row(g_merge), cast_w)


def _lru_step_kernel(xb_ref, gb_ref, c0_ref, c1_ref, c2_ref, h0_ref, cw_ref, cb_ref,
                     wa_ref, ba_ref, wi_ref, bi_ref, lam_ref, gm_ref, o_ref, h_ref):
    xb = xb_ref[...]
    cw = cw_ref[...]
    xc = (cb_ref[...] + c0_ref[...] * cw[0:1, :] + c1_ref[...] * cw[1:2, :]
          + c2_ref[...] * cw[2:3, :] + xb * cw[3:4, :])
    a, b = _lru_gates(xc, wa_ref, ba_ref[...], wi_ref, bi_ref[...], lam_ref[...])
    h = a * h0_ref[...] + b
    h_ref[...] = h
    out = h * jax.nn.gelu(gb_ref[...])
    o_ref[...] = _rms(out, gm_ref[...]).astype(BF16)


def _lru_step(z, conv0, h0, conv_w, conv_b, w_a, b_a, w_i, b_i, lam, g_merge):
    n = z.shape[0]
    d = conv_w.shape[1]
    row = lambda v: v.reshape(1, d)
    full = lambda shape: pl.BlockSpec(shape, lambda i: (0,) * len(shape))
    nd = full((n, d))
    rd = full((1, d))
    return pl.pallas_call(
        _lru_step_kernel,
        out_shape=(jax.ShapeDtypeStruct((n, d), BF16), jax.ShapeDtypeStruct((n, d), F32)),
        grid=(1,),
        in_specs=[
            pl.BlockSpec((n, d), lambda i: (0, 1)),
            pl.BlockSpec((n, d), lambda i: (0, 2)),
            nd, nd, nd, nd,
            full((CONV_W, d)), rd,
            full(w_a.shape), rd, full(w_i.shape), rd, rd, rd,
        ],
        out_specs=(nd, nd),
        compiler_params=_params("arbitrary"),
        name="rglru_step",
    )(z, z, conv0[:, 0], conv0[:, 1], conv0[:, 2], h0, conv_w, row(conv_b),
      w_a, row(b_a), w_i, row(b_i), row(lam), row(g_merge))


def _glu_kernel(y_ref, w_ref, b_ref, g_ref, o_ref):
    for rows in _row_parts(y_ref.shape[0], 4):
        y = y_ref[rows, :]
        gate = jax.nn.sigmoid(_dot(y.astype(BF16), w_ref[...]) + b_ref[...])
        o_ref[rows, :] = _rms(y.astype(F32) * gate, g_ref[...]).astype(BF16)


def _glu(y, w, b, g, tm):
    m, d = y.shape
    return pl.pallas_call(
        _glu_kernel,
        out_shape=jax.ShapeDtypeStruct((m, d), BF16),
        grid=(m // tm,),
        in_specs=[
            pl.BlockSpec((tm, d), lambda i: (i, 0)),
            pl.BlockSpec((d, d), lambda i: (0, 0)),
            pl.BlockSpec((1, d), lambda i: (0, 0)),
            pl.BlockSpec((1, d), lambda i: (0, 0)),
        ],
        out_specs=pl.BlockSpec((tm, d), lambda i: (i, 0)),
        compiler_params=_params("parallel"),
        name="s5_glu",
    )(y, w, b.reshape(1, d), g.reshape(1, d))


def _out_proj_kernel(x_ref, ma_ref, mb_ref, w_ref, o_ref, wb_ref):
    @pl.when(pl.program_id(0) == 0)
    def _():
        wb_ref[...] = w_ref[...].astype(BF16)

    ka = ma_ref.shape[1]
    for rows in _row_parts(x_ref.shape[0], 2):
        o_ref[rows, :] = (x_ref[rows, :] + _dot(ma_ref[rows, :], wb_ref[:ka, :])
                          + _dot(mb_ref[rows, :], wb_ref[ka:, :]))


def _out_proj(x, ma, mb, w, tm):
    m, n = x.shape
    ka = ma.shape[1]
    kb = mb.shape[1]
    return pl.pallas_call(
        _out_proj_kernel,
        out_shape=jax.ShapeDtypeStruct((m, n), F32),
        grid=(m // tm,),
        in_specs=[
            pl.BlockSpec((tm, n), lambda i: (i, 0)),
            pl.BlockSpec((tm, ka), lambda i: (i, 0)),
            pl.BlockSpec((tm, kb), lambda i: (i, 0)),
            pl.BlockSpec((ka + kb, n), lambda i: (0, 0), pipeline_mode=pl.Buffered(1)),
        ],
        out_specs=pl.BlockSpec((tm, n), lambda i: (i, 0)),
        scratch_shapes=[pltpu.VMEM((ka + kb, n), BF16)],
        compiler_params=_params("arbitrary"),
        name="out_proj",
    )(x, ma, mb, w)


def _mlp_kernel(x_ref, g_ref, wu_ref, wd_ref, o_ref, h_ref):
    @pl.when(pl.program_id(1) == 0)
    def _():
        x = x_ref[...]
        h_ref[...] = _rms(x, g_ref[...]).astype(BF16)
        o_ref[...] = x

    up = _dot(h_ref[...], wu_ref[...])
    act = jnp.square(jnp.maximum(up, 0.0)).astype(BF16)
    o_ref[...] += _dot(act, wd_ref[...])


def _mlp(x, g, w_up, w_down, tm, tf):
    m, d = x.shape
    f = w_up.shape[1]
    return pl.pallas_call(
        _mlp_kernel,
        out_shape=jax.ShapeDtypeStruct((m, d), F32),
        grid=(m // tm, f // tf),
        in_specs=[
            pl.BlockSpec((tm, d), lambda i, j: (i, 0)),
            pl.BlockSpec((1, d), lambda i, j: (0, 0)),
            pl.BlockSpec((d, tf), lambda i, j: (0, j)),
            pl.BlockSpec((tf, d), lambda i, j: (j, 0)),
        ],
        out_specs=pl.BlockSpec((tm, d), lambda i, j: (i, 0)),
        scratch_shapes=[pltpu.VMEM((tm, d), BF16)],
        compiler_params=_params("parallel", "arbitrary"),
        name="mlp",
    )(x, g.reshape(1, d), w_up, w_down)


def _ple_kernel(x_ref, p_ref, g_ref, wg_ref, wp_ref, gf_ref, o_ref, wgb_ref, wpb_ref, *, final):
    @pl.when(pl.program_id(0) == 0)
    def _():
        wgb_ref[...] = wg_ref[...].astype(BF16)
        wpb_ref[...] = wp_ref[...].astype(BF16)

    for rows in _row_parts(x_ref.shape[0], 2):
        x = x_ref[rows, :]
        gate = jax.nn.sigmoid(_dot(_rms(x, g_ref[...]).astype(BF16), wgb_ref[...]))
        pe = _dot(p_ref[rows, :].astype(BF16), wpb_ref[...])
        x = x + pe * gate
        o_ref[rows, :] = _rms(x, gf_ref[...]) if final else x


def _ple(x, p, g, w_gate, w_ple, g_final, tm, final):
    m, d = x.shape
    dp = p.shape[1]
    once = pl.Buffered(1)
    return pl.pallas_call(
        functools.partial(_ple_kernel, final=final),
        out_shape=jax.ShapeDtypeStruct((m, d), F32),
        grid=(m // tm,),
        in_specs=[
            pl.BlockSpec((tm, d), lambda i: (i, 0)),
            pl.BlockSpec((tm, dp), lambda i: (i, 0)),
            pl.BlockSpec((1, d), lambda i: (0, 0)),
            pl.BlockSpec((d, d), lambda i: (0, 0), pipeline_mode=once),
            pl.BlockSpec((dp, d), lambda i: (0, 0), pipeline_mode=once),
            pl.BlockSpec((1, d), lambda i: (0, 0)),
        ],
        out_specs=pl.BlockSpec((tm, d), lambda i: (i, 0)),
        scratch_shapes=[pltpu.VMEM((d, d), BF16), pltpu.VMEM((dp, d), BF16)],
        compiler_params=_params("arbitrary"),
        name="ple_final",
    )(x, p, g.reshape(1, d), w_gate, w_ple, g_final.reshape(1, d))


def _tail(x, ma, mb, p, w, g_final, final, tm, tm_mlp, tm_ple):
    x = _out_proj(x, ma, mb, w['w_out'], tm)
    x = _mlp(x, w['g_mlp'], w['w_up'], w['w_down'], tm_mlp, 1024)
    return _ple(x, p, w['g_ple'], w['w_ple_gate'], w['w_ple'], g_final, tm_ple, final)


def kernel(x_prompt, x_sample, state_s5_re, state_s5_im, state_lru, state_conv, p_prompt, p_sample,
           g_mix, w_in, s5_lam_re, s5_lam_im, s5_log_step, s5_b_re, s5_b_im, s5_c_re, s5_c_im, s5_d,
           s5_w_glu, s5_b_glu, conv_w, conv_b, lru_w_a, lru_b_a, lru_w_i, lru_b_i, lru_lam,
           g_merge_a, g_merge_b, w_out, g_mlp, w_up, w_down, g_ple, w_ple_gate, w_ple, g_final):
    depth = g_mix.shape[0]
    nb, t_len, d_model = x_prompt.shape
    ns = x_sample.shape[0]
    d_s5 = s5_d.shape[1]
    d_lru = conv_w.shape[2]
    groups = d_s5 // S5_CH

    xp = x_prompt.reshape(nb * t_len, d_model)
    xs = x_sample.reshape(ns, d_model)
    outs = [[] for _ in range(8)]
    for l in range(depth):
        final = l == depth - 1
        w_in_b = w_in[l].astype(BF16)
        w_glu_b = s5_w_glu[l].astype(BF16)
        lru_args = (conv_w[l], conv_b[l], lru_w_a[l].astype(BF16), lru_b_a[l],
                    lru_w_i[l].astype(BF16), lru_b_i[l], lru_lam[l], g_merge_b[l])
        w_t, w_s, w_c, tab, wb, wct, arow = _s5_params(
            s5_lam_re[l], s5_lam_im[l], s5_log_step[l], s5_b_re[l], s5_b_im[l],
            s5_c_re[l], s5_c_im[l], s5_d[l])

        z = _norm_matmul(xp, g_mix[l], w_in_b, 1024, 1024)
        z4 = z.reshape(nb, t_len, 3 * d_s5)
        y, hf, w_down_b = _s5_seq(z4, w_t, w_s, w_c, tab, d_s5, w_down[l])
        ma = _glu(y.reshape(nb * t_len, d_s5), w_glu_b, s5_b_glu[l], g_merge_a[l], 1024)
        mb, lru_h, w_up_b = _lru_seq(z4, *lru_args, w_up[l])
        w = dict(w_out=w_out[l], g_mlp=g_mlp[l], w_up=w_up_b, w_down=w_down_b, g_ple=g_ple[l],
                 w_ple_gate=w_ple_gate[l], w_ple=w_ple[l])
        xp = _tail(xp, ma, mb.reshape(nb * t_len, d_lru), p_prompt[l].reshape(nb * t_len, -1), w,
                   g_final, final, 512, 512, 256)
        hf = jnp.transpose(hf[:, :, :, :nb], (1, 3, 0, 2))
        outs[0].append(hf[0])
        outs[1].append(hf[1])
        outs[2].append(lru_h.reshape(nb, d_lru))
        outs[3].append(z4[:, t_len - (CONV_W - 1):, d_s5:d_s5 + d_lru])

        zs = _norm_matmul(xs, g_mix[l], w_in_b, ns, 512)
        ys, hsr, hsi = _s5_step(zs, state_s5_re[l].reshape(ns, groups * S5_STATE),
                                state_s5_im[l].reshape(ns, groups * S5_STATE),
                                wb, wct, arow, s5_d[l].reshape(1, d_s5))
        mas = _glu(ys, w_glu_b, s5_b_glu[l], g_merge_a[l], ns)
        mbs, lru_hs = _lru_step(zs, state_conv[l], state_lru[l], *lru_args)
        xs = _tail(xs, mas, mbs, p_sample[l].reshape(ns, -1), w, g_final, final, ns, ns, ns)
        outs[4].append(hsr.reshape(ns, groups, S5_STATE))
        outs[5].append(hsi.reshape(ns, groups, S5_STATE))
        outs[6].append(lru_hs)
        outs[7].append(jnp.concatenate(
            [state_conv[l][:, 1:], zs[:, None, d_s5:d_s5 + d_lru]], axis=1))
    return (xp.reshape(nb, t_len, d_model), xs.reshape(ns, 1, d_model),
            *(jnp.stack(o) for o in outs))
```

```python
import functools

import jax
import jax.numpy as jnp
from jax import lax
from jax.experimental import pallas as pl
from jax.experimental.pallas import tpu as pltpu

F32 = jnp.float32
BF16 = jnp.bfloat16
HIGHEST = lax.Precision.HIGHEST

EPS = 1e-6
LRU_C = 8.0
S5_CH = 16
S5_STATE = 64
LRU_HEADS = 4
CONV_W = 4
CHUNK = 16
SUBLANES = 8
LANES = 128
LANE_GROUPS = LANES // S5_CH
LANE_PAIRS = LANE_GROUPS // 2
ROWS = CHUNK * S5_CH
POW_ROWS = 32
SCAN_ROW0 = POW_ROWS
VMEM_LIMIT = 56 * 1024 * 1024


def _params(*sem):
    return pltpu.CompilerParams(dimension_semantics=sem, vmem_limit_bytes=VMEM_LIMIT)


def _rms(x, g):
    return x * lax.rsqrt(jnp.mean(x * x, axis=-1, keepdims=True) + EPS) * g


def _dot(a, b):
    return jnp.dot(a, b, preferred_element_type=F32)


def _dot_nt(a, b, precision=None):
    return lax.dot_general(a, b, (((1,), (1,)), ((), ())), precision=precision,
                           preferred_element_type=F32)


def _transpose_tiles(x):
    r, c = x.shape
    return jnp.concatenate(
        [jnp.concatenate([x[i:i + LANES, j:j + LANES].T for i in range(0, r, LANES)], axis=1)
         for j in range(0, c, LANES)], axis=0)


def _row_parts(tm, want):
    parts = want if tm % (want * 16) == 0 else 1
    step = tm // parts
    return [slice(r * step, (r + 1) * step) for r in range(parts)]


def _on_last_row_tile(fn):
    pl.when(pl.program_id(0) == pl.num_programs(0) - 1)(fn)


def _norm_matmul_kernel(x_ref, xs_ref, g_ref, w_ref, o_ref, os_ref):
    for rows in _row_parts(x_ref.shape[0], 4):
        h = _rms(x_ref[rows, :], g_ref[...]).astype(BF16)
        o_ref[rows, :] = _dot(h, w_ref[...])

    @_on_last_row_tile
    def _():
        os_ref[...] = _dot(_rms(xs_ref[...], g_ref[...]).astype(BF16), w_ref[...])


def _norm_matmul(x, xs, g, w, tm, tn):
    m, k = x.shape
    ms = xs.shape[0]
    n = w.shape[1]
    ni = m // tm
    return pl.pallas_call(
        _norm_matmul_kernel,
        out_shape=(jax.ShapeDtypeStruct((m, n), F32), jax.ShapeDtypeStruct((ms, n), F32)),
        grid=(ni, n // tn),
        in_specs=[
            pl.BlockSpec((tm, k), lambda i, j: (i, 0)),
            pl.BlockSpec((ms, k), lambda i, j: (0, 0)),
            pl.BlockSpec((1, k), lambda i, j: (0, 0)),
            pl.BlockSpec((k, tn), lambda i, j: (0, j)),
        ],
        out_specs=(pl.BlockSpec((tm, tn), lambda i, j: (i, j)),
                   pl.BlockSpec((ms, tn), lambda i, j: (0, jnp.where(i == ni - 1, j, 0)))),
        compiler_params=_params("arbitrary", "arbitrary"),
        name="norm_w_in",
    )(x, xs, g.reshape(1, k), w)


def _s5_param_kernel(lr_ref, li_ref, ls_ref, bt_r_ref, bt_i_ref, c_r_ref, c_i_ref, d_ref, cast_ref,
                     wt_ref, ws_ref, wc_ref, tab_ref, wb_ref, wct_ref, arow_ref, castb_ref):
    castb_ref[...] = cast_ref[...].astype(BF16)
    row_p = lax.broadcasted_iota(jnp.int32, (POW_ROWS, LANES), 0)
    lane = lax.broadcasted_iota(jnp.int32, (ROWS, LANES), 1)
    lane16 = lax.broadcasted_iota(jnp.int32, (S5_CH, LANES), 1)
    row16 = lax.broadcasted_iota(jnp.int32, (S5_CH, LANES), 0)
    lo = lane < S5_STATE

    wb_ref[...] = jnp.zeros_like(wb_ref)
    wct_ref[...] = jnp.zeros_like(wct_ref)

    for q in range(LANE_PAIRS):
        lr, li = lr_ref[q], li_ref[q]
        step = jnp.exp(ls_ref[q])
        mag = jnp.exp(lr * step)
        ar = mag * jnp.cos(li * step)
        ai = mag * jnp.sin(li * step)
        nr, ni = ar - 1.0, ai
        den = lr * lr + li * li
        cr = (nr * lr + ni * li) / den
        ci = (ni * lr - nr * li) / den
        bt_r, bt_i = bt_r_ref[q], bt_i_ref[q]
        bb_r = cr * bt_r - ci * bt_i
        bb_i = cr * bt_i + ci * bt_r
        c_r, c_i = c_r_ref[q], c_i_ref[q]

        pr = jnp.ones((POW_ROWS, LANES), F32)
        pi = jnp.zeros((POW_ROWS, LANES), F32)
        sr, si = ar, ai
        sq = []
        for m in range(POW_ROWS.bit_length() - 1 + SUBLANES):
            sq.append((sr, si))
            if (1 << m) < POW_ROWS:
                bit = ((row_p >> m) & 1) == 1
                pr, pi = (jnp.where(bit, pr * sr - pi * si, pr),
                          jnp.where(bit, pr * si + pi * sr, pi))
            sr, si = sr * sr - si * si, 2.0 * sr * si
        m0 = CHUNK.bit_length() - 1
        scan_r = jnp.concatenate([sq[m0 + j][0] for j in range(SUBLANES)], axis=0)
        scan_i = jnp.concatenate([sq[m0 + j][1] for j in range(SUBLANES)], axis=0)
        pad = jnp.zeros((LANES - POW_ROWS - SUBLANES, LANES), F32)
        tab_ref[q, 0] = jnp.concatenate([pr, scan_r, pad], axis=0).T
        tab_ref[q, 1] = jnp.concatenate([pi, scan_i, pad], axis=0).T
        arow_ref[q, 0] = pr[:SUBLANES]
        arow_ref[q, 1] = pi[:SUBLANES]

        def expand(tab, k0, sign):
            return jnp.concatenate(
                [jnp.broadcast_to(tab[k0 + sign * s:k0 + sign * s + 1, :], (S5_CH, LANES))
                 for s in range(CHUNK)], axis=0)

        tile = lambda v: jnp.concatenate([v] * CHUNK, axis=0)

        e_r, e_i = expand(pr, CHUNK - 1, -1), expand(pi, CHUNK - 1, -1)
        t_r, t_i = tile(bb_r), tile(bb_i)
        ws_r = _transpose_tiles(t_r * e_r - t_i * e_i)
        ws_i = _transpose_tiles(t_r * e_i + t_i * e_r)
        for h in range(2):
            rows = slice(h * S5_STATE, (h + 1) * S5_STATE)
            ws_ref[2 * q + h] = jnp.concatenate([ws_r[rows], ws_i[rows]], axis=0).astype(BF16)

        e_r, e_i = expand(pr, 1, 1), expand(pi, 1, 1)
        t_r, t_i = tile(c_r), tile(c_i)
        ca_r = t_r * e_r - t_i * e_i
        ca_n = -(t_r * e_i + t_i * e_r)
        wc_ref[2 * q] = jnp.where(lo, ca_r, pltpu.roll(ca_n, S5_STATE, 1)).astype(BF16)
        wc_ref[2 * q + 1] = jnp.where(lo, pltpu.roll(ca_r, S5_STATE, 1), ca_n).astype(BF16)

        e_r, e_i = expand(pr, 0, 1), expand(pi, 0, 1)
        c0_r = t_r * e_r - t_i * e_i
        c0_i = t_r * e_i + t_i * e_r

        for h in range(2):
            g = 2 * q + h
            mine = (lane16 < S5_STATE) == (h == 0)
            bm_r = jnp.where(mine, bb_r, 0.0)
            bm_i = jnp.where(mine, bb_i, 0.0)
            kt = _dot_nt(bm_r, c0_r, HIGHEST) - _dot_nt(bm_i, c0_i, HIGHEST)
            k_lo = kt[:, :LANES] + jnp.where(row16 == lane16, d_ref[g], 0.0)
            k_hi = kt[:, LANES:]
            blocks = []
            half = LANES // S5_CH
            for s in range(CHUNK):
                sh = (s % half) * S5_CH
                keep = lane16 >= sh
                r_lo = pltpu.roll(k_lo, sh, 1) if sh else k_lo
                r_hi = pltpu.roll(k_hi, sh, 1) if sh else k_hi
                if s < half:
                    blk = jnp.concatenate([jnp.where(keep, r_lo, 0.0),
                                           jnp.where(keep, r_hi, r_lo)], axis=1)
                else:
                    blk = jnp.concatenate([jnp.zeros_like(r_lo), jnp.where(keep, r_lo, 0.0)], axis=1)
                blocks.append(blk)
            wt_ref[g] = _transpose_tiles(jnp.concatenate(blocks, axis=0)).astype(BF16)

            r0 = (2 * q + h) * S5_CH
            c0 = q * 2 * LANES
            wb_ref[r0:r0 + S5_CH, c0:c0 + LANES] = bm_r.astype(BF16)
            wb_ref[r0:r0 + S5_CH, c0 + LANES:c0 + 2 * LANES] = bm_i.astype(BF16)
            wct_ref[q, r0:r0 + S5_CH, :LANES] = jnp.where(mine, c_r, 0.0).astype(BF16)
            wct_ref[q, r0:r0 + S5_CH, LANES:] = jnp.where(mine, -c_i, 0.0).astype(BF16)


def _s5_params(lam_re, lam_im, log_step, b_re, b_im, c_re, c_im, d, cast_w):
    groups = lam_re.shape[0]
    pairs = groups // 2
    slabs = groups // LANE_GROUPS
    cast_blk = pl.BlockSpec((cast_w.shape[0] // slabs, cast_w.shape[1]), lambda i: (i, 0))
    pair_row = lambda v: v.reshape(pairs, 1, LANES)
    pair_ch = lambda v: jnp.transpose(v.reshape(pairs, 2, S5_CH, S5_STATE), (0, 2, 1, 3)).reshape(
        pairs, S5_CH, LANES)
    ls = jnp.broadcast_to(log_step[:, None], (groups, S5_STATE))
    d_pad = jnp.pad(d.reshape(groups, 1, S5_CH), ((0, 0), (0, 0), (0, LANES - S5_CH)))
    blk3 = lambda i: (i, 0, 0)
    blk4 = lambda i: (i, 0, 0, 0)
    prow = pl.BlockSpec((LANE_PAIRS, 1, LANES), blk3)
    pch = pl.BlockSpec((LANE_PAIRS, S5_CH, LANES), blk3)
    return pl.pallas_call(
        _s5_param_kernel,
        out_shape=(jax.ShapeDtypeStruct((groups, ROWS, ROWS), BF16),
                   jax.ShapeDtypeStruct((groups, 2 * S5_STATE, ROWS), BF16),
                   jax.ShapeDtypeStruct((groups, ROWS, 2 * S5_STATE), BF16),
                   jax.ShapeDtypeStruct((pairs, 2, LANES, LANES), F32),
                   jax.ShapeDtypeStruct((slabs, LANES, LANE_PAIRS * 2 * LANES), BF16),
                   jax.ShapeDtypeStruct((slabs, LANE_PAIRS, LANES, 2 * LANES), BF16),
                   jax.ShapeDtypeStruct((pairs, 2, SUBLANES, LANES), F32),
                   jax.ShapeDtypeStruct(cast_w.shape, BF16)),
        grid=(slabs,),
        in_specs=[prow, prow, prow, pch, pch, pch, pch,
                  pl.BlockSpec((LANE_GROUPS, 1, LANES), blk3), cast_blk],
        out_specs=(pl.BlockSpec((LANE_GROUPS, ROWS, ROWS), blk3),
                   pl.BlockSpec((LANE_GROUPS, 2 * S5_STATE, ROWS), blk3),
                   pl.BlockSpec((LANE_GROUPS, ROWS, 2 * S5_STATE), blk3),
                   pl.BlockSpec((LANE_PAIRS, 2, LANES, LANES), blk4),
                   pl.BlockSpec((None, LANES, LANE_PAIRS * 2 * LANES), blk3),
                   pl.BlockSpec((None, LANE_PAIRS, LANES, 2 * LANES), blk4),
                   pl.BlockSpec((LANE_PAIRS, 2, SUBLANES, LANES), blk4),
                   cast_blk),
        compiler_params=_params("parallel"),
        name="s5_params",
    )(pair_row(lam_re), pair_row(lam_im), pair_row(ls),
      pair_ch(jnp.transpose(b_re, (0, 2, 1))), pair_ch(jnp.transpose(b_im, (0, 2, 1))),
      pair_ch(c_re), pair_ch(c_im), d_pad, cast_w)


def _s5_seq_kernel(z_ref, wt_ref, ws_ref, wc_ref, tab_ref, cast_ref, y_ref, hf_ref, castb_ref,
                   ut_ref, yt_ref, sr_ref, si_ref, *, nb, n_chunks):
    castb_ref[...] = cast_ref[...].astype(BF16)
    scan_steps = n_chunks.bit_length() - 1
    for n in range(nb):
        for s in range(CHUNK):
            xs = z_ref[n, pl.ds(s, n_chunks, stride=CHUNK), :]
            ut_ref[:, s * S5_CH:(s + 1) * S5_CH, n * n_chunks:(n + 1) * n_chunks] = (
                xs.T.astype(BF16).reshape(LANE_GROUPS, S5_CH, n_chunks))

    for g in range(LANE_GROUPS):
        st = _dot(ws_ref[g], ut_ref[g])
        sr_ref[g * S5_STATE:(g + 1) * S5_STATE, :] = st[:S5_STATE]
        si_ref[g * S5_STATE:(g + 1) * S5_STATE, :] = st[S5_STATE:]

    srows = LANE_GROUPS * S5_STATE
    lane = lax.broadcasted_iota(jnp.int32, (srows, n_chunks), 1)
    tab_r = jnp.concatenate([tab_ref[q, 0] for q in range(LANE_PAIRS)], axis=0)
    tab_i = jnp.concatenate([tab_ref[q, 1] for q in range(LANE_PAIRS)], axis=0)
    fin_r = jnp.zeros((srows, n_chunks), F32)
    fin_i = jnp.zeros((srows, n_chunks), F32)
    for n in range(nb):
        cols = slice(n * n_chunks, (n + 1) * n_chunks)
        xr = sr_ref[:, cols]
        xi = si_ref[:, cols]
        for k in range(scan_steps):
            d = 1 << k
            keep = lane >= d
            sr = jnp.where(keep, pltpu.roll(xr, d, 1), 0.0)
            si = jnp.where(keep, pltpu.roll(xi, d, 1), 0.0)
            pr = tab_r[:, SCAN_ROW0 + k:SCAN_ROW0 + k + 1]
            pi = tab_i[:, SCAN_ROW0 + k:SCAN_ROW0 + k + 1]
            xr, xi = xr + (pr * sr - pi * si), xi + (pr * si + pi * sr)
        last = lane == n_chunks - 1
        fin_r = fin_r + jnp.where(
            lane == n, jnp.sum(jnp.where(last, xr, 0.0), axis=1, keepdims=True), 0.0)
        fin_i = fin_i + jnp.where(
            lane == n, jnp.sum(jnp.where(last, xi, 0.0), axis=1, keepdims=True), 0.0)
        keep = lane >= 1
        sr_ref[:, cols] = jnp.where(keep, pltpu.roll(xr, 1, 1), 0.0)
        si_ref[:, cols] = jnp.where(keep, pltpu.roll(xi, 1, 1), 0.0)

    for g in range(LANE_GROUPS):
        rows = slice(g * S5_STATE, (g + 1) * S5_STATE)
        hf_ref[g, 0] = fin_r[rows]
        hf_ref[g, 1] = fin_i[rows]
        hs = jnp.concatenate([sr_ref[rows, :], si_ref[rows, :]], axis=0).astype(BF16)
        yt_ref[g] = jax.nn.gelu(_dot(wc_ref[g], hs) + _dot(wt_ref[g], ut_ref[g]))

    for n in range(nb):
        for s in range(CHUNK):
            blk = yt_ref[:, s * S5_CH:(s + 1) * S5_CH, n * n_chunks:(n + 1) * n_chunks]
            y_ref[n, pl.ds(s, n_chunks, stride=CHUNK), :] = blk.reshape(LANES, n_chunks).T


def _s5_seq(z, w_t, w_s, w_c, tab, d_s5, cast_w):
    nb, t_len, _ = z.shape
    n_chunks = t_len // CHUNK
    assert n_chunks == LANES, "the chunk axis must fill one 128-lane tile"
    groups = d_s5 // S5_CH
    steps = groups // LANE_GROUPS
    cast_blk = pl.BlockSpec((cast_w.shape[0] // steps, cast_w.shape[1]), lambda i: (i, 0))
    kern = functools.partial(_s5_seq_kernel, nb=nb, n_chunks=n_chunks)
    blk3 = lambda i: (i, 0, 0)
    blk4 = lambda i: (i, 0, 0, 0)
    return pl.pallas_call(
        kern,
        out_shape=(jax.ShapeDtypeStruct((nb, t_len, d_s5), F32),
                   jax.ShapeDtypeStruct((groups, 2, S5_STATE, LANES), F32),
                   jax.ShapeDtypeStruct(cast_w.shape, BF16)),
        grid=(steps,),
        in_specs=[
            pl.BlockSpec((nb, t_len, LANES), lambda i: (0, 0, i)),
            pl.BlockSpec((LANE_GROUPS, ROWS, ROWS), blk3),
            pl.BlockSpec((LANE_GROUPS, 2 * S5_STATE, ROWS), blk3),
            pl.BlockSpec((LANE_GROUPS, ROWS, 2 * S5_STATE), blk3),
            pl.BlockSpec((LANE_PAIRS, 2, LANES, LANES), blk4),
            cast_blk,
        ],
        out_specs=(pl.BlockSpec((nb, t_len, LANES), lambda i: (0, 0, i)),
                   pl.BlockSpec((LANE_GROUPS, 2, S5_STATE, LANES), blk4),
                   cast_blk),
        scratch_shapes=[pltpu.VMEM((LANE_GROUPS, ROWS, nb * n_chunks), BF16),
                        pltpu.VMEM((LANE_GROUPS, ROWS, nb * n_chunks), F32),
                        pltpu.VMEM((LANE_GROUPS * S5_STATE, nb * n_chunks), F32),
                        pltpu.VMEM((LANE_GROUPS * S5_STATE, nb * n_chunks), F32)],
        compiler_params=_params("parallel"),
        name="s5_seq",
    )(z, w_t, w_s, w_c, tab, cast_w)


def _s5_step_kernel(u_ref, h0r_ref, h0i_ref, wb_ref, wct_ref, arow_ref, d_ref,
                    y_ref, hr_ref, hi_ref):
    u = u_ref[...]
    bu = _dot(u.astype(BF16), wb_ref[...])
    y = d_ref[...] * u
    for q in range(LANE_PAIRS):
        cols = slice(q * LANES, (q + 1) * LANES)
        ar = arow_ref[q, 0, 1:2, :]
        ai = arow_ref[q, 1, 1:2, :]
        h0r, h0i = h0r_ref[:, cols], h0i_ref[:, cols]
        hr = ar * h0r - ai * h0i + bu[:, 2 * q * LANES:(2 * q + 1) * LANES]
        hi = ar * h0i + ai * h0r + bu[:, (2 * q + 1) * LANES:(2 * q + 2) * LANES]
        hr_ref[:, cols] = hr
        hi_ref[:, cols] = hi
        y = y + _dot_nt(jnp.concatenate([hr, hi], axis=1).astype(BF16), wct_ref[q])
    y_ref[...] = jax.nn.gelu(y).astype(BF16)


def _s5_step(z, h0r, h0i, wb, wct, arow, d):
    n = z.shape[0]
    slabs = wb.shape[0]
    sw = LANE_PAIRS * LANES
    col = lambda i: (0, i)
    blk3 = lambda i: (i, 0, 0)
    blk4 = lambda i: (i, 0, 0, 0)
    return pl.pallas_call(
        _s5_step_kernel,
        out_shape=(jax.ShapeDtypeStruct((n, slabs * LANES), BF16),
                   jax.ShapeDtypeStruct((n, slabs * sw), F32),
                   jax.ShapeDtypeStruct((n, slabs * sw), F32)),
        grid=(slabs,),
        in_specs=[
            pl.BlockSpec((n, LANES), col),
            pl.BlockSpec((n, sw), col),
            pl.BlockSpec((n, sw), col),
            pl.BlockSpec((None, LANES, 2 * sw), blk3),
            pl.BlockSpec((None, LANE_PAIRS, LANES, 2 * LANES), blk4),
            pl.BlockSpec((LANE_PAIRS, 2, SUBLANES, LANES), blk4),
            pl.BlockSpec((1, LANES), col),
        ],
        out_specs=(pl.BlockSpec((n, LANES), col),
                   pl.BlockSpec((n, sw), col),
                   pl.BlockSpec((n, sw), col)),
        compiler_params=_params("parallel"),
        name="s5_step",
    )(z, h0r, h0i, wb, wct, arow, d)


def _lru_gate_block(xc, wa, ba, wi, bi, lam):
    xb16 = xc.astype(BF16)
    r = jax.nn.sigmoid(_dot(xb16, wa) + ba)
    ig = jax.nn.sigmoid(_dot(xb16, wi) + bi)
    log_a = -LRU_C * r * jax.nn.softplus(-lam)
    a = jnp.exp(log_a)
    mult = jnp.sqrt(1.0 - a * a)
    return a, mult * (ig * xc)


def _lru_gates(xc, wa_ref, ba, wi_ref, bi, lam):
    blk = xc.shape[1] // LRU_HEADS
    parts = [_lru_gate_block(xc[:, h * blk:(h + 1) * blk], wa_ref[h], ba[:, h * blk:(h + 1) * blk],
                             wi_ref[h], bi[:, h * blk:(h + 1) * blk],
                             lam[:, h * blk:(h + 1) * blk]) for h in range(LRU_HEADS)]
    return (jnp.concatenate([p[0] for p in parts], axis=-1),
            jnp.concatenate([p[1] for p in parts], axis=-1))


def _lru_seq_kernel(xb_ref, gb_ref, cw_ref, cb_ref, wa_ref, ba_ref, wi_ref, bi_ref, lam_ref,
                    gm_ref, cast_ref, o_ref, hl_ref, castb_ref, xe_ref, a_ref, b_ref, hc_ref,
                    *, tt):
    castb_ref[...] = cast_ref[...].astype(BF16)
    halo = SUBLANES
    nseq, _, d = xb_ref.shape
    blk = d // LRU_HEADS

    @pl.when(pl.program_id(0) == 0)
    def _():
        xe_ref[:, 0:halo, :] = jnp.zeros((nseq, halo, d), F32)
        hc_ref[...] = jnp.zeros_like(hc_ref)

    xe_ref[:, halo:halo + tt, :] = xb_ref[...]
    for h in range(LRU_HEADS):
        cols = slice(h * blk, (h + 1) * blk)
        cw = cw_ref[:, cols]
        xc = cb_ref[:, cols] + xe_ref[:, halo:halo + tt, cols] * cw[CONV_W - 1:CONV_W, :]
        for k in range(1, CONV_W):
            xc = xc + xe_ref[:, halo - k:halo - k + tt, cols] * cw[CONV_W - 1 - k:CONV_W - k, :]
        a, b = _lru_gate_block(xc.reshape(nseq * tt, blk), wa_ref[h], ba_ref[:, cols],
                               wi_ref[h], bi_ref[:, cols], lam_ref[:, cols])
        a_ref[:, :, cols] = a.reshape(nseq, tt, blk)
        b_ref[:, :, cols] = b.reshape(nseq, tt, blk)
    xe_ref[:, 0:halo, :] = xb_ref[:, tt - halo:tt, :]

    def block(i, h):
        base = pl.multiple_of(i * SUBLANES, SUBLANES)
        for j in range(SUBLANES):
            row = pl.ds(base + j, 1)
            h = a_ref[:, row, :] * h + b_ref[:, row, :]
            b_ref[:, row, :] = h
        return h

    h = lax.fori_loop(0, tt // SUBLANES, block, hc_ref[...])
    hc_ref[...] = h
    hl_ref[...] = h

    out = b_ref[...] * jax.nn.gelu(gb_ref[...])
    o_ref[...] = _rms(out, gm_ref[...]).astype(BF16)


def _lru_seq(z, conv_w, conv_b, w_a, b_a, w_i, b_i, lam, g_merge, cast_w, tt=64):
    nseq, t_len, _ = z.shape
    d = conv_w.shape[1]
    steps = t_len // tt
    cast_blk = pl.BlockSpec((cast_w.shape[0] // steps, cast_w.shape[1]), lambda t: (t, 0))
    row = lambda v: v.reshape(1, d)
    const2 = lambda t: (0, 0)
    const3 = lambda t: (0, 0, 0)
    kern = functools.partial(_lru_seq_kernel, tt=tt)
    return pl.pallas_call(
        kern,
        out_shape=(jax.ShapeDtypeStruct((nseq, t_len, d), BF16),
                   jax.ShapeDtypeStruct((nseq, 1, d), F32),
                   jax.ShapeDtypeStruct(cast_w.shape, BF16)),
        grid=(steps,),
        in_specs=[
            pl.BlockSpec((nseq, tt, d), lambda t: (0, t, 1)),
            pl.BlockSpec((nseq, tt, d), lambda t: (0, t, 2)),
            pl.BlockSpec((CONV_W, d), const2),
            pl.BlockSpec((1, d), const2),
            pl.BlockSpec(w_a.shape, const3),
            pl.BlockSpec((1, d), const2),
            pl.BlockSpec(w_i.shape, const3),
            pl.BlockSpec((1, d), const2),
            pl.BlockSpec((1, d), const2),
            pl.BlockSpec((1, d), const2),
            cast_blk,
        ],
        out_specs=(pl.BlockSpec((nseq, tt, d), lambda t: (0, t, 0)),
                   pl.BlockSpec((nseq, 1, d), const3),
                   cast_blk),
        scratch_shapes=[pltpu.VMEM((nseq, tt + SUBLANES, d), F32),
                        pltpu.VMEM((nseq, tt, d), F32),
                        pltpu.VMEM((nseq, tt, d), F32),
                        pltpu.VMEM((nseq, 1, d), F32)],
        compiler_params=_params("arbitrary"),
        name="rglru_seq",
    )(z, z, conv_w, row(conv_b), w_a, row(b_a), w_i, row(b_i), row(lam), row(g_merge), cast_w)


def _lru_step_kernel(xb_ref, gb_ref, c0_ref, c1_ref, c2_ref, h0_ref, cw_ref, cb_ref,
                     wa_ref, ba_ref, wi_ref, bi_ref, lam_ref, gm_ref, o_ref, h_ref):
    xb = xb_ref[...]
    cw = cw_ref[...]
    xc = (cb_ref[...] + c0_ref[...] * cw[0:1, :] + c1_ref[...] * cw[1:2, :]
          + c2_ref[...] * cw[2:3, :] + xb * cw[3:4, :])
    a, b = _lru_gates(xc, wa_ref, ba_ref[...], wi_ref, bi_ref[...], lam_ref[...])
    h = a * h0_ref[...] + b
    h_ref[...] = h
    out = h * jax.nn.gelu(gb_ref[...])
    o_ref[...] = _rms(out, gm_ref[...]).astype(BF16)


def _lru_step(z, conv0, h0, conv_w, conv_b, w_a, b_a, w_i, b_i, lam, g_merge):
    n = z.shape[0]
    d = conv_w.shape[1]
    row = lambda v: v.reshape(1, d)
    full = lambda shape: pl.BlockSpec(shape, lambda i: (0,) * len(shape))
    nd = full((n, d))
    rd = full((1, d))
    return pl.pallas_call(
        _lru_step_kernel,
        out_shape=(jax.ShapeDtypeStruct((n, d), BF16), jax.ShapeDtypeStruct((n, d), F32)),
        grid=(1,),
        in_specs=[
            pl.BlockSpec((n, d), lambda i: (0, 1)),
            pl.BlockSpec((n, d), lambda i: (0, 2)),
            nd, nd, nd, nd,
            full((CONV_W, d)), rd,
            full(w_a.shape), rd, full(w_i.shape), rd, rd, rd,
        ],
        out_specs=(nd, nd),
        compiler_params=_params("arbitrary"),
        name="rglru_step",
    )(z, z, conv0[:, 0], conv0[:, 1], conv0[:, 2], h0, conv_w, row(conv_b),
      w_a, row(b_a), w_i, row(b_i), row(lam), row(g_merge))


def _glu_rows(y, w_ref, b_ref, g_ref):
    gate = jax.nn.sigmoid(_dot(y.astype(BF16), w_ref[...]) + b_ref[...])
    return _rms(y.astype(F32) * gate, g_ref[...]).astype(BF16)


def _glu_kernel(y_ref, ys_ref, w_ref, b_ref, g_ref, o_ref, os_ref):
    for rows in _row_parts(y_ref.shape[0], 4):
        o_ref[rows, :] = _glu_rows(y_ref[rows, :], w_ref, b_ref, g_ref)

    @_on_last_row_tile
    def _():
        os_ref[...] = _glu_rows(ys_ref[...], w_ref, b_ref, g_ref)


def _glu(y, ys, w, b, g, tm):
    m, d = y.shape
    ms = ys.shape[0]
    const = lambda i: (0, 0)
    return pl.pallas_call(
        _glu_kernel,
        out_shape=(jax.ShapeDtypeStruct((m, d), BF16), jax.ShapeDtypeStruct((ms, d), BF16)),
        grid=(m // tm,),
        in_specs=[
            pl.BlockSpec((tm, d), lambda i: (i, 0)),
            pl.BlockSpec((ms, d), const),
            pl.BlockSpec((d, d), const),
            pl.BlockSpec((1, d), const),
            pl.BlockSpec((1, d), const),
        ],
        out_specs=(pl.BlockSpec((tm, d), lambda i: (i, 0)), pl.BlockSpec((ms, d), const)),
        compiler_params=_params("arbitrary"),
        name="s5_glu",
    )(y, ys, w, b.reshape(1, d), g.reshape(1, d))


def _out_proj_kernel(x_ref, ma_ref, mb_ref, xs_ref, mas_ref, mbs_ref, w_ref, o_ref, os_ref, wb_ref):
    @pl.when(pl.program_id(0) == 0)
    def _():
        wb_ref[...] = w_ref[...].astype(BF16)

    ka = ma_ref.shape[1]
    proj = lambda x, ma, mb: x + _dot(ma, wb_ref[:ka, :]) + _dot(mb, wb_ref[ka:, :])
    for rows in _row_parts(x_ref.shape[0], 2):
        o_ref[rows, :] = proj(x_ref[rows, :], ma_ref[rows, :], mb_ref[rows, :])

    @_on_last_row_tile
    def _():
        os_ref[...] = proj(xs_ref[...], mas_ref[...], mbs_ref[...])


def _out_proj(x, ma, mb, xs, mas, mbs, w, tm):
    m, n = x.shape
    ms = xs.shape[0]
    ka = ma.shape[1]
    kb = mb.shape[1]
    const = lambda i: (0, 0)
    tile = lambda i: (i, 0)
    return pl.pallas_call(
        _out_proj_kernel,
        out_shape=(jax.ShapeDtypeStruct((m, n), F32), jax.ShapeDtypeStruct((ms, n), F32)),
        grid=(m // tm,),
        in_specs=[
            pl.BlockSpec((tm, n), tile),
            pl.BlockSpec((tm, ka), tile),
            pl.BlockSpec((tm, kb), tile),
            pl.BlockSpec((ms, n), const),
            pl.BlockSpec((ms, ka), const),
            pl.BlockSpec((ms, kb), const),
            pl.BlockSpec((ka + kb, n), const, pipeline_mode=pl.Buffered(1)),
        ],
        out_specs=(pl.BlockSpec((tm, n), tile), pl.BlockSpec((ms, n), const)),
        scratch_shapes=[pltpu.VMEM((ka + kb, n), BF16)],
        compiler_params=_params("arbitrary"),
        name="out_proj",
    )(x, ma, mb, xs, mas, mbs, w)


def _mlp_kernel(x_ref, xs_ref, g_ref, wu_ref, wd_ref, o_ref, os_ref, h_ref, hs_ref):
    def rows(x_ref, o_ref, h_ref):
        @pl.when(pl.program_id(1) == 0)
        def _():
            x = x_ref[...]
            h_ref[...] = _rms(x, g_ref[...]).astype(BF16)
            o_ref[...] = x

        up = _dot(h_ref[...], wu_ref[...])
        act = jnp.square(jnp.maximum(up, 0.0)).astype(BF16)
        o_ref[...] += _dot(act, wd_ref[...])

    rows(x_ref, o_ref, h_ref)
    _on_last_row_tile(lambda: rows(xs_ref, os_ref, hs_ref))


def _mlp(x, xs, g, w_up, w_down, tm, tf):
    m, d = x.shape
    ms = xs.shape[0]
    f = w_up.shape[1]
    const = lambda i, j: (0, 0)
    return pl.pallas_call(
        _mlp_kernel,
        out_shape=(jax.ShapeDtypeStruct((m, d), F32), jax.ShapeDtypeStruct((ms, d), F32)),
        grid=(m // tm, f // tf),
        in_specs=[
            pl.BlockSpec((tm, d), lambda i, j: (i, 0)),
            pl.BlockSpec((ms, d), const),
            pl.BlockSpec((1, d), const),
            pl.BlockSpec((d, tf), lambda i, j: (0, j)),
            pl.BlockSpec((tf, d), lambda i, j: (j, 0)),
        ],
        out_specs=(pl.BlockSpec((tm, d), lambda i, j: (i, 0)), pl.BlockSpec((ms, d), const)),
        scratch_shapes=[pltpu.VMEM((tm, d), BF16), pltpu.VMEM((ms, d), BF16)],
        compiler_params=_params("arbitrary", "arbitrary"),
        name="mlp",
    )(x, xs, g.reshape(1, d), w_up, w_down)


def _ple_kernel(x_ref, p_ref, xs_ref, ps_ref, g_ref, wg_ref, wp_ref, gf_ref, o_ref, os_ref,
                wgb_ref, wpb_ref, *, final):
    @pl.when(pl.program_id(0) == 0)
    def _():
        wgb_ref[...] = wg_ref[...].astype(BF16)
        wpb_ref[...] = wp_ref[...].astype(BF16)

    def ple(x, p):
        gate = jax.nn.sigmoid(_dot(_rms(x, g_ref[...]).astype(BF16), wgb_ref[...]))
        pe = _dot(p.astype(BF16), wpb_ref[...])
        x = x + pe * gate
        return _rms(x, gf_ref[...]) if final else x

    for rows in _row_parts(x_ref.shape[0], 2):
        o_ref[rows, :] = ple(x_ref[rows, :], p_ref[rows, :])

    @_on_last_row_tile
    def _():
        os_ref[...] = ple(xs_ref[...], ps_ref[...])


def _ple(x, p, xs, ps, g, w_gate, w_ple, g_final, tm, final):
    m, d = x.shape
    ms = xs.shape[0]
    dp = p.shape[1]
    once = pl.Buffered(1)
    const = lambda i: (0, 0)
    tile = lambda i: (i, 0)
    return pl.pallas_call(
        functools.partial(_ple_kernel, final=final),
        out_shape=(jax.ShapeDtypeStruct((m, d), F32), jax.ShapeDtypeStruct((ms, d), F32)),
        grid=(m // tm,),
        in_specs=[
            pl.BlockSpec((tm, d), tile),
            pl.BlockSpec((tm, dp), tile),
            pl.BlockSpec((ms, d), const),
            pl.BlockSpec((ms, dp), const),
            pl.BlockSpec((1, d), const),
            pl.BlockSpec((d, d), const, pipeline_mode=once),
            pl.BlockSpec((dp, d), const, pipeline_mode=once),
            pl.BlockSpec((1, d), const),
        ],
        out_specs=(pl.BlockSpec((tm, d), tile), pl.BlockSpec((ms, d), const)),
        scratch_shapes=[pltpu.VMEM((d, d), BF16), pltpu.VMEM((dp, d), BF16)],
        compiler_params=_params("arbitrary"),
        name="ple_final",
    )(x, p, xs, ps, g.reshape(1, d), w_gate, w_ple, g_final.reshape(1, d))


TM_IN, TN_IN = 1024, 1024
TM_GLU = 1024
TM_OUT = 512
TM_MLP, TF_MLP = 512, 1024
TM_PLE = 256


def kernel(x_prompt, x_sample, state_s5_re, state_s5_im, state_lru, state_conv, p_prompt, p_sample,
           g_mix, w_in, s5_lam_re, s5_lam_im, s5_log_step, s5_b_re, s5_b_im, s5_c_re, s5_c_im, s5_d,
           s5_w_glu, s5_b_glu, conv_w, conv_b, lru_w_a, lru_b_a, lru_w_i, lru_b_i, lru_lam,
           g_merge_a, g_merge_b, w_out, g_mlp, w_up, w_down, g_ple, w_ple_gate, w_ple, g_final):
    depth = g_mix.shape[0]
    nb, t_len, d_model = x_prompt.shape
    ns = x_sample.shape[0]
    d_s5 = s5_d.shape[1]
    d_lru = conv_w.shape[2]
    groups = d_s5 // S5_CH

    xp = x_prompt.reshape(nb * t_len, d_model)
    xs = x_sample.reshape(ns, d_model)
    outs = [[] for _ in range(8)]
    for l in range(depth):
        final = l == depth - 1
        w_glu_b = s5_w_glu[l].astype(BF16)
        lru_args = (conv_w[l], conv_b[l], lru_w_a[l].astype(BF16), lru_b_a[l],
                    lru_w_i[l].astype(BF16), lru_b_i[l], lru_lam[l], g_merge_b[l])
        w_t, w_s, w_c, tab, wb, wct, arow, w_in_b = _s5_params(
            s5_lam_re[l], s5_lam_im[l], s5_log_step[l], s5_b_re[l], s5_b_im[l],
            s5_c_re[l], s5_c_im[l], s5_d[l], w_in[l])

        z, zs = _norm_matmul(xp, xs, g_mix[l], w_in_b, TM_IN, TN_IN)
        z4 = z.reshape(nb, t_len, 3 * d_s5)
        y, hf, w_down_b = _s5_seq(z4, w_t, w_s, w_c, tab, d_s5, w_down[l])
        mb, lru_h, w_up_b = _lru_seq(z4, *lru_args, w_up[l])
        ys, hsr, hsi = _s5_step(zs, state_s5_re[l].reshape(ns, groups * S5_STATE),
                                state_s5_im[l].reshape(ns, groups * S5_STATE),
                                wb, wct, arow, s5_d[l].reshape(1, d_s5))
        mbs, lru_hs = _lru_step(zs, state_conv[l], state_lru[l], *lru_args)
        ma, mas = _glu(y.reshape(nb * t_len, d_s5), ys, w_glu_b, s5_b_glu[l], g_merge_a[l], TM_GLU)
        xp, xs = _out_proj(xp, ma, mb.reshape(nb * t_len, d_lru), xs, mas, mbs, w_out[l], TM_OUT)
        xp, xs = _mlp(xp, xs, g_mlp[l], w_up_b, w_down_b, TM_MLP, TF_MLP)
        xp, xs = _ple(xp, p_prompt[l].reshape(nb * t_len, -1), xs, p_sample[l].reshape(ns, -1),
                      g_ple[l], w_ple_gate[l], w_ple[l], g_final, TM_PLE, final)
        hf = jnp.transpose(hf[:, :, :, :nb], (1, 3, 0, 2))
        outs[0].append(hf[0])
        outs[1].append(hf[1])
        outs[2].append(lru_h.reshape(nb, d_lru))
        outs[3].append(z4[:, t_len - (CONV_W - 1):, d_s5:d_s5 + d_lru])
        outs[4].append(hsr.reshape(ns, groups, S5_STATE))
        outs[5].append(hsi.reshape(ns, groups, S5_STATE))
        outs[6].append(lru_hs)
        outs[7].append(jnp.concatenate(
            [state_conv[l][:, 1:], zs[:, None, d_s5:d_s5 + d_lru]], axis=1))
    return (xp.reshape(nb, t_len, d_model), xs.reshape(ns, 1, d_model),
            *(jnp.stack(o) for o in outs))
```

```python
import functools

import jax
import jax.numpy as jnp
from jax import lax
from jax.experimental import pallas as pl
from jax.experimental.pallas import tpu as pltpu

F32 = jnp.float32
BF16 = jnp.bfloat16
HIGHEST = lax.Precision.HIGHEST

EPS = 1e-6
LRU_C = 8.0
S5_CH = 16
S5_STATE = 64
LRU_HEADS = 4
CONV_W = 4
CHUNK = 16
SUBLANES = 8
LANES = 128
LANE_GROUPS = LANES // S5_CH
LANE_PAIRS = LANE_GROUPS // 2
ROWS = CHUNK * S5_CH
POW_ROWS = 32
SCAN_ROW0 = POW_ROWS
VMEM_LIMIT = 56 * 1024 * 1024


def _params(*sem):
    return pltpu.CompilerParams(dimension_semantics=sem, vmem_limit_bytes=VMEM_LIMIT)


def _rms(x, g):
    return x * lax.rsqrt(jnp.mean(x * x, axis=-1, keepdims=True) + EPS) * g


def _dot(a, b):
    return jnp.dot(a, b, preferred_element_type=F32)


def _dot_nt(a, b, precision=None):
    return lax.dot_general(a, b, (((1,), (1,)), ((), ())), precision=precision,
                           preferred_element_type=F32)


def _transpose_tiles(x):
    r, c = x.shape
    return jnp.concatenate(
        [jnp.concatenate([x[i:i + LANES, j:j + LANES].T for i in range(0, r, LANES)], axis=1)
         for j in range(0, c, LANES)], axis=0)


def _row_parts(tm, want):
    parts = want if tm % (want * 16) == 0 else 1
    step = tm // parts
    return [slice(r * step, (r + 1) * step) for r in range(parts)]


def _on_last_row_tile(fn):
    pl.when(pl.program_id(0) == pl.num_programs(0) - 1)(fn)


def _norm_matmul_kernel(x_ref, xs_ref, g_ref, w_ref, *rest):
    n_cast = (len(rest) - 2) // 2
    cast_refs, (o_ref, os_ref), castb_refs = rest[:n_cast], rest[n_cast:n_cast + 2], rest[n_cast + 2:]

    @pl.when(pl.program_id(1) == 0)
    def _():
        for src, dst in zip(cast_refs, castb_refs):
            dst[...] = src[...].astype(BF16)

    for rows in _row_parts(x_ref.shape[0], 4):
        h = _rms(x_ref[rows, :], g_ref[...]).astype(BF16)
        o_ref[rows, :] = _dot(h, w_ref[...])

    @_on_last_row_tile
    def _():
        os_ref[...] = _dot(_rms(xs_ref[...], g_ref[...]).astype(BF16), w_ref[...])


def _norm_matmul(x, xs, g, w, cast_ws, tm, tn):
    m, k = x.shape
    ms = xs.shape[0]
    n = w.shape[1]
    ni = m // tm
    cast_blks = [pl.BlockSpec((c.shape[0] // ni, c.shape[1]), lambda i, j: (i, 0)) for c in cast_ws]
    outs = pl.pallas_call(
        _norm_matmul_kernel,
        out_shape=(jax.ShapeDtypeStruct((m, n), F32), jax.ShapeDtypeStruct((ms, n), F32),
                   *(jax.ShapeDtypeStruct(c.shape, BF16) for c in cast_ws)),
        grid=(ni, n // tn),
        in_specs=[
            pl.BlockSpec((tm, k), lambda i, j: (i, 0)),
            pl.BlockSpec((ms, k), lambda i, j: (0, 0)),
            pl.BlockSpec((1, k), lambda i, j: (0, 0)),
            pl.BlockSpec((k, tn), lambda i, j: (0, j)),
            *cast_blks,
        ],
        out_specs=(pl.BlockSpec((tm, tn), lambda i, j: (i, j)),
                   pl.BlockSpec((ms, tn), lambda i, j: (0, jnp.where(i == ni - 1, j, 0))),
                   *cast_blks),
        compiler_params=_params("arbitrary", "arbitrary"),
        name="norm_w_in",
    )(x, xs, g.reshape(1, k), w, *cast_ws)
    return outs[0], outs[1], outs[2:]


def _s5_param_kernel(lr_ref, li_ref, ls_ref, bt_r_ref, bt_i_ref, c_r_ref, c_i_ref, d_ref, cast_ref,
                     wt_ref, ws_ref, wc_ref, tab_ref, wb_ref, wct_ref, arow_ref, castb_ref):
    castb_ref[...] = cast_ref[...].astype(BF16)
    row_p = lax.broadcasted_iota(jnp.int32, (POW_ROWS, LANES), 0)
    lane = lax.broadcasted_iota(jnp.int32, (ROWS, LANES), 1)
    lane16 = lax.broadcasted_iota(jnp.int32, (S5_CH, LANES), 1)
    row16 = lax.broadcasted_iota(jnp.int32, (S5_CH, LANES), 0)
    lo = lane < S5_STATE

    wb_ref[...] = jnp.zeros_like(wb_ref)
    wct_ref[...] = jnp.zeros_like(wct_ref)

    for q in range(LANE_PAIRS):
        lr, li = lr_ref[q], li_ref[q]
        step = jnp.exp(ls_ref[q])
        mag = jnp.exp(lr * step)
        ar = mag * jnp.cos(li * step)
        ai = mag * jnp.sin(li * step)
        nr, ni = ar - 1.0, ai
        den = lr * lr + li * li
        cr = (nr * lr + ni * li) / den
        ci = (ni * lr - nr * li) / den
        bt_r, bt_i = bt_r_ref[q], bt_i_ref[q]
        bb_r = cr * bt_r - ci * bt_i
        bb_i = cr * bt_i + ci * bt_r
        c_r, c_i = c_r_ref[q], c_i_ref[q]

        pr = jnp.ones((POW_ROWS, LANES), F32)
        pi = jnp.zeros((POW_ROWS, LANES), F32)
        sr, si = ar, ai
        sq = []
        for m in range(POW_ROWS.bit_length() - 1 + SUBLANES):
            sq.append((sr, si))
            if (1 << m) < POW_ROWS:
                bit = ((row_p >> m) & 1) == 1
                pr, pi = (jnp.where(bit, pr * sr - pi * si, pr),
                          jnp.where(bit, pr * si + pi * sr, pi))
            sr, si = sr * sr - si * si, 2.0 * sr * si
        m0 = CHUNK.bit_length() - 1
        scan_r = jnp.concatenate([sq[m0 + j][0] for j in range(SUBLANES)], axis=0)
        scan_i = jnp.concatenate([sq[m0 + j][1] for j in range(SUBLANES)], axis=0)
        pad = jnp.zeros((LANES - POW_ROWS - SUBLANES, LANES), F32)
        tab_ref[q, 0] = jnp.concatenate([pr, scan_r, pad], axis=0).T
        tab_ref[q, 1] = jnp.concatenate([pi, scan_i, pad], axis=0).T
        arow_ref[q, 0] = pr[:SUBLANES]
        arow_ref[q, 1] = pi[:SUBLANES]

        def expand(tab, k0, sign):
            return jnp.concatenate(
                [jnp.broadcast_to(tab[k0 + sign * s:k0 + sign * s + 1, :], (S5_CH, LANES))
                 for s in range(CHUNK)], axis=0)

        tile = lambda v: jnp.concatenate([v] * CHUNK, axis=0)

        e_r, e_i = expand(pr, CHUNK - 1, -1), expand(pi, CHUNK - 1, -1)
        t_r, t_i = tile(bb_r), tile(bb_i)
        ws_r = _transpose_tiles(t_r * e_r - t_i * e_i)
        ws_i = _transpose_tiles(t_r * e_i + t_i * e_r)
        for h in range(2):
            rows = slice(h * S5_STATE, (h + 1) * S5_STATE)
            ws_ref[2 * q + h] = jnp.concatenate([ws_r[rows], ws_i[rows]], axis=0).astype(BF16)

        e_r, e_i = expand(pr, 1, 1), expand(pi, 1, 1)
        t_r, t_i = tile(c_r), tile(c_i)
        ca_r = t_r * e_r - t_i * e_i
        ca_n = -(t_r * e_i + t_i * e_r)
        wc_ref[2 * q] = jnp.where(lo, ca_r, pltpu.roll(ca_n, S5_STATE, 1)).astype(BF16)
        wc_ref[2 * q + 1] = jnp.where(lo, pltpu.roll(ca_r, S5_STATE, 1), ca_n).astype(BF16)

        e_r, e_i = expand(pr, 0, 1), expand(pi, 0, 1)
        c0_r = t_r * e_r - t_i * e_i
        c0_i = t_r * e_i + t_i * e_r

        for h in range(2):
            g = 2 * q + h
            mine = (lane16 < S5_STATE) == (h == 0)
            bm_r = jnp.where(mine, bb_r, 0.0)
            bm_i = jnp.where(mine, bb_i, 0.0)
            kt = _dot_nt(bm_r, c0_r, HIGHEST) - _dot_nt(bm_i, c0_i, HIGHEST)
            k_lo = kt[:, :LANES] + jnp.where(row16 == lane16, d_ref[g], 0.0)
            k_hi = kt[:, LANES:]
            blocks = []
            half = LANES // S5_CH
            for s in range(CHUNK):
                sh = (s % half) * S5_CH
                keep = lane16 >= sh
                r_lo = pltpu.roll(k_lo, sh, 1) if sh else k_lo
                r_hi = pltpu.roll(k_hi, sh, 1) if sh else k_hi
                if s < half:
                    blk = jnp.concatenate([jnp.where(keep, r_lo, 0.0),
                                           jnp.where(keep, r_hi, r_lo)], axis=1)
                else:
                    blk = jnp.concatenate([jnp.zeros_like(r_lo), jnp.where(keep, r_lo, 0.0)], axis=1)
                blocks.append(blk)
            wt_ref[g] = _transpose_tiles(jnp.concatenate(blocks, axis=0)).astype(BF16)

            r0 = (2 * q + h) * S5_CH
            c0 = q * 2 * LANES
            wb_ref[r0:r0 + S5_CH, c0:c0 + LANES] = bm_r.astype(BF16)
            wb_ref[r0:r0 + S5_CH, c0 + LANES:c0 + 2 * LANES] = bm_i.astype(BF16)
            wct_ref[q, r0:r0 + S5_CH, :LANES] = jnp.where(mine, c_r, 0.0).astype(BF16)
            wct_ref[q, r0:r0 + S5_CH, LANES:] = jnp.where(mine, -c_i, 0.0).astype(BF16)


def _s5_params(lam_re, lam_im, log_step, b_re, b_im, c_re, c_im, d, cast_w):
    groups = lam_re.shape[0]
    pairs = groups // 2
    slabs = groups // LANE_GROUPS
    cast_blk = pl.BlockSpec((cast_w.shape[0] // slabs, cast_w.shape[1]), lambda i: (i, 0))
    pair_row = lambda v: v.reshape(pairs, 1, LANES)
    pair_ch = lambda v: jnp.transpose(v.reshape(pairs, 2, S5_CH, S5_STATE), (0, 2, 1, 3)).reshape(
        pairs, S5_CH, LANES)
    ls = jnp.broadcast_to(log_step[:, None], (groups, S5_STATE))
    d_pad = jnp.pad(d.reshape(groups, 1, S5_CH), ((0, 0), (0, 0), (0, LANES - S5_CH)))
    blk3 = lambda i: (i, 0, 0)
    blk4 = lambda i: (i, 0, 0, 0)
    prow = pl.BlockSpec((LANE_PAIRS, 1, LANES), blk3)
    pch = pl.BlockSpec((LANE_PAIRS, S5_CH, LANES), blk3)
    return pl.pallas_call(
        _s5_param_kernel,
        out_shape=(jax.ShapeDtypeStruct((groups, ROWS, ROWS), BF16),
                   jax.ShapeDtypeStruct((groups, 2 * S5_STATE, ROWS), BF16),
                   jax.ShapeDtypeStruct((groups, ROWS, 2 * S5_STATE), BF16),
                   jax.ShapeDtypeStruct((pairs, 2, LANES, LANES), F32),
                   jax.ShapeDtypeStruct((slabs, LANES, LANE_PAIRS * 2 * LANES), BF16),
                   jax.ShapeDtypeStruct((slabs, LANE_PAIRS, LANES, 2 * LANES), BF16),
                   jax.ShapeDtypeStruct((pairs, 2, SUBLANES, LANES), F32),
                   jax.ShapeDtypeStruct(cast_w.shape, BF16)),
        grid=(slabs,),
        in_specs=[prow, prow, prow, pch, pch, pch, pch,
                  pl.BlockSpec((LANE_GROUPS, 1, LANES), blk3), cast_blk],
        out_specs=(pl.BlockSpec((LANE_GROUPS, ROWS, ROWS), blk3),
                   pl.BlockSpec((LANE_GROUPS, 2 * S5_STATE, ROWS), blk3),
                   pl.BlockSpec((LANE_GROUPS, ROWS, 2 * S5_STATE), blk3),
                   pl.BlockSpec((LANE_PAIRS, 2, LANES, LANES), blk4),
                   pl.BlockSpec((None, LANES, LANE_PAIRS * 2 * LANES), blk3),
                   pl.BlockSpec((None, LANE_PAIRS, LANES, 2 * LANES), blk4),
                   pl.BlockSpec((LANE_PAIRS, 2, SUBLANES, LANES), blk4),
                   cast_blk),
        compiler_params=_params("parallel"),
        name="s5_params",
    )(pair_row(lam_re), pair_row(lam_im), pair_row(ls),
      pair_ch(jnp.transpose(b_re, (0, 2, 1))), pair_ch(jnp.transpose(b_im, (0, 2, 1))),
      pair_ch(c_re), pair_ch(c_im), d_pad, cast_w)


def _s5_seq_kernel(z_ref, wt_ref, ws_ref, wc_ref, tab_ref, cast_ref, y_ref, hf_ref, castb_ref,
                   ut_ref, yt_ref, sr_ref, si_ref, *, nb, n_chunks):
    castb_ref[...] = cast_ref[...].astype(BF16)
    scan_steps = n_chunks.bit_length() - 1
    for n in range(nb):
        for s in range(CHUNK):
            xs = z_ref[n, pl.ds(s, n_chunks, stride=CHUNK), :]
            ut_ref[:, s * S5_CH:(s + 1) * S5_CH, n * n_chunks:(n + 1) * n_chunks] = (
                xs.T.astype(BF16).reshape(LANE_GROUPS, S5_CH, n_chunks))

    for g in range(LANE_GROUPS):
        st = _dot(ws_ref[g], ut_ref[g])
        sr_ref[g * S5_STATE:(g + 1) * S5_STATE, :] = st[:S5_STATE]
        si_ref[g * S5_STATE:(g + 1) * S5_STATE, :] = st[S5_STATE:]

    srows = LANE_GROUPS * S5_STATE
    lane = lax.broadcasted_iota(jnp.int32, (srows, n_chunks), 1)
    tab_r = jnp.concatenate([tab_ref[q, 0] for q in range(LANE_PAIRS)], axis=0)
    tab_i = jnp.concatenate([tab_ref[q, 1] for q in range(LANE_PAIRS)], axis=0)
    fin_r = jnp.zeros((srows, n_chunks), F32)
    fin_i = jnp.zeros((srows, n_chunks), F32)
    for n in range(nb):
        cols = slice(n * n_chunks, (n + 1) * n_chunks)
        xr = sr_ref[:, cols]
        xi = si_ref[:, cols]
        for k in range(scan_steps):
            d = 1 << k
            keep = lane >= d
            sr = jnp.where(keep, pltpu.roll(xr, d, 1), 0.0)
            si = jnp.where(keep, pltpu.roll(xi, d, 1), 0.0)
            pr = tab_r[:, SCAN_ROW0 + k:SCAN_ROW0 + k + 1]
            pi = tab_i[:, SCAN_ROW0 + k:SCAN_ROW0 + k + 1]
            xr, xi = xr + (pr * sr - pi * si), xi + (pr * si + pi * sr)
        last = lane == n_chunks - 1
        fin_r = fin_r + jnp.where(
            lane == n, jnp.sum(jnp.where(last, xr, 0.0), axis=1, keepdims=True), 0.0)
        fin_i = fin_i + jnp.where(
            lane == n, jnp.sum(jnp.where(last, xi, 0.0), axis=1, keepdims=True), 0.0)
        keep = lane >= 1
        sr_ref[:, cols] = jnp.where(keep, pltpu.roll(xr, 1, 1), 0.0)
        si_ref[:, cols] = jnp.where(keep, pltpu.roll(xi, 1, 1), 0.0)

    for g in range(LANE_GROUPS):
        rows = slice(g * S5_STATE, (g + 1) * S5_STATE)
        hf_ref[g, 0] = fin_r[rows]
        hf_ref[g, 1] = fin_i[rows]
        hs = jnp.concatenate([sr_ref[rows, :], si_ref[rows, :]], axis=0).astype(BF16)
        yt_ref[g] = jax.nn.gelu(_dot(wc_ref[g], hs) + _dot(wt_ref[g], ut_ref[g]))

    for n in range(nb):
        for s in range(CHUNK):
            blk = yt_ref[:, s * S5_CH:(s + 1) * S5_CH, n * n_chunks:(n + 1) * n_chunks]
            y_ref[n, pl.ds(s, n_chunks, stride=CHUNK), :] = blk.reshape(LANES, n_chunks).T


def _s5_seq(z, w_t, w_s, w_c, tab, d_s5, cast_w):
    nb, t_len, _ = z.shape
    n_chunks = t_len // CHUNK
    assert n_chunks == LANES, "the chunk axis must fill one 128-lane tile"
    groups = d_s5 // S5_CH
    steps = groups // LANE_GROUPS
    cast_blk = pl.BlockSpec((cast_w.shape[0] // steps, cast_w.shape[1]), lambda i: (i, 0))
    kern = functools.partial(_s5_seq_kernel, nb=nb, n_chunks=n_chunks)
    blk3 = lambda i: (i, 0, 0)
    blk4 = lambda i: (i, 0, 0, 0)
    return pl.pallas_call(
        kern,
        out_shape=(jax.ShapeDtypeStruct((nb, t_len, d_s5), F32),
                   jax.ShapeDtypeStruct((groups, 2, S5_STATE, LANES), F32),
                   jax.ShapeDtypeStruct(cast_w.shape, BF16)),
        grid=(steps,),
        in_specs=[
            pl.BlockSpec((nb, t_len, LANES), lambda i: (0, 0, i)),
            pl.BlockSpec((LANE_GROUPS, ROWS, ROWS), blk3),
            pl.BlockSpec((LANE_GROUPS, 2 * S5_STATE, ROWS), blk3),
            pl.BlockSpec((LANE_GROUPS, ROWS, 2 * S5_STATE), blk3),
            pl.BlockSpec((LANE_PAIRS, 2, LANES, LANES), blk4),
            cast_blk,
        ],
        out_specs=(pl.BlockSpec((nb, t_len, LANES), lambda i: (0, 0, i)),
                   pl.BlockSpec((LANE_GROUPS, 2, S5_STATE, LANES), blk4),
                   cast_blk),
        scratch_shapes=[pltpu.VMEM((LANE_GROUPS, ROWS, nb * n_chunks), BF16),
                        pltpu.VMEM((LANE_GROUPS, ROWS, nb * n_chunks), F32),
                        pltpu.VMEM((LANE_GROUPS * S5_STATE, nb * n_chunks), F32),
                        pltpu.VMEM((LANE_GROUPS * S5_STATE, nb * n_chunks), F32)],
        compiler_params=_params("parallel"),
        name="s5_seq",
    )(z, w_t, w_s, w_c, tab, cast_w)


def _s5_step_kernel(u_ref, h0r_ref, h0i_ref, wb_ref, wct_ref, arow_ref, d_ref,
                    y_ref, hr_ref, hi_ref):
    u = u_ref[...]
    bu = _dot(u.astype(BF16), wb_ref[...])
    y = d_ref[...] * u
    for q in range(LANE_PAIRS):
        cols = slice(q * LANES, (q + 1) * LANES)
        ar = arow_ref[q, 0, 1:2, :]
        ai = arow_ref[q, 1, 1:2, :]
        h0r, h0i = h0r_ref[:, cols], h0i_ref[:, cols]
        hr = ar * h0r - ai * h0i + bu[:, 2 * q * LANES:(2 * q + 1) * LANES]
        hi = ar * h0i + ai * h0r + bu[:, (2 * q + 1) * LANES:(2 * q + 2) * LANES]
        hr_ref[:, cols] = hr
        hi_ref[:, cols] = hi
        y = y + _dot_nt(jnp.concatenate([hr, hi], axis=1).astype(BF16), wct_ref[q])
    y_ref[...] = jax.nn.gelu(y).astype(BF16)


def _s5_step(z, h0r, h0i, wb, wct, arow, d):
    n = z.shape[0]
    slabs = wb.shape[0]
    sw = LANE_PAIRS * LANES
    col = lambda i: (0, i)
    blk3 = lambda i: (i, 0, 0)
    blk4 = lambda i: (i, 0, 0, 0)
    return pl.pallas_call(
        _s5_step_kernel,
        out_shape=(jax.ShapeDtypeStruct((n, slabs * LANES), BF16),
                   jax.ShapeDtypeStruct((n, slabs * sw), F32),
                   jax.ShapeDtypeStruct((n, slabs * sw), F32)),
        grid=(slabs,),
        in_specs=[
            pl.BlockSpec((n, LANES), col),
            pl.BlockSpec((n, sw), col),
            pl.BlockSpec((n, sw), col),
            pl.BlockSpec((None, LANES, 2 * sw), blk3),
            pl.BlockSpec((None, LANE_PAIRS, LANES, 2 * LANES), blk4),
            pl.BlockSpec((LANE_PAIRS, 2, SUBLANES, LANES), blk4),
            pl.BlockSpec((1, LANES), col),
        ],
        out_specs=(pl.BlockSpec((n, LANES), col),
                   pl.BlockSpec((n, sw), col),
                   pl.BlockSpec((n, sw), col)),
        compiler_params=_params("parallel"),
        name="s5_step",
    )(z, h0r, h0i, wb, wct, arow, d)


def _lru_gate_block(xc, wa, ba, wi, bi, lam):
    xb16 = xc.astype(BF16)
    r = jax.nn.sigmoid(_dot(xb16, wa) + ba)
    ig = jax.nn.sigmoid(_dot(xb16, wi) + bi)
    log_a = -LRU_C * r * jax.nn.softplus(-lam)
    a = jnp.exp(log_a)
    mult = jnp.sqrt(1.0 - a * a)
    return a, mult * (ig * xc)


def _lru_gates(xc, wa_ref, ba, wi_ref, bi, lam):
    blk = xc.shape[1] // LRU_HEADS
    parts = [_lru_gate_block(xc[:, h * blk:(h + 1) * blk], wa_ref[h], ba[:, h * blk:(h + 1) * blk],
                             wi_ref[h], bi[:, h * blk:(h + 1) * blk],
                             lam[:, h * blk:(h + 1) * blk]) for h in range(LRU_HEADS)]
    return (jnp.concatenate([p[0] for p in parts], axis=-1),
            jnp.concatenate([p[1] for p in parts], axis=-1))


def _lru_seq_kernel(xb_ref, gb_ref, cw_ref, cb_ref, wa_ref, ba_ref, wi_ref, bi_ref, lam_ref,
                    gm_ref, cast_ref, o_ref, hl_ref, castb_ref, xe_ref, a_ref, b_ref, hc_ref,
                    *, tt):
    castb_ref[...] = cast_ref[...].astype(BF16)
    halo = SUBLANES
    nseq, _, d = xb_ref.shape
    blk = d // LRU_HEADS

    @pl.when(pl.program_id(0) == 0)
    def _():
        xe_ref[:, 0:halo, :] = jnp.zeros((nseq, halo, d), F32)
        hc_ref[...] = jnp.zeros_like(hc_ref)

    xe_ref[:, halo:halo + tt, :] = xb_ref[...]
    for h in range(LRU_HEADS):
        cols = slice(h * blk, (h + 1) * blk)
        cw = cw_ref[:, cols]
        xc = cb_ref[:, cols] + xe_ref[:, halo:halo + tt, cols] * cw[CONV_W - 1:CONV_W, :]
        for k in range(1, CONV_W):
            xc = xc + xe_ref[:, halo - k:halo - k + tt, cols] * cw[CONV_W - 1 - k:CONV_W - k, :]
        a, b = _lru_gate_block(xc.reshape(nseq * tt, blk), wa_ref[h], ba_ref[:, cols],
                               wi_ref[h], bi_ref[:, cols], lam_ref[:, cols])
        a_ref[:, :, cols] = a.reshape(nseq, tt, blk)
        b_ref[:, :, cols] = b.reshape(nseq, tt, blk)
    xe_ref[:, 0:halo, :] = xb_ref[:, tt - halo:tt, :]

    def block(i, h):
        base = pl.multiple_of(i * SUBLANES, SUBLANES)
        for j in range(SUBLANES):
            row = pl.ds(base + j, 1)
            h = a_ref[:, row, :] * h + b_ref[:, row, :]
            b_ref[:, row, :] = h
        return h

    h = lax.fori_loop(0, tt // SUBLANES, block, hc_ref[...])
    hc_ref[...] = h
    hl_ref[...] = h

    out = b_ref[...] * jax.nn.gelu(gb_ref[...])
    o_ref[...] = _rms(out, gm_ref[...]).astype(BF16)


def _lru_seq(z, conv_w, conv_b, w_a, b_a, w_i, b_i, lam, g_merge, cast_w, tt=64):
    nseq, t_len, _ = z.shape
    d = conv_w.shape[1]
    steps = t_len // tt
    cast_blk = pl.BlockSpec((cast_w.shape[0] // steps, cast_w.shape[1]), lambda t: (t, 0))
    row = lambda v: v.reshape(1, d)
    const2 = lambda t: (0, 0)
    const3 = lambda t: (0, 0, 0)
    kern = functools.partial(_lru_seq_kernel, tt=tt)
    return pl.pallas_call(
        kern,
        out_shape=(jax.ShapeDtypeStruct((nseq, t_len, d), BF16),
                   jax.ShapeDtypeStruct((nseq, 1, d), F32),
                   jax.ShapeDtypeStruct(cast_w.shape, BF16)),
        grid=(steps,),
        in_specs=[
            pl.BlockSpec((nseq, tt, d), lambda t: (0, t, 1)),
            pl.BlockSpec((nseq, tt, d), lambda t: (0, t, 2)),
            pl.BlockSpec((CONV_W, d), const2),
            pl.BlockSpec((1, d), const2),
            pl.BlockSpec(w_a.shape, const3),
            pl.BlockSpec((1, d), const2),
            pl.BlockSpec(w_i.shape, const3),
            pl.BlockSpec((1, d), const2),
            pl.BlockSpec((1, d), const2),
            pl.BlockSpec((1, d), const2),
            cast_blk,
        ],
        out_specs=(pl.BlockSpec((nseq, tt, d), lambda t: (0, t, 0)),
                   pl.BlockSpec((nseq, 1, d), const3),
                   cast_blk),
        scratch_shapes=[pltpu.VMEM((nseq, tt + SUBLANES, d), F32),
                        pltpu.VMEM((nseq, tt, d), F32),
                        pltpu.VMEM((nseq, tt, d), F32),
                        pltpu.VMEM((nseq, 1, d), F32)],
        compiler_params=_params("arbitrary"),
        name="rglru_seq",
    )(z, z, conv_w, row(conv_b), w_a, row(b_a), w_i, row(b_i), row(lam), row(g_merge), cast_w)


def _lru_step_kernel(xb_ref, gb_ref, c0_ref, c1_ref, c2_ref, h0_ref, cw_ref, cb_ref,
                     wa_ref, ba_ref, wi_ref, bi_ref, lam_ref, gm_ref, o_ref, h_ref):
    xb = xb_ref[...]
    cw = cw_ref[...]
    xc = (cb_ref[...] + c0_ref[...] * cw[0:1, :] + c1_ref[...] * cw[1:2, :]
          + c2_ref[...] * cw[2:3, :] + xb * cw[3:4, :])
    a, b = _lru_gates(xc, wa_ref, ba_ref[...], wi_ref, bi_ref[...], lam_ref[...])
    h = a * h0_ref[...] + b
    h_ref[...] = h
    out = h * jax.nn.gelu(gb_ref[...])
    o_ref[...] = _rms(out, gm_ref[...]).astype(BF16)


def _lru_step(z, conv0, h0, conv_w, conv_b, w_a, b_a, w_i, b_i, lam, g_merge):
    n = z.shape[0]
    d = conv_w.shape[1]
    row = lambda v: v.reshape(1, d)
    full = lambda shape: pl.BlockSpec(shape, lambda i: (0,) * len(shape))
    nd = full((n, d))
    rd = full((1, d))
    return pl.pallas_call(
        _lru_step_kernel,
        out_shape=(jax.ShapeDtypeStruct((n, d), BF16), jax.ShapeDtypeStruct((n, d), F32)),
        grid=(1,),
        in_specs=[
            pl.BlockSpec((n, d), lambda i: (0, 1)),
            pl.BlockSpec((n, d), lambda i: (0, 2)),
            nd, nd, nd, nd,
            full((CONV_W, d)), rd,
            full(w_a.shape), rd, full(w_i.shape), rd, rd, rd,
        ],
        out_specs=(nd, nd),
        compiler_params=_params("arbitrary"),
        name="rglru_step",
    )(z, z, conv0[:, 0], conv0[:, 1], conv0[:, 2], h0, conv_w, row(conv_b),
      w_a, row(b_a), w_i, row(b_i), row(lam), row(g_merge))


def _glu_rows(y, w_ref, b_ref, g_ref):
    gate = jax.nn.sigmoid(_dot(y.astype(BF16), w_ref[...]) + b_ref[...])
    return _rms(y.astype(F32) * gate, g_ref[...]).astype(BF16)


def _out_proj_kernel(x_ref, y_ref, mb_ref, xs_ref, ys_ref, mbs_ref, wg_ref, bg_ref, ga_ref, w_ref,
                     o_ref, os_ref):
    ka = y_ref.shape[1]

    def proj(x, y, mb):
        ma = _glu_rows(y, wg_ref, bg_ref, ga_ref)
        return x + _dot(ma, w_ref[:ka, :]) + _dot(mb, w_ref[ka:, :])

    for rows in _row_parts(x_ref.shape[0], 4):
        o_ref[rows, :] = proj(x_ref[rows, :], y_ref[rows, :], mb_ref[rows, :])

    @_on_last_row_tile
    def _():
        os_ref[...] = proj(xs_ref[...], ys_ref[...], mbs_ref[...])


def _out_proj(x, y, mb, xs, ys, mbs, w_glu, b_glu, g_a, w, tm):
    m, n = x.shape
    ms = xs.shape[0]
    ka = y.shape[1]
    kb = mb.shape[1]
    const = lambda i: (0, 0)
    tile = lambda i: (i, 0)
    once = pl.Buffered(1)
    return pl.pallas_call(
        _out_proj_kernel,
        out_shape=(jax.ShapeDtypeStruct((m, n), F32), jax.ShapeDtypeStruct((ms, n), F32)),
        grid=(m // tm,),
        in_specs=[
            pl.BlockSpec((tm, n), tile),
            pl.BlockSpec((tm, ka), tile),
            pl.BlockSpec((tm, kb), tile),
            pl.BlockSpec((ms, n), const),
            pl.BlockSpec((ms, ka), const),
            pl.BlockSpec((ms, kb), const),
            pl.BlockSpec((ka, ka), const, pipeline_mode=once),
            pl.BlockSpec((1, ka), const),
            pl.BlockSpec((1, ka), const),
            pl.BlockSpec((ka + kb, n), const, pipeline_mode=once),
        ],
        out_specs=(pl.BlockSpec((tm, n), tile), pl.BlockSpec((ms, n), const)),
        compiler_params=_params("arbitrary"),
        name="out_proj",
    )(x, y, mb, xs, ys, mbs, w_glu, b_glu.reshape(1, ka), g_a.reshape(1, ka), w)


def _mlp_kernel(x_ref, xs_ref, g_ref, wu_ref, wd_ref, o_ref, os_ref, h_ref, hs_ref):
    def rows(x_ref, o_ref, h_ref):
        @pl.when(pl.program_id(1) == 0)
        def _():
            x = x_ref[...]
            h_ref[...] = _rms(x, g_ref[...]).astype(BF16)
            o_ref[...] = x

        up = _dot(h_ref[...], wu_ref[...])
        act = jnp.square(jnp.maximum(up, 0.0)).astype(BF16)
        o_ref[...] += _dot(act, wd_ref[...])

    rows(x_ref, o_ref, h_ref)
    _on_last_row_tile(lambda: rows(xs_ref, os_ref, hs_ref))


def _mlp(x, xs, g, w_up, w_down, tm, tf):
    m, d = x.shape
    ms = xs.shape[0]
    f = w_up.shape[1]
    const = lambda i, j: (0, 0)
    return pl.pallas_call(
        _mlp_kernel,
        out_shape=(jax.ShapeDtypeStruct((m, d), F32), jax.ShapeDtypeStruct((ms, d), F32)),
        grid=(m // tm, f // tf),
        in_specs=[
            pl.BlockSpec((tm, d), lambda i, j: (i, 0)),
            pl.BlockSpec((ms, d), const),
            pl.BlockSpec((1, d), const),
            pl.BlockSpec((d, tf), lambda i, j: (0, j)),
            pl.BlockSpec((tf, d), lambda i, j: (j, 0)),
        ],
        out_specs=(pl.BlockSpec((tm, d), lambda i, j: (i, 0)), pl.BlockSpec((ms, d), const)),
        scratch_shapes=[pltpu.VMEM((tm, d), BF16), pltpu.VMEM((ms, d), BF16)],
        compiler_params=_params("arbitrary", "arbitrary"),
        name="mlp",
    )(x, xs, g.reshape(1, d), w_up, w_down)


def _ple_kernel(x_ref, p_ref, xs_ref, ps_ref, g_ref, wg_ref, wp_ref, gf_ref, o_ref, os_ref,
                *, final):
    def ple(x, p):
        gate = jax.nn.sigmoid(_dot(_rms(x, g_ref[...]).astype(BF16), wg_ref[...]))
        pe = _dot(p.astype(BF16), wp_ref[...])
        x = x + pe * gate
        return _rms(x, gf_ref[...]) if final else x

    for rows in _row_parts(x_ref.shape[0], 4):
        o_ref[rows, :] = ple(x_ref[rows, :], p_ref[rows, :])

    @_on_last_row_tile
    def _():
        os_ref[...] = ple(xs_ref[...], ps_ref[...])


def _ple(x, p, xs, ps, g, w_gate, w_ple, g_final, tm, final):
    m, d = x.shape
    ms = xs.shape[0]
    dp = p.shape[1]
    once = pl.Buffered(1)
    const = lambda i: (0, 0)
    tile = lambda i: (i, 0)
    return pl.pallas_call(
        functools.partial(_ple_kernel, final=final),
        out_shape=(jax.ShapeDtypeStruct((m, d), F32), jax.ShapeDtypeStruct((ms, d), F32)),
        grid=(m // tm,),
        in_specs=[
            pl.BlockSpec((tm, d), tile),
            pl.BlockSpec((tm, dp), tile),
            pl.BlockSpec((ms, d), const),
            pl.BlockSpec((ms, dp), const),
            pl.BlockSpec((1, d), const),
            pl.BlockSpec((d, d), const, pipeline_mode=once),
            pl.BlockSpec((dp, d), const, pipeline_mode=once),
            pl.BlockSpec((1, d), const),
        ],
        out_specs=(pl.BlockSpec((tm, d), tile), pl.BlockSpec((ms, d), const)),
        compiler_params=_params("arbitrary"),
        name="ple_final",
    )(x, p, xs, ps, g.reshape(1, d), w_gate, w_ple, g_final.reshape(1, d))


TM_IN, TN_IN = 1024, 1024
TM_OUT = 512
TM_MLP, TF_MLP = 512, 1024
TM_PLE = 512


def kernel(x_prompt, x_sample, state_s5_re, state_s5_im, state_lru, state_conv, p_prompt, p_sample,
           g_mix, w_in, s5_lam_re, s5_lam_im, s5_log_step, s5_b_re, s5_b_im, s5_c_re, s5_c_im, s5_d,
           s5_w_glu, s5_b_glu, conv_w, conv_b, lru_w_a, lru_b_a, lru_w_i, lru_b_i, lru_lam,
           g_merge_a, g_merge_b, w_out, g_mlp, w_up, w_down, g_ple, w_ple_gate, w_ple, g_final):
    depth = g_mix.shape[0]
    nb, t_len, d_model = x_prompt.shape
    ns = x_sample.shape[0]
    d_s5 = s5_d.shape[1]
    d_lru = conv_w.shape[2]
    groups = d_s5 // S5_CH

    xp = x_prompt.reshape(nb * t_len, d_model)
    xs = x_sample.reshape(ns, d_model)
    outs = [[] for _ in range(8)]
    for l in range(depth):
        final = l == depth - 1
        w_glu_b = s5_w_glu[l].astype(BF16)
        lru_args = (conv_w[l], conv_b[l], lru_w_a[l].astype(BF16), lru_b_a[l],
                    lru_w_i[l].astype(BF16), lru_b_i[l], lru_lam[l], g_merge_b[l])
        w_t, w_s, w_c, tab, wb, wct, arow, w_in_b = _s5_params(
            s5_lam_re[l], s5_lam_im[l], s5_log_step[l], s5_b_re[l], s5_b_im[l],
            s5_c_re[l], s5_c_im[l], s5_d[l], w_in[l])

        z, zs, (w_out_b, w_gate_b, w_ple_b) = _norm_matmul(
            xp, xs, g_mix[l], w_in_b, (w_out[l], w_ple_gate[l], w_ple[l]), TM_IN, TN_IN)
        z4 = z.reshape(nb, t_len, 3 * d_s5)
        y, hf, w_down_b = _s5_seq(z4, w_t, w_s, w_c, tab, d_s5, w_down[l])
        mb, lru_h, w_up_b = _lru_seq(z4, *lru_args, w_up[l])
        ys, hsr, hsi = _s5_step(zs, state_s5_re[l].reshape(ns, groups * S5_STATE),
                                state_s5_im[l].reshape(ns, groups * S5_STATE),
                                wb, wct, arow, s5_d[l].reshape(1, d_s5))
        mbs, lru_hs = _lru_step(zs, state_conv[l], state_lru[l], *lru_args)
        xp, xs = _out_proj(xp, y.reshape(nb * t_len, d_s5), mb.reshape(nb * t_len, d_lru),
                           xs, ys, mbs, w_glu_b, s5_b_glu[l], g_merge_a[l], w_out_b, TM_OUT)
        xp, xs = _mlp(xp, xs, g_mlp[l], w_up_b, w_down_b, TM_MLP, TF_MLP)
        xp, xs = _ple(xp, p_prompt[l].reshape(nb * t_len, -1), xs, p_sample[l].reshape(ns, -1),
                      g_ple[l], w_gate_b, w_ple_b, g_final, TM_PLE, final)
        hf = jnp.transpose(hf[:, :, :, :nb], (1, 3, 0, 2))
        outs[0].append(hf[0])
        outs[1].append(hf[1])
        outs[2].append(lru_h.reshape(nb, d_lru))
        outs[3].append(z4[:, t_len - (CONV_W - 1):, d_s5:d_s5 + d_lru])
        outs[4].append(hsr.reshape(ns, groups, S5_STATE))
        outs[5].append(hsi.reshape(ns, groups, S5_STATE))
        outs[6].append(lru_hs)
        outs[7].append(jnp.concatenate(
            [state_conv[l][:, 1:], zs[:, None, d_s5:d_s5 + d_lru]], axis=1))
    return (xp.reshape(nb, t_len, d_model), xs.reshape(ns, 1, d_model),
            *(jnp.stack(o) for o in outs))
```

```python
import functools

import jax
import jax.numpy as jnp
from jax import lax
from jax.experimental import pallas as pl
from jax.experimental.pallas import tpu as pltpu

F32 = jnp.float32
BF16 = jnp.bfloat16
HIGHEST = lax.Precision.HIGHEST

EPS = 1e-6
LRU_C = 8.0
S5_CH = 16
S5_STATE = 64
LRU_HEADS = 4
CONV_W = 4
CHUNK = 16
SUBLANES = 8
LANES = 128
LANE_GROUPS = LANES // S5_CH
LANE_PAIRS = LANE_GROUPS // 2
ROWS = CHUNK * S5_CH
POW_ROWS = 32
SCAN_ROW0 = POW_ROWS
VMEM_LIMIT = 56 * 1024 * 1024


def _params(*sem):
    return pltpu.CompilerParams(dimension_semantics=sem, vmem_limit_bytes=VMEM_LIMIT)


def _rms(x, g):
    return x * lax.rsqrt(jnp.mean(x * x, axis=-1, keepdims=True) + EPS) * g


def _dot(a, b):
    return jnp.dot(a, b, preferred_element_type=F32)


def _dot_nt(a, b, precision=None):
    return lax.dot_general(a, b, (((1,), (1,)), ((), ())), precision=precision,
                           preferred_element_type=F32)


def _transpose_tiles(x):
    r, c = x.shape
    return jnp.concatenate(
        [jnp.concatenate([x[i:i + LANES, j:j + LANES].T for i in range(0, r, LANES)], axis=1)
         for j in range(0, c, LANES)], axis=0)


def _row_parts(tm, want):
    parts = want if tm % (want * 16) == 0 else 1
    step = tm // parts
    return [slice(r * step, (r + 1) * step) for r in range(parts)]


def _on_last_row_tile(fn):
    pl.when(pl.program_id(0) == pl.num_programs(0) - 1)(fn)


X_SPLIT = 4
W_SPLIT = 2


def _norm_matmul_kernel(*refs):
    x_refs = refs[:X_SPLIT]
    xs_ref, g_ref = refs[X_SPLIT:X_SPLIT + 2]
    w_refs = refs[X_SPLIT + 2:X_SPLIT + 2 + W_SPLIT]
    o_ref, os_ref = refs[X_SPLIT + 2 + W_SPLIT:]
    tr = x_refs[0].shape[0]
    tc = w_refs[0].shape[1]
    for r, x_ref in enumerate(x_refs):
        h = _rms(x_ref[...], g_ref[...]).astype(BF16)
        for c, w_ref in enumerate(w_refs):
            o_ref[r * tr:(r + 1) * tr, c * tc:(c + 1) * tc] = _dot(h, w_ref[...])

    @_on_last_row_tile
    def _():
        hs = _rms(xs_ref[...], g_ref[...]).astype(BF16)
        for c, w_ref in enumerate(w_refs):
            os_ref[:, c * tc:(c + 1) * tc] = _dot(hs, w_ref[...])


def _norm_matmul(x, xs, g, w, tm, tn):
    m, k = x.shape
    ms = xs.shape[0]
    n = w.shape[1]
    ni = m // tm
    x_specs = [pl.BlockSpec((tm // X_SPLIT, k), lambda i, j, r=r: (i * X_SPLIT + r, 0))
               for r in range(X_SPLIT)]
    w_specs = [pl.BlockSpec((k, tn // W_SPLIT), lambda i, j, c=c: (0, j * W_SPLIT + c))
               for c in range(W_SPLIT)]
    return pl.pallas_call(
        _norm_matmul_kernel,
        out_shape=(jax.ShapeDtypeStruct((m, n), F32), jax.ShapeDtypeStruct((ms, n), F32)),
        grid=(ni, n // tn),
        in_specs=[
            *x_specs,
            pl.BlockSpec((ms, k), lambda i, j: (0, 0)),
            pl.BlockSpec((1, k), lambda i, j: (0, 0)),
            *w_specs,
        ],
        out_specs=(pl.BlockSpec((tm, tn), lambda i, j: (i, j)),
                   pl.BlockSpec((ms, tn), lambda i, j: (0, jnp.where(i == ni - 1, j, 0)))),
        compiler_params=_params("arbitrary", "arbitrary"),
        name="norm_w_in",
    )(*[x] * X_SPLIT, xs, g.reshape(1, k), *[w] * W_SPLIT)


def _s5_param_kernel(lr_ref, li_ref, ls_ref, bt_r_ref, bt_i_ref, c_r_ref, c_i_ref, d_ref, cast_ref,
                     wt_ref, ws_ref, wc_ref, tab_ref, wb_ref, wct_ref, arow_ref, castb_ref):
    castb_ref[...] = cast_ref[...].astype(BF16)
    row_p = lax.broadcasted_iota(jnp.int32, (POW_ROWS, LANES), 0)
    lane = lax.broadcasted_iota(jnp.int32, (ROWS, LANES), 1)
    lane16 = lax.broadcasted_iota(jnp.int32, (S5_CH, LANES), 1)
    row16 = lax.broadcasted_iota(jnp.int32, (S5_CH, LANES), 0)
    lo = lane < S5_STATE

    wb_ref[...] = jnp.zeros_like(wb_ref)
    wct_ref[...] = jnp.zeros_like(wct_ref)

    for q in range(LANE_PAIRS):
        lr, li = lr_ref[q], li_ref[q]
        step = jnp.exp(ls_ref[q])
        mag = jnp.exp(lr * step)
        ar = mag * jnp.cos(li * step)
        ai = mag * jnp.sin(li * step)
        nr, ni = ar - 1.0, ai
        den = lr * lr + li * li
        cr = (nr * lr + ni * li) / den
        ci = (ni * lr - nr * li) / den
        bt_r, bt_i = bt_r_ref[q], bt_i_ref[q]
        bb_r = cr * bt_r - ci * bt_i
        bb_i = cr * bt_i + ci * bt_r
        c_r, c_i = c_r_ref[q], c_i_ref[q]

        pr = jnp.ones((POW_ROWS, LANES), F32)
        pi = jnp.zeros((POW_ROWS, LANES), F32)
        sr, si = ar, ai
        sq = []
        for m in range(POW_ROWS.bit_length() - 1 + SUBLANES):
            sq.append((sr, si))
            if (1 << m) < POW_ROWS:
                bit = ((row_p >> m) & 1) == 1
                pr, pi = (jnp.where(bit, pr * sr - pi * si, pr),
                          jnp.where(bit, pr * si + pi * sr, pi))
            sr, si = sr * sr - si * si, 2.0 * sr * si
        m0 = CHUNK.bit_length() - 1
        scan_r = jnp.concatenate([sq[m0 + j][0] for j in range(SUBLANES)], axis=0)
        scan_i = jnp.concatenate([sq[m0 + j][1] for j in range(SUBLANES)], axis=0)
        pad = jnp.zeros((LANES - POW_ROWS - SUBLANES, LANES), F32)
        tab_ref[q, 0] = jnp.concatenate([pr, scan_r, pad], axis=0).T
        tab_ref[q, 1] = jnp.concatenate([pi, scan_i, pad], axis=0).T
        arow_ref[q, 0] = pr[:SUBLANES]
        arow_ref[q, 1] = pi[:SUBLANES]

        def expand(tab, k0, sign):
            return jnp.concatenate(
                [jnp.broadcast_to(tab[k0 + sign * s:k0 + sign * s + 1, :], (S5_CH, LANES))
                 for s in range(CHUNK)], axis=0)

        tile = lambda v: jnp.concatenate([v] * CHUNK, axis=0)

        e_r, e_i = expand(pr, CHUNK - 1, -1), expand(pi, CHUNK - 1, -1)
        t_r, t_i = tile(bb_r), tile(bb_i)
        ws_r = _transpose_tiles(t_r * e_r - t_i * e_i)
        ws_i = _transpose_tiles(t_r * e_i + t_i * e_r)
        for h in range(2):
            rows = slice(h * S5_STATE, (h + 1) * S5_STATE)
            ws_ref[2 * q + h] = jnp.concatenate([ws_r[rows], ws_i[rows]], axis=0).astype(BF16)

        e_r, e_i = expand(pr, 1, 1), expand(pi, 1, 1)
        t_r, t_i = tile(c_r), tile(c_i)
        ca_r = t_r * e_r - t_i * e_i
        ca_n = -(t_r * e_i + t_i * e_r)
        wc_ref[2 * q] = jnp.where(lo, ca_r, pltpu.roll(ca_n, S5_STATE, 1)).astype(BF16)
        wc_ref[2 * q + 1] = jnp.where(lo, pltpu.roll(ca_r, S5_STATE, 1), ca_n).astype(BF16)

        e_r, e_i = expand(pr, 0, 1), expand(pi, 0, 1)
        c0_r = t_r * e_r - t_i * e_i
        c0_i = t_r * e_i + t_i * e_r

        for h in range(2):
            g = 2 * q + h
            mine = (lane16 < S5_STATE) == (h == 0)
            bm_r = jnp.where(mine, bb_r, 0.0)
            bm_i = jnp.where(mine, bb_i, 0.0)
            kt = _dot_nt(bm_r, c0_r, HIGHEST) - _dot_nt(bm_i, c0_i, HIGHEST)
            k_lo = kt[:, :LANES] + jnp.where(row16 == lane16, d_ref[g], 0.0)
            k_hi = kt[:, LANES:]
            blocks = []
            half = LANES // S5_CH
            for s in range(CHUNK):
                sh = (s % half) * S5_CH
                keep = lane16 >= sh
                r_lo = pltpu.roll(k_lo, sh, 1) if sh else k_lo
                r_hi = pltpu.roll(k_hi, sh, 1) if sh else k_hi
                if s < half:
                    blk = jnp.concatenate([jnp.where(keep, r_lo, 0.0),
                                           jnp.where(keep, r_hi, r_lo)], axis=1)
                else:
                    blk = jnp.concatenate([jnp.zeros_like(r_lo), jnp.where(keep, r_lo, 0.0)], axis=1)
                blocks.append(blk)
            wt_ref[g] = _transpose_tiles(jnp.concatenate(blocks, axis=0)).astype(BF16)

            r0 = (2 * q + h) * S5_CH
            c0 = q * 2 * LANES
            wb_ref[r0:r0 + S5_CH, c0:c0 + LANES] = bm_r.astype(BF16)
            wb_ref[r0:r0 + S5_CH, c0 + LANES:c0 + 2 * LANES] = bm_i.astype(BF16)
            wct_ref[q, r0:r0 + S5_CH, :LANES] = jnp.where(mine, c_r, 0.0).astype(BF16)
            wct_ref[q, r0:r0 + S5_CH, LANES:] = jnp.where(mine, -c_i, 0.0).astype(BF16)


def _s5_params(lam_re, lam_im, log_step, b_re, b_im, c_re, c_im, d, cast_w):
    groups = lam_re.shape[0]
    pairs = groups // 2
    slabs = groups // LANE_GROUPS
    cast_blk = pl.BlockSpec((cast_w.shape[0] // slabs, cast_w.shape[1]), lambda i: (i, 0))
    pair_row = lambda v: v.reshape(pairs, 1, LANES)
    pair_ch = lambda v: jnp.transpose(v.reshape(pairs, 2, S5_CH, S5_STATE), (0, 2, 1, 3)).reshape(
        pairs, S5_CH, LANES)
    ls = jnp.broadcast_to(log_step[:, None], (groups, S5_STATE))
    d_pad = jnp.pad(d.reshape(groups, 1, S5_CH), ((0, 0), (0, 0), (0, LANES - S5_CH)))
    blk3 = lambda i: (i, 0, 0)
    blk4 = lambda i: (i, 0, 0, 0)
    prow = pl.BlockSpec((LANE_PAIRS, 1, LANES), blk3)
    pch = pl.BlockSpec((LANE_PAIRS, S5_CH, LANES), blk3)
    return pl.pallas_call(
        _s5_param_kernel,
        out_shape=(jax.ShapeDtypeStruct((groups, ROWS, ROWS), BF16),
                   jax.ShapeDtypeStruct((groups, 2 * S5_STATE, ROWS), BF16),
                   jax.ShapeDtypeStruct((groups, ROWS, 2 * S5_STATE), BF16),
                   jax.ShapeDtypeStruct((pairs, 2, LANES, LANES), F32),
                   jax.ShapeDtypeStruct((slabs, LANES, LANE_PAIRS * 2 * LANES), BF16),
                   jax.ShapeDtypeStruct((slabs, LANE_PAIRS, LANES, 2 * LANES), BF16),
                   jax.ShapeDtypeStruct((pairs, 2, SUBLANES, LANES), F32),
                   jax.ShapeDtypeStruct(cast_w.shape, BF16)),
        grid=(slabs,),
        in_specs=[prow, prow, prow, pch, pch, pch, pch,
                  pl.BlockSpec((LANE_GROUPS, 1, LANES), blk3), cast_blk],
        out_specs=(pl.BlockSpec((LANE_GROUPS, ROWS, ROWS), blk3),
                   pl.BlockSpec((LANE_GROUPS, 2 * S5_STATE, ROWS), blk3),
                   pl.BlockSpec((LANE_GROUPS, ROWS, 2 * S5_STATE), blk3),
                   pl.BlockSpec((LANE_PAIRS, 2, LANES, LANES), blk4),
                   pl.BlockSpec((None, LANES, LANE_PAIRS * 2 * LANES), blk3),
                   pl.BlockSpec((None, LANE_PAIRS, LANES, 2 * LANES), blk4),
                   pl.BlockSpec((LANE_PAIRS, 2, SUBLANES, LANES), blk4),
                   cast_blk),
        compiler_params=_params("parallel"),
        name="s5_params",
    )(pair_row(lam_re), pair_row(lam_im), pair_row(ls),
      pair_ch(jnp.transpose(b_re, (0, 2, 1))), pair_ch(jnp.transpose(b_im, (0, 2, 1))),
      pair_ch(c_re), pair_ch(c_im), d_pad, cast_w)


def _s5_seq_kernel(z_ref, wt_ref, ws_ref, wc_ref, tab_ref, cast_ref, y_ref, hf_ref, castb_ref,
                   ut_ref, yt_ref, sr_ref, si_ref, *, nb, n_chunks):
    castb_ref[...] = cast_ref[...].astype(BF16)
    scan_steps = n_chunks.bit_length() - 1
    for n in range(nb):
        for s in range(CHUNK):
            xs = z_ref[n, pl.ds(s, n_chunks, stride=CHUNK), :]
            ut_ref[:, s * S5_CH:(s + 1) * S5_CH, n * n_chunks:(n + 1) * n_chunks] = (
                xs.T.astype(BF16).reshape(LANE_GROUPS, S5_CH, n_chunks))

    for g in range(LANE_GROUPS):
        st = _dot(ws_ref[g], ut_ref[g])
        sr_ref[g * S5_STATE:(g + 1) * S5_STATE, :] = st[:S5_STATE]
        si_ref[g * S5_STATE:(g + 1) * S5_STATE, :] = st[S5_STATE:]

    srows = LANE_GROUPS * S5_STATE
    lane = lax.broadcasted_iota(jnp.int32, (srows, n_chunks), 1)
    tab_r = jnp.concatenate([tab_ref[q, 0] for q in range(LANE_PAIRS)], axis=0)
    tab_i = jnp.concatenate([tab_ref[q, 1] for q in range(LANE_PAIRS)], axis=0)
    fin_r = jnp.zeros((srows, n_chunks), F32)
    fin_i = jnp.zeros((srows, n_chunks), F32)
    for n in range(nb):
        cols = slice(n * n_chunks, (n + 1) * n_chunks)
        xr = sr_ref[:, cols]
        xi = si_ref[:, cols]
        for k in range(scan_steps):
            d = 1 << k
            keep = lane >= d
            sr = jnp.where(keep, pltpu.roll(xr, d, 1), 0.0)
            si = jnp.where(keep, pltpu.roll(xi, d, 1), 0.0)
            pr = tab_r[:, SCAN_ROW0 + k:SCAN_ROW0 + k + 1]
            pi = tab_i[:, SCAN_ROW0 + k:SCAN_ROW0 + k + 1]
            xr, xi = xr + (pr * sr - pi * si), xi + (pr * si + pi * sr)
        last = lane == n_chunks - 1
        fin_r = fin_r + jnp.where(
            lane == n, jnp.sum(jnp.where(last, xr, 0.0), axis=1, keepdims=True), 0.0)
        fin_i = fin_i + jnp.where(
            lane == n, jnp.sum(jnp.where(last, xi, 0.0), axis=1, keepdims=True), 0.0)
        keep = lane >= 1
        sr_ref[:, cols] = jnp.where(keep, pltpu.roll(xr, 1, 1), 0.0)
        si_ref[:, cols] = jnp.where(keep, pltpu.roll(xi, 1, 1), 0.0)

    for g in range(LANE_GROUPS):
        rows = slice(g * S5_STATE, (g + 1) * S5_STATE)
        hf_ref[g, 0] = fin_r[rows]
        hf_ref[g, 1] = fin_i[rows]
        hs = jnp.concatenate([sr_ref[rows, :], si_ref[rows, :]], axis=0).astype(BF16)
        yt_ref[g] = jax.nn.gelu(_dot(wc_ref[g], hs) + _dot(wt_ref[g], ut_ref[g]))

    for n in range(nb):
        for s in range(CHUNK):
            blk = yt_ref[:, s * S5_CH:(s + 1) * S5_CH, n * n_chunks:(n + 1) * n_chunks]
            y_ref[n, pl.ds(s, n_chunks, stride=CHUNK), :] = blk.reshape(LANES, n_chunks).T


def _s5_seq(z, w_t, w_s, w_c, tab, d_s5, cast_w):
    nb, t_len, _ = z.shape
    n_chunks = t_len // CHUNK
    assert n_chunks == LANES, "the chunk axis must fill one 128-lane tile"
    groups = d_s5 // S5_CH
    steps = groups // LANE_GROUPS
    cast_blk = pl.BlockSpec((cast_w.shape[0] // steps, cast_w.shape[1]), lambda i: (i, 0))
    kern = functools.partial(_s5_seq_kernel, nb=nb, n_chunks=n_chunks)
    blk3 = lambda i: (i, 0, 0)
    blk4 = lambda i: (i, 0, 0, 0)
    return pl.pallas_call(
        kern,
        out_shape=(jax.ShapeDtypeStruct((nb, t_len, d_s5), F32),
                   jax.ShapeDtypeStruct((groups, 2, S5_STATE, LANES), F32),
                   jax.ShapeDtypeStruct(cast_w.shape, BF16)),
        grid=(steps,),
        in_specs=[
            pl.BlockSpec((nb, t_len, LANES), lambda i: (0, 0, i)),
            pl.BlockSpec((LANE_GROUPS, ROWS, ROWS), blk3),
            pl.BlockSpec((LANE_GROUPS, 2 * S5_STATE, ROWS), blk3),
            pl.BlockSpec((LANE_GROUPS, ROWS, 2 * S5_STATE), blk3),
            pl.BlockSpec((LANE_PAIRS, 2, LANES, LANES), blk4),
            cast_blk,
        ],
        out_specs=(pl.BlockSpec((nb, t_len, LANES), lambda i: (0, 0, i)),
                   pl.BlockSpec((LANE_GROUPS, 2, S5_STATE, LANES), blk4),
                   cast_blk),
        scratch_shapes=[pltpu.VMEM((LANE_GROUPS, ROWS, nb * n_chunks), BF16),
                        pltpu.VMEM((LANE_GROUPS, ROWS, nb * n_chunks), F32),
                        pltpu.VMEM((LANE_GROUPS * S5_STATE, nb * n_chunks), F32),
                        pltpu.VMEM((LANE_GROUPS * S5_STATE, nb * n_chunks), F32)],
        compiler_params=_params("parallel"),
        name="s5_seq",
    )(z, w_t, w_s, w_c, tab, cast_w)


def _s5_step_kernel(u_ref, h0r_ref, h0i_ref, wb_ref, wct_ref, arow_ref, d_ref,
                    y_ref, hr_ref, hi_ref):
    u = u_ref[...]
    bu = _dot(u.astype(BF16), wb_ref[...])
    y = d_ref[...] * u
    for q in range(LANE_PAIRS):
        cols = slice(q * LANES, (q + 1) * LANES)
        ar = arow_ref[q, 0, 1:2, :]
        ai = arow_ref[q, 1, 1:2, :]
        h0r, h0i = h0r_ref[:, cols], h0i_ref[:, cols]
        hr = ar * h0r - ai * h0i + bu[:, 2 * q * LANES:(2 * q + 1) * LANES]
        hi = ar * h0i + ai * h0r + bu[:, (2 * q + 1) * LANES:(2 * q + 2) * LANES]
        hr_ref[:, cols] = hr
        hi_ref[:, cols] = hi
        y = y + _dot_nt(jnp.concatenate([hr, hi], axis=1).astype(BF16), wct_ref[q])
    y_ref[...] = jax.nn.gelu(y).astype(BF16)


def _s5_step(z, h0r, h0i, wb, wct, arow, d):
    n = z.shape[0]
    slabs = wb.shape[0]
    sw = LANE_PAIRS * LANES
    col = lambda i: (0, i)
    blk3 = lambda i: (i, 0, 0)
    blk4 = lambda i: (i, 0, 0, 0)
    return pl.pallas_call(
        _s5_step_kernel,
        out_shape=(jax.ShapeDtypeStruct((n, slabs * LANES), BF16),
                   jax.ShapeDtypeStruct((n, slabs * sw), F32),
                   jax.ShapeDtypeStruct((n, slabs * sw), F32)),
        grid=(slabs,),
        in_specs=[
            pl.BlockSpec((n, LANES), col),
            pl.BlockSpec((n, sw), col),
            pl.BlockSpec((n, sw), col),
            pl.BlockSpec((None, LANES, 2 * sw), blk3),
            pl.BlockSpec((None, LANE_PAIRS, LANES, 2 * LANES), blk4),
            pl.BlockSpec((LANE_PAIRS, 2, SUBLANES, LANES), blk4),
            pl.BlockSpec((1, LANES), col),
        ],
        out_specs=(pl.BlockSpec((n, LANES), col),
                   pl.BlockSpec((n, sw), col),
                   pl.BlockSpec((n, sw), col)),
        compiler_params=_params("parallel"),
        name="s5_step",
    )(z, h0r, h0i, wb, wct, arow, d)


def _lru_gate_block(xc, wa, ba, wi, bi, lam):
    xb16 = xc.astype(BF16)
    r = jax.nn.sigmoid(_dot(xb16, wa) + ba)
    ig = jax.nn.sigmoid(_dot(xb16, wi) + bi)
    log_a = -LRU_C * r * jax.nn.softplus(-lam)
    a = jnp.exp(log_a)
    mult = jnp.sqrt(1.0 - a * a)
    return a, mult * (ig * xc)


def _lru_gates(xc, wa_ref, ba, wi_ref, bi, lam):
    blk = xc.shape[1] // LRU_HEADS
    parts = [_lru_gate_block(xc[:, h * blk:(h + 1) * blk], wa_ref[h], ba[:, h * blk:(h + 1) * blk],
                             wi_ref[h], bi[:, h * blk:(h + 1) * blk],
                             lam[:, h * blk:(h + 1) * blk]) for h in range(LRU_HEADS)]
    return (jnp.concatenate([p[0] for p in parts], axis=-1),
            jnp.concatenate([p[1] for p in parts], axis=-1))


def _lru_seq_kernel(xb_ref, gb_ref, cw_ref, cb_ref, wa_ref, ba_ref, wi_ref, bi_ref, lam_ref,
                    gm_ref, cast_ref, o_ref, hl_ref, castb_ref, xe_ref, a_ref, b_ref, hc_ref,
                    *, tt):
    castb_ref[...] = cast_ref[...].astype(BF16)
    halo = SUBLANES
    nseq, _, d = xb_ref.shape
    blk = d // LRU_HEADS

    @pl.when(pl.program_id(0) == 0)
    def _():
        xe_ref[:, 0:halo, :] = jnp.zeros((nseq, halo, d), F32)
        hc_ref[...] = jnp.zeros_like(hc_ref)

    xe_ref[:, halo:halo + tt, :] = xb_ref[...]
    for h in range(LRU_HEADS):
        cols = slice(h * blk, (h + 1) * blk)
        cw = cw_ref[:, cols]
        xc = cb_ref[:, cols] + xe_ref[:, halo:halo + tt, cols] * cw[CONV_W - 1:CONV_W, :]
        for k in range(1, CONV_W):
            xc = xc + xe_ref[:, halo - k:halo - k + tt, cols] * cw[CONV_W - 1 - k:CONV_W - k, :]
        a, b = _lru_gate_block(xc.reshape(nseq * tt, blk), wa_ref[h], ba_ref[:, cols],
                               wi_ref[h], bi_ref[:, cols], lam_ref[:, cols])
        a_ref[:, :, cols] = a.reshape(nseq, tt, blk)
        b_ref[:, :, cols] = b.reshape(nseq, tt, blk)
    xe_ref[:, 0:halo, :] = xb_ref[:, tt - halo:tt, :]

    def block(i, h):
        base = pl.multiple_of(i * SUBLANES, SUBLANES)
        for j in range(SUBLANES):
            row = pl.ds(base + j, 1)
            h = a_ref[:, row, :] * h + b_ref[:, row, :]
            b_ref[:, row, :] = h
        return h

    h = lax.fori_loop(0, tt // SUBLANES, block, hc_ref[...])
    hc_ref[...] = h
    hl_ref[...] = h

    out = b_ref[...] * jax.nn.gelu(gb_ref[...])
    o_ref[...] = _rms(out, gm_ref[...]).astype(BF16)


def _lru_seq(z, conv_w, conv_b, w_a, b_a, w_i, b_i, lam, g_merge, cast_w, tt=64):
    nseq, t_len, _ = z.shape
    d = conv_w.shape[1]
    steps = t_len // tt
    cast_blk = pl.BlockSpec((cast_w.shape[0] // steps, cast_w.shape[1]), lambda t: (t, 0))
    row = lambda v: v.reshape(1, d)
    const2 = lambda t: (0, 0)
    const3 = lambda t: (0, 0, 0)
    kern = functools.partial(_lru_seq_kernel, tt=tt)
    return pl.pallas_call(
        kern,
        out_shape=(jax.ShapeDtypeStruct((nseq, t_len, d), BF16),
                   jax.ShapeDtypeStruct((nseq, 1, d), F32),
                   jax.ShapeDtypeStruct(cast_w.shape, BF16)),
        grid=(steps,),
        in_specs=[
            pl.BlockSpec((nseq, tt, d), lambda t: (0, t, 1)),
            pl.BlockSpec((nseq, tt, d), lambda t: (0, t, 2)),
            pl.BlockSpec((CONV_W, d), const2),
            pl.BlockSpec((1, d), const2),
            pl.BlockSpec(w_a.shape, const3),
            pl.BlockSpec((1, d), const2),
            pl.BlockSpec(w_i.shape, const3),
            pl.BlockSpec((1, d), const2),
            pl.BlockSpec((1, d), const2),
            pl.BlockSpec((1, d), const2),
            cast_blk,
        ],
        out_specs=(pl.BlockSpec((nseq, tt, d), lambda t: (0, t, 0)),
                   pl.BlockSpec((nseq, 1, d), const3),
                   cast_blk),
        scratch_shapes=[pltpu.VMEM((nseq, tt + SUBLANES, d), F32),
                        pltpu.VMEM((nseq, tt, d), F32),
                        pltpu.VMEM((nseq, tt, d), F32),
                        pltpu.VMEM((nseq, 1, d), F32)],
        compiler_params=_params("arbitrary"),
        name="rglru_seq",
    )(z, z, conv_w, row(conv_b), w_a, row(b_a), w_i, row(b_i), row(lam), row(g_merge), cast_w)


def _lru_step_kernel(xb_ref, gb_ref, c0_ref, c1_ref, c2_ref, h0_ref, cw_ref, cb_ref,
                     wa_ref, ba_ref, wi_ref, bi_ref, lam_ref, gm_ref, o_ref, h_ref):
    xb = xb_ref[...]
    cw = cw_ref[...]
    xc = (cb_ref[...] + c0_ref[...] * cw[0:1, :] + c1_ref[...] * cw[1:2, :]
          + c2_ref[...] * cw[2:3, :] + xb * cw[3:4, :])
    a, b = _lru_gates(xc, wa_ref, ba_ref[...], wi_ref, bi_ref[...], lam_ref[...])
    h = a * h0_ref[...] + b
    h_ref[...] = h
    out = h * jax.nn.gelu(gb_ref[...])
    o_ref[...] = _rms(out, gm_ref[...]).astype(BF16)


def _lru_step(z, conv0, h0, conv_w, conv_b, w_a, b_a, w_i, b_i, lam, g_merge):
    n = z.shape[0]
    d = conv_w.shape[1]
    row = lambda v: v.reshape(1, d)
    full = lambda shape: pl.BlockSpec(shape, lambda i: (0,) * len(shape))
    nd = full((n, d))
    rd = full((1, d))
    return pl.pallas_call(
        _lru_step_kernel,
        out_shape=(jax.ShapeDtypeStruct((n, d), BF16), jax.ShapeDtypeStruct((n, d), F32)),
        grid=(1,),
        in_specs=[
            pl.BlockSpec((n, d), lambda i: (0, 1)),
            pl.BlockSpec((n, d), lambda i: (0, 2)),
            nd, nd, nd, nd,
            full((CONV_W, d)), rd,
            full(w_a.shape), rd, full(w_i.shape), rd, rd, rd,
        ],
        out_specs=(nd, nd),
        compiler_params=_params("arbitrary"),
        name="rglru_step",
    )(z, z, conv0[:, 0], conv0[:, 1], conv0[:, 2], h0, conv_w, row(conv_b),
      w_a, row(b_a), w_i, row(b_i), row(lam), row(g_merge))


def _glu_rows(y, w_ref, b_ref, g_ref):
    gate = jax.nn.sigmoid(_dot(y.astype(BF16), w_ref[...]) + b_ref[...])
    return _rms(y.astype(F32) * gate, g_ref[...]).astype(BF16)


def _glu_kernel(y_ref, ys_ref, w_ref, b_ref, g_ref, o_ref, os_ref):
    for rows in _row_parts(y_ref.shape[0], 4):
        o_ref[rows, :] = _glu_rows(y_ref[rows, :], w_ref, b_ref, g_ref)

    @_on_last_row_tile
    def _():
        os_ref[...] = _glu_rows(ys_ref[...], w_ref, b_ref, g_ref)


def _glu(y, ys, w, b, g, tm):
    m, d = y.shape
    ms = ys.shape[0]
    const = lambda i: (0, 0)
    return pl.pallas_call(
        _glu_kernel,
        out_shape=(jax.ShapeDtypeStruct((m, d), BF16), jax.ShapeDtypeStruct((ms, d), BF16)),
        grid=(m // tm,),
        in_specs=[
            pl.BlockSpec((tm, d), lambda i: (i, 0)),
            pl.BlockSpec((ms, d), const),
            pl.BlockSpec((d, d), const),
            pl.BlockSpec((1, d), const),
            pl.BlockSpec((1, d), const),
        ],
        out_specs=(pl.BlockSpec((tm, d), lambda i: (i, 0)), pl.BlockSpec((ms, d), const)),
        compiler_params=_params("arbitrary"),
        name="s5_glu",
    )(y, ys, w, b.reshape(1, d), g.reshape(1, d))


def _out_proj_kernel(x_ref, ma_ref, mb_ref, xs_ref, mas_ref, mbs_ref, w_ref, o_ref, os_ref, wb_ref):
    @pl.when(pl.program_id(0) == 0)
    def _():
        wb_ref[...] = w_ref[...].astype(BF16)

    ka = ma_ref.shape[1]
    proj = lambda x, ma, mb: x + _dot(ma, wb_ref[:ka, :]) + _dot(mb, wb_ref[ka:, :])
    for rows in _row_parts(x_ref.shape[0], 2):
        o_ref[rows, :] = proj(x_ref[rows, :], ma_ref[rows, :], mb_ref[rows, :])

    @_on_last_row_tile
    def _():
        os_ref[...] = proj(xs_ref[...], mas_ref[...], mbs_ref[...])


def _out_proj(x, ma, mb, xs, mas, mbs, w, tm):
    m, n = x.shape
    ms = xs.shape[0]
    ka = ma.shape[1]
    kb = mb.shape[1]
    const = lambda i: (0, 0)
    tile = lambda i: (i, 0)
    return pl.pallas_call(
        _out_proj_kernel,
        out_shape=(jax.ShapeDtypeStruct((m, n), F32), jax.ShapeDtypeStruct((ms, n), F32)),
        grid=(m // tm,),
        in_specs=[
            pl.BlockSpec((tm, n), tile),
            pl.BlockSpec((tm, ka), tile),
            pl.BlockSpec((tm, kb), tile),
            pl.BlockSpec((ms, n), const),
            pl.BlockSpec((ms, ka), const),
            pl.BlockSpec((ms, kb), const),
            pl.BlockSpec((ka + kb, n), const, pipeline_mode=pl.Buffered(1)),
        ],
        out_specs=(pl.BlockSpec((tm, n), tile), pl.BlockSpec((ms, n), const)),
        scratch_shapes=[pltpu.VMEM((ka + kb, n), BF16)],
        compiler_params=_params("arbitrary"),
        name="out_proj",
    )(x, ma, mb, xs, mas, mbs, w)


def _mlp_kernel(*refs):
    x_ref, xs_ref, g_ref = refs[:3]
    wu_refs = refs[3:3 + W_SPLIT]
    wd_refs = refs[3 + W_SPLIT:3 + 2 * W_SPLIT]
    o_ref, os_ref, h_ref, hs_ref = refs[3 + 2 * W_SPLIT:]

    def rows(x_ref, o_ref, h_ref):
        @pl.when(pl.program_id(1) == 0)
        def _():
            x = x_ref[...]
            h_ref[...] = _rms(x, g_ref[...]).astype(BF16)
            o_ref[...] = x

        for wu_ref, wd_ref in zip(wu_refs, wd_refs):
            up = _dot(h_ref[...], wu_ref[...])
            act = jnp.square(jnp.maximum(up, 0.0)).astype(BF16)
            o_ref[...] += _dot(act, wd_ref[...])

    rows(x_ref, o_ref, h_ref)
    _on_last_row_tile(lambda: rows(xs_ref, os_ref, hs_ref))


def _mlp(x, xs, g, w_up, w_down, tm, tf):
    m, d = x.shape
    ms = xs.shape[0]
    f = w_up.shape[1]
    const = lambda i, j: (0, 0)
    tc = tf // W_SPLIT
    wu_specs = [pl.BlockSpec((d, tc), lambda i, j, c=c: (0, j * W_SPLIT + c)) for c in range(W_SPLIT)]
    wd_specs = [pl.BlockSpec((tc, d), lambda i, j, c=c: (j * W_SPLIT + c, 0)) for c in range(W_SPLIT)]
    return pl.pallas_call(
        _mlp_kernel,
        out_shape=(jax.ShapeDtypeStruct((m, d), F32), jax.ShapeDtypeStruct((ms, d), F32)),
        grid=(m // tm, f // tf),
        in_specs=[
            pl.BlockSpec((tm, d), lambda i, j: (i, 0)),
            pl.BlockSpec((ms, d), const),
            pl.BlockSpec((1, d), const),
            *wu_specs,
            *wd_specs,
        ],
        out_specs=(pl.BlockSpec((tm, d), lambda i, j: (i, 0)), pl.BlockSpec((ms, d), const)),
        scratch_shapes=[pltpu.VMEM((tm, d), BF16), pltpu.VMEM((ms, d), BF16)],
        compiler_params=_params("arbitrary", "arbitrary"),
        name="mlp",
    )(x, xs, g.reshape(1, d), *[w_up] * W_SPLIT, *[w_down] * W_SPLIT)


def _ple_kernel(x_ref, p_ref, xs_ref, ps_ref, g_ref, wg_ref, wp_ref, gf_ref, o_ref, os_ref,
                wgb_ref, wpb_ref, *, final):
    @pl.when(pl.program_id(0) == 0)
    def _():
        wgb_ref[...] = wg_ref[...].astype(BF16)
        wpb_ref[...] = wp_ref[...].astype(BF16)

    def ple(x, p):
        gate = jax.nn.sigmoid(_dot(_rms(x, g_ref[...]).astype(BF16), wgb_ref[...]))
        pe = _dot(p.astype(BF16), wpb_ref[...])
        x = x + pe * gate
        return _rms(x, gf_ref[...]) if final else x

    for rows in _row_parts(x_ref.shape[0], 2):
        o_ref[rows, :] = ple(x_ref[rows, :], p_ref[rows, :])

    @_on_last_row_tile
    def _():
        os_ref[...] = ple(xs_ref[...], ps_ref[...])


def _ple(x, p, xs, ps, g, w_gate, w_ple, g_final, tm, final):
    m, d = x.shape
    ms = xs.shape[0]
    dp = p.shape[1]
    once = pl.Buffered(1)
    const = lambda i: (0, 0)
    tile = lambda i: (i, 0)
    return pl.pallas_call(
        functools.partial(_ple_kernel, final=final),
        out_shape=(jax.ShapeDtypeStruct((m, d), F32), jax.ShapeDtypeStruct((ms, d), F32)),
        grid=(m // tm,),
        in_specs=[
            pl.BlockSpec((tm, d), tile),
            pl.BlockSpec((tm, dp), tile),
            pl.BlockSpec((ms, d), const),
            pl.BlockSpec((ms, dp), const),
            pl.BlockSpec((1, d), const),
            pl.BlockSpec((d, d), const, pipeline_mode=once),
            pl.BlockSpec((dp, d), const, pipeline_mode=once),
            pl.BlockSpec((1, d), const),
        ],
        out_specs=(pl.BlockSpec((tm, d), tile), pl.BlockSpec((ms, d), const)),
        scratch_shapes=[pltpu.VMEM((d, d), BF16), pltpu.VMEM((dp, d), BF16)],
        compiler_params=_params("arbitrary"),
        name="ple_final",
    )(x, p, xs, ps, g.reshape(1, d), w_gate, w_ple, g_final.reshape(1, d))


TM_IN, TN_IN = 1024, 1024
TM_GLU = 1024
TM_OUT = 512
TM_MLP, TF_MLP = 512, 1024
TM_PLE = 256


def kernel(x_prompt, x_sample, state_s5_re, state_s5_im, state_lru, state_conv, p_prompt, p_sample,
           g_mix, w_in, s5_lam_re, s5_lam_im, s5_log_step, s5_b_re, s5_b_im, s5_c_re, s5_c_im, s5_d,
           s5_w_glu, s5_b_glu, conv_w, conv_b, lru_w_a, lru_b_a, lru_w_i, lru_b_i, lru_lam,
           g_merge_a, g_merge_b, w_out, g_mlp, w_up, w_down, g_ple, w_ple_gate, w_ple, g_final):
    depth = g_mix.shape[0]
    nb, t_len, d_model = x_prompt.shape
    ns = x_sample.shape[0]
    d_s5 = s5_d.shape[1]
    d_lru = conv_w.shape[2]
    groups = d_s5 // S5_CH

    xp = x_prompt.reshape(nb * t_len, d_model)
    xs = x_sample.reshape(ns, d_model)
    outs = [[] for _ in range(8)]
    for l in range(depth):
        final = l == depth - 1
        w_glu_b = s5_w_glu[l].astype(BF16)
        lru_args = (conv_w[l], conv_b[l], lru_w_a[l].astype(BF16), lru_b_a[l],
                    lru_w_i[l].astype(BF16), lru_b_i[l], lru_lam[l], g_merge_b[l])
        w_t, w_s, w_c, tab, wb, wct, arow, w_in_b = _s5_params(
            s5_lam_re[l], s5_lam_im[l], s5_log_step[l], s5_b_re[l], s5_b_im[l],
            s5_c_re[l], s5_c_im[l], s5_d[l], w_in[l])

        z, zs = _norm_matmul(xp, xs, g_mix[l], w_in_b, TM_IN, TN_IN)
        z4 = z.reshape(nb, t_len, 3 * d_s5)
        y, hf, w_down_b = _s5_seq(z4, w_t, w_s, w_c, tab, d_s5, w_down[l])
        mb, lru_h, w_up_b = _lru_seq(z4, *lru_args, w_up[l])
        ys, hsr, hsi = _s5_step(zs, state_s5_re[l].reshape(ns, groups * S5_STATE),
                                state_s5_im[l].reshape(ns, groups * S5_STATE),
                                wb, wct, arow, s5_d[l].reshape(1, d_s5))
        mbs, lru_hs = _lru_step(zs, state_conv[l], state_lru[l], *lru_args)
        ma, mas = _glu(y.reshape(nb * t_len, d_s5), ys, w_glu_b, s5_b_glu[l], g_merge_a[l], TM_GLU)
        xp, xs = _out_proj(xp, ma, mb.reshape(nb * t_len, d_lru), xs, mas, mbs, w_out[l], TM_OUT)
        xp, xs = _mlp(xp, xs, g_mlp[l], w_up_b, w_down_b, TM_MLP, TF_MLP)
        xp, xs = _ple(xp, p_prompt[l].reshape(nb * t_len, -1), xs, p_sample[l].reshape(ns, -1),
                      g_ple[l], w_ple_gate[l], w_ple[l], g_final, TM_PLE, final)
        hf = jnp.transpose(hf[:, :, :, :nb], (1, 3, 0, 2))
        outs[0].append(hf[0])
        outs[1].append(hf[1])
        outs[2].append(lru_h.reshape(nb, d_lru))
        outs[3].append(z4[:, t_len - (CONV_W - 1):, d_s5:d_s5 + d_lru])
        outs[4].append(hsr.reshape(ns, groups, S5_STATE))
        outs[5].append(hsi.reshape(ns, groups, S5_STATE))
        outs[6].append(lru_hs)
        outs[7].append(jnp.concatenate(
            [state_conv[l][:, 1:], zs[:, None, d_s5:d_s5 + d_lru]], axis=1))
    return (xp.reshape(nb, t_len, d_model), xs.reshape(ns, 1, d_model),
            *(jnp.stack(o) for o in outs))
```

```python
import functools

import jax
import jax.numpy as jnp
from jax import lax
from jax.experimental import pallas as pl
from jax.experimental.pallas import tpu as pltpu

F32 = jnp.float32
BF16 = jnp.bfloat16
HIGHEST = lax.Precision.HIGHEST

EPS = 1e-6
LRU_C = 8.0
S5_CH = 16
S5_STATE = 64
LRU_HEADS = 4
CONV_W = 4
CHUNK = 16
SUBLANES = 8
LANES = 128
LANE_GROUPS = LANES // S5_CH
LANE_PAIRS = LANE_GROUPS // 2
ROWS = CHUNK * S5_CH
POW_ROWS = 32
SCAN_ROW0 = POW_ROWS
VMEM_LIMIT = 56 * 1024 * 1024


def _params(*sem):
    return pltpu.CompilerParams(dimension_semantics=sem, vmem_limit_bytes=VMEM_LIMIT)


def _rms(x, g):
    return x * lax.rsqrt(jnp.mean(x * x, axis=-1, keepdims=True) + EPS) * g


def _dot(a, b):
    return jnp.dot(a, b, preferred_element_type=F32)


def _dot_nt(a, b, precision=None):
    return lax.dot_general(a, b, (((1,), (1,)), ((), ())), precision=precision,
                           preferred_element_type=F32)


def _transpose_tiles(x):
    r, c = x.shape
    return jnp.concatenate(
        [jnp.concatenate([x[i:i + LANES, j:j + LANES].T for i in range(0, r, LANES)], axis=1)
         for j in range(0, c, LANES)], axis=0)


def _row_parts(tm, want):
    parts = want if tm % (want * 16) == 0 else 1
    step = tm // parts
    return [slice(r * step, (r + 1) * step) for r in range(parts)]


def _on_last_row_tile(fn):
    pl.when(pl.program_id(0) == pl.num_programs(0) - 1)(fn)


def _norm_matmul_kernel(x_ref, xs_ref, g_ref, w_ref, o_ref, os_ref):
    tn = o_ref.shape[1]
    w = w_ref[:, pl.ds(pl.multiple_of(pl.program_id(1) * tn, tn), tn)]
    for rows in _row_parts(x_ref.shape[0], 4):
        h = _rms(x_ref[rows, :], g_ref[...]).astype(BF16)
        o_ref[rows, :] = _dot(h, w)

    @_on_last_row_tile
    def _():
        os_ref[...] = _dot(_rms(xs_ref[...], g_ref[...]).astype(BF16), w)


def _norm_matmul(x, xs, g, w, tm, tn):
    m, k = x.shape
    ms = xs.shape[0]
    n = w.shape[1]
    ni = m // tm
    return pl.pallas_call(
        _norm_matmul_kernel,
        out_shape=(jax.ShapeDtypeStruct((m, n), F32), jax.ShapeDtypeStruct((ms, n), F32)),
        grid=(ni, n // tn),
        in_specs=[
            pl.BlockSpec((tm, k), lambda i, j: (i, 0)),
            pl.BlockSpec((ms, k), lambda i, j: (0, 0)),
            pl.BlockSpec((1, k), lambda i, j: (0, 0)),
            pl.BlockSpec((k, n), lambda i, j: (0, 0), pipeline_mode=pl.Buffered(1)),
        ],
        out_specs=(pl.BlockSpec((tm, tn), lambda i, j: (i, j)),
                   pl.BlockSpec((ms, tn), lambda i, j: (0, jnp.where(i == ni - 1, j, 0)))),
        compiler_params=_params("arbitrary", "arbitrary"),
        name="norm_w_in",
    )(x, xs, g.reshape(1, k), w)


def _s5_param_kernel(lr_ref, li_ref, ls_ref, bt_r_ref, bt_i_ref, c_r_ref, c_i_ref, d_ref, cast_ref,
                     wt_ref, ws_ref, wc_ref, tab_ref, wb_ref, wct_ref, arow_ref, castb_ref):
    castb_ref[...] = cast_ref[...].astype(BF16)
    row_p = lax.broadcasted_iota(jnp.int32, (POW_ROWS, LANES), 0)
    lane = lax.broadcasted_iota(jnp.int32, (ROWS, LANES), 1)
    lane16 = lax.broadcasted_iota(jnp.int32, (S5_CH, LANES), 1)
    row16 = lax.broadcasted_iota(jnp.int32, (S5_CH, LANES), 0)
    lo = lane < S5_STATE

    wb_ref[...] = jnp.zeros_like(wb_ref)
    wct_ref[...] = jnp.zeros_like(wct_ref)

    for q in range(LANE_PAIRS):
        lr, li = lr_ref[q], li_ref[q]
        step = jnp.exp(ls_ref[q])
        mag = jnp.exp(lr * step)
        ar = mag * jnp.cos(li * step)
        ai = mag * jnp.sin(li * step)
        nr, ni = ar - 1.0, ai
        den = lr * lr + li * li
        cr = (nr * lr + ni * li) / den
        ci = (ni * lr - nr * li) / den
        bt_r, bt_i = bt_r_ref[q], bt_i_ref[q]
        bb_r = cr * bt_r - ci * bt_i
        bb_i = cr * bt_i + ci * bt_r
        c_r, c_i = c_r_ref[q], c_i_ref[q]

        pr = jnp.ones((POW_ROWS, LANES), F32)
        pi = jnp.zeros((POW_ROWS, LANES), F32)
        sr, si = ar, ai
        sq = []
        for m in range(POW_ROWS.bit_length() - 1 + SUBLANES):
            sq.append((sr, si))
            if (1 << m) < POW_ROWS:
                bit = ((row_p >> m) & 1) == 1
                pr, pi = (jnp.where(bit, pr * sr - pi * si, pr),
                          jnp.where(bit, pr * si + pi * sr, pi))
            sr, si = sr * sr - si * si, 2.0 * sr * si
        m0 = CHUNK.bit_length() - 1
        scan_r = jnp.concatenate([sq[m0 + j][0] for j in range(SUBLANES)], axis=0)
        scan_i = jnp.concatenate([sq[m0 + j][1] for j in range(SUBLANES)], axis=0)
        pad = jnp.zeros((LANES - POW_ROWS - SUBLANES, LANES), F32)
        tab_ref[q, 0] = jnp.concatenate([pr, scan_r, pad], axis=0).T
        tab_ref[q, 1] = jnp.concatenate([pi, scan_i, pad], axis=0).T
        arow_ref[q, 0] = pr[:SUBLANES]
        arow_ref[q, 1] = pi[:SUBLANES]

        def expand(tab, k0, sign):
            return jnp.concatenate(
                [jnp.broadcast_to(tab[k0 + sign * s:k0 + sign * s + 1, :], (S5_CH, LANES))
                 for s in range(CHUNK)], axis=0)

        tile = lambda v: jnp.concatenate([v] * CHUNK, axis=0)

        e_r, e_i = expand(pr, CHUNK - 1, -1), expand(pi, CHUNK - 1, -1)
        t_r, t_i = tile(bb_r), tile(bb_i)
        ws_r = _transpose_tiles(t_r * e_r - t_i * e_i)
        ws_i = _transpose_tiles(t_r * e_i + t_i * e_r)
        for h in range(2):
            rows = slice(h * S5_STATE, (h + 1) * S5_STATE)
            ws_ref[2 * q + h] = jnp.concatenate([ws_r[rows], ws_i[rows]], axis=0).astype(BF16)

        e_r, e_i = expand(pr, 1, 1), expand(pi, 1, 1)
        t_r, t_i = tile(c_r), tile(c_i)
        ca_r = t_r * e_r - t_i * e_i
        ca_n = -(t_r * e_i + t_i * e_r)
        wc_ref[2 * q] = jnp.where(lo, ca_r, pltpu.roll(ca_n, S5_STATE, 1)).astype(BF16)
        wc_ref[2 * q + 1] = jnp.where(lo, pltpu.roll(ca_r, S5_STATE, 1), ca_n).astype(BF16)

        e_r, e_i = expand(pr, 0, 1), expand(pi, 0, 1)
        c0_r = t_r * e_r - t_i * e_i
        c0_i = t_r * e_i + t_i * e_r

        for h in range(2):
            g = 2 * q + h
            mine = (lane16 < S5_STATE) == (h == 0)
            bm_r = jnp.where(mine, bb_r, 0.0)
            bm_i = jnp.where(mine, bb_i, 0.0)
            kt = _dot_nt(bm_r, c0_r, HIGHEST) - _dot_nt(bm_i, c0_i, HIGHEST)
            k_lo = kt[:, :LANES] + jnp.where(row16 == lane16, d_ref[g], 0.0)
            k_hi = kt[:, LANES:]
            blocks = []
            half = LANES // S5_CH
            for s in range(CHUNK):
                sh = (s % half) * S5_CH
                keep = lane16 >= sh
                r_lo = pltpu.roll(k_lo, sh, 1) if sh else k_lo
                r_hi = pltpu.roll(k_hi, sh, 1) if sh else k_hi
                if s < half:
                    blk = jnp.concatenate([jnp.where(keep, r_lo, 0.0),
                                           jnp.where(keep, r_hi, r_lo)], axis=1)
                else:
                    blk = jnp.concatenate([jnp.zeros_like(r_lo), jnp.where(keep, r_lo, 0.0)], axis=1)
                blocks.append(blk)
            wt_ref[g] = _transpose_tiles(jnp.concatenate(blocks, axis=0)).astype(BF16)

            r0 = (2 * q + h) * S5_CH
            c0 = q * 2 * LANES
            wb_ref[r0:r0 + S5_CH, c0:c0 + LANES] = bm_r.astype(BF16)
            wb_ref[r0:r0 + S5_CH, c0 + LANES:c0 + 2 * LANES] = bm_i.astype(BF16)
            wct_ref[q, r0:r0 + S5_CH, :LANES] = jnp.where(mine, c_r, 0.0).astype(BF16)
            wct_ref[q, r0:r0 + S5_CH, LANES:] = jnp.where(mine, -c_i, 0.0).astype(BF16)


def _s5_params(lam_re, lam_im, log_step, b_re, b_im, c_re, c_im, d, cast_w):
    groups = lam_re.shape[0]
    pairs = groups // 2
    slabs = groups // LANE_GROUPS
    cast_blk = pl.BlockSpec((cast_w.shape[0] // slabs, cast_w.shape[1]), lambda i: (i, 0))
    pair_row = lambda v: v.reshape(pairs, 1, LANES)
    pair_ch = lambda v: jnp.transpose(v.reshape(pairs, 2, S5_CH, S5_STATE), (0, 2, 1, 3)).reshape(
        pairs, S5_CH, LANES)
    ls = jnp.broadcast_to(log_step[:, None], (groups, S5_STATE))
    d_pad = jnp.pad(d.reshape(groups, 1, S5_CH), ((0, 0), (0, 0), (0, LANES - S5_CH)))
    blk3 = lambda i: (i, 0, 0)
    blk4 = lambda i: (i, 0, 0, 0)
    prow = pl.BlockSpec((LANE_PAIRS, 1, LANES), blk3)
    pch = pl.BlockSpec((LANE_PAIRS, S5_CH, LANES), blk3)
    return pl.pallas_call(
        _s5_param_kernel,
        out_shape=(jax.ShapeDtypeStruct((groups, ROWS, ROWS), BF16),
                   jax.ShapeDtypeStruct((groups, 2 * S5_STATE, ROWS), BF16),
                   jax.ShapeDtypeStruct((groups, ROWS, 2 * S5_STATE), BF16),
                   jax.ShapeDtypeStruct((pairs, 2, LANES, LANES), F32),
                   jax.ShapeDtypeStruct((slabs, LANES, LANE_PAIRS * 2 * LANES), BF16),
                   jax.ShapeDtypeStruct((slabs, LANE_PAIRS, LANES, 2 * LANES), BF16),
                   jax.ShapeDtypeStruct((pairs, 2, SUBLANES, LANES), F32),
                   jax.ShapeDtypeStruct(cast_w.shape, BF16)),
        grid=(slabs,),
        in_specs=[prow, prow, prow, pch, pch, pch, pch,
                  pl.BlockSpec((LANE_GROUPS, 1, LANES), blk3), cast_blk],
        out_specs=(pl.BlockSpec((LANE_GROUPS, ROWS, ROWS), blk3),
                   pl.BlockSpec((LANE_GROUPS, 2 * S5_STATE, ROWS), blk3),
                   pl.BlockSpec((LANE_GROUPS, ROWS, 2 * S5_STATE), blk3),
                   pl.BlockSpec((LANE_PAIRS, 2, LANES, LANES), blk4),
                   pl.BlockSpec((None, LANES, LANE_PAIRS * 2 * LANES), blk3),
                   pl.BlockSpec((None, LANE_PAIRS, LANES, 2 * LANES), blk4),
                   pl.BlockSpec((LANE_PAIRS, 2, SUBLANES, LANES), blk4),
                   cast_blk),
        compiler_params=_params("parallel"),
        name="s5_params",
    )(pair_row(lam_re), pair_row(lam_im), pair_row(ls),
      pair_ch(jnp.transpose(b_re, (0, 2, 1))), pair_ch(jnp.transpose(b_im, (0, 2, 1))),
      pair_ch(c_re), pair_ch(c_im), d_pad, cast_w)


def _s5_seq_kernel(z_ref, wt_ref, ws_ref, wc_ref, tab_ref, cast_ref, y_ref, hf_ref, castb_ref,
                   ut_ref, yt_ref, sr_ref, si_ref, *, nb, n_chunks):
    castb_ref[...] = cast_ref[...].astype(BF16)
    scan_steps = n_chunks.bit_length() - 1
    for n in range(nb):
        for s in range(CHUNK):
            xs = z_ref[n, pl.ds(s, n_chunks, stride=CHUNK), :]
            ut_ref[:, s * S5_CH:(s + 1) * S5_CH, n * n_chunks:(n + 1) * n_chunks] = (
                xs.T.astype(BF16).reshape(LANE_GROUPS, S5_CH, n_chunks))

    for g in range(LANE_GROUPS):
        st = _dot(ws_ref[g], ut_ref[g])
        sr_ref[g * S5_STATE:(g + 1) * S5_STATE, :] = st[:S5_STATE]
        si_ref[g * S5_STATE:(g + 1) * S5_STATE, :] = st[S5_STATE:]

    srows = LANE_GROUPS * S5_STATE
    lane = lax.broadcasted_iota(jnp.int32, (srows, n_chunks), 1)
    tab_r = jnp.concatenate([tab_ref[q, 0] for q in range(LANE_PAIRS)], axis=0)
    tab_i = jnp.concatenate([tab_ref[q, 1] for q in range(LANE_PAIRS)], axis=0)
    fin_r = jnp.zeros((srows, n_chunks), F32)
    fin_i = jnp.zeros((srows, n_chunks), F32)
    for n in range(nb):
        cols = slice(n * n_chunks, (n + 1) * n_chunks)
        xr = sr_ref[:, cols]
        xi = si_ref[:, cols]
        for k in range(scan_steps):
            d = 1 << k
            keep = lane >= d
            sr = jnp.where(keep, pltpu.roll(xr, d, 1), 0.0)
            si = jnp.where(keep, pltpu.roll(xi, d, 1), 0.0)
            pr = tab_r[:, SCAN_ROW0 + k:SCAN_ROW0 + k + 1]
            pi = tab_i[:, SCAN_ROW0 + k:SCAN_ROW0 + k + 1]
            xr, xi = xr + (pr * sr - pi * si), xi + (pr * si + pi * sr)
        last = lane == n_chunks - 1
        fin_r = fin_r + jnp.where(
            lane == n, jnp.sum(jnp.where(last, xr, 0.0), axis=1, keepdims=True), 0.0)
        fin_i = fin_i + jnp.where(
            lane == n, jnp.sum(jnp.where(last, xi, 0.0), axis=1, keepdims=True), 0.0)
        keep = lane >= 1
        sr_ref[:, cols] = jnp.where(keep, pltpu.roll(xr, 1, 1), 0.0)
        si_ref[:, cols] = jnp.where(keep, pltpu.roll(xi, 1, 1), 0.0)

    for g in range(LANE_GROUPS):
        rows = slice(g * S5_STATE, (g + 1) * S5_STATE)
        hf_ref[g, 0] = fin_r[rows]
        hf_ref[g, 1] = fin_i[rows]
        hs = jnp.concatenate([sr_ref[rows, :], si_ref[rows, :]], axis=0).astype(BF16)
        yt_ref[g] = jax.nn.gelu(_dot(wc_ref[g], hs) + _dot(wt_ref[g], ut_ref[g]))

    for n in range(nb):
        for s in range(CHUNK):
            blk = yt_ref[:, s * S5_CH:(s + 1) * S5_CH, n * n_chunks:(n + 1) * n_chunks]
            y_ref[n, pl.ds(s, n_chunks, stride=CHUNK), :] = blk.reshape(LANES, n_chunks).T


def _s5_seq(z, w_t, w_s, w_c, tab, d_s5, cast_w):
    nb, t_len, _ = z.shape
    n_chunks = t_len // CHUNK
    assert n_chunks == LANES, "the chunk axis must fill one 128-lane tile"
    groups = d_s5 // S5_CH
    steps = groups // LANE_GROUPS
    cast_blk = pl.BlockSpec((cast_w.shape[0] // steps, cast_w.shape[1]), lambda i: (i, 0))
    kern = functools.partial(_s5_seq_kernel, nb=nb, n_chunks=n_chunks)
    blk3 = lambda i: (i, 0, 0)
    blk4 = lambda i: (i, 0, 0, 0)
    return pl.pallas_call(
        kern,
        out_shape=(jax.ShapeDtypeStruct((nb, t_len, d_s5), F32),
                   jax.ShapeDtypeStruct((groups, 2, S5_STATE, LANES), F32),
                   jax.ShapeDtypeStruct(cast_w.shape, BF16)),
        grid=(steps,),
        in_specs=[
            pl.BlockSpec((nb, t_len, LANES), lambda i: (0, 0, i)),
            pl.BlockSpec((LANE_GROUPS, ROWS, ROWS), blk3),
            pl.BlockSpec((LANE_GROUPS, 2 * S5_STATE, ROWS), blk3),
            pl.BlockSpec((LANE_GROUPS, ROWS, 2 * S5_STATE), blk3),
            pl.BlockSpec((LANE_PAIRS, 2, LANES, LANES), blk4),
            cast_blk,
        ],
        out_specs=(pl.BlockSpec((nb, t_len, LANES), lambda i: (0, 0, i)),
                   pl.BlockSpec((LANE_GROUPS, 2, S5_STATE, LANES), blk4),
                   cast_blk),
        scratch_shapes=[pltpu.VMEM((LANE_GROUPS, ROWS, nb * n_chunks), BF16),
                        pltpu.VMEM((LANE_GROUPS, ROWS, nb * n_chunks), F32),
                        pltpu.VMEM((LANE_GROUPS * S5_STATE, nb * n_chunks), F32),
                        pltpu.VMEM((LANE_GROUPS * S5_STATE, nb * n_chunks), F32)],
        compiler_params=_params("parallel"),
        name="s5_seq",
    )(z, w_t, w_s, w_c, tab, cast_w)


def _s5_step_kernel(u_ref, h0r_ref, h0i_ref, wb_ref, wct_ref, arow_ref, d_ref,
                    y_ref, hr_ref, hi_ref):
    u = u_ref[...]
    bu = _dot(u.astype(BF16), wb_ref[...])
    y = d_ref[...] * u
    for q in range(LANE_PAIRS):
        cols = slice(q * LANES, (q + 1) * LANES)
        ar = arow_ref[q, 0, 1:2, :]
        ai = arow_ref[q, 1, 1:2, :]
        h0r, h0i = h0r_ref[:, cols], h0i_ref[:, cols]
        hr = ar * h0r - ai * h0i + bu[:, 2 * q * LANES:(2 * q + 1) * LANES]
        hi = ar * h0i + ai * h0r + bu[:, (2 * q + 1) * LANES:(2 * q + 2) * LANES]
        hr_ref[:, cols] = hr
        hi_ref[:, cols] = hi
        y = y + _dot_nt(jnp.concatenate([hr, hi], axis=1).astype(BF16), wct_ref[q])
    y_ref[...] = jax.nn.gelu(y).astype(BF16)


def _s5_step(z, h0r, h0i, wb, wct, arow, d):
    n = z.shape[0]
    slabs = wb.shape[0]
    sw = LANE_PAIRS * LANES
    col = lambda i: (0, i)
    blk3 = lambda i: (i, 0, 0)
    blk4 = lambda i: (i, 0, 0, 0)
    return pl.pallas_call(
        _s5_step_kernel,
        out_shape=(jax.ShapeDtypeStruct((n, slabs * LANES), BF16),
                   jax.ShapeDtypeStruct((n, slabs * sw), F32),
                   jax.ShapeDtypeStruct((n, slabs * sw), F32)),
        grid=(slabs,),
        in_specs=[
            pl.BlockSpec((n, LANES), col),
            pl.BlockSpec((n, sw), col),
            pl.BlockSpec((n, sw), col),
            pl.BlockSpec((None, LANES, 2 * sw), blk3),
            pl.BlockSpec((None, LANE_PAIRS, LANES, 2 * LANES), blk4),
            pl.BlockSpec((LANE_PAIRS, 2, SUBLANES, LANES), blk4),
            pl.BlockSpec((1, LANES), col),
        ],
        out_specs=(pl.BlockSpec((n, LANES), col),
                   pl.BlockSpec((n, sw), col),
                   pl.BlockSpec((n, sw), col)),
        compiler_params=_params("parallel"),
        name="s5_step",
    )(z, h0r, h0i, wb, wct, arow, d)


def _lru_gate_block(xc, wa, ba, wi, bi, lam):
    xb16 = xc.astype(BF16)
    r = jax.nn.sigmoid(_dot(xb16, wa) + ba)
    ig = jax.nn.sigmoid(_dot(xb16, wi) + bi)
    log_a = -LRU_C * r * jax.nn.softplus(-lam)
    a = jnp.exp(log_a)
    mult = jnp.sqrt(1.0 - a * a)
    return a, mult * (ig * xc)


def _lru_gates(xc, wa_ref, ba, wi_ref, bi, lam):
    blk = xc.shape[1] // LRU_HEADS
    parts = [_lru_gate_block(xc[:, h * blk:(h + 1) * blk], wa_ref[h], ba[:, h * blk:(h + 1) * blk],
                             wi_ref[h], bi[:, h * blk:(h + 1) * blk],
                             lam[:, h * blk:(h + 1) * blk]) for h in range(LRU_HEADS)]
    return (jnp.concatenate([p[0] for p in parts], axis=-1),
            jnp.concatenate([p[1] for p in parts], axis=-1))


def _lru_seq_kernel(xb_ref, gb_ref, cw_ref, cb_ref, wa_ref, ba_ref, wi_ref, bi_ref, lam_ref,
                    gm_ref, cast_ref, o_ref, hl_ref, castb_ref, xe_ref, a_ref, b_ref, hc_ref,
                    *, tt):
    castb_ref[...] = cast_ref[...].astype(BF16)
    halo = SUBLANES
    nseq, _, d = xb_ref.shape
    blk = d // LRU_HEADS

    @pl.when(pl.program_id(0) == 0)
    def _():
        xe_ref[:, 0:halo, :] = jnp.zeros((nseq, halo, d), F32)
        hc_ref[...] = jnp.zeros_like(hc_ref)

    xe_ref[:, halo:halo + tt, :] = xb_ref[...]
    for h in range(LRU_HEADS):
        cols = slice(h * blk, (h + 1) * blk)
        cw = cw_ref[:, cols]
        xc = cb_ref[:, cols] + xe_ref[:, halo:halo + tt, cols] * cw[CONV_W - 1:CONV_W, :]
        for k in range(1, CONV_W):
            xc = xc + xe_ref[:, halo - k:halo - k + tt, cols] * cw[CONV_W - 1 - k:CONV_W - k, :]
        a, b = _lru_gate_block(xc.reshape(nseq * tt, blk), wa_ref[h], ba_ref[:, cols],
                               wi_ref[h], bi_ref[:, cols], lam_ref[:, cols])
        a_ref[:, :, cols] = a.reshape(nseq, tt, blk)
        b_ref[:, :, cols] = b.reshape(nseq, tt, blk)
    xe_ref[:, 0:halo, :] = xb_ref[:, tt - halo:tt, :]

    def block(i, h):
        base = pl.multiple_of(i * SUBLANES, SUBLANES)
        for j in range(SUBLANES):
            row = pl.ds(base + j, 1)
            h = a_ref[:, row, :] * h + b_ref[:, row, :]
            b_ref[:, row, :] = h
        return h

    h = lax.fori_loop(0, tt // SUBLANES, block, hc_ref[...])
    hc_ref[...] = h
    hl_ref[...] = h

    out = b_ref[...] * jax.nn.gelu(gb_ref[...])
    o_ref[...] = _rms(out, gm_ref[...]).astype(BF16)


def _lru_seq(z, conv_w, conv_b, w_a, b_a, w_i, b_i, lam, g_merge, cast_w, tt=64):
    nseq, t_len, _ = z.shape
    d = conv_w.shape[1]
    steps = t_len // tt
    cast_blk = pl.BlockSpec((cast_w.shape[0] // steps, cast_w.shape[1]), lambda t: (t, 0))
    row = lambda v: v.reshape(1, d)
    const2 = lambda t: (0, 0)
    const3 = lambda t: (0, 0, 0)
    kern = functools.partial(_lru_seq_kernel, tt=tt)
    return pl.pallas_call(
        kern,
        out_shape=(jax.ShapeDtypeStruct((nseq, t_len, d), BF16),
                   jax.ShapeDtypeStruct((nseq, 1, d), F32),
                   jax.ShapeDtypeStruct(cast_w.shape, BF16)),
        grid=(steps,),
        in_specs=[
            pl.BlockSpec((nseq, tt, d), lambda t: (0, t, 1)),
            pl.BlockSpec((nseq, tt, d), lambda t: (0, t, 2)),
            pl.BlockSpec((CONV_W, d), const2),
            pl.BlockSpec((1, d), const2),
            pl.BlockSpec(w_a.shape, const3),
            pl.BlockSpec((1, d), const2),
            pl.BlockSpec(w_i.shape, const3),
            pl.BlockSpec((1, d), const2),
            pl.BlockSpec((1, d), const2),
            pl.BlockSpec((1, d), const2),
            cast_blk,
        ],
        out_specs=(pl.BlockSpec((nseq, tt, d), lambda t: (0, t, 0)),
                   pl.BlockSpec((nseq, 1, d), const3),
                   cast_blk),
        scratch_shapes=[pltpu.VMEM((nseq, tt + SUBLANES, d), F32),
                        pltpu.VMEM((nseq, tt, d), F32),
                        pltpu.VMEM((nseq, tt, d), F32),
                        pltpu.VMEM((nseq, 1, d), F32)],
        compiler_params=_params("arbitrary"),
        name="rglru_seq",
    )(z, z, conv_w, row(conv_b), w_a, row(b_a), w_i, row(b_i), row(lam), row(g_merge), cast_w)


def _lru_step_kernel(xb_ref, gb_ref, c0_ref, c1_ref, c2_ref, h0_ref, cw_ref, cb_ref,
                     wa_ref, ba_ref, wi_ref, bi_ref, lam_ref, gm_ref, o_ref, h_ref):
    xb = xb_ref[...]
    cw = cw_ref[...]
    xc = (cb_ref[...] + c0_ref[...] * cw[0:1, :] + c1_ref[...] * cw[1:2, :]
          + c2_ref[...] * cw[2:3, :] + xb * cw[3:4, :])
    a, b = _lru_gates(xc, wa_ref, ba_ref[...], wi_ref, bi_ref[...], lam_ref[...])
    h = a * h0_ref[...] + b
    h_ref[...] = h
    out = h * jax.nn.gelu(gb_ref[...])
    o_ref[...] = _rms(out, gm_ref[...]).astype(BF16)


def _lru_step(z, conv0, h0, conv_w, conv_b, w_a, b_a, w_i, b_i, lam, g_merge):
    n = z.shape[0]
    d = conv_w.shape[1]
    row = lambda v: v.reshape(1, d)
    full = lambda shape: pl.BlockSpec(shape, lambda i: (0,) * len(shape))
    nd = full((n, d))
    rd = full((1, d))
    return pl.pallas_call(
        _lru_step_kernel,
        out_shape=(jax.ShapeDtypeStruct((n, d), BF16), jax.ShapeDtypeStruct((n, d), F32)),
        grid=(1,),
        in_specs=[
            pl.BlockSpec((n, d), lambda i: (0, 1)),
            pl.BlockSpec((n, d), lambda i: (0, 2)),
            nd, nd, nd, nd,
            full((CONV_W, d)), rd,
            full(w_a.shape), rd, full(w_i.shape), rd, rd, rd,
        ],
        out_specs=(nd, nd),
        compiler_params=_params("arbitrary"),
        name="rglru_step",
    )(z, z, conv0[:, 0], conv0[:, 1], conv0[:, 2], h0, conv_w, row(conv_b),
      w_a, row(b_a), w_i, row(b_i), row(lam), row(g_merge))


def _glu_rows(y, w_ref, b_ref, g_ref):
    gate = jax.nn.sigmoid(_dot(y.astype(BF16), w_ref[...]) + b_ref[...])
    return _rms(y.astype(F32) * gate, g_ref[...]).astype(BF16)


def _glu_kernel(y_ref, ys_ref, w_ref, b_ref, g_ref, o_ref, os_ref):
    for rows in _row_parts(y_ref.shape[0], 4):
        o_ref[rows, :] = _glu_rows(y_ref[rows, :], w_ref, b_ref, g_ref)

    @_on_last_row_tile
    def _():
        os_ref[...] = _glu_rows(ys_ref[...], w_ref, b_ref, g_ref)


def _glu(y, ys, w, b, g, tm):
    m, d = y.shape
    ms = ys.shape[0]
    const = lambda i: (0, 0)
    return pl.pallas_call(
        _glu_kernel,
        out_shape=(jax.ShapeDtypeStruct((m, d), BF16), jax.ShapeDtypeStruct((ms, d), BF16)),
        grid=(m // tm,),
        in_specs=[
            pl.BlockSpec((tm, d), lambda i: (i, 0)),
            pl.BlockSpec((ms, d), const),
            pl.BlockSpec((d, d), const),
            pl.BlockSpec((1, d), const),
            pl.BlockSpec((1, d), const),
        ],
        out_specs=(pl.BlockSpec((tm, d), lambda i: (i, 0)), pl.BlockSpec((ms, d), const)),
        compiler_params=_params("arbitrary"),
        name="s5_glu",
    )(y, ys, w, b.reshape(1, d), g.reshape(1, d))


def _out_proj_kernel(x_ref, ma_ref, mb_ref, xs_ref, mas_ref, mbs_ref, w_ref, o_ref, os_ref, wb_ref):
    @pl.when(pl.program_id(0) == 0)
    def _():
        wb_ref[...] = w_ref[...].astype(BF16)

    ka = ma_ref.shape[1]
    proj = lambda x, ma, mb: x + _dot(ma, wb_ref[:ka, :]) + _dot(mb, wb_ref[ka:, :])
    for rows in _row_parts(x_ref.shape[0], 2):
        o_ref[rows, :] = proj(x_ref[rows, :], ma_ref[rows, :], mb_ref[rows, :])

    @_on_last_row_tile
    def _():
        os_ref[...] = proj(xs_ref[...], mas_ref[...], mbs_ref[...])


def _out_proj(x, ma, mb, xs, mas, mbs, w, tm):
    m, n = x.shape
    ms = xs.shape[0]
    ka = ma.shape[1]
    kb = mb.shape[1]
    const = lambda i: (0, 0)
    tile = lambda i: (i, 0)
    return pl.pallas_call(
        _out_proj_kernel,
        out_shape=(jax.ShapeDtypeStruct((m, n), F32), jax.ShapeDtypeStruct((ms, n), F32)),
        grid=(m // tm,),
        in_specs=[
            pl.BlockSpec((tm, n), tile),
            pl.BlockSpec((tm, ka), tile),
            pl.BlockSpec((tm, kb), tile),
            pl.BlockSpec((ms, n), const),
            pl.BlockSpec((ms, ka), const),
            pl.BlockSpec((ms, kb), const),
            pl.BlockSpec((ka + kb, n), const, pipeline_mode=pl.Buffered(1)),
        ],
        out_specs=(pl.BlockSpec((tm, n), tile), pl.BlockSpec((ms, n), const)),
        scratch_shapes=[pltpu.VMEM((ka + kb, n), BF16)],
        compiler_params=_params("arbitrary"),
        name="out_proj",
    )(x, ma, mb, xs, mas, mbs, w)


def _mlp_kernel(x_ref, xs_ref, g_ref, wu_ref, wd_ref, o_ref, os_ref, h_ref, hs_ref):
    def rows(x_ref, o_ref, h_ref):
        @pl.when(pl.program_id(1) == 0)
        def _():
            x = x_ref[...]
            h_ref[...] = _rms(x, g_ref[...]).astype(BF16)
            o_ref[...] = x

        up = _dot(h_ref[...], wu_ref[...])
        act = jnp.square(jnp.maximum(up, 0.0)).astype(BF16)
        o_ref[...] += _dot(act, wd_ref[...])

    rows(x_ref, o_ref, h_ref)
    _on_last_row_tile(lambda: rows(xs_ref, os_ref, hs_ref))


def _mlp(x, xs, g, w_up, w_down, tm, tf):
    m, d = x.shape
    ms = xs.shape[0]
    f = w_up.shape[1]
    const = lambda i, j: (0, 0)
    return pl.pallas_call(
        _mlp_kernel,
        out_shape=(jax.ShapeDtypeStruct((m, d), F32), jax.ShapeDtypeStruct((ms, d), F32)),
        grid=(m // tm, f // tf),
        in_specs=[
            pl.BlockSpec((tm, d), lambda i, j: (i, 0)),
            pl.BlockSpec((ms, d), const),
            pl.BlockSpec((1, d), const),
            pl.BlockSpec((d, tf), lambda i, j: (0, j)),
            pl.BlockSpec((tf, d), lambda i, j: (j, 0)),
        ],
        out_specs=(pl.BlockSpec((tm, d), lambda i, j: (i, 0)), pl.BlockSpec((ms, d), const)),
        scratch_shapes=[pltpu.VMEM((tm, d), BF16), pltpu.VMEM((ms, d), BF16)],
        compiler_params=_params("arbitrary", "arbitrary"),
        name="mlp",
    )(x, xs, g.reshape(1, d), w_up, w_down)


def _ple_kernel(x_ref, p_ref, xs_ref, ps_ref, g_ref, wg_ref, wp_ref, gf_ref, o_ref, os_ref,
                wgb_ref, wpb_ref, *, final):
    @pl.when(pl.program_id(0) == 0)
    def _():
        wgb_ref[...] = wg_ref[...].astype(BF16)
        wpb_ref[...] = wp_ref[...].astype(BF16)

    def ple(x, p):
        gate = jax.nn.sigmoid(_dot(_rms(x, g_ref[...]).astype(BF16), wgb_ref[...]))
        pe = _dot(p.astype(BF16), wpb_ref[...])
        x = x + pe * gate
        return _rms(x, gf_ref[...]) if final else x

    for rows in _row_parts(x_ref.shape[0], 2):
        o_ref[rows, :] = ple(x_ref[rows, :], p_ref[rows, :])

    @_on_last_row_tile
    def _():
        os_ref[...] = ple(xs_ref[...], ps_ref[...])


def _ple(x, p, xs, ps, g, w_gate, w_ple, g_final, tm, final):
    m, d = x.shape
    ms = xs.shape[0]
    dp = p.shape[1]
    once = pl.Buffered(1)
    const = lambda i: (0, 0)
    tile = lambda i: (i, 0)
    return pl.pallas_call(
        functools.partial(_ple_kernel, final=final),
        out_shape=(jax.ShapeDtypeStruct((m, d), F32), jax.ShapeDtypeStruct((ms, d), F32)),
        grid=(m // tm,),
        in_specs=[
            pl.BlockSpec((tm, d), tile),
            pl.BlockSpec((tm, dp), tile),
            pl.BlockSpec((ms, d), const),
            pl.BlockSpec((ms, dp), const),
            pl.BlockSpec((1, d), const),
            pl.BlockSpec((d, d), const, pipeline_mode=once),
            pl.BlockSpec((dp, d), const, pipeline_mode=once),
            pl.BlockSpec((1, d), const),
        ],
        out_specs=(pl.BlockSpec((tm, d), tile), pl.BlockSpec((ms, d), const)),
        scratch_shapes=[pltpu.VMEM((d, d), BF16), pltpu.VMEM((dp, d), BF16)],
        compiler_params=_params("arbitrary"),
        name="ple_final",
    )(x, p, xs, ps, g.reshape(1, d), w_gate, w_ple, g_final.reshape(1, d))


TM_IN, TN_IN = 1024, 1024
TM_GLU = 1024
TM_OUT = 512
TM_MLP, TF_MLP = 512, 1024
TM_PLE = 256


def kernel(x_prompt, x_sample, state_s5_re, state_s5_im, state_lru, state_conv, p_prompt, p_sample,
           g_mix, w_in, s5_lam_re, s5_lam_im, s5_log_step, s5_b_re, s5_b_im, s5_c_re, s5_c_im, s5_d,
           s5_w_glu, s5_b_glu, conv_w, conv_b, lru_w_a, lru_b_a, lru_w_i, lru_b_i, lru_lam,
           g_merge_a, g_merge_b, w_out, g_mlp, w_up, w_down, g_ple, w_ple_gate, w_ple, g_final):
    depth = g_mix.shape[0]
    nb, t_len, d_model = x_prompt.shape
    ns = x_sample.shape[0]
    d_s5 = s5_d.shape[1]
    d_lru = conv_w.shape[2]
    groups = d_s5 // S5_CH

    xp = x_prompt.reshape(nb * t_len, d_model)
    xs = x_sample.reshape(ns, d_model)
    outs = [[] for _ in range(8)]
    for l in range(depth):
        final = l == depth - 1
        w_glu_b = s5_w_glu[l].astype(BF16)
        lru_args = (conv_w[l], conv_b[l], lru_w_a[l].astype(BF16), lru_b_a[l],
                    lru_w_i[l].astype(BF16), lru_b_i[l], lru_lam[l], g_merge_b[l])
        w_t, w_s, w_c, tab, wb, wct, arow, w_in_b = _s5_params(
            s5_lam_re[l], s5_lam_im[l], s5_log_step[l], s5_b_re[l], s5_b_im[l],
            s5_c_re[l], s5_c_im[l], s5_d[l], w_in[l])

        z, zs = _norm_matmul(xp, xs, g_mix[l], w_in_b, TM_IN, TN_IN)
        z4 = z.reshape(nb, t_len, 3 * d_s5)
        y, hf, w_down_b = _s5_seq(z4, w_t, w_s, w_c, tab, d_s5, w_down[l])
        mb, lru_h, w_up_b = _lru_seq(z4, *lru_args, w_up[l])
        ys, hsr, hsi = _s5_step(zs, state_s5_re[l].reshape(ns, groups * S5_STATE),
                                state_s5_im[l].reshape(ns, groups * S5_STATE),
                                wb, wct, arow, s5_d[l].reshape(1, d_s5))
        mbs, lru_hs = _lru_step(zs, state_conv[l], state_lru[l], *lru_args)
        ma, mas = _glu(y.reshape(nb * t_len, d_s5), ys, w_glu_b, s5_b_glu[l], g_merge_a[l], TM_GLU)
        xp, xs = _out_proj(xp, ma, mb.reshape(nb * t_len, d_lru), xs, mas, mbs, w_out[l], TM_OUT)
        xp, xs = _mlp(xp, xs, g_mlp[l], w_up_b, w_down_b, TM_MLP, TF_MLP)
        xp, xs = _ple(xp, p_prompt[l].reshape(nb * t_len, -1), xs, p_sample[l].reshape(ns, -1),
                      g_ple[l], w_ple_gate[l], w_ple[l], g_final, TM_PLE, final)
        hf = jnp.transpose(hf[:, :, :, :nb], (1, 3, 0, 2))
        outs[0].append(hf[0])
        outs[1].append(hf[1])
        outs[2].append(lru_h.reshape(nb, d_lru))
        outs[3].append(z4[:, t_len - (CONV_W - 1):, d_s5:d_s5 + d_lru])
        outs[4].append(hsr.reshape(ns, groups, S5_STATE))
        outs[5].append(hsi.reshape(ns, groups, S5_STATE))
        outs[6].append(lru_hs)
        outs[7].append(jnp.concatenate(
            [state_conv[l][:, 1:], zs[:, None, d_s5:d_s5 + d_lru]], axis=1))
    return (xp.reshape(nb, t_len, d_model), xs.reshape(ns, 1, d_model),
            *(jnp.stack(o) for o in outs))
```

```python
import functools

import jax
import jax.numpy as jnp
from jax import lax
from jax.experimental import pallas as pl
from jax.experimental.pallas import tpu as pltpu

F32 = jnp.float32
BF16 = jnp.bfloat16
HIGHEST = lax.Precision.HIGHEST

EPS = 1e-6
LRU_C = 8.0
S5_CH = 16
S5_STATE = 64
LRU_HEADS = 4
CONV_W = 4
CHUNK = 16
SUBLANES = 8
LANES = 128
LANE_GROUPS = LANES // S5_CH
LANE_PAIRS = LANE_GROUPS // 2
ROWS = CHUNK * S5_CH
POW_ROWS = 32
SCAN_ROW0 = POW_ROWS
VMEM_LIMIT = 56 * 1024 * 1024


def _params(*sem):
    return pltpu.CompilerParams(dimension_semantics=sem, vmem_limit_bytes=VMEM_LIMIT)


def _rms(x, g):
    return x * lax.rsqrt(jnp.mean(x * x, axis=-1, keepdims=True) + EPS) * g


def _dot(a, b):
    return jnp.dot(a, b, preferred_element_type=F32)


def _dot_nt(a, b, precision=None):
    return lax.dot_general(a, b, (((1,), (1,)), ((), ())), precision=precision,
                           preferred_element_type=F32)


def _transpose_tiles(x):
    r, c = x.shape
    return jnp.concatenate(
        [jnp.concatenate([x[i:i + LANES, j:j + LANES].T for i in range(0, r, LANES)], axis=1)
         for j in range(0, c, LANES)], axis=0)


def _row_parts(tm, want):
    parts = want if tm % (want * 16) == 0 else 1
    step = tm // parts
    return [slice(r * step, (r + 1) * step) for r in range(parts)]


def _on_last_row_tile(fn):
    pl.when(pl.program_id(0) == pl.num_programs(0) - 1)(fn)


def _norm_matmul_kernel(x_ref, xs_ref, g_ref, w_ref, o_ref, os_ref):
    tn = o_ref.shape[1]
    w = w_ref[:, pl.ds(pl.multiple_of(pl.program_id(1) * tn, tn), tn)]
    for rows in _row_parts(x_ref.shape[0], 4):
        h = _rms(x_ref[rows, :], g_ref[...]).astype(BF16)
        o_ref[rows, :] = _dot(h, w)

    @_on_last_row_tile
    def _():
        os_ref[...] = _dot(_rms(xs_ref[...], g_ref[...]).astype(BF16), w)


def _norm_matmul(x, xs, g, w, tm, tn):
    m, k = x.shape
    ms = xs.shape[0]
    n = w.shape[1]
    ni = m // tm
    return pl.pallas_call(
        _norm_matmul_kernel,
        out_shape=(jax.ShapeDtypeStruct((m, n), F32), jax.ShapeDtypeStruct((ms, n), F32)),
        grid=(ni, n // tn),
        in_specs=[
            pl.BlockSpec((tm, k), lambda i, j: (i, 0)),
            pl.BlockSpec((ms, k), lambda i, j: (0, 0)),
            pl.BlockSpec((1, k), lambda i, j: (0, 0)),
            pl.BlockSpec((k, n), lambda i, j: (0, 0), pipeline_mode=pl.Buffered(1)),
        ],
        out_specs=(pl.BlockSpec((tm, tn), lambda i, j: (i, j)),
                   pl.BlockSpec((ms, tn), lambda i, j: (0, jnp.where(i == ni - 1, j, 0)))),
        compiler_params=_params("arbitrary", "arbitrary"),
        name="norm_w_in",
    )(x, xs, g.reshape(1, k), w)


def _s5_param_kernel(lr_ref, li_ref, ls_ref, b_r_ref, b_i_ref, c_r_ref, c_i_ref, d_ref, cast_ref,
                     wt_ref, ws_ref, wc_ref, tab_ref, wb_ref, wct_ref, arow_ref, castb_ref):
    castb_ref[...] = cast_ref[...].astype(BF16)
    row_p = lax.broadcasted_iota(jnp.int32, (POW_ROWS, LANES), 0)
    lane = lax.broadcasted_iota(jnp.int32, (ROWS, LANES), 1)
    lane16 = lax.broadcasted_iota(jnp.int32, (S5_CH, LANES), 1)
    row16 = lax.broadcasted_iota(jnp.int32, (S5_CH, LANES), 0)
    lo = lane < S5_STATE
    eye = (lax.broadcasted_iota(jnp.int32, (S5_CH, S5_CH), 0)
           == lax.broadcasted_iota(jnp.int32, (S5_CH, S5_CH), 1)).astype(F32)

    wb_ref[...] = jnp.zeros_like(wb_ref)
    wct_ref[...] = jnp.zeros_like(wct_ref)

    for q in range(LANE_PAIRS):
        lr, li = lr_ref[q], li_ref[q]
        step = jnp.exp(ls_ref[q])
        mag = jnp.exp(lr * step)
        ar = mag * jnp.cos(li * step)
        ai = mag * jnp.sin(li * step)
        nr, ni = ar - 1.0, ai
        den = lr * lr + li * li
        cr = (nr * lr + ni * li) / den
        ci = (ni * lr - nr * li) / den
        pair = lambda f, ref: jnp.concatenate([f(ref[2 * q]), f(ref[2 * q + 1])], axis=1)
        bt_r = pair(lambda b: _dot_nt(eye, b, HIGHEST), b_r_ref)
        bt_i = pair(lambda b: _dot_nt(eye, b, HIGHEST), b_i_ref)
        bb_r = cr * bt_r - ci * bt_i
        bb_i = cr * bt_i + ci * bt_r
        c_r = pair(lambda c: c, c_r_ref)
        c_i = pair(lambda c: c, c_i_ref)

        pr = jnp.ones((POW_ROWS, LANES), F32)
        pi = jnp.zeros((POW_ROWS, LANES), F32)
        sr, si = ar, ai
        sq = []
        for m in range(POW_ROWS.bit_length() - 1 + SUBLANES):
            sq.append((sr, si))
            if (1 << m) < POW_ROWS:
                bit = ((row_p >> m) & 1) == 1
                pr, pi = (jnp.where(bit, pr * sr - pi * si, pr),
                          jnp.where(bit, pr * si + pi * sr, pi))
            sr, si = sr * sr - si * si, 2.0 * sr * si
        m0 = CHUNK.bit_length() - 1
        scan_r = jnp.concatenate([sq[m0 + j][0] for j in range(SUBLANES)], axis=0)
        scan_i = jnp.concatenate([sq[m0 + j][1] for j in range(SUBLANES)], axis=0)
        pad = jnp.zeros((LANES - POW_ROWS - SUBLANES, LANES), F32)
        tab_ref[q, 0] = jnp.concatenate([pr, scan_r, pad], axis=0).T
        tab_ref[q, 1] = jnp.concatenate([pi, scan_i, pad], axis=0).T
        arow_ref[q, 0] = pr[:SUBLANES]
        arow_ref[q, 1] = pi[:SUBLANES]

        def expand(tab, k0, sign):
            return jnp.concatenate(
                [jnp.broadcast_to(tab[k0 + sign * s:k0 + sign * s + 1, :], (S5_CH, LANES))
                 for s in range(CHUNK)], axis=0)

        tile = lambda v: jnp.concatenate([v] * CHUNK, axis=0)

        e_r, e_i = expand(pr, CHUNK - 1, -1), expand(pi, CHUNK - 1, -1)
        t_r, t_i = tile(bb_r), tile(bb_i)
        ws_r = _transpose_tiles(t_r * e_r - t_i * e_i)
        ws_i = _transpose_tiles(t_r * e_i + t_i * e_r)
        for h in range(2):
            rows = slice(h * S5_STATE, (h + 1) * S5_STATE)
            ws_ref[2 * q + h] = jnp.concatenate([ws_r[rows], ws_i[rows]], axis=0).astype(BF16)

        e_r, e_i = expand(pr, 1, 1), expand(pi, 1, 1)
        t_r, t_i = tile(c_r), tile(c_i)
        ca_r = t_r * e_r - t_i * e_i
        ca_n = -(t_r * e_i + t_i * e_r)
        wc_ref[2 * q] = jnp.where(lo, ca_r, pltpu.roll(ca_n, S5_STATE, 1)).astype(BF16)
        wc_ref[2 * q + 1] = jnp.where(lo, pltpu.roll(ca_r, S5_STATE, 1), ca_n).astype(BF16)

        e_r, e_i = expand(pr, 0, 1), expand(pi, 0, 1)
        c0_r = t_r * e_r - t_i * e_i
        c0_i = t_r * e_i + t_i * e_r

        for h in range(2):
            g = 2 * q + h
            mine = (lane16 < S5_STATE) == (h == 0)
            bm_r = jnp.where(mine, bb_r, 0.0)
            bm_i = jnp.where(mine, bb_i, 0.0)
            kt = _dot_nt(bm_r, c0_r, HIGHEST) - _dot_nt(bm_i, c0_i, HIGHEST)
            d_col = jnp.sum(eye * d_ref[g], axis=1, keepdims=True)
            k_lo = kt[:, :LANES] + jnp.where(row16 == lane16, d_col, 0.0)
            k_hi = kt[:, LANES:]
            blocks = []
            half = LANES // S5_CH
            for s in range(CHUNK):
                sh = (s % half) * S5_CH
                keep = lane16 >= sh
                r_lo = pltpu.roll(k_lo, sh, 1) if sh else k_lo
                r_hi = pltpu.roll(k_hi, sh, 1) if sh else k_hi
                if s < half:
                    blk = jnp.concatenate([jnp.where(keep, r_lo, 0.0),
                                           jnp.where(keep, r_hi, r_lo)], axis=1)
                else:
                    blk = jnp.concatenate([jnp.zeros_like(r_lo), jnp.where(keep, r_lo, 0.0)], axis=1)
                blocks.append(blk)
            wt_ref[g] = _transpose_tiles(jnp.concatenate(blocks, axis=0)).astype(BF16)

            r0 = (2 * q + h) * S5_CH
            c0 = q * 2 * LANES
            wb_ref[r0:r0 + S5_CH, c0:c0 + LANES] = bm_r.astype(BF16)
            wb_ref[r0:r0 + S5_CH, c0 + LANES:c0 + 2 * LANES] = bm_i.astype(BF16)
            wct_ref[q, r0:r0 + S5_CH, :LANES] = jnp.where(mine, c_r, 0.0).astype(BF16)
            wct_ref[q, r0:r0 + S5_CH, LANES:] = jnp.where(mine, -c_i, 0.0).astype(BF16)


def _s5_params(lam_re, lam_im, log_step, b_re, b_im, c_re, c_im, d, cast_w):
    groups = lam_re.shape[0]
    pairs = groups // 2
    slabs = groups // LANE_GROUPS
    cast_blk = pl.BlockSpec((cast_w.shape[0] // slabs, cast_w.shape[1]), lambda i: (i, 0))
    pair_row = lambda v: v.reshape(pairs, 1, LANES)
    ls = jnp.broadcast_to(log_step[:, None], (groups, S5_STATE))
    blk3 = lambda i: (i, 0, 0)
    blk4 = lambda i: (i, 0, 0, 0)
    prow = pl.BlockSpec((LANE_PAIRS, 1, LANES), blk3)
    bspec = pl.BlockSpec((LANE_GROUPS, S5_STATE, S5_CH), blk3)
    cspec = pl.BlockSpec((LANE_GROUPS, S5_CH, S5_STATE), blk3)
    return pl.pallas_call(
        _s5_param_kernel,
        out_shape=(jax.ShapeDtypeStruct((groups, ROWS, ROWS), BF16),
                   jax.ShapeDtypeStruct((groups, 2 * S5_STATE, ROWS), BF16),
                   jax.ShapeDtypeStruct((groups, ROWS, 2 * S5_STATE), BF16),
                   jax.ShapeDtypeStruct((pairs, 2, LANES, LANES), F32),
                   jax.ShapeDtypeStruct((slabs, LANES, LANE_PAIRS * 2 * LANES), BF16),
                   jax.ShapeDtypeStruct((slabs, LANE_PAIRS, LANES, 2 * LANES), BF16),
                   jax.ShapeDtypeStruct((pairs, 2, SUBLANES, LANES), F32),
                   jax.ShapeDtypeStruct(cast_w.shape, BF16)),
        grid=(slabs,),
        in_specs=[prow, prow, prow, bspec, bspec, cspec, cspec,
                  pl.BlockSpec((LANE_GROUPS, 1, S5_CH), blk3), cast_blk],
        out_specs=(pl.BlockSpec((LANE_GROUPS, ROWS, ROWS), blk3),
                   pl.BlockSpec((LANE_GROUPS, 2 * S5_STATE, ROWS), blk3),
                   pl.BlockSpec((LANE_GROUPS, ROWS, 2 * S5_STATE), blk3),
                   pl.BlockSpec((LANE_PAIRS, 2, LANES, LANES), blk4),
                   pl.BlockSpec((None, LANES, LANE_PAIRS * 2 * LANES), blk3),
                   pl.BlockSpec((None, LANE_PAIRS, LANES, 2 * LANES), blk4),
                   pl.BlockSpec((LANE_PAIRS, 2, SUBLANES, LANES), blk4),
                   cast_blk),
        compiler_params=_params("parallel"),
        name="s5_params",
    )(pair_row(lam_re), pair_row(lam_im), pair_row(ls), b_re, b_im, c_re, c_im,
      d.reshape(groups, 1, S5_CH), cast_w)


def _s5_seq_kernel(z_ref, wt_ref, ws_ref, wc_ref, tab_ref, cast_ref, y_ref, hf_ref, castb_ref,
                   ut_ref, yt_ref, sr_ref, si_ref, *, nb, n_chunks):
    castb_ref[...] = cast_ref[...].astype(BF16)
    scan_steps = n_chunks.bit_length() - 1
    for n in range(nb):
        for s in range(CHUNK):
            xs = z_ref[n, pl.ds(s, n_chunks, stride=CHUNK), :]
            ut_ref[:, s * S5_CH:(s + 1) * S5_CH, n * n_chunks:(n + 1) * n_chunks] = (
                xs.T.astype(BF16).reshape(LANE_GROUPS, S5_CH, n_chunks))

    for g in range(LANE_GROUPS):
        st = _dot(ws_ref[g], ut_ref[g])
        sr_ref[g * S5_STATE:(g + 1) * S5_STATE, :] = st[:S5_STATE]
        si_ref[g * S5_STATE:(g + 1) * S5_STATE, :] = st[S5_STATE:]

    srows = LANE_GROUPS * S5_STATE
    lane = lax.broadcasted_iota(jnp.int32, (srows, n_chunks), 1)
    tab_r = jnp.concatenate([tab_ref[q, 0] for q in range(LANE_PAIRS)], axis=0)
    tab_i = jnp.concatenate([tab_ref[q, 1] for q in range(LANE_PAIRS)], axis=0)
    fin_r = jnp.zeros((srows, n_chunks), F32)
    fin_i = jnp.zeros((srows, n_chunks), F32)
    for n in range(nb):
        cols = slice(n * n_chunks, (n + 1) * n_chunks)
        xr = sr_ref[:, cols]
        xi = si_ref[:, cols]
        for k in range(scan_steps):
            d = 1 << k
            keep = lane >= d
            sr = jnp.where(keep, pltpu.roll(xr, d, 1), 0.0)
            si = jnp.where(keep, pltpu.roll(xi, d, 1), 0.0)
            pr = tab_r[:, SCAN_ROW0 + k:SCAN_ROW0 + k + 1]
            pi = tab_i[:, SCAN_ROW0 + k:SCAN_ROW0 + k + 1]
            xr, xi = xr + (pr * sr - pi * si), xi + (pr * si + pi * sr)
        last = lane == n_chunks - 1
        fin_r = fin_r + jnp.where(
            lane == n, jnp.sum(jnp.where(last, xr, 0.0), axis=1, keepdims=True), 0.0)
        fin_i = fin_i + jnp.where(
            lane == n, jnp.sum(jnp.where(last, xi, 0.0), axis=1, keepdims=True), 0.0)
        keep = lane >= 1
        sr_ref[:, cols] = jnp.where(keep, pltpu.roll(xr, 1, 1), 0.0)
        si_ref[:, cols] = jnp.where(keep, pltpu.roll(xi, 1, 1), 0.0)

    for g in range(LANE_GROUPS):
        rows = slice(g * S5_STATE, (g + 1) * S5_STATE)
        hf_ref[g, 0] = fin_r[rows]
        hf_ref[g, 1] = fin_i[rows]
        hs = jnp.concatenate([sr_ref[rows, :], si_ref[rows, :]], axis=0).astype(BF16)
        yt_ref[g] = jax.nn.gelu(_dot(wc_ref[g], hs) + _dot(wt_ref[g], ut_ref[g]))

    for n in range(nb):
        for s in range(CHUNK):
            blk = yt_ref[:, s * S5_CH:(s + 1) * S5_CH, n * n_chunks:(n + 1) * n_chunks]
            y_ref[n, pl.ds(s, n_chunks, stride=CHUNK), :] = blk.reshape(LANES, n_chunks).T


def _s5_seq(z, w_t, w_s, w_c, tab, d_s5, cast_w):
    nb, t_len, _ = z.shape
    n_chunks = t_len // CHUNK
    assert n_chunks == LANES, "the chunk axis must fill one 128-lane tile"
    groups = d_s5 // S5_CH
    steps = groups // LANE_GROUPS
    cast_blk = pl.BlockSpec((cast_w.shape[0] // steps, cast_w.shape[1]), lambda i: (i, 0))
    kern = functools.partial(_s5_seq_kernel, nb=nb, n_chunks=n_chunks)
    blk3 = lambda i: (i, 0, 0)
    blk4 = lambda i: (i, 0, 0, 0)
    return pl.pallas_call(
        kern,
        out_shape=(jax.ShapeDtypeStruct((nb, t_len, d_s5), F32),
                   jax.ShapeDtypeStruct((groups, 2, S5_STATE, LANES), F32),
                   jax.ShapeDtypeStruct(cast_w.shape, BF16)),
        grid=(steps,),
        in_specs=[
            pl.BlockSpec((nb, t_len, LANES), lambda i: (0, 0, i)),
            pl.BlockSpec((LANE_GROUPS, ROWS, ROWS), blk3),
            pl.BlockSpec((LANE_GROUPS, 2 * S5_STATE, ROWS), blk3),
            pl.BlockSpec((LANE_GROUPS, ROWS, 2 * S5_STATE), blk3),
            pl.BlockSpec((LANE_PAIRS, 2, LANES, LANES), blk4),
            cast_blk,
        ],
        out_specs=(pl.BlockSpec((nb, t_len, LANES), lambda i: (0, 0, i)),
                   pl.BlockSpec((LANE_GROUPS, 2, S5_STATE, LANES), blk4),
                   cast_blk),
        scratch_shapes=[pltpu.VMEM((LANE_GROUPS, ROWS, nb * n_chunks), BF16),
                        pltpu.VMEM((LANE_GROUPS, ROWS, nb * n_chunks), F32),
                        pltpu.VMEM((LANE_GROUPS * S5_STATE, nb * n_chunks), F32),
                        pltpu.VMEM((LANE_GROUPS * S5_STATE, nb * n_chunks), F32)],
        compiler_params=_params("parallel"),
        name="s5_seq",
    )(z, w_t, w_s, w_c, tab, cast_w)


def _s5_step_kernel(u_ref, h0r_ref, h0i_ref, wb_ref, wct_ref, arow_ref, d_ref,
                    y_ref, hr_ref, hi_ref):
    u = u_ref[...]
    bu = _dot(u.astype(BF16), wb_ref[...])
    y = d_ref[...] * u
    for q in range(LANE_PAIRS):
        cols = slice(q * LANES, (q + 1) * LANES)
        ar = arow_ref[q, 0, 1:2, :]
        ai = arow_ref[q, 1, 1:2, :]
        h0r, h0i = h0r_ref[:, cols], h0i_ref[:, cols]
        hr = ar * h0r - ai * h0i + bu[:, 2 * q * LANES:(2 * q + 1) * LANES]
        hi = ar * h0i + ai * h0r + bu[:, (2 * q + 1) * LANES:(2 * q + 2) * LANES]
        hr_ref[:, cols] = hr
        hi_ref[:, cols] = hi
        y = y + _dot_nt(jnp.concatenate([hr, hi], axis=1).astype(BF16), wct_ref[q])
    y_ref[...] = jax.nn.gelu(y).astype(BF16)


def _s5_step(z, h0r, h0i, wb, wct, arow, d):
    n = z.shape[0]
    slabs = wb.shape[0]
    sw = LANE_PAIRS * LANES
    col = lambda i: (0, i)
    blk3 = lambda i: (i, 0, 0)
    blk4 = lambda i: (i, 0, 0, 0)
    return pl.pallas_call(
        _s5_step_kernel,
        out_shape=(jax.ShapeDtypeStruct((n, slabs * LANES), BF16),
                   jax.ShapeDtypeStruct((n, slabs * sw), F32),
                   jax.ShapeDtypeStruct((n, slabs * sw), F32)),
        grid=(slabs,),
        in_specs=[
            pl.BlockSpec((n, LANES), col),
            pl.BlockSpec((n, sw), col),
            pl.BlockSpec((n, sw), col),
            pl.BlockSpec((None, LANES, 2 * sw), blk3),
            pl.BlockSpec((None, LANE_PAIRS, LANES, 2 * LANES), blk4),
            pl.BlockSpec((LANE_PAIRS, 2, SUBLANES, LANES), blk4),
            pl.BlockSpec((1, LANES), col),
        ],
        out_specs=(pl.BlockSpec((n, LANES), col),
                   pl.BlockSpec((n, sw), col),
                   pl.BlockSpec((n, sw), col)),
        compiler_params=_params("parallel"),
        name="s5_step",
    )(z, h0r, h0i, wb, wct, arow, d)


def _lru_gate_block(xc, wa, ba, wi, bi, lam):
    xb16 = xc.astype(BF16)
    r = jax.nn.sigmoid(_dot(xb16, wa) + ba)
    ig = jax.nn.sigmoid(_dot(xb16, wi) + bi)
    log_a = -LRU_C * r * jax.nn.softplus(-lam)
    a = jnp.exp(log_a)
    mult = jnp.sqrt(1.0 - a * a)
    return a, mult * (ig * xc)


def _lru_gates(xc, wa_ref, ba, wi_ref, bi, lam):
    blk = xc.shape[1] // LRU_HEADS
    parts = [_lru_gate_block(xc[:, h * blk:(h + 1) * blk], wa_ref[h], ba[:, h * blk:(h + 1) * blk],
                             wi_ref[h], bi[:, h * blk:(h + 1) * blk],
                             lam[:, h * blk:(h + 1) * blk]) for h in range(LRU_HEADS)]
    return (jnp.concatenate([p[0] for p in parts], axis=-1),
            jnp.concatenate([p[1] for p in parts], axis=-1))


def _lru_seq_kernel(xb_ref, gb_ref, cw_ref, cb_ref, wa_ref, ba_ref, wi_ref, bi_ref, lam_ref,
                    gm_ref, cast_ref, o_ref, hl_ref, castb_ref, xe_ref, a_ref, b_ref, hc_ref,
                    *, tt):
    castb_ref[...] = cast_ref[...].astype(BF16)
    halo = SUBLANES
    nseq, _, d = xb_ref.shape
    blk = d // LRU_HEADS

    @pl.when(pl.program_id(0) == 0)
    def _():
        xe_ref[:, 0:halo, :] = jnp.zeros((nseq, halo, d), F32)
        hc_ref[...] = jnp.zeros_like(hc_ref)

    xe_ref[:, halo:halo + tt, :] = xb_ref[...]
    for h in range(LRU_HEADS):
        cols = slice(h * blk, (h + 1) * blk)
        cw = cw_ref[:, cols]
        xc = cb_ref[:, cols] + xe_ref[:, halo:halo + tt, cols] * cw[CONV_W - 1:CONV_W, :]
        for k in range(1, CONV_W):
            xc = xc + xe_ref[:, halo - k:halo - k + tt, cols] * cw[CONV_W - 1 - k:CONV_W - k, :]
        a, b = _lru_gate_block(xc.reshape(nseq * tt, blk), wa_ref[h], ba_ref[:, cols],
                               wi_ref[h], bi_ref[:, cols], lam_ref[:, cols])
        a_ref[:, :, cols] = a.reshape(nseq, tt, blk)
        b_ref[:, :, cols] = b.reshape(nseq, tt, blk)
    xe_ref[:, 0:halo, :] = xb_ref[:, tt - halo:tt, :]

    def block(i, h):
        base = pl.multiple_of(i * SUBLANES, SUBLANES)
        for j in range(SUBLANES):
            row = pl.ds(base + j, 1)
            h = a_ref[:, row, :] * h + b_ref[:, row, :]
            b_ref[:, row, :] = h
        return h

    h = lax.fori_loop(0, tt // SUBLANES, block, hc_ref[...])
    hc_ref[...] = h
    hl_ref[...] = h

    out = b_ref[...] * jax.nn.gelu(gb_ref[...])
    o_ref[...] = _rms(out, gm_ref[...]).astype(BF16)


def _lru_seq(z, conv_w, conv_b, w_a, b_a, w_i, b_i, lam, g_merge, cast_w, tt=64):
    nseq, t_len, _ = z.shape
    d = conv_w.shape[1]
    steps = t_len // tt
    cast_blk = pl.BlockSpec((cast_w.shape[0] // steps, cast_w.shape[1]), lambda t: (t, 0))
    row = lambda v: v.reshape(1, d)
    const2 = lambda t: (0, 0)
    const3 = lambda t: (0, 0, 0)
    kern = functools.partial(_lru_seq_kernel, tt=tt)
    return pl.pallas_call(
        kern,
        out_shape=(jax.ShapeDtypeStruct((nseq, t_len, d), BF16),
                   jax.ShapeDtypeStruct((nseq, 1, d), F32),
                   jax.ShapeDtypeStruct(cast_w.shape, BF16)),
        grid=(steps,),
        in_specs=[
            pl.BlockSpec((nseq, tt, d), lambda t: (0, t, 1)),
            pl.BlockSpec((nseq, tt, d), lambda t: (0, t, 2)),
            pl.BlockSpec((CONV_W, d), const2),
            pl.BlockSpec((1, d), const2),
            pl.BlockSpec(w_a.shape, const3),
            pl.BlockSpec((1, d), const2),
            pl.BlockSpec(w_i.shape, const3),
            pl.BlockSpec((1, d), const2),
            pl.BlockSpec((1, d), const2),
            pl.BlockSpec((1, d), const2),
            cast_blk,
        ],
        out_specs=(pl.BlockSpec((nseq, tt, d), lambda t: (0, t, 0)),
                   pl.BlockSpec((nseq, 1, d), const3),
                   cast_blk),
        scratch_shapes=[pltpu.VMEM((nseq, tt + SUBLANES, d), F32),
                        pltpu.VMEM((nseq, tt, d), F32),
                        pltpu.VMEM((nseq, tt, d), F32),
                        pltpu.VMEM((nseq, 1, d), F32)],
        compiler_params=_params("arbitrary"),
        name="rglru_seq",
    )(z, z, conv_w, row(conv_b), w_a, row(b_a), w_i, row(b_i), row(lam), row(g_merge), cast_w)


def _lru_step_kernel(xb_ref, gb_ref, c0_ref, c1_ref, c2_ref, h0_ref, cw_ref, cb_ref,
                     wa_ref, ba_ref, wi_ref, bi_ref, lam_ref, gm_ref, o_ref, h_ref, cv_ref):
    d = xb_ref.shape[1]
    xb = xb_ref[...]
    cw = cw_ref[...]
    xc = (cb_ref[...] + c0_ref[...] * cw[0:1, :] + c1_ref[...] * cw[1:2, :]
          + c2_ref[...] * cw[2:3, :] + xb * cw[3:4, :])
    a, b = _lru_gates(xc, wa_ref, ba_ref[...], wi_ref, bi_ref[...], lam_ref[...])
    h = a * h0_ref[...] + b
    h_ref[...] = h
    out = h * jax.nn.gelu(gb_ref[...])
    o_ref[...] = _rms(out, gm_ref[...]).astype(BF16)
    cv_ref[:, 0:d] = c1_ref[...]
    cv_ref[:, d:2 * d] = c2_ref[...]
    cv_ref[:, 2 * d:3 * d] = xb


def _lru_step(z, conv0, h0, conv_w, conv_b, w_a, b_a, w_i, b_i, lam, g_merge):
    n = z.shape[0]
    d = conv_w.shape[1]
    taps = CONV_W - 1
    conv2d = conv0.reshape(n, taps * d)
    row = lambda v: v.reshape(1, d)
    full = lambda shape: pl.BlockSpec(shape, lambda i: (0,) * len(shape))
    col = lambda c: pl.BlockSpec((n, d), lambda i: (0, c))
    nd = full((n, d))
    rd = full((1, d))
    o, h, cv = pl.pallas_call(
        _lru_step_kernel,
        out_shape=(jax.ShapeDtypeStruct((n, d), BF16), jax.ShapeDtypeStruct((n, d), F32),
                   jax.ShapeDtypeStruct((n, taps * d), F32)),
        grid=(1,),
        in_specs=[
            col(1), col(2), col(0), col(1), col(2), nd,
            full((CONV_W, d)), rd,
            full(w_a.shape), rd, full(w_i.shape), rd, rd, rd,
        ],
        out_specs=(nd, nd, full((n, taps * d))),
        compiler_params=_params("arbitrary"),
        name="rglru_step",
    )(z, z, conv2d, conv2d, conv2d, h0, conv_w, row(conv_b),
      w_a, row(b_a), w_i, row(b_i), row(lam), row(g_merge))
    return o, h, cv.reshape(n, taps, d)


def _glu_rows(y, w_ref, b_ref, g_ref):
    gate = jax.nn.sigmoid(_dot(y.astype(BF16), w_ref[...]) + b_ref[...])
    return _rms(y.astype(F32) * gate, g_ref[...]).astype(BF16)


def _glu_kernel(y_ref, ys_ref, w_ref, b_ref, g_ref, o_ref, os_ref):
    for rows in _row_parts(y_ref.shape[0], 4):
        o_ref[rows, :] = _glu_rows(y_ref[rows, :], w_ref, b_ref, g_ref)

    @_on_last_row_tile
    def _():
        os_ref[...] = _glu_rows(ys_ref[...], w_ref, b_ref, g_ref)


def _glu(y, ys, w, b, g, tm):
    m, d = y.shape
    ms = ys.shape[0]
    const = lambda i: (0, 0)
    return pl.pallas_call(
        _glu_kernel,
        out_shape=(jax.ShapeDtypeStruct((m, d), BF16), jax.ShapeDtypeStruct((ms, d), BF16)),
        grid=(m // tm,),
        in_specs=[
            pl.BlockSpec((tm, d), lambda i: (i, 0)),
            pl.BlockSpec((ms, d), const),
            pl.BlockSpec((d, d), const),
            pl.BlockSpec((1, d), const),
            pl.BlockSpec((1, d), const),
        ],
        out_specs=(pl.BlockSpec((tm, d), lambda i: (i, 0)), pl.BlockSpec((ms, d), const)),
        compiler_params=_params("arbitrary"),
        name="s5_glu",
    )(y, ys, w, b.reshape(1, d), g.reshape(1, d))


def _out_proj_kernel(x_ref, ma_ref, mb_ref, xs_ref, mas_ref, mbs_ref, w_ref, o_ref, os_ref, wb_ref):
    @pl.when(pl.program_id(0) == 0)
    def _():
        wb_ref[...] = w_ref[...].astype(BF16)

    ka = ma_ref.shape[1]
    proj = lambda x, ma, mb: x + _dot(ma, wb_ref[:ka, :]) + _dot(mb, wb_ref[ka:, :])
    for rows in _row_parts(x_ref.shape[0], 2):
        o_ref[rows, :] = proj(x_ref[rows, :], ma_ref[rows, :], mb_ref[rows, :])

    @_on_last_row_tile
    def _():
        os_ref[...] = proj(xs_ref[...], mas_ref[...], mbs_ref[...])


def _out_proj(x, ma, mb, xs, mas, mbs, w, tm):
    m, n = x.shape
    ms = xs.shape[0]
    ka = ma.shape[1]
    kb = mb.shape[1]
    const = lambda i: (0, 0)
    tile = lambda i: (i, 0)
    return pl.pallas_call(
        _out_proj_kernel,
        out_shape=(jax.ShapeDtypeStruct((m, n), F32), jax.ShapeDtypeStruct((ms, n), F32)),
        grid=(m // tm,),
        in_specs=[
            pl.BlockSpec((tm, n), tile),
            pl.BlockSpec((tm, ka), tile),
            pl.BlockSpec((tm, kb), tile),
            pl.BlockSpec((ms, n), const),
            pl.BlockSpec((ms, ka), const),
            pl.BlockSpec((ms, kb), const),
            pl.BlockSpec((ka + kb, n), const, pipeline_mode=pl.Buffered(1)),
        ],
        out_specs=(pl.BlockSpec((tm, n), tile), pl.BlockSpec((ms, n), const)),
        scratch_shapes=[pltpu.VMEM((ka + kb, n), BF16)],
        compiler_params=_params("arbitrary"),
        name="out_proj",
    )(x, ma, mb, xs, mas, mbs, w)


def _mlp_kernel(x_ref, xs_ref, g_ref, wu_ref, wd_ref, o_ref, os_ref, h_ref, hs_ref):
    def rows(x_ref, o_ref, h_ref):
        @pl.when(pl.program_id(1) == 0)
        def _():
            x = x_ref[...]
            h_ref[...] = _rms(x, g_ref[...]).astype(BF16)
            o_ref[...] = x

        up = _dot(h_ref[...], wu_ref[...])
        act = jnp.square(jnp.maximum(up, 0.0)).astype(BF16)
        o_ref[...] += _dot(act, wd_ref[...])

    rows(x_ref, o_ref, h_ref)
    _on_last_row_tile(lambda: rows(xs_ref, os_ref, hs_ref))


def _mlp(x, xs, g, w_up, w_down, tm, tf):
    m, d = x.shape
    ms = xs.shape[0]
    f = w_up.shape[1]
    const = lambda i, j: (0, 0)
    return pl.pallas_call(
        _mlp_kernel,
        out_shape=(jax.ShapeDtypeStruct((m, d), F32), jax.ShapeDtypeStruct((ms, d), F32)),
        grid=(m // tm, f // tf),
        in_specs=[
            pl.BlockSpec((tm, d), lambda i, j: (i, 0)),
            pl.BlockSpec((ms, d), const),
            pl.BlockSpec((1, d), const),
            pl.BlockSpec((d, tf), lambda i, j: (0, j)),
            pl.BlockSpec((tf, d), lambda i, j: (j, 0)),
        ],
        out_specs=(pl.BlockSpec((tm, d), lambda i, j: (i, 0)), pl.BlockSpec((ms, d), const)),
        scratch_shapes=[pltpu.VMEM((tm, d), BF16), pltpu.VMEM((ms, d), BF16)],
        compiler_params=_params("arbitrary", "arbitrary"),
        name="mlp",
    )(x, xs, g.reshape(1, d), w_up, w_down)


def _ple_kernel(x_ref, p_ref, xs_ref, ps_ref, g_ref, wg_ref, wp_ref, gf_ref, o_ref, os_ref,
                wgb_ref, wpb_ref, *, final):
    @pl.when(pl.program_id(0) == 0)
    def _():
        wgb_ref[...] = wg_ref[...].astype(BF16)
        wpb_ref[...] = wp_ref[...].astype(BF16)

    def ple(x, p):
        gate = jax.nn.sigmoid(_dot(_rms(x, g_ref[...]).astype(BF16), wgb_ref[...]))
        pe = _dot(p.astype(BF16), wpb_ref[...])
        x = x + pe * gate
        return _rms(x, gf_ref[...]) if final else x

    for rows in _row_parts(x_ref.shape[0], 2):
        o_ref[rows, :] = ple(x_ref[rows, :], p_ref[rows, :])

    @_on_last_row_tile
    def _():
        os_ref[...] = ple(xs_ref[...], ps_ref[...])


def _ple(x, p, xs, ps, g, w_gate, w_ple, g_final, tm, final):
    m, d = x.shape
    ms = xs.shape[0]
    dp = p.shape[1]
    once = pl.Buffered(1)
    const = lambda i: (0, 0)
    tile = lambda i: (i, 0)
    return pl.pallas_call(
        functools.partial(_ple_kernel, final=final),
        out_shape=(jax.ShapeDtypeStruct((m, d), F32), jax.ShapeDtypeStruct((ms, d), F32)),
        grid=(m // tm,),
        in_specs=[
            pl.BlockSpec((tm, d), tile),
            pl.BlockSpec((tm, dp), tile),
            pl.BlockSpec((ms, d), const),
            pl.BlockSpec((ms, dp), const),
            pl.BlockSpec((1, d), const),
            pl.BlockSpec((d, d), const, pipeline_mode=once),
            pl.BlockSpec((dp, d), const, pipeline_mode=once),
            pl.BlockSpec((1, d), const),
        ],
        out_specs=(pl.BlockSpec((tm, d), tile), pl.BlockSpec((ms, d), const)),
        scratch_shapes=[pltpu.VMEM((d, d), BF16), pltpu.VMEM((dp, d), BF16)],
        compiler_params=_params("arbitrary"),
        name="ple_final",
    )(x, p, xs, ps, g.reshape(1, d), w_gate, w_ple, g_final.reshape(1, d))


TM_IN, TN_IN = 1024, 1024
TM_GLU = 1024
TM_OUT = 512
TM_MLP, TF_MLP = 512, 1024
TM_PLE = 256


def kernel(x_prompt, x_sample, state_s5_re, state_s5_im, state_lru, state_conv, p_prompt, p_sample,
           g_mix, w_in, s5_lam_re, s5_lam_im, s5_log_step, s5_b_re, s5_b_im, s5_c_re, s5_c_im, s5_d,
           s5_w_glu, s5_b_glu, conv_w, conv_b, lru_w_a, lru_b_a, lru_w_i, lru_b_i, lru_lam,
           g_merge_a, g_merge_b, w_out, g_mlp, w_up, w_down, g_ple, w_ple_gate, w_ple, g_final):
    depth = g_mix.shape[0]
    nb, t_len, d_model = x_prompt.shape
    ns = x_sample.shape[0]
    d_s5 = s5_d.shape[1]
    d_lru = conv_w.shape[2]
    groups = d_s5 // S5_CH

    xp = x_prompt.reshape(nb * t_len, d_model)
    xs = x_sample.reshape(ns, d_model)
    outs = [[] for _ in range(8)]
    for l in range(depth):
        final = l == depth - 1
        w_glu_b = s5_w_glu[l].astype(BF16)
        lru_args = (conv_w[l], conv_b[l], lru_w_a[l].astype(BF16), lru_b_a[l],
                    lru_w_i[l].astype(BF16), lru_b_i[l], lru_lam[l], g_merge_b[l])
        w_t, w_s, w_c, tab, wb, wct, arow, w_in_b = _s5_params(
            s5_lam_re[l], s5_lam_im[l], s5_log_step[l], s5_b_re[l], s5_b_im[l],
            s5_c_re[l], s5_c_im[l], s5_d[l], w_in[l])

        z, zs = _norm_matmul(xp, xs, g_mix[l], w_in_b, TM_IN, TN_IN)
        z4 = z.reshape(nb, t_len, 3 * d_s5)
        y, hf, w_down_b = _s5_seq(z4, w_t, w_s, w_c, tab, d_s5, w_down[l])
        mb, lru_h, w_up_b = _lru_seq(z4, *lru_args, w_up[l])
        ys, hsr, hsi = _s5_step(zs, state_s5_re[l].reshape(ns, groups * S5_STATE),
                                state_s5_im[l].reshape(ns, groups * S5_STATE),
                                wb, wct, arow, s5_d[l].reshape(1, d_s5))
        mbs, lru_hs, conv_s = _lru_step(zs, state_conv[l], state_lru[l], *lru_args)
        ma, mas = _glu(y.reshape(nb * t_len, d_s5), ys, w_glu_b, s5_b_glu[l], g_merge_a[l], TM_GLU)
        xp, xs = _out_proj(xp, ma, mb.reshape(nb * t_len, d_lru), xs, mas, mbs, w_out[l], TM_OUT)
        xp, xs = _mlp(xp, xs, g_mlp[l], w_up_b, w_down_b, TM_MLP, TF_MLP)
        xp, xs = _ple(xp, p_prompt[l].reshape(nb * t_len, -1), xs, p_sample[l].reshape(ns, -1),
                      g_ple[l], w_ple_gate[l], w_ple[l], g_final, TM_PLE, final)
        hf = jnp.transpose(hf[:, :, :, :nb], (1, 3, 0, 2))
        outs[0].append(hf[0])
        outs[1].append(hf[1])
        outs[2].append(lru_h.reshape(nb, d_lru))
        outs[3].append(z4[:, t_len - (CONV_W - 1):, d_s5:d_s5 + d_lru])
        outs[4].append(hsr.reshape(ns, groups, S5_STATE))
        outs[5].append(hsi.reshape(ns, groups, S5_STATE))
        outs[6].append(lru_hs)
        outs[7].append(conv_s)
    return (xp.reshape(nb, t_len, d_model), xs.reshape(ns, 1, d_model),
            *(jnp.stack(o) for o in outs))
```

```python
import functools

import jax
import jax.numpy as jnp
from jax import lax
from jax.experimental import pallas as pl
from jax.experimental.pallas import tpu as pltpu

F32 = jnp.float32
BF16 = jnp.bfloat16
HIGHEST = lax.Precision.HIGHEST

EPS = 1e-6
LRU_C = 8.0
S5_CH = 16
S5_STATE = 64
LRU_HEADS = 4
CONV_W = 4
CHUNK = 16
SUBLANES = 8
LANES = 128
LANE_GROUPS = LANES // S5_CH
LANE_PAIRS = LANE_GROUPS // 2
ROWS = CHUNK * S5_CH
POW_ROWS = 32
SCAN_ROW0 = POW_ROWS
VMEM_LIMIT = 56 * 1024 * 1024


def _params(*sem):
    return pltpu.CompilerParams(dimension_semantics=sem, vmem_limit_bytes=VMEM_LIMIT)


def _rms(x, g):
    return x * lax.rsqrt(jnp.mean(x * x, axis=-1, keepdims=True) + EPS) * g


def _dot(a, b):
    return jnp.dot(a, b, preferred_element_type=F32)


def _dot_nt(a, b, precision=None):
    return lax.dot_general(a, b, (((1,), (1,)), ((), ())), precision=precision,
                           preferred_element_type=F32)


def _transpose_tiles(x):
    r, c = x.shape
    return jnp.concatenate(
        [jnp.concatenate([x[i:i + LANES, j:j + LANES].T for i in range(0, r, LANES)], axis=1)
         for j in range(0, c, LANES)], axis=0)


def _row_parts(tm, want):
    parts = want if tm % (want * 16) == 0 else 1
    step = tm // parts
    return [slice(r * step, (r + 1) * step) for r in range(parts)]


def _on_last_row_tile(fn):
    pl.when(pl.program_id(0) == pl.num_programs(0) - 1)(fn)


def _norm_matmul_kernel(x_ref, xs_ref, g_ref, w_ref, u_ref, xg_ref, os_ref):
    tn = u_ref.shape[1]
    j = pl.program_id(1)
    w = w_ref[:, pl.ds(pl.multiple_of(j * tn, tn), tn)]

    def tile(o_ref):
        for rows in _row_parts(x_ref.shape[0], 4):
            h = _rms(x_ref[rows, :], g_ref[...]).astype(BF16)
            o_ref[rows, :] = _dot(h, w).astype(o_ref.dtype)

    pl.when(j == 0)(lambda: tile(u_ref))
    pl.when(j > 0)(lambda: tile(xg_ref))

    @_on_last_row_tile
    def _():
        os_ref[...] = _dot(_rms(xs_ref[...], g_ref[...]).astype(BF16), w)


def _norm_matmul(x, xs, g, w, tm, tn):
    m, k = x.shape
    ms = xs.shape[0]
    n = w.shape[1]
    ni = m // tm
    return pl.pallas_call(
        _norm_matmul_kernel,
        out_shape=(jax.ShapeDtypeStruct((m, tn), F32), jax.ShapeDtypeStruct((m, n - tn), BF16),
                   jax.ShapeDtypeStruct((ms, n), F32)),
        grid=(ni, n // tn),
        in_specs=[
            pl.BlockSpec((tm, k), lambda i, j: (i, 0)),
            pl.BlockSpec((ms, k), lambda i, j: (0, 0)),
            pl.BlockSpec((1, k), lambda i, j: (0, 0)),
            pl.BlockSpec((k, n), lambda i, j: (0, 0), pipeline_mode=pl.Buffered(1)),
        ],
        out_specs=(pl.BlockSpec((tm, tn), lambda i, j: (i, 0)),
                   pl.BlockSpec((tm, tn), lambda i, j: (i, jnp.maximum(j - 1, 0))),
                   pl.BlockSpec((ms, tn), lambda i, j: (0, jnp.where(i == ni - 1, j, 0)))),
        compiler_params=_params("arbitrary", "arbitrary"),
        name="norm_w_in",
    )(x, xs, g.reshape(1, k), w)


def _s5_param_kernel(lr_ref, li_ref, ls_ref, bt_r_ref, bt_i_ref, c_r_ref, c_i_ref, d_ref, cast_ref,
                     wt_ref, ws_ref, wc_ref, tab_ref, wb_ref, wct_ref, arow_ref, castb_ref):
    castb_ref[...] = cast_ref[...].astype(BF16)
    row_p = lax.broadcasted_iota(jnp.int32, (POW_ROWS, LANES), 0)
    lane = lax.broadcasted_iota(jnp.int32, (ROWS, LANES), 1)
    lane16 = lax.broadcasted_iota(jnp.int32, (S5_CH, LANES), 1)
    row16 = lax.broadcasted_iota(jnp.int32, (S5_CH, LANES), 0)
    lo = lane < S5_STATE

    wb_ref[...] = jnp.zeros_like(wb_ref)
    wct_ref[...] = jnp.zeros_like(wct_ref)

    for q in range(LANE_PAIRS):
        lr, li = lr_ref[q], li_ref[q]
        step = jnp.exp(ls_ref[q])
        mag = jnp.exp(lr * step)
        ar = mag * jnp.cos(li * step)
        ai = mag * jnp.sin(li * step)
        nr, ni = ar - 1.0, ai
        den = lr * lr + li * li
        cr = (nr * lr + ni * li) / den
        ci = (ni * lr - nr * li) / den
        bt_r, bt_i = bt_r_ref[q], bt_i_ref[q]
        bb_r = cr * bt_r - ci * bt_i
        bb_i = cr * bt_i + ci * bt_r
        c_r, c_i = c_r_ref[q], c_i_ref[q]

        pr = jnp.ones((POW_ROWS, LANES), F32)
        pi = jnp.zeros((POW_ROWS, LANES), F32)
        sr, si = ar, ai
        sq = []
        for m in range(POW_ROWS.bit_length() - 1 + SUBLANES):
            sq.append((sr, si))
            if (1 << m) < POW_ROWS:
                bit = ((row_p >> m) & 1) == 1
                pr, pi = (jnp.where(bit, pr * sr - pi * si, pr),
                          jnp.where(bit, pr * si + pi * sr, pi))
            sr, si = sr * sr - si * si, 2.0 * sr * si
        m0 = CHUNK.bit_length() - 1
        scan_r = jnp.concatenate([sq[m0 + j][0] for j in range(SUBLANES)], axis=0)
        scan_i = jnp.concatenate([sq[m0 + j][1] for j in range(SUBLANES)], axis=0)
        pad = jnp.zeros((LANES - POW_ROWS - SUBLANES, LANES), F32)
        tab_ref[q, 0] = jnp.concatenate([pr, scan_r, pad], axis=0).T
        tab_ref[q, 1] = jnp.concatenate([pi, scan_i, pad], axis=0).T
        arow_ref[q, 0] = pr[:SUBLANES]
        arow_ref[q, 1] = pi[:SUBLANES]

        def expand(tab, k0, sign):
            return jnp.concatenate(
                [jnp.broadcast_to(tab[k0 + sign * s:k0 + sign * s + 1, :], (S5_CH, LANES))
                 for s in range(CHUNK)], axis=0)

        tile = lambda v: jnp.concatenate([v] * CHUNK, axis=0)

        e_r, e_i = expand(pr, CHUNK - 1, -1), expand(pi, CHUNK - 1, -1)
        t_r, t_i = tile(bb_r), tile(bb_i)
        ws_r = _transpose_tiles(t_r * e_r - t_i * e_i)
        ws_i = _transpose_tiles(t_r * e_i + t_i * e_r)
        for h in range(2):
            rows = slice(h * S5_STATE, (h + 1) * S5_STATE)
            ws_ref[2 * q + h] = jnp.concatenate([ws_r[rows], ws_i[rows]], axis=0).astype(BF16)

        e_r, e_i = expand(pr, 1, 1), expand(pi, 1, 1)
        t_r, t_i = tile(c_r), tile(c_i)
        ca_r = t_r * e_r - t_i * e_i
        ca_n = -(t_r * e_i + t_i * e_r)
        wc_ref[2 * q] = jnp.where(lo, ca_r, pltpu.roll(ca_n, S5_STATE, 1)).astype(BF16)
        wc_ref[2 * q + 1] = jnp.where(lo, pltpu.roll(ca_r, S5_STATE, 1), ca_n).astype(BF16)

        e_r, e_i = expand(pr, 0, 1), expand(pi, 0, 1)
        c0_r = t_r * e_r - t_i * e_i
        c0_i = t_r * e_i + t_i * e_r

        for h in range(2):
            g = 2 * q + h
            mine = (lane16 < S5_STATE) == (h == 0)
            bm_r = jnp.where(mine, bb_r, 0.0)
            bm_i = jnp.where(mine, bb_i, 0.0)
            kt = _dot_nt(bm_r, c0_r, HIGHEST) - _dot_nt(bm_i, c0_i, HIGHEST)
            k_lo = kt[:, :LANES] + jnp.where(row16 == lane16, d_ref[g], 0.0)
            k_hi = kt[:, LANES:]
            blocks = []
            half = LANES // S5_CH
            for s in range(CHUNK):
                sh = (s % half) * S5_CH
                keep = lane16 >= sh
                r_lo = pltpu.roll(k_lo, sh, 1) if sh else k_lo
                r_hi = pltpu.roll(k_hi, sh, 1) if sh else k_hi
                if s < half:
                    blk = jnp.concatenate([jnp.where(keep, r_lo, 0.0),
                                           jnp.where(keep, r_hi, r_lo)], axis=1)
                else:
                    blk = jnp.concatenate([jnp.zeros_like(r_lo), jnp.where(keep, r_lo, 0.0)], axis=1)
                blocks.append(blk)
            wt_ref[g] = _transpose_tiles(jnp.concatenate(blocks, axis=0)).astype(BF16)

            r0 = (2 * q + h) * S5_CH
            c0 = q * 2 * LANES
            wb_ref[r0:r0 + S5_CH, c0:c0 + LANES] = bm_r.astype(BF16)
            wb_ref[r0:r0 + S5_CH, c0 + LANES:c0 + 2 * LANES] = bm_i.astype(BF16)
            wct_ref[q, r0:r0 + S5_CH, :LANES] = jnp.where(mine, c_r, 0.0).astype(BF16)
            wct_ref[q, r0:r0 + S5_CH, LANES:] = jnp.where(mine, -c_i, 0.0).astype(BF16)


def _s5_params(lam_re, lam_im, log_step, b_re, b_im, c_re, c_im, d, cast_w):
    groups = lam_re.shape[0]
    pairs = groups // 2
    slabs = groups // LANE_GROUPS
    cast_blk = pl.BlockSpec((cast_w.shape[0] // slabs, cast_w.shape[1]), lambda i: (i, 0))
    pair_row = lambda v: v.reshape(pairs, 1, LANES)
    pair_ch = lambda v: jnp.transpose(v.reshape(pairs, 2, S5_CH, S5_STATE), (0, 2, 1, 3)).reshape(
        pairs, S5_CH, LANES)
    ls = jnp.broadcast_to(log_step[:, None], (groups, S5_STATE))
    d_pad = jnp.pad(d.reshape(groups, 1, S5_CH), ((0, 0), (0, 0), (0, LANES - S5_CH)))
    blk3 = lambda i: (i, 0, 0)
    blk4 = lambda i: (i, 0, 0, 0)
    prow = pl.BlockSpec((LANE_PAIRS, 1, LANES), blk3)
    pch = pl.BlockSpec((LANE_PAIRS, S5_CH, LANES), blk3)
    return pl.pallas_call(
        _s5_param_kernel,
        out_shape=(jax.ShapeDtypeStruct((groups, ROWS, ROWS), BF16),
                   jax.ShapeDtypeStruct((groups, 2 * S5_STATE, ROWS), BF16),
                   jax.ShapeDtypeStruct((groups, ROWS, 2 * S5_STATE), BF16),
                   jax.ShapeDtypeStruct((pairs, 2, LANES, LANES), F32),
                   jax.ShapeDtypeStruct((slabs, LANES, LANE_PAIRS * 2 * LANES), BF16),
                   jax.ShapeDtypeStruct((slabs, LANE_PAIRS, LANES, 2 * LANES), BF16),
                   jax.ShapeDtypeStruct((pairs, 2, SUBLANES, LANES), F32),
                   jax.ShapeDtypeStruct(cast_w.shape, BF16)),
        grid=(slabs,),
        in_specs=[prow, prow, prow, pch, pch, pch, pch,
                  pl.BlockSpec((LANE_GROUPS, 1, LANES), blk3), cast_blk],
        out_specs=(pl.BlockSpec((LANE_GROUPS, ROWS, ROWS), blk3),
                   pl.BlockSpec((LANE_GROUPS, 2 * S5_STATE, ROWS), blk3),
                   pl.BlockSpec((LANE_GROUPS, ROWS, 2 * S5_STATE), blk3),
                   pl.BlockSpec((LANE_PAIRS, 2, LANES, LANES), blk4),
                   pl.BlockSpec((None, LANES, LANE_PAIRS * 2 * LANES), blk3),
                   pl.BlockSpec((None, LANE_PAIRS, LANES, 2 * LANES), blk4),
                   pl.BlockSpec((LANE_PAIRS, 2, SUBLANES, LANES), blk4),
                   cast_blk),
        compiler_params=_params("parallel"),
        name="s5_params",
    )(pair_row(lam_re), pair_row(lam_im), pair_row(ls),
      pair_ch(jnp.transpose(b_re, (0, 2, 1))), pair_ch(jnp.transpose(b_im, (0, 2, 1))),
      pair_ch(c_re), pair_ch(c_im), d_pad, cast_w)


def _s5_seq_kernel(z_ref, wt_ref, ws_ref, wc_ref, tab_ref, cast_ref, y_ref, hf_ref, castb_ref,
                   ut_ref, yt_ref, sr_ref, si_ref, *, nb, n_chunks):
    castb_ref[...] = cast_ref[...].astype(BF16)
    scan_steps = n_chunks.bit_length() - 1
    for n in range(nb):
        for s in range(CHUNK):
            xs = z_ref[n, pl.ds(s, n_chunks, stride=CHUNK), :]
            ut_ref[:, s * S5_CH:(s + 1) * S5_CH, n * n_chunks:(n + 1) * n_chunks] = (
                xs.T.astype(BF16).reshape(LANE_GROUPS, S5_CH, n_chunks))

    for g in range(LANE_GROUPS):
        st = _dot(ws_ref[g], ut_ref[g])
        sr_ref[g * S5_STATE:(g + 1) * S5_STATE, :] = st[:S5_STATE]
        si_ref[g * S5_STATE:(g + 1) * S5_STATE, :] = st[S5_STATE:]

    srows = LANE_GROUPS * S5_STATE
    lane = lax.broadcasted_iota(jnp.int32, (srows, n_chunks), 1)
    tab_r = jnp.concatenate([tab_ref[q, 0] for q in range(LANE_PAIRS)], axis=0)
    tab_i = jnp.concatenate([tab_ref[q, 1] for q in range(LANE_PAIRS)], axis=0)
    fin_r = jnp.zeros((srows, n_chunks), F32)
    fin_i = jnp.zeros((srows, n_chunks), F32)
    for n in range(nb):
        cols = slice(n * n_chunks, (n + 1) * n_chunks)
        xr = sr_ref[:, cols]
        xi = si_ref[:, cols]
        for k in range(scan_steps):
            d = 1 << k
            keep = lane >= d
            sr = jnp.where(keep, pltpu.roll(xr, d, 1), 0.0)
            si = jnp.where(keep, pltpu.roll(xi, d, 1), 0.0)
            pr = tab_r[:, SCAN_ROW0 + k:SCAN_ROW0 + k + 1]
            pi = tab_i[:, SCAN_ROW0 + k:SCAN_ROW0 + k + 1]
            xr, xi = xr + (pr * sr - pi * si), xi + (pr * si + pi * sr)
        last = lane == n_chunks - 1
        fin_r = fin_r + jnp.where(
            lane == n, jnp.sum(jnp.where(last, xr, 0.0), axis=1, keepdims=True), 0.0)
        fin_i = fin_i + jnp.where(
            lane == n, jnp.sum(jnp.where(last, xi, 0.0), axis=1, keepdims=True), 0.0)
        keep = lane >= 1
        sr_ref[:, cols] = jnp.where(keep, pltpu.roll(xr, 1, 1), 0.0)
        si_ref[:, cols] = jnp.where(keep, pltpu.roll(xi, 1, 1), 0.0)

    for g in range(LANE_GROUPS):
        rows = slice(g * S5_STATE, (g + 1) * S5_STATE)
        hf_ref[g, 0] = fin_r[rows]
        hf_ref[g, 1] = fin_i[rows]
        hs = jnp.concatenate([sr_ref[rows, :], si_ref[rows, :]], axis=0).astype(BF16)
        yt_ref[g] = jax.nn.gelu(_dot(wc_ref[g], hs) + _dot(wt_ref[g], ut_ref[g]))

    for n in range(nb):
        for s in range(CHUNK):
            blk = yt_ref[:, s * S5_CH:(s + 1) * S5_CH, n * n_chunks:(n + 1) * n_chunks]
            y_ref[n, pl.ds(s, n_chunks, stride=CHUNK), :] = blk.reshape(LANES, n_chunks).T


def _s5_seq(z, w_t, w_s, w_c, tab, d_s5, cast_w):
    nb, t_len, _ = z.shape
    n_chunks = t_len // CHUNK
    assert n_chunks == LANES, "the chunk axis must fill one 128-lane tile"
    groups = d_s5 // S5_CH
    steps = groups // LANE_GROUPS
    cast_blk = pl.BlockSpec((cast_w.shape[0] // steps, cast_w.shape[1]), lambda i: (i, 0))
    kern = functools.partial(_s5_seq_kernel, nb=nb, n_chunks=n_chunks)
    blk3 = lambda i: (i, 0, 0)
    blk4 = lambda i: (i, 0, 0, 0)
    return pl.pallas_call(
        kern,
        out_shape=(jax.ShapeDtypeStruct((nb, t_len, d_s5), F32),
                   jax.ShapeDtypeStruct((groups, 2, S5_STATE, LANES), F32),
                   jax.ShapeDtypeStruct(cast_w.shape, BF16)),
        grid=(steps,),
        in_specs=[
            pl.BlockSpec((nb, t_len, LANES), lambda i: (0, 0, i)),
            pl.BlockSpec((LANE_GROUPS, ROWS, ROWS), blk3),
            pl.BlockSpec((LANE_GROUPS, 2 * S5_STATE, ROWS), blk3),
            pl.BlockSpec((LANE_GROUPS, ROWS, 2 * S5_STATE), blk3),
            pl.BlockSpec((LANE_PAIRS, 2, LANES, LANES), blk4),
            cast_blk,
        ],
        out_specs=(pl.BlockSpec((nb, t_len, LANES), lambda i: (0, 0, i)),
                   pl.BlockSpec((LANE_GROUPS, 2, S5_STATE, LANES), blk4),
                   cast_blk),
        scratch_shapes=[pltpu.VMEM((LANE_GROUPS, ROWS, nb * n_chunks), BF16),
                        pltpu.VMEM((LANE_GROUPS, ROWS, nb * n_chunks), F32),
                        pltpu.VMEM((LANE_GROUPS * S5_STATE, nb * n_chunks), F32),
                        pltpu.VMEM((LANE_GROUPS * S5_STATE, nb * n_chunks), F32)],
        compiler_params=_params("parallel"),
        name="s5_seq",
    )(z, w_t, w_s, w_c, tab, cast_w)


def _s5_step_kernel(u_ref, h0r_ref, h0i_ref, wb_ref, wct_ref, arow_ref, d_ref,
                    y_ref, hr_ref, hi_ref):
    u = u_ref[...]
    bu = _dot(u.astype(BF16), wb_ref[...])
    y = d_ref[...] * u
    for q in range(LANE_PAIRS):
        cols = slice(q * LANES, (q + 1) * LANES)
        ar = arow_ref[q, 0, 1:2, :]
        ai = arow_ref[q, 1, 1:2, :]
        h0r, h0i = h0r_ref[:, cols], h0i_ref[:, cols]
        hr = ar * h0r - ai * h0i + bu[:, 2 * q * LANES:(2 * q + 1) * LANES]
        hi = ar * h0i + ai * h0r + bu[:, (2 * q + 1) * LANES:(2 * q + 2) * LANES]
        hr_ref[:, cols] = hr
        hi_ref[:, cols] = hi
        y = y + _dot_nt(jnp.concatenate([hr, hi], axis=1).astype(BF16), wct_ref[q])
    y_ref[...] = jax.nn.gelu(y).astype(BF16)


def _s5_step(z, h0r, h0i, wb, wct, arow, d):
    n = z.shape[0]
    slabs = wb.shape[0]
    sw = LANE_PAIRS * LANES
    col = lambda i: (0, i)
    blk3 = lambda i: (i, 0, 0)
    blk4 = lambda i: (i, 0, 0, 0)
    return pl.pallas_call(
        _s5_step_kernel,
        out_shape=(jax.ShapeDtypeStruct((n, slabs * LANES), BF16),
                   jax.ShapeDtypeStruct((n, slabs * sw), F32),
                   jax.ShapeDtypeStruct((n, slabs * sw), F32)),
        grid=(slabs,),
        in_specs=[
            pl.BlockSpec((n, LANES), col),
            pl.BlockSpec((n, sw), col),
            pl.BlockSpec((n, sw), col),
            pl.BlockSpec((None, LANES, 2 * sw), blk3),
            pl.BlockSpec((None, LANE_PAIRS, LANES, 2 * LANES), blk4),
            pl.BlockSpec((LANE_PAIRS, 2, SUBLANES, LANES), blk4),
            pl.BlockSpec((1, LANES), col),
        ],
        out_specs=(pl.BlockSpec((n, LANES), col),
                   pl.BlockSpec((n, sw), col),
                   pl.BlockSpec((n, sw), col)),
        compiler_params=_params("parallel"),
        name="s5_step",
    )(z, h0r, h0i, wb, wct, arow, d)


def _lru_gate_block(xc, wa, ba, wi, bi, lam):
    xb16 = xc.astype(BF16)
    r = jax.nn.sigmoid(_dot(xb16, wa) + ba)
    ig = jax.nn.sigmoid(_dot(xb16, wi) + bi)
    log_a = -LRU_C * r * jax.nn.softplus(-lam)
    a = jnp.exp(log_a)
    mult = jnp.sqrt(1.0 - a * a)
    return a, mult * (ig * xc)


def _lru_gates(xc, wa_ref, ba, wi_ref, bi, lam):
    blk = xc.shape[1] // LRU_HEADS
    parts = [_lru_gate_block(xc[:, h * blk:(h + 1) * blk], wa_ref[h], ba[:, h * blk:(h + 1) * blk],
                             wi_ref[h], bi[:, h * blk:(h + 1) * blk],
                             lam[:, h * blk:(h + 1) * blk]) for h in range(LRU_HEADS)]
    return (jnp.concatenate([p[0] for p in parts], axis=-1),
            jnp.concatenate([p[1] for p in parts], axis=-1))


def _lru_seq_kernel(xb_ref, gb_ref, cw_ref, cb_ref, wa_ref, ba_ref, wi_ref, bi_ref, lam_ref,
                    gm_ref, cast_ref, o_ref, hl_ref, castb_ref, xe_ref, a_ref, b_ref, hc_ref,
                    *, tt):
    castb_ref[...] = cast_ref[...].astype(BF16)
    halo = SUBLANES
    nseq, _, d = xb_ref.shape
    blk = d // LRU_HEADS

    @pl.when(pl.program_id(0) == 0)
    def _():
        xe_ref[:, 0:halo, :] = jnp.zeros((nseq, halo, d), F32)
        hc_ref[...] = jnp.zeros_like(hc_ref)

    xe_ref[:, halo:halo + tt, :] = xb_ref[...].astype(F32)
    for h in range(LRU_HEADS):
        cols = slice(h * blk, (h + 1) * blk)
        cw = cw_ref[:, cols]
        xc = cb_ref[:, cols] + xe_ref[:, halo:halo + tt, cols] * cw[CONV_W - 1:CONV_W, :]
        for k in range(1, CONV_W):
            xc = xc + xe_ref[:, halo - k:halo - k + tt, cols] * cw[CONV_W - 1 - k:CONV_W - k, :]
        a, b = _lru_gate_block(xc.reshape(nseq * tt, blk), wa_ref[h], ba_ref[:, cols],
                               wi_ref[h], bi_ref[:, cols], lam_ref[:, cols])
        a_ref[:, :, cols] = a.reshape(nseq, tt, blk)
        b_ref[:, :, cols] = b.reshape(nseq, tt, blk)
    xe_ref[:, 0:halo, :] = xe_ref[:, tt:tt + halo, :]

    def block(i, h):
        base = pl.multiple_of(i * SUBLANES, SUBLANES)
        for j in range(SUBLANES):
            row = pl.ds(base + j, 1)
            h = a_ref[:, row, :] * h + b_ref[:, row, :]
            b_ref[:, row, :] = h
        return h

    h = lax.fori_loop(0, tt // SUBLANES, block, hc_ref[...])
    hc_ref[...] = h
    hl_ref[...] = h

    out = b_ref[...] * jax.nn.gelu(gb_ref[...].astype(F32))
    o_ref[...] = _rms(out, gm_ref[...]).astype(BF16)


def _lru_seq(z, conv_w, conv_b, w_a, b_a, w_i, b_i, lam, g_merge, cast_w, tt=64):
    nseq, t_len, _ = z.shape
    d = conv_w.shape[1]
    steps = t_len // tt
    cast_blk = pl.BlockSpec((cast_w.shape[0] // steps, cast_w.shape[1]), lambda t: (t, 0))
    row = lambda v: v.reshape(1, d)
    const2 = lambda t: (0, 0)
    const3 = lambda t: (0, 0, 0)
    kern = functools.partial(_lru_seq_kernel, tt=tt)
    return pl.pallas_call(
        kern,
        out_shape=(jax.ShapeDtypeStruct((nseq, t_len, d), BF16),
                   jax.ShapeDtypeStruct((nseq, 1, d), F32),
                   jax.ShapeDtypeStruct(cast_w.shape, BF16)),
        grid=(steps,),
        in_specs=[
            pl.BlockSpec((nseq, tt, d), lambda t: (0, t, 0)),
            pl.BlockSpec((nseq, tt, d), lambda t: (0, t, 1)),
            pl.BlockSpec((CONV_W, d), const2),
            pl.BlockSpec((1, d), const2),
            pl.BlockSpec(w_a.shape, const3),
            pl.BlockSpec((1, d), const2),
            pl.BlockSpec(w_i.shape, const3),
            pl.BlockSpec((1, d), const2),
            pl.BlockSpec((1, d), const2),
            pl.BlockSpec((1, d), const2),
            cast_blk,
        ],
        out_specs=(pl.BlockSpec((nseq, tt, d), lambda t: (0, t, 0)),
                   pl.BlockSpec((nseq, 1, d), const3),
                   cast_blk),
        scratch_shapes=[pltpu.VMEM((nseq, tt + SUBLANES, d), F32),
                        pltpu.VMEM((nseq, tt, d), F32),
                        pltpu.VMEM((nseq, tt, d), F32),
                        pltpu.VMEM((nseq, 1, d), F32)],
        compiler_params=_params("arbitrary"),
        name="rglru_seq",
    )(z, z, conv_w, row(conv_b), w_a, row(b_a), w_i, row(b_i), row(lam), row(g_merge), cast_w)


def _lru_step_kernel(xb_ref, gb_ref, c0_ref, c1_ref, c2_ref, h0_ref, cw_ref, cb_ref,
                     wa_ref, ba_ref, wi_ref, bi_ref, lam_ref, gm_ref, o_ref, h_ref, cv_ref):
    d = xb_ref.shape[1]
    xb = xb_ref[...]
    cw = cw_ref[...]
    xc = (cb_ref[...] + c0_ref[...] * cw[0:1, :] + c1_ref[...] * cw[1:2, :]
          + c2_ref[...] * cw[2:3, :] + xb * cw[3:4, :])
    a, b = _lru_gates(xc, wa_ref, ba_ref[...], wi_ref, bi_ref[...], lam_ref[...])
    h = a * h0_ref[...] + b
    h_ref[...] = h
    out = h * jax.nn.gelu(gb_ref[...])
    o_ref[...] = _rms(out, gm_ref[...]).astype(BF16)
    cv_ref[:, 0:d] = c1_ref[...]
    cv_ref[:, d:2 * d] = c2_ref[...]
    cv_ref[:, 2 * d:3 * d] = xb


def _lru_step(z, conv0, h0, conv_w, conv_b, w_a, b_a, w_i, b_i, lam, g_merge):
    n = z.shape[0]
    d = conv_w.shape[1]
    taps = CONV_W - 1
    conv2d = conv0.reshape(n, taps * d)
    row = lambda v: v.reshape(1, d)
    full = lambda shape: pl.BlockSpec(shape, lambda i: (0,) * len(shape))
    col = lambda c: pl.BlockSpec((n, d), lambda i: (0, c))
    nd = full((n, d))
    rd = full((1, d))
    o, h, cv = pl.pallas_call(
        _lru_step_kernel,
        out_shape=(jax.ShapeDtypeStruct((n, d), BF16), jax.ShapeDtypeStruct((n, d), F32),
                   jax.ShapeDtypeStruct((n, taps * d), F32)),
        grid=(1,),
        in_specs=[
            col(1), col(2), col(0), col(1), col(2), nd,
            full((CONV_W, d)), rd,
            full(w_a.shape), rd, full(w_i.shape), rd, rd, rd,
        ],
        out_specs=(nd, nd, full((n, taps * d))),
        compiler_params=_params("arbitrary"),
        name="rglru_step",
    )(z, z, conv2d, conv2d, conv2d, h0, conv_w, row(conv_b),
      w_a, row(b_a), w_i, row(b_i), row(lam), row(g_merge))
    return o, h, cv.reshape(n, taps, d)


def _glu_rows(y, w_ref, b_ref, g_ref):
    gate = jax.nn.sigmoid(_dot(y.astype(BF16), w_ref[...]) + b_ref[...])
    return _rms(y.astype(F32) * gate, g_ref[...]).astype(BF16)


def _glu_kernel(y_ref, ys_ref, w_ref, b_ref, g_ref, o_ref, os_ref):
    for rows in _row_parts(y_ref.shape[0], 4):
        o_ref[rows, :] = _glu_rows(y_ref[rows, :], w_ref, b_ref, g_ref)

    @_on_last_row_tile
    def _():
        os_ref[...] = _glu_rows(ys_ref[...], w_ref, b_ref, g_ref)


def _glu(y, ys, w, b, g, tm):
    m, d = y.shape
    ms = ys.shape[0]
    const = lambda i: (0, 0)
    return pl.pallas_call(
        _glu_kernel,
        out_shape=(jax.ShapeDtypeStruct((m, d), BF16), jax.ShapeDtypeStruct((ms, d), BF16)),
        grid=(m // tm,),
        in_specs=[
            pl.BlockSpec((tm, d), lambda i: (i, 0)),
            pl.BlockSpec((ms, d), const),
            pl.BlockSpec((d, d), const),
            pl.BlockSpec((1, d), const),
            pl.BlockSpec((1, d), const),
        ],
        out_specs=(pl.BlockSpec((tm, d), lambda i: (i, 0)), pl.BlockSpec((ms, d), const)),
        compiler_params=_params("arbitrary"),
        name="s5_glu",
    )(y, ys, w, b.reshape(1, d), g.reshape(1, d))


def _out_proj_kernel(x_ref, ma_ref, mb_ref, xs_ref, mas_ref, mbs_ref, w_ref, o_ref, os_ref, wb_ref):
    @pl.when(pl.program_id(0) == 0)
    def _():
        wb_ref[...] = w_ref[...].astype(BF16)

    ka = ma_ref.shape[1]
    proj = lambda x, ma, mb: x + _dot(ma, wb_ref[:ka, :]) + _dot(mb, wb_ref[ka:, :])
    for rows in _row_parts(x_ref.shape[0], 2):
        o_ref[rows, :] = proj(x_ref[rows, :], ma_ref[rows, :], mb_ref[rows, :])

    @_on_last_row_tile
    def _():
        os_ref[...] = proj(xs_ref[...], mas_ref[...], mbs_ref[...])


def _out_proj(x, ma, mb, xs, mas, mbs, w, tm):
    m, n = x.shape
    ms = xs.shape[0]
    ka = ma.shape[1]
    kb = mb.shape[1]
    const = lambda i: (0, 0)
    tile = lambda i: (i, 0)
    return pl.pallas_call(
        _out_proj_kernel,
        out_shape=(jax.ShapeDtypeStruct((m, n), F32), jax.ShapeDtypeStruct((ms, n), F32)),
        grid=(m // tm,),
        in_specs=[
            pl.BlockSpec((tm, n), tile),
            pl.BlockSpec((tm, ka), tile),
            pl.BlockSpec((tm, kb), tile),
            pl.BlockSpec((ms, n), const),
            pl.BlockSpec((ms, ka), const),
            pl.BlockSpec((ms, kb), const),
            pl.BlockSpec((ka + kb, n), const, pipeline_mode=pl.Buffered(1)),
        ],
        out_specs=(pl.BlockSpec((tm, n), tile), pl.BlockSpec((ms, n), const)),
        scratch_shapes=[pltpu.VMEM((ka + kb, n), BF16)],
        compiler_params=_params("arbitrary"),
        name="out_proj",
    )(x, ma, mb, xs, mas, mbs, w)


def _mlp_kernel(x_ref, xs_ref, g_ref, wu_ref, wd_ref, o_ref, os_ref, h_ref, hs_ref):
    def rows(x_ref, o_ref, h_ref):
        @pl.when(pl.program_id(1) == 0)
        def _():
            x = x_ref[...]
            h_ref[...] = _rms(x, g_ref[...]).astype(BF16)
            o_ref[...] = x

        up = _dot(h_ref[...], wu_ref[...])
        act = jnp.square(jnp.maximum(up, 0.0)).astype(BF16)
        o_ref[...] += _dot(act, wd_ref[...])

    rows(x_ref, o_ref, h_ref)
    _on_last_row_tile(lambda: rows(xs_ref, os_ref, hs_ref))


def _mlp(x, xs, g, w_up, w_down, tm, tf):
    m, d = x.shape
    ms = xs.shape[0]
    f = w_up.shape[1]
    const = lambda i, j: (0, 0)
    return pl.pallas_call(
        _mlp_kernel,
        out_shape=(jax.ShapeDtypeStruct((m, d), F32), jax.ShapeDtypeStruct((ms, d), F32)),
        grid=(m // tm, f // tf),
        in_specs=[
            pl.BlockSpec((tm, d), lambda i, j: (i, 0)),
            pl.BlockSpec((ms, d), const),
            pl.BlockSpec((1, d), const),
            pl.BlockSpec((d, tf), lambda i, j: (0, j)),
            pl.BlockSpec((tf, d), lambda i, j: (j, 0)),
        ],
        out_specs=(pl.BlockSpec((tm, d), lambda i, j: (i, 0)), pl.BlockSpec((ms, d), const)),
        scratch_shapes=[pltpu.VMEM((tm, d), BF16), pltpu.VMEM((ms, d), BF16)],
        compiler_params=_params("arbitrary", "arbitrary"),
        name="mlp",
    )(x, xs, g.reshape(1, d), w_up, w_down)


def _ple_kernel(x_ref, p_ref, xs_ref, ps_ref, g_ref, wg_ref, wp_ref, gf_ref, o_ref, os_ref,
                wgb_ref, wpb_ref, *, final):
    @pl.when(pl.program_id(0) == 0)
    def _():
        wgb_ref[...] = wg_ref[...].astype(BF16)
        wpb_ref[...] = wp_ref[...].astype(BF16)

    def ple(x, p):
        gate = jax.nn.sigmoid(_dot(_rms(x, g_ref[...]).astype(BF16), wgb_ref[...]))
        pe = _dot(p.astype(BF16), wpb_ref[...])
        x = x + pe * gate
        return _rms(x, gf_ref[...]) if final else x

    for rows in _row_parts(x_ref.shape[0], 2):
        o_ref[rows, :] = ple(x_ref[rows, :], p_ref[rows, :])

    @_on_last_row_tile
    def _():
        os_ref[...] = ple(xs_ref[...], ps_ref[...])


def _ple(x, p, xs, ps, g, w_gate, w_ple, g_final, tm, final):
    m, d = x.shape
    ms = xs.shape[0]
    dp = p.shape[1]
    once = pl.Buffered(1)
    const = lambda i: (0, 0)
    tile = lambda i: (i, 0)
    return pl.pallas_call(
        functools.partial(_ple_kernel, final=final),
        out_shape=(jax.ShapeDtypeStruct((m, d), F32), jax.ShapeDtypeStruct((ms, d), F32)),
        grid=(m // tm,),
        in_specs=[
            pl.BlockSpec((tm, d), tile),
            pl.BlockSpec((tm, dp), tile),
            pl.BlockSpec((ms, d), const),
            pl.BlockSpec((ms, dp), const),
            pl.BlockSpec((1, d), const),
            pl.BlockSpec((d, d), const, pipeline_mode=once),
            pl.BlockSpec((dp, d), const, pipeline_mode=once),
            pl.BlockSpec((1, d), const),
        ],
        out_specs=(pl.BlockSpec((tm, d), tile), pl.BlockSpec((ms, d), const)),
        scratch_shapes=[pltpu.VMEM((d, d), BF16), pltpu.VMEM((dp, d), BF16)],
        compiler_params=_params("arbitrary"),
        name="ple_final",
    )(x, p, xs, ps, g.reshape(1, d), w_gate, w_ple, g_final.reshape(1, d))


TM_IN, TN_IN = 1024, 1024
TM_GLU = 1024
TM_OUT = 512
TM_MLP, TF_MLP = 512, 1024
TM_PLE = 256


def kernel(x_prompt, x_sample, state_s5_re, state_s5_im, state_lru, state_conv, p_prompt, p_sample,
           g_mix, w_in, s5_lam_re, s5_lam_im, s5_log_step, s5_b_re, s5_b_im, s5_c_re, s5_c_im, s5_d,
           s5_w_glu, s5_b_glu, conv_w, conv_b, lru_w_a, lru_b_a, lru_w_i, lru_b_i, lru_lam,
           g_merge_a, g_merge_b, w_out, g_mlp, w_up, w_down, g_ple, w_ple_gate, w_ple, g_final):
    depth = g_mix.shape[0]
    nb, t_len, d_model = x_prompt.shape
    ns = x_sample.shape[0]
    d_s5 = s5_d.shape[1]
    d_lru = conv_w.shape[2]
    groups = d_s5 // S5_CH

    xp = x_prompt.reshape(nb * t_len, d_model)
    xs = x_sample.reshape(ns, d_model)
    outs = [[] for _ in range(8)]
    for l in range(depth):
        final = l == depth - 1
        w_glu_b = s5_w_glu[l].astype(BF16)
        lru_args = (conv_w[l], conv_b[l], lru_w_a[l].astype(BF16), lru_b_a[l],
                    lru_w_i[l].astype(BF16), lru_b_i[l], lru_lam[l], g_merge_b[l])
        w_t, w_s, w_c, tab, wb, wct, arow, w_in_b = _s5_params(
            s5_lam_re[l], s5_lam_im[l], s5_log_step[l], s5_b_re[l], s5_b_im[l],
            s5_c_re[l], s5_c_im[l], s5_d[l], w_in[l])

        u, xg, zs = _norm_matmul(xp, xs, g_mix[l], w_in_b, TM_IN, TN_IN)
        u4 = u.reshape(nb, t_len, d_s5)
        xg4 = xg.reshape(nb, t_len, 2 * d_lru)
        y, hf, w_down_b = _s5_seq(u4, w_t, w_s, w_c, tab, d_s5, w_down[l])
        mb, lru_h, w_up_b = _lru_seq(xg4, *lru_args, w_up[l])
        ys, hsr, hsi = _s5_step(zs, state_s5_re[l].reshape(ns, groups * S5_STATE),
                                state_s5_im[l].reshape(ns, groups * S5_STATE),
                                wb, wct, arow, s5_d[l].reshape(1, d_s5))
        mbs, lru_hs, conv_s = _lru_step(zs, state_conv[l], state_lru[l], *lru_args)
        ma, mas = _glu(y.reshape(nb * t_len, d_s5), ys, w_glu_b, s5_b_glu[l], g_merge_a[l], TM_GLU)
        xp, xs = _out_proj(xp, ma, mb.reshape(nb * t_len, d_lru), xs, mas, mbs, w_out[l], TM_OUT)
        xp, xs = _mlp(xp, xs, g_mlp[l], w_up_b, w_down_b, TM_MLP, TF_MLP)
        xp, xs = _ple(xp, p_prompt[l].reshape(nb * t_len, -1), xs, p_sample[l].reshape(ns, -1),
                      g_ple[l], w_ple_gate[l], w_ple[l], g_final, TM_PLE, final)
        hf = jnp.transpose(hf[:, :, :, :nb], (1, 3, 0, 2))
        outs[0].append(hf[0])
        outs[1].append(hf[1])
        outs[2].append(lru_h.reshape(nb, d_lru))
        outs[3].append(xg4[:, t_len - (CONV_W - 1):, :d_lru].astype(F32))
        outs[4].append(hsr.reshape(ns, groups, S5_STATE))
        outs[5].append(hsi.reshape(ns, groups, S5_STATE))
        outs[6].append(lru_hs)
        outs[7].append(conv_s)
    return (xp.reshape(nb, t_len, d_model), xs.reshape(ns, 1, d_model),
            *(jnp.stack(o) for o in outs))
```

```python
import functools

import jax
import jax.numpy as jnp
from jax import lax
from jax.experimental import pallas as pl
from jax.experimental.pallas import tpu as pltpu

F32 = jnp.float32
BF16 = jnp.bfloat16
HIGHEST = lax.Precision.HIGHEST

EPS = 1e-6
LRU_C = 8.0
S5_CH = 16
S5_STATE = 64
LRU_HEADS = 4
CONV_W = 4
CHUNK = 16
SUBLANES = 8
LANES = 128
LANE_GROUPS = LANES // S5_CH
LANE_PAIRS = LANE_GROUPS // 2
ROWS = CHUNK * S5_CH
POW_ROWS = 32
SCAN_ROW0 = POW_ROWS
VMEM_LIMIT = 56 * 1024 * 1024


def _params(*sem):
    return pltpu.CompilerParams(dimension_semantics=sem, vmem_limit_bytes=VMEM_LIMIT)


def _rms(x, g):
    return x * lax.rsqrt(jnp.mean(x * x, axis=-1, keepdims=True) + EPS) * g


def _dot(a, b):
    return jnp.dot(a, b, preferred_element_type=F32)


def _dot_nt(a, b, precision=None):
    return lax.dot_general(a, b, (((1,), (1,)), ((), ())), precision=precision,
                           preferred_element_type=F32)


def _transpose_tiles(x):
    r, c = x.shape
    return jnp.concatenate(
        [jnp.concatenate([x[i:i + LANES, j:j + LANES].T for i in range(0, r, LANES)], axis=1)
         for j in range(0, c, LANES)], axis=0)


def _row_parts(tm, want):
    parts = want if tm % (want * 16) == 0 else 1
    step = tm // parts
    return [slice(r * step, (r + 1) * step) for r in range(parts)]


def _on_last_row_tile(fn):
    pl.when(pl.program_id(0) == pl.num_programs(0) - 1)(fn)


def _norm_matmul_kernel(x_ref, xs_ref, g_ref, w_ref, o_ref, os_ref):
    tn = o_ref.shape[1]
    w = w_ref[:, pl.ds(pl.multiple_of(pl.program_id(1) * tn, tn), tn)]
    for rows in _row_parts(x_ref.shape[0], 4):
        h = _rms(x_ref[rows, :], g_ref[...]).astype(BF16)
        o_ref[rows, :] = _dot(h, w)

    @_on_last_row_tile
    def _():
        os_ref[...] = _dot(_rms(xs_ref[...], g_ref[...]).astype(BF16), w)


def _norm_matmul(x, xs, g, w, tm, tn):
    m, k = x.shape
    ms = xs.shape[0]
    n = w.shape[1]
    ni = m // tm
    return pl.pallas_call(
        _norm_matmul_kernel,
        out_shape=(jax.ShapeDtypeStruct((m, n), F32), jax.ShapeDtypeStruct((ms, n), F32)),
        grid=(ni, n // tn),
        in_specs=[
            pl.BlockSpec((tm, k), lambda i, j: (i, 0)),
            pl.BlockSpec((ms, k), lambda i, j: (0, 0)),
            pl.BlockSpec((1, k), lambda i, j: (0, 0)),
            pl.BlockSpec((k, n), lambda i, j: (0, 0), pipeline_mode=pl.Buffered(1)),
        ],
        out_specs=(pl.BlockSpec((tm, tn), lambda i, j: (i, j)),
                   pl.BlockSpec((ms, tn), lambda i, j: (0, jnp.where(i == ni - 1, j, 0)))),
        compiler_params=_params("arbitrary", "arbitrary"),
        name="norm_w_in",
    )(x, xs, g.reshape(1, k), w)


def _s5_param_kernel(lr_ref, li_ref, ls_ref, bt_r_ref, bt_i_ref, c_r_ref, c_i_ref, d_ref, cast_ref,
                     wt_ref, ws_ref, wc_ref, tab_ref, wb_ref, wct_ref, arow_ref, castb_ref):
    castb_ref[...] = cast_ref[...].astype(BF16)
    row_p = lax.broadcasted_iota(jnp.int32, (POW_ROWS, LANES), 0)
    lane = lax.broadcasted_iota(jnp.int32, (ROWS, LANES), 1)
    lane16 = lax.broadcasted_iota(jnp.int32, (S5_CH, LANES), 1)
    row16 = lax.broadcasted_iota(jnp.int32, (S5_CH, LANES), 0)
    lo = lane < S5_STATE

    wb_ref[...] = jnp.zeros_like(wb_ref)
    wct_ref[...] = jnp.zeros_like(wct_ref)

    for q in range(LANE_PAIRS):
        lr, li = lr_ref[q], li_ref[q]
        step = jnp.exp(ls_ref[q])
        mag = jnp.exp(lr * step)
        ar = mag * jnp.cos(li * step)
        ai = mag * jnp.sin(li * step)
        nr, ni = ar - 1.0, ai
        den = lr * lr + li * li
        cr = (nr * lr + ni * li) / den
        ci = (ni * lr - nr * li) / den
        bt_r, bt_i = bt_r_ref[q], bt_i_ref[q]
        bb_r = cr * bt_r - ci * bt_i
        bb_i = cr * bt_i + ci * bt_r
        c_r, c_i = c_r_ref[q], c_i_ref[q]

        pr = jnp.ones((POW_ROWS, LANES), F32)
        pi = jnp.zeros((POW_ROWS, LANES), F32)
        sr, si = ar, ai
        sq = []
        for m in range(POW_ROWS.bit_length() - 1 + SUBLANES):
            sq.append((sr, si))
            if (1 << m) < POW_ROWS:
                bit = ((row_p >> m) & 1) == 1
                pr, pi = (jnp.where(bit, pr * sr - pi * si, pr),
                          jnp.where(bit, pr * si + pi * sr, pi))
            sr, si = sr * sr - si * si, 2.0 * sr * si
        m0 = CHUNK.bit_length() - 1
        scan_r = jnp.concatenate([sq[m0 + j][0] for j in range(SUBLANES)], axis=0)
        scan_i = jnp.concatenate([sq[m0 + j][1] for j in range(SUBLANES)], axis=0)
        pad = jnp.zeros((LANES - POW_ROWS - SUBLANES, LANES), F32)
        tab_ref[q, 0] = jnp.concatenate([pr, scan_r, pad], axis=0).T
        tab_ref[q, 1] = jnp.concatenate([pi, scan_i, pad], axis=0).T
        arow_ref[q, 0] = pr[:SUBLANES]
        arow_ref[q, 1] = pi[:SUBLANES]

        def expand(tab, k0, sign):
            return jnp.concatenate(
                [jnp.broadcast_to(tab[k0 + sign * s:k0 + sign * s + 1, :], (S5_CH, LANES))
                 for s in range(CHUNK)], axis=0)

        tile = lambda v: jnp.concatenate([v] * CHUNK, axis=0)

        e_r, e_i = expand(pr, CHUNK - 1, -1), expand(pi, CHUNK - 1, -1)
        t_r, t_i = tile(bb_r), tile(bb_i)
        ws_r = _transpose_tiles(t_r * e_r - t_i * e_i)
        ws_i = _transpose_tiles(t_r * e_i + t_i * e_r)
        for h in range(2):
            rows = slice(h * S5_STATE, (h + 1) * S5_STATE)
            ws_ref[2 * q + h] = jnp.concatenate([ws_r[rows], ws_i[rows]], axis=0).astype(BF16)

        e_r, e_i = expand(pr, 1, 1), expand(pi, 1, 1)
        t_r, t_i = tile(c_r), tile(c_i)
        ca_r = t_r * e_r - t_i * e_i
        ca_n = -(t_r * e_i + t_i * e_r)
        wc_ref[2 * q] = jnp.where(lo, ca_r, pltpu.roll(ca_n, S5_STATE, 1)).astype(BF16)
        wc_ref[2 * q + 1] = jnp.where(lo, pltpu.roll(ca_r, S5_STATE, 1), ca_n).astype(BF16)

        e_r, e_i = expand(pr, 0, 1), expand(pi, 0, 1)
        c0_r = t_r * e_r - t_i * e_i
        c0_i = t_r * e_i + t_i * e_r

        for h in range(2):
            g = 2 * q + h
            mine = (lane16 < S5_STATE) == (h == 0)
            bm_r = jnp.where(mine, bb_r, 0.0)
            bm_i = jnp.where(mine, bb_i, 0.0)
            kt = _dot_nt(bm_r, c0_r, HIGHEST) - _dot_nt(bm_i, c0_i, HIGHEST)
            k_lo = kt[:, :LANES] + jnp.where(row16 == lane16, d_ref[g], 0.0)
            k_hi = kt[:, LANES:]
            blocks = []
            half = LANES // S5_CH
            for s in range(CHUNK):
                sh = (s % half) * S5_CH
                keep = lane16 >= sh
                r_lo = pltpu.roll(k_lo, sh, 1) if sh else k_lo
                r_hi = pltpu.roll(k_hi, sh, 1) if sh else k_hi
                if s < half:
                    blk = jnp.concatenate([jnp.where(keep, r_lo, 0.0),
                                           jnp.where(keep, r_hi, r_lo)], axis=1)
                else:
                    blk = jnp.concatenate([jnp.zeros_like(r_lo), jnp.where(keep, r_lo, 0.0)], axis=1)
                blocks.append(blk)
            wt_ref[g] = _transpose_tiles(jnp.concatenate(blocks, axis=0)).astype(BF16)

            r0 = (2 * q + h) * S5_CH
            c0 = q * 2 * LANES
            wb_ref[r0:r0 + S5_CH, c0:c0 + LANES] = bm_r.astype(BF16)
            wb_ref[r0:r0 + S5_CH, c0 + LANES:c0 + 2 * LANES] = bm_i.astype(BF16)
            wct_ref[q, r0:r0 + S5_CH, :LANES] = jnp.where(mine, c_r, 0.0).astype(BF16)
            wct_ref[q, r0:r0 + S5_CH, LANES:] = jnp.where(mine, -c_i, 0.0).astype(BF16)


def _s5_params(lam_re, lam_im, log_step, b_re, b_im, c_re, c_im, d, cast_w):
    groups = lam_re.shape[0]
    pairs = groups // 2
    slabs = groups // LANE_GROUPS
    cast_blk = pl.BlockSpec((cast_w.shape[0] // slabs, cast_w.shape[1]), lambda i: (i, 0))
    pair_row = lambda v: v.reshape(pairs, 1, LANES)
    pair_ch = lambda v: jnp.transpose(v.reshape(pairs, 2, S5_CH, S5_STATE), (0, 2, 1, 3)).reshape(
        pairs, S5_CH, LANES)
    ls = jnp.broadcast_to(log_step[:, None], (groups, S5_STATE))
    d_pad = jnp.pad(d.reshape(groups, 1, S5_CH), ((0, 0), (0, 0), (0, LANES - S5_CH)))
    blk3 = lambda i: (i, 0, 0)
    blk4 = lambda i: (i, 0, 0, 0)
    prow = pl.BlockSpec((LANE_PAIRS, 1, LANES), blk3)
    pch = pl.BlockSpec((LANE_PAIRS, S5_CH, LANES), blk3)
    return pl.pallas_call(
        _s5_param_kernel,
        out_shape=(jax.ShapeDtypeStruct((groups, ROWS, ROWS), BF16),
                   jax.ShapeDtypeStruct((groups, 2 * S5_STATE, ROWS), BF16),
                   jax.ShapeDtypeStruct((groups, ROWS, 2 * S5_STATE), BF16),
                   jax.ShapeDtypeStruct((pairs, 2, LANES, LANES), F32),
                   jax.ShapeDtypeStruct((slabs, LANES, LANE_PAIRS * 2 * LANES), BF16),
                   jax.ShapeDtypeStruct((slabs, LANE_PAIRS, LANES, 2 * LANES), BF16),
                   jax.ShapeDtypeStruct((pairs, 2, SUBLANES, LANES), F32),
                   jax.ShapeDtypeStruct(cast_w.shape, BF16)),
        grid=(slabs,),
        in_specs=[prow, prow, prow, pch, pch, pch, pch,
                  pl.BlockSpec((LANE_GROUPS, 1, LANES), blk3), cast_blk],
        out_specs=(pl.BlockSpec((LANE_GROUPS, ROWS, ROWS), blk3),
                   pl.BlockSpec((LANE_GROUPS, 2 * S5_STATE, ROWS), blk3),
                   pl.BlockSpec((LANE_GROUPS, ROWS, 2 * S5_STATE), blk3),
                   pl.BlockSpec((LANE_PAIRS, 2, LANES, LANES), blk4),
                   pl.BlockSpec((None, LANES, LANE_PAIRS * 2 * LANES), blk3),
                   pl.BlockSpec((None, LANE_PAIRS, LANES, 2 * LANES), blk4),
                   pl.BlockSpec((LANE_PAIRS, 2, SUBLANES, LANES), blk4),
                   cast_blk),
        compiler_params=_params("parallel"),
        name="s5_params",
    )(pair_row(lam_re), pair_row(lam_im), pair_row(ls),
      pair_ch(jnp.transpose(b_re, (0, 2, 1))), pair_ch(jnp.transpose(b_im, (0, 2, 1))),
      pair_ch(c_re), pair_ch(c_im), d_pad, cast_w)


def _s5_seq_kernel(z_ref, wt_ref, ws_ref, wc_ref, tab_ref, cast_ref, y_ref, hf_ref, castb_ref,
                   ut_ref, yt_ref, sr_ref, si_ref, *, nb, n_chunks):
    castb_ref[...] = cast_ref[...].astype(BF16)
    scan_steps = n_chunks.bit_length() - 1
    for n in range(nb):
        for s in range(CHUNK):
            xs = z_ref[n, pl.ds(s, n_chunks, stride=CHUNK), :]
            ut_ref[:, s * S5_CH:(s + 1) * S5_CH, n * n_chunks:(n + 1) * n_chunks] = (
                xs.T.astype(BF16).reshape(LANE_GROUPS, S5_CH, n_chunks))

    for g in range(LANE_GROUPS):
        st = _dot(ws_ref[g], ut_ref[g])
        sr_ref[g * S5_STATE:(g + 1) * S5_STATE, :] = st[:S5_STATE]
        si_ref[g * S5_STATE:(g + 1) * S5_STATE, :] = st[S5_STATE:]

    srows = LANE_GROUPS * S5_STATE
    lane = lax.broadcasted_iota(jnp.int32, (srows, n_chunks), 1)
    tab_r = jnp.concatenate([tab_ref[q, 0] for q in range(LANE_PAIRS)], axis=0)
    tab_i = jnp.concatenate([tab_ref[q, 1] for q in range(LANE_PAIRS)], axis=0)
    fin_r = jnp.zeros((srows, n_chunks), F32)
    fin_i = jnp.zeros((srows, n_chunks), F32)
    for n in range(nb):
        cols = slice(n * n_chunks, (n + 1) * n_chunks)
        xr = sr_ref[:, cols]
        xi = si_ref[:, cols]
        for k in range(scan_steps):
            d = 1 << k
            keep = lane >= d
            sr = jnp.where(keep, pltpu.roll(xr, d, 1), 0.0)
            si = jnp.where(keep, pltpu.roll(xi, d, 1), 0.0)
            pr = tab_r[:, SCAN_ROW0 + k:SCAN_ROW0 + k + 1]
            pi = tab_i[:, SCAN_ROW0 + k:SCAN_ROW0 + k + 1]
            xr, xi = xr + (pr * sr - pi * si), xi + (pr * si + pi * sr)
        last = lane == n_chunks - 1
        fin_r = fin_r + jnp.where(
            lane == n, jnp.sum(jnp.where(last, xr, 0.0), axis=1, keepdims=True), 0.0)
        fin_i = fin_i + jnp.where(
            lane == n, jnp.sum(jnp.where(last, xi, 0.0), axis=1, keepdims=True), 0.0)
        keep = lane >= 1
        sr_ref[:, cols] = jnp.where(keep, pltpu.roll(xr, 1, 1), 0.0)
        si_ref[:, cols] = jnp.where(keep, pltpu.roll(xi, 1, 1), 0.0)

    for g in range(LANE_GROUPS):
        rows = slice(g * S5_STATE, (g + 1) * S5_STATE)
        hf_ref[g, 0] = fin_r[rows]
        hf_ref[g, 1] = fin_i[rows]
        hs = jnp.concatenate([sr_ref[rows, :], si_ref[rows, :]], axis=0).astype(BF16)
        yt_ref[g] = jax.nn.gelu(_dot(wc_ref[g], hs) + _dot(wt_ref[g], ut_ref[g]))

    for n in range(nb):
        for s in range(CHUNK):
            blk = yt_ref[:, s * S5_CH:(s + 1) * S5_CH, n * n_chunks:(n + 1) * n_chunks]
            y_ref[n, pl.ds(s, n_chunks, stride=CHUNK), :] = blk.reshape(LANES, n_chunks).T


def _s5_seq(z, w_t, w_s, w_c, tab, d_s5, cast_w):
    nb, t_len, _ = z.shape
    n_chunks = t_len // CHUNK
    assert n_chunks == LANES, "the chunk axis must fill one 128-lane tile"
    groups = d_s5 // S5_CH
    steps = groups // LANE_GROUPS
    cast_blk = pl.BlockSpec((cast_w.shape[0] // steps, cast_w.shape[1]), lambda i: (i, 0))
    kern = functools.partial(_s5_seq_kernel, nb=nb, n_chunks=n_chunks)
    blk3 = lambda i: (i, 0, 0)
    blk4 = lambda i: (i, 0, 0, 0)
    return pl.pallas_call(
        kern,
        out_shape=(jax.ShapeDtypeStruct((nb, t_len, d_s5), F32),
                   jax.ShapeDtypeStruct((groups, 2, S5_STATE, LANES), F32),
                   jax.ShapeDtypeStruct(cast_w.shape, BF16)),
        grid=(steps,),
        in_specs=[
            pl.BlockSpec((nb, t_len, LANES), lambda i: (0, 0, i)),
            pl.BlockSpec((LANE_GROUPS, ROWS, ROWS), blk3),
            pl.BlockSpec((LANE_GROUPS, 2 * S5_STATE, ROWS), blk3),
            pl.BlockSpec((LANE_GROUPS, ROWS, 2 * S5_STATE), blk3),
            pl.BlockSpec((LANE_PAIRS, 2, LANES, LANES), blk4),
            cast_blk,
        ],
        out_specs=(pl.BlockSpec((nb, t_len, LANES), lambda i: (0, 0, i)),
                   pl.BlockSpec((LANE_GROUPS, 2, S5_STATE, LANES), blk4),
                   cast_blk),
        scratch_shapes=[pltpu.VMEM((LANE_GROUPS, ROWS, nb * n_chunks), BF16),
                        pltpu.VMEM((LANE_GROUPS, ROWS, nb * n_chunks), F32),
                        pltpu.VMEM((LANE_GROUPS * S5_STATE, nb * n_chunks), F32),
                        pltpu.VMEM((LANE_GROUPS * S5_STATE, nb * n_chunks), F32)],
        compiler_params=_params("parallel"),
        name="s5_seq",
    )(z, w_t, w_s, w_c, tab, cast_w)


def _s5_step_kernel(u_ref, h0r_ref, h0i_ref, wb_ref, wct_ref, arow_ref, d_ref,
                    y_ref, hr_ref, hi_ref):
    u = u_ref[...]
    bu = _dot(u.astype(BF16), wb_ref[...])
    y = d_ref[...] * u
    for q in range(LANE_PAIRS):
        cols = slice(q * LANES, (q + 1) * LANES)
        ar = arow_ref[q, 0, 1:2, :]
        ai = arow_ref[q, 1, 1:2, :]
        h0r, h0i = h0r_ref[:, cols], h0i_ref[:, cols]
        hr = ar * h0r - ai * h0i + bu[:, 2 * q * LANES:(2 * q + 1) * LANES]
        hi = ar * h0i + ai * h0r + bu[:, (2 * q + 1) * LANES:(2 * q + 2) * LANES]
        hr_ref[:, cols] = hr
        hi_ref[:, cols] = hi
        y = y + _dot_nt(jnp.concatenate([hr, hi], axis=1).astype(BF16), wct_ref[q])
    y_ref[...] = jax.nn.gelu(y).astype(BF16)


def _s5_step(z, h0r, h0i, wb, wct, arow, d):
    n = z.shape[0]
    slabs = wb.shape[0]
    sw = LANE_PAIRS * LANES
    col = lambda i: (0, i)
    blk3 = lambda i: (i, 0, 0)
    blk4 = lambda i: (i, 0, 0, 0)
    return pl.pallas_call(
        _s5_step_kernel,
        out_shape=(jax.ShapeDtypeStruct((n, slabs * LANES), BF16),
                   jax.ShapeDtypeStruct((n, slabs * sw), F32),
                   jax.ShapeDtypeStruct((n, slabs * sw), F32)),
        grid=(slabs,),
        in_specs=[
            pl.BlockSpec((n, LANES), col),
            pl.BlockSpec((n, sw), col),
            pl.BlockSpec((n, sw), col),
            pl.BlockSpec((None, LANES, 2 * sw), blk3),
            pl.BlockSpec((None, LANE_PAIRS, LANES, 2 * LANES), blk4),
            pl.BlockSpec((LANE_PAIRS, 2, SUBLANES, LANES), blk4),
            pl.BlockSpec((1, LANES), col),
        ],
        out_specs=(pl.BlockSpec((n, LANES), col),
                   pl.BlockSpec((n, sw), col),
                   pl.BlockSpec((n, sw), col)),
        compiler_params=_params("parallel"),
        name="s5_step",
    )(z, h0r, h0i, wb, wct, arow, d)


def _lru_gate_block(xc, wa, ba, wi, bi, lam):
    xb16 = xc.astype(BF16)
    r = jax.nn.sigmoid(_dot(xb16, wa) + ba)
    ig = jax.nn.sigmoid(_dot(xb16, wi) + bi)
    log_a = -LRU_C * r * jax.nn.softplus(-lam)
    a = jnp.exp(log_a)
    mult = jnp.sqrt(1.0 - a * a)
    return a, mult * (ig * xc)


def _lru_gates(xc, wa_ref, ba, wi_ref, bi, lam):
    blk = xc.shape[1] // LRU_HEADS
    parts = [_lru_gate_block(xc[:, h * blk:(h + 1) * blk], wa_ref[h], ba[:, h * blk:(h + 1) * blk],
                             wi_ref[h], bi[:, h * blk:(h + 1) * blk],
                             lam[:, h * blk:(h + 1) * blk]) for h in range(LRU_HEADS)]
    return (jnp.concatenate([p[0] for p in parts], axis=-1),
            jnp.concatenate([p[1] for p in parts], axis=-1))


def _lru_seq_kernel(xb_ref, gb_ref, cw_ref, cb_ref, wa_ref, ba_ref, wi_ref, bi_ref, lam_ref,
                    gm_ref, cast_ref, o_ref, hl_ref, castb_ref, xe_ref, a_ref, b_ref, hc_ref,
                    *, tt):
    castb_ref[...] = cast_ref[...].astype(BF16)
    halo = SUBLANES
    nseq, _, d = xb_ref.shape
    blk = d // LRU_HEADS

    @pl.when(pl.program_id(0) == 0)
    def _():
        xe_ref[:, 0:halo, :] = jnp.zeros((nseq, halo, d), F32)
        hc_ref[...] = jnp.zeros_like(hc_ref)

    xe_ref[:, halo:halo + tt, :] = xb_ref[...]
    for h in range(LRU_HEADS):
        cols = slice(h * blk, (h + 1) * blk)
        cw = cw_ref[:, cols]
        xc = cb_ref[:, cols] + xe_ref[:, halo:halo + tt, cols] * cw[CONV_W - 1:CONV_W, :]
        for k in range(1, CONV_W):
            xc = xc + xe_ref[:, halo - k:halo - k + tt, cols] * cw[CONV_W - 1 - k:CONV_W - k, :]
        a, b = _lru_gate_block(xc.reshape(nseq * tt, blk), wa_ref[h], ba_ref[:, cols],
                               wi_ref[h], bi_ref[:, cols], lam_ref[:, cols])
        a_ref[:, :, cols] = a.reshape(nseq, tt, blk)
        b_ref[:, :, cols] = b.reshape(nseq, tt, blk)
    xe_ref[:, 0:halo, :] = xb_ref[:, tt - halo:tt, :]

    def block(i, h):
        base = pl.multiple_of(i * SUBLANES, SUBLANES)
        for j in range(SUBLANES):
            row = pl.ds(base + j, 1)
            h = a_ref[:, row, :] * h + b_ref[:, row, :]
            b_ref[:, row, :] = h
        return h

    h = lax.fori_loop(0, tt // SUBLANES, block, hc_ref[...])
    hc_ref[...] = h
    hl_ref[...] = h

    out = b_ref[...] * jax.nn.gelu(gb_ref[...])
    o_ref[...] = _rms(out, gm_ref[...]).astype(BF16)


def _lru_seq(z, conv_w, conv_b, w_a, b_a, w_i, b_i, lam, g_merge, cast_w, tt=64):
    nseq, t_len, _ = z.shape
    d = conv_w.shape[1]
    steps = t_len // tt
    cast_blk = pl.BlockSpec((cast_w.shape[0] // steps, cast_w.shape[1]), lambda t: (t, 0))
    row = lambda v: v.reshape(1, d)
    const2 = lambda t: (0, 0)
    const3 = lambda t: (0, 0, 0)
    kern = functools.partial(_lru_seq_kernel, tt=tt)
    return pl.pallas_call(
        kern,
        out_shape=(jax.ShapeDtypeStruct((nseq, t_len, d), BF16),
                   jax.ShapeDtypeStruct((nseq, 1, d), F32),
                   jax.ShapeDtypeStruct(cast_w.shape, BF16)),
        grid=(steps,),
        in_specs=[
            pl.BlockSpec((nseq, tt, d), lambda t: (0, t, 1)),
            pl.BlockSpec((nseq, tt, d), lambda t: (0, t, 2)),
            pl.BlockSpec((CONV_W, d), const2),
            pl.BlockSpec((1, d), const2),
            pl.BlockSpec(w_a.shape, const3),
            pl.BlockSpec((1, d), const2),
            pl.BlockSpec(w_i.shape, const3),
            pl.BlockSpec((1, d), const2),
            pl.BlockSpec((1, d), const2),
            pl.BlockSpec((1, d), const2),
            cast_blk,
        ],
        out_specs=(pl.BlockSpec((nseq, tt, d), lambda t: (0, t, 0)),
                   pl.BlockSpec((nseq, 1, d), const3),
                   cast_blk),
        scratch_shapes=[pltpu.VMEM((nseq, tt + SUBLANES, d), F32),
                        pltpu.VMEM((nseq, tt, d), F32),
                        pltpu.VMEM((nseq, tt, d), F32),
                        pltpu.VMEM((nseq, 1, d), F32)],
        compiler_params=_params("arbitrary"),
        name="rglru_seq",
    )(z, z, conv_w, row(conv_b), w_a, row(b_a), w_i, row(b_i), row(lam), row(g_merge), cast_w)


def _lru_step_kernel(xb_ref, gb_ref, c0_ref, c1_ref, c2_ref, h0_ref, cw_ref, cb_ref,
                     wa_ref, ba_ref, wi_ref, bi_ref, lam_ref, gm_ref, o_ref, h_ref, cv_ref):
    d = xb_ref.shape[1]
    xb = xb_ref[...]
    cw = cw_ref[...]
    xc = (cb_ref[...] + c0_ref[...] * cw[0:1, :] + c1_ref[...] * cw[1:2, :]
          + c2_ref[...] * cw[2:3, :] + xb * cw[3:4, :])
    a, b = _lru_gates(xc, wa_ref, ba_ref[...], wi_ref, bi_ref[...], lam_ref[...])
    h = a * h0_ref[...] + b
    h_ref[...] = h
    out = h * jax.nn.gelu(gb_ref[...])
    o_ref[...] = _rms(out, gm_ref[...]).astype(BF16)
    cv_ref[:, 0:d] = c1_ref[...]
    cv_ref[:, d:2 * d] = c2_ref[...]
    cv_ref[:, 2 * d:3 * d] = xb


def _lru_step(z, conv0, h0, conv_w, conv_b, w_a, b_a, w_i, b_i, lam, g_merge):
    n = z.shape[0]
    d = conv_w.shape[1]
    taps = CONV_W - 1
    conv2d = conv0.reshape(n, taps * d)
    row = lambda v: v.reshape(1, d)
    full = lambda shape: pl.BlockSpec(shape, lambda i: (0,) * len(shape))
    col = lambda c: pl.BlockSpec((n, d), lambda i: (0, c))
    nd = full((n, d))
    rd = full((1, d))
    o, h, cv = pl.pallas_call(
        _lru_step_kernel,
        out_shape=(jax.ShapeDtypeStruct((n, d), BF16), jax.ShapeDtypeStruct((n, d), F32),
                   jax.ShapeDtypeStruct((n, taps * d), F32)),
        grid=(1,),
        in_specs=[
            col(1), col(2), col(0), col(1), col(2), nd,
            full((CONV_W, d)), rd,
            full(w_a.shape), rd, full(w_i.shape), rd, rd, rd,
        ],
        out_specs=(nd, nd, full((n, taps * d))),
        compiler_params=_params("arbitrary"),
        name="rglru_step",
    )(z, z, conv2d, conv2d, conv2d, h0, conv_w, row(conv_b),
      w_a, row(b_a), w_i, row(b_i), row(lam), row(g_merge))
    return o, h, cv.reshape(n, taps, d)


def _glu_rows(y, w_ref, b_ref, g_ref):
    gate = jax.nn.sigmoid(_dot(y.astype(BF16), w_ref[...]) + b_ref[...])
    return _rms(y.astype(F32) * gate, g_ref[...]).astype(BF16)


def _glu_kernel(y_ref, ys_ref, w_ref, b_ref, g_ref, o_ref, os_ref):
    for rows in _row_parts(y_ref.shape[0], 4):
        o_ref[rows, :] = _glu_rows(y_ref[rows, :], w_ref, b_ref, g_ref)

    @_on_last_row_tile
    def _():
        os_ref[...] = _glu_rows(ys_ref[...], w_ref, b_ref, g_ref)


def _glu(y, ys, w, b, g, tm):
    m, d = y.shape
    ms = ys.shape[0]
    const = lambda i: (0, 0)
    return pl.pallas_call(
        _glu_kernel,
        out_shape=(jax.ShapeDtypeStruct((m, d), BF16), jax.ShapeDtypeStruct((ms, d), BF16)),
        grid=(m // tm,),
        in_specs=[
            pl.BlockSpec((tm, d), lambda i: (i, 0)),
            pl.BlockSpec((ms, d), const),
            pl.BlockSpec((d, d), const),
            pl.BlockSpec((1, d), const),
            pl.BlockSpec((1, d), const),
        ],
        out_specs=(pl.BlockSpec((tm, d), lambda i: (i, 0)), pl.BlockSpec((ms, d), const)),
        compiler_params=_params("arbitrary"),
        name="s5_glu",
    )(y, ys, w, b.reshape(1, d), g.reshape(1, d))


def _out_proj_kernel(x_ref, ma_ref, mb_ref, xs_ref, mas_ref, mbs_ref, w_ref, o_ref, os_ref, wb_ref):
    @pl.when(pl.program_id(0) == 0)
    def _():
        wb_ref[...] = w_ref[...].astype(BF16)

    ka = ma_ref.shape[1]
    proj = lambda x, ma, mb: x + _dot(ma, wb_ref[:ka, :]) + _dot(mb, wb_ref[ka:, :])
    for rows in _row_parts(x_ref.shape[0], 2):
        o_ref[rows, :] = proj(x_ref[rows, :], ma_ref[rows, :], mb_ref[rows, :])

    @_on_last_row_tile
    def _():
        os_ref[...] = proj(xs_ref[...], mas_ref[...], mbs_ref[...])


def _out_proj(x, ma, mb, xs, mas, mbs, w, tm):
    m, n = x.shape
    ms = xs.shape[0]
    ka = ma.shape[1]
    kb = mb.shape[1]
    const = lambda i: (0, 0)
    tile = lambda i: (i, 0)
    return pl.pallas_call(
        _out_proj_kernel,
        out_shape=(jax.ShapeDtypeStruct((m, n), F32), jax.ShapeDtypeStruct((ms, n), F32)),
        grid=(m // tm,),
        in_specs=[
            pl.BlockSpec((tm, n), tile),
            pl.BlockSpec((tm, ka), tile),
            pl.BlockSpec((tm, kb), tile),
            pl.BlockSpec((ms, n), const),
            pl.BlockSpec((ms, ka), const),
            pl.BlockSpec((ms, kb), const),
            pl.BlockSpec((ka + kb, n), const, pipeline_mode=pl.Buffered(1)),
        ],
        out_specs=(pl.BlockSpec((tm, n), tile), pl.BlockSpec((ms, n), const)),
        scratch_shapes=[pltpu.VMEM((ka + kb, n), BF16)],
        compiler_params=_params("arbitrary"),
        name="out_proj",
    )(x, ma, mb, xs, mas, mbs, w)


def _mlp_kernel(x_ref, xs_ref, g_ref, wu_ref, wd_ref, o_ref, os_ref, h_ref, hs_ref):
    def rows(x_ref, o_ref, h_ref):
        @pl.when(pl.program_id(1) == 0)
        def _():
            x = x_ref[...]
            h_ref[...] = _rms(x, g_ref[...]).astype(BF16)
            o_ref[...] = x

        up = _dot(h_ref[...], wu_ref[...])
        act = jnp.square(jnp.maximum(up, 0.0)).astype(BF16)
        o_ref[...] += _dot(act, wd_ref[...])

    rows(x_ref, o_ref, h_ref)
    _on_last_row_tile(lambda: rows(xs_ref, os_ref, hs_ref))


def _mlp(x, xs, g, w_up, w_down, tm, tf):
    m, d = x.shape
    ms = xs.shape[0]
    f = w_up.shape[1]
    const = lambda i, j: (0, 0)
    return pl.pallas_call(
        _mlp_kernel,
        out_shape=(jax.ShapeDtypeStruct((m, d), F32), jax.ShapeDtypeStruct((ms, d), F32)),
        grid=(m // tm, f // tf),
        in_specs=[
            pl.BlockSpec((tm, d), lambda i, j: (i, 0)),
            pl.BlockSpec((ms, d), const),
            pl.BlockSpec((1, d), const),
            pl.BlockSpec((d, tf), lambda i, j: (0, j)),
            pl.BlockSpec((tf, d), lambda i, j: (j, 0)),
        ],
        out_specs=(pl.BlockSpec((tm, d), lambda i, j: (i, 0)), pl.BlockSpec((ms, d), const)),
        scratch_shapes=[pltpu.VMEM((tm, d), BF16), pltpu.VMEM((ms, d), BF16)],
        compiler_params=_params("arbitrary", "arbitrary"),
        name="mlp",
    )(x, xs, g.reshape(1, d), w_up, w_down)


def _ple_kernel(x_ref, p_ref, xs_ref, ps_ref, g_ref, wg_ref, wp_ref, gf_ref, o_ref, os_ref,
                wgb_ref, wpb_ref, *, final):
    @pl.when(pl.program_id(0) == 0)
    def _():
        wgb_ref[...] = wg_ref[...].astype(BF16)
        wpb_ref[...] = wp_ref[...].astype(BF16)

    def ple(x, p):
        gate = jax.nn.sigmoid(_dot(_rms(x, g_ref[...]).astype(BF16), wgb_ref[...]))
        pe = _dot(p.astype(BF16), wpb_ref[...])
        x = x + pe * gate
        return _rms(x, gf_ref[...]) if final else x

    for rows in _row_parts(x_ref.shape[0], 2):
        o_ref[rows, :] = ple(x_ref[rows, :], p_ref[rows, :])

    @_on_last_row_tile
    def _():
        os_ref[...] = ple(xs_ref[...], ps_ref[...])


def _ple(x, p, xs, ps, g, w_gate, w_ple, g_final, tm, final):
    m, d = x.shape
    ms = xs.shape[0]
    dp = p.shape[1]
    once = pl.Buffered(1)
    const = lambda i: (0, 0)
    tile = lambda i: (i, 0)
    return pl.pallas_call(
        functools.partial(_ple_kernel, final=final),
        out_shape=(jax.ShapeDtypeStruct((m, d), F32), jax.ShapeDtypeStruct((ms, d), F32)),
        grid=(m // tm,),
        in_specs=[
            pl.BlockSpec((tm, d), tile),
            pl.BlockSpec((tm, dp), tile),
            pl.BlockSpec((ms, d), const),
            pl.BlockSpec((ms, dp), const),
            pl.BlockSpec((1, d), const),
            pl.BlockSpec((d, d), const, pipeline_mode=once),
            pl.BlockSpec((dp, d), const, pipeline_mode=once),
            pl.BlockSpec((1, d), const),
        ],
        out_specs=(pl.BlockSpec((tm, d), tile), pl.BlockSpec((ms, d), const)),
        scratch_shapes=[pltpu.VMEM((d, d), BF16), pltpu.VMEM((dp, d), BF16)],
        compiler_params=_params("arbitrary"),
        name="ple_final",
    )(x, p, xs, ps, g.reshape(1, d), w_gate, w_ple, g_final.reshape(1, d))


TM_IN, TN_IN = 1024, 1024
TM_GLU = 1024
TM_OUT = 512
TM_MLP, TF_MLP = 512, 1024
TM_PLE = 512


def kernel(x_prompt, x_sample, state_s5_re, state_s5_im, state_lru, state_conv, p_prompt, p_sample,
           g_mix, w_in, s5_lam_re, s5_lam_im, s5_log_step, s5_b_re, s5_b_im, s5_c_re, s5_c_im, s5_d,
           s5_w_glu, s5_b_glu, conv_w, conv_b, lru_w_a, lru_b_a, lru_w_i, lru_b_i, lru_lam,
           g_merge_a, g_merge_b, w_out, g_mlp, w_up, w_down, g_ple, w_ple_gate, w_ple, g_final):
    depth = g_mix.shape[0]
    nb, t_len, d_model = x_prompt.shape
    ns = x_sample.shape[0]
    d_s5 = s5_d.shape[1]
    d_lru = conv_w.shape[2]
    groups = d_s5 // S5_CH

    xp = x_prompt.reshape(nb * t_len, d_model)
    xs = x_sample.reshape(ns, d_model)
    outs = [[] for _ in range(8)]
    for l in range(depth):
        final = l == depth - 1
        w_glu_b = s5_w_glu[l].astype(BF16)
        lru_args = (conv_w[l], conv_b[l], lru_w_a[l].astype(BF16), lru_b_a[l],
                    lru_w_i[l].astype(BF16), lru_b_i[l], lru_lam[l], g_merge_b[l])
        w_t, w_s, w_c, tab, wb, wct, arow, w_in_b = _s5_params(
            s5_lam_re[l], s5_lam_im[l], s5_log_step[l], s5_b_re[l], s5_b_im[l],
            s5_c_re[l], s5_c_im[l], s5_d[l], w_in[l])

        z, zs = _norm_matmul(xp, xs, g_mix[l], w_in_b, TM_IN, TN_IN)
        z4 = z.reshape(nb, t_len, 3 * d_s5)
        y, hf, w_down_b = _s5_seq(z4, w_t, w_s, w_c, tab, d_s5, w_down[l])
        mb, lru_h, w_up_b = _lru_seq(z4, *lru_args, w_up[l])
        ys, hsr, hsi = _s5_step(zs, state_s5_re[l].reshape(ns, groups * S5_STATE),
                                state_s5_im[l].reshape(ns, groups * S5_STATE),
                                wb, wct, arow, s5_d[l].reshape(1, d_s5))
        mbs, lru_hs, conv_s = _lru_step(zs, state_conv[l], state_lru[l], *lru_args)
        ma, mas = _glu(y.reshape(nb * t_len, d_s5), ys, w_glu_b, s5_b_glu[l], g_merge_a[l], TM_GLU)
        xp, xs = _out_proj(xp, ma, mb.reshape(nb * t_len, d_lru), xs, mas, mbs, w_out[l], TM_OUT)
        xp, xs = _mlp(xp, xs, g_mlp[l], w_up_b, w_down_b, TM_MLP, TF_MLP)
        xp, xs = _ple(xp, p_prompt[l].reshape(nb * t_len, -1), xs, p_sample[l].reshape(ns, -1),
                      g_ple[l], w_ple_gate[l], w_ple[l], g_final, TM_PLE, final)
        hf = jnp.transpose(hf[:, :, :, :nb], (1, 3, 0, 2))
        outs[0].append(hf[0])
        outs[1].append(hf[1])
        outs[2].append(lru_h.reshape(nb, d_lru))
        outs[3].append(z4[:, t_len - (CONV_W - 1):, d_s5:d_s5 + d_lru])
        outs[4].append(hsr.reshape(ns, groups, S5_STATE))
        outs[5].append(hsi.reshape(ns, groups, S5_STATE))
        outs[6].append(lru_hs)
        outs[7].append(conv_s)
    return (xp.reshape(nb, t_len, d_model), xs.reshape(ns, 1, d_model),
            *(jnp.stack(o) for o in outs))
```

```python
import functools

import jax
import jax.numpy as jnp
from jax import lax
from jax.experimental import pallas as pl
from jax.experimental.pallas import tpu as pltpu

F32 = jnp.float32
BF16 = jnp.bfloat16
HIGHEST = lax.Precision.HIGHEST

EPS = 1e-6
LRU_C = 8.0
S5_CH = 16
S5_STATE = 64
LRU_HEADS = 4
CONV_W = 4
CHUNK = 16
SUBLANES = 8
LANES = 128
LANE_GROUPS = LANES // S5_CH
LANE_PAIRS = LANE_GROUPS // 2
ROWS = CHUNK * S5_CH
POW_ROWS = 32
SCAN_ROW0 = POW_ROWS
VMEM_LIMIT = 56 * 1024 * 1024


def _params(*sem):
    return pltpu.CompilerParams(dimension_semantics=sem, vmem_limit_bytes=VMEM_LIMIT)


def _rms(x, g):
    return x * lax.rsqrt(jnp.mean(x * x, axis=-1, keepdims=True) + EPS) * g


def _dot(a, b):
    return jnp.dot(a, b, preferred_element_type=F32)


def _dot_nt(a, b, precision=None):
    return lax.dot_general(a, b, (((1,), (1,)), ((), ())), precision=precision,
                           preferred_element_type=F32)


def _transpose_tiles(x):
    r, c = x.shape
    return jnp.concatenate(
        [jnp.concatenate([x[i:i + LANES, j:j + LANES].T for i in range(0, r, LANES)], axis=1)
         for j in range(0, c, LANES)], axis=0)


def _row_parts(tm, want):
    parts = max(1, min(want, tm // LANES))
    step = tm // parts
    return [slice(r * step, (r + 1) * step) for r in range(parts)]


def _on_last_row_tile(fn):
    pl.when(pl.program_id(0) == pl.num_programs(0) - 1)(fn)


def _norm_matmul_kernel(x_ref, xs_ref, g_ref, w_ref, o_ref, os_ref):
    tn = o_ref.shape[1]
    w = w_ref[:, pl.ds(pl.multiple_of(pl.program_id(1) * tn, tn), tn)]
    for rows in _row_parts(x_ref.shape[0], 4):
        h = _rms(x_ref[rows, :], g_ref[...]).astype(BF16)
        o_ref[rows, :] = _dot(h, w)

    @_on_last_row_tile
    def _():
        os_ref[...] = _dot(_rms(xs_ref[...], g_ref[...]).astype(BF16), w)


def _norm_matmul(x, xs, g, w, tm, tn):
    m, k = x.shape
    ms = xs.shape[0]
    n = w.shape[1]
    ni = m // tm
    return pl.pallas_call(
        _norm_matmul_kernel,
        out_shape=(jax.ShapeDtypeStruct((m, n), F32), jax.ShapeDtypeStruct((ms, n), F32)),
        grid=(ni, n // tn),
        in_specs=[
            pl.BlockSpec((tm, k), lambda i, j: (i, 0)),
            pl.BlockSpec((ms, k), lambda i, j: (0, 0)),
            pl.BlockSpec((1, k), lambda i, j: (0, 0)),
            pl.BlockSpec((k, n), lambda i, j: (0, 0), pipeline_mode=pl.Buffered(1)),
        ],
        out_specs=(pl.BlockSpec((tm, tn), lambda i, j: (i, j)),
                   pl.BlockSpec((ms, tn), lambda i, j: (0, jnp.where(i == ni - 1, j, 0)))),
        compiler_params=_params("arbitrary", "arbitrary"),
        name="norm_w_in",
    )(x, xs, g.reshape(1, k), w)


def _s5_param_kernel(lr_ref, li_ref, ls_ref, bt_r_ref, bt_i_ref, c_r_ref, c_i_ref, d_ref, cast_ref,
                     wt_ref, ws_ref, wc_ref, tab_ref, wb_ref, wct_ref, arow_ref, castb_ref):
    castb_ref[...] = cast_ref[...].astype(BF16)
    row_p = lax.broadcasted_iota(jnp.int32, (POW_ROWS, LANES), 0)
    lane = lax.broadcasted_iota(jnp.int32, (ROWS, LANES), 1)
    lane16 = lax.broadcasted_iota(jnp.int32, (S5_CH, LANES), 1)
    row16 = lax.broadcasted_iota(jnp.int32, (S5_CH, LANES), 0)
    lo = lane < S5_STATE

    wb_ref[...] = jnp.zeros_like(wb_ref)
    wct_ref[...] = jnp.zeros_like(wct_ref)

    for q in range(LANE_PAIRS):
        lr, li = lr_ref[q], li_ref[q]
        step = jnp.exp(ls_ref[q])
        mag = jnp.exp(lr * step)
        ar = mag * jnp.cos(li * step)
        ai = mag * jnp.sin(li * step)
        nr, ni = ar - 1.0, ai
        den = lr * lr + li * li
        cr = (nr * lr + ni * li) / den
        ci = (ni * lr - nr * li) / den
        bt_r, bt_i = bt_r_ref[q], bt_i_ref[q]
        bb_r = cr * bt_r - ci * bt_i
        bb_i = cr * bt_i + ci * bt_r
        c_r, c_i = c_r_ref[q], c_i_ref[q]

        pr = jnp.ones((POW_ROWS, LANES), F32)
        pi = jnp.zeros((POW_ROWS, LANES), F32)
        sr, si = ar, ai
        sq = []
        for m in range(POW_ROWS.bit_length() - 1 + SUBLANES):
            sq.append((sr, si))
            if (1 << m) < POW_ROWS:
                bit = ((row_p >> m) & 1) == 1
                pr, pi = (jnp.where(bit, pr * sr - pi * si, pr),
                          jnp.where(bit, pr * si + pi * sr, pi))
            sr, si = sr * sr - si * si, 2.0 * sr * si
        m0 = CHUNK.bit_length() - 1
        scan_r = jnp.concatenate([sq[m0 + j][0] for j in range(SUBLANES)], axis=0)
        scan_i = jnp.concatenate([sq[m0 + j][1] for j in range(SUBLANES)], axis=0)
        pad = jnp.zeros((LANES - POW_ROWS - SUBLANES, LANES), F32)
        tab_ref[q, 0] = jnp.concatenate([pr, scan_r, pad], axis=0).T
        tab_ref[q, 1] = jnp.concatenate([pi, scan_i, pad], axis=0).T
        arow_ref[q, 0] = pr[:SUBLANES]
        arow_ref[q, 1] = pi[:SUBLANES]

        def expand(tab, k0, sign):
            return jnp.concatenate(
                [jnp.broadcast_to(tab[k0 + sign * s:k0 + sign * s + 1, :], (S5_CH, LANES))
                 for s in range(CHUNK)], axis=0)

        tile = lambda v: jnp.concatenate([v] * CHUNK, axis=0)

        e_r, e_i = expand(pr, CHUNK - 1, -1), expand(pi, CHUNK - 1, -1)
        t_r, t_i = tile(bb_r), tile(bb_i)
        ws_r = _transpose_tiles(t_r * e_r - t_i * e_i)
        ws_i = _transpose_tiles(t_r * e_i + t_i * e_r)
        for h in range(2):
            rows = slice(h * S5_STATE, (h + 1) * S5_STATE)
            ws_ref[2 * q + h] = jnp.concatenate([ws_r[rows], ws_i[rows]], axis=0).astype(BF16)

        e_r, e_i = expand(pr, 1, 1), expand(pi, 1, 1)
        t_r, t_i = tile(c_r), tile(c_i)
        ca_r = t_r * e_r - t_i * e_i
        ca_n = -(t_r * e_i + t_i * e_r)
        wc_ref[2 * q] = jnp.where(lo, ca_r, pltpu.roll(ca_n, S5_STATE, 1)).astype(BF16)
        wc_ref[2 * q + 1] = jnp.where(lo, pltpu.roll(ca_r, S5_STATE, 1), ca_n).astype(BF16)

        e_r, e_i = expand(pr, 0, 1), expand(pi, 0, 1)
        c0_r = t_r * e_r - t_i * e_i
        c0_i = t_r * e_i + t_i * e_r

        for h in range(2):
            g = 2 * q + h
            mine = (lane16 < S5_STATE) == (h == 0)
            bm_r = jnp.where(mine, bb_r, 0.0)
            bm_i = jnp.where(mine, bb_i, 0.0)
            kt = _dot_nt(bm_r, c0_r, HIGHEST) - _dot_nt(bm_i, c0_i, HIGHEST)
            k_lo = kt[:, :LANES] + jnp.where(row16 == lane16, d_ref[g], 0.0)
            k_hi = kt[:, LANES:]
            blocks = []
            half = LANES // S5_CH
            for s in range(CHUNK):
                sh = (s % half) * S5_CH
                keep = lane16 >= sh
                r_lo = pltpu.roll(k_lo, sh, 1) if sh else k_lo
                r_hi = pltpu.roll(k_hi, sh, 1) if sh else k_hi
                if s < half:
                    blk = jnp.concatenate([jnp.where(keep, r_lo, 0.0),
                                           jnp.where(keep, r_hi, r_lo)], axis=1)
                else:
                    blk = jnp.concatenate([jnp.zeros_like(r_lo), jnp.where(keep, r_lo, 0.0)], axis=1)
                blocks.append(blk)
            wt_ref[g] = _transpose_tiles(jnp.concatenate(blocks, axis=0)).astype(BF16)

            r0 = (2 * q + h) * S5_CH
            c0 = q * 2 * LANES
            wb_ref[r0:r0 + S5_CH, c0:c0 + LANES] = bm_r.astype(BF16)
            wb_ref[r0:r0 + S5_CH, c0 + LANES:c0 + 2 * LANES] = bm_i.astype(BF16)
            wct_ref[q, r0:r0 + S5_CH, :LANES] = jnp.where(mine, c_r, 0.0).astype(BF16)
            wct_ref[q, r0:r0 + S5_CH, LANES:] = jnp.where(mine, -c_i, 0.0).astype(BF16)


def _s5_params(lam_re, lam_im, log_step, b_re, b_im, c_re, c_im, d, cast_w):
    groups = lam_re.shape[0]
    pairs = groups // 2
    slabs = groups // LANE_GROUPS
    cast_blk = pl.BlockSpec((cast_w.shape[0] // slabs, cast_w.shape[1]), lambda i: (i, 0))
    pair_row = lambda v: v.reshape(pairs, 1, LANES)
    pair_ch = lambda v: jnp.transpose(v.reshape(pairs, 2, S5_CH, S5_STATE), (0, 2, 1, 3)).reshape(
        pairs, S5_CH, LANES)
    ls = jnp.broadcast_to(log_step[:, None], (groups, S5_STATE))
    d_pad = jnp.pad(d.reshape(groups, 1, S5_CH), ((0, 0), (0, 0), (0, LANES - S5_CH)))
    blk3 = lambda i: (i, 0, 0)
    blk4 = lambda i: (i, 0, 0, 0)
    prow = pl.BlockSpec((LANE_PAIRS, 1, LANES), blk3)
    pch = pl.BlockSpec((LANE_PAIRS, S5_CH, LANES), blk3)
    return pl.pallas_call(
        _s5_param_kernel,
        out_shape=(jax.ShapeDtypeStruct((groups, ROWS, ROWS), BF16),
                   jax.ShapeDtypeStruct((groups, 2 * S5_STATE, ROWS), BF16),
                   jax.ShapeDtypeStruct((groups, ROWS, 2 * S5_STATE), BF16),
                   jax.ShapeDtypeStruct((pairs, 2, LANES, LANES), F32),
                   jax.ShapeDtypeStruct((slabs, LANES, LANE_PAIRS * 2 * LANES), BF16),
                   jax.ShapeDtypeStruct((slabs, LANE_PAIRS, LANES, 2 * LANES), BF16),
                   jax.ShapeDtypeStruct((pairs, 2, SUBLANES, LANES), F32),
                   jax.ShapeDtypeStruct(cast_w.shape, BF16)),
        grid=(slabs,),
        in_specs=[prow, prow, prow, pch, pch, pch, pch,
                  pl.BlockSpec((LANE_GROUPS, 1, LANES), blk3), cast_blk],
        out_specs=(pl.BlockSpec((LANE_GROUPS, ROWS, ROWS), blk3),
                   pl.BlockSpec((LANE_GROUPS, 2 * S5_STATE, ROWS), blk3),
                   pl.BlockSpec((LANE_GROUPS, ROWS, 2 * S5_STATE), blk3),
                   pl.BlockSpec((LANE_PAIRS, 2, LANES, LANES), blk4),
                   pl.BlockSpec((None, LANES, LANE_PAIRS * 2 * LANES), blk3),
                   pl.BlockSpec((None, LANE_PAIRS, LANES, 2 * LANES), blk4),
                   pl.BlockSpec((LANE_PAIRS, 2, SUBLANES, LANES), blk4),
                   cast_blk),
        compiler_params=_params("parallel"),
        name="s5_params",
    )(pair_row(lam_re), pair_row(lam_im), pair_row(ls),
      pair_ch(jnp.transpose(b_re, (0, 2, 1))), pair_ch(jnp.transpose(b_im, (0, 2, 1))),
      pair_ch(c_re), pair_ch(c_im), d_pad, cast_w)


def _s5_seq_kernel(z_ref, wt_ref, ws_ref, wc_ref, tab_ref, cast_ref, y_ref, hf_ref, castb_ref,
                   ut_ref, yt_ref, sr_ref, si_ref, *, nb, n_chunks):
    castb_ref[...] = cast_ref[...].astype(BF16)
    scan_steps = n_chunks.bit_length() - 1
    for n in range(nb):
        for s in range(CHUNK):
            xs = z_ref[n, pl.ds(s, n_chunks, stride=CHUNK), :]
            ut_ref[:, s * S5_CH:(s + 1) * S5_CH, n * n_chunks:(n + 1) * n_chunks] = (
                xs.T.astype(BF16).reshape(LANE_GROUPS, S5_CH, n_chunks))

    for g in range(LANE_GROUPS):
        st = _dot(ws_ref[g], ut_ref[g])
        sr_ref[g * S5_STATE:(g + 1) * S5_STATE, :] = st[:S5_STATE]
        si_ref[g * S5_STATE:(g + 1) * S5_STATE, :] = st[S5_STATE:]

    srows = LANE_GROUPS * S5_STATE
    lane = lax.broadcasted_iota(jnp.int32, (srows, n_chunks), 1)
    tab_r = jnp.concatenate([tab_ref[q, 0] for q in range(LANE_PAIRS)], axis=0)
    tab_i = jnp.concatenate([tab_ref[q, 1] for q in range(LANE_PAIRS)], axis=0)
    fin_r = jnp.zeros((srows, n_chunks), F32)
    fin_i = jnp.zeros((srows, n_chunks), F32)
    for n in range(nb):
        cols = slice(n * n_chunks, (n + 1) * n_chunks)
        xr = sr_ref[:, cols]
        xi = si_ref[:, cols]
        for k in range(scan_steps):
            d = 1 << k
            keep = lane >= d
            sr = jnp.where(keep, pltpu.roll(xr, d, 1), 0.0)
            si = jnp.where(keep, pltpu.roll(xi, d, 1), 0.0)
            pr = tab_r[:, SCAN_ROW0 + k:SCAN_ROW0 + k + 1]
            pi = tab_i[:, SCAN_ROW0 + k:SCAN_ROW0 + k + 1]
            xr, xi = xr + (pr * sr - pi * si), xi + (pr * si + pi * sr)
        last = lane == n_chunks - 1
        fin_r = fin_r + jnp.where(
            lane == n, jnp.sum(jnp.where(last, xr, 0.0), axis=1, keepdims=True), 0.0)
        fin_i = fin_i + jnp.where(
            lane == n, jnp.sum(jnp.where(last, xi, 0.0), axis=1, keepdims=True), 0.0)
        keep = lane >= 1
        sr_ref[:, cols] = jnp.where(keep, pltpu.roll(xr, 1, 1), 0.0)
        si_ref[:, cols] = jnp.where(keep, pltpu.roll(xi, 1, 1), 0.0)

    for g in range(LANE_GROUPS):
        rows = slice(g * S5_STATE, (g + 1) * S5_STATE)
        hf_ref[g, 0] = fin_r[rows]
        hf_ref[g, 1] = fin_i[rows]
        hs = jnp.concatenate([sr_ref[rows, :], si_ref[rows, :]], axis=0).astype(BF16)
        yt_ref[g] = jax.nn.gelu(_dot(wc_ref[g], hs) + _dot(wt_ref[g], ut_ref[g]))

    for n in range(nb):
        for s in range(CHUNK):
            blk = yt_ref[:, s * S5_CH:(s + 1) * S5_CH, n * n_chunks:(n + 1) * n_chunks]
            y_ref[n, pl.ds(s, n_chunks, stride=CHUNK), :] = blk.reshape(LANES, n_chunks).T


def _s5_seq(z, w_t, w_s, w_c, tab, d_s5, cast_w):
    nb, t_len, _ = z.shape
    n_chunks = t_len // CHUNK
    assert n_chunks == LANES, "the chunk axis must fill one 128-lane tile"
    groups = d_s5 // S5_CH
    steps = groups // LANE_GROUPS
    cast_blk = pl.BlockSpec((cast_w.shape[0] // steps, cast_w.shape[1]), lambda i: (i, 0))
    kern = functools.partial(_s5_seq_kernel, nb=nb, n_chunks=n_chunks)
    blk3 = lambda i: (i, 0, 0)
    blk4 = lambda i: (i, 0, 0, 0)
    return pl.pallas_call(
        kern,
        out_shape=(jax.ShapeDtypeStruct((nb, t_len, d_s5), F32),
                   jax.ShapeDtypeStruct((groups, 2, S5_STATE, LANES), F32),
                   jax.ShapeDtypeStruct(cast_w.shape, BF16)),
        grid=(steps,),
        in_specs=[
            pl.BlockSpec((nb, t_len, LANES), lambda i: (0, 0, i)),
            pl.BlockSpec((LANE_GROUPS, ROWS, ROWS), blk3),
            pl.BlockSpec((LANE_GROUPS, 2 * S5_STATE, ROWS), blk3),
            pl.BlockSpec((LANE_GROUPS, ROWS, 2 * S5_STATE), blk3),
            pl.BlockSpec((LANE_PAIRS, 2, LANES, LANES), blk4),
            cast_blk,
        ],
        out_specs=(pl.BlockSpec((nb, t_len, LANES), lambda i: (0, 0, i)),
                   pl.BlockSpec((LANE_GROUPS, 2, S5_STATE, LANES), blk4),
                   cast_blk),
        scratch_shapes=[pltpu.VMEM((LANE_GROUPS, ROWS, nb * n_chunks), BF16),
                        pltpu.VMEM((LANE_GROUPS, ROWS, nb * n_chunks), F32),
                        pltpu.VMEM((LANE_GROUPS * S5_STATE, nb * n_chunks), F32),
                        pltpu.VMEM((LANE_GROUPS * S5_STATE, nb * n_chunks), F32)],
        compiler_params=_params("parallel"),
        name="s5_seq",
    )(z, w_t, w_s, w_c, tab, cast_w)


def _s5_step_kernel(u_ref, h0r_ref, h0i_ref, wb_ref, wct_ref, arow_ref, d_ref,
                    y_ref, hr_ref, hi_ref):
    u = u_ref[...]
    bu = _dot(u.astype(BF16), wb_ref[...])
    y = d_ref[...] * u
    for q in range(LANE_PAIRS):
        cols = slice(q * LANES, (q + 1) * LANES)
        ar = arow_ref[q, 0, 1:2, :]
        ai = arow_ref[q, 1, 1:2, :]
        h0r, h0i = h0r_ref[:, cols], h0i_ref[:, cols]
        hr = ar * h0r - ai * h0i + bu[:, 2 * q * LANES:(2 * q + 1) * LANES]
        hi = ar * h0i + ai * h0r + bu[:, (2 * q + 1) * LANES:(2 * q + 2) * LANES]
        hr_ref[:, cols] = hr
        hi_ref[:, cols] = hi
        y = y + _dot_nt(jnp.concatenate([hr, hi], axis=1).astype(BF16), wct_ref[q])
    y_ref[...] = jax.nn.gelu(y).astype(BF16)


def _s5_step(z, h0r, h0i, wb, wct, arow, d):
    n = z.shape[0]
    slabs = wb.shape[0]
    sw = LANE_PAIRS * LANES
    col = lambda i: (0, i)
    blk3 = lambda i: (i, 0, 0)
    blk4 = lambda i: (i, 0, 0, 0)
    return pl.pallas_call(
        _s5_step_kernel,
        out_shape=(jax.ShapeDtypeStruct((n, slabs * LANES), BF16),
                   jax.ShapeDtypeStruct((n, slabs * sw), F32),
                   jax.ShapeDtypeStruct((n, slabs * sw), F32)),
        grid=(slabs,),
        in_specs=[
            pl.BlockSpec((n, LANES), col),
            pl.BlockSpec((n, sw), col),
            pl.BlockSpec((n, sw), col),
            pl.BlockSpec((None, LANES, 2 * sw), blk3),
            pl.BlockSpec((None, LANE_PAIRS, LANES, 2 * LANES), blk4),
            pl.BlockSpec((LANE_PAIRS, 2, SUBLANES, LANES), blk4),
            pl.BlockSpec((1, LANES), col),
        ],
        out_specs=(pl.BlockSpec((n, LANES), col),
                   pl.BlockSpec((n, sw), col),
                   pl.BlockSpec((n, sw), col)),
        compiler_params=_params("parallel"),
        name="s5_step",
    )(z, h0r, h0i, wb, wct, arow, d)


def _lru_gate_block(xc, wa, ba, wi, bi, lam):
    xb16 = xc.astype(BF16)
    r = jax.nn.sigmoid(_dot(xb16, wa) + ba)
    ig = jax.nn.sigmoid(_dot(xb16, wi) + bi)
    log_a = -LRU_C * r * jax.nn.softplus(-lam)
    a = jnp.exp(log_a)
    mult = jnp.sqrt(1.0 - a * a)
    return a, mult * (ig * xc)


def _lru_gates(xc, wa_ref, ba, wi_ref, bi, lam):
    blk = xc.shape[1] // LRU_HEADS
    parts = [_lru_gate_block(xc[:, h * blk:(h + 1) * blk], wa_ref[h], ba[:, h * blk:(h + 1) * blk],
                             wi_ref[h], bi[:, h * blk:(h + 1) * blk],
                             lam[:, h * blk:(h + 1) * blk]) for h in range(LRU_HEADS)]
    return (jnp.concatenate([p[0] for p in parts], axis=-1),
            jnp.concatenate([p[1] for p in parts], axis=-1))


def _lru_seq_kernel(xb_ref, gb_ref, cw_ref, cb_ref, wa_ref, ba_ref, wi_ref, bi_ref, lam_ref,
                    gm_ref, cast_ref, o_ref, hl_ref, castb_ref, xe_ref, a_ref, b_ref, hc_ref,
                    *, tt):
    castb_ref[...] = cast_ref[...].astype(BF16)
    halo = SUBLANES
    nseq, _, d = xb_ref.shape
    blk = d // LRU_HEADS

    @pl.when(pl.program_id(0) == 0)
    def _():
        xe_ref[:, 0:halo, :] = jnp.zeros((nseq, halo, d), F32)
        hc_ref[...] = jnp.zeros_like(hc_ref)

    xe_ref[:, halo:halo + tt, :] = xb_ref[...]
    for h in range(LRU_HEADS):
        cols = slice(h * blk, (h + 1) * blk)
        cw = cw_ref[:, cols]
        xc = cb_ref[:, cols] + xe_ref[:, halo:halo + tt, cols] * cw[CONV_W - 1:CONV_W, :]
        for k in range(1, CONV_W):
            xc = xc + xe_ref[:, halo - k:halo - k + tt, cols] * cw[CONV_W - 1 - k:CONV_W - k, :]
        a, b = _lru_gate_block(xc.reshape(nseq * tt, blk), wa_ref[h], ba_ref[:, cols],
                               wi_ref[h], bi_ref[:, cols], lam_ref[:, cols])
        a_ref[:, :, cols] = a.reshape(nseq, tt, blk)
        b_ref[:, :, cols] = b.reshape(nseq, tt, blk)
    xe_ref[:, 0:halo, :] = xb_ref[:, tt - halo:tt, :]

    def block(i, h):
        base = pl.multiple_of(i * SUBLANES, SUBLANES)
        for j in range(SUBLANES):
            row = pl.ds(base + j, 1)
            h = a_ref[:, row, :] * h + b_ref[:, row, :]
            b_ref[:, row, :] = h
        return h

    h = lax.fori_loop(0, tt // SUBLANES, block, hc_ref[...])
    hc_ref[...] = h
    hl_ref[...] = h

    out = b_ref[...] * jax.nn.gelu(gb_ref[...])
    o_ref[...] = _rms(out, gm_ref[...]).astype(BF16)


def _lru_seq(z, conv_w, conv_b, w_a, b_a, w_i, b_i, lam, g_merge, cast_w, tt=64):
    nseq, t_len, _ = z.shape
    d = conv_w.shape[1]
    steps = t_len // tt
    cast_blk = pl.BlockSpec((cast_w.shape[0] // steps, cast_w.shape[1]), lambda t: (t, 0))
    row = lambda v: v.reshape(1, d)
    const2 = lambda t: (0, 0)
    const3 = lambda t: (0, 0, 0)
    kern = functools.partial(_lru_seq_kernel, tt=tt)
    return pl.pallas_call(
        kern,
        out_shape=(jax.ShapeDtypeStruct((nseq, t_len, d), BF16),
                   jax.ShapeDtypeStruct((nseq, 1, d), F32),
                   jax.ShapeDtypeStruct(cast_w.shape, BF16)),
        grid=(steps,),
        in_specs=[
            pl.BlockSpec((nseq, tt, d), lambda t: (0, t, 1)),
            pl.BlockSpec((nseq, tt, d), lambda t: (0, t, 2)),
            pl.BlockSpec((CONV_W, d), const2),
            pl.BlockSpec((1, d), const2),
            pl.BlockSpec(w_a.shape, const3),
            pl.BlockSpec((1, d), const2),
            pl.BlockSpec(w_i.shape, const3),
            pl.BlockSpec((1, d), const2),
            pl.BlockSpec((1, d), const2),
            pl.BlockSpec((1, d), const2),
            cast_blk,
        ],
        out_specs=(pl.BlockSpec((nseq, tt, d), lambda t: (0, t, 0)),
                   pl.BlockSpec((nseq, 1, d), const3),
                   cast_blk),
        scratch_shapes=[pltpu.VMEM((nseq, tt + SUBLANES, d), F32),
                        pltpu.VMEM((nseq, tt, d), F32),
                        pltpu.VMEM((nseq, tt, d), F32),
                        pltpu.VMEM((nseq, 1, d), F32)],
        compiler_params=_params("arbitrary"),
        name="rglru_seq",
    )(z, z, conv_w, row(conv_b), w_a, row(b_a), w_i, row(b_i), row(lam), row(g_merge), cast_w)


def _lru_step_kernel(xb_ref, gb_ref, c0_ref, c1_ref, c2_ref, h0_ref, cw_ref, cb_ref,
                     wa_ref, ba_ref, wi_ref, bi_ref, lam_ref, gm_ref, o_ref, h_ref, cv_ref):
    d = xb_ref.shape[1]
    xb = xb_ref[...]
    cw = cw_ref[...]
    xc = (cb_ref[...] + c0_ref[...] * cw[0:1, :] + c1_ref[...] * cw[1:2, :]
          + c2_ref[...] * cw[2:3, :] + xb * cw[3:4, :])
    a, b = _lru_gates(xc, wa_ref, ba_ref[...], wi_ref, bi_ref[...], lam_ref[...])
    h = a * h0_ref[...] + b
    h_ref[...] = h
    out = h * jax.nn.gelu(gb_ref[...])
    o_ref[...] = _rms(out, gm_ref[...]).astype(BF16)
    cv_ref[:, 0:d] = c1_ref[...]
    cv_ref[:, d:2 * d] = c2_ref[...]
    cv_ref[:, 2 * d:3 * d] = xb


def _lru_step(z, conv0, h0, conv_w, conv_b, w_a, b_a, w_i, b_i, lam, g_merge):
    n = z.shape[0]
    d = conv_w.shape[1]
    taps = CONV_W - 1
    conv2d = conv0.reshape(n, taps * d)
    row = lambda v: v.reshape(1, d)
    full = lambda shape: pl.BlockSpec(shape, lambda i: (0,) * len(shape))
    col = lambda c: pl.BlockSpec((n, d), lambda i: (0, c))
    nd = full((n, d))
    rd = full((1, d))
    o, h, cv = pl.pallas_call(
        _lru_step_kernel,
        out_shape=(jax.ShapeDtypeStruct((n, d), BF16), jax.ShapeDtypeStruct((n, d), F32),
                   jax.ShapeDtypeStruct((n, taps * d), F32)),
        grid=(1,),
        in_specs=[
            col(1), col(2), col(0), col(1), col(2), nd,
            full((CONV_W, d)), rd,
            full(w_a.shape), rd, full(w_i.shape), rd, rd, rd,
        ],
        out_specs=(nd, nd, full((n, taps * d))),
        compiler_params=_params("arbitrary"),
        name="rglru_step",
    )(z, z, conv2d, conv2d, conv2d, h0, conv_w, row(conv_b),
      w_a, row(b_a), w_i, row(b_i), row(lam), row(g_merge))
    return o, h, cv.reshape(n, taps, d)


def _glu_rows(y, w_ref, b_ref, g_ref):
    gate = jax.nn.sigmoid(_dot(y.astype(BF16), w_ref[...]) + b_ref[...])
    return _rms(y.astype(F32) * gate, g_ref[...]).astype(BF16)


def _glu_kernel(y_ref, ys_ref, w_ref, b_ref, g_ref, o_ref, os_ref):
    for rows in _row_parts(y_ref.shape[0], 4):
        o_ref[rows, :] = _glu_rows(y_ref[rows, :], w_ref, b_ref, g_ref)

    @_on_last_row_tile
    def _():
        os_ref[...] = _glu_rows(ys_ref[...], w_ref, b_ref, g_ref)


def _glu(y, ys, w, b, g, tm):
    m, d = y.shape
    ms = ys.shape[0]
    const = lambda i: (0, 0)
    return pl.pallas_call(
        _glu_kernel,
        out_shape=(jax.ShapeDtypeStruct((m, d), BF16), jax.ShapeDtypeStruct((ms, d), BF16)),
        grid=(m // tm,),
        in_specs=[
            pl.BlockSpec((tm, d), lambda i: (i, 0)),
            pl.BlockSpec((ms, d), const),
            pl.BlockSpec((d, d), const),
            pl.BlockSpec((1, d), const),
            pl.BlockSpec((1, d), const),
        ],
        out_specs=(pl.BlockSpec((tm, d), lambda i: (i, 0)), pl.BlockSpec((ms, d), const)),
        compiler_params=_params("arbitrary"),
        name="s5_glu",
    )(y, ys, w, b.reshape(1, d), g.reshape(1, d))


def _out_proj_kernel(x_ref, ma_ref, mb_ref, xs_ref, mas_ref, mbs_ref, w_ref, o_ref, os_ref, wb_ref):
    @pl.when(pl.program_id(0) == 0)
    def _():
        wb_ref[...] = w_ref[...].astype(BF16)

    ka = ma_ref.shape[1]
    proj = lambda x, ma, mb: x + _dot(ma, wb_ref[:ka, :]) + _dot(mb, wb_ref[ka:, :])
    for rows in _row_parts(x_ref.shape[0], 2):
        o_ref[rows, :] = proj(x_ref[rows, :], ma_ref[rows, :], mb_ref[rows, :])

    @_on_last_row_tile
    def _():
        os_ref[...] = proj(xs_ref[...], mas_ref[...], mbs_ref[...])


def _out_proj(x, ma, mb, xs, mas, mbs, w, tm):
    m, n = x.shape
    ms = xs.shape[0]
    ka = ma.shape[1]
    kb = mb.shape[1]
    const = lambda i: (0, 0)
    tile = lambda i: (i, 0)
    return pl.pallas_call(
        _out_proj_kernel,
        out_shape=(jax.ShapeDtypeStruct((m, n), F32), jax.ShapeDtypeStruct((ms, n), F32)),
        grid=(m // tm,),
        in_specs=[
            pl.BlockSpec((tm, n), tile),
            pl.BlockSpec((tm, ka), tile),
            pl.BlockSpec((tm, kb), tile),
            pl.BlockSpec((ms, n), const),
            pl.BlockSpec((ms, ka), const),
            pl.BlockSpec((ms, kb), const),
            pl.BlockSpec((ka + kb, n), const, pipeline_mode=pl.Buffered(1)),
        ],
        out_specs=(pl.BlockSpec((tm, n), tile), pl.BlockSpec((ms, n), const)),
        scratch_shapes=[pltpu.VMEM((ka + kb, n), BF16)],
        compiler_params=_params("arbitrary"),
        name="out_proj",
    )(x, ma, mb, xs, mas, mbs, w)


def _mlp_kernel(x_ref, xs_ref, g_ref, wu_ref, wd_ref, o_ref, os_ref, h_ref, hs_ref):
    def ffn(h):
        act = jnp.square(jnp.maximum(_dot(h, wu_ref[...]), 0.0)).astype(BF16)
        return _dot(act, wd_ref[...])

    def rows(x_ref, o_ref, h_ref):
        first = pl.program_id(1) == 0

        @pl.when(first)
        def _():
            for part in _row_parts(x_ref.shape[0], 4):
                x = x_ref[part, :]
                h = _rms(x, g_ref[...]).astype(BF16)
                h_ref[part, :] = h
                o_ref[part, :] = x + ffn(h)

        @pl.when(jnp.logical_not(first))
        def _():
            o_ref[...] += ffn(h_ref[...])

    rows(x_ref, o_ref, h_ref)
    _on_last_row_tile(lambda: rows(xs_ref, os_ref, hs_ref))


def _mlp(x, xs, g, w_up, w_down, tm, tf):
    m, d = x.shape
    ms = xs.shape[0]
    f = w_up.shape[1]
    const = lambda i, j: (0, 0)
    return pl.pallas_call(
        _mlp_kernel,
        out_shape=(jax.ShapeDtypeStruct((m, d), F32), jax.ShapeDtypeStruct((ms, d), F32)),
        grid=(m // tm, f // tf),
        in_specs=[
            pl.BlockSpec((tm, d), lambda i, j: (i, 0)),
            pl.BlockSpec((ms, d), const),
            pl.BlockSpec((1, d), const),
            pl.BlockSpec((d, tf), lambda i, j: (0, j)),
            pl.BlockSpec((tf, d), lambda i, j: (j, 0)),
        ],
        out_specs=(pl.BlockSpec((tm, d), lambda i, j: (i, 0)), pl.BlockSpec((ms, d), const)),
        scratch_shapes=[pltpu.VMEM((tm, d), BF16), pltpu.VMEM((ms, d), BF16)],
        compiler_params=_params("arbitrary", "arbitrary"),
        name="mlp",
    )(x, xs, g.reshape(1, d), w_up, w_down)


def _ple_kernel(x_ref, p_ref, xs_ref, ps_ref, g_ref, wg_ref, wp_ref, gf_ref, o_ref, os_ref,
                wgb_ref, wpb_ref, *, final):
    @pl.when(pl.program_id(0) == 0)
    def _():
        wgb_ref[...] = wg_ref[...].astype(BF16)
        wpb_ref[...] = wp_ref[...].astype(BF16)

    def ple(x, p):
        gate = jax.nn.sigmoid(_dot(_rms(x, g_ref[...]).astype(BF16), wgb_ref[...]))
        pe = _dot(p.astype(BF16), wpb_ref[...])
        x = x + pe * gate
        return _rms(x, gf_ref[...]) if final else x

    for rows in _row_parts(x_ref.shape[0], 2):
        o_ref[rows, :] = ple(x_ref[rows, :], p_ref[rows, :])

    @_on_last_row_tile
    def _():
        os_ref[...] = ple(xs_ref[...], ps_ref[...])


def _ple(x, p, xs, ps, g, w_gate, w_ple, g_final, tm, final):
    m, d = x.shape
    ms = xs.shape[0]
    dp = p.shape[1]
    once = pl.Buffered(1)
    const = lambda i: (0, 0)
    tile = lambda i: (i, 0)
    return pl.pallas_call(
        functools.partial(_ple_kernel, final=final),
        out_shape=(jax.ShapeDtypeStruct((m, d), F32), jax.ShapeDtypeStruct((ms, d), F32)),
        grid=(m // tm,),
        in_specs=[
            pl.BlockSpec((tm, d), tile),
            pl.BlockSpec((tm, dp), tile),
            pl.BlockSpec((ms, d), const),
            pl.BlockSpec((ms, dp), const),
            pl.BlockSpec((1, d), const),
            pl.BlockSpec((d, d), const, pipeline_mode=once),
            pl.BlockSpec((dp, d), const, pipeline_mode=once),
            pl.BlockSpec((1, d), const),
        ],
        out_specs=(pl.BlockSpec((tm, d), tile), pl.BlockSpec((ms, d), const)),
        scratch_shapes=[pltpu.VMEM((d, d), BF16), pltpu.VMEM((dp, d), BF16)],
        compiler_params=_params("arbitrary"),
        name="ple_final",
    )(x, p, xs, ps, g.reshape(1, d), w_gate, w_ple, g_final.reshape(1, d))


TM_IN, TN_IN = 1024, 1024
TM_GLU = 1024
TM_OUT = 512
TM_MLP, TF_MLP = 512, 1024
TM_PLE = 512


def kernel(x_prompt, x_sample, state_s5_re, state_s5_im, state_lru, state_conv, p_prompt, p_sample,
           g_mix, w_in, s5_lam_re, s5_lam_im, s5_log_step, s5_b_re, s5_b_im, s5_c_re, s5_c_im, s5_d,
           s5_w_glu, s5_b_glu, conv_w, conv_b, lru_w_a, lru_b_a, lru_w_i, lru_b_i, lru_lam,
           g_merge_a, g_merge_b, w_out, g_mlp, w_up, w_down, g_ple, w_ple_gate, w_ple, g_final):
    depth = g_mix.shape[0]
    nb, t_len, d_model = x_prompt.shape
    ns = x_sample.shape[0]
    d_s5 = s5_d.shape[1]
    d_lru = conv_w.shape[2]
    groups = d_s5 // S5_CH

    xp = x_prompt.reshape(nb * t_len, d_model)
    xs = x_sample.reshape(ns, d_model)
    outs = [[] for _ in range(8)]
    for l in range(depth):
        final = l == depth - 1
        w_glu_b = s5_w_glu[l].astype(BF16)
        lru_args = (conv_w[l], conv_b[l], lru_w_a[l].astype(BF16), lru_b_a[l],
                    lru_w_i[l].astype(BF16), lru_b_i[l], lru_lam[l], g_merge_b[l])
        w_t, w_s, w_c, tab, wb, wct, arow, w_in_b = _s5_params(
            s5_lam_re[l], s5_lam_im[l], s5_log_step[l], s5_b_re[l], s5_b_im[l],
            s5_c_re[l], s5_c_im[l], s5_d[l], w_in[l])

        z, zs = _norm_matmul(xp, xs, g_mix[l], w_in_b, TM_IN, TN_IN)
        z4 = z.reshape(nb, t_len, 3 * d_s5)
        y, hf, w_down_b = _s5_seq(z4, w_t, w_s, w_c, tab, d_s5, w_down[l])
        mb, lru_h, w_up_b = _lru_seq(z4, *lru_args, w_up[l])
        ys, hsr, hsi = _s5_step(zs, state_s5_re[l].reshape(ns, groups * S5_STATE),
                                state_s5_im[l].reshape(ns, groups * S5_STATE),
                                wb, wct, arow, s5_d[l].reshape(1, d_s5))
        mbs, lru_hs, conv_s = _lru_step(zs, state_conv[l], state_lru[l], *lru_args)
        ma, mas = _glu(y.reshape(nb * t_len, d_s5), ys, w_glu_b, s5_b_glu[l], g_merge_a[l], TM_GLU)
        xp, xs = _out_proj(xp, ma, mb.reshape(nb * t_len, d_lru), xs, mas, mbs, w_out[l], TM_OUT)
        xp, xs = _mlp(xp, xs, g_mlp[l], w_up_b, w_down_b, TM_MLP, TF_MLP)
        xp, xs = _ple(xp, p_prompt[l].reshape(nb * t_len, -1), xs, p_sample[l].reshape(ns, -1),
                      g_ple[l], w_ple_gate[l], w_ple[l], g_final, TM_PLE, final)
        hf = jnp.transpose(hf[:, :, :, :nb], (1, 3, 0, 2))
        outs[0].append(hf[0])
        outs[1].append(hf[1])
        outs[2].append(lru_h.reshape(nb, d_lru))
        outs[3].append(z4[:, t_len - (CONV_W - 1):, d_s5:d_s5 + d_lru])
        outs[4].append(hsr.reshape(ns, groups, S5_STATE))
        outs[5].append(hsi.reshape(ns, groups, S5_STATE))
        outs[6].append(lru_hs)
        outs[7].append(conv_s)
    return (xp.reshape(nb, t_len, d_model), xs.reshape(ns, 1, d_model),
            *(jnp.stack(o) for o in outs))
```

```python
import functools

import jax
import jax.numpy as jnp
from jax import lax
from jax.experimental import pallas as pl
from jax.experimental.pallas import tpu as pltpu

F32 = jnp.float32
BF16 = jnp.bfloat16
HIGHEST = lax.Precision.HIGHEST

EPS = 1e-6
LRU_C = 8.0
S5_CH = 16
S5_STATE = 64
LRU_HEADS = 4
CONV_W = 4
CHUNK = 16
SUBLANES = 8
LANES = 128
LANE_GROUPS = LANES // S5_CH
LANE_PAIRS = LANE_GROUPS // 2
ROWS = CHUNK * S5_CH
POW_ROWS = 32
SCAN_ROW0 = POW_ROWS
VMEM_LIMIT = 56 * 1024 * 1024


def _params(*sem):
    return pltpu.CompilerParams(dimension_semantics=sem, vmem_limit_bytes=VMEM_LIMIT)


def _rms(x, g):
    return x * lax.rsqrt(jnp.mean(x * x, axis=-1, keepdims=True) + EPS) * g


def _dot(a, b):
    return jnp.dot(a, b, preferred_element_type=F32)


def _dot_nt(a, b, precision=None):
    return lax.dot_general(a, b, (((1,), (1,)), ((), ())), precision=precision,
                           preferred_element_type=F32)


def _transpose_tiles(x):
    r, c = x.shape
    return jnp.concatenate(
        [jnp.concatenate([x[i:i + LANES, j:j + LANES].T for i in range(0, r, LANES)], axis=1)
         for j in range(0, c, LANES)], axis=0)


def _row_parts(tm, want):
    parts = max(1, min(want, tm // LANES))
    step = tm // parts
    return [slice(r * step, (r + 1) * step) for r in range(parts)]


def _on_last_row_tile(fn):
    pl.when(pl.program_id(0) == pl.num_programs(0) - 1)(fn)


def _norm_matmul_kernel(x_ref, xs_ref, g_ref, w_ref, o_ref, os_ref):
    tn = o_ref.shape[1]
    w = w_ref[:, pl.ds(pl.multiple_of(pl.program_id(1) * tn, tn), tn)]
    for rows in _row_parts(x_ref.shape[0], 8):
        h = _rms(x_ref[rows, :], g_ref[...]).astype(BF16)
        o_ref[rows, :] = _dot(h, w)

    @_on_last_row_tile
    def _():
        os_ref[...] = _dot(_rms(xs_ref[...], g_ref[...]).astype(BF16), w)


def _norm_matmul(x, xs, g, w, tm, tn):
    m, k = x.shape
    ms = xs.shape[0]
    n = w.shape[1]
    ni = m // tm
    return pl.pallas_call(
        _norm_matmul_kernel,
        out_shape=(jax.ShapeDtypeStruct((m, n), F32), jax.ShapeDtypeStruct((ms, n), F32)),
        grid=(ni, n // tn),
        in_specs=[
            pl.BlockSpec((tm, k), lambda i, j: (i, 0)),
            pl.BlockSpec((ms, k), lambda i, j: (0, 0)),
            pl.BlockSpec((1, k), lambda i, j: (0, 0)),
            pl.BlockSpec((k, n), lambda i, j: (0, 0), pipeline_mode=pl.Buffered(1)),
        ],
        out_specs=(pl.BlockSpec((tm, tn), lambda i, j: (i, j)),
                   pl.BlockSpec((ms, tn), lambda i, j: (0, jnp.where(i == ni - 1, j, 0)))),
        compiler_params=_params("arbitrary", "arbitrary"),
        name="norm_w_in",
    )(x, xs, g.reshape(1, k), w)


def _s5_param_kernel(lr_ref, li_ref, ls_ref, bt_r_ref, bt_i_ref, c_r_ref, c_i_ref, d_ref, cast_ref,
                     wt_ref, ws_ref, wc_ref, tab_ref, wb_ref, wct_ref, arow_ref, castb_ref):
    castb_ref[...] = cast_ref[...].astype(BF16)
    row_p = lax.broadcasted_iota(jnp.int32, (POW_ROWS, LANES), 0)
    lane = lax.broadcasted_iota(jnp.int32, (ROWS, LANES), 1)
    lane16 = lax.broadcasted_iota(jnp.int32, (S5_CH, LANES), 1)
    row16 = lax.broadcasted_iota(jnp.int32, (S5_CH, LANES), 0)
    lo = lane < S5_STATE

    wb_ref[...] = jnp.zeros_like(wb_ref)
    wct_ref[...] = jnp.zeros_like(wct_ref)

    for q in range(LANE_PAIRS):
        lr, li = lr_ref[q], li_ref[q]
        step = jnp.exp(ls_ref[q])
        mag = jnp.exp(lr * step)
        ar = mag * jnp.cos(li * step)
        ai = mag * jnp.sin(li * step)
        nr, ni = ar - 1.0, ai
        den = lr * lr + li * li
        cr = (nr * lr + ni * li) / den
        ci = (ni * lr - nr * li) / den
        bt_r, bt_i = bt_r_ref[q], bt_i_ref[q]
        bb_r = cr * bt_r - ci * bt_i
        bb_i = cr * bt_i + ci * bt_r
        c_r, c_i = c_r_ref[q], c_i_ref[q]

        pr = jnp.ones((POW_ROWS, LANES), F32)
        pi = jnp.zeros((POW_ROWS, LANES), F32)
        sr, si = ar, ai
        sq = []
        for m in range(POW_ROWS.bit_length() - 1 + SUBLANES):
            sq.append((sr, si))
            if (1 << m) < POW_ROWS:
                bit = ((row_p >> m) & 1) == 1
                pr, pi = (jnp.where(bit, pr * sr - pi * si, pr),
                          jnp.where(bit, pr * si + pi * sr, pi))
            sr, si = sr * sr - si * si, 2.0 * sr * si
        m0 = CHUNK.bit_length() - 1
        scan_r = jnp.concatenate([sq[m0 + j][0] for j in range(SUBLANES)], axis=0)
        scan_i = jnp.concatenate([sq[m0 + j][1] for j in range(SUBLANES)], axis=0)
        pad = jnp.zeros((LANES - POW_ROWS - SUBLANES, LANES), F32)
        tab_ref[q, 0] = jnp.concatenate([pr, scan_r, pad], axis=0).T
        tab_ref[q, 1] = jnp.concatenate([pi, scan_i, pad], axis=0).T
        arow_ref[q, 0] = pr[:SUBLANES]
        arow_ref[q, 1] = pi[:SUBLANES]

        def expand(tab, k0, sign):
            return jnp.concatenate(
                [jnp.broadcast_to(tab[k0 + sign * s:k0 + sign * s + 1, :], (S5_CH, LANES))
                 for s in range(CHUNK)], axis=0)

        tile = lambda v: jnp.concatenate([v] * CHUNK, axis=0)

        e_r, e_i = expand(pr, CHUNK - 1, -1), expand(pi, CHUNK - 1, -1)
        t_r, t_i = tile(bb_r), tile(bb_i)
        ws_r = _transpose_tiles(t_r * e_r - t_i * e_i)
        ws_i = _transpose_tiles(t_r * e_i + t_i * e_r)
        for h in range(2):
            rows = slice(h * S5_STATE, (h + 1) * S5_STATE)
            ws_ref[2 * q + h] = jnp.concatenate([ws_r[rows], ws_i[rows]], axis=0).astype(BF16)

        e_r, e_i = expand(pr, 1, 1), expand(pi, 1, 1)
        t_r, t_i = tile(c_r), tile(c_i)
        ca_r = t_r * e_r - t_i * e_i
        ca_n = -(t_r * e_i + t_i * e_r)
        wc_ref[2 * q] = jnp.where(lo, ca_r, pltpu.roll(ca_n, S5_STATE, 1)).astype(BF16)
        wc_ref[2 * q + 1] = jnp.where(lo, pltpu.roll(ca_r, S5_STATE, 1), ca_n).astype(BF16)

        e_r, e_i = expand(pr, 0, 1), expand(pi, 0, 1)
        c0_r = t_r * e_r - t_i * e_i
        c0_i = t_r * e_i + t_i * e_r

        for h in range(2):
            g = 2 * q + h
            mine = (lane16 < S5_STATE) == (h == 0)
            bm_r = jnp.where(mine, bb_r, 0.0)
            bm_i = jnp.where(mine, bb_i, 0.0)
            kt = _dot_nt(bm_r, c0_r, HIGHEST) - _dot_nt(bm_i, c0_i, HIGHEST)
            k_lo = kt[:, :LANES] + jnp.where(row16 == lane16, d_ref[g], 0.0)
            k_hi = kt[:, LANES:]
            blocks = []
            half = LANES // S5_CH
            for s in range(CHUNK):
                sh = (s % half) * S5_CH
                keep = lane16 >= sh
                r_lo = pltpu.roll(k_lo, sh, 1) if sh else k_lo
                r_hi = pltpu.roll(k_hi, sh, 1) if sh else k_hi
                if s < half:
                    blk = jnp.concatenate([jnp.where(keep, r_lo, 0.0),
                                           jnp.where(keep, r_hi, r_lo)], axis=1)
                else:
                    blk = jnp.concatenate([jnp.zeros_like(r_lo), jnp.where(keep, r_lo, 0.0)], axis=1)
                blocks.append(blk)
            wt_ref[g] = _transpose_tiles(jnp.concatenate(blocks, axis=0)).astype(BF16)

            r0 = (2 * q + h) * S5_CH
            c0 = q * 2 * LANES
            wb_ref[r0:r0 + S5_CH, c0:c0 + LANES] = bm_r.astype(BF16)
            wb_ref[r0:r0 + S5_CH, c0 + LANES:c0 + 2 * LANES] = bm_i.astype(BF16)
            wct_ref[q, r0:r0 + S5_CH, :LANES] = jnp.where(mine, c_r, 0.0).astype(BF16)
            wct_ref[q, r0:r0 + S5_CH, LANES:] = jnp.where(mine, -c_i, 0.0).astype(BF16)


def _s5_params(lam_re, lam_im, log_step, b_re, b_im, c_re, c_im, d, cast_w):
    groups = lam_re.shape[0]
    pairs = groups // 2
    slabs = groups // LANE_GROUPS
    cast_blk = pl.BlockSpec((cast_w.shape[0] // slabs, cast_w.shape[1]), lambda i: (i, 0))
    pair_row = lambda v: v.reshape(pairs, 1, LANES)
    pair_ch = lambda v: jnp.transpose(v.reshape(pairs, 2, S5_CH, S5_STATE), (0, 2, 1, 3)).reshape(
        pairs, S5_CH, LANES)
    ls = jnp.broadcast_to(log_step[:, None], (groups, S5_STATE))
    d_pad = jnp.pad(d.reshape(groups, 1, S5_CH), ((0, 0), (0, 0), (0, LANES - S5_CH)))
    blk3 = lambda i: (i, 0, 0)
    blk4 = lambda i: (i, 0, 0, 0)
    prow = pl.BlockSpec((LANE_PAIRS, 1, LANES), blk3)
    pch = pl.BlockSpec((LANE_PAIRS, S5_CH, LANES), blk3)
    return pl.pallas_call(
        _s5_param_kernel,
        out_shape=(jax.ShapeDtypeStruct((groups, ROWS, ROWS), BF16),
                   jax.ShapeDtypeStruct((groups, 2 * S5_STATE, ROWS), BF16),
                   jax.ShapeDtypeStruct((groups, ROWS, 2 * S5_STATE), BF16),
                   jax.ShapeDtypeStruct((pairs, 2, LANES, LANES), F32),
                   jax.ShapeDtypeStruct((slabs, LANES, LANE_PAIRS * 2 * LANES), BF16),
                   jax.ShapeDtypeStruct((slabs, LANE_PAIRS, LANES, 2 * LANES), BF16),
                   jax.ShapeDtypeStruct((pairs, 2, SUBLANES, LANES), F32),
                   jax.ShapeDtypeStruct(cast_w.shape, BF16)),
        grid=(slabs,),
        in_specs=[prow, prow, prow, pch, pch, pch, pch,
                  pl.BlockSpec((LANE_GROUPS, 1, LANES), blk3), cast_blk],
        out_specs=(pl.BlockSpec((LANE_GROUPS, ROWS, ROWS), blk3),
                   pl.BlockSpec((LANE_GROUPS, 2 * S5_STATE, ROWS), blk3),
                   pl.BlockSpec((LANE_GROUPS, ROWS, 2 * S5_STATE), blk3),
                   pl.BlockSpec((LANE_PAIRS, 2, LANES, LANES), blk4),
                   pl.BlockSpec((None, LANES, LANE_PAIRS * 2 * LANES), blk3),
                   pl.BlockSpec((None, LANE_PAIRS, LANES, 2 * LANES), blk4),
                   pl.BlockSpec((LANE_PAIRS, 2, SUBLANES, LANES), blk4),
                   cast_blk),
        compiler_params=_params("parallel"),
        name="s5_params",
    )(pair_row(lam_re), pair_row(lam_im), pair_row(ls),
      pair_ch(jnp.transpose(b_re, (0, 2, 1))), pair_ch(jnp.transpose(b_im, (0, 2, 1))),
      pair_ch(c_re), pair_ch(c_im), d_pad, cast_w)


def _s5_seq_kernel(z_ref, wt_ref, ws_ref, wc_ref, tab_ref, cast_ref, y_ref, hf_ref, castb_ref,
                   ut_ref, yt_ref, sr_ref, si_ref, *, nb, n_chunks):
    castb_ref[...] = cast_ref[...].astype(BF16)
    scan_steps = n_chunks.bit_length() - 1
    for n in range(nb):
        for s in range(CHUNK):
            xs = z_ref[n, pl.ds(s, n_chunks, stride=CHUNK), :]
            ut_ref[:, s * S5_CH:(s + 1) * S5_CH, n * n_chunks:(n + 1) * n_chunks] = (
                xs.T.astype(BF16).reshape(LANE_GROUPS, S5_CH, n_chunks))

    for g in range(LANE_GROUPS):
        st = _dot(ws_ref[g], ut_ref[g])
        sr_ref[g * S5_STATE:(g + 1) * S5_STATE, :] = st[:S5_STATE]
        si_ref[g * S5_STATE:(g + 1) * S5_STATE, :] = st[S5_STATE:]

    srows = LANE_GROUPS * S5_STATE
    lane = lax.broadcasted_iota(jnp.int32, (srows, n_chunks), 1)
    tab_r = jnp.concatenate([tab_ref[q, 0] for q in range(LANE_PAIRS)], axis=0)
    tab_i = jnp.concatenate([tab_ref[q, 1] for q in range(LANE_PAIRS)], axis=0)
    fin_r = jnp.zeros((srows, n_chunks), F32)
    fin_i = jnp.zeros((srows, n_chunks), F32)
    for n in range(nb):
        cols = slice(n * n_chunks, (n + 1) * n_chunks)
        xr = sr_ref[:, cols]
        xi = si_ref[:, cols]
        for k in range(scan_steps):
            d = 1 << k
            keep = lane >= d
            sr = jnp.where(keep, pltpu.roll(xr, d, 1), 0.0)
            si = jnp.where(keep, pltpu.roll(xi, d, 1), 0.0)
            pr = tab_r[:, SCAN_ROW0 + k:SCAN_ROW0 + k + 1]
            pi = tab_i[:, SCAN_ROW0 + k:SCAN_ROW0 + k + 1]
            xr, xi = xr + (pr * sr - pi * si), xi + (pr * si + pi * sr)
        last = lane == n_chunks - 1
        fin_r = fin_r + jnp.where(
            lane == n, jnp.sum(jnp.where(last, xr, 0.0), axis=1, keepdims=True), 0.0)
        fin_i = fin_i + jnp.where(
            lane == n, jnp.sum(jnp.where(last, xi, 0.0), axis=1, keepdims=True), 0.0)
        keep = lane >= 1
        sr_ref[:, cols] = jnp.where(keep, pltpu.roll(xr, 1, 1), 0.0)
        si_ref[:, cols] = jnp.where(keep, pltpu.roll(xi, 1, 1), 0.0)

    for g in range(LANE_GROUPS):
        rows = slice(g * S5_STATE, (g + 1) * S5_STATE)
        hf_ref[g, 0] = fin_r[rows]
        hf_ref[g, 1] = fin_i[rows]
        hs = jnp.concatenate([sr_ref[rows, :], si_ref[rows, :]], axis=0).astype(BF16)
        yt_ref[g] = jax.nn.gelu(_dot(wc_ref[g], hs) + _dot(wt_ref[g], ut_ref[g]))

    for n in range(nb):
        for s in range(CHUNK):
            blk = yt_ref[:, s * S5_CH:(s + 1) * S5_CH, n * n_chunks:(n + 1) * n_chunks]
            y_ref[n, pl.ds(s, n_chunks, stride=CHUNK), :] = blk.reshape(LANES, n_chunks).T


def _s5_seq(z, w_t, w_s, w_c, tab, d_s5, cast_w):
    nb, t_len, _ = z.shape
    n_chunks = t_len // CHUNK
    assert n_chunks == LANES, "the chunk axis must fill one 128-lane tile"
    groups = d_s5 // S5_CH
    steps = groups // LANE_GROUPS
    cast_blk = pl.BlockSpec((cast_w.shape[0] // steps, cast_w.shape[1]), lambda i: (i, 0))
    kern = functools.partial(_s5_seq_kernel, nb=nb, n_chunks=n_chunks)
    blk3 = lambda i: (i, 0, 0)
    blk4 = lambda i: (i, 0, 0, 0)
    return pl.pallas_call(
        kern,
        out_shape=(jax.ShapeDtypeStruct((nb, t_len, d_s5), F32),
                   jax.ShapeDtypeStruct((groups, 2, S5_STATE, LANES), F32),
                   jax.ShapeDtypeStruct(cast_w.shape, BF16)),
        grid=(steps,),
        in_specs=[
            pl.BlockSpec((nb, t_len, LANES), lambda i: (0, 0, i)),
            pl.BlockSpec((LANE_GROUPS, ROWS, ROWS), blk3),
            pl.BlockSpec((LANE_GROUPS, 2 * S5_STATE, ROWS), blk3),
            pl.BlockSpec((LANE_GROUPS, ROWS, 2 * S5_STATE), blk3),
            pl.BlockSpec((LANE_PAIRS, 2, LANES, LANES), blk4),
            cast_blk,
        ],
        out_specs=(pl.BlockSpec((nb, t_len, LANES), lambda i: (0, 0, i)),
                   pl.BlockSpec((LANE_GROUPS, 2, S5_STATE, LANES), blk4),
                   cast_blk),
        scratch_shapes=[pltpu.VMEM((LANE_GROUPS, ROWS, nb * n_chunks), BF16),
                        pltpu.VMEM((LANE_GROUPS, ROWS, nb * n_chunks), F32),
                        pltpu.VMEM((LANE_GROUPS * S5_STATE, nb * n_chunks), F32),
                        pltpu.VMEM((LANE_GROUPS * S5_STATE, nb * n_chunks), F32)],
        compiler_params=_params("parallel"),
        name="s5_seq",
    )(z, w_t, w_s, w_c, tab, cast_w)


def _s5_step_kernel(u_ref, h0r_ref, h0i_ref, wb_ref, wct_ref, arow_ref, d_ref,
                    y_ref, hr_ref, hi_ref):
    u = u_ref[...]
    bu = _dot(u.astype(BF16), wb_ref[...])
    y = d_ref[...] * u
    for q in range(LANE_PAIRS):
        cols = slice(q * LANES, (q + 1) * LANES)
        ar = arow_ref[q, 0, 1:2, :]
        ai = arow_ref[q, 1, 1:2, :]
        h0r, h0i = h0r_ref[:, cols], h0i_ref[:, cols]
        hr = ar * h0r - ai * h0i + bu[:, 2 * q * LANES:(2 * q + 1) * LANES]
        hi = ar * h0i + ai * h0r + bu[:, (2 * q + 1) * LANES:(2 * q + 2) * LANES]
        hr_ref[:, cols] = hr
        hi_ref[:, cols] = hi
        y = y + _dot_nt(jnp.concatenate([hr, hi], axis=1).astype(BF16), wct_ref[q])
    y_ref[...] = jax.nn.gelu(y).astype(BF16)


def _s5_step(z, h0r, h0i, wb, wct, arow, d):
    n = z.shape[0]
    slabs = wb.shape[0]
    sw = LANE_PAIRS * LANES
    col = lambda i: (0, i)
    blk3 = lambda i: (i, 0, 0)
    blk4 = lambda i: (i, 0, 0, 0)
    return pl.pallas_call(
        _s5_step_kernel,
        out_shape=(jax.ShapeDtypeStruct((n, slabs * LANES), BF16),
                   jax.ShapeDtypeStruct((n, slabs * sw), F32),
                   jax.ShapeDtypeStruct((n, slabs * sw), F32)),
        grid=(slabs,),
        in_specs=[
            pl.BlockSpec((n, LANES), col),
            pl.BlockSpec((n, sw), col),
            pl.BlockSpec((n, sw), col),
            pl.BlockSpec((None, LANES, 2 * sw), blk3),
            pl.BlockSpec((None, LANE_PAIRS, LANES, 2 * LANES), blk4),
            pl.BlockSpec((LANE_PAIRS, 2, SUBLANES, LANES), blk4),
            pl.BlockSpec((1, LANES), col),
        ],
        out_specs=(pl.BlockSpec((n, LANES), col),
                   pl.BlockSpec((n, sw), col),
                   pl.BlockSpec((n, sw), col)),
        compiler_params=_params("parallel"),
        name="s5_step",
    )(z, h0r, h0i, wb, wct, arow, d)


def _lru_gate_block(xc, wa, ba, wi, bi, lam):
    xb16 = xc.astype(BF16)
    r = jax.nn.sigmoid(_dot(xb16, wa) + ba)
    ig = jax.nn.sigmoid(_dot(xb16, wi) + bi)
    log_a = -LRU_C * r * jax.nn.softplus(-lam)
    a = jnp.exp(log_a)
    mult = jnp.sqrt(1.0 - a * a)
    return a, mult * (ig * xc)


def _lru_gates(xc, wa_ref, ba, wi_ref, bi, lam):
    blk = xc.shape[1] // LRU_HEADS
    parts = [_lru_gate_block(xc[:, h * blk:(h + 1) * blk], wa_ref[h], ba[:, h * blk:(h + 1) * blk],
                             wi_ref[h], bi[:, h * blk:(h + 1) * blk],
                             lam[:, h * blk:(h + 1) * blk]) for h in range(LRU_HEADS)]
    return (jnp.concatenate([p[0] for p in parts], axis=-1),
            jnp.concatenate([p[1] for p in parts], axis=-1))


def _lru_seq_kernel(xb_ref, gb_ref, cw_ref, cb_ref, wa_ref, ba_ref, wi_ref, bi_ref, lam_ref,
                    gm_ref, cast_ref, o_ref, hl_ref, castb_ref, xe_ref, a_ref, b_ref, hc_ref,
                    *, tt):
    castb_ref[...] = cast_ref[...].astype(BF16)
    halo = SUBLANES
    nseq, _, d = xb_ref.shape
    blk = d // LRU_HEADS

    @pl.when(pl.program_id(0) == 0)
    def _():
        xe_ref[:, 0:halo, :] = jnp.zeros((nseq, halo, d), F32)
        hc_ref[...] = jnp.zeros_like(hc_ref)

    xe_ref[:, halo:halo + tt, :] = xb_ref[...]
    for h in range(LRU_HEADS):
        cols = slice(h * blk, (h + 1) * blk)
        cw = cw_ref[:, cols]
        xc = cb_ref[:, cols] + xe_ref[:, halo:halo + tt, cols] * cw[CONV_W - 1:CONV_W, :]
        for k in range(1, CONV_W):
            xc = xc + xe_ref[:, halo - k:halo - k + tt, cols] * cw[CONV_W - 1 - k:CONV_W - k, :]
        a, b = _lru_gate_block(xc.reshape(nseq * tt, blk), wa_ref[h], ba_ref[:, cols],
                               wi_ref[h], bi_ref[:, cols], lam_ref[:, cols])
        a_ref[:, :, cols] = a.reshape(nseq, tt, blk)
        b_ref[:, :, cols] = b.reshape(nseq, tt, blk)
    xe_ref[:, 0:halo, :] = xb_ref[:, tt - halo:tt, :]

    def block(i, h):
        base = pl.multiple_of(i * SUBLANES, SUBLANES)
        for j in range(SUBLANES):
            row = pl.ds(base + j, 1)
            h = a_ref[:, row, :] * h + b_ref[:, row, :]
            b_ref[:, row, :] = h
        return h

    h = lax.fori_loop(0, tt // SUBLANES, block, hc_ref[...])
    hc_ref[...] = h
    hl_ref[...] = h

    out = b_ref[...] * jax.nn.gelu(gb_ref[...])
    o_ref[...] = _rms(out, gm_ref[...]).astype(BF16)


def _lru_seq(z, conv_w, conv_b, w_a, b_a, w_i, b_i, lam, g_merge, cast_w, tt):
    nseq, t_len, _ = z.shape
    d = conv_w.shape[1]
    steps = t_len // tt
    cast_blk = pl.BlockSpec((cast_w.shape[0] // steps, cast_w.shape[1]), lambda t: (t, 0))
    row = lambda v: v.reshape(1, d)
    const2 = lambda t: (0, 0)
    const3 = lambda t: (0, 0, 0)
    kern = functools.partial(_lru_seq_kernel, tt=tt)
    return pl.pallas_call(
        kern,
        out_shape=(jax.ShapeDtypeStruct((nseq, t_len, d), BF16),
                   jax.ShapeDtypeStruct((nseq, 1, d), F32),
                   jax.ShapeDtypeStruct(cast_w.shape, BF16)),
        grid=(steps,),
        in_specs=[
            pl.BlockSpec((nseq, tt, d), lambda t: (0, t, 1)),
            pl.BlockSpec((nseq, tt, d), lambda t: (0, t, 2)),
            pl.BlockSpec((CONV_W, d), const2),
            pl.BlockSpec((1, d), const2),
            pl.BlockSpec(w_a.shape, const3),
            pl.BlockSpec((1, d), const2),
            pl.BlockSpec(w_i.shape, const3),
            pl.BlockSpec((1, d), const2),
            pl.BlockSpec((1, d), const2),
            pl.BlockSpec((1, d), const2),
            cast_blk,
        ],
        out_specs=(pl.BlockSpec((nseq, tt, d), lambda t: (0, t, 0)),
                   pl.BlockSpec((nseq, 1, d), const3),
                   cast_blk),
        scratch_shapes=[pltpu.VMEM((nseq, tt + SUBLANES, d), F32),
                        pltpu.VMEM((nseq, tt, d), F32),
                        pltpu.VMEM((nseq, tt, d), F32),
                        pltpu.VMEM((nseq, 1, d), F32)],
        compiler_params=_params("arbitrary"),
        name="rglru_seq",
    )(z, z, conv_w, row(conv_b), w_a, row(b_a), w_i, row(b_i), row(lam), row(g_merge), cast_w)


def _lru_step_kernel(xb_ref, gb_ref, c0_ref, c1_ref, c2_ref, h0_ref, cw_ref, cb_ref,
                     wa_ref, ba_ref, wi_ref, bi_ref, lam_ref, gm_ref, o_ref, h_ref, cv_ref):
    d = xb_ref.shape[1]
    xb = xb_ref[...]
    cw = cw_ref[...]
    xc = (cb_ref[...] + c0_ref[...] * cw[0:1, :] + c1_ref[...] * cw[1:2, :]
          + c2_ref[...] * cw[2:3, :] + xb * cw[3:4, :])
    a, b = _lru_gates(xc, wa_ref, ba_ref[...], wi_ref, bi_ref[...], lam_ref[...])
    h = a * h0_ref[...] + b
    h_ref[...] = h
    out = h * jax.nn.gelu(gb_ref[...])
    o_ref[...] = _rms(out, gm_ref[...]).astype(BF16)
    cv_ref[:, 0:d] = c1_ref[...]
    cv_ref[:, d:2 * d] = c2_ref[...]
    cv_ref[:, 2 * d:3 * d] = xb


def _lru_step(z, conv0, h0, conv_w, conv_b, w_a, b_a, w_i, b_i, lam, g_merge):
    n = z.shape[0]
    d = conv_w.shape[1]
    taps = CONV_W - 1
    conv2d = conv0.reshape(n, taps * d)
    row = lambda v: v.reshape(1, d)
    full = lambda shape: pl.BlockSpec(shape, lambda i: (0,) * len(shape))
    col = lambda c: pl.BlockSpec((n, d), lambda i: (0, c))
    nd = full((n, d))
    rd = full((1, d))
    o, h, cv = pl.pallas_call(
        _lru_step_kernel,
        out_shape=(jax.ShapeDtypeStruct((n, d), BF16), jax.ShapeDtypeStruct((n, d), F32),
                   jax.ShapeDtypeStruct((n, taps * d), F32)),
        grid=(1,),
        in_specs=[
            col(1), col(2), col(0), col(1), col(2), nd,
            full((CONV_W, d)), rd,
            full(w_a.shape), rd, full(w_i.shape), rd, rd, rd,
        ],
        out_specs=(nd, nd, full((n, taps * d))),
        compiler_params=_params("arbitrary"),
        name="rglru_step",
    )(z, z, conv2d, conv2d, conv2d, h0, conv_w, row(conv_b),
      w_a, row(b_a), w_i, row(b_i), row(lam), row(g_merge))
    return o, h, cv.reshape(n, taps, d)


def _glu_rows(y, w_ref, b_ref, g_ref):
    gate = jax.nn.sigmoid(_dot(y.astype(BF16), w_ref[...]) + b_ref[...])
    return _rms(y.astype(F32) * gate, g_ref[...]).astype(BF16)


def _glu_kernel(y_ref, ys_ref, w_ref, b_ref, g_ref, o_ref, os_ref):
    for rows in _row_parts(y_ref.shape[0], 4):
        o_ref[rows, :] = _glu_rows(y_ref[rows, :], w_ref, b_ref, g_ref)

    @_on_last_row_tile
    def _():
        os_ref[...] = _glu_rows(ys_ref[...], w_ref, b_ref, g_ref)


def _glu(y, ys, w, b, g, tm):
    m, d = y.shape
    ms = ys.shape[0]
    const = lambda i: (0, 0)
    return pl.pallas_call(
        _glu_kernel,
        out_shape=(jax.ShapeDtypeStruct((m, d), BF16), jax.ShapeDtypeStruct((ms, d), BF16)),
        grid=(m // tm,),
        in_specs=[
            pl.BlockSpec((tm, d), lambda i: (i, 0)),
            pl.BlockSpec((ms, d), const),
            pl.BlockSpec((d, d), const),
            pl.BlockSpec((1, d), const),
            pl.BlockSpec((1, d), const),
        ],
        out_specs=(pl.BlockSpec((tm, d), lambda i: (i, 0)), pl.BlockSpec((ms, d), const)),
        compiler_params=_params("arbitrary"),
        name="s5_glu",
    )(y, ys, w, b.reshape(1, d), g.reshape(1, d))


def _out_proj_kernel(x_ref, ma_ref, mb_ref, xs_ref, mas_ref, mbs_ref, w_ref, o_ref, os_ref, wb_ref):
    @pl.when(pl.program_id(0) == 0)
    def _():
        wb_ref[...] = w_ref[...].astype(BF16)

    ka = ma_ref.shape[1]
    proj = lambda x, ma, mb: x + _dot(ma, wb_ref[:ka, :]) + _dot(mb, wb_ref[ka:, :])
    for rows in _row_parts(x_ref.shape[0], 2):
        o_ref[rows, :] = proj(x_ref[rows, :], ma_ref[rows, :], mb_ref[rows, :])

    @_on_last_row_tile
    def _():
        os_ref[...] = proj(xs_ref[...], mas_ref[...], mbs_ref[...])


def _out_proj(x, ma, mb, xs, mas, mbs, w, tm):
    m, n = x.shape
    ms = xs.shape[0]
    ka = ma.shape[1]
    kb = mb.shape[1]
    const = lambda i: (0, 0)
    tile = lambda i: (i, 0)
    return pl.pallas_call(
        _out_proj_kernel,
        out_shape=(jax.ShapeDtypeStruct((m, n), F32), jax.ShapeDtypeStruct((ms, n), F32)),
        grid=(m // tm,),
        in_specs=[
            pl.BlockSpec((tm, n), tile),
            pl.BlockSpec((tm, ka), tile),
            pl.BlockSpec((tm, kb), tile),
            pl.BlockSpec((ms, n), const),
            pl.BlockSpec((ms, ka), const),
            pl.BlockSpec((ms, kb), const),
            pl.BlockSpec((ka + kb, n), const, pipeline_mode=pl.Buffered(1)),
        ],
        out_specs=(pl.BlockSpec((tm, n), tile), pl.BlockSpec((ms, n), const)),
        scratch_shapes=[pltpu.VMEM((ka + kb, n), BF16)],
        compiler_params=_params("arbitrary"),
        name="out_proj",
    )(x, ma, mb, xs, mas, mbs, w)


def _mlp_kernel(x_ref, xs_ref, g_ref, wu_ref, wd_ref, o_ref, os_ref, h_ref, hs_ref):
    def ffn(h):
        act = jnp.square(jnp.maximum(_dot(h, wu_ref[...]), 0.0)).astype(BF16)
        return _dot(act, wd_ref[...])

    def rows(x_ref, o_ref, h_ref):
        first = pl.program_id(1) == 0

        @pl.when(first)
        def _():
            for part in _row_parts(x_ref.shape[0], 4):
                x = x_ref[part, :]
                h = _rms(x, g_ref[...]).astype(BF16)
                h_ref[part, :] = h
                o_ref[part, :] = x + ffn(h)

        @pl.when(jnp.logical_not(first))
        def _():
            o_ref[...] += ffn(h_ref[...])

    rows(x_ref, o_ref, h_ref)
    _on_last_row_tile(lambda: rows(xs_ref, os_ref, hs_ref))


def _mlp(x, xs, g, w_up, w_down, tm, tf):
    m, d = x.shape
    ms = xs.shape[0]
    f = w_up.shape[1]
    const = lambda i, j: (0, 0)
    return pl.pallas_call(
        _mlp_kernel,
        out_shape=(jax.ShapeDtypeStruct((m, d), F32), jax.ShapeDtypeStruct((ms, d), F32)),
        grid=(m // tm, f // tf),
        in_specs=[
            pl.BlockSpec((tm, d), lambda i, j: (i, 0)),
            pl.BlockSpec((ms, d), const),
            pl.BlockSpec((1, d), const),
            pl.BlockSpec((d, tf), lambda i, j: (0, j)),
            pl.BlockSpec((tf, d), lambda i, j: (j, 0)),
        ],
        out_specs=(pl.BlockSpec((tm, d), lambda i, j: (i, 0)), pl.BlockSpec((ms, d), const)),
        scratch_shapes=[pltpu.VMEM((tm, d), BF16), pltpu.VMEM((ms, d), BF16)],
        compiler_params=_params("arbitrary", "arbitrary"),
        name="mlp",
    )(x, xs, g.reshape(1, d), w_up, w_down)


def _ple_kernel(x_ref, p_ref, xs_ref, ps_ref, g_ref, wg_ref, wp_ref, gf_ref, o_ref, os_ref,
                wgb_ref, wpb_ref, *, final):
    @pl.when(pl.program_id(0) == 0)
    def _():
        wgb_ref[...] = wg_ref[...].astype(BF16)
        wpb_ref[...] = wp_ref[...].astype(BF16)

    def ple(x, p):
        gate = jax.nn.sigmoid(_dot(_rms(x, g_ref[...]).astype(BF16), wgb_ref[...]))
        pe = _dot(p.astype(BF16), wpb_ref[...])
        x = x + pe * gate
        return _rms(x, gf_ref[...]) if final else x

    for rows in _row_parts(x_ref.shape[0], 2):
        o_ref[rows, :] = ple(x_ref[rows, :], p_ref[rows, :])

    @_on_last_row_tile
    def _():
        os_ref[...] = ple(xs_ref[...], ps_ref[...])


def _ple(x, p, xs, ps, g, w_gate, w_ple, g_final, tm, final):
    m, d = x.shape
    ms = xs.shape[0]
    dp = p.shape[1]
    once = pl.Buffered(1)
    const = lambda i: (0, 0)
    tile = lambda i: (i, 0)
    return pl.pallas_call(
        functools.partial(_ple_kernel, final=final),
        out_shape=(jax.ShapeDtypeStruct((m, d), F32), jax.ShapeDtypeStruct((ms, d), F32)),
        grid=(m // tm,),
        in_specs=[
            pl.BlockSpec((tm, d), tile),
            pl.BlockSpec((tm, dp), tile),
            pl.BlockSpec((ms, d), const),
            pl.BlockSpec((ms, dp), const),
            pl.BlockSpec((1, d), const),
            pl.BlockSpec((d, d), const, pipeline_mode=once),
            pl.BlockSpec((dp, d), const, pipeline_mode=once),
            pl.BlockSpec((1, d), const),
        ],
        out_specs=(pl.BlockSpec((tm, d), tile), pl.BlockSpec((ms, d), const)),
        scratch_shapes=[pltpu.VMEM((d, d), BF16), pltpu.VMEM((dp, d), BF16)],
        compiler_params=_params("arbitrary"),
        name="ple_final",
    )(x, p, xs, ps, g.reshape(1, d), w_gate, w_ple, g_final.reshape(1, d))


TT_LRU = 64
TM_IN, TN_IN = 1024, 1024
TM_GLU = 1024
TM_OUT = 512
TM_MLP, TF_MLP = 512, 1024
TM_PLE = 512


def kernel(x_prompt, x_sample, state_s5_re, state_s5_im, state_lru, state_conv, p_prompt, p_sample,
           g_mix, w_in, s5_lam_re, s5_lam_im, s5_log_step, s5_b_re, s5_b_im, s5_c_re, s5_c_im, s5_d,
           s5_w_glu, s5_b_glu, conv_w, conv_b, lru_w_a, lru_b_a, lru_w_i, lru_b_i, lru_lam,
           g_merge_a, g_merge_b, w_out, g_mlp, w_up, w_down, g_ple, w_ple_gate, w_ple, g_final):
    depth = g_mix.shape[0]
    nb, t_len, d_model = x_prompt.shape
    ns = x_sample.shape[0]
    d_s5 = s5_d.shape[1]
    d_lru = conv_w.shape[2]
    groups = d_s5 // S5_CH

    xp = x_prompt.reshape(nb * t_len, d_model)
    xs = x_sample.reshape(ns, d_model)
    outs = [[] for _ in range(8)]
    for l in range(depth):
        final = l == depth - 1
        w_glu_b = s5_w_glu[l].astype(BF16)
        lru_args = (conv_w[l], conv_b[l], lru_w_a[l].astype(BF16), lru_b_a[l],
                    lru_w_i[l].astype(BF16), lru_b_i[l], lru_lam[l], g_merge_b[l])
        w_t, w_s, w_c, tab, wb, wct, arow, w_in_b = _s5_params(
            s5_lam_re[l], s5_lam_im[l], s5_log_step[l], s5_b_re[l], s5_b_im[l],
            s5_c_re[l], s5_c_im[l], s5_d[l], w_in[l])

        z, zs = _norm_matmul(xp, xs, g_mix[l], w_in_b, TM_IN, TN_IN)
        z4 = z.reshape(nb, t_len, 3 * d_s5)
        y, hf, w_down_b = _s5_seq(z4, w_t, w_s, w_c, tab, d_s5, w_down[l])
        mb, lru_h, w_up_b = _lru_seq(z4, *lru_args, w_up[l], TT_LRU)
        ys, hsr, hsi = _s5_step(zs, state_s5_re[l].reshape(ns, groups * S5_STATE),
                                state_s5_im[l].reshape(ns, groups * S5_STATE),
                                wb, wct, arow, s5_d[l].reshape(1, d_s5))
        mbs, lru_hs, conv_s = _lru_step(zs, state_conv[l], state_lru[l], *lru_args)
        ma, mas = _glu(y.reshape(nb * t_len, d_s5), ys, w_glu_b, s5_b_glu[l], g_merge_a[l], TM_GLU)
        xp, xs = _out_proj(xp, ma, mb.reshape(nb * t_len, d_lru), xs, mas, mbs, w_out[l], TM_OUT)
        xp, xs = _mlp(xp, xs, g_mlp[l], w_up_b, w_down_b, TM_MLP, TF_MLP)
        xp, xs = _ple(xp, p_prompt[l].reshape(nb * t_len, -1), xs, p_sample[l].reshape(ns, -1),
                      g_ple[l], w_ple_gate[l], w_ple[l], g_final, TM_PLE, final)
        hf = jnp.transpose(hf[:, :, :, :nb], (1, 3, 0, 2))
        outs[0].append(hf[0])
        outs[1].append(hf[1])
        outs[2].append(lru_h.reshape(nb, d_lru))
        outs[3].append(z4[:, t_len - (CONV_W - 1):, d_s5:d_s5 + d_lru])
        outs[4].append(hsr.reshape(ns, groups, S5_STATE))
        outs[5].append(hsi.reshape(ns, groups, S5_STATE))
        outs[6].append(lru_hs)
        outs[7].append(conv_s)
    return (xp.reshape(nb, t_len, d_model), xs.reshape(ns, 1, d_model),
            *(jnp.stack(o) for o in outs))
```

```python
import functools

import jax
import jax.numpy as jnp
from jax import lax
from jax.experimental import pallas as pl
from jax.experimental.pallas import tpu as pltpu

F32 = jnp.float32
BF16 = jnp.bfloat16
HIGHEST = lax.Precision.HIGHEST

EPS = 1e-6
LRU_C = 8.0
S5_CH = 16
S5_STATE = 64
LRU_HEADS = 4
CONV_W = 4
CHUNK = 16
SUBLANES = 8
LANES = 128
LANE_GROUPS = LANES // S5_CH
LANE_PAIRS = LANE_GROUPS // 2
ROWS = CHUNK * S5_CH
POW_ROWS = 32
SCAN_ROW0 = POW_ROWS
VMEM_LIMIT = 56 * 1024 * 1024


def _params(*sem):
    return pltpu.CompilerParams(dimension_semantics=sem, vmem_limit_bytes=VMEM_LIMIT)


def _rms(x, g):
    return x * lax.rsqrt(jnp.mean(x * x, axis=-1, keepdims=True) + EPS) * g


def _dot(a, b):
    return jnp.dot(a, b, preferred_element_type=F32)


def _dot_nt(a, b, precision=None):
    return lax.dot_general(a, b, (((1,), (1,)), ((), ())), precision=precision,
                           preferred_element_type=F32)


def _transpose_tiles(x):
    r, c = x.shape
    return jnp.concatenate(
        [jnp.concatenate([x[i:i + LANES, j:j + LANES].T for i in range(0, r, LANES)], axis=1)
         for j in range(0, c, LANES)], axis=0)


def _row_parts(tm, want):
    parts = max(1, min(want, tm // LANES))
    step = tm // parts
    return [slice(r * step, (r + 1) * step) for r in range(parts)]


def _on_last_row_tile(fn):
    pl.when(pl.program_id(0) == pl.num_programs(0) - 1)(fn)


def _norm_matmul_kernel(x_ref, xs_ref, g_ref, w_ref, o_ref, os_ref):
    tn = o_ref.shape[1]
    w = w_ref[:, pl.ds(pl.multiple_of(pl.program_id(1) * tn, tn), tn)]
    for rows in _row_parts(x_ref.shape[0], 8):
        h = _rms(x_ref[rows, :], g_ref[...]).astype(BF16)
        o_ref[rows, :] = _dot(h, w)

    @_on_last_row_tile
    def _():
        os_ref[...] = _dot(_rms(xs_ref[...], g_ref[...]).astype(BF16), w)


def _norm_matmul(x, xs, g, w, tm, tn):
    m, k = x.shape
    ms = xs.shape[0]
    n = w.shape[1]
    ni = m // tm
    return pl.pallas_call(
        _norm_matmul_kernel,
        out_shape=(jax.ShapeDtypeStruct((m, n), F32), jax.ShapeDtypeStruct((ms, n), F32)),
        grid=(ni, n // tn),
        in_specs=[
            pl.BlockSpec((tm, k), lambda i, j: (i, 0)),
            pl.BlockSpec((ms, k), lambda i, j: (0, 0)),
            pl.BlockSpec((1, k), lambda i, j: (0, 0)),
            pl.BlockSpec((k, n), lambda i, j: (0, 0), pipeline_mode=pl.Buffered(1)),
        ],
        out_specs=(pl.BlockSpec((tm, tn), lambda i, j: (i, j)),
                   pl.BlockSpec((ms, tn), lambda i, j: (0, jnp.where(i == ni - 1, j, 0)))),
        compiler_params=_params("arbitrary", "arbitrary"),
        name="norm_w_in",
    )(x, xs, g.reshape(1, k), w)


def _s5_param_kernel(lam_ref, bc_ref, d_ref, cast_ref,
                     wt_ref, ws_ref, wc_ref, tab_ref, wb_ref, wct_ref, arow_ref, castb_ref):
    castb_ref[...] = cast_ref[...].astype(BF16)
    row_p = lax.broadcasted_iota(jnp.int32, (POW_ROWS, LANES), 0)
    lane = lax.broadcasted_iota(jnp.int32, (ROWS, LANES), 1)
    lane16 = lax.broadcasted_iota(jnp.int32, (S5_CH, LANES), 1)
    row16 = lax.broadcasted_iota(jnp.int32, (S5_CH, LANES), 0)
    lo = lane < S5_STATE

    wb_ref[...] = jnp.zeros_like(wb_ref)
    wct_ref[...] = jnp.zeros_like(wct_ref)

    for q in range(LANE_PAIRS):
        lr, li = lam_ref[0, q], lam_ref[1, q]
        step = jnp.exp(lam_ref[2, q])
        mag = jnp.exp(lr * step)
        ar = mag * jnp.cos(li * step)
        ai = mag * jnp.sin(li * step)
        nr, ni = ar - 1.0, ai
        den = lr * lr + li * li
        cr = (nr * lr + ni * li) / den
        ci = (ni * lr - nr * li) / den
        bt_r, bt_i = bc_ref[0, q], bc_ref[1, q]
        bb_r = cr * bt_r - ci * bt_i
        bb_i = cr * bt_i + ci * bt_r
        c_r, c_i = bc_ref[2, q], bc_ref[3, q]

        pr = jnp.ones((POW_ROWS, LANES), F32)
        pi = jnp.zeros((POW_ROWS, LANES), F32)
        sr, si = ar, ai
        sq = []
        for m in range(POW_ROWS.bit_length() - 1 + SUBLANES):
            sq.append((sr, si))
            if (1 << m) < POW_ROWS:
                bit = ((row_p >> m) & 1) == 1
                pr, pi = (jnp.where(bit, pr * sr - pi * si, pr),
                          jnp.where(bit, pr * si + pi * sr, pi))
            sr, si = sr * sr - si * si, 2.0 * sr * si
        m0 = CHUNK.bit_length() - 1
        scan_r = jnp.concatenate([sq[m0 + j][0] for j in range(SUBLANES)], axis=0)
        scan_i = jnp.concatenate([sq[m0 + j][1] for j in range(SUBLANES)], axis=0)
        pad = jnp.zeros((LANES - POW_ROWS - SUBLANES, LANES), F32)
        tab_ref[q, 0] = jnp.concatenate([pr, scan_r, pad], axis=0).T
        tab_ref[q, 1] = jnp.concatenate([pi, scan_i, pad], axis=0).T
        arow_ref[q, 0] = pr[:SUBLANES]
        arow_ref[q, 1] = pi[:SUBLANES]

        def expand(tab, k0, sign):
            return jnp.concatenate(
                [jnp.broadcast_to(tab[k0 + sign * s:k0 + sign * s + 1, :], (S5_CH, LANES))
                 for s in range(CHUNK)], axis=0)

        tile = lambda v: jnp.concatenate([v] * CHUNK, axis=0)

        e_r, e_i = expand(pr, CHUNK - 1, -1), expand(pi, CHUNK - 1, -1)
        t_r, t_i = tile(bb_r), tile(bb_i)
        ws_r = _transpose_tiles(t_r * e_r - t_i * e_i)
        ws_i = _transpose_tiles(t_r * e_i + t_i * e_r)
        for h in range(2):
            rows = slice(h * S5_STATE, (h + 1) * S5_STATE)
            ws_ref[2 * q + h] = jnp.concatenate([ws_r[rows], ws_i[rows]], axis=0).astype(BF16)

        e_r, e_i = expand(pr, 1, 1), expand(pi, 1, 1)
        t_r, t_i = tile(c_r), tile(c_i)
        ca_r = t_r * e_r - t_i * e_i
        ca_n = -(t_r * e_i + t_i * e_r)
        wc_ref[2 * q] = jnp.where(lo, ca_r, pltpu.roll(ca_n, S5_STATE, 1)).astype(BF16)
        wc_ref[2 * q + 1] = jnp.where(lo, pltpu.roll(ca_r, S5_STATE, 1), ca_n).astype(BF16)

        e_r, e_i = expand(pr, 0, 1), expand(pi, 0, 1)
        c0_r = t_r * e_r - t_i * e_i
        c0_i = t_r * e_i + t_i * e_r

        for h in range(2):
            g = 2 * q + h
            mine = (lane16 < S5_STATE) == (h == 0)
            bm_r = jnp.where(mine, bb_r, 0.0)
            bm_i = jnp.where(mine, bb_i, 0.0)
            kt = _dot_nt(bm_r, c0_r, HIGHEST) - _dot_nt(bm_i, c0_i, HIGHEST)
            k_lo = kt[:, :LANES] + jnp.where(row16 == lane16, d_ref[g], 0.0)
            k_hi = kt[:, LANES:]
            blocks = []
            half = LANES // S5_CH
            for s in range(CHUNK):
                sh = (s % half) * S5_CH
                keep = lane16 >= sh
                r_lo = pltpu.roll(k_lo, sh, 1) if sh else k_lo
                r_hi = pltpu.roll(k_hi, sh, 1) if sh else k_hi
                if s < half:
                    blk = jnp.concatenate([jnp.where(keep, r_lo, 0.0),
                                           jnp.where(keep, r_hi, r_lo)], axis=1)
                else:
                    blk = jnp.concatenate([jnp.zeros_like(r_lo), jnp.where(keep, r_lo, 0.0)], axis=1)
                blocks.append(blk)
            wt_ref[g] = _transpose_tiles(jnp.concatenate(blocks, axis=0)).astype(BF16)

            r0 = (2 * q + h) * S5_CH
            c0 = q * 2 * LANES
            wb_ref[r0:r0 + S5_CH, c0:c0 + LANES] = bm_r.astype(BF16)
            wb_ref[r0:r0 + S5_CH, c0 + LANES:c0 + 2 * LANES] = bm_i.astype(BF16)
            wct_ref[q, r0:r0 + S5_CH, :LANES] = jnp.where(mine, c_r, 0.0).astype(BF16)
            wct_ref[q, r0:r0 + S5_CH, LANES:] = jnp.where(mine, -c_i, 0.0).astype(BF16)


def _s5_params(lam_re, lam_im, log_step, b_re, b_im, c_re, c_im, d, cast_w):
    groups = lam_re.shape[0]
    pairs = groups // 2
    slabs = groups // LANE_GROUPS
    cast_blk = pl.BlockSpec((cast_w.shape[0] // slabs, cast_w.shape[1]), lambda i: (i, 0))
    ls = jnp.broadcast_to(log_step[:, None], (groups, S5_STATE))
    lam = jnp.stack([lam_re, lam_im, ls]).reshape(3, pairs, 1, LANES)
    bc = jnp.stack([jnp.transpose(b_re, (0, 2, 1)), jnp.transpose(b_im, (0, 2, 1)), c_re, c_im])
    bc = jnp.transpose(bc.reshape(4, pairs, 2, S5_CH, S5_STATE), (0, 1, 3, 2, 4)).reshape(
        4, pairs, S5_CH, LANES)
    d_pad = jnp.pad(d.reshape(groups, 1, S5_CH), ((0, 0), (0, 0), (0, LANES - S5_CH)))
    blk3 = lambda i: (i, 0, 0)
    blk4 = lambda i: (i, 0, 0, 0)
    pblk = lambda i: (0, i, 0, 0)
    return pl.pallas_call(
        _s5_param_kernel,
        out_shape=(jax.ShapeDtypeStruct((groups, ROWS, ROWS), BF16),
                   jax.ShapeDtypeStruct((groups, 2 * S5_STATE, ROWS), BF16),
                   jax.ShapeDtypeStruct((groups, ROWS, 2 * S5_STATE), BF16),
                   jax.ShapeDtypeStruct((pairs, 2, LANES, LANES), F32),
                   jax.ShapeDtypeStruct((slabs, LANES, LANE_PAIRS * 2 * LANES), BF16),
                   jax.ShapeDtypeStruct((slabs, LANE_PAIRS, LANES, 2 * LANES), BF16),
                   jax.ShapeDtypeStruct((pairs, 2, SUBLANES, LANES), F32),
                   jax.ShapeDtypeStruct(cast_w.shape, BF16)),
        grid=(slabs,),
        in_specs=[pl.BlockSpec((3, LANE_PAIRS, 1, LANES), pblk),
                  pl.BlockSpec((4, LANE_PAIRS, S5_CH, LANES), pblk),
                  pl.BlockSpec((LANE_GROUPS, 1, LANES), blk3), cast_blk],
        out_specs=(pl.BlockSpec((LANE_GROUPS, ROWS, ROWS), blk3),
                   pl.BlockSpec((LANE_GROUPS, 2 * S5_STATE, ROWS), blk3),
                   pl.BlockSpec((LANE_GROUPS, ROWS, 2 * S5_STATE), blk3),
                   pl.BlockSpec((LANE_PAIRS, 2, LANES, LANES), blk4),
                   pl.BlockSpec((None, LANES, LANE_PAIRS * 2 * LANES), blk3),
                   pl.BlockSpec((None, LANE_PAIRS, LANES, 2 * LANES), blk4),
                   pl.BlockSpec((LANE_PAIRS, 2, SUBLANES, LANES), blk4),
                   cast_blk),
        compiler_params=_params("parallel"),
        name="s5_params",
    )(lam, bc, d_pad, cast_w)


def _s5_seq_kernel(z_ref, wt_ref, ws_ref, wc_ref, tab_ref, cast_ref, y_ref, hf_ref, castb_ref,
                   ut_ref, yt_ref, sr_ref, si_ref, *, nb, n_chunks):
    castb_ref[...] = cast_ref[...].astype(BF16)
    scan_steps = n_chunks.bit_length() - 1
    for n in range(nb):
        for s in range(CHUNK):
            xs = z_ref[n, pl.ds(s, n_chunks, stride=CHUNK), :]
            ut_ref[:, s * S5_CH:(s + 1) * S5_CH, n * n_chunks:(n + 1) * n_chunks] = (
                xs.T.astype(BF16).reshape(LANE_GROUPS, S5_CH, n_chunks))

    for g in range(LANE_GROUPS):
        st = _dot(ws_ref[g], ut_ref[g])
        sr_ref[g * S5_STATE:(g + 1) * S5_STATE, :] = st[:S5_STATE]
        si_ref[g * S5_STATE:(g + 1) * S5_STATE, :] = st[S5_STATE:]

    srows = LANE_GROUPS * S5_STATE
    lane = lax.broadcasted_iota(jnp.int32, (srows, n_chunks), 1)
    tab_r = jnp.concatenate([tab_ref[q, 0] for q in range(LANE_PAIRS)], axis=0)
    tab_i = jnp.concatenate([tab_ref[q, 1] for q in range(LANE_PAIRS)], axis=0)
    fin_r = jnp.zeros((srows, n_chunks), F32)
    fin_i = jnp.zeros((srows, n_chunks), F32)
    for n in range(nb):
        cols = slice(n * n_chunks, (n + 1) * n_chunks)
        xr = sr_ref[:, cols]
        xi = si_ref[:, cols]
        for k in range(scan_steps):
            d = 1 << k
            keep = lane >= d
            sr = jnp.where(keep, pltpu.roll(xr, d, 1), 0.0)
            si = jnp.where(keep, pltpu.roll(xi, d, 1), 0.0)
            pr = tab_r[:, SCAN_ROW0 + k:SCAN_ROW0 + k + 1]
            pi = tab_i[:, SCAN_ROW0 + k:SCAN_ROW0 + k + 1]
            xr, xi = xr + (pr * sr - pi * si), xi + (pr * si + pi * sr)
        last = lane == n_chunks - 1
        fin_r = fin_r + jnp.where(
            lane == n, jnp.sum(jnp.where(last, xr, 0.0), axis=1, keepdims=True), 0.0)
        fin_i = fin_i + jnp.where(
            lane == n, jnp.sum(jnp.where(last, xi, 0.0), axis=1, keepdims=True), 0.0)
        keep = lane >= 1
        sr_ref[:, cols] = jnp.where(keep, pltpu.roll(xr, 1, 1), 0.0)
        si_ref[:, cols] = jnp.where(keep, pltpu.roll(xi, 1, 1), 0.0)

    for g in range(LANE_GROUPS):
        rows = slice(g * S5_STATE, (g + 1) * S5_STATE)
        hf_ref[g, 0] = fin_r[rows]
        hf_ref[g, 1] = fin_i[rows]
        hs = jnp.concatenate([sr_ref[rows, :], si_ref[rows, :]], axis=0).astype(BF16)
        yt_ref[g] = jax.nn.gelu(_dot(wc_ref[g], hs) + _dot(wt_ref[g], ut_ref[g]))

    for n in range(nb):
        for s in range(CHUNK):
            blk = yt_ref[:, s * S5_CH:(s + 1) * S5_CH, n * n_chunks:(n + 1) * n_chunks]
            y_ref[n, pl.ds(s, n_chunks, stride=CHUNK), :] = blk.reshape(LANES, n_chunks).T


def _s5_seq(z, w_t, w_s, w_c, tab, d_s5, cast_w):
    nb, t_len, _ = z.shape
    n_chunks = t_len // CHUNK
    assert n_chunks == LANES, "the chunk axis must fill one 128-lane tile"
    groups = d_s5 // S5_CH
    steps = groups // LANE_GROUPS
    cast_blk = pl.BlockSpec((cast_w.shape[0] // steps, cast_w.shape[1]), lambda i: (i, 0))
    kern = functools.partial(_s5_seq_kernel, nb=nb, n_chunks=n_chunks)
    blk3 = lambda i: (i, 0, 0)
    blk4 = lambda i: (i, 0, 0, 0)
    return pl.pallas_call(
        kern,
        out_shape=(jax.ShapeDtypeStruct((nb, t_len, d_s5), F32),
                   jax.ShapeDtypeStruct((groups, 2, S5_STATE, LANES), F32),
                   jax.ShapeDtypeStruct(cast_w.shape, BF16)),
        grid=(steps,),
        in_specs=[
            pl.BlockSpec((nb, t_len, LANES), lambda i: (0, 0, i)),
            pl.BlockSpec((LANE_GROUPS, ROWS, ROWS), blk3),
            pl.BlockSpec((LANE_GROUPS, 2 * S5_STATE, ROWS), blk3),
            pl.BlockSpec((LANE_GROUPS, ROWS, 2 * S5_STATE), blk3),
            pl.BlockSpec((LANE_PAIRS, 2, LANES, LANES), blk4),
            cast_blk,
        ],
        out_specs=(pl.BlockSpec((nb, t_len, LANES), lambda i: (0, 0, i)),
                   pl.BlockSpec((LANE_GROUPS, 2, S5_STATE, LANES), blk4),
                   cast_blk),
        scratch_shapes=[pltpu.VMEM((LANE_GROUPS, ROWS, nb * n_chunks), BF16),
                        pltpu.VMEM((LANE_GROUPS, ROWS, nb * n_chunks), F32),
                        pltpu.VMEM((LANE_GROUPS * S5_STATE, nb * n_chunks), F32),
                        pltpu.VMEM((LANE_GROUPS * S5_STATE, nb * n_chunks), F32)],
        compiler_params=_params("parallel"),
        name="s5_seq",
    )(z, w_t, w_s, w_c, tab, cast_w)


def _s5_step_kernel(u_ref, h0r_ref, h0i_ref, wb_ref, wct_ref, arow_ref, d_ref,
                    y_ref, hr_ref, hi_ref):
    u = u_ref[...]
    bu = _dot(u.astype(BF16), wb_ref[...])
    y = d_ref[...] * u
    for q in range(LANE_PAIRS):
        cols = slice(q * LANES, (q + 1) * LANES)
        ar = arow_ref[q, 0, 1:2, :]
        ai = arow_ref[q, 1, 1:2, :]
        h0r, h0i = h0r_ref[:, cols], h0i_ref[:, cols]
        hr = ar * h0r - ai * h0i + bu[:, 2 * q * LANES:(2 * q + 1) * LANES]
        hi = ar * h0i + ai * h0r + bu[:, (2 * q + 1) * LANES:(2 * q + 2) * LANES]
        hr_ref[:, cols] = hr
        hi_ref[:, cols] = hi
        y = y + _dot_nt(jnp.concatenate([hr, hi], axis=1).astype(BF16), wct_ref[q])
    y_ref[...] = jax.nn.gelu(y).astype(BF16)


def _s5_step(z, h0r, h0i, wb, wct, arow, d):
    n = z.shape[0]
    slabs = wb.shape[0]
    sw = LANE_PAIRS * LANES
    col = lambda i: (0, i)
    blk3 = lambda i: (i, 0, 0)
    blk4 = lambda i: (i, 0, 0, 0)
    return pl.pallas_call(
        _s5_step_kernel,
        out_shape=(jax.ShapeDtypeStruct((n, slabs * LANES), BF16),
                   jax.ShapeDtypeStruct((n, slabs * sw), F32),
                   jax.ShapeDtypeStruct((n, slabs * sw), F32)),
        grid=(slabs,),
        in_specs=[
            pl.BlockSpec((n, LANES), col),
            pl.BlockSpec((n, sw), col),
            pl.BlockSpec((n, sw), col),
            pl.BlockSpec((None, LANES, 2 * sw), blk3),
            pl.BlockSpec((None, LANE_PAIRS, LANES, 2 * LANES), blk4),
            pl.BlockSpec((LANE_PAIRS, 2, SUBLANES, LANES), blk4),
            pl.BlockSpec((1, LANES), col),
        ],
        out_specs=(pl.BlockSpec((n, LANES), col),
                   pl.BlockSpec((n, sw), col),
                   pl.BlockSpec((n, sw), col)),
        compiler_params=_params("parallel"),
        name="s5_step",
    )(z, h0r, h0i, wb, wct, arow, d)


def _lru_gate_block(xc, wa, ba, wi, bi, lam):
    xb16 = xc.astype(BF16)
    r = jax.nn.sigmoid(_dot(xb16, wa) + ba)
    ig = jax.nn.sigmoid(_dot(xb16, wi) + bi)
    log_a = -LRU_C * r * jax.nn.softplus(-lam)
    a = jnp.exp(log_a)
    mult = jnp.sqrt(1.0 - a * a)
    return a, mult * (ig * xc)


def _lru_gates(xc, wa_ref, ba, wi_ref, bi, lam):
    blk = xc.shape[1] // LRU_HEADS
    parts = [_lru_gate_block(xc[:, h * blk:(h + 1) * blk], wa_ref[h], ba[:, h * blk:(h + 1) * blk],
                             wi_ref[h], bi[:, h * blk:(h + 1) * blk],
                             lam[:, h * blk:(h + 1) * blk]) for h in range(LRU_HEADS)]
    return (jnp.concatenate([p[0] for p in parts], axis=-1),
            jnp.concatenate([p[1] for p in parts], axis=-1))


def _lru_seq_kernel(xb_ref, gb_ref, cw_ref, cb_ref, wa_ref, ba_ref, wi_ref, bi_ref, lam_ref,
                    gm_ref, cast_ref, o_ref, hl_ref, castb_ref, xe_ref, a_ref, b_ref, hc_ref,
                    *, tt):
    castb_ref[...] = cast_ref[...].astype(BF16)
    halo = SUBLANES
    nseq, _, d = xb_ref.shape
    blk = d // LRU_HEADS

    @pl.when(pl.program_id(0) == 0)
    def _():
        xe_ref[:, 0:halo, :] = jnp.zeros((nseq, halo, d), F32)
        hc_ref[...] = jnp.zeros_like(hc_ref)

    xe_ref[:, halo:halo + tt, :] = xb_ref[...]
    for h in range(LRU_HEADS):
        cols = slice(h * blk, (h + 1) * blk)
        cw = cw_ref[:, cols]
        xc = cb_ref[:, cols] + xe_ref[:, halo:halo + tt, cols] * cw[CONV_W - 1:CONV_W, :]
        for k in range(1, CONV_W):
            xc = xc + xe_ref[:, halo - k:halo - k + tt, cols] * cw[CONV_W - 1 - k:CONV_W - k, :]
        a, b = _lru_gate_block(xc.reshape(nseq * tt, blk), wa_ref[h], ba_ref[:, cols],
                               wi_ref[h], bi_ref[:, cols], lam_ref[:, cols])
        a_ref[:, :, cols] = a.reshape(nseq, tt, blk)
        b_ref[:, :, cols] = b.reshape(nseq, tt, blk)
    xe_ref[:, 0:halo, :] = xb_ref[:, tt - halo:tt, :]

    def block(i, h):
        base = pl.multiple_of(i * SUBLANES, SUBLANES)
        for j in range(SUBLANES):
            row = pl.ds(base + j, 1)
            h = a_ref[:, row, :] * h + b_ref[:, row, :]
            b_ref[:, row, :] = h
        return h

    h = lax.fori_loop(0, tt // SUBLANES, block, hc_ref[...])
    hc_ref[...] = h
    hl_ref[...] = h

    out = b_ref[...] * jax.nn.gelu(gb_ref[...])
    o_ref[...] = _rms(out, gm_ref[...]).astype(BF16)


def _lru_seq(z, conv_w, conv_b, w_a, b_a, w_i, b_i, lam, g_merge, cast_w, tt):
    nseq, t_len, _ = z.shape
    d = conv_w.shape[1]
    steps = t_len // tt
    cast_blk = pl.BlockSpec((cast_w.shape[0] // steps, cast_w.shape[1]), lambda t: (t, 0))
    row = lambda v: v.reshape(1, d)
    const2 = lambda t: (0, 0)
    const3 = lambda t: (0, 0, 0)
    kern = functools.partial(_lru_seq_kernel, tt=tt)
    return pl.pallas_call(
        kern,
        out_shape=(jax.ShapeDtypeStruct((nseq, t_len, d), BF16),
                   jax.ShapeDtypeStruct((nseq, 1, d), F32),
                   jax.ShapeDtypeStruct(cast_w.shape, BF16)),
        grid=(steps,),
        in_specs=[
            pl.BlockSpec((nseq, tt, d), lambda t: (0, t, 1)),
            pl.BlockSpec((nseq, tt, d), lambda t: (0, t, 2)),
            pl.BlockSpec((CONV_W, d), const2),
            pl.BlockSpec((1, d), const2),
            pl.BlockSpec(w_a.shape, const3),
            pl.BlockSpec((1, d), const2),
            pl.BlockSpec(w_i.shape, const3),
            pl.BlockSpec((1, d), const2),
            pl.BlockSpec((1, d), const2),
            pl.BlockSpec((1, d), const2),
            cast_blk,
        ],
        out_specs=(pl.BlockSpec((nseq, tt, d), lambda t: (0, t, 0)),
                   pl.BlockSpec((nseq, 1, d), const3),
                   cast_blk),
        scratch_shapes=[pltpu.VMEM((nseq, tt + SUBLANES, d), F32),
                        pltpu.VMEM((nseq, tt, d), F32),
                        pltpu.VMEM((nseq, tt, d), F32),
                        pltpu.VMEM((nseq, 1, d), F32)],
        compiler_params=_params("arbitrary"),
        name="rglru_seq",
    )(z, z, conv_w, row(conv_b), w_a, row(b_a), w_i, row(b_i), row(lam), row(g_merge), cast_w)


def _lru_step_kernel(xb_ref, gb_ref, c0_ref, c1_ref, c2_ref, h0_ref, cw_ref, cb_ref,
                     wa_ref, ba_ref, wi_ref, bi_ref, lam_ref, gm_ref, o_ref, h_ref, cv_ref):
    d = xb_ref.shape[1]
    xb = xb_ref[...]
    cw = cw_ref[...]
    xc = (cb_ref[...] + c0_ref[...] * cw[0:1, :] + c1_ref[...] * cw[1:2, :]
          + c2_ref[...] * cw[2:3, :] + xb * cw[3:4, :])
    a, b = _lru_gates(xc, wa_ref, ba_ref[...], wi_ref, bi_ref[...], lam_ref[...])
    h = a * h0_ref[...] + b
    h_ref[...] = h
    out = h * jax.nn.gelu(gb_ref[...])
    o_ref[...] = _rms(out, gm_ref[...]).astype(BF16)
    cv_ref[:, 0:d] = c1_ref[...]
    cv_ref[:, d:2 * d] = c2_ref[...]
    cv_ref[:, 2 * d:3 * d] = xb


def _lru_step(z, conv0, h0, conv_w, conv_b, w_a, b_a, w_i, b_i, lam, g_merge):
    n = z.shape[0]
    d = conv_w.shape[1]
    taps = CONV_W - 1
    conv2d = conv0.reshape(n, taps * d)
    row = lambda v: v.reshape(1, d)
    full = lambda shape: pl.BlockSpec(shape, lambda i: (0,) * len(shape))
    col = lambda c: pl.BlockSpec((n, d), lambda i: (0, c))
    nd = full((n, d))
    rd = full((1, d))
    o, h, cv = pl.pallas_call(
        _lru_step_kernel,
        out_shape=(jax.ShapeDtypeStruct((n, d), BF16), jax.ShapeDtypeStruct((n, d), F32),
                   jax.ShapeDtypeStruct((n, taps * d), F32)),
        grid=(1,),
        in_specs=[
            col(1), col(2), col(0), col(1), col(2), nd,
            full((CONV_W, d)), rd,
            full(w_a.shape), rd, full(w_i.shape), rd, rd, rd,
        ],
        out_specs=(nd, nd, full((n, taps * d))),
        compiler_params=_params("arbitrary"),
        name="rglru_step",
    )(z, z, conv2d, conv2d, conv2d, h0, conv_w, row(conv_b),
      w_a, row(b_a), w_i, row(b_i), row(lam), row(g_merge))
    return o, h, cv.reshape(n, taps, d)


def _glu_rows(y, w_ref, b_ref, g_ref):
    gate = jax.nn.sigmoid(_dot(y.astype(BF16), w_ref[...]) + b_ref[...])
    return _rms(y.astype(F32) * gate, g_ref[...]).astype(BF16)


def _glu_kernel(y_ref, ys_ref, w_ref, b_ref, g_ref, o_ref, os_ref):
    for rows in _row_parts(y_ref.shape[0], 4):
        o_ref[rows, :] = _glu_rows(y_ref[rows, :], w_ref, b_ref, g_ref)

    @_on_last_row_tile
    def _():
        os_ref[...] = _glu_rows(ys_ref[...], w_ref, b_ref, g_ref)


def _glu(y, ys, w, b, g, tm):
    m, d = y.shape
    ms = ys.shape[0]
    const = lambda i: (0, 0)
    return pl.pallas_call(
        _glu_kernel,
        out_shape=(jax.ShapeDtypeStruct((m, d), BF16), jax.ShapeDtypeStruct((ms, d), BF16)),
        grid=(m // tm,),
        in_specs=[
            pl.BlockSpec((tm, d), lambda i: (i, 0)),
            pl.BlockSpec((ms, d), const),
            pl.BlockSpec((d, d), const),
            pl.BlockSpec((1, d), const),
            pl.BlockSpec((1, d), const),
        ],
        out_specs=(pl.BlockSpec((tm, d), lambda i: (i, 0)), pl.BlockSpec((ms, d), const)),
        compiler_params=_params("arbitrary"),
        name="s5_glu",
    )(y, ys, w, b.reshape(1, d), g.reshape(1, d))


def _out_proj_kernel(x_ref, ma_ref, mb_ref, xs_ref, mas_ref, mbs_ref, w_ref, o_ref, os_ref, wb_ref):
    @pl.when(pl.program_id(0) == 0)
    def _():
        wb_ref[...] = w_ref[...].astype(BF16)

    ka = ma_ref.shape[1]
    proj = lambda x, ma, mb: x + _dot(ma, wb_ref[:ka, :]) + _dot(mb, wb_ref[ka:, :])
    for rows in _row_parts(x_ref.shape[0], 2):
        o_ref[rows, :] = proj(x_ref[rows, :], ma_ref[rows, :], mb_ref[rows, :])

    @_on_last_row_tile
    def _():
        os_ref[...] = proj(xs_ref[...], mas_ref[...], mbs_ref[...])


def _out_proj(x, ma, mb, xs, mas, mbs, w, tm):
    m, n = x.shape
    ms = xs.shape[0]
    ka = ma.shape[1]
    kb = mb.shape[1]
    const = lambda i: (0, 0)
    tile = lambda i: (i, 0)
    return pl.pallas_call(
        _out_proj_kernel,
        out_shape=(jax.ShapeDtypeStruct((m, n), F32), jax.ShapeDtypeStruct((ms, n), F32)),
        grid=(m // tm,),
        in_specs=[
            pl.BlockSpec((tm, n), tile),
            pl.BlockSpec((tm, ka), tile),
            pl.BlockSpec((tm, kb), tile),
            pl.BlockSpec((ms, n), const),
            pl.BlockSpec((ms, ka), const),
            pl.BlockSpec((ms, kb), const),
            pl.BlockSpec((ka + kb, n), const, pipeline_mode=pl.Buffered(1)),
        ],
        out_specs=(pl.BlockSpec((tm, n), tile), pl.BlockSpec((ms, n), const)),
        scratch_shapes=[pltpu.VMEM((ka + kb, n), BF16)],
        compiler_params=_params("arbitrary"),
        name="out_proj",
    )(x, ma, mb, xs, mas, mbs, w)


def _mlp_kernel(x_ref, xs_ref, g_ref, wu_ref, wd_ref, o_ref, os_ref, h_ref, hs_ref):
    def ffn(h):
        act = jnp.square(jnp.maximum(_dot(h, wu_ref[...]), 0.0)).astype(BF16)
        return _dot(act, wd_ref[...])

    def rows(x_ref, o_ref, h_ref):
        first = pl.program_id(1) == 0

        @pl.when(first)
        def _():
            for part in _row_parts(x_ref.shape[0], 4):
                x = x_ref[part, :]
                h = _rms(x, g_ref[...]).astype(BF16)
                h_ref[part, :] = h
                o_ref[part, :] = x + ffn(h)

        @pl.when(jnp.logical_not(first))
        def _():
            o_ref[...] += ffn(h_ref[...])

    rows(x_ref, o_ref, h_ref)
    _on_last_row_tile(lambda: rows(xs_ref, os_ref, hs_ref))


def _mlp(x, xs, g, w_up, w_down, tm, tf):
    m, d = x.shape
    ms = xs.shape[0]
    f = w_up.shape[1]
    const = lambda i, j: (0, 0)
    return pl.pallas_call(
        _mlp_kernel,
        out_shape=(jax.ShapeDtypeStruct((m, d), F32), jax.ShapeDtypeStruct((ms, d), F32)),
        grid=(m // tm, f // tf),
        in_specs=[
            pl.BlockSpec((tm, d), lambda i, j: (i, 0)),
            pl.BlockSpec((ms, d), const),
            pl.BlockSpec((1, d), const),
            pl.BlockSpec((d, tf), lambda i, j: (0, j)),
            pl.BlockSpec((tf, d), lambda i, j: (j, 0)),
        ],
        out_specs=(pl.BlockSpec((tm, d), lambda i, j: (i, 0)), pl.BlockSpec((ms, d), const)),
        scratch_shapes=[pltpu.VMEM((tm, d), BF16), pltpu.VMEM((ms, d), BF16)],
        compiler_params=_params("arbitrary", "arbitrary"),
        name="mlp",
    )(x, xs, g.reshape(1, d), w_up, w_down)


def _ple_kernel(x_ref, p_ref, xs_ref, ps_ref, g_ref, wg_ref, wp_ref, gf_ref, o_ref, os_ref,
                wgb_ref, wpb_ref, *, final):
    @pl.when(pl.program_id(0) == 0)
    def _():
        wgb_ref[...] = wg_ref[...].astype(BF16)
        wpb_ref[...] = wp_ref[...].astype(BF16)

    def ple(x, p):
        gate = jax.nn.sigmoid(_dot(_rms(x, g_ref[...]).astype(BF16), wgb_ref[...]))
        pe = _dot(p.astype(BF16), wpb_ref[...])
        x = x + pe * gate
        return _rms(x, gf_ref[...]) if final else x

    for rows in _row_parts(x_ref.shape[0], 2):
        o_ref[rows, :] = ple(x_ref[rows, :], p_ref[rows, :])

    @_on_last_row_tile
    def _():
        os_ref[...] = ple(xs_ref[...], ps_ref[...])


def _ple(x, p, xs, ps, g, w_gate, w_ple, g_final, tm, final):
    m, d = x.shape
    ms = xs.shape[0]
    dp = p.shape[1]
    once = pl.Buffered(1)
    const = lambda i: (0, 0)
    tile = lambda i: (i, 0)
    return pl.pallas_call(
        functools.partial(_ple_kernel, final=final),
        out_shape=(jax.ShapeDtypeStruct((m, d), F32), jax.ShapeDtypeStruct((ms, d), F32)),
        grid=(m // tm,),
        in_specs=[
            pl.BlockSpec((tm, d), tile),
            pl.BlockSpec((tm, dp), tile),
            pl.BlockSpec((ms, d), const),
            pl.BlockSpec((ms, dp), const),
            pl.BlockSpec((1, d), const),
            pl.BlockSpec((d, d), const, pipeline_mode=once),
            pl.BlockSpec((dp, d), const, pipeline_mode=once),
            pl.BlockSpec((1, d), const),
        ],
        out_specs=(pl.BlockSpec((tm, d), tile), pl.BlockSpec((ms, d), const)),
        scratch_shapes=[pltpu.VMEM((d, d), BF16), pltpu.VMEM((dp, d), BF16)],
        compiler_params=_params("arbitrary"),
        name="ple_final",
    )(x, p, xs, ps, g.reshape(1, d), w_gate, w_ple, g_final.reshape(1, d))


TT_LRU = 64
TM_IN, TN_IN = 1024, 1024
TM_GLU = 1024
TM_OUT = 512
TM_MLP, TF_MLP = 512, 1024
TM_PLE = 512


def kernel(x_prompt, x_sample, state_s5_re, state_s5_im, state_lru, state_conv, p_prompt, p_sample,
           g_mix, w_in, s5_lam_re, s5_lam_im, s5_log_step, s5_b_re, s5_b_im, s5_c_re, s5_c_im, s5_d,
           s5_w_glu, s5_b_glu, conv_w, conv_b, lru_w_a, lru_b_a, lru_w_i, lru_b_i, lru_lam,
           g_merge_a, g_merge_b, w_out, g_mlp, w_up, w_down, g_ple, w_ple_gate, w_ple, g_final):
    depth = g_mix.shape[0]
    nb, t_len, d_model = x_prompt.shape
    ns = x_sample.shape[0]
    d_s5 = s5_d.shape[1]
    d_lru = conv_w.shape[2]
    groups = d_s5 // S5_CH

    xp = x_prompt.reshape(nb * t_len, d_model)
    xs = x_sample.reshape(ns, d_model)
    outs = [[] for _ in range(8)]
    for l in range(depth):
        final = l == depth - 1
        w_glu_b = s5_w_glu[l].astype(BF16)
        lru_args = (conv_w[l], conv_b[l], lru_w_a[l].astype(BF16), lru_b_a[l],
                    lru_w_i[l].astype(BF16), lru_b_i[l], lru_lam[l], g_merge_b[l])
        w_t, w_s, w_c, tab, wb, wct, arow, w_in_b = _s5_params(
            s5_lam_re[l], s5_lam_im[l], s5_log_step[l], s5_b_re[l], s5_b_im[l],
            s5_c_re[l], s5_c_im[l], s5_d[l], w_in[l])

        z, zs = _norm_matmul(xp, xs, g_mix[l], w_in_b, TM_IN, TN_IN)
        z4 = z.reshape(nb, t_len, 3 * d_s5)
        y, hf, w_down_b = _s5_seq(z4, w_t, w_s, w_c, tab, d_s5, w_down[l])
        mb, lru_h, w_up_b = _lru_seq(z4, *lru_args, w_up[l], TT_LRU)
        ys, hsr, hsi = _s5_step(zs, state_s5_re[l].reshape(ns, groups * S5_STATE),
                                state_s5_im[l].reshape(ns, groups * S5_STATE),
                                wb, wct, arow, s5_d[l].reshape(1, d_s5))
        mbs, lru_hs, conv_s = _lru_step(zs, state_conv[l], state_lru[l], *lru_args)
        ma, mas = _glu(y.reshape(nb * t_len, d_s5), ys, w_glu_b, s5_b_glu[l], g_merge_a[l], TM_GLU)
        xp, xs = _out_proj(xp, ma, mb.reshape(nb * t_len, d_lru), xs, mas, mbs, w_out[l], TM_OUT)
        xp, xs = _mlp(xp, xs, g_mlp[l], w_up_b, w_down_b, TM_MLP, TF_MLP)
        xp, xs = _ple(xp, p_prompt[l].reshape(nb * t_len, -1), xs, p_sample[l].reshape(ns, -1),
                      g_ple[l], w_ple_gate[l], w_ple[l], g_final, TM_PLE, final)
        hf = jnp.transpose(hf[:, :, :, :nb], (1, 3, 0, 2))
        outs[0].append(hf[0])
        outs[1].append(hf[1])
        outs[2].append(lru_h.reshape(nb, d_lru))
        outs[3].append(z4[:, t_len - (CONV_W - 1):, d_s5:d_s5 + d_lru])
        outs[4].append(hsr.reshape(ns, groups, S5_STATE))
        outs[5].append(hsi.reshape(ns, groups, S5_STATE))
        outs[6].append(lru_hs)
        outs[7].append(conv_s)
    return (xp.reshape(nb, t_len, d_model), xs.reshape(ns, 1, d_model),
            *(jnp.stack(o) for o in outs))
```

```python
import functools

import jax
import jax.numpy as jnp
from jax import lax
from jax.experimental import pallas as pl
from jax.experimental.pallas import tpu as pltpu

F32 = jnp.float32
BF16 = jnp.bfloat16
HIGHEST = lax.Precision.HIGHEST

EPS = 1e-6
LRU_C = 8.0
S5_CH = 16
S5_STATE = 64
LRU_HEADS = 4
CONV_W = 4
CHUNK = 16
SUBLANES = 8
LANES = 128
LANE_GROUPS = LANES // S5_CH
LANE_PAIRS = LANE_GROUPS // 2
ROWS = CHUNK * S5_CH
POW_ROWS = 32
SCAN_ROW0 = POW_ROWS
VMEM_LIMIT = 56 * 1024 * 1024


def _params(*sem):
    return pltpu.CompilerParams(dimension_semantics=sem, vmem_limit_bytes=VMEM_LIMIT)


def _rms(x, g):
    return x * lax.rsqrt(jnp.mean(x * x, axis=-1, keepdims=True) + EPS) * g


def _dot(a, b):
    return jnp.dot(a, b, preferred_element_type=F32)


def _dot_nt(a, b, precision=None):
    return lax.dot_general(a, b, (((1,), (1,)), ((), ())), precision=precision,
                           preferred_element_type=F32)


def _transpose_tiles(x):
    r, c = x.shape
    return jnp.concatenate(
        [jnp.concatenate([x[i:i + LANES, j:j + LANES].T for i in range(0, r, LANES)], axis=1)
         for j in range(0, c, LANES)], axis=0)


def _row_parts(tm, want):
    parts = max(1, min(want, tm // LANES))
    step = tm // parts
    return [slice(r * step, (r + 1) * step) for r in range(parts)]


def _on_last_row_tile(fn):
    pl.when(pl.program_id(0) == pl.num_programs(0) - 1)(fn)


def _norm_matmul_kernel(x_ref, xs_ref, g_ref, w_ref, o_ref, os_ref):
    tn = o_ref.shape[1]
    w = w_ref[:, pl.ds(pl.multiple_of(pl.program_id(1) * tn, tn), tn)]
    for rows in _row_parts(x_ref.shape[0], 8):
        h = _rms(x_ref[rows, :], g_ref[...]).astype(BF16)
        o_ref[rows, :] = _dot(h, w)

    @_on_last_row_tile
    def _():
        os_ref[...] = _dot(_rms(xs_ref[...], g_ref[...]).astype(BF16), w)


def _norm_matmul(x, xs, g, w, tm, tn):
    m, k = x.shape
    ms = xs.shape[0]
    n = w.shape[1]
    ni = m // tm
    return pl.pallas_call(
        _norm_matmul_kernel,
        out_shape=(jax.ShapeDtypeStruct((m, n), F32), jax.ShapeDtypeStruct((ms, n), F32)),
        grid=(ni, n // tn),
        in_specs=[
            pl.BlockSpec((tm, k), lambda i, j: (i, 0)),
            pl.BlockSpec((ms, k), lambda i, j: (0, 0)),
            pl.BlockSpec((1, k), lambda i, j: (0, 0)),
            pl.BlockSpec((k, n), lambda i, j: (0, 0), pipeline_mode=pl.Buffered(1)),
        ],
        out_specs=(pl.BlockSpec((tm, tn), lambda i, j: (i, j)),
                   pl.BlockSpec((ms, tn), lambda i, j: (0, jnp.where(i == ni - 1, j, 0)))),
        compiler_params=_params("arbitrary", "arbitrary"),
        name="norm_w_in",
    )(x, xs, g.reshape(1, k), w)


def _s5_param_kernel(lam_ref, bc_ref, d_ref, cast_ref,
                     wt_ref, ws_ref, wc_ref, tab_ref, wb_ref, wct_ref, arow_ref, castb_ref):
    castb_ref[...] = cast_ref[...].astype(BF16)
    row_p = lax.broadcasted_iota(jnp.int32, (POW_ROWS, LANES), 0)
    lane = lax.broadcasted_iota(jnp.int32, (ROWS, LANES), 1)
    lane16 = lax.broadcasted_iota(jnp.int32, (S5_CH, LANES), 1)
    row16 = lax.broadcasted_iota(jnp.int32, (S5_CH, LANES), 0)
    lo = lane < S5_STATE

    wb_ref[...] = jnp.zeros_like(wb_ref)
    wct_ref[...] = jnp.zeros_like(wct_ref)

    for q in range(LANE_PAIRS):
        lr, li = lam_ref[0, q], lam_ref[1, q]
        step = jnp.exp(lam_ref[2, q])
        mag = jnp.exp(lr * step)
        ar = mag * jnp.cos(li * step)
        ai = mag * jnp.sin(li * step)
        nr, ni = ar - 1.0, ai
        den = lr * lr + li * li
        cr = (nr * lr + ni * li) / den
        ci = (ni * lr - nr * li) / den
        bt_r, bt_i = bc_ref[0, q], bc_ref[1, q]
        bb_r = cr * bt_r - ci * bt_i
        bb_i = cr * bt_i + ci * bt_r
        c_r, c_i = bc_ref[2, q], bc_ref[3, q]

        pr = jnp.ones((POW_ROWS, LANES), F32)
        pi = jnp.zeros((POW_ROWS, LANES), F32)
        sr, si = ar, ai
        sq = []
        for m in range(POW_ROWS.bit_length() - 1 + SUBLANES):
            sq.append((sr, si))
            if (1 << m) < POW_ROWS:
                bit = ((row_p >> m) & 1) == 1
                pr, pi = (jnp.where(bit, pr * sr - pi * si, pr),
                          jnp.where(bit, pr * si + pi * sr, pi))
            sr, si = sr * sr - si * si, 2.0 * sr * si
        m0 = CHUNK.bit_length() - 1
        scan_r = jnp.concatenate([sq[m0 + j][0] for j in range(SUBLANES)], axis=0)
        scan_i = jnp.concatenate([sq[m0 + j][1] for j in range(SUBLANES)], axis=0)
        pad = jnp.zeros((LANES - POW_ROWS - SUBLANES, LANES), F32)
        tab_ref[q, 0] = jnp.concatenate([pr, scan_r, pad], axis=0).T
        tab_ref[q, 1] = jnp.concatenate([pi, scan_i, pad], axis=0).T
        arow_ref[q, 0] = pr[:SUBLANES]
        arow_ref[q, 1] = pi[:SUBLANES]

        def expand(tab, k0, sign):
            return jnp.concatenate(
                [jnp.broadcast_to(tab[k0 + sign * s:k0 + sign * s + 1, :], (S5_CH, LANES))
                 for s in range(CHUNK)], axis=0)

        tile = lambda v: jnp.concatenate([v] * CHUNK, axis=0)

        e_r, e_i = expand(pr, CHUNK - 1, -1), expand(pi, CHUNK - 1, -1)
        t_r, t_i = tile(bb_r), tile(bb_i)
        ws_r = _transpose_tiles(t_r * e_r - t_i * e_i)
        ws_i = _transpose_tiles(t_r * e_i + t_i * e_r)
        for h in range(2):
            rows = slice(h * S5_STATE, (h + 1) * S5_STATE)
            ws_ref[2 * q + h] = jnp.concatenate([ws_r[rows], ws_i[rows]], axis=0).astype(BF16)

        e_r, e_i = expand(pr, 1, 1), expand(pi, 1, 1)
        t_r, t_i = tile(c_r), tile(c_i)
        ca_r = t_r * e_r - t_i * e_i
        ca_n = -(t_r * e_i + t_i * e_r)
        wc_ref[2 * q] = jnp.where(lo, ca_r, pltpu.roll(ca_n, S5_STATE, 1)).astype(BF16)
        wc_ref[2 * q + 1] = jnp.where(lo, pltpu.roll(ca_r, S5_STATE, 1), ca_n).astype(BF16)

        e_r, e_i = expand(pr, 0, 1), expand(pi, 0, 1)
        c0_r = t_r * e_r - t_i * e_i
        c0_i = t_r * e_i + t_i * e_r

        for h in range(2):
            g = 2 * q + h
            mine = (lane16 < S5_STATE) == (h == 0)
            bm_r = jnp.where(mine, bb_r, 0.0)
            bm_i = jnp.where(mine, bb_i, 0.0)
            kt = _dot_nt(bm_r, c0_r, HIGHEST) - _dot_nt(bm_i, c0_i, HIGHEST)
            k_lo = kt[:, :LANES] + jnp.where(row16 == lane16, d_ref[g], 0.0)
            k_hi = kt[:, LANES:]
            blocks = []
            half = LANES // S5_CH
            for s in range(CHUNK):
                sh = (s % half) * S5_CH
                keep = lane16 >= sh
                r_lo = pltpu.roll(k_lo, sh, 1) if sh else k_lo
                r_hi = pltpu.roll(k_hi, sh, 1) if sh else k_hi
                if s < half:
                    blk = jnp.concatenate([jnp.where(keep, r_lo, 0.0),
                                           jnp.where(keep, r_hi, r_lo)], axis=1)
                else:
                    blk = jnp.concatenate([jnp.zeros_like(r_lo), jnp.where(keep, r_lo, 0.0)], axis=1)
                blocks.append(blk)
            wt_ref[g] = _transpose_tiles(jnp.concatenate(blocks, axis=0)).astype(BF16)

            r0 = (2 * q + h) * S5_CH
            c0 = q * 2 * LANES
            wb_ref[r0:r0 + S5_CH, c0:c0 + LANES] = bm_r.astype(BF16)
            wb_ref[r0:r0 + S5_CH, c0 + LANES:c0 + 2 * LANES] = bm_i.astype(BF16)
            wct_ref[q, r0:r0 + S5_CH, :LANES] = jnp.where(mine, c_r, 0.0).astype(BF16)
            wct_ref[q, r0:r0 + S5_CH, LANES:] = jnp.where(mine, -c_i, 0.0).astype(BF16)


def _s5_params(lam_re, lam_im, log_step, b_re, b_im, c_re, c_im, d, cast_w):
    groups = lam_re.shape[0]
    pairs = groups // 2
    slabs = groups // LANE_GROUPS
    cast_blk = pl.BlockSpec((cast_w.shape[0] // slabs, cast_w.shape[1]), lambda i: (i, 0))
    ls = jnp.broadcast_to(log_step[:, None], (groups, S5_STATE))
    lam = jnp.stack([lam_re, lam_im, ls]).reshape(3, pairs, 1, LANES)
    bc = jnp.stack([jnp.transpose(b_re, (0, 2, 1)), jnp.transpose(b_im, (0, 2, 1)), c_re, c_im])
    bc = jnp.transpose(bc.reshape(4, pairs, 2, S5_CH, S5_STATE), (0, 1, 3, 2, 4)).reshape(
        4, pairs, S5_CH, LANES)
    d_pad = jnp.pad(d.reshape(groups, 1, S5_CH), ((0, 0), (0, 0), (0, LANES - S5_CH)))
    blk3 = lambda i: (i, 0, 0)
    blk4 = lambda i: (i, 0, 0, 0)
    pblk = lambda i: (0, i, 0, 0)
    return pl.pallas_call(
        _s5_param_kernel,
        out_shape=(jax.ShapeDtypeStruct((groups, ROWS, ROWS), BF16),
                   jax.ShapeDtypeStruct((groups, 2 * S5_STATE, ROWS), BF16),
                   jax.ShapeDtypeStruct((groups, ROWS, 2 * S5_STATE), BF16),
                   jax.ShapeDtypeStruct((pairs, 2, LANES, LANES), F32),
                   jax.ShapeDtypeStruct((slabs, LANES, LANE_PAIRS * 2 * LANES), BF16),
                   jax.ShapeDtypeStruct((slabs, LANE_PAIRS, LANES, 2 * LANES), BF16),
                   jax.ShapeDtypeStruct((pairs, 2, SUBLANES, LANES), F32),
                   jax.ShapeDtypeStruct(cast_w.shape, BF16)),
        grid=(slabs,),
        in_specs=[pl.BlockSpec((3, LANE_PAIRS, 1, LANES), pblk),
                  pl.BlockSpec((4, LANE_PAIRS, S5_CH, LANES), pblk),
                  pl.BlockSpec((LANE_GROUPS, 1, LANES), blk3), cast_blk],
        out_specs=(pl.BlockSpec((LANE_GROUPS, ROWS, ROWS), blk3),
                   pl.BlockSpec((LANE_GROUPS, 2 * S5_STATE, ROWS), blk3),
                   pl.BlockSpec((LANE_GROUPS, ROWS, 2 * S5_STATE), blk3),
                   pl.BlockSpec((LANE_PAIRS, 2, LANES, LANES), blk4),
                   pl.BlockSpec((None, LANES, LANE_PAIRS * 2 * LANES), blk3),
                   pl.BlockSpec((None, LANE_PAIRS, LANES, 2 * LANES), blk4),
                   pl.BlockSpec((LANE_PAIRS, 2, SUBLANES, LANES), blk4),
                   cast_blk),
        compiler_params=_params("parallel"),
        name="s5_params",
    )(lam, bc, d_pad, cast_w)


def _cast_specs(cast_ws, steps, index_map):
    return [pl.BlockSpec((c.shape[0] // steps, c.shape[1]), index_map) for c in cast_ws]


def _cast_slices(cast_refs, castb_refs):
    for src, dst in zip(cast_refs, castb_refs):
        dst[...] = src[...].astype(BF16)


def _s5_seq_kernel(z_ref, wt_ref, ws_ref, wc_ref, tab_ref, *rest, nb, n_chunks, n_cast):
    cast_refs, (y_ref, hf_ref) = rest[:n_cast], rest[n_cast:n_cast + 2]
    castb_refs = rest[n_cast + 2:2 * n_cast + 2]
    ut_ref, yt_ref, sr_ref, si_ref = rest[2 * n_cast + 2:]
    _cast_slices(cast_refs, castb_refs)
    scan_steps = n_chunks.bit_length() - 1
    for n in range(nb):
        for s in range(CHUNK):
            xs = z_ref[n, pl.ds(s, n_chunks, stride=CHUNK), :]
            ut_ref[:, s * S5_CH:(s + 1) * S5_CH, n * n_chunks:(n + 1) * n_chunks] = (
                xs.T.astype(BF16).reshape(LANE_GROUPS, S5_CH, n_chunks))

    for g in range(LANE_GROUPS):
        st = _dot(ws_ref[g], ut_ref[g])
        sr_ref[g * S5_STATE:(g + 1) * S5_STATE, :] = st[:S5_STATE]
        si_ref[g * S5_STATE:(g + 1) * S5_STATE, :] = st[S5_STATE:]

    srows = LANE_GROUPS * S5_STATE
    lane = lax.broadcasted_iota(jnp.int32, (srows, n_chunks), 1)
    tab_r = jnp.concatenate([tab_ref[q, 0] for q in range(LANE_PAIRS)], axis=0)
    tab_i = jnp.concatenate([tab_ref[q, 1] for q in range(LANE_PAIRS)], axis=0)
    fin_r = jnp.zeros((srows, n_chunks), F32)
    fin_i = jnp.zeros((srows, n_chunks), F32)
    for n in range(nb):
        cols = slice(n * n_chunks, (n + 1) * n_chunks)
        xr = sr_ref[:, cols]
        xi = si_ref[:, cols]
        for k in range(scan_steps):
            d = 1 << k
            keep = lane >= d
            sr = jnp.where(keep, pltpu.roll(xr, d, 1), 0.0)
            si = jnp.where(keep, pltpu.roll(xi, d, 1), 0.0)
            pr = tab_r[:, SCAN_ROW0 + k:SCAN_ROW0 + k + 1]
            pi = tab_i[:, SCAN_ROW0 + k:SCAN_ROW0 + k + 1]
            xr, xi = xr + (pr * sr - pi * si), xi + (pr * si + pi * sr)
        last = lane == n_chunks - 1
        fin_r = fin_r + jnp.where(
            lane == n, jnp.sum(jnp.where(last, xr, 0.0), axis=1, keepdims=True), 0.0)
        fin_i = fin_i + jnp.where(
            lane == n, jnp.sum(jnp.where(last, xi, 0.0), axis=1, keepdims=True), 0.0)
        keep = lane >= 1
        sr_ref[:, cols] = jnp.where(keep, pltpu.roll(xr, 1, 1), 0.0)
        si_ref[:, cols] = jnp.where(keep, pltpu.roll(xi, 1, 1), 0.0)

    for g in range(LANE_GROUPS):
        rows = slice(g * S5_STATE, (g + 1) * S5_STATE)
        hf_ref[g, 0] = fin_r[rows]
        hf_ref[g, 1] = fin_i[rows]
        hs = jnp.concatenate([sr_ref[rows, :], si_ref[rows, :]], axis=0).astype(BF16)
        yt_ref[g] = jax.nn.gelu(_dot(wc_ref[g], hs) + _dot(wt_ref[g], ut_ref[g]))

    for n in range(nb):
        for s in range(CHUNK):
            blk = yt_ref[:, s * S5_CH:(s + 1) * S5_CH, n * n_chunks:(n + 1) * n_chunks]
            y_ref[n, pl.ds(s, n_chunks, stride=CHUNK), :] = blk.reshape(LANES, n_chunks).T


def _s5_seq(z, w_t, w_s, w_c, tab, d_s5, cast_ws):
    nb, t_len, _ = z.shape
    n_chunks = t_len // CHUNK
    assert n_chunks == LANES, "the chunk axis must fill one 128-lane tile"
    groups = d_s5 // S5_CH
    steps = groups // LANE_GROUPS
    cast_blks = _cast_specs(cast_ws, steps, lambda i: (i, 0))
    kern = functools.partial(_s5_seq_kernel, nb=nb, n_chunks=n_chunks, n_cast=len(cast_ws))
    blk3 = lambda i: (i, 0, 0)
    blk4 = lambda i: (i, 0, 0, 0)
    outs = pl.pallas_call(
        kern,
        out_shape=(jax.ShapeDtypeStruct((nb, t_len, d_s5), F32),
                   jax.ShapeDtypeStruct((groups, 2, S5_STATE, LANES), F32),
                   *(jax.ShapeDtypeStruct(c.shape, BF16) for c in cast_ws)),
        grid=(steps,),
        in_specs=[
            pl.BlockSpec((nb, t_len, LANES), lambda i: (0, 0, i)),
            pl.BlockSpec((LANE_GROUPS, ROWS, ROWS), blk3),
            pl.BlockSpec((LANE_GROUPS, 2 * S5_STATE, ROWS), blk3),
            pl.BlockSpec((LANE_GROUPS, ROWS, 2 * S5_STATE), blk3),
            pl.BlockSpec((LANE_PAIRS, 2, LANES, LANES), blk4),
            *cast_blks,
        ],
        out_specs=(pl.BlockSpec((nb, t_len, LANES), lambda i: (0, 0, i)),
                   pl.BlockSpec((LANE_GROUPS, 2, S5_STATE, LANES), blk4),
                   *cast_blks),
        scratch_shapes=[pltpu.VMEM((LANE_GROUPS, ROWS, nb * n_chunks), BF16),
                        pltpu.VMEM((LANE_GROUPS, ROWS, nb * n_chunks), F32),
                        pltpu.VMEM((LANE_GROUPS * S5_STATE, nb * n_chunks), F32),
                        pltpu.VMEM((LANE_GROUPS * S5_STATE, nb * n_chunks), F32)],
        compiler_params=_params("parallel"),
        name="s5_seq",
    )(z, w_t, w_s, w_c, tab, *cast_ws)
    return outs[0], outs[1], outs[2:]


def _s5_step_kernel(u_ref, h0r_ref, h0i_ref, wb_ref, wct_ref, arow_ref, d_ref,
                    y_ref, hr_ref, hi_ref):
    u = u_ref[...]
    bu = _dot(u.astype(BF16), wb_ref[...])
    y = d_ref[...] * u
    for q in range(LANE_PAIRS):
        cols = slice(q * LANES, (q + 1) * LANES)
        ar = arow_ref[q, 0, 1:2, :]
        ai = arow_ref[q, 1, 1:2, :]
        h0r, h0i = h0r_ref[:, cols], h0i_ref[:, cols]
        hr = ar * h0r - ai * h0i + bu[:, 2 * q * LANES:(2 * q + 1) * LANES]
        hi = ar * h0i + ai * h0r + bu[:, (2 * q + 1) * LANES:(2 * q + 2) * LANES]
        hr_ref[:, cols] = hr
        hi_ref[:, cols] = hi
        y = y + _dot_nt(jnp.concatenate([hr, hi], axis=1).astype(BF16), wct_ref[q])
    y_ref[...] = jax.nn.gelu(y).astype(BF16)


def _s5_step(z, h0r, h0i, wb, wct, arow, d):
    n = z.shape[0]
    slabs = wb.shape[0]
    sw = LANE_PAIRS * LANES
    col = lambda i: (0, i)
    blk3 = lambda i: (i, 0, 0)
    blk4 = lambda i: (i, 0, 0, 0)
    return pl.pallas_call(
        _s5_step_kernel,
        out_shape=(jax.ShapeDtypeStruct((n, slabs * LANES), BF16),
                   jax.ShapeDtypeStruct((n, slabs * sw), F32),
                   jax.ShapeDtypeStruct((n, slabs * sw), F32)),
        grid=(slabs,),
        in_specs=[
            pl.BlockSpec((n, LANES), col),
            pl.BlockSpec((n, sw), col),
            pl.BlockSpec((n, sw), col),
            pl.BlockSpec((None, LANES, 2 * sw), blk3),
            pl.BlockSpec((None, LANE_PAIRS, LANES, 2 * LANES), blk4),
            pl.BlockSpec((LANE_PAIRS, 2, SUBLANES, LANES), blk4),
            pl.BlockSpec((1, LANES), col),
        ],
        out_specs=(pl.BlockSpec((n, LANES), col),
                   pl.BlockSpec((n, sw), col),
                   pl.BlockSpec((n, sw), col)),
        compiler_params=_params("parallel"),
        name="s5_step",
    )(z, h0r, h0i, wb, wct, arow, d)


def _lru_gate_block(xc, wa, ba, wi, bi, lam):
    xb16 = xc.astype(BF16)
    r = jax.nn.sigmoid(_dot(xb16, wa) + ba)
    ig = jax.nn.sigmoid(_dot(xb16, wi) + bi)
    log_a = -LRU_C * r * jax.nn.softplus(-lam)
    a = jnp.exp(log_a)
    mult = jnp.sqrt(1.0 - a * a)
    return a, mult * (ig * xc)


def _lru_gates(xc, wa_ref, ba, wi_ref, bi, lam):
    blk = xc.shape[1] // LRU_HEADS
    parts = [_lru_gate_block(xc[:, h * blk:(h + 1) * blk], wa_ref[h], ba[:, h * blk:(h + 1) * blk],
                             wi_ref[h], bi[:, h * blk:(h + 1) * blk],
                             lam[:, h * blk:(h + 1) * blk]) for h in range(LRU_HEADS)]
    return (jnp.concatenate([p[0] for p in parts], axis=-1),
            jnp.concatenate([p[1] for p in parts], axis=-1))


def _lru_seq_kernel(xb_ref, gb_ref, cw_ref, cb_ref, wa_ref, ba_ref, wi_ref, bi_ref, lam_ref,
                    gm_ref, *rest, tt, n_cast):
    cast_refs, (o_ref, hl_ref) = rest[:n_cast], rest[n_cast:n_cast + 2]
    castb_refs = rest[n_cast + 2:2 * n_cast + 2]
    xe_ref, a_ref, b_ref, hc_ref = rest[2 * n_cast + 2:]
    _cast_slices(cast_refs, castb_refs)
    halo = SUBLANES
    nseq, _, d = xb_ref.shape
    blk = d // LRU_HEADS

    @pl.when(pl.program_id(0) == 0)
    def _():
        xe_ref[:, 0:halo, :] = jnp.zeros((nseq, halo, d), F32)
        hc_ref[...] = jnp.zeros_like(hc_ref)

    xe_ref[:, halo:halo + tt, :] = xb_ref[...]
    for h in range(LRU_HEADS):
        cols = slice(h * blk, (h + 1) * blk)
        cw = cw_ref[:, cols]
        xc = cb_ref[:, cols] + xe_ref[:, halo:halo + tt, cols] * cw[CONV_W - 1:CONV_W, :]
        for k in range(1, CONV_W):
            xc = xc + xe_ref[:, halo - k:halo - k + tt, cols] * cw[CONV_W - 1 - k:CONV_W - k, :]
        a, b = _lru_gate_block(xc.reshape(nseq * tt, blk), wa_ref[h], ba_ref[:, cols],
                               wi_ref[h], bi_ref[:, cols], lam_ref[:, cols])
        a_ref[:, :, cols] = a.reshape(nseq, tt, blk)
        b_ref[:, :, cols] = b.reshape(nseq, tt, blk)
    xe_ref[:, 0:halo, :] = xb_ref[:, tt - halo:tt, :]

    def block(i, h):
        base = pl.multiple_of(i * SUBLANES, SUBLANES)
        for j in range(SUBLANES):
            row = pl.ds(base + j, 1)
            h = a_ref[:, row, :] * h + b_ref[:, row, :]
            b_ref[:, row, :] = h
        return h

    h = lax.fori_loop(0, tt // SUBLANES, block, hc_ref[...])
    hc_ref[...] = h
    hl_ref[...] = h

    out = b_ref[...] * jax.nn.gelu(gb_ref[...])
    o_ref[...] = _rms(out, gm_ref[...]).astype(BF16)


def _lru_seq(z, conv_w, conv_b, w_a, b_a, w_i, b_i, lam, g_merge, cast_ws, tt):
    nseq, t_len, _ = z.shape
    d = conv_w.shape[1]
    steps = t_len // tt
    cast_blks = _cast_specs(cast_ws, steps, lambda t: (t, 0))
    row = lambda v: v.reshape(1, d)
    const2 = lambda t: (0, 0)
    const3 = lambda t: (0, 0, 0)
    kern = functools.partial(_lru_seq_kernel, tt=tt, n_cast=len(cast_ws))
    outs = pl.pallas_call(
        kern,
        out_shape=(jax.ShapeDtypeStruct((nseq, t_len, d), BF16),
                   jax.ShapeDtypeStruct((nseq, 1, d), F32),
                   *(jax.ShapeDtypeStruct(c.shape, BF16) for c in cast_ws)),
        grid=(steps,),
        in_specs=[
            pl.BlockSpec((nseq, tt, d), lambda t: (0, t, 1)),
            pl.BlockSpec((nseq, tt, d), lambda t: (0, t, 2)),
            pl.BlockSpec((CONV_W, d), const2),
            pl.BlockSpec((1, d), const2),
            pl.BlockSpec(w_a.shape, const3),
            pl.BlockSpec((1, d), const2),
            pl.BlockSpec(w_i.shape, const3),
            pl.BlockSpec((1, d), const2),
            pl.BlockSpec((1, d), const2),
            pl.BlockSpec((1, d), const2),
            *cast_blks,
        ],
        out_specs=(pl.BlockSpec((nseq, tt, d), lambda t: (0, t, 0)),
                   pl.BlockSpec((nseq, 1, d), const3),
                   *cast_blks),
        scratch_shapes=[pltpu.VMEM((nseq, tt + SUBLANES, d), F32),
                        pltpu.VMEM((nseq, tt, d), F32),
                        pltpu.VMEM((nseq, tt, d), F32),
                        pltpu.VMEM((nseq, 1, d), F32)],
        compiler_params=_params("arbitrary"),
        name="rglru_seq",
    )(z, z, conv_w, row(conv_b), w_a, row(b_a), w_i, row(b_i), row(lam), row(g_merge), *cast_ws)
    return outs[0], outs[1], outs[2:]


def _lru_step_kernel(xb_ref, gb_ref, c0_ref, c1_ref, c2_ref, h0_ref, cw_ref, cb_ref,
                     wa_ref, ba_ref, wi_ref, bi_ref, lam_ref, gm_ref, o_ref, h_ref, cv_ref):
    d = xb_ref.shape[1]
    xb = xb_ref[...]
    cw = cw_ref[...]
    xc = (cb_ref[...] + c0_ref[...] * cw[0:1, :] + c1_ref[...] * cw[1:2, :]
          + c2_ref[...] * cw[2:3, :] + xb * cw[3:4, :])
    a, b = _lru_gates(xc, wa_ref, ba_ref[...], wi_ref, bi_ref[...], lam_ref[...])
    h = a * h0_ref[...] + b
    h_ref[...] = h
    out = h * jax.nn.gelu(gb_ref[...])
    o_ref[...] = _rms(out, gm_ref[...]).astype(BF16)
    cv_ref[:, 0:d] = c1_ref[...]
    cv_ref[:, d:2 * d] = c2_ref[...]
    cv_ref[:, 2 * d:3 * d] = xb


def _lru_step(z, conv0, h0, conv_w, conv_b, w_a, b_a, w_i, b_i, lam, g_merge):
    n = z.shape[0]
    d = conv_w.shape[1]
    taps = CONV_W - 1
    conv2d = conv0.reshape(n, taps * d)
    row = lambda v: v.reshape(1, d)
    full = lambda shape: pl.BlockSpec(shape, lambda i: (0,) * len(shape))
    col = lambda c: pl.BlockSpec((n, d), lambda i: (0, c))
    nd = full((n, d))
    rd = full((1, d))
    o, h, cv = pl.pallas_call(
        _lru_step_kernel,
        out_shape=(jax.ShapeDtypeStruct((n, d), BF16), jax.ShapeDtypeStruct((n, d), F32),
                   jax.ShapeDtypeStruct((n, taps * d), F32)),
        grid=(1,),
        in_specs=[
            col(1), col(2), col(0), col(1), col(2), nd,
            full((CONV_W, d)), rd,
            full(w_a.shape), rd, full(w_i.shape), rd, rd, rd,
        ],
        out_specs=(nd, nd, full((n, taps * d))),
        compiler_params=_params("arbitrary"),
        name="rglru_step",
    )(z, z, conv2d, conv2d, conv2d, h0, conv_w, row(conv_b),
      w_a, row(b_a), w_i, row(b_i), row(lam), row(g_merge))
    return o, h, cv.reshape(n, taps, d)


def _glu_rows(y, w_ref, b_ref, g_ref):
    gate = jax.nn.sigmoid(_dot(y.astype(BF16), w_ref[...]) + b_ref[...])
    return _rms(y.astype(F32) * gate, g_ref[...]).astype(BF16)


def _glu_kernel(y_ref, ys_ref, w_ref, b_ref, g_ref, o_ref, os_ref):
    for rows in _row_parts(y_ref.shape[0], 4):
        o_ref[rows, :] = _glu_rows(y_ref[rows, :], w_ref, b_ref, g_ref)

    @_on_last_row_tile
    def _():
        os_ref[...] = _glu_rows(ys_ref[...], w_ref, b_ref, g_ref)


def _glu(y, ys, w, b, g, tm):
    m, d = y.shape
    ms = ys.shape[0]
    const = lambda i: (0, 0)
    return pl.pallas_call(
        _glu_kernel,
        out_shape=(jax.ShapeDtypeStruct((m, d), BF16), jax.ShapeDtypeStruct((ms, d), BF16)),
        grid=(m // tm,),
        in_specs=[
            pl.BlockSpec((tm, d), lambda i: (i, 0)),
            pl.BlockSpec((ms, d), const),
            pl.BlockSpec((d, d), const),
            pl.BlockSpec((1, d), const),
            pl.BlockSpec((1, d), const),
        ],
        out_specs=(pl.BlockSpec((tm, d), lambda i: (i, 0)), pl.BlockSpec((ms, d), const)),
        compiler_params=_params("arbitrary"),
        name="s5_glu",
    )(y, ys, w, b.reshape(1, d), g.reshape(1, d))


def _out_proj_kernel(x_ref, ma_ref, mb_ref, xs_ref, mas_ref, mbs_ref, w_ref, o_ref, os_ref, wb_ref):
    @pl.when(pl.program_id(0) == 0)
    def _():
        wb_ref[...] = w_ref[...].astype(BF16)

    ka = ma_ref.shape[1]
    proj = lambda x, ma, mb: x + _dot(ma, wb_ref[:ka, :]) + _dot(mb, wb_ref[ka:, :])
    for rows in _row_parts(x_ref.shape[0], 2):
        o_ref[rows, :] = proj(x_ref[rows, :], ma_ref[rows, :], mb_ref[rows, :])

    @_on_last_row_tile
    def _():
        os_ref[...] = proj(xs_ref[...], mas_ref[...], mbs_ref[...])


def _out_proj(x, ma, mb, xs, mas, mbs, w, tm):
    m, n = x.shape
    ms = xs.shape[0]
    ka = ma.shape[1]
    kb = mb.shape[1]
    const = lambda i: (0, 0)
    tile = lambda i: (i, 0)
    return pl.pallas_call(
        _out_proj_kernel,
        out_shape=(jax.ShapeDtypeStruct((m, n), F32), jax.ShapeDtypeStruct((ms, n), F32)),
        grid=(m // tm,),
        in_specs=[
            pl.BlockSpec((tm, n), tile),
            pl.BlockSpec((tm, ka), tile),
            pl.BlockSpec((tm, kb), tile),
            pl.BlockSpec((ms, n), const),
            pl.BlockSpec((ms, ka), const),
            pl.BlockSpec((ms, kb), const),
            pl.BlockSpec((ka + kb, n), const, pipeline_mode=pl.Buffered(1)),
        ],
        out_specs=(pl.BlockSpec((tm, n), tile), pl.BlockSpec((ms, n), const)),
        scratch_shapes=[pltpu.VMEM((ka + kb, n), BF16)],
        compiler_params=_params("arbitrary"),
        name="out_proj",
    )(x, ma, mb, xs, mas, mbs, w)


def _mlp_ple_kernel(x_ref, p_ref, xs_ref, ps_ref, g_ref, wu_ref, wd_ref, gp_ref, wg_ref, wp_ref,
                    gf_ref, o_ref, os_ref, h_ref, hs_ref, *, final):
    def ffn(h):
        act = jnp.square(jnp.maximum(_dot(h, wu_ref[...]), 0.0)).astype(BF16)
        return _dot(act, wd_ref[...])

    def ple(x, p):
        gate = jax.nn.sigmoid(_dot(_rms(x, gp_ref[...]).astype(BF16), wg_ref[...]))
        pe = _dot(p.astype(BF16), wp_ref[...])
        x = x + pe * gate
        return _rms(x, gf_ref[...]) if final else x

    def rows(x_ref, p_ref, o_ref, h_ref):
        first = pl.program_id(1) == 0

        @pl.when(first)
        def _():
            for part in _row_parts(x_ref.shape[0], 4):
                x = x_ref[part, :]
                h = _rms(x, g_ref[...]).astype(BF16)
                h_ref[part, :] = h
                o_ref[part, :] = x + ffn(h)

        @pl.when(jnp.logical_not(first))
        def _():
            o_ref[...] += ffn(h_ref[...])

        @pl.when(pl.program_id(1) == pl.num_programs(1) - 1)
        def _():
            for part in _row_parts(x_ref.shape[0], 4):
                o_ref[part, :] = ple(o_ref[part, :], p_ref[part, :])

    rows(x_ref, p_ref, o_ref, h_ref)
    _on_last_row_tile(lambda: rows(xs_ref, ps_ref, os_ref, hs_ref))


def _mlp_ple(x, p, xs, ps, g, w_up, w_down, g_ple, w_gate, w_ple, g_final, tm, tf, final):
    m, d = x.shape
    ms = xs.shape[0]
    dp = p.shape[1]
    f = w_up.shape[1]
    const = lambda i, j: (0, 0)
    tile = lambda i, j: (i, 0)
    once = pl.Buffered(1)
    return pl.pallas_call(
        functools.partial(_mlp_ple_kernel, final=final),
        out_shape=(jax.ShapeDtypeStruct((m, d), F32), jax.ShapeDtypeStruct((ms, d), F32)),
        grid=(m // tm, f // tf),
        in_specs=[
            pl.BlockSpec((tm, d), tile),
            pl.BlockSpec((tm, dp), tile),
            pl.BlockSpec((ms, d), const),
            pl.BlockSpec((ms, dp), const),
            pl.BlockSpec((1, d), const),
            pl.BlockSpec((d, tf), lambda i, j: (0, j)),
            pl.BlockSpec((tf, d), lambda i, j: (j, 0)),
            pl.BlockSpec((1, d), const),
            pl.BlockSpec((d, d), const, pipeline_mode=once),
            pl.BlockSpec((dp, d), const, pipeline_mode=once),
            pl.BlockSpec((1, d), const),
        ],
        out_specs=(pl.BlockSpec((tm, d), tile), pl.BlockSpec((ms, d), const)),
        scratch_shapes=[pltpu.VMEM((tm, d), BF16), pltpu.VMEM((ms, d), BF16)],
        compiler_params=_params("arbitrary", "arbitrary"),
        name="mlp_ple",
    )(x, p, xs, ps, g.reshape(1, d), w_up, w_down, g_ple.reshape(1, d), w_gate, w_ple,
      g_final.reshape(1, d))


TT_LRU = 64
TM_IN, TN_IN = 1024, 1024
TM_GLU = 1024
TM_OUT = 512
TM_MLP, TF_MLP = 512, 1024


def kernel(x_prompt, x_sample, state_s5_re, state_s5_im, state_lru, state_conv, p_prompt, p_sample,
           g_mix, w_in, s5_lam_re, s5_lam_im, s5_log_step, s5_b_re, s5_b_im, s5_c_re, s5_c_im, s5_d,
           s5_w_glu, s5_b_glu, conv_w, conv_b, lru_w_a, lru_b_a, lru_w_i, lru_b_i, lru_lam,
           g_merge_a, g_merge_b, w_out, g_mlp, w_up, w_down, g_ple, w_ple_gate, w_ple, g_final):
    depth = g_mix.shape[0]
    nb, t_len, d_model = x_prompt.shape
    ns = x_sample.shape[0]
    d_s5 = s5_d.shape[1]
    d_lru = conv_w.shape[2]
    groups = d_s5 // S5_CH

    xp = x_prompt.reshape(nb * t_len, d_model)
    xs = x_sample.reshape(ns, d_model)
    outs = [[] for _ in range(8)]
    for l in range(depth):
        final = l == depth - 1
        w_glu_b = s5_w_glu[l].astype(BF16)
        lru_args = (conv_w[l], conv_b[l], lru_w_a[l].astype(BF16), lru_b_a[l],
                    lru_w_i[l].astype(BF16), lru_b_i[l], lru_lam[l], g_merge_b[l])
        w_t, w_s, w_c, tab, wb, wct, arow, w_in_b = _s5_params(
            s5_lam_re[l], s5_lam_im[l], s5_log_step[l], s5_b_re[l], s5_b_im[l],
            s5_c_re[l], s5_c_im[l], s5_d[l], w_in[l])

        z, zs = _norm_matmul(xp, xs, g_mix[l], w_in_b, TM_IN, TN_IN)
        z4 = z.reshape(nb, t_len, 3 * d_s5)
        y, hf, (w_down_b, w_ple_b) = _s5_seq(z4, w_t, w_s, w_c, tab, d_s5, (w_down[l], w_ple[l]))
        mb, lru_h, (w_up_b, w_gate_b) = _lru_seq(z4, *lru_args, (w_up[l], w_ple_gate[l]), TT_LRU)
        ys, hsr, hsi = _s5_step(zs, state_s5_re[l].reshape(ns, groups * S5_STATE),
                                state_s5_im[l].reshape(ns, groups * S5_STATE),
                                wb, wct, arow, s5_d[l].reshape(1, d_s5))
        mbs, lru_hs, conv_s = _lru_step(zs, state_conv[l], state_lru[l], *lru_args)
        ma, mas = _glu(y.reshape(nb * t_len, d_s5), ys, w_glu_b, s5_b_glu[l], g_merge_a[l], TM_GLU)
        xp, xs = _out_proj(xp, ma, mb.reshape(nb * t_len, d_lru), xs, mas, mbs, w_out[l], TM_OUT)
        xp, xs = _mlp_ple(xp, p_prompt[l].reshape(nb * t_len, -1), xs, p_sample[l].reshape(ns, -1),
                          g_mlp[l], w_up_b, w_down_b, g_ple[l], w_gate_b, w_ple_b, g_final,
                          TM_MLP, TF_MLP, final)
        hf = jnp.transpose(hf[:, :, :, :nb], (1, 3, 0, 2))
        outs[0].append(hf[0])
        outs[1].append(hf[1])
        outs[2].append(lru_h.reshape(nb, d_lru))
        outs[3].append(z4[:, t_len - (CONV_W - 1):, d_s5:d_s5 + d_lru])
        outs[4].append(hsr.reshape(ns, groups, S5_STATE))
        outs[5].append(hsi.reshape(ns, groups, S5_STATE))
        outs[6].append(lru_hs)
        outs[7].append(conv_s)
    return (xp.reshape(nb, t_len, d_model), xs.reshape(ns, 1, d_model),
            *(jnp.stack(o) for o in outs))
```

```python
import functools

import jax
import jax.numpy as jnp
from jax import lax
from jax.experimental import pallas as pl
from jax.experimental.pallas import tpu as pltpu

F32 = jnp.float32
BF16 = jnp.bfloat16
HIGHEST = lax.Precision.HIGHEST

EPS = 1e-6
LRU_C = 8.0
S5_CH = 16
S5_STATE = 64
LRU_HEADS = 4
CONV_W = 4
CHUNK = 16
SUBLANES = 8
LANES = 128
LANE_GROUPS = LANES // S5_CH
LANE_PAIRS = LANE_GROUPS // 2
ROWS = CHUNK * S5_CH
POW_ROWS = 32
SCAN_ROW0 = POW_ROWS
VMEM_LIMIT = 56 * 1024 * 1024


def _params(*sem):
    return pltpu.CompilerParams(dimension_semantics=sem, vmem_limit_bytes=VMEM_LIMIT)


def _rms(x, g):
    return x * lax.rsqrt(jnp.mean(x * x, axis=-1, keepdims=True) + EPS) * g


def _dot(a, b):
    return jnp.dot(a, b, preferred_element_type=F32)


def _dot_nt(a, b, precision=None):
    return lax.dot_general(a, b, (((1,), (1,)), ((), ())), precision=precision,
                           preferred_element_type=F32)


def _transpose_tiles(x):
    r, c = x.shape
    return jnp.concatenate(
        [jnp.concatenate([x[i:i + LANES, j:j + LANES].T for i in range(0, r, LANES)], axis=1)
         for j in range(0, c, LANES)], axis=0)


def _row_parts(tm, want):
    parts = max(1, min(want, tm // LANES))
    step = tm // parts
    return [slice(r * step, (r + 1) * step) for r in range(parts)]


def _on_last_row_tile(fn):
    pl.when(pl.program_id(0) == pl.num_programs(0) - 1)(fn)


def _norm_matmul_kernel(x_ref, xs_ref, g_ref, w_ref, o_ref, os_ref):
    tn = o_ref.shape[1]
    w = w_ref[:, pl.ds(pl.multiple_of(pl.program_id(1) * tn, tn), tn)]
    for rows in _row_parts(x_ref.shape[0], 8):
        h = _rms(x_ref[rows, :], g_ref[...]).astype(BF16)
        o_ref[rows, :] = _dot(h, w)

    @_on_last_row_tile
    def _():
        os_ref[...] = _dot(_rms(xs_ref[...], g_ref[...]).astype(BF16), w)


def _norm_matmul(x, xs, g, w, tm, tn):
    m, k = x.shape
    ms = xs.shape[0]
    n = w.shape[1]
    ni = m // tm
    return pl.pallas_call(
        _norm_matmul_kernel,
        out_shape=(jax.ShapeDtypeStruct((m, n), F32), jax.ShapeDtypeStruct((ms, n), F32)),
        grid=(ni, n // tn),
        in_specs=[
            pl.BlockSpec((tm, k), lambda i, j: (i, 0)),
            pl.BlockSpec((ms, k), lambda i, j: (0, 0)),
            pl.BlockSpec((1, k), lambda i, j: (0, 0)),
            pl.BlockSpec((k, n), lambda i, j: (0, 0), pipeline_mode=pl.Buffered(1)),
        ],
        out_specs=(pl.BlockSpec((tm, tn), lambda i, j: (i, j)),
                   pl.BlockSpec((ms, tn), lambda i, j: (0, jnp.where(i == ni - 1, j, 0)))),
        compiler_params=_params("arbitrary", "arbitrary"),
        name="norm_w_in",
    )(x, xs, g.reshape(1, k), w)


def _s5_param_kernel(lam_ref, bc_ref, d_ref, cast_ref,
                     wt_ref, ws_ref, wc_ref, tab_ref, wb_ref, wct_ref, arow_ref, castb_ref):
    castb_ref[...] = cast_ref[...].astype(BF16)
    row_p = lax.broadcasted_iota(jnp.int32, (POW_ROWS, LANES), 0)
    lane = lax.broadcasted_iota(jnp.int32, (ROWS, LANES), 1)
    lane16 = lax.broadcasted_iota(jnp.int32, (S5_CH, LANES), 1)
    row16 = lax.broadcasted_iota(jnp.int32, (S5_CH, LANES), 0)
    lo = lane < S5_STATE

    wb_ref[...] = jnp.zeros_like(wb_ref)
    wct_ref[...] = jnp.zeros_like(wct_ref)

    for q in range(LANE_PAIRS):
        lr, li = lam_ref[0, q], lam_ref[1, q]
        step = jnp.exp(lam_ref[2, q])
        mag = jnp.exp(lr * step)
        ar = mag * jnp.cos(li * step)
        ai = mag * jnp.sin(li * step)
        nr, ni = ar - 1.0, ai
        den = lr * lr + li * li
        cr = (nr * lr + ni * li) / den
        ci = (ni * lr - nr * li) / den
        bt_r, bt_i = bc_ref[0, q], bc_ref[1, q]
        bb_r = cr * bt_r - ci * bt_i
        bb_i = cr * bt_i + ci * bt_r
        c_r, c_i = bc_ref[2, q], bc_ref[3, q]

        pr = jnp.ones((POW_ROWS, LANES), F32)
        pi = jnp.zeros((POW_ROWS, LANES), F32)
        sr, si = ar, ai
        sq = []
        for m in range(POW_ROWS.bit_length() - 1 + SUBLANES):
            sq.append((sr, si))
            if (1 << m) < POW_ROWS:
                bit = ((row_p >> m) & 1) == 1
                pr, pi = (jnp.where(bit, pr * sr - pi * si, pr),
                          jnp.where(bit, pr * si + pi * sr, pi))
            sr, si = sr * sr - si * si, 2.0 * sr * si
        m0 = CHUNK.bit_length() - 1
        scan_r = jnp.concatenate([sq[m0 + j][0] for j in range(SUBLANES)], axis=0)
        scan_i = jnp.concatenate([sq[m0 + j][1] for j in range(SUBLANES)], axis=0)
        pad = jnp.zeros((LANES - POW_ROWS - SUBLANES, LANES), F32)
        tab_ref[q, 0] = jnp.concatenate([pr, scan_r, pad], axis=0).T
        tab_ref[q, 1] = jnp.concatenate([pi, scan_i, pad], axis=0).T
        arow_ref[q, 0] = pr[:SUBLANES]
        arow_ref[q, 1] = pi[:SUBLANES]

        def expand(tab, k0, sign):
            return jnp.concatenate(
                [jnp.broadcast_to(tab[k0 + sign * s:k0 + sign * s + 1, :], (S5_CH, LANES))
                 for s in range(CHUNK)], axis=0)

        tile = lambda v: jnp.concatenate([v] * CHUNK, axis=0)

        e_r, e_i = expand(pr, CHUNK - 1, -1), expand(pi, CHUNK - 1, -1)
        t_r, t_i = tile(bb_r), tile(bb_i)
        ws_r = _transpose_tiles(t_r * e_r - t_i * e_i)
        ws_i = _transpose_tiles(t_r * e_i + t_i * e_r)
        for h in range(2):
            rows = slice(h * S5_STATE, (h + 1) * S5_STATE)
            ws_ref[2 * q + h] = jnp.concatenate([ws_r[rows], ws_i[rows]], axis=0).astype(BF16)

        e_r, e_i = expand(pr, 1, 1), expand(pi, 1, 1)
        t_r, t_i = tile(c_r), tile(c_i)
        ca_r = t_r * e_r - t_i * e_i
        ca_n = -(t_r * e_i + t_i * e_r)
        wc_ref[2 * q] = jnp.where(lo, ca_r, pltpu.roll(ca_n, S5_STATE, 1)).astype(BF16)
        wc_ref[2 * q + 1] = jnp.where(lo, pltpu.roll(ca_r, S5_STATE, 1), ca_n).astype(BF16)

        e_r, e_i = expand(pr, 0, 1), expand(pi, 0, 1)
        c0_r = t_r * e_r - t_i * e_i
        c0_i = t_r * e_i + t_i * e_r

        for h in range(2):
            g = 2 * q + h
            mine = (lane16 < S5_STATE) == (h == 0)
            bm_r = jnp.where(mine, bb_r, 0.0)
            bm_i = jnp.where(mine, bb_i, 0.0)
            kt = _dot_nt(bm_r, c0_r, HIGHEST) - _dot_nt(bm_i, c0_i, HIGHEST)
            k_lo = kt[:, :LANES] + jnp.where(row16 == lane16, d_ref[g], 0.0)
            k_hi = kt[:, LANES:]
            blocks = []
            half = LANES // S5_CH
            for s in range(CHUNK):
                sh = (s % half) * S5_CH
                keep = lane16 >= sh
                r_lo = pltpu.roll(k_lo, sh, 1) if sh else k_lo
                r_hi = pltpu.roll(k_hi, sh, 1) if sh else k_hi
                if s < half:
                    blk = jnp.concatenate([jnp.where(keep, r_lo, 0.0),
                                           jnp.where(keep, r_hi, r_lo)], axis=1)
                else:
                    blk = jnp.concatenate([jnp.zeros_like(r_lo), jnp.where(keep, r_lo, 0.0)], axis=1)
                blocks.append(blk)
            wt_ref[g] = _transpose_tiles(jnp.concatenate(blocks, axis=0)).astype(BF16)

            r0 = (2 * q + h) * S5_CH
            c0 = q * 2 * LANES
            wb_ref[r0:r0 + S5_CH, c0:c0 + LANES] = bm_r.astype(BF16)
            wb_ref[r0:r0 + S5_CH, c0 + LANES:c0 + 2 * LANES] = bm_i.astype(BF16)
            wct_ref[q, r0:r0 + S5_CH, :LANES] = jnp.where(mine, c_r, 0.0).astype(BF16)
            wct_ref[q, r0:r0 + S5_CH, LANES:] = jnp.where(mine, -c_i, 0.0).astype(BF16)


def _s5_params(lam_re, lam_im, log_step, b_re, b_im, c_re, c_im, d, cast_w):
    groups = lam_re.shape[0]
    pairs = groups // 2
    slabs = groups // LANE_GROUPS
    cast_blk = pl.BlockSpec((cast_w.shape[0] // slabs, cast_w.shape[1]), lambda i: (i, 0))
    ls = jnp.broadcast_to(log_step[:, None], (groups, S5_STATE))
    lam = jnp.stack([lam_re, lam_im, ls]).reshape(3, pairs, 1, LANES)
    bc = jnp.stack([jnp.transpose(b_re, (0, 2, 1)), jnp.transpose(b_im, (0, 2, 1)), c_re, c_im])
    bc = jnp.transpose(bc.reshape(4, pairs, 2, S5_CH, S5_STATE), (0, 1, 3, 2, 4)).reshape(
        4, pairs, S5_CH, LANES)
    d_pad = jnp.pad(d.reshape(groups, 1, S5_CH), ((0, 0), (0, 0), (0, LANES - S5_CH)))
    blk3 = lambda i: (i, 0, 0)
    blk4 = lambda i: (i, 0, 0, 0)
    pblk = lambda i: (0, i, 0, 0)
    return pl.pallas_call(
        _s5_param_kernel,
        out_shape=(jax.ShapeDtypeStruct((groups, ROWS, ROWS), BF16),
                   jax.ShapeDtypeStruct((groups, 2 * S5_STATE, ROWS), BF16),
                   jax.ShapeDtypeStruct((groups, ROWS, 2 * S5_STATE), BF16),
                   jax.ShapeDtypeStruct((pairs, 2, LANES, LANES), F32),
                   jax.ShapeDtypeStruct((slabs, LANES, LANE_PAIRS * 2 * LANES), BF16),
                   jax.ShapeDtypeStruct((slabs, LANE_PAIRS, LANES, 2 * LANES), BF16),
                   jax.ShapeDtypeStruct((pairs, 2, SUBLANES, LANES), F32),
                   jax.ShapeDtypeStruct(cast_w.shape, BF16)),
        grid=(slabs,),
        in_specs=[pl.BlockSpec((3, LANE_PAIRS, 1, LANES), pblk),
                  pl.BlockSpec((4, LANE_PAIRS, S5_CH, LANES), pblk),
                  pl.BlockSpec((LANE_GROUPS, 1, LANES), blk3), cast_blk],
        out_specs=(pl.BlockSpec((LANE_GROUPS, ROWS, ROWS), blk3),
                   pl.BlockSpec((LANE_GROUPS, 2 * S5_STATE, ROWS), blk3),
                   pl.BlockSpec((LANE_GROUPS, ROWS, 2 * S5_STATE), blk3),
                   pl.BlockSpec((LANE_PAIRS, 2, LANES, LANES), blk4),
                   pl.BlockSpec((None, LANES, LANE_PAIRS * 2 * LANES), blk3),
                   pl.BlockSpec((None, LANE_PAIRS, LANES, 2 * LANES), blk4),
                   pl.BlockSpec((LANE_PAIRS, 2, SUBLANES, LANES), blk4),
                   cast_blk),
        compiler_params=_params("parallel"),
        name="s5_params",
    )(lam, bc, d_pad, cast_w)


def _s5_seq_kernel(z_ref, wt_ref, ws_ref, wc_ref, tab_ref, cast_ref, y_ref, hf_ref, castb_ref,
                   ut_ref, yt_ref, sr_ref, si_ref, *, nb, n_chunks):
    castb_ref[...] = cast_ref[...].astype(BF16)
    scan_steps = n_chunks.bit_length() - 1
    for n in range(nb):
        for s in range(CHUNK):
            xs = z_ref[n, pl.ds(s, n_chunks, stride=CHUNK), :]
            ut_ref[:, s * S5_CH:(s + 1) * S5_CH, n * n_chunks:(n + 1) * n_chunks] = (
                xs.T.astype(BF16).reshape(LANE_GROUPS, S5_CH, n_chunks))

    for g in range(LANE_GROUPS):
        st = _dot(ws_ref[g], ut_ref[g])
        sr_ref[g * S5_STATE:(g + 1) * S5_STATE, :] = st[:S5_STATE]
        si_ref[g * S5_STATE:(g + 1) * S5_STATE, :] = st[S5_STATE:]

    srows = LANE_GROUPS * S5_STATE
    lane = lax.broadcasted_iota(jnp.int32, (srows, n_chunks), 1)
    tab_r = jnp.concatenate([tab_ref[q, 0] for q in range(LANE_PAIRS)], axis=0)
    tab_i = jnp.concatenate([tab_ref[q, 1] for q in range(LANE_PAIRS)], axis=0)
    fin_r = jnp.zeros((srows, n_chunks), F32)
    fin_i = jnp.zeros((srows, n_chunks), F32)
    for n in range(nb):
        cols = slice(n * n_chunks, (n + 1) * n_chunks)
        xr = sr_ref[:, cols]
        xi = si_ref[:, cols]
        for k in range(scan_steps):
            d = 1 << k
            keep = lane >= d
            sr = jnp.where(keep, pltpu.roll(xr, d, 1), 0.0)
            si = jnp.where(keep, pltpu.roll(xi, d, 1), 0.0)
            pr = tab_r[:, SCAN_ROW0 + k:SCAN_ROW0 + k + 1]
            pi = tab_i[:, SCAN_ROW0 + k:SCAN_ROW0 + k + 1]
            xr, xi = xr + (pr * sr - pi * si), xi + (pr * si + pi * sr)
        last = lane == n_chunks - 1
        fin_r = fin_r + jnp.where(
            lane == n, jnp.sum(jnp.where(last, xr, 0.0), axis=1, keepdims=True), 0.0)
        fin_i = fin_i + jnp.where(
            lane == n, jnp.sum(jnp.where(last, xi, 0.0), axis=1, keepdims=True), 0.0)
        keep = lane >= 1
        sr_ref[:, cols] = jnp.where(keep, pltpu.roll(xr, 1, 1), 0.0)
        si_ref[:, cols] = jnp.where(keep, pltpu.roll(xi, 1, 1), 0.0)

    for g in range(LANE_GROUPS):
        rows = slice(g * S5_STATE, (g + 1) * S5_STATE)
        hf_ref[g, 0] = fin_r[rows]
        hf_ref[g, 1] = fin_i[rows]
        hs = jnp.concatenate([sr_ref[rows, :], si_ref[rows, :]], axis=0).astype(BF16)
        yt_ref[g] = jax.nn.gelu(_dot(wc_ref[g], hs) + _dot(wt_ref[g], ut_ref[g]))

    for n in range(nb):
        for s in range(CHUNK):
            blk = yt_ref[:, s * S5_CH:(s + 1) * S5_CH, n * n_chunks:(n + 1) * n_chunks]
            y_ref[n, pl.ds(s, n_chunks, stride=CHUNK), :] = blk.reshape(LANES, n_chunks).T


def _s5_seq(z, w_t, w_s, w_c, tab, d_s5, cast_w):
    nb, t_len, _ = z.shape
    n_chunks = t_len // CHUNK
    assert n_chunks == LANES, "the chunk axis must fill one 128-lane tile"
    groups = d_s5 // S5_CH
    steps = groups // LANE_GROUPS
    cast_blk = pl.BlockSpec((cast_w.shape[0] // steps, cast_w.shape[1]), lambda i: (i, 0))
    kern = functools.partial(_s5_seq_kernel, nb=nb, n_chunks=n_chunks)
    blk3 = lambda i: (i, 0, 0)
    blk4 = lambda i: (i, 0, 0, 0)
    return pl.pallas_call(
        kern,
        out_shape=(jax.ShapeDtypeStruct((nb, t_len, d_s5), F32),
                   jax.ShapeDtypeStruct((groups, 2, S5_STATE, LANES), F32),
                   jax.ShapeDtypeStruct(cast_w.shape, BF16)),
        grid=(steps,),
        in_specs=[
            pl.BlockSpec((nb, t_len, LANES), lambda i: (0, 0, i)),
            pl.BlockSpec((LANE_GROUPS, ROWS, ROWS), blk3),
            pl.BlockSpec((LANE_GROUPS, 2 * S5_STATE, ROWS), blk3),
            pl.BlockSpec((LANE_GROUPS, ROWS, 2 * S5_STATE), blk3),
            pl.BlockSpec((LANE_PAIRS, 2, LANES, LANES), blk4),
            cast_blk,
        ],
        out_specs=(pl.BlockSpec((nb, t_len, LANES), lambda i: (0, 0, i)),
                   pl.BlockSpec((LANE_GROUPS, 2, S5_STATE, LANES), blk4),
                   cast_blk),
        scratch_shapes=[pltpu.VMEM((LANE_GROUPS, ROWS, nb * n_chunks), BF16),
                        pltpu.VMEM((LANE_GROUPS, ROWS, nb * n_chunks), F32),
                        pltpu.VMEM((LANE_GROUPS * S5_STATE, nb * n_chunks), F32),
                        pltpu.VMEM((LANE_GROUPS * S5_STATE, nb * n_chunks), F32)],
        compiler_params=_params("parallel"),
        name="s5_seq",
    )(z, w_t, w_s, w_c, tab, cast_w)


def _s5_step_kernel(u_ref, h0r_ref, h0i_ref, wb_ref, wct_ref, arow_ref, d_ref,
                    y_ref, hr_ref, hi_ref):
    u = u_ref[...]
    bu = _dot(u.astype(BF16), wb_ref[...])
    y = d_ref[...] * u
    for q in range(LANE_PAIRS):
        cols = slice(q * LANES, (q + 1) * LANES)
        ar = arow_ref[q, 0, 1:2, :]
        ai = arow_ref[q, 1, 1:2, :]
        h0r, h0i = h0r_ref[:, cols], h0i_ref[:, cols]
        hr = ar * h0r - ai * h0i + bu[:, 2 * q * LANES:(2 * q + 1) * LANES]
        hi = ar * h0i + ai * h0r + bu[:, (2 * q + 1) * LANES:(2 * q + 2) * LANES]
        hr_ref[:, cols] = hr
        hi_ref[:, cols] = hi
        y = y + _dot_nt(jnp.concatenate([hr, hi], axis=1).astype(BF16), wct_ref[q])
    y_ref[...] = jax.nn.gelu(y).astype(BF16)


def _s5_step(z, h0r, h0i, wb, wct, arow, d):
    n = z.shape[0]
    slabs = wb.shape[0]
    sw = LANE_PAIRS * LANES
    col = lambda i: (0, i)
    blk3 = lambda i: (i, 0, 0)
    blk4 = lambda i: (i, 0, 0, 0)
    return pl.pallas_call(
        _s5_step_kernel,
        out_shape=(jax.ShapeDtypeStruct((n, slabs * LANES), BF16),
                   jax.ShapeDtypeStruct((n, slabs * sw), F32),
                   jax.ShapeDtypeStruct((n, slabs * sw), F32)),
        grid=(slabs,),
        in_specs=[
            pl.BlockSpec((n, LANES), col),
            pl.BlockSpec((n, sw), col),
            pl.BlockSpec((n, sw), col),
            pl.BlockSpec((None, LANES, 2 * sw), blk3),
            pl.BlockSpec((None, LANE_PAIRS, LANES, 2 * LANES), blk4),
            pl.BlockSpec((LANE_PAIRS, 2, SUBLANES, LANES), blk4),
            pl.BlockSpec((1, LANES), col),
        ],
        out_specs=(pl.BlockSpec((n, LANES), col),
                   pl.BlockSpec((n, sw), col),
                   pl.BlockSpec((n, sw), col)),
        compiler_params=_params("parallel"),
        name="s5_step",
    )(z, h0r, h0i, wb, wct, arow, d)


def _lru_gate_block(xc, wa, ba, wi, bi, lam):
    xb16 = xc.astype(BF16)
    r = jax.nn.sigmoid(_dot(xb16, wa) + ba)
    ig = jax.nn.sigmoid(_dot(xb16, wi) + bi)
    log_a = -LRU_C * r * jax.nn.softplus(-lam)
    a = jnp.exp(log_a)
    mult = jnp.sqrt(1.0 - a * a)
    return a, mult * (ig * xc)


def _lru_gates(xc, wa_ref, ba, wi_ref, bi, lam):
    blk = xc.shape[1] // LRU_HEADS
    parts = [_lru_gate_block(xc[:, h * blk:(h + 1) * blk], wa_ref[h], ba[:, h * blk:(h + 1) * blk],
                             wi_ref[h], bi[:, h * blk:(h + 1) * blk],
                             lam[:, h * blk:(h + 1) * blk]) for h in range(LRU_HEADS)]
    return (jnp.concatenate([p[0] for p in parts], axis=-1),
            jnp.concatenate([p[1] for p in parts], axis=-1))


def _lru_seq_kernel(xb_ref, gb_ref, cw_ref, cb_ref, wa_ref, ba_ref, wi_ref, bi_ref, lam_ref,
                    gm_ref, cast_ref, o_ref, hl_ref, castb_ref, xe_ref, a_ref, b_ref, hc_ref,
                    *, tt):
    castb_ref[...] = cast_ref[...].astype(BF16)
    halo = SUBLANES
    nseq, _, d = xb_ref.shape
    blk = d // LRU_HEADS

    @pl.when(pl.program_id(0) == 0)
    def _():
        xe_ref[:, 0:halo, :] = jnp.zeros((nseq, halo, d), F32)
        hc_ref[...] = jnp.zeros_like(hc_ref)

    xe_ref[:, halo:halo + tt, :] = xb_ref[...]
    for h in range(LRU_HEADS):
        cols = slice(h * blk, (h + 1) * blk)
        cw = cw_ref[:, cols]
        xc = cb_ref[:, cols] + xe_ref[:, halo:halo + tt, cols] * cw[CONV_W - 1:CONV_W, :]
        for k in range(1, CONV_W):
            xc = xc + xe_ref[:, halo - k:halo - k + tt, cols] * cw[CONV_W - 1 - k:CONV_W - k, :]
        a, b = _lru_gate_block(xc.reshape(nseq * tt, blk), wa_ref[h], ba_ref[:, cols],
                               wi_ref[h], bi_ref[:, cols], lam_ref[:, cols])
        a_ref[:, :, cols] = a.reshape(nseq, tt, blk)
        b_ref[:, :, cols] = b.reshape(nseq, tt, blk)
    xe_ref[:, 0:halo, :] = xb_ref[:, tt - halo:tt, :]

    def block(i, h):
        base = pl.multiple_of(i * SUBLANES, SUBLANES)
        for j in range(SUBLANES):
            row = pl.ds(base + j, 1)
            h = a_ref[:, row, :] * h + b_ref[:, row, :]
            b_ref[:, row, :] = h
        return h

    h = lax.fori_loop(0, tt // SUBLANES, block, hc_ref[...])
    hc_ref[...] = h
    hl_ref[...] = h

    out = b_ref[...] * jax.nn.gelu(gb_ref[...])
    o_ref[...] = _rms(out, gm_ref[...]).astype(BF16)


def _lru_seq(z, conv_w, conv_b, w_a, b_a, w_i, b_i, lam, g_merge, cast_w, tt):
    nseq, t_len, _ = z.shape
    d = conv_w.shape[1]
    steps = t_len // tt
    cast_blk = pl.BlockSpec((cast_w.shape[0] // steps, cast_w.shape[1]), lambda t: (t, 0))
    row = lambda v: v.reshape(1, d)
    const2 = lambda t: (0, 0)
    const3 = lambda t: (0, 0, 0)
    kern = functools.partial(_lru_seq_kernel, tt=tt)
    return pl.pallas_call(
        kern,
        out_shape=(jax.ShapeDtypeStruct((nseq, t_len, d), BF16),
                   jax.ShapeDtypeStruct((nseq, 1, d), F32),
                   jax.ShapeDtypeStruct(cast_w.shape, BF16)),
        grid=(steps,),
        in_specs=[
            pl.BlockSpec((nseq, tt, d), lambda t: (0, t, 1)),
            pl.BlockSpec((nseq, tt, d), lambda t: (0, t, 2)),
            pl.BlockSpec((CONV_W, d), const2),
            pl.BlockSpec((1, d), const2),
            pl.BlockSpec(w_a.shape, const3),
            pl.BlockSpec((1, d), const2),
            pl.BlockSpec(w_i.shape, const3),
            pl.BlockSpec((1, d), const2),
            pl.BlockSpec((1, d), const2),
            pl.BlockSpec((1, d), const2),
            cast_blk,
        ],
        out_specs=(pl.BlockSpec((nseq, tt, d), lambda t: (0, t, 0)),
                   pl.BlockSpec((nseq, 1, d), const3),
                   cast_blk),
        scratch_shapes=[pltpu.VMEM((nseq, tt + SUBLANES, d), F32),
                        pltpu.VMEM((nseq, tt, d), F32),
                        pltpu.VMEM((nseq, tt, d), F32),
                        pltpu.VMEM((nseq, 1, d), F32)],
        compiler_params=_params("arbitrary"),
        name="rglru_seq",
    )(z, z, conv_w, row(conv_b), w_a, row(b_a), w_i, row(b_i), row(lam), row(g_merge), cast_w)


def _lru_step_kernel(xb_ref, gb_ref, c0_ref, c1_ref, c2_ref, h0_ref, cw_ref, cb_ref,
                     wa_ref, ba_ref, wi_ref, bi_ref, lam_ref, gm_ref, o_ref, h_ref, cv_ref):
    d = xb_ref.shape[1]
    xb = xb_ref[...]
    cw = cw_ref[...]
    xc = (cb_ref[...] + c0_ref[...] * cw[0:1, :] + c1_ref[...] * cw[1:2, :]
          + c2_ref[...] * cw[2:3, :] + xb * cw[3:4, :])
    a, b = _lru_gates(xc, wa_ref, ba_ref[...], wi_ref, bi_ref[...], lam_ref[...])
    h = a * h0_ref[...] + b
    h_ref[...] = h
    out = h * jax.nn.gelu(gb_ref[...])
    o_ref[...] = _rms(out, gm_ref[...]).astype(BF16)
    cv_ref[:, 0:d] = c1_ref[...]
    cv_ref[:, d:2 * d] = c2_ref[...]
    cv_ref[:, 2 * d:3 * d] = xb


def _lru_step(z, conv0, h0, conv_w, conv_b, w_a, b_a, w_i, b_i, lam, g_merge):
    n = z.shape[0]
    d = conv_w.shape[1]
    taps = CONV_W - 1
    conv2d = conv0.reshape(n, taps * d)
    row = lambda v: v.reshape(1, d)
    full = lambda shape: pl.BlockSpec(shape, lambda i: (0,) * len(shape))
    col = lambda c: pl.BlockSpec((n, d), lambda i: (0, c))
    nd = full((n, d))
    rd = full((1, d))
    o, h, cv = pl.pallas_call(
        _lru_step_kernel,
        out_shape=(jax.ShapeDtypeStruct((n, d), BF16), jax.ShapeDtypeStruct((n, d), F32),
                   jax.ShapeDtypeStruct((n, taps * d), F32)),
        grid=(1,),
        in_specs=[
            col(1), col(2), col(0), col(1), col(2), nd,
            full((CONV_W, d)), rd,
            full(w_a.shape), rd, full(w_i.shape), rd, rd, rd,
        ],
        out_specs=(nd, nd, full((n, taps * d))),
        compiler_params=_params("arbitrary"),
        name="rglru_step",
    )(z, z, conv2d, conv2d, conv2d, h0, conv_w, row(conv_b),
      w_a, row(b_a), w_i, row(b_i), row(lam), row(g_merge))
    return o, h, cv.reshape(n, taps, d)


def _glu_rows(y, w_ref, b_ref, g_ref):
    gate = jax.nn.sigmoid(_dot(y.astype(BF16), w_ref[...]) + b_ref[...])
    return _rms(y.astype(F32) * gate, g_ref[...]).astype(BF16)


def _glu_kernel(y_ref, ys_ref, w_ref, b_ref, g_ref, o_ref, os_ref):
    for rows in _row_parts(y_ref.shape[0], 4):
        o_ref[rows, :] = _glu_rows(y_ref[rows, :], w_ref, b_ref, g_ref)

    @_on_last_row_tile
    def _():
        os_ref[...] = _glu_rows(ys_ref[...], w_ref, b_ref, g_ref)


def _glu(y, ys, w, b, g, tm):
    m, d = y.shape
    ms = ys.shape[0]
    const = lambda i: (0, 0)
    return pl.pallas_call(
        _glu_kernel,
        out_shape=(jax.ShapeDtypeStruct((m, d), BF16), jax.ShapeDtypeStruct((ms, d), BF16)),
        grid=(m // tm,),
        in_specs=[
            pl.BlockSpec((tm, d), lambda i: (i, 0)),
            pl.BlockSpec((ms, d), const),
            pl.BlockSpec((d, d), const),
            pl.BlockSpec((1, d), const),
            pl.BlockSpec((1, d), const),
        ],
        out_specs=(pl.BlockSpec((tm, d), lambda i: (i, 0)), pl.BlockSpec((ms, d), const)),
        compiler_params=_params("arbitrary"),
        name="s5_glu",
    )(y, ys, w, b.reshape(1, d), g.reshape(1, d))


def _out_proj_kernel(x_ref, ma_ref, mb_ref, xs_ref, mas_ref, mbs_ref, w_ref, o_ref, os_ref, wb_ref):
    @pl.when(pl.program_id(0) == 0)
    def _():
        wb_ref[...] = w_ref[...].astype(BF16)

    ka = ma_ref.shape[1]
    proj = lambda x, ma, mb: x + _dot(ma, wb_ref[:ka, :]) + _dot(mb, wb_ref[ka:, :])
    for rows in _row_parts(x_ref.shape[0], 2):
        o_ref[rows, :] = proj(x_ref[rows, :], ma_ref[rows, :], mb_ref[rows, :])

    @_on_last_row_tile
    def _():
        os_ref[...] = proj(xs_ref[...], mas_ref[...], mbs_ref[...])


def _out_proj(x, ma, mb, xs, mas, mbs, w, tm):
    m, n = x.shape
    ms = xs.shape[0]
    ka = ma.shape[1]
    kb = mb.shape[1]
    const = lambda i: (0, 0)
    tile = lambda i: (i, 0)
    return pl.pallas_call(
        _out_proj_kernel,
        out_shape=(jax.ShapeDtypeStruct((m, n), F32), jax.ShapeDtypeStruct((ms, n), F32)),
        grid=(m // tm,),
        in_specs=[
            pl.BlockSpec((tm, n), tile),
            pl.BlockSpec((tm, ka), tile),
            pl.BlockSpec((tm, kb), tile),
            pl.BlockSpec((ms, n), const),
            pl.BlockSpec((ms, ka), const),
            pl.BlockSpec((ms, kb), const),
            pl.BlockSpec((ka + kb, n), const, pipeline_mode=pl.Buffered(1)),
        ],
        out_specs=(pl.BlockSpec((tm, n), tile), pl.BlockSpec((ms, n), const)),
        scratch_shapes=[pltpu.VMEM((ka + kb, n), BF16)],
        compiler_params=_params("arbitrary"),
        name="out_proj",
    )(x, ma, mb, xs, mas, mbs, w)


def _mlp_kernel(x_ref, xs_ref, g_ref, wu_ref, wd_ref, c0_ref, c1_ref, o_ref, os_ref, c0b_ref, c1b_ref,
                h_ref, hs_ref):
    @pl.when(pl.program_id(1) == 0)
    def _():
        c0b_ref[...] = c0_ref[...].astype(BF16)
        c1b_ref[...] = c1_ref[...].astype(BF16)

    def ffn(h):
        act = jnp.square(jnp.maximum(_dot(h, wu_ref[...]), 0.0)).astype(BF16)
        return _dot(act, wd_ref[...])

    def rows(x_ref, o_ref, h_ref):
        first = pl.program_id(1) == 0

        @pl.when(first)
        def _():
            for part in _row_parts(x_ref.shape[0], 4):
                x = x_ref[part, :]
                h = _rms(x, g_ref[...]).astype(BF16)
                h_ref[part, :] = h
                o_ref[part, :] = x + ffn(h)

        @pl.when(jnp.logical_not(first))
        def _():
            o_ref[...] += ffn(h_ref[...])

    rows(x_ref, o_ref, h_ref)
    _on_last_row_tile(lambda: rows(xs_ref, os_ref, hs_ref))


def _mlp(x, xs, g, w_up, w_down, cast_ws, tm, tf):
    m, d = x.shape
    ms = xs.shape[0]
    f = w_up.shape[1]
    ni = m // tm
    const = lambda i, j: (0, 0)
    cast_blks = [pl.BlockSpec((c.shape[0] // ni, c.shape[1]), lambda i, j: (i, 0)) for c in cast_ws]
    outs = pl.pallas_call(
        _mlp_kernel,
        out_shape=(jax.ShapeDtypeStruct((m, d), F32), jax.ShapeDtypeStruct((ms, d), F32),
                   *(jax.ShapeDtypeStruct(c.shape, BF16) for c in cast_ws)),
        grid=(ni, f // tf),
        in_specs=[
            pl.BlockSpec((tm, d), lambda i, j: (i, 0)),
            pl.BlockSpec((ms, d), const),
            pl.BlockSpec((1, d), const),
            pl.BlockSpec((d, tf), lambda i, j: (0, j)),
            pl.BlockSpec((tf, d), lambda i, j: (j, 0)),
            *cast_blks,
        ],
        out_specs=(pl.BlockSpec((tm, d), lambda i, j: (i, 0)), pl.BlockSpec((ms, d), const),
                   *cast_blks),
        scratch_shapes=[pltpu.VMEM((tm, d), BF16), pltpu.VMEM((ms, d), BF16)],
        compiler_params=_params("arbitrary", "arbitrary"),
        name="mlp",
    )(x, xs, g.reshape(1, d), w_up, w_down, *cast_ws)
    return outs[0], outs[1], outs[2:]


def _ple_kernel(x_ref, p_ref, xs_ref, ps_ref, g_ref, wg_ref, wp_ref, gf_ref, o_ref, os_ref,
                *, final):
    def ple(x, p):
        gate = jax.nn.sigmoid(_dot(_rms(x, g_ref[...]).astype(BF16), wg_ref[...]))
        pe = _dot(p.astype(BF16), wp_ref[...])
        x = x + pe * gate
        return _rms(x, gf_ref[...]) if final else x

    for rows in _row_parts(x_ref.shape[0], 2):
        o_ref[rows, :] = ple(x_ref[rows, :], p_ref[rows, :])

    @_on_last_row_tile
    def _():
        os_ref[...] = ple(xs_ref[...], ps_ref[...])


def _ple(x, p, xs, ps, g, w_gate, w_ple, g_final, tm, final):
    m, d = x.shape
    ms = xs.shape[0]
    dp = p.shape[1]
    once = pl.Buffered(1)
    const = lambda i: (0, 0)
    tile = lambda i: (i, 0)
    return pl.pallas_call(
        functools.partial(_ple_kernel, final=final),
        out_shape=(jax.ShapeDtypeStruct((m, d), F32), jax.ShapeDtypeStruct((ms, d), F32)),
        grid=(m // tm,),
        in_specs=[
            pl.BlockSpec((tm, d), tile),
            pl.BlockSpec((tm, dp), tile),
            pl.BlockSpec((ms, d), const),
            pl.BlockSpec((ms, dp), const),
            pl.BlockSpec((1, d), const),
            pl.BlockSpec((d, d), const, pipeline_mode=once),
            pl.BlockSpec((dp, d), const, pipeline_mode=once),
            pl.BlockSpec((1, d), const),
        ],
        out_specs=(pl.BlockSpec((tm, d), tile), pl.BlockSpec((ms, d), const)),
        compiler_params=_params("arbitrary"),
        name="ple_final",
    )(x, p, xs, ps, g.reshape(1, d), w_gate, w_ple, g_final.reshape(1, d))


TT_LRU = 64
TM_IN, TN_IN = 1024, 1024
TM_GLU = 1024
TM_OUT = 512
TM_MLP, TF_MLP = 512, 1024
TM_PLE = 512


def kernel(x_prompt, x_sample, state_s5_re, state_s5_im, state_lru, state_conv, p_prompt, p_sample,
           g_mix, w_in, s5_lam_re, s5_lam_im, s5_log_step, s5_b_re, s5_b_im, s5_c_re, s5_c_im, s5_d,
           s5_w_glu, s5_b_glu, conv_w, conv_b, lru_w_a, lru_b_a, lru_w_i, lru_b_i, lru_lam,
           g_merge_a, g_merge_b, w_out, g_mlp, w_up, w_down, g_ple, w_ple_gate, w_ple, g_final):
    depth = g_mix.shape[0]
    nb, t_len, d_model = x_prompt.shape
    ns = x_sample.shape[0]
    d_s5 = s5_d.shape[1]
    d_lru = conv_w.shape[2]
    groups = d_s5 // S5_CH

    xp = x_prompt.reshape(nb * t_len, d_model)
    xs = x_sample.reshape(ns, d_model)
    outs = [[] for _ in range(8)]
    for l in range(depth):
        final = l == depth - 1
        w_glu_b = s5_w_glu[l].astype(BF16)
        lru_args = (conv_w[l], conv_b[l], lru_w_a[l].astype(BF16), lru_b_a[l],
                    lru_w_i[l].astype(BF16), lru_b_i[l], lru_lam[l], g_merge_b[l])
        w_t, w_s, w_c, tab, wb, wct, arow, w_in_b = _s5_params(
            s5_lam_re[l], s5_lam_im[l], s5_log_step[l], s5_b_re[l], s5_b_im[l],
            s5_c_re[l], s5_c_im[l], s5_d[l], w_in[l])

        z, zs = _norm_matmul(xp, xs, g_mix[l], w_in_b, TM_IN, TN_IN)
        z4 = z.reshape(nb, t_len, 3 * d_s5)
        y, hf, w_down_b = _s5_seq(z4, w_t, w_s, w_c, tab, d_s5, w_down[l])
        mb, lru_h, w_up_b = _lru_seq(z4, *lru_args, w_up[l], TT_LRU)
        ys, hsr, hsi = _s5_step(zs, state_s5_re[l].reshape(ns, groups * S5_STATE),
                                state_s5_im[l].reshape(ns, groups * S5_STATE),
                                wb, wct, arow, s5_d[l].reshape(1, d_s5))
        mbs, lru_hs, conv_s = _lru_step(zs, state_conv[l], state_lru[l], *lru_args)
        ma, mas = _glu(y.reshape(nb * t_len, d_s5), ys, w_glu_b, s5_b_glu[l], g_merge_a[l], TM_GLU)
        xp, xs = _out_proj(xp, ma, mb.reshape(nb * t_len, d_lru), xs, mas, mbs, w_out[l], TM_OUT)
        xp, xs, (w_gate_b, w_ple_b) = _mlp(xp, xs, g_mlp[l], w_up_b, w_down_b,
                                           (w_ple_gate[l], w_ple[l]), TM_MLP, TF_MLP)
        xp, xs = _ple(xp, p_prompt[l].reshape(nb * t_len, -1), xs, p_sample[l].reshape(ns, -1),
                      g_ple[l], w_gate_b, w_ple_b, g_final, TM_PLE, final)
        hf = jnp.transpose(hf[:, :, :, :nb], (1, 3, 0, 2))
        outs[0].append(hf[0])
        outs[1].append(hf[1])
        outs[2].append(lru_h.reshape(nb, d_lru))
        outs[3].append(z4[:, t_len - (CONV_W - 1):, d_s5:d_s5 + d_lru])
        outs[4].append(hsr.reshape(ns, groups, S5_STATE))
        outs[5].append(hsi.reshape(ns, groups, S5_STATE))
        outs[6].append(lru_hs)
        outs[7].append(conv_s)
    return (xp.reshape(nb, t_len, d_model), xs.reshape(ns, 1, d_model),
            *(jnp.stack(o) for o in outs))
```

```python
import functools

import jax
import jax.numpy as jnp
from jax import lax
from jax.experimental import pallas as pl
from jax.experimental.pallas import tpu as pltpu

F32 = jnp.float32
BF16 = jnp.bfloat16
HIGHEST = lax.Precision.HIGHEST

EPS = 1e-6
LRU_C = 8.0
S5_CH = 16
S5_STATE = 64
LRU_HEADS = 4
CONV_W = 4
CHUNK = 16
SUBLANES = 8
LANES = 128
LANE_GROUPS = LANES // S5_CH
LANE_PAIRS = LANE_GROUPS // 2
ROWS = CHUNK * S5_CH
POW_ROWS = 32
SCAN_ROW0 = POW_ROWS
VMEM_LIMIT = 56 * 1024 * 1024


def _params(*sem):
    return pltpu.CompilerParams(dimension_semantics=sem, vmem_limit_bytes=VMEM_LIMIT)


def _rms(x, g):
    return x * lax.rsqrt(jnp.mean(x * x, axis=-1, keepdims=True) + EPS) * g


def _dot(a, b):
    return jnp.dot(a, b, preferred_element_type=F32)


def _dot_nt(a, b, precision=None):
    return lax.dot_general(a, b, (((1,), (1,)), ((), ())), precision=precision,
                           preferred_element_type=F32)


def _transpose_tiles(x):
    r, c = x.shape
    return jnp.concatenate(
        [jnp.concatenate([x[i:i + LANES, j:j + LANES].T for i in range(0, r, LANES)], axis=1)
         for j in range(0, c, LANES)], axis=0)


def _row_parts(tm, want):
    parts = max(1, min(want, tm // LANES))
    step = tm // parts
    return [slice(r * step, (r + 1) * step) for r in range(parts)]


def _on_last_row_tile(fn):
    pl.when(pl.program_id(0) == pl.num_programs(0) - 1)(fn)


def _norm_matmul_kernel(x_ref, xs_ref, g_ref, w_ref, o_ref, os_ref):
    tn = o_ref.shape[1]
    w = w_ref[:, pl.ds(pl.multiple_of(pl.program_id(1) * tn, tn), tn)]
    for rows in _row_parts(x_ref.shape[0], 8):
        h = _rms(x_ref[rows, :], g_ref[...]).astype(BF16)
        o_ref[rows, :] = _dot(h, w)

    @_on_last_row_tile
    def _():
        os_ref[...] = _dot(_rms(xs_ref[...], g_ref[...]).astype(BF16), w)


def _norm_matmul(x, xs, g, w, tm, tn):
    m, k = x.shape
    ms = xs.shape[0]
    n = w.shape[1]
    ni = m // tm
    return pl.pallas_call(
        _norm_matmul_kernel,
        out_shape=(jax.ShapeDtypeStruct((m, n), F32), jax.ShapeDtypeStruct((ms, n), F32)),
        grid=(ni, n // tn),
        in_specs=[
            pl.BlockSpec((tm, k), lambda i, j: (i, 0)),
            pl.BlockSpec((ms, k), lambda i, j: (0, 0)),
            pl.BlockSpec((1, k), lambda i, j: (0, 0)),
            pl.BlockSpec((k, n), lambda i, j: (0, 0), pipeline_mode=pl.Buffered(1)),
        ],
        out_specs=(pl.BlockSpec((tm, tn), lambda i, j: (i, j)),
                   pl.BlockSpec((ms, tn), lambda i, j: (0, jnp.where(i == ni - 1, j, 0)))),
        compiler_params=_params("arbitrary", "arbitrary"),
        name="norm_w_in",
    )(x, xs, g.reshape(1, k), w)


def _s5_param_kernel(lam_ref, bc_ref, d_ref, cast_ref,
                     wt_ref, ws_ref, wc_ref, tab_ref, wb_ref, wct_ref, arow_ref, castb_ref):
    castb_ref[...] = cast_ref[...].astype(BF16)
    row_p = lax.broadcasted_iota(jnp.int32, (POW_ROWS, LANES), 0)
    lane = lax.broadcasted_iota(jnp.int32, (ROWS, LANES), 1)
    lane16 = lax.broadcasted_iota(jnp.int32, (S5_CH, LANES), 1)
    row16 = lax.broadcasted_iota(jnp.int32, (S5_CH, LANES), 0)
    lo = lane < S5_STATE

    wb_ref[...] = jnp.zeros_like(wb_ref)
    wct_ref[...] = jnp.zeros_like(wct_ref)

    for q in range(LANE_PAIRS):
        lr, li = lam_ref[0, q], lam_ref[1, q]
        step = jnp.exp(lam_ref[2, q])
        mag = jnp.exp(lr * step)
        ar = mag * jnp.cos(li * step)
        ai = mag * jnp.sin(li * step)
        nr, ni = ar - 1.0, ai
        den = lr * lr + li * li
        cr = (nr * lr + ni * li) / den
        ci = (ni * lr - nr * li) / den
        bt_r, bt_i = bc_ref[0, q], bc_ref[1, q]
        bb_r = cr * bt_r - ci * bt_i
        bb_i = cr * bt_i + ci * bt_r
        c_r, c_i = bc_ref[2, q], bc_ref[3, q]

        pr = jnp.ones((POW_ROWS, LANES), F32)
        pi = jnp.zeros((POW_ROWS, LANES), F32)
        sr, si = ar, ai
        sq = []
        for m in range(POW_ROWS.bit_length() - 1 + SUBLANES):
            sq.append((sr, si))
            if (1 << m) < POW_ROWS:
                bit = ((row_p >> m) & 1) == 1
                pr, pi = (jnp.where(bit, pr * sr - pi * si, pr),
                          jnp.where(bit, pr * si + pi * sr, pi))
            sr, si = sr * sr - si * si, 2.0 * sr * si
        m0 = CHUNK.bit_length() - 1
        scan_r = jnp.concatenate([sq[m0 + j][0] for j in range(SUBLANES)], axis=0)
        scan_i = jnp.concatenate([sq[m0 + j][1] for j in range(SUBLANES)], axis=0)
        pad = jnp.zeros((LANES - POW_ROWS - SUBLANES, LANES), F32)
        tab_ref[q, 0] = jnp.concatenate([pr, scan_r, pad], axis=0).T
        tab_ref[q, 1] = jnp.concatenate([pi, scan_i, pad], axis=0).T
        arow_ref[q, 0] = pr[:SUBLANES]
        arow_ref[q, 1] = pi[:SUBLANES]

        def expand(tab, k0, sign):
            return jnp.concatenate(
                [jnp.broadcast_to(tab[k0 + sign * s:k0 + sign * s + 1, :], (S5_CH, LANES))
                 for s in range(CHUNK)], axis=0)

        tile = lambda v: jnp.concatenate([v] * CHUNK, axis=0)

        e_r, e_i = expand(pr, CHUNK - 1, -1), expand(pi, CHUNK - 1, -1)
        t_r, t_i = tile(bb_r), tile(bb_i)
        ws_r = _transpose_tiles(t_r * e_r - t_i * e_i)
        ws_i = _transpose_tiles(t_r * e_i + t_i * e_r)
        for h in range(2):
            rows = slice(h * S5_STATE, (h + 1) * S5_STATE)
            ws_ref[2 * q + h] = jnp.concatenate([ws_r[rows], ws_i[rows]], axis=0).astype(BF16)

        e_r, e_i = expand(pr, 1, 1), expand(pi, 1, 1)
        t_r, t_i = tile(c_r), tile(c_i)
        ca_r = t_r * e_r - t_i * e_i
        ca_n = -(t_r * e_i + t_i * e_r)
        wc_ref[2 * q] = jnp.where(lo, ca_r, pltpu.roll(ca_n, S5_STATE, 1)).astype(BF16)
        wc_ref[2 * q + 1] = jnp.where(lo, pltpu.roll(ca_r, S5_STATE, 1), ca_n).astype(BF16)

        e_r, e_i = expand(pr, 0, 1), expand(pi, 0, 1)
        c0_r = t_r * e_r - t_i * e_i
        c0_i = t_r * e_i + t_i * e_r

        for h in range(2):
            g = 2 * q + h
            mine = (lane16 < S5_STATE) == (h == 0)
            bm_r = jnp.where(mine, bb_r, 0.0)
            bm_i = jnp.where(mine, bb_i, 0.0)
            kt = _dot_nt(bm_r, c0_r, HIGHEST) - _dot_nt(bm_i, c0_i, HIGHEST)
            k_lo = kt[:, :LANES] + jnp.where(row16 == lane16, d_ref[g], 0.0)
            k_hi = kt[:, LANES:]
            blocks = []
            half = LANES // S5_CH
            for s in range(CHUNK):
                sh = (s % half) * S5_CH
                keep = lane16 >= sh
                r_lo = pltpu.roll(k_lo, sh, 1) if sh else k_lo
                r_hi = pltpu.roll(k_hi, sh, 1) if sh else k_hi
                if s < half:
                    blk = jnp.concatenate([jnp.where(keep, r_lo, 0.0),
                                           jnp.where(keep, r_hi, r_lo)], axis=1)
                else:
                    blk = jnp.concatenate([jnp.zeros_like(r_lo), jnp.where(keep, r_lo, 0.0)], axis=1)
                blocks.append(blk)
            wt_ref[g] = _transpose_tiles(jnp.concatenate(blocks, axis=0)).astype(BF16)

            r0 = (2 * q + h) * S5_CH
            c0 = q * 2 * LANES
            wb_ref[r0:r0 + S5_CH, c0:c0 + LANES] = bm_r.astype(BF16)
            wb_ref[r0:r0 + S5_CH, c0 + LANES:c0 + 2 * LANES] = bm_i.astype(BF16)
            wct_ref[q, r0:r0 + S5_CH, :LANES] = jnp.where(mine, c_r, 0.0).astype(BF16)
            wct_ref[q, r0:r0 + S5_CH, LANES:] = jnp.where(mine, -c_i, 0.0).astype(BF16)


def _s5_params(lam_re, lam_im, log_step, b_re, b_im, c_re, c_im, d, cast_w):
    groups = lam_re.shape[0]
    pairs = groups // 2
    slabs = groups // LANE_GROUPS
    cast_blk = pl.BlockSpec((cast_w.shape[0] // slabs, cast_w.shape[1]), lambda i: (i, 0))
    ls = jnp.broadcast_to(log_step[:, None], (groups, S5_STATE))
    lam = jnp.stack([lam_re, lam_im, ls]).reshape(3, pairs, 1, LANES)
    bc = jnp.stack([jnp.transpose(b_re, (0, 2, 1)), jnp.transpose(b_im, (0, 2, 1)), c_re, c_im])
    bc = jnp.transpose(bc.reshape(4, pairs, 2, S5_CH, S5_STATE), (0, 1, 3, 2, 4)).reshape(
        4, pairs, S5_CH, LANES)
    d_pad = jnp.pad(d.reshape(groups, 1, S5_CH), ((0, 0), (0, 0), (0, LANES - S5_CH)))
    blk3 = lambda i: (i, 0, 0)
    blk4 = lambda i: (i, 0, 0, 0)
    pblk = lambda i: (0, i, 0, 0)
    return pl.pallas_call(
        _s5_param_kernel,
        out_shape=(jax.ShapeDtypeStruct((groups, ROWS, ROWS), BF16),
                   jax.ShapeDtypeStruct((groups, 2 * S5_STATE, ROWS), BF16),
                   jax.ShapeDtypeStruct((groups, ROWS, 2 * S5_STATE), BF16),
                   jax.ShapeDtypeStruct((pairs, 2, LANES, LANES), F32),
                   jax.ShapeDtypeStruct((slabs, LANES, LANE_PAIRS * 2 * LANES), BF16),
                   jax.ShapeDtypeStruct((slabs, LANE_PAIRS, LANES, 2 * LANES), BF16),
                   jax.ShapeDtypeStruct((pairs, 2, SUBLANES, LANES), F32),
                   jax.ShapeDtypeStruct(cast_w.shape, BF16)),
        grid=(slabs,),
        in_specs=[pl.BlockSpec((3, LANE_PAIRS, 1, LANES), pblk),
                  pl.BlockSpec((4, LANE_PAIRS, S5_CH, LANES), pblk),
                  pl.BlockSpec((LANE_GROUPS, 1, LANES), blk3), cast_blk],
        out_specs=(pl.BlockSpec((LANE_GROUPS, ROWS, ROWS), blk3),
                   pl.BlockSpec((LANE_GROUPS, 2 * S5_STATE, ROWS), blk3),
                   pl.BlockSpec((LANE_GROUPS, ROWS, 2 * S5_STATE), blk3),
                   pl.BlockSpec((LANE_PAIRS, 2, LANES, LANES), blk4),
                   pl.BlockSpec((None, LANES, LANE_PAIRS * 2 * LANES), blk3),
                   pl.BlockSpec((None, LANE_PAIRS, LANES, 2 * LANES), blk4),
                   pl.BlockSpec((LANE_PAIRS, 2, SUBLANES, LANES), blk4),
                   cast_blk),
        compiler_params=_params("parallel"),
        name="s5_params",
    )(lam, bc, d_pad, cast_w)


def _s5_seq_kernel(z_ref, wt_ref, ws_ref, wc_ref, tab_ref, cast_ref, y_ref, hf_ref, castb_ref,
                   ut_ref, yt_ref, sr_ref, si_ref, *, nb, n_chunks):
    castb_ref[...] = cast_ref[...].astype(BF16)
    scan_steps = n_chunks.bit_length() - 1
    for n in range(nb):
        for s in range(CHUNK):
            xs = z_ref[n, pl.ds(s, n_chunks, stride=CHUNK), :]
            ut_ref[:, s * S5_CH:(s + 1) * S5_CH, n * n_chunks:(n + 1) * n_chunks] = (
                xs.T.astype(BF16).reshape(LANE_GROUPS, S5_CH, n_chunks))

    for g in range(LANE_GROUPS):
        st = _dot(ws_ref[g], ut_ref[g])
        sr_ref[g * S5_STATE:(g + 1) * S5_STATE, :] = st[:S5_STATE]
        si_ref[g * S5_STATE:(g + 1) * S5_STATE, :] = st[S5_STATE:]

    srows = LANE_GROUPS * S5_STATE
    lane = lax.broadcasted_iota(jnp.int32, (srows, n_chunks), 1)
    tab_r = jnp.concatenate([tab_ref[q, 0] for q in range(LANE_PAIRS)], axis=0)
    tab_i = jnp.concatenate([tab_ref[q, 1] for q in range(LANE_PAIRS)], axis=0)
    fin_r = jnp.zeros((srows, n_chunks), F32)
    fin_i = jnp.zeros((srows, n_chunks), F32)
    for n in range(nb):
        cols = slice(n * n_chunks, (n + 1) * n_chunks)
        xr = sr_ref[:, cols]
        xi = si_ref[:, cols]
        for k in range(scan_steps):
            d = 1 << k
            keep = lane >= d
            sr = jnp.where(keep, pltpu.roll(xr, d, 1), 0.0)
            si = jnp.where(keep, pltpu.roll(xi, d, 1), 0.0)
            pr = tab_r[:, SCAN_ROW0 + k:SCAN_ROW0 + k + 1]
            pi = tab_i[:, SCAN_ROW0 + k:SCAN_ROW0 + k + 1]
            xr, xi = xr + (pr * sr - pi * si), xi + (pr * si + pi * sr)
        last = lane == n_chunks - 1
        fin_r = fin_r + jnp.where(
            lane == n, jnp.sum(jnp.where(last, xr, 0.0), axis=1, keepdims=True), 0.0)
        fin_i = fin_i + jnp.where(
            lane == n, jnp.sum(jnp.where(last, xi, 0.0), axis=1, keepdims=True), 0.0)
        keep = lane >= 1
        sr_ref[:, cols] = jnp.where(keep, pltpu.roll(xr, 1, 1), 0.0)
        si_ref[:, cols] = jnp.where(keep, pltpu.roll(xi, 1, 1), 0.0)

    for g in range(LANE_GROUPS):
        rows = slice(g * S5_STATE, (g + 1) * S5_STATE)
        hf_ref[g, 0] = fin_r[rows]
        hf_ref[g, 1] = fin_i[rows]
        hs = jnp.concatenate([sr_ref[rows, :], si_ref[rows, :]], axis=0).astype(BF16)
        yt_ref[g] = jax.nn.gelu(_dot(wc_ref[g], hs) + _dot(wt_ref[g], ut_ref[g]))

    for n in range(nb):
        for s in range(CHUNK):
            blk = yt_ref[:, s * S5_CH:(s + 1) * S5_CH, n * n_chunks:(n + 1) * n_chunks]
            y_ref[n, pl.ds(s, n_chunks, stride=CHUNK), :] = blk.reshape(LANES, n_chunks).T


def _s5_seq(z, w_t, w_s, w_c, tab, d_s5, cast_w):
    nb, t_len, _ = z.shape
    n_chunks = t_len // CHUNK
    assert n_chunks == LANES, "the chunk axis must fill one 128-lane tile"
    groups = d_s5 // S5_CH
    steps = groups // LANE_GROUPS
    cast_blk = pl.BlockSpec((cast_w.shape[0] // steps, cast_w.shape[1]), lambda i: (i, 0))
    kern = functools.partial(_s5_seq_kernel, nb=nb, n_chunks=n_chunks)
    blk3 = lambda i: (i, 0, 0)
    blk4 = lambda i: (i, 0, 0, 0)
    return pl.pallas_call(
        kern,
        out_shape=(jax.ShapeDtypeStruct((nb, t_len, d_s5), F32),
                   jax.ShapeDtypeStruct((groups, 2, S5_STATE, LANES), F32),
                   jax.ShapeDtypeStruct(cast_w.shape, BF16)),
        grid=(steps,),
        in_specs=[
            pl.BlockSpec((nb, t_len, LANES), lambda i: (0, 0, i)),
            pl.BlockSpec((LANE_GROUPS, ROWS, ROWS), blk3),
            pl.BlockSpec((LANE_GROUPS, 2 * S5_STATE, ROWS), blk3),
            pl.BlockSpec((LANE_GROUPS, ROWS, 2 * S5_STATE), blk3),
            pl.BlockSpec((LANE_PAIRS, 2, LANES, LANES), blk4),
            cast_blk,
        ],
        out_specs=(pl.BlockSpec((nb, t_len, LANES), lambda i: (0, 0, i)),
                   pl.BlockSpec((LANE_GROUPS, 2, S5_STATE, LANES), blk4),
                   cast_blk),
        scratch_shapes=[pltpu.VMEM((LANE_GROUPS, ROWS, nb * n_chunks), BF16),
                        pltpu.VMEM((LANE_GROUPS, ROWS, nb * n_chunks), F32),
                        pltpu.VMEM((LANE_GROUPS * S5_STATE, nb * n_chunks), F32),
                        pltpu.VMEM((LANE_GROUPS * S5_STATE, nb * n_chunks), F32)],
        compiler_params=_params("parallel"),
        name="s5_seq",
    )(z, w_t, w_s, w_c, tab, cast_w)


def _s5_step_kernel(u_ref, h0r_ref, h0i_ref, wb_ref, wct_ref, arow_ref, d_ref,
                    y_ref, hr_ref, hi_ref):
    u = u_ref[...]
    bu = _dot(u.astype(BF16), wb_ref[...])
    y = d_ref[...] * u
    for q in range(LANE_PAIRS):
        cols = slice(q * LANES, (q + 1) * LANES)
        ar = arow_ref[q, 0, 1:2, :]
        ai = arow_ref[q, 1, 1:2, :]
        h0r, h0i = h0r_ref[:, cols], h0i_ref[:, cols]
        hr = ar * h0r - ai * h0i + bu[:, 2 * q * LANES:(2 * q + 1) * LANES]
        hi = ar * h0i + ai * h0r + bu[:, (2 * q + 1) * LANES:(2 * q + 2) * LANES]
        hr_ref[:, cols] = hr
        hi_ref[:, cols] = hi
        y = y + _dot_nt(jnp.concatenate([hr, hi], axis=1).astype(BF16), wct_ref[q])
    y_ref[...] = jax.nn.gelu(y).astype(BF16)


def _s5_step(z, h0r, h0i, wb, wct, arow, d):
    n = z.shape[0]
    slabs = wb.shape[0]
    sw = LANE_PAIRS * LANES
    col = lambda i: (0, i)
    blk3 = lambda i: (i, 0, 0)
    blk4 = lambda i: (i, 0, 0, 0)
    return pl.pallas_call(
        _s5_step_kernel,
        out_shape=(jax.ShapeDtypeStruct((n, slabs * LANES), BF16),
                   jax.ShapeDtypeStruct((n, slabs * sw), F32),
                   jax.ShapeDtypeStruct((n, slabs * sw), F32)),
        grid=(slabs,),
        in_specs=[
            pl.BlockSpec((n, LANES), col),
            pl.BlockSpec((n, sw), col),
            pl.BlockSpec((n, sw), col),
            pl.BlockSpec((None, LANES, 2 * sw), blk3),
            pl.BlockSpec((None, LANE_PAIRS, LANES, 2 * LANES), blk4),
            pl.BlockSpec((LANE_PAIRS, 2, SUBLANES, LANES), blk4),
            pl.BlockSpec((1, LANES), col),
        ],
        out_specs=(pl.BlockSpec((n, LANES), col),
                   pl.BlockSpec((n, sw), col),
                   pl.BlockSpec((n, sw), col)),
        compiler_params=_params("parallel"),
        name="s5_step",
    )(z, h0r, h0i, wb, wct, arow, d)


def _lru_gate_block(xc, wa, ba, wi, bi, lam):
    xb16 = xc.astype(BF16)
    r = jax.nn.sigmoid(_dot(xb16, wa) + ba)
    ig = jax.nn.sigmoid(_dot(xb16, wi) + bi)
    log_a = -LRU_C * r * jax.nn.softplus(-lam)
    a = jnp.exp(log_a)
    mult = jnp.sqrt(1.0 - a * a)
    return a, mult * (ig * xc)


def _lru_gates(xc, wa_ref, ba, wi_ref, bi, lam):
    blk = xc.shape[1] // LRU_HEADS
    parts = [_lru_gate_block(xc[:, h * blk:(h + 1) * blk], wa_ref[h], ba[:, h * blk:(h + 1) * blk],
                             wi_ref[h], bi[:, h * blk:(h + 1) * blk],
                             lam[:, h * blk:(h + 1) * blk]) for h in range(LRU_HEADS)]
    return (jnp.concatenate([p[0] for p in parts], axis=-1),
            jnp.concatenate([p[1] for p in parts], axis=-1))


def _lru_seq_kernel(xb_ref, gb_ref, cw_ref, cb_ref, wa_ref, ba_ref, wi_ref, bi_ref, lam_ref,
                    gm_ref, cast_ref, o_ref, hl_ref, castb_ref, xe_ref, a_ref, b_ref, hc_ref,
                    *, tt):
    castb_ref[...] = cast_ref[...].astype(BF16)
    halo = SUBLANES
    nseq, _, d = xb_ref.shape
    blk = d // LRU_HEADS

    @pl.when(pl.program_id(0) == 0)
    def _():
        xe_ref[:, 0:halo, :] = jnp.zeros((nseq, halo, d), F32)
        hc_ref[...] = jnp.zeros_like(hc_ref)

    xe_ref[:, halo:halo + tt, :] = xb_ref[...]
    for h in range(LRU_HEADS):
        cols = slice(h * blk, (h + 1) * blk)
        cw = cw_ref[:, cols]
        xc = cb_ref[:, cols] + xe_ref[:, halo:halo + tt, cols] * cw[CONV_W - 1:CONV_W, :]
        for k in range(1, CONV_W):
            xc = xc + xe_ref[:, halo - k:halo - k + tt, cols] * cw[CONV_W - 1 - k:CONV_W - k, :]
        a, b = _lru_gate_block(xc.reshape(nseq * tt, blk), wa_ref[h], ba_ref[:, cols],
                               wi_ref[h], bi_ref[:, cols], lam_ref[:, cols])
        a_ref[:, :, cols] = a.reshape(nseq, tt, blk)
        b_ref[:, :, cols] = b.reshape(nseq, tt, blk)
    xe_ref[:, 0:halo, :] = xb_ref[:, tt - halo:tt, :]

    def block(i, h):
        base = pl.multiple_of(i * SUBLANES, SUBLANES)
        for j in range(SUBLANES):
            row = pl.ds(base + j, 1)
            h = a_ref[:, row, :] * h + b_ref[:, row, :]
            b_ref[:, row, :] = h
        return h

    h = lax.fori_loop(0, tt // SUBLANES, block, hc_ref[...])
    hc_ref[...] = h
    hl_ref[...] = h

    out = b_ref[...] * jax.nn.gelu(gb_ref[...])
    o_ref[...] = _rms(out, gm_ref[...]).astype(BF16)


def _lru_seq(z, conv_w, conv_b, w_a, b_a, w_i, b_i, lam, g_merge, cast_w, tt):
    nseq, t_len, _ = z.shape
    d = conv_w.shape[1]
    steps = t_len // tt
    cast_blk = pl.BlockSpec((cast_w.shape[0] // steps, cast_w.shape[1]), lambda t: (t, 0))
    row = lambda v: v.reshape(1, d)
    const2 = lambda t: (0, 0)
    const3 = lambda t: (0, 0, 0)
    kern = functools.partial(_lru_seq_kernel, tt=tt)
    return pl.pallas_call(
        kern,
        out_shape=(jax.ShapeDtypeStruct((nseq, t_len, d), BF16),
                   jax.ShapeDtypeStruct((nseq, 1, d), F32),
                   jax.ShapeDtypeStruct(cast_w.shape, BF16)),
        grid=(steps,),
        in_specs=[
            pl.BlockSpec((nseq, tt, d), lambda t: (0, t, 1)),
            pl.BlockSpec((nseq, tt, d), lambda t: (0, t, 2)),
            pl.BlockSpec((CONV_W, d), const2),
            pl.BlockSpec((1, d), const2),
            pl.BlockSpec(w_a.shape, const3),
            pl.BlockSpec((1, d), const2),
            pl.BlockSpec(w_i.shape, const3),
            pl.BlockSpec((1, d), const2),
            pl.BlockSpec((1, d), const2),
            pl.BlockSpec((1, d), const2),
            cast_blk,
        ],
        out_specs=(pl.BlockSpec((nseq, tt, d), lambda t: (0, t, 0)),
                   pl.BlockSpec((nseq, 1, d), const3),
                   cast_blk),
        scratch_shapes=[pltpu.VMEM((nseq, tt + SUBLANES, d), F32),
                        pltpu.VMEM((nseq, tt, d), F32),
                        pltpu.VMEM((nseq, tt, d), F32),
                        pltpu.VMEM((nseq, 1, d), F32)],
        compiler_params=_params("arbitrary"),
        name="rglru_seq",
    )(z, z, conv_w, row(conv_b), w_a, row(b_a), w_i, row(b_i), row(lam), row(g_merge), cast_w)


def _lru_step_kernel(xb_ref, gb_ref, c0_ref, c1_ref, c2_ref, h0_ref, cw_ref, cb_ref,
                     wa_ref, ba_ref, wi_ref, bi_ref, lam_ref, gm_ref, o_ref, h_ref, cv_ref):
    d = xb_ref.shape[1]
    xb = xb_ref[...]
    cw = cw_ref[...]
    xc = (cb_ref[...] + c0_ref[...] * cw[0:1, :] + c1_ref[...] * cw[1:2, :]
          + c2_ref[...] * cw[2:3, :] + xb * cw[3:4, :])
    a, b = _lru_gates(xc, wa_ref, ba_ref[...], wi_ref, bi_ref[...], lam_ref[...])
    h = a * h0_ref[...] + b
    h_ref[...] = h
    out = h * jax.nn.gelu(gb_ref[...])
    o_ref[...] = _rms(out, gm_ref[...]).astype(BF16)
    cv_ref[:, 0:d] = c1_ref[...]
    cv_ref[:, d:2 * d] = c2_ref[...]
    cv_ref[:, 2 * d:3 * d] = xb


def _lru_step(z, conv0, h0, conv_w, conv_b, w_a, b_a, w_i, b_i, lam, g_merge):
    n = z.shape[0]
    d = conv_w.shape[1]
    taps = CONV_W - 1
    conv2d = conv0.reshape(n, taps * d)
    row = lambda v: v.reshape(1, d)
    full = lambda shape: pl.BlockSpec(shape, lambda i: (0,) * len(shape))
    col = lambda c: pl.BlockSpec((n, d), lambda i: (0, c))
    nd = full((n, d))
    rd = full((1, d))
    o, h, cv = pl.pallas_call(
        _lru_step_kernel,
        out_shape=(jax.ShapeDtypeStruct((n, d), BF16), jax.ShapeDtypeStruct((n, d), F32),
                   jax.ShapeDtypeStruct((n, taps * d), F32)),
        grid=(1,),
        in_specs=[
            col(1), col(2), col(0), col(1), col(2), nd,
            full((CONV_W, d)), rd,
            full(w_a.shape), rd, full(w_i.shape), rd, rd, rd,
        ],
        out_specs=(nd, nd, full((n, taps * d))),
        compiler_params=_params("arbitrary"),
        name="rglru_step",
    )(z, z, conv2d, conv2d, conv2d, h0, conv_w, row(conv_b),
      w_a, row(b_a), w_i, row(b_i), row(lam), row(g_merge))
    return o, h, cv.reshape(n, taps, d)


def _glu_rows(y, w_ref, b_ref, g_ref):
    gate = jax.nn.sigmoid(_dot(y.astype(BF16), w_ref[...]) + b_ref[...])
    return _rms(y.astype(F32) * gate, g_ref[...]).astype(BF16)


def _glu_kernel(y_ref, ys_ref, w_ref, b_ref, g_ref, o_ref, os_ref):
    for rows in _row_parts(y_ref.shape[0], 4):
        o_ref[rows, :] = _glu_rows(y_ref[rows, :], w_ref, b_ref, g_ref)

    @_on_last_row_tile
    def _():
        os_ref[...] = _glu_rows(ys_ref[...], w_ref, b_ref, g_ref)


def _glu(y, ys, w, b, g, tm):
    m, d = y.shape
    ms = ys.shape[0]
    const = lambda i: (0, 0)
    return pl.pallas_call(
        _glu_kernel,
        out_shape=(jax.ShapeDtypeStruct((m, d), BF16), jax.ShapeDtypeStruct((ms, d), BF16)),
        grid=(m // tm,),
        in_specs=[
            pl.BlockSpec((tm, d), lambda i: (i, 0)),
            pl.BlockSpec((ms, d), const),
            pl.BlockSpec((d, d), const),
            pl.BlockSpec((1, d), const),
            pl.BlockSpec((1, d), const),
        ],
        out_specs=(pl.BlockSpec((tm, d), lambda i: (i, 0)), pl.BlockSpec((ms, d), const)),
        compiler_params=_params("arbitrary"),
        name="s5_glu",
    )(y, ys, w, b.reshape(1, d), g.reshape(1, d))


def _out_proj_kernel(x_ref, ma_ref, mb_ref, xs_ref, mas_ref, mbs_ref, w_ref, o_ref, os_ref, wb_ref):
    @pl.when(pl.program_id(0) == 0)
    def _():
        wb_ref[...] = w_ref[...].astype(BF16)

    ka = ma_ref.shape[1]
    proj = lambda x, ma, mb: x + _dot(ma, wb_ref[:ka, :]) + _dot(mb, wb_ref[ka:, :])
    for rows in _row_parts(x_ref.shape[0], 2):
        o_ref[rows, :] = proj(x_ref[rows, :], ma_ref[rows, :], mb_ref[rows, :])

    @_on_last_row_tile
    def _():
        os_ref[...] = proj(xs_ref[...], mas_ref[...], mbs_ref[...])


def _out_proj(x, ma, mb, xs, mas, mbs, w, tm):
    m, n = x.shape
    ms = xs.shape[0]
    ka = ma.shape[1]
    kb = mb.shape[1]
    const = lambda i: (0, 0)
    tile = lambda i: (i, 0)
    return pl.pallas_call(
        _out_proj_kernel,
        out_shape=(jax.ShapeDtypeStruct((m, n), F32), jax.ShapeDtypeStruct((ms, n), F32)),
        grid=(m // tm,),
        in_specs=[
            pl.BlockSpec((tm, n), tile),
            pl.BlockSpec((tm, ka), tile),
            pl.BlockSpec((tm, kb), tile),
            pl.BlockSpec((ms, n), const),
            pl.BlockSpec((ms, ka), const),
            pl.BlockSpec((ms, kb), const),
            pl.BlockSpec((ka + kb, n), const, pipeline_mode=pl.Buffered(1)),
        ],
        out_specs=(pl.BlockSpec((tm, n), tile), pl.BlockSpec((ms, n), const)),
        scratch_shapes=[pltpu.VMEM((ka + kb, n), BF16)],
        compiler_params=_params("arbitrary"),
        name="out_proj",
    )(x, ma, mb, xs, mas, mbs, w)


def _mlp_kernel(x_ref, xs_ref, g_ref, wu_ref, wd_ref, o_ref, os_ref, h_ref, hs_ref):
    def ffn(h):
        act = jnp.square(jnp.maximum(_dot(h, wu_ref[...]), 0.0)).astype(BF16)
        return _dot(act, wd_ref[...])

    def rows(x_ref, o_ref, h_ref):
        first = pl.program_id(1) == 0

        @pl.when(first)
        def _():
            for part in _row_parts(x_ref.shape[0], 4):
                x = x_ref[part, :]
                h = _rms(x, g_ref[...]).astype(BF16)
                h_ref[part, :] = h
                o_ref[part, :] = x + ffn(h)

        @pl.when(jnp.logical_not(first))
        def _():
            o_ref[...] += ffn(h_ref[...])

    rows(x_ref, o_ref, h_ref)
    _on_last_row_tile(lambda: rows(xs_ref, os_ref, hs_ref))


def _mlp(x, xs, g, w_up, w_down, tm, tf):
    m, d = x.shape
    ms = xs.shape[0]
    f = w_up.shape[1]
    const = lambda i, j: (0, 0)
    return pl.pallas_call(
        _mlp_kernel,
        out_shape=(jax.ShapeDtypeStruct((m, d), F32), jax.ShapeDtypeStruct((ms, d), F32)),
        grid=(m // tm, f // tf),
        in_specs=[
            pl.BlockSpec((tm, d), lambda i, j: (i, 0)),
            pl.BlockSpec((ms, d), const),
            pl.BlockSpec((1, d), const),
            pl.BlockSpec((d, tf), lambda i, j: (0, j)),
            pl.BlockSpec((tf, d), lambda i, j: (j, 0)),
        ],
        out_specs=(pl.BlockSpec((tm, d), lambda i, j: (i, 0)), pl.BlockSpec((ms, d), const)),
        scratch_shapes=[pltpu.VMEM((tm, d), BF16), pltpu.VMEM((ms, d), BF16)],
        compiler_params=_params("arbitrary", "arbitrary"),
        name="mlp",
    )(x, xs, g.reshape(1, d), w_up, w_down)


def _ple_kernel(x_ref, p_ref, xs_ref, ps_ref, g_ref, wg_ref, wp_ref, gf_ref, o_ref, os_ref,
                wgb_ref, wpb_ref, *, final):
    @pl.when(pl.program_id(0) == 0)
    def _():
        wgb_ref[...] = wg_ref[...].astype(BF16)
        wpb_ref[...] = wp_ref[...].astype(BF16)

    def ple(x, p):
        gate = jax.nn.sigmoid(_dot(_rms(x, g_ref[...]).astype(BF16), wgb_ref[...]))
        pe = _dot(p.astype(BF16), wpb_ref[...])
        x = x + pe * gate
        return _rms(x, gf_ref[...]) if final else x

    for rows in _row_parts(x_ref.shape[0], 2):
        o_ref[rows, :] = ple(x_ref[rows, :], p_ref[rows, :])

    @_on_last_row_tile
    def _():
        os_ref[...] = ple(xs_ref[...], ps_ref[...])


def _ple(x, p, xs, ps, g, w_gate, w_ple, g_final, tm, final):
    m, d = x.shape
    ms = xs.shape[0]
    dp = p.shape[1]
    once = pl.Buffered(1)
    const = lambda i: (0, 0)
    tile = lambda i: (i, 0)
    return pl.pallas_call(
        functools.partial(_ple_kernel, final=final),
        out_shape=(jax.ShapeDtypeStruct((m, d), F32), jax.ShapeDtypeStruct((ms, d), F32)),
        grid=(m // tm,),
        in_specs=[
            pl.BlockSpec((tm, d), tile),
            pl.BlockSpec((tm, dp), tile),
            pl.BlockSpec((ms, d), const),
            pl.BlockSpec((ms, dp), const),
            pl.BlockSpec((1, d), const),
            pl.BlockSpec((d, d), const, pipeline_mode=once),
            pl.BlockSpec((dp, d), const, pipeline_mode=once),
            pl.BlockSpec((1, d), const),
        ],
        out_specs=(pl.BlockSpec((tm, d), tile), pl.BlockSpec((ms, d), const)),
        scratch_shapes=[pltpu.VMEM((d, d), BF16), pltpu.VMEM((dp, d), BF16)],
        compiler_params=_params("arbitrary"),
        name="ple_final",
    )(x, p, xs, ps, g.reshape(1, d), w_gate, w_ple, g_final.reshape(1, d))


TT_LRU = 64
TM_IN, TN_IN = 1024, 1536
TM_GLU = 1024
TM_OUT = 512
TM_MLP, TF_MLP = 512, 1024
TM_PLE = 512


def kernel(x_prompt, x_sample, state_s5_re, state_s5_im, state_lru, state_conv, p_prompt, p_sample,
           g_mix, w_in, s5_lam_re, s5_lam_im, s5_log_step, s5_b_re, s5_b_im, s5_c_re, s5_c_im, s5_d,
           s5_w_glu, s5_b_glu, conv_w, conv_b, lru_w_a, lru_b_a, lru_w_i, lru_b_i, lru_lam,
           g_merge_a, g_merge_b, w_out, g_mlp, w_up, w_down, g_ple, w_ple_gate, w_ple, g_final):
    depth = g_mix.shape[0]
    nb, t_len, d_model = x_prompt.shape
    ns = x_sample.shape[0]
    d_s5 = s5_d.shape[1]
    d_lru = conv_w.shape[2]
    groups = d_s5 // S5_CH

    xp = x_prompt.reshape(nb * t_len, d_model)
    xs = x_sample.reshape(ns, d_model)
    outs = [[] for _ in range(8)]
    for l in range(depth):
        final = l == depth - 1
        w_glu_b = s5_w_glu[l].astype(BF16)
        lru_args = (conv_w[l], conv_b[l], lru_w_a[l].astype(BF16), lru_b_a[l],
                    lru_w_i[l].astype(BF16), lru_b_i[l], lru_lam[l], g_merge_b[l])
        w_t, w_s, w_c, tab, wb, wct, arow, w_in_b = _s5_params(
            s5_lam_re[l], s5_lam_im[l], s5_log_step[l], s5_b_re[l], s5_b_im[l],
            s5_c_re[l], s5_c_im[l], s5_d[l], w_in[l])

        z, zs = _norm_matmul(xp, xs, g_mix[l], w_in_b, TM_IN, TN_IN)
        z4 = z.reshape(nb, t_len, 3 * d_s5)
        y, hf, w_down_b = _s5_seq(z4, w_t, w_s, w_c, tab, d_s5, w_down[l])
        mb, lru_h, w_up_b = _lru_seq(z4, *lru_args, w_up[l], TT_LRU)
        ys, hsr, hsi = _s5_step(zs, state_s5_re[l].reshape(ns, groups * S5_STATE),
                                state_s5_im[l].reshape(ns, groups * S5_STATE),
                                wb, wct, arow, s5_d[l].reshape(1, d_s5))
        mbs, lru_hs, conv_s = _lru_step(zs, state_conv[l], state_lru[l], *lru_args)
        ma, mas = _glu(y.reshape(nb * t_len, d_s5), ys, w_glu_b, s5_b_glu[l], g_merge_a[l], TM_GLU)
        xp, xs = _out_proj(xp, ma, mb.reshape(nb * t_len, d_lru), xs, mas, mbs, w_out[l], TM_OUT)
        xp, xs = _mlp(xp, xs, g_mlp[l], w_up_b, w_down_b, TM_MLP, TF_MLP)
        xp, xs = _ple(xp, p_prompt[l].reshape(nb * t_len, -1), xs, p_sample[l].reshape(ns, -1),
                      g_ple[l], w_ple_gate[l], w_ple[l], g_final, TM_PLE, final)
        hf = jnp.transpose(hf[:, :, :, :nb], (1, 3, 0, 2))
        outs[0].append(hf[0])
        outs[1].append(hf[1])
        outs[2].append(lru_h.reshape(nb, d_lru))
        outs[3].append(z4[:, t_len - (CONV_W - 1):, d_s5:d_s5 + d_lru])
        outs[4].append(hsr.reshape(ns, groups, S5_STATE))
        outs[5].append(hsi.reshape(ns, groups, S5_STATE))
        outs[6].append(lru_hs)
        outs[7].append(conv_s)
    return (xp.reshape(nb, t_len, d_model), xs.reshape(ns, 1, d_model),
            *(jnp.stack(o) for o in outs))
```

```python
import functools

import jax
import jax.numpy as jnp
from jax import lax
from jax.experimental import pallas as pl
from jax.experimental.pallas import tpu as pltpu

F32 = jnp.float32
BF16 = jnp.bfloat16
HIGHEST = lax.Precision.HIGHEST

EPS = 1e-6
LRU_C = 8.0
S5_CH = 16
S5_STATE = 64
LRU_HEADS = 4
CONV_W = 4
CHUNK = 16
SUBLANES = 8
LANES = 128
LANE_GROUPS = LANES // S5_CH
LANE_PAIRS = LANE_GROUPS // 2
ROWS = CHUNK * S5_CH
POW_ROWS = 32
SCAN_ROW0 = POW_ROWS
VMEM_LIMIT = 56 * 1024 * 1024


def _params(*sem):
    return pltpu.CompilerParams(dimension_semantics=sem, vmem_limit_bytes=VMEM_LIMIT)


def _rms(x, g):
    return x * lax.rsqrt(jnp.mean(x * x, axis=-1, keepdims=True) + EPS) * g


def _dot(a, b):
    return jnp.dot(a, b, preferred_element_type=F32)


def _dot_nt(a, b, precision=None):
    return lax.dot_general(a, b, (((1,), (1,)), ((), ())), precision=precision,
                           preferred_element_type=F32)


def _transpose_tiles(x):
    r, c = x.shape
    return jnp.concatenate(
        [jnp.concatenate([x[i:i + LANES, j:j + LANES].T for i in range(0, r, LANES)], axis=1)
         for j in range(0, c, LANES)], axis=0)


def _row_parts(tm, want):
    parts = max(1, min(want, tm // LANES))
    step = tm // parts
    return [slice(r * step, (r + 1) * step) for r in range(parts)]


def _on_last_row_tile(fn):
    pl.when(pl.program_id(0) == pl.num_programs(0) - 1)(fn)


def _norm_matmul_kernel(x_ref, xs_ref, g_ref, w_ref, o_ref, os_ref):
    tn = o_ref.shape[1]
    w = w_ref[:, pl.ds(pl.multiple_of(pl.program_id(1) * tn, tn), tn)]
    for rows in _row_parts(x_ref.shape[0], 8):
        h = _rms(x_ref[rows, :], g_ref[...]).astype(BF16)
        o_ref[rows, :] = _dot(h, w)

    @_on_last_row_tile
    def _():
        os_ref[...] = _dot(_rms(xs_ref[...], g_ref[...]).astype(BF16), w)


def _norm_matmul(x, xs, g, w, tm, tn):
    m, k = x.shape
    ms = xs.shape[0]
    n = w.shape[1]
    ni = m // tm
    return pl.pallas_call(
        _norm_matmul_kernel,
        out_shape=(jax.ShapeDtypeStruct((m, n), F32), jax.ShapeDtypeStruct((ms, n), F32)),
        grid=(ni, n // tn),
        in_specs=[
            pl.BlockSpec((tm, k), lambda i, j: (i, 0)),
            pl.BlockSpec((ms, k), lambda i, j: (0, 0)),
            pl.BlockSpec((1, k), lambda i, j: (0, 0)),
            pl.BlockSpec((k, n), lambda i, j: (0, 0), pipeline_mode=pl.Buffered(1)),
        ],
        out_specs=(pl.BlockSpec((tm, tn), lambda i, j: (i, j)),
                   pl.BlockSpec((ms, tn), lambda i, j: (0, jnp.where(i == ni - 1, j, 0)))),
        compiler_params=_params("arbitrary", "arbitrary"),
        name="norm_w_in",
    )(x, xs, g.reshape(1, k), w)


def _s5_param_kernel(lam_ref, bc_ref, d_ref, cast_ref,
                     wt_ref, ws_ref, wc_ref, tab_ref, wb_ref, wct_ref, arow_ref, castb_ref):
    castb_ref[...] = cast_ref[...].astype(BF16)
    row_p = lax.broadcasted_iota(jnp.int32, (POW_ROWS, LANES), 0)
    lane = lax.broadcasted_iota(jnp.int32, (ROWS, LANES), 1)
    lane16 = lax.broadcasted_iota(jnp.int32, (S5_CH, LANES), 1)
    row16 = lax.broadcasted_iota(jnp.int32, (S5_CH, LANES), 0)
    lo = lane < S5_STATE

    wb_ref[...] = jnp.zeros_like(wb_ref)
    wct_ref[...] = jnp.zeros_like(wct_ref)

    for q in range(LANE_PAIRS):
        lr, li = lam_ref[0, q], lam_ref[1, q]
        step = jnp.exp(lam_ref[2, q])
        mag = jnp.exp(lr * step)
        ar = mag * jnp.cos(li * step)
        ai = mag * jnp.sin(li * step)
        nr, ni = ar - 1.0, ai
        den = lr * lr + li * li
        cr = (nr * lr + ni * li) / den
        ci = (ni * lr - nr * li) / den
        bt_r, bt_i = bc_ref[0, q], bc_ref[1, q]
        bb_r = cr * bt_r - ci * bt_i
        bb_i = cr * bt_i + ci * bt_r
        c_r, c_i = bc_ref[2, q], bc_ref[3, q]

        pr = jnp.ones((POW_ROWS, LANES), F32)
        pi = jnp.zeros((POW_ROWS, LANES), F32)
        sr, si = ar, ai
        sq = []
        for m in range(POW_ROWS.bit_length() - 1 + SUBLANES):
            sq.append((sr, si))
            if (1 << m) < POW_ROWS:
                bit = ((row_p >> m) & 1) == 1
                pr, pi = (jnp.where(bit, pr * sr - pi * si, pr),
                          jnp.where(bit, pr * si + pi * sr, pi))
            sr, si = sr * sr - si * si, 2.0 * sr * si
        m0 = CHUNK.bit_length() - 1
        scan_r = jnp.concatenate([sq[m0 + j][0] for j in range(SUBLANES)], axis=0)
        scan_i = jnp.concatenate([sq[m0 + j][1] for j in range(SUBLANES)], axis=0)
        pad = jnp.zeros((LANES - POW_ROWS - SUBLANES, LANES), F32)
        tab_ref[q, 0] = jnp.concatenate([pr, scan_r, pad], axis=0).T
        tab_ref[q, 1] = jnp.concatenate([pi, scan_i, pad], axis=0).T
        arow_ref[q, 0] = pr[:SUBLANES]
        arow_ref[q, 1] = pi[:SUBLANES]

        def expand(tab, k0, sign):
            return jnp.concatenate(
                [jnp.broadcast_to(tab[k0 + sign * s:k0 + sign * s + 1, :], (S5_CH, LANES))
                 for s in range(CHUNK)], axis=0)

        tile = lambda v: jnp.concatenate([v] * CHUNK, axis=0)

        e_r, e_i = expand(pr, CHUNK - 1, -1), expand(pi, CHUNK - 1, -1)
        t_r, t_i = tile(bb_r), tile(bb_i)
        ws_r = _transpose_tiles(t_r * e_r - t_i * e_i)
        ws_i = _transpose_tiles(t_r * e_i + t_i * e_r)
        for h in range(2):
            rows = slice(h * S5_STATE, (h + 1) * S5_STATE)
            ws_ref[2 * q + h] = jnp.concatenate([ws_r[rows], ws_i[rows]], axis=0).astype(BF16)

        e_r, e_i = expand(pr, 1, 1), expand(pi, 1, 1)
        t_r, t_i = tile(c_r), tile(c_i)
        ca_r = t_r * e_r - t_i * e_i
        ca_n = -(t_r * e_i + t_i * e_r)
        wc_ref[2 * q] = jnp.where(lo, ca_r, pltpu.roll(ca_n, S5_STATE, 1)).astype(BF16)
        wc_ref[2 * q + 1] = jnp.where(lo, pltpu.roll(ca_r, S5_STATE, 1), ca_n).astype(BF16)

        e_r, e_i = expand(pr, 0, 1), expand(pi, 0, 1)
        c0_r = t_r * e_r - t_i * e_i
        c0_i = t_r * e_i + t_i * e_r

        for h in range(2):
            g = 2 * q + h
            mine = (lane16 < S5_STATE) == (h == 0)
            bm_r = jnp.where(mine, bb_r, 0.0)
            bm_i = jnp.where(mine, bb_i, 0.0)
            kt = _dot_nt(bm_r, c0_r, HIGHEST) - _dot_nt(bm_i, c0_i, HIGHEST)
            k_lo = kt[:, :LANES] + jnp.where(row16 == lane16, d_ref[g], 0.0)
            k_hi = kt[:, LANES:]
            blocks = []
            half = LANES // S5_CH
            for s in range(CHUNK):
                sh = (s % half) * S5_CH
                keep = lane16 >= sh
                r_lo = pltpu.roll(k_lo, sh, 1) if sh else k_lo
                r_hi = pltpu.roll(k_hi, sh, 1) if sh else k_hi
                if s < half:
                    blk = jnp.concatenate([jnp.where(keep, r_lo, 0.0),
                                           jnp.where(keep, r_hi, r_lo)], axis=1)
                else:
                    blk = jnp.concatenate([jnp.zeros_like(r_lo), jnp.where(keep, r_lo, 0.0)], axis=1)
                blocks.append(blk)
            wt_ref[g] = _transpose_tiles(jnp.concatenate(blocks, axis=0)).astype(BF16)

            r0 = (2 * q + h) * S5_CH
            c0 = q * 2 * LANES
            wb_ref[r0:r0 + S5_CH, c0:c0 + LANES] = bm_r.astype(BF16)
            wb_ref[r0:r0 + S5_CH, c0 + LANES:c0 + 2 * LANES] = bm_i.astype(BF16)
            wct_ref[q, r0:r0 + S5_CH, :LANES] = jnp.where(mine, c_r, 0.0).astype(BF16)
            wct_ref[q, r0:r0 + S5_CH, LANES:] = jnp.where(mine, -c_i, 0.0).astype(BF16)


def _s5_params(lam_re, lam_im, log_step, b_re, b_im, c_re, c_im, d, cast_w):
    groups = lam_re.shape[0]
    pairs = groups // 2
    slabs = groups // LANE_GROUPS
    cast_blk = pl.BlockSpec((cast_w.shape[0] // slabs, cast_w.shape[1]), lambda i: (i, 0))
    ls = jnp.broadcast_to(log_step[:, None], (groups, S5_STATE))
    lam = jnp.stack([lam_re, lam_im, ls]).reshape(3, pairs, 1, LANES)
    bc = jnp.stack([jnp.transpose(b_re, (0, 2, 1)), jnp.transpose(b_im, (0, 2, 1)), c_re, c_im])
    bc = jnp.transpose(bc.reshape(4, pairs, 2, S5_CH, S5_STATE), (0, 1, 3, 2, 4)).reshape(
        4, pairs, S5_CH, LANES)
    d_pad = jnp.pad(d.reshape(groups, 1, S5_CH), ((0, 0), (0, 0), (0, LANES - S5_CH)))
    blk3 = lambda i: (i, 0, 0)
    blk4 = lambda i: (i, 0, 0, 0)
    pblk = lambda i: (0, i, 0, 0)
    return pl.pallas_call(
        _s5_param_kernel,
        out_shape=(jax.ShapeDtypeStruct((groups, ROWS, ROWS), BF16),
                   jax.ShapeDtypeStruct((groups, 2 * S5_STATE, ROWS), BF16),
                   jax.ShapeDtypeStruct((groups, ROWS, 2 * S5_STATE), BF16),
                   jax.ShapeDtypeStruct((pairs, 2, LANES, LANES), F32),
                   jax.ShapeDtypeStruct((slabs, LANES, LANE_PAIRS * 2 * LANES), BF16),
                   jax.ShapeDtypeStruct((slabs, LANE_PAIRS, LANES, 2 * LANES), BF16),
                   jax.ShapeDtypeStruct((pairs, 2, SUBLANES, LANES), F32),
                   jax.ShapeDtypeStruct(cast_w.shape, BF16)),
        grid=(slabs,),
        in_specs=[pl.BlockSpec((3, LANE_PAIRS, 1, LANES), pblk),
                  pl.BlockSpec((4, LANE_PAIRS, S5_CH, LANES), pblk),
                  pl.BlockSpec((LANE_GROUPS, 1, LANES), blk3), cast_blk],
        out_specs=(pl.BlockSpec((LANE_GROUPS, ROWS, ROWS), blk3),
                   pl.BlockSpec((LANE_GROUPS, 2 * S5_STATE, ROWS), blk3),
                   pl.BlockSpec((LANE_GROUPS, ROWS, 2 * S5_STATE), blk3),
                   pl.BlockSpec((LANE_PAIRS, 2, LANES, LANES), blk4),
                   pl.BlockSpec((None, LANES, LANE_PAIRS * 2 * LANES), blk3),
                   pl.BlockSpec((None, LANE_PAIRS, LANES, 2 * LANES), blk4),
                   pl.BlockSpec((LANE_PAIRS, 2, SUBLANES, LANES), blk4),
                   cast_blk),
        compiler_params=_params("parallel"),
        name="s5_params",
    )(lam, bc, d_pad, cast_w)


def _s5_seq_kernel(z_ref, wt_ref, ws_ref, wc_ref, tab_ref, cast_ref, y_ref, hf_ref, castb_ref,
                   ut_ref, yt_ref, sr_ref, si_ref, *, nb, n_chunks):
    castb_ref[...] = cast_ref[...].astype(BF16)
    scan_steps = n_chunks.bit_length() - 1
    for n in range(nb):
        for s in range(CHUNK):
            xs = z_ref[n, pl.ds(s, n_chunks, stride=CHUNK), :]
            ut_ref[:, s * S5_CH:(s + 1) * S5_CH, n * n_chunks:(n + 1) * n_chunks] = (
                xs.T.astype(BF16).reshape(LANE_GROUPS, S5_CH, n_chunks))

    for g in range(LANE_GROUPS):
        st = _dot(ws_ref[g], ut_ref[g])
        sr_ref[g * S5_STATE:(g + 1) * S5_STATE, :] = st[:S5_STATE]
        si_ref[g * S5_STATE:(g + 1) * S5_STATE, :] = st[S5_STATE:]

    srows = LANE_GROUPS * S5_STATE
    lane = lax.broadcasted_iota(jnp.int32, (srows, n_chunks), 1)
    tab_r = jnp.concatenate([tab_ref[q, 0] for q in range(LANE_PAIRS)], axis=0)
    tab_i = jnp.concatenate([tab_ref[q, 1] for q in range(LANE_PAIRS)], axis=0)
    fin_r = jnp.zeros((srows, n_chunks), F32)
    fin_i = jnp.zeros((srows, n_chunks), F32)
    for n in range(nb):
        cols = slice(n * n_chunks, (n + 1) * n_chunks)
        xr = sr_ref[:, cols]
        xi = si_ref[:, cols]
        for k in range(scan_steps):
            d = 1 << k
            keep = lane >= d
            sr = jnp.where(keep, pltpu.roll(xr, d, 1), 0.0)
            si = jnp.where(keep, pltpu.roll(xi, d, 1), 0.0)
            pr = tab_r[:, SCAN_ROW0 + k:SCAN_ROW0 + k + 1]
            pi = tab_i[:, SCAN_ROW0 + k:SCAN_ROW0 + k + 1]
            xr, xi = xr + (pr * sr - pi * si), xi + (pr * si + pi * sr)
        last = lane == n_chunks - 1
        fin_r = fin_r + jnp.where(
            lane == n, jnp.sum(jnp.where(last, xr, 0.0), axis=1, keepdims=True), 0.0)
        fin_i = fin_i + jnp.where(
            lane == n, jnp.sum(jnp.where(last, xi, 0.0), axis=1, keepdims=True), 0.0)
        keep = lane >= 1
        sr_ref[:, cols] = jnp.where(keep, pltpu.roll(xr, 1, 1), 0.0)
        si_ref[:, cols] = jnp.where(keep, pltpu.roll(xi, 1, 1), 0.0)

    for g in range(LANE_GROUPS):
        rows = slice(g * S5_STATE, (g + 1) * S5_STATE)
        hf_ref[g, 0] = fin_r[rows]
        hf_ref[g, 1] = fin_i[rows]
        hs = jnp.concatenate([sr_ref[rows, :], si_ref[rows, :]], axis=0).astype(BF16)
        yt_ref[g] = jax.nn.gelu(_dot(wc_ref[g], hs) + _dot(wt_ref[g], ut_ref[g]))

    for n in range(nb):
        for s in range(CHUNK):
            blk = yt_ref[:, s * S5_CH:(s + 1) * S5_CH, n * n_chunks:(n + 1) * n_chunks]
            y_ref[n, pl.ds(s, n_chunks, stride=CHUNK), :] = blk.reshape(LANES, n_chunks).T


def _s5_seq(z, w_t, w_s, w_c, tab, d_s5, cast_w):
    nb, t_len, _ = z.shape
    n_chunks = t_len // CHUNK
    assert n_chunks == LANES, "the chunk axis must fill one 128-lane tile"
    groups = d_s5 // S5_CH
    steps = groups // LANE_GROUPS
    cast_blk = pl.BlockSpec((cast_w.shape[0] // steps, cast_w.shape[1]), lambda i: (i, 0))
    kern = functools.partial(_s5_seq_kernel, nb=nb, n_chunks=n_chunks)
    blk3 = lambda i: (i, 0, 0)
    blk4 = lambda i: (i, 0, 0, 0)
    return pl.pallas_call(
        kern,
        out_shape=(jax.ShapeDtypeStruct((nb, t_len, d_s5), F32),
                   jax.ShapeDtypeStruct((groups, 2, S5_STATE, LANES), F32),
                   jax.ShapeDtypeStruct(cast_w.shape, BF16)),
        grid=(steps,),
        in_specs=[
            pl.BlockSpec((nb, t_len, LANES), lambda i: (0, 0, i)),
            pl.BlockSpec((LANE_GROUPS, ROWS, ROWS), blk3),
            pl.BlockSpec((LANE_GROUPS, 2 * S5_STATE, ROWS), blk3),
            pl.BlockSpec((LANE_GROUPS, ROWS, 2 * S5_STATE), blk3),
            pl.BlockSpec((LANE_PAIRS, 2, LANES, LANES), blk4),
            cast_blk,
        ],
        out_specs=(pl.BlockSpec((nb, t_len, LANES), lambda i: (0, 0, i)),
                   pl.BlockSpec((LANE_GROUPS, 2, S5_STATE, LANES), blk4),
                   cast_blk),
        scratch_shapes=[pltpu.VMEM((LANE_GROUPS, ROWS, nb * n_chunks), BF16),
                        pltpu.VMEM((LANE_GROUPS, ROWS, nb * n_chunks), F32),
                        pltpu.VMEM((LANE_GROUPS * S5_STATE, nb * n_chunks), F32),
                        pltpu.VMEM((LANE_GROUPS * S5_STATE, nb * n_chunks), F32)],
        compiler_params=_params("parallel"),
        name="s5_seq",
    )(z, w_t, w_s, w_c, tab, cast_w)


def _s5_step_kernel(u_ref, h0r_ref, h0i_ref, wb_ref, wct_ref, arow_ref, d_ref,
                    y_ref, hr_ref, hi_ref):
    u = u_ref[...]
    bu = _dot(u.astype(BF16), wb_ref[...])
    y = d_ref[...] * u
    for q in range(LANE_PAIRS):
        cols = slice(q * LANES, (q + 1) * LANES)
        ar = arow_ref[q, 0, 1:2, :]
        ai = arow_ref[q, 1, 1:2, :]
        h0r, h0i = h0r_ref[:, cols], h0i_ref[:, cols]
        hr = ar * h0r - ai * h0i + bu[:, 2 * q * LANES:(2 * q + 1) * LANES]
        hi = ar * h0i + ai * h0r + bu[:, (2 * q + 1) * LANES:(2 * q + 2) * LANES]
        hr_ref[:, cols] = hr
        hi_ref[:, cols] = hi
        y = y + _dot_nt(jnp.concatenate([hr, hi], axis=1).astype(BF16), wct_ref[q])
    y_ref[...] = jax.nn.gelu(y).astype(BF16)


def _s5_step(z, h0r, h0i, wb, wct, arow, d):
    n = z.shape[0]
    slabs = wb.shape[0]
    sw = LANE_PAIRS * LANES
    col = lambda i: (0, i)
    blk3 = lambda i: (i, 0, 0)
    blk4 = lambda i: (i, 0, 0, 0)
    return pl.pallas_call(
        _s5_step_kernel,
        out_shape=(jax.ShapeDtypeStruct((n, slabs * LANES), BF16),
                   jax.ShapeDtypeStruct((n, slabs * sw), F32),
                   jax.ShapeDtypeStruct((n, slabs * sw), F32)),
        grid=(slabs,),
        in_specs=[
            pl.BlockSpec((n, LANES), col),
            pl.BlockSpec((n, sw), col),
            pl.BlockSpec((n, sw), col),
            pl.BlockSpec((None, LANES, 2 * sw), blk3),
            pl.BlockSpec((None, LANE_PAIRS, LANES, 2 * LANES), blk4),
            pl.BlockSpec((LANE_PAIRS, 2, SUBLANES, LANES), blk4),
            pl.BlockSpec((1, LANES), col),
        ],
        out_specs=(pl.BlockSpec((n, LANES), col),
                   pl.BlockSpec((n, sw), col),
                   pl.BlockSpec((n, sw), col)),
        compiler_params=_params("parallel"),
        name="s5_step",
    )(z, h0r, h0i, wb, wct, arow, d)


def _lru_gate_block(xc, wa, ba, wi, bi, lam):
    xb16 = xc.astype(BF16)
    r = jax.nn.sigmoid(_dot(xb16, wa) + ba)
    ig = jax.nn.sigmoid(_dot(xb16, wi) + bi)
    log_a = -LRU_C * r * jax.nn.softplus(-lam)
    a = jnp.exp(log_a)
    mult = jnp.sqrt(1.0 - a * a)
    return a, mult * (ig * xc)


def _lru_gates(xc, wa_ref, ba, wi_ref, bi, lam):
    blk = xc.shape[1] // LRU_HEADS
    parts = [_lru_gate_block(xc[:, h * blk:(h + 1) * blk], wa_ref[h], ba[:, h * blk:(h + 1) * blk],
                             wi_ref[h], bi[:, h * blk:(h + 1) * blk],
                             lam[:, h * blk:(h + 1) * blk]) for h in range(LRU_HEADS)]
    return (jnp.concatenate([p[0] for p in parts], axis=-1),
            jnp.concatenate([p[1] for p in parts], axis=-1))


def _lru_seq_kernel(xb_ref, gb_ref, cw_ref, cb_ref, wa_ref, ba_ref, wi_ref, bi_ref, lam_ref,
                    gm_ref, cast_ref, o_ref, hl_ref, castb_ref, xe_ref, a_ref, b_ref, hc_ref,
                    *, tt):
    castb_ref[...] = cast_ref[...].astype(BF16)
    halo = SUBLANES
    nseq, _, d = xb_ref.shape
    blk = d // LRU_HEADS

    @pl.when(pl.program_id(0) == 0)
    def _():
        xe_ref[:, 0:halo, :] = jnp.zeros((nseq, halo, d), F32)
        hc_ref[...] = jnp.zeros_like(hc_ref)

    xe_ref[:, halo:halo + tt, :] = xb_ref[...]
    for h in range(LRU_HEADS):
        cols = slice(h * blk, (h + 1) * blk)
        cw = cw_ref[:, cols]
        xc = cb_ref[:, cols] + xe_ref[:, halo:halo + tt, cols] * cw[CONV_W - 1:CONV_W, :]
        for k in range(1, CONV_W):
            xc = xc + xe_ref[:, halo - k:halo - k + tt, cols] * cw[CONV_W - 1 - k:CONV_W - k, :]
        a, b = _lru_gate_block(xc.reshape(nseq * tt, blk), wa_ref[h], ba_ref[:, cols],
                               wi_ref[h], bi_ref[:, cols], lam_ref[:, cols])
        a_ref[:, :, cols] = a.reshape(nseq, tt, blk)
        b_ref[:, :, cols] = b.reshape(nseq, tt, blk)
    xe_ref[:, 0:halo, :] = xb_ref[:, tt - halo:tt, :]

    def block(i, h):
        base = pl.multiple_of(i * SUBLANES, SUBLANES)
        for j in range(SUBLANES):
            row = pl.ds(base + j, 1)
            h = a_ref[:, row, :] * h + b_ref[:, row, :]
            b_ref[:, row, :] = h
        return h

    h = lax.fori_loop(0, tt // SUBLANES, block, hc_ref[...])
    hc_ref[...] = h
    hl_ref[...] = h

    out = b_ref[...] * jax.nn.gelu(gb_ref[...])
    o_ref[...] = _rms(out, gm_ref[...]).astype(BF16)


def _lru_seq(z, conv_w, conv_b, w_a, b_a, w_i, b_i, lam, g_merge, cast_w, tt):
    nseq, t_len, _ = z.shape
    d = conv_w.shape[1]
    steps = t_len // tt
    cast_blk = pl.BlockSpec((cast_w.shape[0] // steps, cast_w.shape[1]), lambda t: (t, 0))
    row = lambda v: v.reshape(1, d)
    const2 = lambda t: (0, 0)
    const3 = lambda t: (0, 0, 0)
    kern = functools.partial(_lru_seq_kernel, tt=tt)
    return pl.pallas_call(
        kern,
        out_shape=(jax.ShapeDtypeStruct((nseq, t_len, d), BF16),
                   jax.ShapeDtypeStruct((nseq, 1, d), F32),
                   jax.ShapeDtypeStruct(cast_w.shape, BF16)),
        grid=(steps,),
        in_specs=[
            pl.BlockSpec((nseq, tt, d), lambda t: (0, t, 1)),
            pl.BlockSpec((nseq, tt, d), lambda t: (0, t, 2)),
            pl.BlockSpec((CONV_W, d), const2),
            pl.BlockSpec((1, d), const2),
            pl.BlockSpec(w_a.shape, const3),
            pl.BlockSpec((1, d), const2),
            pl.BlockSpec(w_i.shape, const3),
            pl.BlockSpec((1, d), const2),
            pl.BlockSpec((1, d), const2),
            pl.BlockSpec((1, d), const2),
            cast_blk,
        ],
        out_specs=(pl.BlockSpec((nseq, tt, d), lambda t: (0, t, 0)),
                   pl.BlockSpec((nseq, 1, d), const3),
                   cast_blk),
        scratch_shapes=[pltpu.VMEM((nseq, tt + SUBLANES, d), F32),
                        pltpu.VMEM((nseq, tt, d), F32),
                        pltpu.VMEM((nseq, tt, d), F32),
                        pltpu.VMEM((nseq, 1, d), F32)],
        compiler_params=_params("arbitrary"),
        name="rglru_seq",
    )(z, z, conv_w, row(conv_b), w_a, row(b_a), w_i, row(b_i), row(lam), row(g_merge), cast_w)


def _lru_step_kernel(xb_ref, gb_ref, c0_ref, c1_ref, c2_ref, h0_ref, cw_ref, cb_ref,
                     wa_ref, ba_ref, wi_ref, bi_ref, lam_ref, gm_ref, o_ref, h_ref, cv_ref):
    d = xb_ref.shape[1]
    xb = xb_ref[...]
    cw = cw_ref[...]
    xc = (cb_ref[...] + c0_ref[...] * cw[0:1, :] + c1_ref[...] * cw[1:2, :]
          + c2_ref[...] * cw[2:3, :] + xb * cw[3:4, :])
    a, b = _lru_gates(xc, wa_ref, ba_ref[...], wi_ref, bi_ref[...], lam_ref[...])
    h = a * h0_ref[...] + b
    h_ref[...] = h
    out = h * jax.nn.gelu(gb_ref[...])
    o_ref[...] = _rms(out, gm_ref[...]).astype(BF16)
    cv_ref[:, 0:d] = c1_ref[...]
    cv_ref[:, d:2 * d] = c2_ref[...]
    cv_ref[:, 2 * d:3 * d] = xb


def _lru_step(z, conv0, h0, conv_w, conv_b, w_a, b_a, w_i, b_i, lam, g_merge):
    n = z.shape[0]
    d = conv_w.shape[1]
    taps = CONV_W - 1
    conv2d = conv0.reshape(n, taps * d)
    row = lambda v: v.reshape(1, d)
    full = lambda shape: pl.BlockSpec(shape, lambda i: (0,) * len(shape))
    col = lambda c: pl.BlockSpec((n, d), lambda i: (0, c))
    nd = full((n, d))
    rd = full((1, d))
    o, h, cv = pl.pallas_call(
        _lru_step_kernel,
        out_shape=(jax.ShapeDtypeStruct((n, d), BF16), jax.ShapeDtypeStruct((n, d), F32),
                   jax.ShapeDtypeStruct((n, taps * d), F32)),
        grid=(1,),
        in_specs=[
            col(1), col(2), col(0), col(1), col(2), nd,
            full((CONV_W, d)), rd,
            full(w_a.shape), rd, full(w_i.shape), rd, rd, rd,
        ],
        out_specs=(nd, nd, full((n, taps * d))),
        compiler_params=_params("arbitrary"),
        name="rglru_step",
    )(z, z, conv2d, conv2d, conv2d, h0, conv_w, row(conv_b),
      w_a, row(b_a), w_i, row(b_i), row(lam), row(g_merge))
    return o, h, cv.reshape(n, taps, d)


def _glu_rows(y, w_ref, b_ref, g_ref):
    gate = jax.nn.sigmoid(_dot(y.astype(BF16), w_ref[...]) + b_ref[...])
    return _rms(y.astype(F32) * gate, g_ref[...]).astype(BF16)


def _glu_kernel(y_ref, ys_ref, w_ref, b_ref, g_ref, cast_ref, o_ref, os_ref, castb_ref):
    castb_ref[...] = cast_ref[...].astype(BF16)
    for rows in _row_parts(y_ref.shape[0], 4):
        o_ref[rows, :] = _glu_rows(y_ref[rows, :], w_ref, b_ref, g_ref)

    @_on_last_row_tile
    def _():
        os_ref[...] = _glu_rows(ys_ref[...], w_ref, b_ref, g_ref)


def _glu(y, ys, w, b, g, cast_w, tm):
    m, d = y.shape
    ms = ys.shape[0]
    steps = m // tm
    const = lambda i: (0, 0)
    cast_blk = pl.BlockSpec((cast_w.shape[0] // steps, cast_w.shape[1]), lambda i: (i, 0))
    return pl.pallas_call(
        _glu_kernel,
        out_shape=(jax.ShapeDtypeStruct((m, d), BF16), jax.ShapeDtypeStruct((ms, d), BF16),
                   jax.ShapeDtypeStruct(cast_w.shape, BF16)),
        grid=(steps,),
        in_specs=[
            pl.BlockSpec((tm, d), lambda i: (i, 0)),
            pl.BlockSpec((ms, d), const),
            pl.BlockSpec((d, d), const),
            pl.BlockSpec((1, d), const),
            pl.BlockSpec((1, d), const),
            cast_blk,
        ],
        out_specs=(pl.BlockSpec((tm, d), lambda i: (i, 0)), pl.BlockSpec((ms, d), const), cast_blk),
        compiler_params=_params("arbitrary"),
        name="s5_glu",
    )(y, ys, w, b.reshape(1, d), g.reshape(1, d), cast_w)


def _out_proj_kernel(x_ref, ma_ref, mb_ref, xs_ref, mas_ref, mbs_ref, w_ref, o_ref, os_ref, wb_ref):
    @pl.when(pl.program_id(0) == 0)
    def _():
        wb_ref[...] = w_ref[...].astype(BF16)

    ka = ma_ref.shape[1]
    proj = lambda x, ma, mb: x + _dot(ma, wb_ref[:ka, :]) + _dot(mb, wb_ref[ka:, :])
    for rows in _row_parts(x_ref.shape[0], 2):
        o_ref[rows, :] = proj(x_ref[rows, :], ma_ref[rows, :], mb_ref[rows, :])

    @_on_last_row_tile
    def _():
        os_ref[...] = proj(xs_ref[...], mas_ref[...], mbs_ref[...])


def _out_proj(x, ma, mb, xs, mas, mbs, w, tm):
    m, n = x.shape
    ms = xs.shape[0]
    ka = ma.shape[1]
    kb = mb.shape[1]
    const = lambda i: (0, 0)
    tile = lambda i: (i, 0)
    return pl.pallas_call(
        _out_proj_kernel,
        out_shape=(jax.ShapeDtypeStruct((m, n), F32), jax.ShapeDtypeStruct((ms, n), F32)),
        grid=(m // tm,),
        in_specs=[
            pl.BlockSpec((tm, n), tile),
            pl.BlockSpec((tm, ka), tile),
            pl.BlockSpec((tm, kb), tile),
            pl.BlockSpec((ms, n), const),
            pl.BlockSpec((ms, ka), const),
            pl.BlockSpec((ms, kb), const),
            pl.BlockSpec((ka + kb, n), const, pipeline_mode=pl.Buffered(1)),
        ],
        out_specs=(pl.BlockSpec((tm, n), tile), pl.BlockSpec((ms, n), const)),
        scratch_shapes=[pltpu.VMEM((ka + kb, n), BF16)],
        compiler_params=_params("arbitrary"),
        name="out_proj",
    )(x, ma, mb, xs, mas, mbs, w)


def _mlp_kernel(x_ref, ma_ref, mb_ref, xs_ref, mas_ref, mbs_ref, wo_ref, g_ref, wu_ref, wd_ref,
                o_ref, os_ref, h_ref, hs_ref):
    ka = ma_ref.shape[1]

    def ffn(h):
        act = jnp.square(jnp.maximum(_dot(h, wu_ref[...]), 0.0)).astype(BF16)
        return _dot(act, wd_ref[...])

    def rows(x_ref, ma_ref, mb_ref, o_ref, h_ref):
        first = pl.program_id(1) == 0

        @pl.when(first)
        def _():
            for part in _row_parts(x_ref.shape[0], 4):
                x = (x_ref[part, :] + _dot(ma_ref[part, :], wo_ref[:ka, :])
                     + _dot(mb_ref[part, :], wo_ref[ka:, :]))
                h = _rms(x, g_ref[...]).astype(BF16)
                h_ref[part, :] = h
                o_ref[part, :] = x + ffn(h)

        @pl.when(jnp.logical_not(first))
        def _():
            o_ref[...] += ffn(h_ref[...])

    rows(x_ref, ma_ref, mb_ref, o_ref, h_ref)
    _on_last_row_tile(lambda: rows(xs_ref, mas_ref, mbs_ref, os_ref, hs_ref))


def _mlp(x, ma, mb, xs, mas, mbs, w_out, g, w_up, w_down, tm, tf):
    m, d = x.shape
    ms = xs.shape[0]
    ka, kb = ma.shape[1], mb.shape[1]
    f = w_up.shape[1]
    const = lambda i, j: (0, 0)
    tile = lambda i, j: (i, 0)
    return pl.pallas_call(
        _mlp_kernel,
        out_shape=(jax.ShapeDtypeStruct((m, d), F32), jax.ShapeDtypeStruct((ms, d), F32)),
        grid=(m // tm, f // tf),
        in_specs=[
            pl.BlockSpec((tm, d), tile),
            pl.BlockSpec((tm, ka), tile),
            pl.BlockSpec((tm, kb), tile),
            pl.BlockSpec((ms, d), const),
            pl.BlockSpec((ms, ka), const),
            pl.BlockSpec((ms, kb), const),
            pl.BlockSpec((ka + kb, d), const, pipeline_mode=pl.Buffered(1)),
            pl.BlockSpec((1, d), const),
            pl.BlockSpec((d, tf), lambda i, j: (0, j)),
            pl.BlockSpec((tf, d), lambda i, j: (j, 0)),
        ],
        out_specs=(pl.BlockSpec((tm, d), tile), pl.BlockSpec((ms, d), const)),
        scratch_shapes=[pltpu.VMEM((tm, d), BF16), pltpu.VMEM((ms, d), BF16)],
        compiler_params=_params("arbitrary", "arbitrary"),
        name="mlp",
    )(x, ma, mb, xs, mas, mbs, w_out, g.reshape(1, d), w_up, w_down)


def _ple_kernel(x_ref, p_ref, xs_ref, ps_ref, g_ref, wg_ref, wp_ref, gf_ref, o_ref, os_ref,
                wgb_ref, wpb_ref, *, final):
    @pl.when(pl.program_id(0) == 0)
    def _():
        wgb_ref[...] = wg_ref[...].astype(BF16)
        wpb_ref[...] = wp_ref[...].astype(BF16)

    def ple(x, p):
        gate = jax.nn.sigmoid(_dot(_rms(x, g_ref[...]).astype(BF16), wgb_ref[...]))
        pe = _dot(p.astype(BF16), wpb_ref[...])
        x = x + pe * gate
        return _rms(x, gf_ref[...]) if final else x

    for rows in _row_parts(x_ref.shape[0], 2):
        o_ref[rows, :] = ple(x_ref[rows, :], p_ref[rows, :])

    @_on_last_row_tile
    def _():
        os_ref[...] = ple(xs_ref[...], ps_ref[...])


def _ple(x, p, xs, ps, g, w_gate, w_ple, g_final, tm, final):
    m, d = x.shape
    ms = xs.shape[0]
    dp = p.shape[1]
    once = pl.Buffered(1)
    const = lambda i: (0, 0)
    tile = lambda i: (i, 0)
    return pl.pallas_call(
        functools.partial(_ple_kernel, final=final),
        out_shape=(jax.ShapeDtypeStruct((m, d), F32), jax.ShapeDtypeStruct((ms, d), F32)),
        grid=(m // tm,),
        in_specs=[
            pl.BlockSpec((tm, d), tile),
            pl.BlockSpec((tm, dp), tile),
            pl.BlockSpec((ms, d), const),
            pl.BlockSpec((ms, dp), const),
            pl.BlockSpec((1, d), const),
            pl.BlockSpec((d, d), const, pipeline_mode=once),
            pl.BlockSpec((dp, d), const, pipeline_mode=once),
            pl.BlockSpec((1, d), const),
        ],
        out_specs=(pl.BlockSpec((tm, d), tile), pl.BlockSpec((ms, d), const)),
        scratch_shapes=[pltpu.VMEM((d, d), BF16), pltpu.VMEM((dp, d), BF16)],
        compiler_params=_params("arbitrary"),
        name="ple_final",
    )(x, p, xs, ps, g.reshape(1, d), w_gate, w_ple, g_final.reshape(1, d))


TT_LRU = 64
TM_IN, TN_IN = 1024, 1536
TM_GLU = 1024
TM_OUT = 512
TM_MLP, TF_MLP = 512, 1024
TM_PLE = 512


def kernel(x_prompt, x_sample, state_s5_re, state_s5_im, state_lru, state_conv, p_prompt, p_sample,
           g_mix, w_in, s5_lam_re, s5_lam_im, s5_log_step, s5_b_re, s5_b_im, s5_c_re, s5_c_im, s5_d,
           s5_w_glu, s5_b_glu, conv_w, conv_b, lru_w_a, lru_b_a, lru_w_i, lru_b_i, lru_lam,
           g_merge_a, g_merge_b, w_out, g_mlp, w_up, w_down, g_ple, w_ple_gate, w_ple, g_final):
    depth = g_mix.shape[0]
    nb, t_len, d_model = x_prompt.shape
    ns = x_sample.shape[0]
    d_s5 = s5_d.shape[1]
    d_lru = conv_w.shape[2]
    groups = d_s5 // S5_CH

    xp = x_prompt.reshape(nb * t_len, d_model)
    xs = x_sample.reshape(ns, d_model)
    outs = [[] for _ in range(8)]
    for l in range(depth):
        final = l == depth - 1
        w_glu_b = s5_w_glu[l].astype(BF16)
        lru_args = (conv_w[l], conv_b[l], lru_w_a[l].astype(BF16), lru_b_a[l],
                    lru_w_i[l].astype(BF16), lru_b_i[l], lru_lam[l], g_merge_b[l])
        w_t, w_s, w_c, tab, wb, wct, arow, w_in_b = _s5_params(
            s5_lam_re[l], s5_lam_im[l], s5_log_step[l], s5_b_re[l], s5_b_im[l],
            s5_c_re[l], s5_c_im[l], s5_d[l], w_in[l])

        z, zs = _norm_matmul(xp, xs, g_mix[l], w_in_b, TM_IN, TN_IN)
        z4 = z.reshape(nb, t_len, 3 * d_s5)
        y, hf, w_down_b = _s5_seq(z4, w_t, w_s, w_c, tab, d_s5, w_down[l])
        mb, lru_h, w_up_b = _lru_seq(z4, *lru_args, w_up[l], TT_LRU)
        ys, hsr, hsi = _s5_step(zs, state_s5_re[l].reshape(ns, groups * S5_STATE),
                                state_s5_im[l].reshape(ns, groups * S5_STATE),
                                wb, wct, arow, s5_d[l].reshape(1, d_s5))
        mbs, lru_hs, conv_s = _lru_step(zs, state_conv[l], state_lru[l], *lru_args)
        ma, mas, w_out_b = _glu(y.reshape(nb * t_len, d_s5), ys, w_glu_b, s5_b_glu[l], g_merge_a[l],
                                w_out[l], TM_GLU)
        xp, xs = _mlp(xp, ma, mb.reshape(nb * t_len, d_lru), xs, mas, mbs, w_out_b, g_mlp[l],
                      w_up_b, w_down_b, TM_MLP, TF_MLP)
        xp, xs = _ple(xp, p_prompt[l].reshape(nb * t_len, -1), xs, p_sample[l].reshape(ns, -1),
                      g_ple[l], w_ple_gate[l], w_ple[l], g_final, TM_PLE, final)
        hf = jnp.transpose(hf[:, :, :, :nb], (1, 3, 0, 2))
        outs[0].append(hf[0])
        outs[1].append(hf[1])
        outs[2].append(lru_h.reshape(nb, d_lru))
        outs[3].append(z4[:, t_len - (CONV_W - 1):, d_s5:d_s5 + d_lru])
        outs[4].append(hsr.reshape(ns, groups, S5_STATE))
        outs[5].append(hsi.reshape(ns, groups, S5_STATE))
        outs[6].append(lru_hs)
        outs[7].append(conv_s)
    return (xp.reshape(nb, t_len, d_model), xs.reshape(ns, 1, d_model),
            *(jnp.stack(o) for o in outs))
```

```python
import functools

import jax
import jax.numpy as jnp
from jax import lax
from jax.experimental import pallas as pl
from jax.experimental.pallas import tpu as pltpu

F32 = jnp.float32
BF16 = jnp.bfloat16
HIGHEST = lax.Precision.HIGHEST

EPS = 1e-6
LRU_C = 8.0
S5_CH = 16
S5_STATE = 64
LRU_HEADS = 4
CONV_W = 4
CHUNK = 16
SUBLANES = 8
LANES = 128
LANE_GROUPS = LANES // S5_CH
LANE_PAIRS = LANE_GROUPS // 2
ROWS = CHUNK * S5_CH
POW_ROWS = 32
SCAN_ROW0 = POW_ROWS
VMEM_LIMIT = 56 * 1024 * 1024


def _params(*sem):
    return pltpu.CompilerParams(dimension_semantics=sem, vmem_limit_bytes=VMEM_LIMIT)


def _rms(x, g):
    return x * lax.rsqrt(jnp.mean(x * x, axis=-1, keepdims=True) + EPS) * g


def _dot(a, b):
    return jnp.dot(a, b, preferred_element_type=F32)


def _dot_nt(a, b, precision=None):
    return lax.dot_general(a, b, (((1,), (1,)), ((), ())), precision=precision,
                           preferred_element_type=F32)


def _transpose_tiles(x):
    r, c = x.shape
    return jnp.concatenate(
        [jnp.concatenate([x[i:i + LANES, j:j + LANES].T for i in range(0, r, LANES)], axis=1)
         for j in range(0, c, LANES)], axis=0)


def _row_parts(tm, want):
    parts = max(1, min(want, tm // LANES))
    step = tm // parts
    return [slice(r * step, (r + 1) * step) for r in range(parts)]


def _on_last_row_tile(fn):
    pl.when(pl.program_id(0) == pl.num_programs(0) - 1)(fn)


def _norm_matmul_kernel(x_ref, xs_ref, g_ref, w_ref, o_ref, os_ref):
    tn = o_ref.shape[1]
    w = w_ref[:, pl.ds(pl.multiple_of(pl.program_id(1) * tn, tn), tn)]
    for rows in _row_parts(x_ref.shape[0], 8):
        h = _rms(x_ref[rows, :], g_ref[...]).astype(BF16)
        o_ref[rows, :] = _dot(h, w)

    @_on_last_row_tile
    def _():
        os_ref[...] = _dot(_rms(xs_ref[...], g_ref[...]).astype(BF16), w)


def _norm_matmul(x, xs, g, w, tm, tn):
    m, k = x.shape
    ms = xs.shape[0]
    n = w.shape[1]
    ni = m // tm
    return pl.pallas_call(
        _norm_matmul_kernel,
        out_shape=(jax.ShapeDtypeStruct((m, n), F32), jax.ShapeDtypeStruct((ms, n), F32)),
        grid=(ni, n // tn),
        in_specs=[
            pl.BlockSpec((tm, k), lambda i, j: (i, 0)),
            pl.BlockSpec((ms, k), lambda i, j: (0, 0)),
            pl.BlockSpec((1, k), lambda i, j: (0, 0)),
            pl.BlockSpec((k, n), lambda i, j: (0, 0), pipeline_mode=pl.Buffered(1)),
        ],
        out_specs=(pl.BlockSpec((tm, tn), lambda i, j: (i, j)),
                   pl.BlockSpec((ms, tn), lambda i, j: (0, jnp.where(i == ni - 1, j, 0)))),
        compiler_params=_params("arbitrary", "arbitrary"),
        name="norm_w_in",
    )(x, xs, g.reshape(1, k), w)


def _s5_param_kernel(lam_ref, bc_ref, d_ref, cast_ref,
                     wt_ref, ws_ref, wc_ref, tab_ref, wb_ref, wct_ref, arow_ref, castb_ref):
    castb_ref[...] = cast_ref[...].astype(BF16)
    row_p = lax.broadcasted_iota(jnp.int32, (POW_ROWS, LANES), 0)
    lane = lax.broadcasted_iota(jnp.int32, (ROWS, LANES), 1)
    lane16 = lax.broadcasted_iota(jnp.int32, (S5_CH, LANES), 1)
    row16 = lax.broadcasted_iota(jnp.int32, (S5_CH, LANES), 0)
    lo = lane < S5_STATE

    wb_ref[...] = jnp.zeros_like(wb_ref)
    wct_ref[...] = jnp.zeros_like(wct_ref)

    for q in range(LANE_PAIRS):
        lr, li = lam_ref[0, q], lam_ref[1, q]
        step = jnp.exp(lam_ref[2, q])
        mag = jnp.exp(lr * step)
        ar = mag * jnp.cos(li * step)
        ai = mag * jnp.sin(li * step)
        nr, ni = ar - 1.0, ai
        den = lr * lr + li * li
        cr = (nr * lr + ni * li) / den
        ci = (ni * lr - nr * li) / den
        bt_r, bt_i = bc_ref[0, q], bc_ref[1, q]
        bb_r = cr * bt_r - ci * bt_i
        bb_i = cr * bt_i + ci * bt_r
        c_r, c_i = bc_ref[2, q], bc_ref[3, q]

        pr = jnp.ones((POW_ROWS, LANES), F32)
        pi = jnp.zeros((POW_ROWS, LANES), F32)
        sr, si = ar, ai
        sq = []
        for m in range(POW_ROWS.bit_length() - 1 + SUBLANES):
            sq.append((sr, si))
            if (1 << m) < POW_ROWS:
                bit = ((row_p >> m) & 1) == 1
                pr, pi = (jnp.where(bit, pr * sr - pi * si, pr),
                          jnp.where(bit, pr * si + pi * sr, pi))
            sr, si = sr * sr - si * si, 2.0 * sr * si
        m0 = CHUNK.bit_length() - 1
        scan_r = jnp.concatenate([sq[m0 + j][0] for j in range(SUBLANES)], axis=0)
        scan_i = jnp.concatenate([sq[m0 + j][1] for j in range(SUBLANES)], axis=0)
        pad = jnp.zeros((LANES - POW_ROWS - SUBLANES, LANES), F32)
        tab_ref[q, 0] = jnp.concatenate([pr, scan_r, pad], axis=0).T
        tab_ref[q, 1] = jnp.concatenate([pi, scan_i, pad], axis=0).T
        arow_ref[q, 0] = pr[:SUBLANES]
        arow_ref[q, 1] = pi[:SUBLANES]

        def expand(tab, k0, sign):
            return jnp.concatenate(
                [jnp.broadcast_to(tab[k0 + sign * s:k0 + sign * s + 1, :], (S5_CH, LANES))
                 for s in range(CHUNK)], axis=0)

        tile = lambda v: jnp.concatenate([v] * CHUNK, axis=0)

        e_r, e_i = expand(pr, CHUNK - 1, -1), expand(pi, CHUNK - 1, -1)
        t_r, t_i = tile(bb_r), tile(bb_i)
        ws_r = _transpose_tiles(t_r * e_r - t_i * e_i)
        ws_i = _transpose_tiles(t_r * e_i + t_i * e_r)
        for h in range(2):
            rows = slice(h * S5_STATE, (h + 1) * S5_STATE)
            ws_ref[2 * q + h] = jnp.concatenate([ws_r[rows], ws_i[rows]], axis=0).astype(BF16)

        e_r, e_i = expand(pr, 1, 1), expand(pi, 1, 1)
        t_r, t_i = tile(c_r), tile(c_i)
        ca_r = t_r * e_r - t_i * e_i
        ca_n = -(t_r * e_i + t_i * e_r)
        wc_ref[2 * q] = jnp.where(lo, ca_r, pltpu.roll(ca_n, S5_STATE, 1)).astype(BF16)
        wc_ref[2 * q + 1] = jnp.where(lo, pltpu.roll(ca_r, S5_STATE, 1), ca_n).astype(BF16)

        e_r, e_i = expand(pr, 0, 1), expand(pi, 0, 1)
        c0_r = t_r * e_r - t_i * e_i
        c0_i = t_r * e_i + t_i * e_r

        for h in range(2):
            g = 2 * q + h
            mine = (lane16 < S5_STATE) == (h == 0)
            bm_r = jnp.where(mine, bb_r, 0.0)
            bm_i = jnp.where(mine, bb_i, 0.0)
            kt = _dot_nt(bm_r, c0_r, HIGHEST) - _dot_nt(bm_i, c0_i, HIGHEST)
            k_lo = kt[:, :LANES] + jnp.where(row16 == lane16, d_ref[g], 0.0)
            k_hi = kt[:, LANES:]
            blocks = []
            half = LANES // S5_CH
            for s in range(CHUNK):
                sh = (s % half) * S5_CH
                keep = lane16 >= sh
                r_lo = pltpu.roll(k_lo, sh, 1) if sh else k_lo
                r_hi = pltpu.roll(k_hi, sh, 1) if sh else k_hi
                if s < half:
                    blk = jnp.concatenate([jnp.where(keep, r_lo, 0.0),
                                           jnp.where(keep, r_hi, r_lo)], axis=1)
                else:
                    blk = jnp.concatenate([jnp.zeros_like(r_lo), jnp.where(keep, r_lo, 0.0)], axis=1)
                blocks.append(blk)
            wt_ref[g] = _transpose_tiles(jnp.concatenate(blocks, axis=0)).astype(BF16)

            r0 = (2 * q + h) * S5_CH
            c0 = q * 2 * LANES
            wb_ref[r0:r0 + S5_CH, c0:c0 + LANES] = bm_r.astype(BF16)
            wb_ref[r0:r0 + S5_CH, c0 + LANES:c0 + 2 * LANES] = bm_i.astype(BF16)
            wct_ref[q, r0:r0 + S5_CH, :LANES] = jnp.where(mine, c_r, 0.0).astype(BF16)
            wct_ref[q, r0:r0 + S5_CH, LANES:] = jnp.where(mine, -c_i, 0.0).astype(BF16)


def _s5_params(lam_re, lam_im, log_step, b_re, b_im, c_re, c_im, d, cast_w):
    groups = lam_re.shape[0]
    pairs = groups // 2
    slabs = groups // LANE_GROUPS
    cast_blk = pl.BlockSpec((cast_w.shape[0] // slabs, cast_w.shape[1]), lambda i: (i, 0))
    ls = jnp.broadcast_to(log_step[:, None], (groups, S5_STATE))
    lam = jnp.stack([lam_re, lam_im, ls]).reshape(3, pairs, 1, LANES)
    bc = jnp.stack([jnp.transpose(b_re, (0, 2, 1)), jnp.transpose(b_im, (0, 2, 1)), c_re, c_im])
    bc = jnp.transpose(bc.reshape(4, pairs, 2, S5_CH, S5_STATE), (0, 1, 3, 2, 4)).reshape(
        4, pairs, S5_CH, LANES)
    d_pad = jnp.pad(d.reshape(groups, 1, S5_CH), ((0, 0), (0, 0), (0, LANES - S5_CH)))
    blk3 = lambda i: (i, 0, 0)
    blk4 = lambda i: (i, 0, 0, 0)
    pblk = lambda i: (0, i, 0, 0)
    return pl.pallas_call(
        _s5_param_kernel,
        out_shape=(jax.ShapeDtypeStruct((groups, ROWS, ROWS), BF16),
                   jax.ShapeDtypeStruct((groups, 2 * S5_STATE, ROWS), BF16),
                   jax.ShapeDtypeStruct((groups, ROWS, 2 * S5_STATE), BF16),
                   jax.ShapeDtypeStruct((pairs, 2, LANES, LANES), F32),
                   jax.ShapeDtypeStruct((slabs, LANES, LANE_PAIRS * 2 * LANES), BF16),
                   jax.ShapeDtypeStruct((slabs, LANE_PAIRS, LANES, 2 * LANES), BF16),
                   jax.ShapeDtypeStruct((pairs, 2, SUBLANES, LANES), F32),
                   jax.ShapeDtypeStruct(cast_w.shape, BF16)),
        grid=(slabs,),
        in_specs=[pl.BlockSpec((3, LANE_PAIRS, 1, LANES), pblk),
                  pl.BlockSpec((4, LANE_PAIRS, S5_CH, LANES), pblk),
                  pl.BlockSpec((LANE_GROUPS, 1, LANES), blk3), cast_blk],
        out_specs=(pl.BlockSpec((LANE_GROUPS, ROWS, ROWS), blk3),
                   pl.BlockSpec((LANE_GROUPS, 2 * S5_STATE, ROWS), blk3),
                   pl.BlockSpec((LANE_GROUPS, ROWS, 2 * S5_STATE), blk3),
                   pl.BlockSpec((LANE_PAIRS, 2, LANES, LANES), blk4),
                   pl.BlockSpec((None, LANES, LANE_PAIRS * 2 * LANES), blk3),
                   pl.BlockSpec((None, LANE_PAIRS, LANES, 2 * LANES), blk4),
                   pl.BlockSpec((LANE_PAIRS, 2, SUBLANES, LANES), blk4),
                   cast_blk),
        compiler_params=_params("parallel"),
        name="s5_params",
    )(lam, bc, d_pad, cast_w)


def _s5_seq_kernel(z_ref, wt_ref, ws_ref, wc_ref, tab_ref, cast_ref, y_ref, hf_ref, castb_ref,
                   ut_ref, yt_ref, sr_ref, si_ref, *, nb, n_chunks):
    castb_ref[...] = cast_ref[...].astype(BF16)
    scan_steps = n_chunks.bit_length() - 1
    for n in range(nb):
        for s in range(CHUNK):
            xs = z_ref[n, pl.ds(s, n_chunks, stride=CHUNK), :]
            ut_ref[:, s * S5_CH:(s + 1) * S5_CH, n * n_chunks:(n + 1) * n_chunks] = (
                xs.T.astype(BF16).reshape(LANE_GROUPS, S5_CH, n_chunks))

    for g in range(LANE_GROUPS):
        st = _dot(ws_ref[g], ut_ref[g])
        sr_ref[g * S5_STATE:(g + 1) * S5_STATE, :] = st[:S5_STATE]
        si_ref[g * S5_STATE:(g + 1) * S5_STATE, :] = st[S5_STATE:]

    srows = LANE_GROUPS * S5_STATE
    lane = lax.broadcasted_iota(jnp.int32, (srows, n_chunks), 1)
    tab_r = jnp.concatenate([tab_ref[q, 0] for q in range(LANE_PAIRS)], axis=0)
    tab_i = jnp.concatenate([tab_ref[q, 1] for q in range(LANE_PAIRS)], axis=0)
    fin_r = jnp.zeros((srows, n_chunks), F32)
    fin_i = jnp.zeros((srows, n_chunks), F32)
    for n in range(nb):
        cols = slice(n * n_chunks, (n + 1) * n_chunks)
        xr = sr_ref[:, cols]
        xi = si_ref[:, cols]
        for k in range(scan_steps):
            d = 1 << k
            keep = lane >= d
            sr = jnp.where(keep, pltpu.roll(xr, d, 1), 0.0)
            si = jnp.where(keep, pltpu.roll(xi, d, 1), 0.0)
            pr = tab_r[:, SCAN_ROW0 + k:SCAN_ROW0 + k + 1]
            pi = tab_i[:, SCAN_ROW0 + k:SCAN_ROW0 + k + 1]
            xr, xi = xr + (pr * sr - pi * si), xi + (pr * si + pi * sr)
        last = lane == n_chunks - 1
        fin_r = fin_r + jnp.where(
            lane == n, jnp.sum(jnp.where(last, xr, 0.0), axis=1, keepdims=True), 0.0)
        fin_i = fin_i + jnp.where(
            lane == n, jnp.sum(jnp.where(last, xi, 0.0), axis=1, keepdims=True), 0.0)
        keep = lane >= 1
        sr_ref[:, cols] = jnp.where(keep, pltpu.roll(xr, 1, 1), 0.0)
        si_ref[:, cols] = jnp.where(keep, pltpu.roll(xi, 1, 1), 0.0)

    for g in range(LANE_GROUPS):
        rows = slice(g * S5_STATE, (g + 1) * S5_STATE)
        hf_ref[g, 0] = fin_r[rows]
        hf_ref[g, 1] = fin_i[rows]
        hs = jnp.concatenate([sr_ref[rows, :], si_ref[rows, :]], axis=0).astype(BF16)
        yt_ref[g] = jax.nn.gelu(_dot(wc_ref[g], hs) + _dot(wt_ref[g], ut_ref[g]))

    for n in range(nb):
        for s in range(CHUNK):
            blk = yt_ref[:, s * S5_CH:(s + 1) * S5_CH, n * n_chunks:(n + 1) * n_chunks]
            y_ref[n, pl.ds(s, n_chunks, stride=CHUNK), :] = blk.reshape(LANES, n_chunks).T


def _s5_seq(z, w_t, w_s, w_c, tab, d_s5, cast_w):
    nb, t_len, _ = z.shape
    n_chunks = t_len // CHUNK
    assert n_chunks == LANES, "the chunk axis must fill one 128-lane tile"
    groups = d_s5 // S5_CH
    steps = groups // LANE_GROUPS
    cast_blk = pl.BlockSpec((cast_w.shape[0] // steps, cast_w.shape[1]), lambda i: (i, 0))
    kern = functools.partial(_s5_seq_kernel, nb=nb, n_chunks=n_chunks)
    blk3 = lambda i: (i, 0, 0)
    blk4 = lambda i: (i, 0, 0, 0)
    return pl.pallas_call(
        kern,
        out_shape=(jax.ShapeDtypeStruct((nb, t_len, d_s5), F32),
                   jax.ShapeDtypeStruct((groups, 2, S5_STATE, LANES), F32),
                   jax.ShapeDtypeStruct(cast_w.shape, BF16)),
        grid=(steps,),
        in_specs=[
            pl.BlockSpec((nb, t_len, LANES), lambda i: (0, 0, i)),
            pl.BlockSpec((LANE_GROUPS, ROWS, ROWS), blk3),
            pl.BlockSpec((LANE_GROUPS, 2 * S5_STATE, ROWS), blk3),
            pl.BlockSpec((LANE_GROUPS, ROWS, 2 * S5_STATE), blk3),
            pl.BlockSpec((LANE_PAIRS, 2, LANES, LANES), blk4),
            cast_blk,
        ],
        out_specs=(pl.BlockSpec((nb, t_len, LANES), lambda i: (0, 0, i)),
                   pl.BlockSpec((LANE_GROUPS, 2, S5_STATE, LANES), blk4),
                   cast_blk),
        scratch_shapes=[pltpu.VMEM((LANE_GROUPS, ROWS, nb * n_chunks), BF16),
                        pltpu.VMEM((LANE_GROUPS, ROWS, nb * n_chunks), F32),
                        pltpu.VMEM((LANE_GROUPS * S5_STATE, nb * n_chunks), F32),
                        pltpu.VMEM((LANE_GROUPS * S5_STATE, nb * n_chunks), F32)],
        compiler_params=_params("parallel"),
        name="s5_seq",
    )(z, w_t, w_s, w_c, tab, cast_w)


def _s5_step_kernel(u_ref, h0r_ref, h0i_ref, wb_ref, wct_ref, arow_ref, d_ref,
                    y_ref, hr_ref, hi_ref):
    u = u_ref[...]
    bu = _dot(u.astype(BF16), wb_ref[...])
    y = d_ref[...] * u
    for q in range(LANE_PAIRS):
        cols = slice(q * LANES, (q + 1) * LANES)
        ar = arow_ref[q, 0, 1:2, :]
        ai = arow_ref[q, 1, 1:2, :]
        h0r, h0i = h0r_ref[:, cols], h0i_ref[:, cols]
        hr = ar * h0r - ai * h0i + bu[:, 2 * q * LANES:(2 * q + 1) * LANES]
        hi = ar * h0i + ai * h0r + bu[:, (2 * q + 1) * LANES:(2 * q + 2) * LANES]
        hr_ref[:, cols] = hr
        hi_ref[:, cols] = hi
        y = y + _dot_nt(jnp.concatenate([hr, hi], axis=1).astype(BF16), wct_ref[q])
    y_ref[...] = jax.nn.gelu(y).astype(BF16)


def _s5_step(z, h0r, h0i, wb, wct, arow, d):
    n = z.shape[0]
    slabs = wb.shape[0]
    sw = LANE_PAIRS * LANES
    col = lambda i: (0, i)
    blk3 = lambda i: (i, 0, 0)
    blk4 = lambda i: (i, 0, 0, 0)
    return pl.pallas_call(
        _s5_step_kernel,
        out_shape=(jax.ShapeDtypeStruct((n, slabs * LANES), BF16),
                   jax.ShapeDtypeStruct((n, slabs * sw), F32),
                   jax.ShapeDtypeStruct((n, slabs * sw), F32)),
        grid=(slabs,),
        in_specs=[
            pl.BlockSpec((n, LANES), col),
            pl.BlockSpec((n, sw), col),
            pl.BlockSpec((n, sw), col),
            pl.BlockSpec((None, LANES, 2 * sw), blk3),
            pl.BlockSpec((None, LANE_PAIRS, LANES, 2 * LANES), blk4),
            pl.BlockSpec((LANE_PAIRS, 2, SUBLANES, LANES), blk4),
            pl.BlockSpec((1, LANES), col),
        ],
        out_specs=(pl.BlockSpec((n, LANES), col),
                   pl.BlockSpec((n, sw), col),
                   pl.BlockSpec((n, sw), col)),
        compiler_params=_params("parallel"),
        name="s5_step",
    )(z, h0r, h0i, wb, wct, arow, d)


def _lru_gate_block(xc, wa, ba, wi, bi, lam):
    xb16 = xc.astype(BF16)
    r = jax.nn.sigmoid(_dot(xb16, wa) + ba)
    ig = jax.nn.sigmoid(_dot(xb16, wi) + bi)
    log_a = -LRU_C * r * jax.nn.softplus(-lam)
    a = jnp.exp(log_a)
    mult = jnp.sqrt(1.0 - a * a)
    return a, mult * (ig * xc)


def _lru_gates(xc, wa_ref, ba, wi_ref, bi, lam):
    blk = xc.shape[1] // LRU_HEADS
    parts = [_lru_gate_block(xc[:, h * blk:(h + 1) * blk], wa_ref[h], ba[:, h * blk:(h + 1) * blk],
                             wi_ref[h], bi[:, h * blk:(h + 1) * blk],
                             lam[:, h * blk:(h + 1) * blk]) for h in range(LRU_HEADS)]
    return (jnp.concatenate([p[0] for p in parts], axis=-1),
            jnp.concatenate([p[1] for p in parts], axis=-1))


def _lru_seq_kernel(xb_ref, gb_ref, cw_ref, cb_ref, wa_ref, ba_ref, wi_ref, bi_ref, lam_ref,
                    gm_ref, cast_ref, o_ref, hl_ref, castb_ref, xe_ref, a_ref, b_ref, hc_ref,
                    *, tt):
    castb_ref[...] = cast_ref[...].astype(BF16)
    halo = SUBLANES
    nseq, _, d = xb_ref.shape
    blk = d // LRU_HEADS
    pitch = a_ref.shape[1] // nseq

    @pl.when(pl.program_id(0) == 0)
    def _():
        xe_ref[:, 0:halo, :] = jnp.zeros((nseq, halo, d), F32)
        hc_ref[...] = jnp.zeros_like(hc_ref)

    xe_ref[:, halo:halo + tt, :] = xb_ref[...]
    for h in range(LRU_HEADS):
        cols = slice(h * blk, (h + 1) * blk)
        cw = cw_ref[:, cols]
        xc = cb_ref[:, cols] + xe_ref[:, halo:halo + tt, cols] * cw[CONV_W - 1:CONV_W, :]
        for k in range(1, CONV_W):
            xc = xc + xe_ref[:, halo - k:halo - k + tt, cols] * cw[CONV_W - 1 - k:CONV_W - k, :]
        a, b = _lru_gate_block(xc.reshape(nseq * tt, blk), wa_ref[h], ba_ref[:, cols],
                               wi_ref[h], bi_ref[:, cols], lam_ref[:, cols])
        for n in range(nseq):
            for s in range(blk // LANES):
                slab = h * (blk // LANES) + s
                rows = slice(n * pitch, n * pitch + tt)
                a_ref[slab, rows, :] = a[n * tt:(n + 1) * tt, s * LANES:(s + 1) * LANES]
                b_ref[slab, rows, :] = b[n * tt:(n + 1) * tt, s * LANES:(s + 1) * LANES]
    xe_ref[:, 0:halo, :] = xb_ref[:, tt - halo:tt, :]

    slabs = d // LANES

    def block(i, hs):
        base = pl.multiple_of(i * SUBLANES, SUBLANES)
        hs = list(hs)
        for j in range(SUBLANES):
            rows = pl.ds(base + j, nseq, stride=pitch)
            for s in range(slabs):
                hs[s] = a_ref[s, rows, :] * hs[s] + b_ref[s, rows, :]
                b_ref[s, rows, :] = hs[s]
        return tuple(hs)

    hs = lax.fori_loop(0, tt // SUBLANES, block, tuple(hc_ref[s] for s in range(slabs)))
    for s in range(slabs):
        hc_ref[s] = hs[s]
        hl_ref[s] = hs[s]

    h_all = jnp.stack([jnp.concatenate([b_ref[s, n * pitch:n * pitch + tt, :] for s in range(slabs)],
                                       axis=-1) for n in range(nseq)])
    out = h_all * jax.nn.gelu(gb_ref[...])
    o_ref[...] = _rms(out, gm_ref[...]).astype(BF16)


def _lru_seq(z, conv_w, conv_b, w_a, b_a, w_i, b_i, lam, g_merge, cast_w, tt):
    nseq, t_len, _ = z.shape
    d = conv_w.shape[1]
    steps = t_len // tt
    cast_blk = pl.BlockSpec((cast_w.shape[0] // steps, cast_w.shape[1]), lambda t: (t, 0))
    row = lambda v: v.reshape(1, d)
    const2 = lambda t: (0, 0)
    const3 = lambda t: (0, 0, 0)
    kern = functools.partial(_lru_seq_kernel, tt=tt)
    return pl.pallas_call(
        kern,
        out_shape=(jax.ShapeDtypeStruct((nseq, t_len, d), BF16),
                   jax.ShapeDtypeStruct((d // LANES, nseq, LANES), F32),
                   jax.ShapeDtypeStruct(cast_w.shape, BF16)),
        grid=(steps,),
        in_specs=[
            pl.BlockSpec((nseq, tt, d), lambda t: (0, t, 1)),
            pl.BlockSpec((nseq, tt, d), lambda t: (0, t, 2)),
            pl.BlockSpec((CONV_W, d), const2),
            pl.BlockSpec((1, d), const2),
            pl.BlockSpec(w_a.shape, const3),
            pl.BlockSpec((1, d), const2),
            pl.BlockSpec(w_i.shape, const3),
            pl.BlockSpec((1, d), const2),
            pl.BlockSpec((1, d), const2),
            pl.BlockSpec((1, d), const2),
            cast_blk,
        ],
        out_specs=(pl.BlockSpec((nseq, tt, d), lambda t: (0, t, 0)),
                   pl.BlockSpec((d // LANES, nseq, LANES), const3),
                   cast_blk),
        scratch_shapes=[pltpu.VMEM((nseq, tt + SUBLANES, d), F32),
                        pltpu.VMEM((d // LANES, nseq * (tt + SUBLANES), LANES), F32),
                        pltpu.VMEM((d // LANES, nseq * (tt + SUBLANES), LANES), F32),
                        pltpu.VMEM((d // LANES, nseq, LANES), F32)],
        compiler_params=_params("arbitrary"),
        name="rglru_seq",
    )(z, z, conv_w, row(conv_b), w_a, row(b_a), w_i, row(b_i), row(lam), row(g_merge), cast_w)


def _lru_step_kernel(xb_ref, gb_ref, c0_ref, c1_ref, c2_ref, h0_ref, cw_ref, cb_ref,
                     wa_ref, ba_ref, wi_ref, bi_ref, lam_ref, gm_ref, o_ref, h_ref, cv_ref):
    d = xb_ref.shape[1]
    xb = xb_ref[...]
    cw = cw_ref[...]
    xc = (cb_ref[...] + c0_ref[...] * cw[0:1, :] + c1_ref[...] * cw[1:2, :]
          + c2_ref[...] * cw[2:3, :] + xb * cw[3:4, :])
    a, b = _lru_gates(xc, wa_ref, ba_ref[...], wi_ref, bi_ref[...], lam_ref[...])
    h = a * h0_ref[...] + b
    h_ref[...] = h
    out = h * jax.nn.gelu(gb_ref[...])
    o_ref[...] = _rms(out, gm_ref[...]).astype(BF16)
    cv_ref[:, 0:d] = c1_ref[...]
    cv_ref[:, d:2 * d] = c2_ref[...]
    cv_ref[:, 2 * d:3 * d] = xb


def _lru_step(z, conv0, h0, conv_w, conv_b, w_a, b_a, w_i, b_i, lam, g_merge):
    n = z.shape[0]
    d = conv_w.shape[1]
    taps = CONV_W - 1
    conv2d = conv0.reshape(n, taps * d)
    row = lambda v: v.reshape(1, d)
    full = lambda shape: pl.BlockSpec(shape, lambda i: (0,) * len(shape))
    col = lambda c: pl.BlockSpec((n, d), lambda i: (0, c))
    nd = full((n, d))
    rd = full((1, d))
    o, h, cv = pl.pallas_call(
        _lru_step_kernel,
        out_shape=(jax.ShapeDtypeStruct((n, d), BF16), jax.ShapeDtypeStruct((n, d), F32),
                   jax.ShapeDtypeStruct((n, taps * d), F32)),
        grid=(1,),
        in_specs=[
            col(1), col(2), col(0), col(1), col(2), nd,
            full((CONV_W, d)), rd,
            full(w_a.shape), rd, full(w_i.shape), rd, rd, rd,
        ],
        out_specs=(nd, nd, full((n, taps * d))),
        compiler_params=_params("arbitrary"),
        name="rglru_step",
    )(z, z, conv2d, conv2d, conv2d, h0, conv_w, row(conv_b),
      w_a, row(b_a), w_i, row(b_i), row(lam), row(g_merge))
    return o, h, cv.reshape(n, taps, d)


def _glu_rows(y, w_ref, b_ref, g_ref):
    gate = jax.nn.sigmoid(_dot(y.astype(BF16), w_ref[...]) + b_ref[...])
    return _rms(y.astype(F32) * gate, g_ref[...]).astype(BF16)


def _glu_kernel(y_ref, ys_ref, w_ref, b_ref, g_ref, o_ref, os_ref):
    for rows in _row_parts(y_ref.shape[0], 4):
        o_ref[rows, :] = _glu_rows(y_ref[rows, :], w_ref, b_ref, g_ref)

    @_on_last_row_tile
    def _():
        os_ref[...] = _glu_rows(ys_ref[...], w_ref, b_ref, g_ref)


def _glu(y, ys, w, b, g, tm):
    m, d = y.shape
    ms = ys.shape[0]
    const = lambda i: (0, 0)
    return pl.pallas_call(
        _glu_kernel,
        out_shape=(jax.ShapeDtypeStruct((m, d), BF16), jax.ShapeDtypeStruct((ms, d), BF16)),
        grid=(m // tm,),
        in_specs=[
            pl.BlockSpec((tm, d), lambda i: (i, 0)),
            pl.BlockSpec((ms, d), const),
            pl.BlockSpec((d, d), const),
            pl.BlockSpec((1, d), const),
            pl.BlockSpec((1, d), const),
        ],
        out_specs=(pl.BlockSpec((tm, d), lambda i: (i, 0)), pl.BlockSpec((ms, d), const)),
        compiler_params=_params("arbitrary"),
        name="s5_glu",
    )(y, ys, w, b.reshape(1, d), g.reshape(1, d))


def _out_proj_kernel(x_ref, ma_ref, mb_ref, xs_ref, mas_ref, mbs_ref, w_ref, o_ref, os_ref, wb_ref):
    @pl.when(pl.program_id(0) == 0)
    def _():
        wb_ref[...] = w_ref[...].astype(BF16)

    ka = ma_ref.shape[1]
    proj = lambda x, ma, mb: x + _dot(ma, wb_ref[:ka, :]) + _dot(mb, wb_ref[ka:, :])
    for rows in _row_parts(x_ref.shape[0], 2):
        o_ref[rows, :] = proj(x_ref[rows, :], ma_ref[rows, :], mb_ref[rows, :])

    @_on_last_row_tile
    def _():
        os_ref[...] = proj(xs_ref[...], mas_ref[...], mbs_ref[...])


def _out_proj(x, ma, mb, xs, mas, mbs, w, tm):
    m, n = x.shape
    ms = xs.shape[0]
    ka = ma.shape[1]
    kb = mb.shape[1]
    const = lambda i: (0, 0)
    tile = lambda i: (i, 0)
    return pl.pallas_call(
        _out_proj_kernel,
        out_shape=(jax.ShapeDtypeStruct((m, n), F32), jax.ShapeDtypeStruct((ms, n), F32)),
        grid=(m // tm,),
        in_specs=[
            pl.BlockSpec((tm, n), tile),
            pl.BlockSpec((tm, ka), tile),
            pl.BlockSpec((tm, kb), tile),
            pl.BlockSpec((ms, n), const),
            pl.BlockSpec((ms, ka), const),
            pl.BlockSpec((ms, kb), const),
            pl.BlockSpec((ka + kb, n), const, pipeline_mode=pl.Buffered(1)),
        ],
        out_specs=(pl.BlockSpec((tm, n), tile), pl.BlockSpec((ms, n), const)),
        scratch_shapes=[pltpu.VMEM((ka + kb, n), BF16)],
        compiler_params=_params("arbitrary"),
        name="out_proj",
    )(x, ma, mb, xs, mas, mbs, w)


def _mlp_kernel(x_ref, xs_ref, g_ref, wu_ref, wd_ref, o_ref, os_ref, h_ref, hs_ref):
    def ffn(h):
        act = jnp.square(jnp.maximum(_dot(h, wu_ref[...]), 0.0)).astype(BF16)
        return _dot(act, wd_ref[...])

    def rows(x_ref, o_ref, h_ref):
        first = pl.program_id(1) == 0

        @pl.when(first)
        def _():
            for part in _row_parts(x_ref.shape[0], 4):
                x = x_ref[part, :]
                h = _rms(x, g_ref[...]).astype(BF16)
                h_ref[part, :] = h
                o_ref[part, :] = x + ffn(h)

        @pl.when(jnp.logical_not(first))
        def _():
            o_ref[...] += ffn(h_ref[...])

    rows(x_ref, o_ref, h_ref)
    _on_last_row_tile(lambda: rows(xs_ref, os_ref, hs_ref))


def _mlp(x, xs, g, w_up, w_down, tm, tf):
    m, d = x.shape
    ms = xs.shape[0]
    f = w_up.shape[1]
    const = lambda i, j: (0, 0)
    return pl.pallas_call(
        _mlp_kernel,
        out_shape=(jax.ShapeDtypeStruct((m, d), F32), jax.ShapeDtypeStruct((ms, d), F32)),
        grid=(m // tm, f // tf),
        in_specs=[
            pl.BlockSpec((tm, d), lambda i, j: (i, 0)),
            pl.BlockSpec((ms, d), const),
            pl.BlockSpec((1, d), const),
            pl.BlockSpec((d, tf), lambda i, j: (0, j)),
            pl.BlockSpec((tf, d), lambda i, j: (j, 0)),
        ],
        out_specs=(pl.BlockSpec((tm, d), lambda i, j: (i, 0)), pl.BlockSpec((ms, d), const)),
        scratch_shapes=[pltpu.VMEM((tm, d), BF16), pltpu.VMEM((ms, d), BF16)],
        compiler_params=_params("arbitrary", "arbitrary"),
        name="mlp",
    )(x, xs, g.reshape(1, d), w_up, w_down)


def _ple_kernel(x_ref, p_ref, xs_ref, ps_ref, g_ref, wg_ref, wp_ref, gf_ref, o_ref, os_ref,
                wgb_ref, wpb_ref, *, final):
    @pl.when(pl.program_id(0) == 0)
    def _():
        wgb_ref[...] = wg_ref[...].astype(BF16)
        wpb_ref[...] = wp_ref[...].astype(BF16)

    def ple(x, p):
        gate = jax.nn.sigmoid(_dot(_rms(x, g_ref[...]).astype(BF16), wgb_ref[...]))
        pe = _dot(p.astype(BF16), wpb_ref[...])
        x = x + pe * gate
        return _rms(x, gf_ref[...]) if final else x

    for rows in _row_parts(x_ref.shape[0], 2):
        o_ref[rows, :] = ple(x_ref[rows, :], p_ref[rows, :])

    @_on_last_row_tile
    def _():
        os_ref[...] = ple(xs_ref[...], ps_ref[...])


def _ple(x, p, xs, ps, g, w_gate, w_ple, g_final, tm, final):
    m, d = x.shape
    ms = xs.shape[0]
    dp = p.shape[1]
    once = pl.Buffered(1)
    const = lambda i: (0, 0)
    tile = lambda i: (i, 0)
    return pl.pallas_call(
        functools.partial(_ple_kernel, final=final),
        out_shape=(jax.ShapeDtypeStruct((m, d), F32), jax.ShapeDtypeStruct((ms, d), F32)),
        grid=(m // tm,),
        in_specs=[
            pl.BlockSpec((tm, d), tile),
            pl.BlockSpec((tm, dp), tile),
            pl.BlockSpec((ms, d), const),
            pl.BlockSpec((ms, dp), const),
            pl.BlockSpec((1, d), const),
            pl.BlockSpec((d, d), const, pipeline_mode=once),
            pl.BlockSpec((dp, d), const, pipeline_mode=once),
            pl.BlockSpec((1, d), const),
        ],
        out_specs=(pl.BlockSpec((tm, d), tile), pl.BlockSpec((ms, d), const)),
        scratch_shapes=[pltpu.VMEM((d, d), BF16), pltpu.VMEM((dp, d), BF16)],
        compiler_params=_params("arbitrary"),
        name="ple_final",
    )(x, p, xs, ps, g.reshape(1, d), w_gate, w_ple, g_final.reshape(1, d))


TT_LRU = 64
TM_IN, TN_IN = 1024, 1536
TM_GLU = 1024
TM_OUT = 512
TM_MLP, TF_MLP = 512, 1024
TM_PLE = 512


def kernel(x_prompt, x_sample, state_s5_re, state_s5_im, state_lru, state_conv, p_prompt, p_sample,
           g_mix, w_in, s5_lam_re, s5_lam_im, s5_log_step, s5_b_re, s5_b_im, s5_c_re, s5_c_im, s5_d,
           s5_w_glu, s5_b_glu, conv_w, conv_b, lru_w_a, lru_b_a, lru_w_i, lru_b_i, lru_lam,
           g_merge_a, g_merge_b, w_out, g_mlp, w_up, w_down, g_ple, w_ple_gate, w_ple, g_final):
    depth = g_mix.shape[0]
    nb, t_len, d_model = x_prompt.shape
    ns = x_sample.shape[0]
    d_s5 = s5_d.shape[1]
    d_lru = conv_w.shape[2]
    groups = d_s5 // S5_CH

    xp = x_prompt.reshape(nb * t_len, d_model)
    xs = x_sample.reshape(ns, d_model)
    outs = [[] for _ in range(8)]
    for l in range(depth):
        final = l == depth - 1
        w_glu_b = s5_w_glu[l].astype(BF16)
        lru_args = (conv_w[l], conv_b[l], lru_w_a[l].astype(BF16), lru_b_a[l],
                    lru_w_i[l].astype(BF16), lru_b_i[l], lru_lam[l], g_merge_b[l])
        w_t, w_s, w_c, tab, wb, wct, arow, w_in_b = _s5_params(
            s5_lam_re[l], s5_lam_im[l], s5_log_step[l], s5_b_re[l], s5_b_im[l],
            s5_c_re[l], s5_c_im[l], s5_d[l], w_in[l])

        z, zs = _norm_matmul(xp, xs, g_mix[l], w_in_b, TM_IN, TN_IN)
        z4 = z.reshape(nb, t_len, 3 * d_s5)
        y, hf, w_down_b = _s5_seq(z4, w_t, w_s, w_c, tab, d_s5, w_down[l])
        mb, lru_h, w_up_b = _lru_seq(z4, *lru_args, w_up[l], TT_LRU)
        ys, hsr, hsi = _s5_step(zs, state_s5_re[l].reshape(ns, groups * S5_STATE),
                                state_s5_im[l].reshape(ns, groups * S5_STATE),
                                wb, wct, arow, s5_d[l].reshape(1, d_s5))
        mbs, lru_hs, conv_s = _lru_step(zs, state_conv[l], state_lru[l], *lru_args)
        ma, mas = _glu(y.reshape(nb * t_len, d_s5), ys, w_glu_b, s5_b_glu[l], g_merge_a[l], TM_GLU)
        xp, xs = _out_proj(xp, ma, mb.reshape(nb * t_len, d_lru), xs, mas, mbs, w_out[l], TM_OUT)
        xp, xs = _mlp(xp, xs, g_mlp[l], w_up_b, w_down_b, TM_MLP, TF_MLP)
        xp, xs = _ple(xp, p_prompt[l].reshape(nb * t_len, -1), xs, p_sample[l].reshape(ns, -1),
                      g_ple[l], w_ple_gate[l], w_ple[l], g_final, TM_PLE, final)
        hf = jnp.transpose(hf[:, :, :, :nb], (1, 3, 0, 2))
        outs[0].append(hf[0])
        outs[1].append(hf[1])
        outs[2].append(jnp.transpose(lru_h, (1, 0, 2)).reshape(nb, d_lru))
        outs[3].append(z4[:, t_len - (CONV_W - 1):, d_s5:d_s5 + d_lru])
        outs[4].append(hsr.reshape(ns, groups, S5_STATE))
        outs[5].append(hsi.reshape(ns, groups, S5_STATE))
        outs[6].append(lru_hs)
        outs[7].append(conv_s)
    return (xp.reshape(nb, t_len, d_model), xs.reshape(ns, 1, d_model),
            *(jnp.stack(o) for o in outs))
```

```python
import functools

import jax
import jax.numpy as jnp
from jax import lax
from jax.experimental import pallas as pl
from jax.experimental.pallas import tpu as pltpu

F32 = jnp.float32
BF16 = jnp.bfloat16
HIGHEST = lax.Precision.HIGHEST

EPS = 1e-6
LRU_C = 8.0
S5_CH = 16
S5_STATE = 64
LRU_HEADS = 4
CONV_W = 4
CHUNK = 16
SUBLANES = 8
LANES = 128
LANE_GROUPS = LANES // S5_CH
LANE_PAIRS = LANE_GROUPS // 2
ROWS = CHUNK * S5_CH
POW_ROWS = 32
SCAN_ROW0 = POW_ROWS
VMEM_LIMIT = 56 * 1024 * 1024


def _params(*sem):
    return pltpu.CompilerParams(dimension_semantics=sem, vmem_limit_bytes=VMEM_LIMIT)


def _rms(x, g):
    return x * lax.rsqrt(jnp.mean(x * x, axis=-1, keepdims=True) + EPS) * g


def _dot(a, b):
    return jnp.dot(a, b, preferred_element_type=F32)


def _dot_nt(a, b, precision=None):
    return lax.dot_general(a, b, (((1,), (1,)), ((), ())), precision=precision,
                           preferred_element_type=F32)


def _transpose_tiles(x):
    r, c = x.shape
    return jnp.concatenate(
        [jnp.concatenate([x[i:i + LANES, j:j + LANES].T for i in range(0, r, LANES)], axis=1)
         for j in range(0, c, LANES)], axis=0)


def _row_parts(tm, want):
    parts = max(1, min(want, tm // LANES))
    step = tm // parts
    return [slice(r * step, (r + 1) * step) for r in range(parts)]


def _on_last_row_tile(fn):
    pl.when(pl.program_id(0) == pl.num_programs(0) - 1)(fn)


def _norm_matmul_kernel(x_ref, xs_ref, g_ref, w_ref, o_ref, os_ref):
    tn = o_ref.shape[1]
    w = w_ref[:, pl.ds(pl.multiple_of(pl.program_id(1) * tn, tn), tn)]
    for rows in _row_parts(x_ref.shape[0], 8):
        h = _rms(x_ref[rows, :], g_ref[...]).astype(BF16)
        o_ref[rows, :] = _dot(h, w)

    @_on_last_row_tile
    def _():
        os_ref[...] = _dot(_rms(xs_ref[...], g_ref[...]).astype(BF16), w)


def _norm_matmul(x, xs, g, w, tm, tn):
    m, k = x.shape
    ms = xs.shape[0]
    n = w.shape[1]
    ni = m // tm
    return pl.pallas_call(
        _norm_matmul_kernel,
        out_shape=(jax.ShapeDtypeStruct((m, n), F32), jax.ShapeDtypeStruct((ms, n), F32)),
        grid=(ni, n // tn),
        in_specs=[
            pl.BlockSpec((tm, k), lambda i, j: (i, 0)),
            pl.BlockSpec((ms, k), lambda i, j: (0, 0)),
            pl.BlockSpec((1, k), lambda i, j: (0, 0)),
            pl.BlockSpec((k, n), lambda i, j: (0, 0), pipeline_mode=pl.Buffered(1)),
        ],
        out_specs=(pl.BlockSpec((tm, tn), lambda i, j: (i, j)),
                   pl.BlockSpec((ms, tn), lambda i, j: (0, jnp.where(i == ni - 1, j, 0)))),
        compiler_params=_params("arbitrary", "arbitrary"),
        name="norm_w_in",
    )(x, xs, g.reshape(1, k), w)


def _s5_param_kernel(lam_ref, bc_ref, d_ref, cast_ref,
                     wt_ref, ws_ref, wc_ref, tab_ref, wb_ref, wct_ref, arow_ref, castb_ref):
    castb_ref[...] = cast_ref[...].astype(BF16)
    row_p = lax.broadcasted_iota(jnp.int32, (POW_ROWS, LANES), 0)
    lane = lax.broadcasted_iota(jnp.int32, (ROWS, LANES), 1)
    lane16 = lax.broadcasted_iota(jnp.int32, (S5_CH, LANES), 1)
    row16 = lax.broadcasted_iota(jnp.int32, (S5_CH, LANES), 0)
    lo = lane < S5_STATE

    wb_ref[...] = jnp.zeros_like(wb_ref)
    wct_ref[...] = jnp.zeros_like(wct_ref)

    for q in range(LANE_PAIRS):
        lr, li = lam_ref[0, q], lam_ref[1, q]
        step = jnp.exp(lam_ref[2, q])
        mag = jnp.exp(lr * step)
        ar = mag * jnp.cos(li * step)
        ai = mag * jnp.sin(li * step)
        nr, ni = ar - 1.0, ai
        den = lr * lr + li * li
        cr = (nr * lr + ni * li) / den
        ci = (ni * lr - nr * li) / den
        bt_r, bt_i = bc_ref[0, q], bc_ref[1, q]
        bb_r = cr * bt_r - ci * bt_i
        bb_i = cr * bt_i + ci * bt_r
        c_r, c_i = bc_ref[2, q], bc_ref[3, q]

        pr = jnp.ones((POW_ROWS, LANES), F32)
        pi = jnp.zeros((POW_ROWS, LANES), F32)
        sr, si = ar, ai
        sq = []
        for m in range(POW_ROWS.bit_length() - 1 + SUBLANES):
            sq.append((sr, si))
            if (1 << m) < POW_ROWS:
                bit = ((row_p >> m) & 1) == 1
                pr, pi = (jnp.where(bit, pr * sr - pi * si, pr),
                          jnp.where(bit, pr * si + pi * sr, pi))
            sr, si = sr * sr - si * si, 2.0 * sr * si
        m0 = CHUNK.bit_length() - 1
        scan_r = jnp.concatenate([sq[m0 + j][0] for j in range(SUBLANES)], axis=0)
        scan_i = jnp.concatenate([sq[m0 + j][1] for j in range(SUBLANES)], axis=0)
        pad = jnp.zeros((LANES - POW_ROWS - SUBLANES, LANES), F32)
        tab_ref[q, 0] = jnp.concatenate([pr, scan_r, pad], axis=0).T
        tab_ref[q, 1] = jnp.concatenate([pi, scan_i, pad], axis=0).T
        arow_ref[q, 0] = pr[:SUBLANES]
        arow_ref[q, 1] = pi[:SUBLANES]

        def expand(tab, k0, sign):
            return jnp.concatenate(
                [jnp.broadcast_to(tab[k0 + sign * s:k0 + sign * s + 1, :], (S5_CH, LANES))
                 for s in range(CHUNK)], axis=0)

        tile = lambda v: jnp.concatenate([v] * CHUNK, axis=0)

        e_r, e_i = expand(pr, CHUNK - 1, -1), expand(pi, CHUNK - 1, -1)
        t_r, t_i = tile(bb_r), tile(bb_i)
        ws_r = _transpose_tiles(t_r * e_r - t_i * e_i)
        ws_i = _transpose_tiles(t_r * e_i + t_i * e_r)
        for h in range(2):
            rows = slice(h * S5_STATE, (h + 1) * S5_STATE)
            ws_ref[2 * q + h] = jnp.concatenate([ws_r[rows], ws_i[rows]], axis=0).astype(BF16)

        e_r, e_i = expand(pr, 1, 1), expand(pi, 1, 1)
        t_r, t_i = tile(c_r), tile(c_i)
        ca_r = t_r * e_r - t_i * e_i
        ca_n = -(t_r * e_i + t_i * e_r)
        wc_ref[2 * q] = jnp.where(lo, ca_r, pltpu.roll(ca_n, S5_STATE, 1)).astype(BF16)
        wc_ref[2 * q + 1] = jnp.where(lo, pltpu.roll(ca_r, S5_STATE, 1), ca_n).astype(BF16)

        e_r, e_i = expand(pr, 0, 1), expand(pi, 0, 1)
        c0_r = t_r * e_r - t_i * e_i
        c0_i = t_r * e_i + t_i * e_r

        for h in range(2):
            g = 2 * q + h
            mine = (lane16 < S5_STATE) == (h == 0)
            bm_r = jnp.where(mine, bb_r, 0.0)
            bm_i = jnp.where(mine, bb_i, 0.0)
            kt = _dot_nt(bm_r, c0_r, HIGHEST) - _dot_nt(bm_i, c0_i, HIGHEST)
            k_lo = kt[:, :LANES] + jnp.where(row16 == lane16, d_ref[g], 0.0)
            k_hi = kt[:, LANES:]
            blocks = []
            half = LANES // S5_CH
            for s in range(CHUNK):
                sh = (s % half) * S5_CH
                keep = lane16 >= sh
                r_lo = pltpu.roll(k_lo, sh, 1) if sh else k_lo
                r_hi = pltpu.roll(k_hi, sh, 1) if sh else k_hi
                if s < half:
                    blk = jnp.concatenate([jnp.where(keep, r_lo, 0.0),
                                           jnp.where(keep, r_hi, r_lo)], axis=1)
                else:
                    blk = jnp.concatenate([jnp.zeros_like(r_lo), jnp.where(keep, r_lo, 0.0)], axis=1)
                blocks.append(blk)
            wt_ref[g] = _transpose_tiles(jnp.concatenate(blocks, axis=0)).astype(BF16)

            r0 = (2 * q + h) * S5_CH
            c0 = q * 2 * LANES
            wb_ref[r0:r0 + S5_CH, c0:c0 + LANES] = bm_r.astype(BF16)
            wb_ref[r0:r0 + S5_CH, c0 + LANES:c0 + 2 * LANES] = bm_i.astype(BF16)
            wct_ref[q, r0:r0 + S5_CH, :LANES] = jnp.where(mine, c_r, 0.0).astype(BF16)
            wct_ref[q, r0:r0 + S5_CH, LANES:] = jnp.where(mine, -c_i, 0.0).astype(BF16)


def _s5_params(lam_re, lam_im, log_step, b_re, b_im, c_re, c_im, d, cast_w):
    groups = lam_re.shape[0]
    pairs = groups // 2
    slabs = groups // LANE_GROUPS
    cast_blk = pl.BlockSpec((cast_w.shape[0] // slabs, cast_w.shape[1]), lambda i: (i, 0))
    ls = jnp.broadcast_to(log_step[:, None], (groups, S5_STATE))
    lam = jnp.stack([lam_re, lam_im, ls]).reshape(3, pairs, 1, LANES)
    bc = jnp.stack([jnp.transpose(b_re, (0, 2, 1)), jnp.transpose(b_im, (0, 2, 1)), c_re, c_im])
    bc = jnp.transpose(bc.reshape(4, pairs, 2, S5_CH, S5_STATE), (0, 1, 3, 2, 4)).reshape(
        4, pairs, S5_CH, LANES)
    d_pad = jnp.pad(d.reshape(groups, 1, S5_CH), ((0, 0), (0, 0), (0, LANES - S5_CH)))
    blk3 = lambda i: (i, 0, 0)
    blk4 = lambda i: (i, 0, 0, 0)
    pblk = lambda i: (0, i, 0, 0)
    return pl.pallas_call(
        _s5_param_kernel,
        out_shape=(jax.ShapeDtypeStruct((groups, ROWS, ROWS), BF16),
                   jax.ShapeDtypeStruct((groups, 2 * S5_STATE, ROWS), BF16),
                   jax.ShapeDtypeStruct((groups, ROWS, 2 * S5_STATE), BF16),
                   jax.ShapeDtypeStruct((pairs, 2, LANES, LANES), F32),
                   jax.ShapeDtypeStruct((slabs, LANES, LANE_PAIRS * 2 * LANES), BF16),
                   jax.ShapeDtypeStruct((slabs, LANE_PAIRS, LANES, 2 * LANES), BF16),
                   jax.ShapeDtypeStruct((pairs, 2, SUBLANES, LANES), F32),
                   jax.ShapeDtypeStruct(cast_w.shape, BF16)),
        grid=(slabs,),
        in_specs=[pl.BlockSpec((3, LANE_PAIRS, 1, LANES), pblk),
                  pl.BlockSpec((4, LANE_PAIRS, S5_CH, LANES), pblk),
                  pl.BlockSpec((LANE_GROUPS, 1, LANES), blk3), cast_blk],
        out_specs=(pl.BlockSpec((LANE_GROUPS, ROWS, ROWS), blk3),
                   pl.BlockSpec((LANE_GROUPS, 2 * S5_STATE, ROWS), blk3),
                   pl.BlockSpec((LANE_GROUPS, ROWS, 2 * S5_STATE), blk3),
                   pl.BlockSpec((LANE_PAIRS, 2, LANES, LANES), blk4),
                   pl.BlockSpec((None, LANES, LANE_PAIRS * 2 * LANES), blk3),
                   pl.BlockSpec((None, LANE_PAIRS, LANES, 2 * LANES), blk4),
                   pl.BlockSpec((LANE_PAIRS, 2, SUBLANES, LANES), blk4),
                   cast_blk),
        compiler_params=_params("parallel"),
        name="s5_params",
    )(lam, bc, d_pad, cast_w)


def _s5_seq_kernel(z_ref, wt_ref, ws_ref, wc_ref, tab_ref, cast_ref, y_ref, hf_ref, castb_ref,
                   ut_ref, yt_ref, sr_ref, si_ref, *, nb, n_chunks):
    castb_ref[...] = cast_ref[...].astype(BF16)
    scan_steps = n_chunks.bit_length() - 1
    for n in range(nb):
        for s in range(CHUNK):
            xs = z_ref[n, pl.ds(s, n_chunks, stride=CHUNK), :]
            ut_ref[:, s * S5_CH:(s + 1) * S5_CH, n * n_chunks:(n + 1) * n_chunks] = (
                xs.T.astype(BF16).reshape(LANE_GROUPS, S5_CH, n_chunks))

    for g in range(LANE_GROUPS):
        st = _dot(ws_ref[g], ut_ref[g])
        sr_ref[g * S5_STATE:(g + 1) * S5_STATE, :] = st[:S5_STATE]
        si_ref[g * S5_STATE:(g + 1) * S5_STATE, :] = st[S5_STATE:]

    srows = LANE_GROUPS * S5_STATE
    lane = lax.broadcasted_iota(jnp.int32, (srows, n_chunks), 1)
    tab_r = jnp.concatenate([tab_ref[q, 0] for q in range(LANE_PAIRS)], axis=0)
    tab_i = jnp.concatenate([tab_ref[q, 1] for q in range(LANE_PAIRS)], axis=0)
    fin_r = jnp.zeros((srows, n_chunks), F32)
    fin_i = jnp.zeros((srows, n_chunks), F32)
    for n in range(nb):
        cols = slice(n * n_chunks, (n + 1) * n_chunks)
        xr = sr_ref[:, cols]
        xi = si_ref[:, cols]
        for k in range(scan_steps):
            d = 1 << k
            keep = lane >= d
            sr = jnp.where(keep, pltpu.roll(xr, d, 1), 0.0)
            si = jnp.where(keep, pltpu.roll(xi, d, 1), 0.0)
            pr = tab_r[:, SCAN_ROW0 + k:SCAN_ROW0 + k + 1]
            pi = tab_i[:, SCAN_ROW0 + k:SCAN_ROW0 + k + 1]
            xr, xi = xr + (pr * sr - pi * si), xi + (pr * si + pi * sr)
        last = lane == n_chunks - 1
        fin_r = fin_r + jnp.where(
            lane == n, jnp.sum(jnp.where(last, xr, 0.0), axis=1, keepdims=True), 0.0)
        fin_i = fin_i + jnp.where(
            lane == n, jnp.sum(jnp.where(last, xi, 0.0), axis=1, keepdims=True), 0.0)
        keep = lane >= 1
        sr_ref[:, cols] = jnp.where(keep, pltpu.roll(xr, 1, 1), 0.0)
        si_ref[:, cols] = jnp.where(keep, pltpu.roll(xi, 1, 1), 0.0)

    for g in range(LANE_GROUPS):
        rows = slice(g * S5_STATE, (g + 1) * S5_STATE)
        hf_ref[g, 0] = fin_r[rows]
        hf_ref[g, 1] = fin_i[rows]
        hs = jnp.concatenate([sr_ref[rows, :], si_ref[rows, :]], axis=0).astype(BF16)
        yt_ref[g] = jax.nn.gelu(_dot(wc_ref[g], hs) + _dot(wt_ref[g], ut_ref[g]))

    for n in range(nb):
        for s in range(CHUNK):
            blk = yt_ref[:, s * S5_CH:(s + 1) * S5_CH, n * n_chunks:(n + 1) * n_chunks]
            y_ref[n, pl.ds(s, n_chunks, stride=CHUNK), :] = blk.reshape(LANES, n_chunks).T


def _s5_seq(z, w_t, w_s, w_c, tab, d_s5, cast_w):
    nb, t_len, _ = z.shape
    n_chunks = t_len // CHUNK
    assert n_chunks == LANES, "the chunk axis must fill one 128-lane tile"
    groups = d_s5 // S5_CH
    steps = groups // LANE_GROUPS
    cast_blk = pl.BlockSpec((cast_w.shape[0] // steps, cast_w.shape[1]), lambda i: (i, 0))
    kern = functools.partial(_s5_seq_kernel, nb=nb, n_chunks=n_chunks)
    blk3 = lambda i: (i, 0, 0)
    blk4 = lambda i: (i, 0, 0, 0)
    return pl.pallas_call(
        kern,
        out_shape=(jax.ShapeDtypeStruct((nb, t_len, d_s5), F32),
                   jax.ShapeDtypeStruct((groups, 2, S5_STATE, LANES), F32),
                   jax.ShapeDtypeStruct(cast_w.shape, BF16)),
        grid=(steps,),
        in_specs=[
            pl.BlockSpec((nb, t_len, LANES), lambda i: (0, 0, i)),
            pl.BlockSpec((LANE_GROUPS, ROWS, ROWS), blk3),
            pl.BlockSpec((LANE_GROUPS, 2 * S5_STATE, ROWS), blk3),
            pl.BlockSpec((LANE_GROUPS, ROWS, 2 * S5_STATE), blk3),
            pl.BlockSpec((LANE_PAIRS, 2, LANES, LANES), blk4),
            cast_blk,
        ],
        out_specs=(pl.BlockSpec((nb, t_len, LANES), lambda i: (0, 0, i)),
                   pl.BlockSpec((LANE_GROUPS, 2, S5_STATE, LANES), blk4),
                   cast_blk),
        scratch_shapes=[pltpu.VMEM((LANE_GROUPS, ROWS, nb * n_chunks), BF16),
                        pltpu.VMEM((LANE_GROUPS, ROWS, nb * n_chunks), F32),
                        pltpu.VMEM((LANE_GROUPS * S5_STATE, nb * n_chunks), F32),
                        pltpu.VMEM((LANE_GROUPS * S5_STATE, nb * n_chunks), F32)],
        compiler_params=_params("parallel"),
        name="s5_seq",
    )(z, w_t, w_s, w_c, tab, cast_w)


def _s5_step_kernel(u_ref, h0r_ref, h0i_ref, wb_ref, wct_ref, arow_ref, d_ref,
                    y_ref, hr_ref, hi_ref):
    u = u_ref[...]
    bu = _dot(u.astype(BF16), wb_ref[...])
    y = d_ref[...] * u
    for q in range(LANE_PAIRS):
        cols = slice(q * LANES, (q + 1) * LANES)
        ar = arow_ref[q, 0, 1:2, :]
        ai = arow_ref[q, 1, 1:2, :]
        h0r, h0i = h0r_ref[:, cols], h0i_ref[:, cols]
        hr = ar * h0r - ai * h0i + bu[:, 2 * q * LANES:(2 * q + 1) * LANES]
        hi = ar * h0i + ai * h0r + bu[:, (2 * q + 1) * LANES:(2 * q + 2) * LANES]
        hr_ref[:, cols] = hr
        hi_ref[:, cols] = hi
        y = y + _dot_nt(jnp.concatenate([hr, hi], axis=1).astype(BF16), wct_ref[q])
    y_ref[...] = jax.nn.gelu(y).astype(BF16)


def _s5_step(z, h0r, h0i, wb, wct, arow, d):
    n = z.shape[0]
    slabs = wb.shape[0]
    sw = LANE_PAIRS * LANES
    col = lambda i: (0, i)
    blk3 = lambda i: (i, 0, 0)
    blk4 = lambda i: (i, 0, 0, 0)
    return pl.pallas_call(
        _s5_step_kernel,
        out_shape=(jax.ShapeDtypeStruct((n, slabs * LANES), BF16),
                   jax.ShapeDtypeStruct((n, slabs * sw), F32),
                   jax.ShapeDtypeStruct((n, slabs * sw), F32)),
        grid=(slabs,),
        in_specs=[
            pl.BlockSpec((n, LANES), col),
            pl.BlockSpec((n, sw), col),
            pl.BlockSpec((n, sw), col),
            pl.BlockSpec((None, LANES, 2 * sw), blk3),
            pl.BlockSpec((None, LANE_PAIRS, LANES, 2 * LANES), blk4),
            pl.BlockSpec((LANE_PAIRS, 2, SUBLANES, LANES), blk4),
            pl.BlockSpec((1, LANES), col),
        ],
        out_specs=(pl.BlockSpec((n, LANES), col),
                   pl.BlockSpec((n, sw), col),
                   pl.BlockSpec((n, sw), col)),
        compiler_params=_params("parallel"),
        name="s5_step",
    )(z, h0r, h0i, wb, wct, arow, d)


def _lru_gate_block(xc, wa, ba, wi, bi, lam):
    xb16 = xc.astype(BF16)
    r = jax.nn.sigmoid(_dot(xb16, wa) + ba)
    ig = jax.nn.sigmoid(_dot(xb16, wi) + bi)
    log_a = -LRU_C * r * jax.nn.softplus(-lam)
    a = jnp.exp(log_a)
    mult = jnp.sqrt(1.0 - a * a)
    return a, mult * (ig * xc)


def _lru_gates(xc, wa_ref, ba, wi_ref, bi, lam):
    blk = xc.shape[1] // LRU_HEADS
    parts = [_lru_gate_block(xc[:, h * blk:(h + 1) * blk], wa_ref[h], ba[:, h * blk:(h + 1) * blk],
                             wi_ref[h], bi[:, h * blk:(h + 1) * blk],
                             lam[:, h * blk:(h + 1) * blk]) for h in range(LRU_HEADS)]
    return (jnp.concatenate([p[0] for p in parts], axis=-1),
            jnp.concatenate([p[1] for p in parts], axis=-1))


def _lru_seq_kernel(xb_ref, gb_ref, cw_ref, cb_ref, wa_ref, ba_ref, wi_ref, bi_ref, lam_ref,
                    gm_ref, cast_ref, xbs_ref, gbs_ref, c0_ref, c1_ref, c2_ref, h0_ref,
                    o_ref, hl_ref, castb_ref, os_ref, hs_ref, cv_ref,
                    xe_ref, a_ref, b_ref, hc_ref, *, tt):
    castb_ref[...] = cast_ref[...].astype(BF16)

    @pl.when(pl.program_id(0) == 0)
    def _():
        _lru_step_kernel(xbs_ref, gbs_ref, c0_ref, c1_ref, c2_ref, h0_ref, cw_ref, cb_ref, wa_ref,
                         ba_ref, wi_ref, bi_ref, lam_ref, gm_ref, os_ref, hs_ref, cv_ref)

    halo = SUBLANES
    nseq, _, d = xb_ref.shape
    blk = d // LRU_HEADS
    pitch = a_ref.shape[1] // nseq

    @pl.when(pl.program_id(0) == 0)
    def _():
        xe_ref[:, 0:halo, :] = jnp.zeros((nseq, halo, d), F32)
        hc_ref[...] = jnp.zeros_like(hc_ref)

    xe_ref[:, halo:halo + tt, :] = xb_ref[...]
    for h in range(LRU_HEADS):
        cols = slice(h * blk, (h + 1) * blk)
        cw = cw_ref[:, cols]
        xc = cb_ref[:, cols] + xe_ref[:, halo:halo + tt, cols] * cw[CONV_W - 1:CONV_W, :]
        for k in range(1, CONV_W):
            xc = xc + xe_ref[:, halo - k:halo - k + tt, cols] * cw[CONV_W - 1 - k:CONV_W - k, :]
        a, b = _lru_gate_block(xc.reshape(nseq * tt, blk), wa_ref[h], ba_ref[:, cols],
                               wi_ref[h], bi_ref[:, cols], lam_ref[:, cols])
        for n in range(nseq):
            for s in range(blk // LANES):
                slab = h * (blk // LANES) + s
                rows = slice(n * pitch, n * pitch + tt)
                a_ref[slab, rows, :] = a[n * tt:(n + 1) * tt, s * LANES:(s + 1) * LANES]
                b_ref[slab, rows, :] = b[n * tt:(n + 1) * tt, s * LANES:(s + 1) * LANES]
    xe_ref[:, 0:halo, :] = xb_ref[:, tt - halo:tt, :]

    slabs = d // LANES

    def block(i, hs):
        base = pl.multiple_of(i * SUBLANES, SUBLANES)
        hs = list(hs)
        for j in range(SUBLANES):
            rows = pl.ds(base + j, nseq, stride=pitch)
            for s in range(slabs):
                hs[s] = a_ref[s, rows, :] * hs[s] + b_ref[s, rows, :]
                b_ref[s, rows, :] = hs[s]
        return tuple(hs)

    hs = lax.fori_loop(0, tt // SUBLANES, block, tuple(hc_ref[s] for s in range(slabs)))
    for s in range(slabs):
        hc_ref[s] = hs[s]
        hl_ref[s] = hs[s]

    h_all = jnp.stack([jnp.concatenate([b_ref[s, n * pitch:n * pitch + tt, :] for s in range(slabs)],
                                       axis=-1) for n in range(nseq)])
    out = h_all * jax.nn.gelu(gb_ref[...])
    o_ref[...] = _rms(out, gm_ref[...]).astype(BF16)


def _lru_seq(z, zs, conv0, h0, conv_w, conv_b, w_a, b_a, w_i, b_i, lam, g_merge, cast_w, tt):
    nseq, t_len, _ = z.shape
    ns = zs.shape[0]
    d = conv_w.shape[1]
    taps = CONV_W - 1
    conv2d = conv0.reshape(ns, taps * d)
    steps = t_len // tt
    cast_blk = pl.BlockSpec((cast_w.shape[0] // steps, cast_w.shape[1]), lambda t: (t, 0))
    row = lambda v: v.reshape(1, d)
    const2 = lambda t: (0, 0)
    const3 = lambda t: (0, 0, 0)
    scol = lambda c: pl.BlockSpec((ns, d), lambda t: (0, c))
    kern = functools.partial(_lru_seq_kernel, tt=tt)
    o, hl, cast_b, os_, hs, cv = pl.pallas_call(
        kern,
        out_shape=(jax.ShapeDtypeStruct((nseq, t_len, d), BF16),
                   jax.ShapeDtypeStruct((d // LANES, nseq, LANES), F32),
                   jax.ShapeDtypeStruct(cast_w.shape, BF16),
                   jax.ShapeDtypeStruct((ns, d), BF16),
                   jax.ShapeDtypeStruct((ns, d), F32),
                   jax.ShapeDtypeStruct((ns, taps * d), F32)),
        grid=(steps,),
        in_specs=[
            pl.BlockSpec((nseq, tt, d), lambda t: (0, t, 1)),
            pl.BlockSpec((nseq, tt, d), lambda t: (0, t, 2)),
            pl.BlockSpec((CONV_W, d), const2),
            pl.BlockSpec((1, d), const2),
            pl.BlockSpec(w_a.shape, const3),
            pl.BlockSpec((1, d), const2),
            pl.BlockSpec(w_i.shape, const3),
            pl.BlockSpec((1, d), const2),
            pl.BlockSpec((1, d), const2),
            pl.BlockSpec((1, d), const2),
            cast_blk,
            scol(1), scol(2), scol(0), scol(1), scol(2), scol(0),
        ],
        out_specs=(pl.BlockSpec((nseq, tt, d), lambda t: (0, t, 0)),
                   pl.BlockSpec((d // LANES, nseq, LANES), const3),
                   cast_blk,
                   scol(0), scol(0), pl.BlockSpec((ns, taps * d), const2)),
        scratch_shapes=[pltpu.VMEM((nseq, tt + SUBLANES, d), F32),
                        pltpu.VMEM((d // LANES, nseq * (tt + SUBLANES), LANES), F32),
                        pltpu.VMEM((d // LANES, nseq * (tt + SUBLANES), LANES), F32),
                        pltpu.VMEM((d // LANES, nseq, LANES), F32)],
        compiler_params=_params("arbitrary"),
        name="rglru_seq",
    )(z, z, conv_w, row(conv_b), w_a, row(b_a), w_i, row(b_i), row(lam), row(g_merge), cast_w,
      zs, zs, conv2d, conv2d, conv2d, h0)
    return o, hl, cast_b, os_, hs, cv.reshape(ns, taps, d)


def _lru_step_kernel(xb_ref, gb_ref, c0_ref, c1_ref, c2_ref, h0_ref, cw_ref, cb_ref,
                     wa_ref, ba_ref, wi_ref, bi_ref, lam_ref, gm_ref, o_ref, h_ref, cv_ref):
    d = xb_ref.shape[1]
    xb = xb_ref[...]
    cw = cw_ref[...]
    xc = (cb_ref[...] + c0_ref[...] * cw[0:1, :] + c1_ref[...] * cw[1:2, :]
          + c2_ref[...] * cw[2:3, :] + xb * cw[3:4, :])
    a, b = _lru_gates(xc, wa_ref, ba_ref[...], wi_ref, bi_ref[...], lam_ref[...])
    h = a * h0_ref[...] + b
    h_ref[...] = h
    out = h * jax.nn.gelu(gb_ref[...])
    o_ref[...] = _rms(out, gm_ref[...]).astype(BF16)
    cv_ref[:, 0:d] = c1_ref[...]
    cv_ref[:, d:2 * d] = c2_ref[...]
    cv_ref[:, 2 * d:3 * d] = xb


def _glu_rows(y, w_ref, b_ref, g_ref):
    gate = jax.nn.sigmoid(_dot(y.astype(BF16), w_ref[...]) + b_ref[...])
    return _rms(y.astype(F32) * gate, g_ref[...]).astype(BF16)


def _glu_kernel(y_ref, ys_ref, w_ref, b_ref, g_ref, o_ref, os_ref):
    for rows in _row_parts(y_ref.shape[0], 4):
        o_ref[rows, :] = _glu_rows(y_ref[rows, :], w_ref, b_ref, g_ref)

    @_on_last_row_tile
    def _():
        os_ref[...] = _glu_rows(ys_ref[...], w_ref, b_ref, g_ref)


def _glu(y, ys, w, b, g, tm):
    m, d = y.shape
    ms = ys.shape[0]
    const = lambda i: (0, 0)
    return pl.pallas_call(
        _glu_kernel,
        out_shape=(jax.ShapeDtypeStruct((m, d), BF16), jax.ShapeDtypeStruct((ms, d), BF16)),
        grid=(m // tm,),
        in_specs=[
            pl.BlockSpec((tm, d), lambda i: (i, 0)),
            pl.BlockSpec((ms, d), const),
            pl.BlockSpec((d, d), const),
            pl.BlockSpec((1, d), const),
            pl.BlockSpec((1, d), const),
        ],
        out_specs=(pl.BlockSpec((tm, d), lambda i: (i, 0)), pl.BlockSpec((ms, d), const)),
        compiler_params=_params("arbitrary"),
        name="s5_glu",
    )(y, ys, w, b.reshape(1, d), g.reshape(1, d))


def _out_proj_kernel(x_ref, ma_ref, mb_ref, xs_ref, mas_ref, mbs_ref, w_ref, o_ref, os_ref, wb_ref):
    @pl.when(pl.program_id(0) == 0)
    def _():
        wb_ref[...] = w_ref[...].astype(BF16)

    ka = ma_ref.shape[1]
    proj = lambda x, ma, mb: x + _dot(ma, wb_ref[:ka, :]) + _dot(mb, wb_ref[ka:, :])
    for rows in _row_parts(x_ref.shape[0], 2):
        o_ref[rows, :] = proj(x_ref[rows, :], ma_ref[rows, :], mb_ref[rows, :])

    @_on_last_row_tile
    def _():
        os_ref[...] = proj(xs_ref[...], mas_ref[...], mbs_ref[...])


def _out_proj(x, ma, mb, xs, mas, mbs, w, tm):
    m, n = x.shape
    ms = xs.shape[0]
    ka = ma.shape[1]
    kb = mb.shape[1]
    const = lambda i: (0, 0)
    tile = lambda i: (i, 0)
    return pl.pallas_call(
        _out_proj_kernel,
        out_shape=(jax.ShapeDtypeStruct((m, n), F32), jax.ShapeDtypeStruct((ms, n), F32)),
        grid=(m // tm,),
        in_specs=[
            pl.BlockSpec((tm, n), tile),
            pl.BlockSpec((tm, ka), tile),
            pl.BlockSpec((tm, kb), tile),
            pl.BlockSpec((ms, n), const),
            pl.BlockSpec((ms, ka), const),
            pl.BlockSpec((ms, kb), const),
            pl.BlockSpec((ka + kb, n), const, pipeline_mode=pl.Buffered(1)),
        ],
        out_specs=(pl.BlockSpec((tm, n), tile), pl.BlockSpec((ms, n), const)),
        scratch_shapes=[pltpu.VMEM((ka + kb, n), BF16)],
        compiler_params=_params("arbitrary"),
        name="out_proj",
    )(x, ma, mb, xs, mas, mbs, w)


def _mlp_kernel(x_ref, xs_ref, g_ref, wu_ref, wd_ref, o_ref, os_ref, h_ref, hs_ref):
    def ffn(h):
        act = jnp.square(jnp.maximum(_dot(h, wu_ref[...]), 0.0)).astype(BF16)
        return _dot(act, wd_ref[...])

    def rows(x_ref, o_ref, h_ref):
        first = pl.program_id(1) == 0

        @pl.when(first)
        def _():
            for part in _row_parts(x_ref.shape[0], 4):
                x = x_ref[part, :]
                h = _rms(x, g_ref[...]).astype(BF16)
                h_ref[part, :] = h
                o_ref[part, :] = x + ffn(h)

        @pl.when(jnp.logical_not(first))
        def _():
            o_ref[...] += ffn(h_ref[...])

    rows(x_ref, o_ref, h_ref)
    _on_last_row_tile(lambda: rows(xs_ref, os_ref, hs_ref))


def _mlp(x, xs, g, w_up, w_down, tm, tf):
    m, d = x.shape
    ms = xs.shape[0]
    f = w_up.shape[1]
    const = lambda i, j: (0, 0)
    return pl.pallas_call(
        _mlp_kernel,
        out_shape=(jax.ShapeDtypeStruct((m, d), F32), jax.ShapeDtypeStruct((ms, d), F32)),
        grid=(m // tm, f // tf),
        in_specs=[
            pl.BlockSpec((tm, d), lambda i, j: (i, 0)),
            pl.BlockSpec((ms, d), const),
            pl.BlockSpec((1, d), const),
            pl.BlockSpec((d, tf), lambda i, j: (0, j)),
            pl.BlockSpec((tf, d), lambda i, j: (j, 0)),
        ],
        out_specs=(pl.BlockSpec((tm, d), lambda i, j: (i, 0)), pl.BlockSpec((ms, d), const)),
        scratch_shapes=[pltpu.VMEM((tm, d), BF16), pltpu.VMEM((ms, d), BF16)],
        compiler_params=_params("arbitrary", "arbitrary"),
        name="mlp",
    )(x, xs, g.reshape(1, d), w_up, w_down)


def _ple_kernel(x_ref, p_ref, xs_ref, ps_ref, g_ref, wg_ref, wp_ref, gf_ref, o_ref, os_ref,
                wgb_ref, wpb_ref, *, final):
    @pl.when(pl.program_id(0) == 0)
    def _():
        wgb_ref[...] = wg_ref[...].astype(BF16)
        wpb_ref[...] = wp_ref[...].astype(BF16)

    def ple(x, p):
        gate = jax.nn.sigmoid(_dot(_rms(x, g_ref[...]).astype(BF16), wgb_ref[...]))
        pe = _dot(p.astype(BF16), wpb_ref[...])
        x = x + pe * gate
        return _rms(x, gf_ref[...]) if final else x

    for rows in _row_parts(x_ref.shape[0], 2):
        o_ref[rows, :] = ple(x_ref[rows, :], p_ref[rows, :])

    @_on_last_row_tile
    def _():
        os_ref[...] = ple(xs_ref[...], ps_ref[...])


def _ple(x, p, xs, ps, g, w_gate, w_ple, g_final, tm, final):
    m, d = x.shape
    ms = xs.shape[0]
    dp = p.shape[1]
    once = pl.Buffered(1)
    const = lambda i: (0, 0)
    tile = lambda i: (i, 0)
    return pl.pallas_call(
        functools.partial(_ple_kernel, final=final),
        out_shape=(jax.ShapeDtypeStruct((m, d), F32), jax.ShapeDtypeStruct((ms, d), F32)),
        grid=(m // tm,),
        in_specs=[
            pl.BlockSpec((tm, d), tile),
            pl.BlockSpec((tm, dp), tile),
            pl.BlockSpec((ms, d), const),
            pl.BlockSpec((ms, dp), const),
            pl.BlockSpec((1, d), const),
            pl.BlockSpec((d, d), const, pipeline_mode=once),
            pl.BlockSpec((dp, d), const, pipeline_mode=once),
            pl.BlockSpec((1, d), const),
        ],
        out_specs=(pl.BlockSpec((tm, d), tile), pl.BlockSpec((ms, d), const)),
        scratch_shapes=[pltpu.VMEM((d, d), BF16), pltpu.VMEM((dp, d), BF16)],
        compiler_params=_params("arbitrary"),
        name="ple_final",
    )(x, p, xs, ps, g.reshape(1, d), w_gate, w_ple, g_final.reshape(1, d))


TT_LRU = 64
TM_IN, TN_IN = 1024, 1536
TM_GLU = 1024
TM_OUT = 512
TM_MLP, TF_MLP = 512, 1024
TM_PLE = 512


def kernel(x_prompt, x_sample, state_s5_re, state_s5_im, state_lru, state_conv, p_prompt, p_sample,
           g_mix, w_in, s5_lam_re, s5_lam_im, s5_log_step, s5_b_re, s5_b_im, s5_c_re, s5_c_im, s5_d,
           s5_w_glu, s5_b_glu, conv_w, conv_b, lru_w_a, lru_b_a, lru_w_i, lru_b_i, lru_lam,
           g_merge_a, g_merge_b, w_out, g_mlp, w_up, w_down, g_ple, w_ple_gate, w_ple, g_final):
    depth = g_mix.shape[0]
    nb, t_len, d_model = x_prompt.shape
    ns = x_sample.shape[0]
    d_s5 = s5_d.shape[1]
    d_lru = conv_w.shape[2]
    groups = d_s5 // S5_CH

    xp = x_prompt.reshape(nb * t_len, d_model)
    xs = x_sample.reshape(ns, d_model)
    outs = [[] for _ in range(8)]
    for l in range(depth):
        final = l == depth - 1
        w_glu_b = s5_w_glu[l].astype(BF16)
        lru_args = (conv_w[l], conv_b[l], lru_w_a[l].astype(BF16), lru_b_a[l],
                    lru_w_i[l].astype(BF16), lru_b_i[l], lru_lam[l], g_merge_b[l])
        w_t, w_s, w_c, tab, wb, wct, arow, w_in_b = _s5_params(
            s5_lam_re[l], s5_lam_im[l], s5_log_step[l], s5_b_re[l], s5_b_im[l],
            s5_c_re[l], s5_c_im[l], s5_d[l], w_in[l])

        z, zs = _norm_matmul(xp, xs, g_mix[l], w_in_b, TM_IN, TN_IN)
        z4 = z.reshape(nb, t_len, 3 * d_s5)
        y, hf, w_down_b = _s5_seq(z4, w_t, w_s, w_c, tab, d_s5, w_down[l])
        mb, lru_h, w_up_b, mbs, lru_hs, conv_s = _lru_seq(
            z4, zs, state_conv[l], state_lru[l], *lru_args, w_up[l], TT_LRU)
        ys, hsr, hsi = _s5_step(zs, state_s5_re[l].reshape(ns, groups * S5_STATE),
                                state_s5_im[l].reshape(ns, groups * S5_STATE),
                                wb, wct, arow, s5_d[l].reshape(1, d_s5))
        ma, mas = _glu(y.reshape(nb * t_len, d_s5), ys, w_glu_b, s5_b_glu[l], g_merge_a[l], TM_GLU)
        xp, xs = _out_proj(xp, ma, mb.reshape(nb * t_len, d_lru), xs, mas, mbs, w_out[l], TM_OUT)
        xp, xs = _mlp(xp, xs, g_mlp[l], w_up_b, w_down_b, TM_MLP, TF_MLP)
        xp, xs = _ple(xp, p_prompt[l].reshape(nb * t_len, -1), xs, p_sample[l].reshape(ns, -1),
                      g_ple[l], w_ple_gate[l], w_ple[l], g_final, TM_PLE, final)
        hf = jnp.transpose(hf[:, :, :, :nb], (1, 3, 0, 2))
        outs[0].append(hf[0])
        outs[1].append(hf[1])
        outs[2].append(jnp.transpose(lru_h, (1, 0, 2)).reshape(nb, d_lru))
        outs[3].append(z4[:, t_len - (CONV_W - 1):, d_s5:d_s5 + d_lru])
        outs[4].append(hsr.reshape(ns, groups, S5_STATE))
        outs[5].append(hsi.reshape(ns, groups, S5_STATE))
        outs[6].append(lru_hs)
        outs[7].append(conv_s)
    return (xp.reshape(nb, t_len, d_model), xs.reshape(ns, 1, d_model),
            *(jnp.stack(o) for o in outs))
```

```python
import functools

import jax
import jax.numpy as jnp
from jax import lax
from jax.experimental import pallas as pl
from jax.experimental.pallas import tpu as pltpu

F32 = jnp.float32
BF16 = jnp.bfloat16
HIGHEST = lax.Precision.HIGHEST

EPS = 1e-6
LRU_C = 8.0
S5_CH = 16
S5_STATE = 64
LRU_HEADS = 4
CONV_W = 4
CHUNK = 16
SUBLANES = 8
LANES = 128
LANE_GROUPS = LANES // S5_CH
LANE_PAIRS = LANE_GROUPS // 2
ROWS = CHUNK * S5_CH
POW_ROWS = 32
SCAN_ROW0 = POW_ROWS
VMEM_LIMIT = 56 * 1024 * 1024


def _params(*sem):
    return pltpu.CompilerParams(dimension_semantics=sem, vmem_limit_bytes=VMEM_LIMIT)


def _rms(x, g):
    return x * lax.rsqrt(jnp.mean(x * x, axis=-1, keepdims=True) + EPS) * g


def _dot(a, b):
    return jnp.dot(a, b, preferred_element_type=F32)


def _dot_nt(a, b, precision=None):
    return lax.dot_general(a, b, (((1,), (1,)), ((), ())), precision=precision,
                           preferred_element_type=F32)


def _transpose_tiles(x):
    r, c = x.shape
    return jnp.concatenate(
        [jnp.concatenate([x[i:i + LANES, j:j + LANES].T for i in range(0, r, LANES)], axis=1)
         for j in range(0, c, LANES)], axis=0)


def _row_parts(tm, want):
    parts = max(1, min(want, tm // LANES))
    step = tm // parts
    return [slice(r * step, (r + 1) * step) for r in range(parts)]


def _on_last_row_tile(fn):
    pl.when(pl.program_id(0) == pl.num_programs(0) - 1)(fn)


def _norm_matmul_kernel(x_ref, xs_ref, g_ref, w_ref, o_ref, os_ref):
    tn = o_ref.shape[1]
    w = w_ref[:, pl.ds(pl.multiple_of(pl.program_id(1) * tn, tn), tn)]
    for rows in _row_parts(x_ref.shape[0], 8):
        h = _rms(x_ref[rows, :], g_ref[...]).astype(BF16)
        o_ref[rows, :] = _dot(h, w)

    @_on_last_row_tile
    def _():
        os_ref[...] = _dot(_rms(xs_ref[...], g_ref[...]).astype(BF16), w)


def _norm_matmul(x, xs, g, w, tm, tn):
    m, k = x.shape
    ms = xs.shape[0]
    n = w.shape[1]
    ni = m // tm
    return pl.pallas_call(
        _norm_matmul_kernel,
        out_shape=(jax.ShapeDtypeStruct((m, n), F32), jax.ShapeDtypeStruct((ms, n), F32)),
        grid=(ni, n // tn),
        in_specs=[
            pl.BlockSpec((tm, k), lambda i, j: (i, 0)),
            pl.BlockSpec((ms, k), lambda i, j: (0, 0)),
            pl.BlockSpec((1, k), lambda i, j: (0, 0)),
            pl.BlockSpec((k, n), lambda i, j: (0, 0), pipeline_mode=pl.Buffered(1)),
        ],
        out_specs=(pl.BlockSpec((tm, tn), lambda i, j: (i, j)),
                   pl.BlockSpec((ms, tn), lambda i, j: (0, jnp.where(i == ni - 1, j, 0)))),
        compiler_params=_params("arbitrary", "arbitrary"),
        name="norm_w_in",
    )(x, xs, g.reshape(1, k), w)


def _s5_param_kernel(lam_ref, bc_ref, d_ref, cast_ref,
                     wt_ref, ws_ref, wc_ref, tab_ref, wb_ref, wct_ref, arow_ref, castb_ref):
    castb_ref[...] = cast_ref[...].astype(BF16)
    row_p = lax.broadcasted_iota(jnp.int32, (POW_ROWS, LANES), 0)
    lane = lax.broadcasted_iota(jnp.int32, (ROWS, LANES), 1)
    lane16 = lax.broadcasted_iota(jnp.int32, (S5_CH, LANES), 1)
    row16 = lax.broadcasted_iota(jnp.int32, (S5_CH, LANES), 0)
    lo = lane < S5_STATE

    wb_ref[...] = jnp.zeros_like(wb_ref)
    wct_ref[...] = jnp.zeros_like(wct_ref)

    for q in range(LANE_PAIRS):
        lr, li = lam_ref[0, q], lam_ref[1, q]
        step = jnp.exp(lam_ref[2, q])
        mag = jnp.exp(lr * step)
        ar = mag * jnp.cos(li * step)
        ai = mag * jnp.sin(li * step)
        nr, ni = ar - 1.0, ai
        den = lr * lr + li * li
        cr = (nr * lr + ni * li) / den
        ci = (ni * lr - nr * li) / den
        bt_r, bt_i = bc_ref[0, q], bc_ref[1, q]
        bb_r = cr * bt_r - ci * bt_i
        bb_i = cr * bt_i + ci * bt_r
        c_r, c_i = bc_ref[2, q], bc_ref[3, q]

        pr = jnp.ones((POW_ROWS, LANES), F32)
        pi = jnp.zeros((POW_ROWS, LANES), F32)
        sr, si = ar, ai
        sq = []
        for m in range(POW_ROWS.bit_length() - 1 + SUBLANES):
            sq.append((sr, si))
            if (1 << m) < POW_ROWS:
                bit = ((row_p >> m) & 1) == 1
                pr, pi = (jnp.where(bit, pr * sr - pi * si, pr),
                          jnp.where(bit, pr * si + pi * sr, pi))
            sr, si = sr * sr - si * si, 2.0 * sr * si
        m0 = CHUNK.bit_length() - 1
        scan_r = jnp.concatenate([sq[m0 + j][0] for j in range(SUBLANES)], axis=0)
        scan_i = jnp.concatenate([sq[m0 + j][1] for j in range(SUBLANES)], axis=0)
        pad = jnp.zeros((LANES - POW_ROWS - SUBLANES, LANES), F32)
        tab_ref[q, 0] = jnp.concatenate([pr, scan_r, pad], axis=0).T
        tab_ref[q, 1] = jnp.concatenate([pi, scan_i, pad], axis=0).T
        arow_ref[q, 0] = pr[:SUBLANES]
        arow_ref[q, 1] = pi[:SUBLANES]

        def expand(tab, k0, sign):
            return jnp.concatenate(
                [jnp.broadcast_to(tab[k0 + sign * s:k0 + sign * s + 1, :], (S5_CH, LANES))
                 for s in range(CHUNK)], axis=0)

        tile = lambda v: jnp.concatenate([v] * CHUNK, axis=0)

        e_r, e_i = expand(pr, CHUNK - 1, -1), expand(pi, CHUNK - 1, -1)
        t_r, t_i = tile(bb_r), tile(bb_i)
        ws_r = _transpose_tiles(t_r * e_r - t_i * e_i)
        ws_i = _transpose_tiles(t_r * e_i + t_i * e_r)
        for h in range(2):
            rows = slice(h * S5_STATE, (h + 1) * S5_STATE)
            ws_ref[2 * q + h] = jnp.concatenate([ws_r[rows], ws_i[rows]], axis=0).astype(BF16)

        e_r, e_i = expand(pr, 1, 1), expand(pi, 1, 1)
        t_r, t_i = tile(c_r), tile(c_i)
        ca_r = t_r * e_r - t_i * e_i
        ca_n = -(t_r * e_i + t_i * e_r)
        wc_ref[2 * q] = jnp.where(lo, ca_r, pltpu.roll(ca_n, S5_STATE, 1)).astype(BF16)
        wc_ref[2 * q + 1] = jnp.where(lo, pltpu.roll(ca_r, S5_STATE, 1), ca_n).astype(BF16)

        e_r, e_i = expand(pr, 0, 1), expand(pi, 0, 1)
        c0_r = t_r * e_r - t_i * e_i
        c0_i = t_r * e_i + t_i * e_r

        for h in range(2):
            g = 2 * q + h
            mine = (lane16 < S5_STATE) == (h == 0)
            bm_r = jnp.where(mine, bb_r, 0.0)
            bm_i = jnp.where(mine, bb_i, 0.0)
            kt = _dot_nt(bm_r, c0_r, HIGHEST) - _dot_nt(bm_i, c0_i, HIGHEST)
            k_lo = kt[:, :LANES] + jnp.where(row16 == lane16, d_ref[g], 0.0)
            k_hi = kt[:, LANES:]
            blocks = []
            half = LANES // S5_CH
            for s in range(CHUNK):
                sh = (s % half) * S5_CH
                keep = lane16 >= sh
                r_lo = pltpu.roll(k_lo, sh, 1) if sh else k_lo
                r_hi = pltpu.roll(k_hi, sh, 1) if sh else k_hi
                if s < half:
                    blk = jnp.concatenate([jnp.where(keep, r_lo, 0.0),
                                           jnp.where(keep, r_hi, r_lo)], axis=1)
                else:
                    blk = jnp.concatenate([jnp.zeros_like(r_lo), jnp.where(keep, r_lo, 0.0)], axis=1)
                blocks.append(blk)
            wt_ref[g] = _transpose_tiles(jnp.concatenate(blocks, axis=0)).astype(BF16)

            r0 = (2 * q + h) * S5_CH
            c0 = q * 2 * LANES
            wb_ref[r0:r0 + S5_CH, c0:c0 + LANES] = bm_r.astype(BF16)
            wb_ref[r0:r0 + S5_CH, c0 + LANES:c0 + 2 * LANES] = bm_i.astype(BF16)
            wct_ref[q, r0:r0 + S5_CH, :LANES] = jnp.where(mine, c_r, 0.0).astype(BF16)
            wct_ref[q, r0:r0 + S5_CH, LANES:] = jnp.where(mine, -c_i, 0.0).astype(BF16)


def _s5_params(lam_re, lam_im, log_step, b_re, b_im, c_re, c_im, d, cast_w):
    groups = lam_re.shape[0]
    pairs = groups // 2
    slabs = groups // LANE_GROUPS
    cast_blk = pl.BlockSpec((cast_w.shape[0] // slabs, cast_w.shape[1]), lambda i: (i, 0))
    ls = jnp.broadcast_to(log_step[:, None], (groups, S5_STATE))
    lam = jnp.stack([lam_re, lam_im, ls]).reshape(3, pairs, 1, LANES)
    bc = jnp.stack([jnp.transpose(b_re, (0, 2, 1)), jnp.transpose(b_im, (0, 2, 1)), c_re, c_im])
    bc = jnp.transpose(bc.reshape(4, pairs, 2, S5_CH, S5_STATE), (0, 1, 3, 2, 4)).reshape(
        4, pairs, S5_CH, LANES)
    d_pad = jnp.pad(d.reshape(groups, 1, S5_CH), ((0, 0), (0, 0), (0, LANES - S5_CH)))
    blk3 = lambda i: (i, 0, 0)
    blk4 = lambda i: (i, 0, 0, 0)
    pblk = lambda i: (0, i, 0, 0)
    return pl.pallas_call(
        _s5_param_kernel,
        out_shape=(jax.ShapeDtypeStruct((groups, ROWS, ROWS), BF16),
                   jax.ShapeDtypeStruct((groups, 2 * S5_STATE, ROWS), BF16),
                   jax.ShapeDtypeStruct((groups, ROWS, 2 * S5_STATE), BF16),
                   jax.ShapeDtypeStruct((pairs, 2, LANES, LANES), F32),
                   jax.ShapeDtypeStruct((slabs, LANES, LANE_PAIRS * 2 * LANES), BF16),
                   jax.ShapeDtypeStruct((slabs, LANE_PAIRS, LANES, 2 * LANES), BF16),
                   jax.ShapeDtypeStruct((pairs, 2, SUBLANES, LANES), F32),
                   jax.ShapeDtypeStruct(cast_w.shape, BF16)),
        grid=(slabs,),
        in_specs=[pl.BlockSpec((3, LANE_PAIRS, 1, LANES), pblk),
                  pl.BlockSpec((4, LANE_PAIRS, S5_CH, LANES), pblk),
                  pl.BlockSpec((LANE_GROUPS, 1, LANES), blk3), cast_blk],
        out_specs=(pl.BlockSpec((LANE_GROUPS, ROWS, ROWS), blk3),
                   pl.BlockSpec((LANE_GROUPS, 2 * S5_STATE, ROWS), blk3),
                   pl.BlockSpec((LANE_GROUPS, ROWS, 2 * S5_STATE), blk3),
                   pl.BlockSpec((LANE_PAIRS, 2, LANES, LANES), blk4),
                   pl.BlockSpec((None, LANES, LANE_PAIRS * 2 * LANES), blk3),
                   pl.BlockSpec((None, LANE_PAIRS, LANES, 2 * LANES), blk4),
                   pl.BlockSpec((LANE_PAIRS, 2, SUBLANES, LANES), blk4),
                   cast_blk),
        compiler_params=_params("parallel"),
        name="s5_params",
    )(lam, bc, d_pad, cast_w)


def _s5_seq_kernel(z_ref, wt_ref, ws_ref, wc_ref, tab_ref, cast_ref, y_ref, hf_ref, castb_ref,
                   ut_ref, yt_ref, sr_ref, si_ref, *, nb, n_chunks):
    castb_ref[...] = cast_ref[...].astype(BF16)
    scan_steps = n_chunks.bit_length() - 1
    for n in range(nb):
        for s in range(CHUNK):
            xs = z_ref[n, pl.ds(s, n_chunks, stride=CHUNK), :]
            ut_ref[:, s * S5_CH:(s + 1) * S5_CH, n * n_chunks:(n + 1) * n_chunks] = (
                xs.T.astype(BF16).reshape(LANE_GROUPS, S5_CH, n_chunks))

    for g in range(LANE_GROUPS):
        st = _dot(ws_ref[g], ut_ref[g])
        sr_ref[g * S5_STATE:(g + 1) * S5_STATE, :] = st[:S5_STATE]
        si_ref[g * S5_STATE:(g + 1) * S5_STATE, :] = st[S5_STATE:]

    srows = LANE_GROUPS * S5_STATE
    lane = lax.broadcasted_iota(jnp.int32, (srows, n_chunks), 1)
    tab_r = jnp.concatenate([tab_ref[q, 0] for q in range(LANE_PAIRS)], axis=0)
    tab_i = jnp.concatenate([tab_ref[q, 1] for q in range(LANE_PAIRS)], axis=0)
    fin_r = jnp.zeros((srows, n_chunks), F32)
    fin_i = jnp.zeros((srows, n_chunks), F32)
    for n in range(nb):
        cols = slice(n * n_chunks, (n + 1) * n_chunks)
        xr = sr_ref[:, cols]
        xi = si_ref[:, cols]
        for k in range(scan_steps):
            d = 1 << k
            keep = lane >= d
            sr = jnp.where(keep, pltpu.roll(xr, d, 1), 0.0)
            si = jnp.where(keep, pltpu.roll(xi, d, 1), 0.0)
            pr = tab_r[:, SCAN_ROW0 + k:SCAN_ROW0 + k + 1]
            pi = tab_i[:, SCAN_ROW0 + k:SCAN_ROW0 + k + 1]
            xr, xi = xr + (pr * sr - pi * si), xi + (pr * si + pi * sr)
        fin_r = jnp.where(lane == n, pltpu.roll(xr, n + 1, 1), fin_r)
        fin_i = jnp.where(lane == n, pltpu.roll(xi, n + 1, 1), fin_i)
        keep = lane >= 1
        sr_ref[:, cols] = jnp.where(keep, pltpu.roll(xr, 1, 1), 0.0)
        si_ref[:, cols] = jnp.where(keep, pltpu.roll(xi, 1, 1), 0.0)

    for g in range(LANE_GROUPS):
        rows = slice(g * S5_STATE, (g + 1) * S5_STATE)
        hf_ref[g, 0] = fin_r[rows]
        hf_ref[g, 1] = fin_i[rows]
        hs = jnp.concatenate([sr_ref[rows, :], si_ref[rows, :]], axis=0).astype(BF16)
        yt_ref[g] = jax.nn.gelu(_dot(wc_ref[g], hs) + _dot(wt_ref[g], ut_ref[g]))

    for n in range(nb):
        for s in range(CHUNK):
            blk = yt_ref[:, s * S5_CH:(s + 1) * S5_CH, n * n_chunks:(n + 1) * n_chunks]
            y_ref[n, pl.ds(s, n_chunks, stride=CHUNK), :] = blk.reshape(LANES, n_chunks).T


def _s5_seq(z, w_t, w_s, w_c, tab, d_s5, cast_w):
    nb, t_len, _ = z.shape
    n_chunks = t_len // CHUNK
    assert n_chunks == LANES, "the chunk axis must fill one 128-lane tile"
    groups = d_s5 // S5_CH
    steps = groups // LANE_GROUPS
    cast_blk = pl.BlockSpec((cast_w.shape[0] // steps, cast_w.shape[1]), lambda i: (i, 0))
    kern = functools.partial(_s5_seq_kernel, nb=nb, n_chunks=n_chunks)
    blk3 = lambda i: (i, 0, 0)
    blk4 = lambda i: (i, 0, 0, 0)
    return pl.pallas_call(
        kern,
        out_shape=(jax.ShapeDtypeStruct((nb, t_len, d_s5), F32),
                   jax.ShapeDtypeStruct((groups, 2, S5_STATE, LANES), F32),
                   jax.ShapeDtypeStruct(cast_w.shape, BF16)),
        grid=(steps,),
        in_specs=[
            pl.BlockSpec((nb, t_len, LANES), lambda i: (0, 0, i)),
            pl.BlockSpec((LANE_GROUPS, ROWS, ROWS), blk3),
            pl.BlockSpec((LANE_GROUPS, 2 * S5_STATE, ROWS), blk3),
            pl.BlockSpec((LANE_GROUPS, ROWS, 2 * S5_STATE), blk3),
            pl.BlockSpec((LANE_PAIRS, 2, LANES, LANES), blk4),
            cast_blk,
        ],
        out_specs=(pl.BlockSpec((nb, t_len, LANES), lambda i: (0, 0, i)),
                   pl.BlockSpec((LANE_GROUPS, 2, S5_STATE, LANES), blk4),
                   cast_blk),
        scratch_shapes=[pltpu.VMEM((LANE_GROUPS, ROWS, nb * n_chunks), BF16),
                        pltpu.VMEM((LANE_GROUPS, ROWS, nb * n_chunks), F32),
                        pltpu.VMEM((LANE_GROUPS * S5_STATE, nb * n_chunks), F32),
                        pltpu.VMEM((LANE_GROUPS * S5_STATE, nb * n_chunks), F32)],
        compiler_params=_params("parallel"),
        name="s5_seq",
    )(z, w_t, w_s, w_c, tab, cast_w)


def _s5_step_kernel(u_ref, h0r_ref, h0i_ref, wb_ref, wct_ref, arow_ref, d_ref,
                    y_ref, hr_ref, hi_ref):
    u = u_ref[...]
    bu = _dot(u.astype(BF16), wb_ref[...])
    y = d_ref[...] * u
    for q in range(LANE_PAIRS):
        cols = slice(q * LANES, (q + 1) * LANES)
        ar = arow_ref[q, 0, 1:2, :]
        ai = arow_ref[q, 1, 1:2, :]
        h0r, h0i = h0r_ref[:, cols], h0i_ref[:, cols]
        hr = ar * h0r - ai * h0i + bu[:, 2 * q * LANES:(2 * q + 1) * LANES]
        hi = ar * h0i + ai * h0r + bu[:, (2 * q + 1) * LANES:(2 * q + 2) * LANES]
        hr_ref[:, cols] = hr
        hi_ref[:, cols] = hi
        y = y + _dot_nt(jnp.concatenate([hr, hi], axis=1).astype(BF16), wct_ref[q])
    y_ref[...] = jax.nn.gelu(y).astype(BF16)


def _s5_step(z, h0r, h0i, wb, wct, arow, d):
    n = z.shape[0]
    slabs = wb.shape[0]
    sw = LANE_PAIRS * LANES
    col = lambda i: (0, i)
    blk3 = lambda i: (i, 0, 0)
    blk4 = lambda i: (i, 0, 0, 0)
    return pl.pallas_call(
        _s5_step_kernel,
        out_shape=(jax.ShapeDtypeStruct((n, slabs * LANES), BF16),
                   jax.ShapeDtypeStruct((n, slabs * sw), F32),
                   jax.ShapeDtypeStruct((n, slabs * sw), F32)),
        grid=(slabs,),
        in_specs=[
            pl.BlockSpec((n, LANES), col),
            pl.BlockSpec((n, sw), col),
            pl.BlockSpec((n, sw), col),
            pl.BlockSpec((None, LANES, 2 * sw), blk3),
            pl.BlockSpec((None, LANE_PAIRS, LANES, 2 * LANES), blk4),
            pl.BlockSpec((LANE_PAIRS, 2, SUBLANES, LANES), blk4),
            pl.BlockSpec((1, LANES), col),
        ],
        out_specs=(pl.BlockSpec((n, LANES), col),
                   pl.BlockSpec((n, sw), col),
                   pl.BlockSpec((n, sw), col)),
        compiler_params=_params("parallel"),
        name="s5_step",
    )(z, h0r, h0i, wb, wct, arow, d)


def _lru_gate_block(xc, wa, ba, wi, bi, lam):
    xb16 = xc.astype(BF16)
    r = jax.nn.sigmoid(_dot(xb16, wa) + ba)
    ig = jax.nn.sigmoid(_dot(xb16, wi) + bi)
    log_a = -LRU_C * r * jax.nn.softplus(-lam)
    a = jnp.exp(log_a)
    mult = jnp.sqrt(1.0 - a * a)
    return a, mult * (ig * xc)


def _lru_gates(xc, wa_ref, ba, wi_ref, bi, lam):
    blk = xc.shape[1] // LRU_HEADS
    parts = [_lru_gate_block(xc[:, h * blk:(h + 1) * blk], wa_ref[h], ba[:, h * blk:(h + 1) * blk],
                             wi_ref[h], bi[:, h * blk:(h + 1) * blk],
                             lam[:, h * blk:(h + 1) * blk]) for h in range(LRU_HEADS)]
    return (jnp.concatenate([p[0] for p in parts], axis=-1),
            jnp.concatenate([p[1] for p in parts], axis=-1))


def _lru_seq_kernel(xb_ref, gb_ref, cw_ref, cb_ref, wa_ref, ba_ref, wi_ref, bi_ref, lam_ref,
                    gm_ref, cast_ref, o_ref, hl_ref, castb_ref, xe_ref, a_ref, b_ref, hc_ref,
                    *, tt):
    castb_ref[...] = cast_ref[...].astype(BF16)
    halo = SUBLANES
    nseq, _, d = xb_ref.shape
    blk = d // LRU_HEADS
    pitch = a_ref.shape[1] // nseq

    @pl.when(pl.program_id(0) == 0)
    def _():
        xe_ref[:, 0:halo, :] = jnp.zeros((nseq, halo, d), F32)
        hc_ref[...] = jnp.zeros_like(hc_ref)

    xe_ref[:, halo:halo + tt, :] = xb_ref[...]
    for h in range(LRU_HEADS):
        cols = slice(h * blk, (h + 1) * blk)
        cw = cw_ref[:, cols]
        xc = cb_ref[:, cols] + xe_ref[:, halo:halo + tt, cols] * cw[CONV_W - 1:CONV_W, :]
        for k in range(1, CONV_W):
            xc = xc + xe_ref[:, halo - k:halo - k + tt, cols] * cw[CONV_W - 1 - k:CONV_W - k, :]
        a, b = _lru_gate_block(xc.reshape(nseq * tt, blk), wa_ref[h], ba_ref[:, cols],
                               wi_ref[h], bi_ref[:, cols], lam_ref[:, cols])
        for n in range(nseq):
            for s in range(blk // LANES):
                slab = h * (blk // LANES) + s
                rows = slice(n * pitch, n * pitch + tt)
                a_ref[slab, rows, :] = a[n * tt:(n + 1) * tt, s * LANES:(s + 1) * LANES]
                b_ref[slab, rows, :] = b[n * tt:(n + 1) * tt, s * LANES:(s + 1) * LANES]
    xe_ref[:, 0:halo, :] = xb_ref[:, tt - halo:tt, :]

    slabs = d // LANES

    def block(i, hs):
        base = pl.multiple_of(i * SUBLANES, SUBLANES)
        hs = list(hs)
        for j in range(SUBLANES):
            rows = pl.ds(base + j, nseq, stride=pitch)
            for s in range(slabs):
                hs[s] = a_ref[s, rows, :] * hs[s] + b_ref[s, rows, :]
                b_ref[s, rows, :] = hs[s]
        return tuple(hs)

    hs = lax.fori_loop(0, tt // SUBLANES, block, tuple(hc_ref[s] for s in range(slabs)))
    for s in range(slabs):
        hc_ref[s] = hs[s]
        hl_ref[s] = hs[s]

    h_all = jnp.stack([jnp.concatenate([b_ref[s, n * pitch:n * pitch + tt, :] for s in range(slabs)],
                                       axis=-1) for n in range(nseq)])
    out = h_all * jax.nn.gelu(gb_ref[...])
    o_ref[...] = _rms(out, gm_ref[...]).astype(BF16)


def _lru_seq(z, conv_w, conv_b, w_a, b_a, w_i, b_i, lam, g_merge, cast_w, tt):
    nseq, t_len, _ = z.shape
    d = conv_w.shape[1]
    steps = t_len // tt
    cast_blk = pl.BlockSpec((cast_w.shape[0] // steps, cast_w.shape[1]), lambda t: (t, 0))
    row = lambda v: v.reshape(1, d)
    const2 = lambda t: (0, 0)
    const3 = lambda t: (0, 0, 0)
    kern = functools.partial(_lru_seq_kernel, tt=tt)
    return pl.pallas_call(
        kern,
        out_shape=(jax.ShapeDtypeStruct((nseq, t_len, d), BF16),
                   jax.ShapeDtypeStruct((d // LANES, nseq, LANES), F32),
                   jax.ShapeDtypeStruct(cast_w.shape, BF16)),
        grid=(steps,),
        in_specs=[
            pl.BlockSpec((nseq, tt, d), lambda t: (0, t, 1)),
            pl.BlockSpec((nseq, tt, d), lambda t: (0, t, 2)),
            pl.BlockSpec((CONV_W, d), const2),
            pl.BlockSpec((1, d), const2),
            pl.BlockSpec(w_a.shape, const3),
            pl.BlockSpec((1, d), const2),
            pl.BlockSpec(w_i.shape, const3),
            pl.BlockSpec((1, d), const2),
            pl.BlockSpec((1, d), const2),
            pl.BlockSpec((1, d), const2),
            cast_blk,
        ],
        out_specs=(pl.BlockSpec((nseq, tt, d), lambda t: (0, t, 0)),
                   pl.BlockSpec((d // LANES, nseq, LANES), const3),
                   cast_blk),
        scratch_shapes=[pltpu.VMEM((nseq, tt + SUBLANES, d), F32),
                        pltpu.VMEM((d // LANES, nseq * (tt + SUBLANES), LANES), F32),
                        pltpu.VMEM((d // LANES, nseq * (tt + SUBLANES), LANES), F32),
                        pltpu.VMEM((d // LANES, nseq, LANES), F32)],
        compiler_params=_params("arbitrary"),
        name="rglru_seq",
    )(z, z, conv_w, row(conv_b), w_a, row(b_a), w_i, row(b_i), row(lam), row(g_merge), cast_w)


def _lru_step_kernel(xb_ref, gb_ref, c0_ref, c1_ref, c2_ref, h0_ref, cw_ref, cb_ref,
                     wa_ref, ba_ref, wi_ref, bi_ref, lam_ref, gm_ref, o_ref, h_ref, cv_ref):
    d = xb_ref.shape[1]
    xb = xb_ref[...]
    cw = cw_ref[...]
    xc = (cb_ref[...] + c0_ref[...] * cw[0:1, :] + c1_ref[...] * cw[1:2, :]
          + c2_ref[...] * cw[2:3, :] + xb * cw[3:4, :])
    a, b = _lru_gates(xc, wa_ref, ba_ref[...], wi_ref, bi_ref[...], lam_ref[...])
    h = a * h0_ref[...] + b
    h_ref[...] = h
    out = h * jax.nn.gelu(gb_ref[...])
    o_ref[...] = _rms(out, gm_ref[...]).astype(BF16)
    cv_ref[:, 0:d] = c1_ref[...]
    cv_ref[:, d:2 * d] = c2_ref[...]
    cv_ref[:, 2 * d:3 * d] = xb


def _lru_step(z, conv0, h0, conv_w, conv_b, w_a, b_a, w_i, b_i, lam, g_merge):
    n = z.shape[0]
    d = conv_w.shape[1]
    taps = CONV_W - 1
    conv2d = conv0.reshape(n, taps * d)
    row = lambda v: v.reshape(1, d)
    full = lambda shape: pl.BlockSpec(shape, lambda i: (0,) * len(shape))
    col = lambda c: pl.BlockSpec((n, d), lambda i: (0, c))
    nd = full((n, d))
    rd = full((1, d))
    o, h, cv = pl.pallas_call(
        _lru_step_kernel,
        out_shape=(jax.ShapeDtypeStruct((n, d), BF16), jax.ShapeDtypeStruct((n, d), F32),
                   jax.ShapeDtypeStruct((n, taps * d), F32)),
        grid=(1,),
        in_specs=[
            col(1), col(2), col(0), col(1), col(2), nd,
            full((CONV_W, d)), rd,
            full(w_a.shape), rd, full(w_i.shape), rd, rd, rd,
        ],
        out_specs=(nd, nd, full((n, taps * d))),
        compiler_params=_params("arbitrary"),
        name="rglru_step",
    )(z, z, conv2d, conv2d, conv2d, h0, conv_w, row(conv_b),
      w_a, row(b_a), w_i, row(b_i), row(lam), row(g_merge))
    return o, h, cv.reshape(n, taps, d)


def _glu_rows(y, w_ref, b_ref, g_ref):
    gate = jax.nn.sigmoid(_dot(y.astype(BF16), w_ref[...]) + b_ref[...])
    return _rms(y.astype(F32) * gate, g_ref[...]).astype(BF16)


def _glu_kernel(y_ref, ys_ref, w_ref, b_ref, g_ref, o_ref, os_ref):
    for rows in _row_parts(y_ref.shape[0], 4):
        o_ref[rows, :] = _glu_rows(y_ref[rows, :], w_ref, b_ref, g_ref)

    @_on_last_row_tile
    def _():
        os_ref[...] = _glu_rows(ys_ref[...], w_ref, b_ref, g_ref)


def _glu(y, ys, w, b, g, tm):
    m, d = y.shape
    ms = ys.shape[0]
    const = lambda i: (0, 0)
    return pl.pallas_call(
        _glu_kernel,
        out_shape=(jax.ShapeDtypeStruct((m, d), BF16), jax.ShapeDtypeStruct((ms, d), BF16)),
        grid=(m // tm,),
        in_specs=[
            pl.BlockSpec((tm, d), lambda i: (i, 0)),
            pl.BlockSpec((ms, d), const),
            pl.BlockSpec((d, d), const),
            pl.BlockSpec((1, d), const),
            pl.BlockSpec((1, d), const),
        ],
        out_specs=(pl.BlockSpec((tm, d), lambda i: (i, 0)), pl.BlockSpec((ms, d), const)),
        compiler_params=_params("arbitrary"),
        name="s5_glu",
    )(y, ys, w, b.reshape(1, d), g.reshape(1, d))


def _out_proj_kernel(x_ref, ma_ref, mb_ref, xs_ref, mas_ref, mbs_ref, w_ref, o_ref, os_ref, wb_ref):
    @pl.when(pl.program_id(0) == 0)
    def _():
        wb_ref[...] = w_ref[...].astype(BF16)

    ka = ma_ref.shape[1]
    proj = lambda x, ma, mb: x + _dot(ma, wb_ref[:ka, :]) + _dot(mb, wb_ref[ka:, :])
    for rows in _row_parts(x_ref.shape[0], 2):
        o_ref[rows, :] = proj(x_ref[rows, :], ma_ref[rows, :], mb_ref[rows, :])

    @_on_last_row_tile
    def _():
        os_ref[...] = proj(xs_ref[...], mas_ref[...], mbs_ref[...])


def _out_proj(x, ma, mb, xs, mas, mbs, w, tm):
    m, n = x.shape
    ms = xs.shape[0]
    ka = ma.shape[1]
    kb = mb.shape[1]
    const = lambda i: (0, 0)
    tile = lambda i: (i, 0)
    return pl.pallas_call(
        _out_proj_kernel,
        out_shape=(jax.ShapeDtypeStruct((m, n), F32), jax.ShapeDtypeStruct((ms, n), F32)),
        grid=(m // tm,),
        in_specs=[
            pl.BlockSpec((tm, n), tile),
            pl.BlockSpec((tm, ka), tile),
            pl.BlockSpec((tm, kb), tile),
            pl.BlockSpec((ms, n), const),
            pl.BlockSpec((ms, ka), const),
            pl.BlockSpec((ms, kb), const),
            pl.BlockSpec((ka + kb, n), const, pipeline_mode=pl.Buffered(1)),
        ],
        out_specs=(pl.BlockSpec((tm, n), tile), pl.BlockSpec((ms, n), const)),
        scratch_shapes=[pltpu.VMEM((ka + kb, n), BF16)],
        compiler_params=_params("arbitrary"),
        name="out_proj",
    )(x, ma, mb, xs, mas, mbs, w)


def _mlp_kernel(x_ref, xs_ref, g_ref, wu_ref, wd_ref, o_ref, os_ref, h_ref, hs_ref):
    def ffn(h):
        act = jnp.square(jnp.maximum(_dot(h, wu_ref[...]), 0.0)).astype(BF16)
        return _dot(act, wd_ref[...])

    def rows(x_ref, o_ref, h_ref):
        first = pl.program_id(1) == 0

        @pl.when(first)
        def _():
            for part in _row_parts(x_ref.shape[0], 4):
                x = x_ref[part, :]
                h = _rms(x, g_ref[...]).astype(BF16)
                h_ref[part, :] = h
                o_ref[part, :] = x + ffn(h)

        @pl.when(jnp.logical_not(first))
        def _():
            o_ref[...] += ffn(h_ref[...])

    rows(x_ref, o_ref, h_ref)
    _on_last_row_tile(lambda: rows(xs_ref, os_ref, hs_ref))


def _mlp(x, xs, g, w_up, w_down, tm, tf):
    m, d = x.shape
    ms = xs.shape[0]
    f = w_up.shape[1]
    const = lambda i, j: (0, 0)
    return pl.pallas_call(
        _mlp_kernel,
        out_shape=(jax.ShapeDtypeStruct((m, d), F32), jax.ShapeDtypeStruct((ms, d), F32)),
        grid=(m // tm, f // tf),
        in_specs=[
            pl.BlockSpec((tm, d), lambda i, j: (i, 0)),
            pl.BlockSpec((ms, d), const),
            pl.BlockSpec((1, d), const),
            pl.BlockSpec((d, tf), lambda i, j: (0, j)),
            pl.BlockSpec((tf, d), lambda i, j: (j, 0)),
        ],
        out_specs=(pl.BlockSpec((tm, d), lambda i, j: (i, 0)), pl.BlockSpec((ms, d), const)),
        scratch_shapes=[pltpu.VMEM((tm, d), BF16), pltpu.VMEM((ms, d), BF16)],
        compiler_params=_params("arbitrary", "arbitrary"),
        name="mlp",
    )(x, xs, g.reshape(1, d), w_up, w_down)


def _ple_kernel(x_ref, p_ref, xs_ref, ps_ref, g_ref, wg_ref, wp_ref, gf_ref, o_ref, os_ref,
                wgb_ref, wpb_ref, *, final):
    @pl.when(pl.program_id(0) == 0)
    def _():
        wgb_ref[...] = wg_ref[...].astype(BF16)
        wpb_ref[...] = wp_ref[...].astype(BF16)

    def ple(x, p):
        gate = jax.nn.sigmoid(_dot(_rms(x, g_ref[...]).astype(BF16), wgb_ref[...]))
        pe = _dot(p.astype(BF16), wpb_ref[...])
        x = x + pe * gate
        return _rms(x, gf_ref[...]) if final else x

    for rows in _row_parts(x_ref.shape[0], 2):
        o_ref[rows, :] = ple(x_ref[rows, :], p_ref[rows, :])

    @_on_last_row_tile
    def _():
        os_ref[...] = ple(xs_ref[...], ps_ref[...])


def _ple(x, p, xs, ps, g, w_gate, w_ple, g_final, tm, final):
    m, d = x.shape
    ms = xs.shape[0]
    dp = p.shape[1]
    once = pl.Buffered(1)
    const = lambda i: (0, 0)
    tile = lambda i: (i, 0)
    return pl.pallas_call(
        functools.partial(_ple_kernel, final=final),
        out_shape=(jax.ShapeDtypeStruct((m, d), F32), jax.ShapeDtypeStruct((ms, d), F32)),
        grid=(m // tm,),
        in_specs=[
            pl.BlockSpec((tm, d), tile),
            pl.BlockSpec((tm, dp), tile),
            pl.BlockSpec((ms, d), const),
            pl.BlockSpec((ms, dp), const),
            pl.BlockSpec((1, d), const),
            pl.BlockSpec((d, d), const, pipeline_mode=once),
            pl.BlockSpec((dp, d), const, pipeline_mode=once),
            pl.BlockSpec((1, d), const),
        ],
        out_specs=(pl.BlockSpec((tm, d), tile), pl.BlockSpec((ms, d), const)),
        scratch_shapes=[pltpu.VMEM((d, d), BF16), pltpu.VMEM((dp, d), BF16)],
        compiler_params=_params("arbitrary"),
        name="ple_final",
    )(x, p, xs, ps, g.reshape(1, d), w_gate, w_ple, g_final.reshape(1, d))


TT_LRU = 64
TM_IN, TN_IN = 1024, 1536
TM_GLU = 1024
TM_OUT = 512
TM_MLP, TF_MLP = 512, 1024
TM_PLE = 512


def kernel(x_prompt, x_sample, state_s5_re, state_s5_im, state_lru, state_conv, p_prompt, p_sample,
           g_mix, w_in, s5_lam_re, s5_lam_im, s5_log_step, s5_b_re, s5_b_im, s5_c_re, s5_c_im, s5_d,
           s5_w_glu, s5_b_glu, conv_w, conv_b, lru_w_a, lru_b_a, lru_w_i, lru_b_i, lru_lam,
           g_merge_a, g_merge_b, w_out, g_mlp, w_up, w_down, g_ple, w_ple_gate, w_ple, g_final):
    depth = g_mix.shape[0]
    nb, t_len, d_model = x_prompt.shape
    ns = x_sample.shape[0]
    d_s5 = s5_d.shape[1]
    d_lru = conv_w.shape[2]
    groups = d_s5 // S5_CH

    xp = x_prompt.reshape(nb * t_len, d_model)
    xs = x_sample.reshape(ns, d_model)
    outs = [[] for _ in range(8)]
    for l in range(depth):
        final = l == depth - 1
        w_glu_b = s5_w_glu[l].astype(BF16)
        lru_args = (conv_w[l], conv_b[l], lru_w_a[l].astype(BF16), lru_b_a[l],
                    lru_w_i[l].astype(BF16), lru_b_i[l], lru_lam[l], g_merge_b[l])
        w_t, w_s, w_c, tab, wb, wct, arow, w_in_b = _s5_params(
            s5_lam_re[l], s5_lam_im[l], s5_log_step[l], s5_b_re[l], s5_b_im[l],
            s5_c_re[l], s5_c_im[l], s5_d[l], w_in[l])

        z, zs = _norm_matmul(xp, xs, g_mix[l], w_in_b, TM_IN, TN_IN)
        z4 = z.reshape(nb, t_len, 3 * d_s5)
        y, hf, w_down_b = _s5_seq(z4, w_t, w_s, w_c, tab, d_s5, w_down[l])
        mb, lru_h, w_up_b = _lru_seq(z4, *lru_args, w_up[l], TT_LRU)
        ys, hsr, hsi = _s5_step(zs, state_s5_re[l].reshape(ns, groups * S5_STATE),
                                state_s5_im[l].reshape(ns, groups * S5_STATE),
                                wb, wct, arow, s5_d[l].reshape(1, d_s5))
        mbs, lru_hs, conv_s = _lru_step(zs, state_conv[l], state_lru[l], *lru_args)
        ma, mas = _glu(y.reshape(nb * t_len, d_s5), ys, w_glu_b, s5_b_glu[l], g_merge_a[l], TM_GLU)
        xp, xs = _out_proj(xp, ma, mb.reshape(nb * t_len, d_lru), xs, mas, mbs, w_out[l], TM_OUT)
        xp, xs = _mlp(xp, xs, g_mlp[l], w_up_b, w_down_b, TM_MLP, TF_MLP)
        xp, xs = _ple(xp, p_prompt[l].reshape(nb * t_len, -1), xs, p_sample[l].reshape(ns, -1),
                      g_ple[l], w_ple_gate[l], w_ple[l], g_final, TM_PLE, final)
        hf = jnp.transpose(hf[:, :, :, :nb], (1, 3, 0, 2))
        outs[0].append(hf[0])
        outs[1].append(hf[1])
        outs[2].append(jnp.transpose(lru_h, (1, 0, 2)).reshape(nb, d_lru))
        outs[3].append(z4[:, t_len - (CONV_W - 1):, d_s5:d_s5 + d_lru])
        outs[4].append(hsr.reshape(ns, groups, S5_STATE))
        outs[5].append(hsi.reshape(ns, groups, S5_STATE))
        outs[6].append(lru_hs)
        outs[7].append(conv_s)
    return (xp.reshape(nb, t_len, d_model), xs.reshape(ns, 1, d_model),
            *(jnp.stack(o) for o in outs))
```

```python
import functools

import jax
import jax.numpy as jnp
from jax import lax
from jax.experimental import pallas as pl
from jax.experimental.pallas import tpu as pltpu

F32 = jnp.float32
BF16 = jnp.bfloat16
HIGHEST = lax.Precision.HIGHEST

EPS = 1e-6
LRU_C = 8.0
S5_CH = 16
S5_STATE = 64
LRU_HEADS = 4
CONV_W = 4
CHUNK = 16
SUBLANES = 8
LANES = 128
LANE_GROUPS = LANES // S5_CH
LANE_PAIRS = LANE_GROUPS // 2
ROWS = CHUNK * S5_CH
POW_ROWS = 32
SCAN_ROW0 = POW_ROWS
VMEM_LIMIT = 56 * 1024 * 1024


def _params(*sem):
    return pltpu.CompilerParams(dimension_semantics=sem, vmem_limit_bytes=VMEM_LIMIT)


def _rms(x, g):
    return x * lax.rsqrt(jnp.mean(x * x, axis=-1, keepdims=True) + EPS) * g


def _dot(a, b):
    return jnp.dot(a, b, preferred_element_type=F32)


def _dot_nt(a, b, precision=None):
    return lax.dot_general(a, b, (((1,), (1,)), ((), ())), precision=precision,
                           preferred_element_type=F32)


def _transpose_tiles(x):
    r, c = x.shape
    return jnp.concatenate(
        [jnp.concatenate([x[i:i + LANES, j:j + LANES].T for i in range(0, r, LANES)], axis=1)
         for j in range(0, c, LANES)], axis=0)


def _row_parts(tm, want):
    parts = max(1, min(want, tm // LANES))
    step = tm // parts
    return [slice(r * step, (r + 1) * step) for r in range(parts)]


def _on_last_row_tile(fn):
    pl.when(pl.program_id(0) == pl.num_programs(0) - 1)(fn)


def _norm_matmul_kernel(x_ref, xs_ref, g_ref, w_ref, o_ref, os_ref):
    tn = o_ref.shape[1]
    w = w_ref[:, pl.ds(pl.multiple_of(pl.program_id(1) * tn, tn), tn)]
    for rows in _row_parts(x_ref.shape[0], 4):
        h = _rms(x_ref[rows, :], g_ref[...]).astype(BF16)
        o_ref[rows, :] = _dot(h, w)

    @_on_last_row_tile
    def _():
        os_ref[...] = _dot(_rms(xs_ref[...], g_ref[...]).astype(BF16), w)


def _norm_matmul(x, xs, g, w, tm, tn):
    m, k = x.shape
    ms = xs.shape[0]
    n = w.shape[1]
    ni = m // tm
    return pl.pallas_call(
        _norm_matmul_kernel,
        out_shape=(jax.ShapeDtypeStruct((m, n), F32), jax.ShapeDtypeStruct((ms, n), F32)),
        grid=(ni, n // tn),
        in_specs=[
            pl.BlockSpec((tm, k), lambda i, j: (i, 0)),
            pl.BlockSpec((ms, k), lambda i, j: (0, 0)),
            pl.BlockSpec((1, k), lambda i, j: (0, 0)),
            pl.BlockSpec((k, n), lambda i, j: (0, 0), pipeline_mode=pl.Buffered(1)),
        ],
        out_specs=(pl.BlockSpec((tm, tn), lambda i, j: (i, j)),
                   pl.BlockSpec((ms, tn), lambda i, j: (0, jnp.where(i == ni - 1, j, 0)))),
        compiler_params=_params("arbitrary", "arbitrary"),
        name="norm_w_in",
    )(x, xs, g.reshape(1, k), w)


def _s5_param_kernel(lam_ref, bc_ref, d_ref, cast_ref,
                     wt_ref, ws_ref, wc_ref, tab_ref, wb_ref, wct_ref, arow_ref, castb_ref):
    castb_ref[...] = cast_ref[...].astype(BF16)
    row_p = lax.broadcasted_iota(jnp.int32, (POW_ROWS, LANES), 0)
    lane = lax.broadcasted_iota(jnp.int32, (ROWS, LANES), 1)
    lane16 = lax.broadcasted_iota(jnp.int32, (S5_CH, LANES), 1)
    row16 = lax.broadcasted_iota(jnp.int32, (S5_CH, LANES), 0)
    lo = lane < S5_STATE

    wb_ref[...] = jnp.zeros_like(wb_ref)
    wct_ref[...] = jnp.zeros_like(wct_ref)

    for q in range(LANE_PAIRS):
        lr, li = lam_ref[0, q], lam_ref[1, q]
        step = jnp.exp(lam_ref[2, q])
        mag = jnp.exp(lr * step)
        ar = mag * jnp.cos(li * step)
        ai = mag * jnp.sin(li * step)
        nr, ni = ar - 1.0, ai
        den = lr * lr + li * li
        cr = (nr * lr + ni * li) / den
        ci = (ni * lr - nr * li) / den
        bt_r, bt_i = bc_ref[0, q], bc_ref[1, q]
        bb_r = cr * bt_r - ci * bt_i
        bb_i = cr * bt_i + ci * bt_r
        c_r, c_i = bc_ref[2, q], bc_ref[3, q]

        pr = jnp.ones((POW_ROWS, LANES), F32)
        pi = jnp.zeros((POW_ROWS, LANES), F32)
        sr, si = ar, ai
        sq = []
        for m in range(POW_ROWS.bit_length() - 1 + SUBLANES):
            sq.append((sr, si))
            if (1 << m) < POW_ROWS:
                bit = ((row_p >> m) & 1) == 1
                pr, pi = (jnp.where(bit, pr * sr - pi * si, pr),
                          jnp.where(bit, pr * si + pi * sr, pi))
            sr, si = sr * sr - si * si, 2.0 * sr * si
        m0 = CHUNK.bit_length() - 1
        scan_r = jnp.concatenate([sq[m0 + j][0] for j in range(SUBLANES)], axis=0)
        scan_i = jnp.concatenate([sq[m0 + j][1] for j in range(SUBLANES)], axis=0)
        pad = jnp.zeros((LANES - POW_ROWS - SUBLANES, LANES), F32)
        tab_ref[q, 0] = jnp.concatenate([pr, scan_r, pad], axis=0).T
        tab_ref[q, 1] = jnp.concatenate([pi, scan_i, pad], axis=0).T
        arow_ref[q, 0] = pr[:SUBLANES]
        arow_ref[q, 1] = pi[:SUBLANES]

        def expand(tab, k0, sign):
            return jnp.concatenate(
                [jnp.broadcast_to(tab[k0 + sign * s:k0 + sign * s + 1, :], (S5_CH, LANES))
                 for s in range(CHUNK)], axis=0)

        tile = lambda v: jnp.concatenate([v] * CHUNK, axis=0)

        e_r, e_i = expand(pr, CHUNK - 1, -1), expand(pi, CHUNK - 1, -1)
        t_r, t_i = tile(bb_r), tile(bb_i)
        ws_r = _transpose_tiles(t_r * e_r - t_i * e_i)
        ws_i = _transpose_tiles(t_r * e_i + t_i * e_r)
        for h in range(2):
            rows = slice(h * S5_STATE, (h + 1) * S5_STATE)
            ws_ref[2 * q + h] = jnp.concatenate([ws_r[rows], ws_i[rows]], axis=0).astype(BF16)

        e_r, e_i = expand(pr, 1, 1), expand(pi, 1, 1)
        t_r, t_i = tile(c_r), tile(c_i)
        ca_r = t_r * e_r - t_i * e_i
        ca_n = -(t_r * e_i + t_i * e_r)
        wc_ref[2 * q] = jnp.where(lo, ca_r, pltpu.roll(ca_n, S5_STATE, 1)).astype(BF16)
        wc_ref[2 * q + 1] = jnp.where(lo, pltpu.roll(ca_r, S5_STATE, 1), ca_n).astype(BF16)

        e_r, e_i = expand(pr, 0, 1), expand(pi, 0, 1)
        c0_r = t_r * e_r - t_i * e_i
        c0_i = t_r * e_i + t_i * e_r

        for h in range(2):
            g = 2 * q + h
            mine = (lane16 < S5_STATE) == (h == 0)
            bm_r = jnp.where(mine, bb_r, 0.0)
            bm_i = jnp.where(mine, bb_i, 0.0)
            kt = _dot_nt(bm_r, c0_r, HIGHEST) - _dot_nt(bm_i, c0_i, HIGHEST)
            k_lo = kt[:, :LANES] + jnp.where(row16 == lane16, d_ref[g], 0.0)
            k_hi = kt[:, LANES:]
            blocks = []
            half = LANES // S5_CH
            for s in range(CHUNK):
                sh = (s % half) * S5_CH
                keep = lane16 >= sh
                r_lo = pltpu.roll(k_lo, sh, 1) if sh else k_lo
                r_hi = pltpu.roll(k_hi, sh, 1) if sh else k_hi
                if s < half:
                    blk = jnp.concatenate([jnp.where(keep, r_lo, 0.0),
                                           jnp.where(keep, r_hi, r_lo)], axis=1)
                else:
                    blk = jnp.concatenate([jnp.zeros_like(r_lo), jnp.where(keep, r_lo, 0.0)], axis=1)
                blocks.append(blk)
            wt_ref[g] = _transpose_tiles(jnp.concatenate(blocks, axis=0)).astype(BF16)

            r0 = (2 * q + h) * S5_CH
            c0 = q * 2 * LANES
            wb_ref[r0:r0 + S5_CH, c0:c0 + LANES] = bm_r.astype(BF16)
            wb_ref[r0:r0 + S5_CH, c0 + LANES:c0 + 2 * LANES] = bm_i.astype(BF16)
            wct_ref[q, r0:r0 + S5_CH, :LANES] = jnp.where(mine, c_r, 0.0).astype(BF16)
            wct_ref[q, r0:r0 + S5_CH, LANES:] = jnp.where(mine, -c_i, 0.0).astype(BF16)


def _s5_params(lam_re, lam_im, log_step, b_re, b_im, c_re, c_im, d, cast_w):
    groups = lam_re.shape[0]
    pairs = groups // 2
    slabs = groups // LANE_GROUPS
    cast_blk = pl.BlockSpec((cast_w.shape[0] // slabs, cast_w.shape[1]), lambda i: (i, 0))
    ls = jnp.broadcast_to(log_step[:, None], (groups, S5_STATE))
    lam = jnp.stack([lam_re, lam_im, ls]).reshape(3, pairs, 1, LANES)
    bc = jnp.stack([jnp.transpose(b_re, (0, 2, 1)), jnp.transpose(b_im, (0, 2, 1)), c_re, c_im])
    bc = jnp.transpose(bc.reshape(4, pairs, 2, S5_CH, S5_STATE), (0, 1, 3, 2, 4)).reshape(
        4, pairs, S5_CH, LANES)
    d_pad = jnp.pad(d.reshape(groups, 1, S5_CH), ((0, 0), (0, 0), (0, LANES - S5_CH)))
    blk3 = lambda i: (i, 0, 0)
    blk4 = lambda i: (i, 0, 0, 0)
    pblk = lambda i: (0, i, 0, 0)
    return pl.pallas_call(
        _s5_param_kernel,
        out_shape=(jax.ShapeDtypeStruct((groups, ROWS, ROWS), BF16),
                   jax.ShapeDtypeStruct((groups, 2 * S5_STATE, ROWS), BF16),
                   jax.ShapeDtypeStruct((groups, ROWS, 2 * S5_STATE), BF16),
                   jax.ShapeDtypeStruct((pairs, 2, LANES, LANES), F32),
                   jax.ShapeDtypeStruct((slabs, LANES, LANE_PAIRS * 2 * LANES), BF16),
                   jax.ShapeDtypeStruct((slabs, LANE_PAIRS, LANES, 2 * LANES), BF16),
                   jax.ShapeDtypeStruct((pairs, 2, SUBLANES, LANES), F32),
                   jax.ShapeDtypeStruct(cast_w.shape, BF16)),
        grid=(slabs,),
        in_specs=[pl.BlockSpec((3, LANE_PAIRS, 1, LANES), pblk),
                  pl.BlockSpec((4, LANE_PAIRS, S5_CH, LANES), pblk),
                  pl.BlockSpec((LANE_GROUPS, 1, LANES), blk3), cast_blk],
        out_specs=(pl.BlockSpec((LANE_GROUPS, ROWS, ROWS), blk3),
                   pl.BlockSpec((LANE_GROUPS, 2 * S5_STATE, ROWS), blk3),
                   pl.BlockSpec((LANE_GROUPS, ROWS, 2 * S5_STATE), blk3),
                   pl.BlockSpec((LANE_PAIRS, 2, LANES, LANES), blk4),
                   pl.BlockSpec((None, LANES, LANE_PAIRS * 2 * LANES), blk3),
                   pl.BlockSpec((None, LANE_PAIRS, LANES, 2 * LANES), blk4),
                   pl.BlockSpec((LANE_PAIRS, 2, SUBLANES, LANES), blk4),
                   cast_blk),
        compiler_params=_params("parallel"),
        name="s5_params",
    )(lam, bc, d_pad, cast_w)


def _s5_seq_kernel(z_ref, wt_ref, ws_ref, wc_ref, tab_ref, cast_ref, y_ref, hf_ref, castb_ref,
                   ut_ref, yt_ref, sr_ref, si_ref, *, nb, n_chunks):
    castb_ref[...] = cast_ref[...].astype(BF16)
    scan_steps = n_chunks.bit_length() - 1
    for n in range(nb):
        for s in range(CHUNK):
            xs = z_ref[n, pl.ds(s, n_chunks, stride=CHUNK), :]
            ut_ref[:, s * S5_CH:(s + 1) * S5_CH, n * n_chunks:(n + 1) * n_chunks] = (
                xs.T.astype(BF16).reshape(LANE_GROUPS, S5_CH, n_chunks))

    for g in range(LANE_GROUPS):
        st = _dot(ws_ref[g], ut_ref[g])
        sr_ref[g * S5_STATE:(g + 1) * S5_STATE, :] = st[:S5_STATE]
        si_ref[g * S5_STATE:(g + 1) * S5_STATE, :] = st[S5_STATE:]

    srows = LANE_GROUPS * S5_STATE
    lane = lax.broadcasted_iota(jnp.int32, (srows, n_chunks), 1)
    tab_r = jnp.concatenate([tab_ref[q, 0] for q in range(LANE_PAIRS)], axis=0)
    tab_i = jnp.concatenate([tab_ref[q, 1] for q in range(LANE_PAIRS)], axis=0)
    fin_r = jnp.zeros((srows, n_chunks), F32)
    fin_i = jnp.zeros((srows, n_chunks), F32)
    for n in range(nb):
        cols = slice(n * n_chunks, (n + 1) * n_chunks)
        xr = sr_ref[:, cols]
        xi = si_ref[:, cols]
        for k in range(scan_steps):
            d = 1 << k
            keep = lane >= d
            sr = jnp.where(keep, pltpu.roll(xr, d, 1), 0.0)
            si = jnp.where(keep, pltpu.roll(xi, d, 1), 0.0)
            pr = tab_r[:, SCAN_ROW0 + k:SCAN_ROW0 + k + 1]
            pi = tab_i[:, SCAN_ROW0 + k:SCAN_ROW0 + k + 1]
            xr, xi = xr + (pr * sr - pi * si), xi + (pr * si + pi * sr)
        fin_r = jnp.where(lane == n, pltpu.roll(xr, n + 1, 1), fin_r)
        fin_i = jnp.where(lane == n, pltpu.roll(xi, n + 1, 1), fin_i)
        keep = lane >= 1
        sr_ref[:, cols] = jnp.where(keep, pltpu.roll(xr, 1, 1), 0.0)
        si_ref[:, cols] = jnp.where(keep, pltpu.roll(xi, 1, 1), 0.0)

    for g in range(LANE_GROUPS):
        rows = slice(g * S5_STATE, (g + 1) * S5_STATE)
        hf_ref[g, 0] = fin_r[rows]
        hf_ref[g, 1] = fin_i[rows]
        hs = jnp.concatenate([sr_ref[rows, :], si_ref[rows, :]], axis=0).astype(BF16)
        yt_ref[g] = jax.nn.gelu(_dot(wc_ref[g], hs) + _dot(wt_ref[g], ut_ref[g]))

    for n in range(nb):
        for s in range(CHUNK):
            blk = yt_ref[:, s * S5_CH:(s + 1) * S5_CH, n * n_chunks:(n + 1) * n_chunks]
            y_ref[n, pl.ds(s, n_chunks, stride=CHUNK), :] = blk.reshape(LANES, n_chunks).T


def _s5_seq(z, w_t, w_s, w_c, tab, d_s5, cast_w):
    nb, t_len, _ = z.shape
    n_chunks = t_len // CHUNK
    assert n_chunks == LANES, "the chunk axis must fill one 128-lane tile"
    groups = d_s5 // S5_CH
    steps = groups // LANE_GROUPS
    cast_blk = pl.BlockSpec((cast_w.shape[0] // steps, cast_w.shape[1]), lambda i: (i, 0))
    kern = functools.partial(_s5_seq_kernel, nb=nb, n_chunks=n_chunks)
    blk3 = lambda i: (i, 0, 0)
    blk4 = lambda i: (i, 0, 0, 0)
    return pl.pallas_call(
        kern,
        out_shape=(jax.ShapeDtypeStruct((nb, t_len, d_s5), F32),
                   jax.ShapeDtypeStruct((groups, 2, S5_STATE, LANES), F32),
                   jax.ShapeDtypeStruct(cast_w.shape, BF16)),
        grid=(steps,),
        in_specs=[
            pl.BlockSpec((nb, t_len, LANES), lambda i: (0, 0, i)),
            pl.BlockSpec((LANE_GROUPS, ROWS, ROWS), blk3),
            pl.BlockSpec((LANE_GROUPS, 2 * S5_STATE, ROWS), blk3),
            pl.BlockSpec((LANE_GROUPS, ROWS, 2 * S5_STATE), blk3),
            pl.BlockSpec((LANE_PAIRS, 2, LANES, LANES), blk4),
            cast_blk,
        ],
        out_specs=(pl.BlockSpec((nb, t_len, LANES), lambda i: (0, 0, i)),
                   pl.BlockSpec((LANE_GROUPS, 2, S5_STATE, LANES), blk4),
                   cast_blk),
        scratch_shapes=[pltpu.VMEM((LANE_GROUPS, ROWS, nb * n_chunks), BF16),
                        pltpu.VMEM((LANE_GROUPS, ROWS, nb * n_chunks), F32),
                        pltpu.VMEM((LANE_GROUPS * S5_STATE, nb * n_chunks), F32),
                        pltpu.VMEM((LANE_GROUPS * S5_STATE, nb * n_chunks), F32)],
        compiler_params=_params("parallel"),
        name="s5_seq",
    )(z, w_t, w_s, w_c, tab, cast_w)


def _s5_step_kernel(u_ref, h0r_ref, h0i_ref, wb_ref, wct_ref, arow_ref, d_ref,
                    y_ref, hr_ref, hi_ref):
    u = u_ref[...]
    bu = _dot(u.astype(BF16), wb_ref[...])
    y = d_ref[...] * u
    for q in range(LANE_PAIRS):
        cols = slice(q * LANES, (q + 1) * LANES)
        ar = arow_ref[q, 0, 1:2, :]
        ai = arow_ref[q, 1, 1:2, :]
        h0r, h0i = h0r_ref[:, cols], h0i_ref[:, cols]
        hr = ar * h0r - ai * h0i + bu[:, 2 * q * LANES:(2 * q + 1) * LANES]
        hi = ar * h0i + ai * h0r + bu[:, (2 * q + 1) * LANES:(2 * q + 2) * LANES]
        hr_ref[:, cols] = hr
        hi_ref[:, cols] = hi
        y = y + _dot_nt(jnp.concatenate([hr, hi], axis=1).astype(BF16), wct_ref[q])
    y_ref[...] = jax.nn.gelu(y).astype(BF16)


def _s5_step(z, h0r, h0i, wb, wct, arow, d):
    n = z.shape[0]
    slabs = wb.shape[0]
    sw = LANE_PAIRS * LANES
    col = lambda i: (0, i)
    blk3 = lambda i: (i, 0, 0)
    blk4 = lambda i: (i, 0, 0, 0)
    return pl.pallas_call(
        _s5_step_kernel,
        out_shape=(jax.ShapeDtypeStruct((n, slabs * LANES), BF16),
                   jax.ShapeDtypeStruct((n, slabs * sw), F32),
                   jax.ShapeDtypeStruct((n, slabs * sw), F32)),
        grid=(slabs,),
        in_specs=[
            pl.BlockSpec((n, LANES), col),
            pl.BlockSpec((n, sw), col),
            pl.BlockSpec((n, sw), col),
            pl.BlockSpec((None, LANES, 2 * sw), blk3),
            pl.BlockSpec((None, LANE_PAIRS, LANES, 2 * LANES), blk4),
            pl.BlockSpec((LANE_PAIRS, 2, SUBLANES, LANES), blk4),
            pl.BlockSpec((1, LANES), col),
        ],
        out_specs=(pl.BlockSpec((n, LANES), col),
                   pl.BlockSpec((n, sw), col),
                   pl.BlockSpec((n, sw), col)),
        compiler_params=_params("parallel"),
        name="s5_step",
    )(z, h0r, h0i, wb, wct, arow, d)


def _lru_gate_block(xc, wa, ba, wi, bi, lam):
    xb16 = xc.astype(BF16)
    r = jax.nn.sigmoid(_dot(xb16, wa) + ba)
    ig = jax.nn.sigmoid(_dot(xb16, wi) + bi)
    log_a = -LRU_C * r * jax.nn.softplus(-lam)
    a = jnp.exp(log_a)
    mult = jnp.sqrt(1.0 - a * a)
    return a, mult * (ig * xc)


def _lru_gates(xc, wa_ref, ba, wi_ref, bi, lam):
    blk = xc.shape[1] // LRU_HEADS
    parts = [_lru_gate_block(xc[:, h * blk:(h + 1) * blk], wa_ref[h], ba[:, h * blk:(h + 1) * blk],
                             wi_ref[h], bi[:, h * blk:(h + 1) * blk],
                             lam[:, h * blk:(h + 1) * blk]) for h in range(LRU_HEADS)]
    return (jnp.concatenate([p[0] for p in parts], axis=-1),
            jnp.concatenate([p[1] for p in parts], axis=-1))


def _lru_seq_kernel(xb_ref, gb_ref, cw_ref, cb_ref, wa_ref, ba_ref, wi_ref, bi_ref, lam_ref,
                    gm_ref, cast_ref, o_ref, hl_ref, castb_ref, xe_ref, a_ref, b_ref, hc_ref,
                    *, tt):
    castb_ref[...] = cast_ref[...].astype(BF16)
    halo = SUBLANES
    nseq, _, d = xb_ref.shape
    blk = d // LRU_HEADS
    pitch = a_ref.shape[1] // nseq

    @pl.when(pl.program_id(0) == 0)
    def _():
        xe_ref[:, 0:halo, :] = jnp.zeros((nseq, halo, d), F32)
        hc_ref[...] = jnp.zeros_like(hc_ref)

    xe_ref[:, halo:halo + tt, :] = xb_ref[...]
    for h in range(LRU_HEADS):
        cols = slice(h * blk, (h + 1) * blk)
        cw = cw_ref[:, cols]
        xc = cb_ref[:, cols] + xe_ref[:, halo:halo + tt, cols] * cw[CONV_W - 1:CONV_W, :]
        for k in range(1, CONV_W):
            xc = xc + xe_ref[:, halo - k:halo - k + tt, cols] * cw[CONV_W - 1 - k:CONV_W - k, :]
        a, b = _lru_gate_block(xc.reshape(nseq * tt, blk), wa_ref[h], ba_ref[:, cols],
                               wi_ref[h], bi_ref[:, cols], lam_ref[:, cols])
        for n in range(nseq):
            for s in range(blk // LANES):
                slab = h * (blk // LANES) + s
                rows = slice(n * pitch, n * pitch + tt)
                a_ref[slab, rows, :] = a[n * tt:(n + 1) * tt, s * LANES:(s + 1) * LANES]
                b_ref[slab, rows, :] = b[n * tt:(n + 1) * tt, s * LANES:(s + 1) * LANES]
    xe_ref[:, 0:halo, :] = xb_ref[:, tt - halo:tt, :]

    slabs = d // LANES

    def block(i, hs):
        base = pl.multiple_of(i * SUBLANES, SUBLANES)
        hs = list(hs)
        for j in range(SUBLANES):
            rows = pl.ds(base + j, nseq, stride=pitch)
            for s in range(slabs):
                hs[s] = a_ref[s, rows, :] * hs[s] + b_ref[s, rows, :]
                b_ref[s, rows, :] = hs[s]
        return tuple(hs)

    hs = lax.fori_loop(0, tt // SUBLANES, block, tuple(hc_ref[s] for s in range(slabs)))
    for s in range(slabs):
        hc_ref[s] = hs[s]
        hl_ref[s] = hs[s]

    h_all = jnp.stack([jnp.concatenate([b_ref[s, n * pitch:n * pitch + tt, :] for s in range(slabs)],
                                       axis=-1) for n in range(nseq)])
    out = h_all * jax.nn.gelu(gb_ref[...])
    o_ref[...] = _rms(out, gm_ref[...]).astype(BF16)


def _lru_seq(z, conv_w, conv_b, w_a, b_a, w_i, b_i, lam, g_merge, cast_w, tt):
    nseq, t_len, _ = z.shape
    d = conv_w.shape[1]
    steps = t_len // tt
    cast_blk = pl.BlockSpec((cast_w.shape[0] // steps, cast_w.shape[1]), lambda t: (t, 0))
    row = lambda v: v.reshape(1, d)
    const2 = lambda t: (0, 0)
    const3 = lambda t: (0, 0, 0)
    kern = functools.partial(_lru_seq_kernel, tt=tt)
    return pl.pallas_call(
        kern,
        out_shape=(jax.ShapeDtypeStruct((nseq, t_len, d), BF16),
                   jax.ShapeDtypeStruct((d // LANES, nseq, LANES), F32),
                   jax.ShapeDtypeStruct(cast_w.shape, BF16)),
        grid=(steps,),
        in_specs=[
            pl.BlockSpec((nseq, tt, d), lambda t: (0, t, 1)),
            pl.BlockSpec((nseq, tt, d), lambda t: (0, t, 2)),
            pl.BlockSpec((CONV_W, d), const2),
            pl.BlockSpec((1, d), const2),
            pl.BlockSpec(w_a.shape, const3),
            pl.BlockSpec((1, d), const2),
            pl.BlockSpec(w_i.shape, const3),
            pl.BlockSpec((1, d), const2),
            pl.BlockSpec((1, d), const2),
            pl.BlockSpec((1, d), const2),
            cast_blk,
        ],
        out_specs=(pl.BlockSpec((nseq, tt, d), lambda t: (0, t, 0)),
                   pl.BlockSpec((d // LANES, nseq, LANES), const3),
                   cast_blk),
        scratch_shapes=[pltpu.VMEM((nseq, tt + SUBLANES, d), F32),
                        pltpu.VMEM((d // LANES, nseq * (tt + SUBLANES), LANES), F32),
                        pltpu.VMEM((d // LANES, nseq * (tt + SUBLANES), LANES), F32),
                        pltpu.VMEM((d // LANES, nseq, LANES), F32)],
        compiler_params=_params("arbitrary"),
        name="rglru_seq",
    )(z, z, conv_w, row(conv_b), w_a, row(b_a), w_i, row(b_i), row(lam), row(g_merge), cast_w)


def _lru_step_kernel(xb_ref, gb_ref, c0_ref, c1_ref, c2_ref, h0_ref, cw_ref, cb_ref,
                     wa_ref, ba_ref, wi_ref, bi_ref, lam_ref, gm_ref, o_ref, h_ref, cv_ref):
    d = xb_ref.shape[1]
    xb = xb_ref[...]
    cw = cw_ref[...]
    xc = (cb_ref[...] + c0_ref[...] * cw[0:1, :] + c1_ref[...] * cw[1:2, :]
          + c2_ref[...] * cw[2:3, :] + xb * cw[3:4, :])
    a, b = _lru_gates(xc, wa_ref, ba_ref[...], wi_ref, bi_ref[...], lam_ref[...])
    h = a * h0_ref[...] + b
    h_ref[...] = h
    out = h * jax.nn.gelu(gb_ref[...])
    o_ref[...] = _rms(out, gm_ref[...]).astype(BF16)
    cv_ref[:, 0:d] = c1_ref[...]
    cv_ref[:, d:2 * d] = c2_ref[...]
    cv_ref[:, 2 * d:3 * d] = xb


def _lru_step(z, conv0, h0, conv_w, conv_b, w_a, b_a, w_i, b_i, lam, g_merge):
    n = z.shape[0]
    d = conv_w.shape[1]
    taps = CONV_W - 1
    conv2d = conv0.reshape(n, taps * d)
    row = lambda v: v.reshape(1, d)
    full = lambda shape: pl.BlockSpec(shape, lambda i: (0,) * len(shape))
    col = lambda c: pl.BlockSpec((n, d), lambda i: (0, c))
    nd = full((n, d))
    rd = full((1, d))
    o, h, cv = pl.pallas_call(
        _lru_step_kernel,
        out_shape=(jax.ShapeDtypeStruct((n, d), BF16), jax.ShapeDtypeStruct((n, d), F32),
                   jax.ShapeDtypeStruct((n, taps * d), F32)),
        grid=(1,),
        in_specs=[
            col(1), col(2), col(0), col(1), col(2), nd,
            full((CONV_W, d)), rd,
            full(w_a.shape), rd, full(w_i.shape), rd, rd, rd,
        ],
        out_specs=(nd, nd, full((n, taps * d))),
        compiler_params=_params("arbitrary"),
        name="rglru_step",
    )(z, z, conv2d, conv2d, conv2d, h0, conv_w, row(conv_b),
      w_a, row(b_a), w_i, row(b_i), row(lam), row(g_merge))
    return o, h, cv.reshape(n, taps, d)


def _glu_rows(y, w_ref, b_ref, g_ref):
    gate = jax.nn.sigmoid(_dot(y.astype(BF16), w_ref[...]) + b_ref[...])
    return _rms(y.astype(F32) * gate, g_ref[...]).astype(BF16)


def _glu_kernel(y_ref, ys_ref, w_ref, b_ref, g_ref, o_ref, os_ref):
    for rows in _row_parts(y_ref.shape[0], 4):
        o_ref[rows, :] = _glu_rows(y_ref[rows, :], w_ref, b_ref, g_ref)

    @_on_last_row_tile
    def _():
        os_ref[...] = _glu_rows(ys_ref[...], w_ref, b_ref, g_ref)


def _glu(y, ys, w, b, g, tm):
    m, d = y.shape
    ms = ys.shape[0]
    const = lambda i: (0, 0)
    return pl.pallas_call(
        _glu_kernel,
        out_shape=(jax.ShapeDtypeStruct((m, d), BF16), jax.ShapeDtypeStruct((ms, d), BF16)),
        grid=(m // tm,),
        in_specs=[
            pl.BlockSpec((tm, d), lambda i: (i, 0)),
            pl.BlockSpec((ms, d), const),
            pl.BlockSpec((d, d), const),
            pl.BlockSpec((1, d), const),
            pl.BlockSpec((1, d), const),
        ],
        out_specs=(pl.BlockSpec((tm, d), lambda i: (i, 0)), pl.BlockSpec((ms, d), const)),
        compiler_params=_params("arbitrary"),
        name="s5_glu",
    )(y, ys, w, b.reshape(1, d), g.reshape(1, d))


def _out_proj_kernel(x_ref, ma_ref, mb_ref, xs_ref, mas_ref, mbs_ref, w_ref, o_ref, os_ref, wb_ref):
    @pl.when(pl.program_id(0) == 0)
    def _():
        wb_ref[...] = w_ref[...].astype(BF16)

    ka = ma_ref.shape[1]
    proj = lambda x, ma, mb: x + _dot(ma, wb_ref[:ka, :]) + _dot(mb, wb_ref[ka:, :])
    for rows in _row_parts(x_ref.shape[0], 2):
        o_ref[rows, :] = proj(x_ref[rows, :], ma_ref[rows, :], mb_ref[rows, :])

    @_on_last_row_tile
    def _():
        os_ref[...] = proj(xs_ref[...], mas_ref[...], mbs_ref[...])


def _out_proj(x, ma, mb, xs, mas, mbs, w, tm):
    m, n = x.shape
    ms = xs.shape[0]
    ka = ma.shape[1]
    kb = mb.shape[1]
    const = lambda i: (0, 0)
    tile = lambda i: (i, 0)
    return pl.pallas_call(
        _out_proj_kernel,
        out_shape=(jax.ShapeDtypeStruct((m, n), F32), jax.ShapeDtypeStruct((ms, n), F32)),
        grid=(m // tm,),
        in_specs=[
            pl.BlockSpec((tm, n), tile),
            pl.BlockSpec((tm, ka), tile),
            pl.BlockSpec((tm, kb), tile),
            pl.BlockSpec((ms, n), const),
            pl.BlockSpec((ms, ka), const),
            pl.BlockSpec((ms, kb), const),
            pl.BlockSpec((ka + kb, n), const, pipeline_mode=pl.Buffered(1)),
        ],
        out_specs=(pl.BlockSpec((tm, n), tile), pl.BlockSpec((ms, n), const)),
        scratch_shapes=[pltpu.VMEM((ka + kb, n), BF16)],
        compiler_params=_params("arbitrary"),
        name="out_proj",
    )(x, ma, mb, xs, mas, mbs, w)


def _mlp_kernel(x_ref, xs_ref, g_ref, wu_ref, wd_ref, o_ref, os_ref, h_ref, hs_ref):
    def ffn(h):
        act = jnp.square(jnp.maximum(_dot(h, wu_ref[...]), 0.0)).astype(BF16)
        return _dot(act, wd_ref[...])

    def rows(x_ref, o_ref, h_ref):
        first = pl.program_id(1) == 0

        @pl.when(first)
        def _():
            for part in _row_parts(x_ref.shape[0], 4):
                x = x_ref[part, :]
                h = _rms(x, g_ref[...]).astype(BF16)
                h_ref[part, :] = h
                o_ref[part, :] = x + ffn(h)

        @pl.when(jnp.logical_not(first))
        def _():
            o_ref[...] += ffn(h_ref[...])

    rows(x_ref, o_ref, h_ref)
    _on_last_row_tile(lambda: rows(xs_ref, os_ref, hs_ref))


def _mlp(x, xs, g, w_up, w_down, tm, tf):
    m, d = x.shape
    ms = xs.shape[0]
    f = w_up.shape[1]
    const = lambda i, j: (0, 0)
    return pl.pallas_call(
        _mlp_kernel,
        out_shape=(jax.ShapeDtypeStruct((m, d), F32), jax.ShapeDtypeStruct((ms, d), F32)),
        grid=(m // tm, f // tf),
        in_specs=[
            pl.BlockSpec((tm, d), lambda i, j: (i, 0)),
            pl.BlockSpec((ms, d), const),
            pl.BlockSpec((1, d), const),
            pl.BlockSpec((d, tf), lambda i, j: (0, j)),
            pl.BlockSpec((tf, d), lambda i, j: (j, 0)),
        ],
        out_specs=(pl.BlockSpec((tm, d), lambda i, j: (i, 0)), pl.BlockSpec((ms, d), const)),
        scratch_shapes=[pltpu.VMEM((tm, d), BF16), pltpu.VMEM((ms, d), BF16)],
        compiler_params=_params("arbitrary", "arbitrary"),
        name="mlp",
    )(x, xs, g.reshape(1, d), w_up, w_down)


def _ple_kernel(x_ref, p_ref, xs_ref, ps_ref, g_ref, wg_ref, wp_ref, gf_ref, o_ref, os_ref,
                wgb_ref, wpb_ref, *, final):
    @pl.when(pl.program_id(0) == 0)
    def _():
        wgb_ref[...] = wg_ref[...].astype(BF16)
        wpb_ref[...] = wp_ref[...].astype(BF16)

    def ple(x, p):
        gate = jax.nn.sigmoid(_dot(_rms(x, g_ref[...]).astype(BF16), wgb_ref[...]))
        pe = _dot(p.astype(BF16), wpb_ref[...])
        x = x + pe * gate
        return _rms(x, gf_ref[...]) if final else x

    for rows in _row_parts(x_ref.shape[0], 2):
        o_ref[rows, :] = ple(x_ref[rows, :], p_ref[rows, :])

    @_on_last_row_tile
    def _():
        os_ref[...] = ple(xs_ref[...], ps_ref[...])


def _ple(x, p, xs, ps, g, w_gate, w_ple, g_final, tm, final):
    m, d = x.shape
    ms = xs.shape[0]
    dp = p.shape[1]
    once = pl.Buffered(1)
    const = lambda i: (0, 0)
    tile = lambda i: (i, 0)
    return pl.pallas_call(
        functools.partial(_ple_kernel, final=final),
        out_shape=(jax.ShapeDtypeStruct((m, d), F32), jax.ShapeDtypeStruct((ms, d), F32)),
        grid=(m // tm,),
        in_specs=[
            pl.BlockSpec((tm, d), tile),
            pl.BlockSpec((tm, dp), tile),
            pl.BlockSpec((ms, d), const),
            pl.BlockSpec((ms, dp), const),
            pl.BlockSpec((1, d), const),
            pl.BlockSpec((d, d), const, pipeline_mode=once),
            pl.BlockSpec((dp, d), const, pipeline_mode=once),
            pl.BlockSpec((1, d), const),
        ],
        out_specs=(pl.BlockSpec((tm, d), tile), pl.BlockSpec((ms, d), const)),
        scratch_shapes=[pltpu.VMEM((d, d), BF16), pltpu.VMEM((dp, d), BF16)],
        compiler_params=_params("arbitrary"),
        name="ple_final",
    )(x, p, xs, ps, g.reshape(1, d), w_gate, w_ple, g_final.reshape(1, d))


TT_LRU = 64
TM_IN, TN_IN = 1024, 1536
TM_GLU = 1024
TM_OUT = 512
TM_MLP, TF_MLP = 512, 1024
TM_PLE = 512


def kernel(x_prompt, x_sample, state_s5_re, state_s5_im, state_lru, state_conv, p_prompt, p_sample,
           g_mix, w_in, s5_lam_re, s5_lam_im, s5_log_step, s5_b_re, s5_b_im, s5_c_re, s5_c_im, s5_d,
           s5_w_glu, s5_b_glu, conv_w, conv_b, lru_w_a, lru_b_a, lru_w_i, lru_b_i, lru_lam,
           g_merge_a, g_merge_b, w_out, g_mlp, w_up, w_down, g_ple, w_ple_gate, w_ple, g_final):
    depth = g_mix.shape[0]
    nb, t_len, d_model = x_prompt.shape
    ns = x_sample.shape[0]
    d_s5 = s5_d.shape[1]
    d_lru = conv_w.shape[2]
    groups = d_s5 // S5_CH

    xp = x_prompt.reshape(nb * t_len, d_model)
    xs = x_sample.reshape(ns, d_model)
    outs = [[] for _ in range(8)]
    for l in range(depth):
        final = l == depth - 1
        w_glu_b = s5_w_glu[l].astype(BF16)
        lru_args = (conv_w[l], conv_b[l], lru_w_a[l].astype(BF16), lru_b_a[l],
                    lru_w_i[l].astype(BF16), lru_b_i[l], lru_lam[l], g_merge_b[l])
        w_t, w_s, w_c, tab, wb, wct, arow, w_in_b = _s5_params(
            s5_lam_re[l], s5_lam_im[l], s5_log_step[l], s5_b_re[l], s5_b_im[l],
            s5_c_re[l], s5_c_im[l], s5_d[l], w_in[l])

        z, zs = _norm_matmul(xp, xs, g_mix[l], w_in_b, TM_IN, TN_IN)
        z4 = z.reshape(nb, t_len, 3 * d_s5)
        y, hf, w_down_b = _s5_seq(z4, w_t, w_s, w_c, tab, d_s5, w_down[l])
        mb, lru_h, w_up_b = _lru_seq(z4, *lru_args, w_up[l], TT_LRU)
        ys, hsr, hsi = _s5_step(zs, state_s5_re[l].reshape(ns, groups * S5_STATE),
                                state_s5_im[l].reshape(ns, groups * S5_STATE),
                                wb, wct, arow, s5_d[l].reshape(1, d_s5))
        mbs, lru_hs, conv_s = _lru_step(zs, state_conv[l], state_lru[l], *lru_args)
        ma, mas = _glu(y.reshape(nb * t_len, d_s5), ys, w_glu_b, s5_b_glu[l], g_merge_a[l], TM_GLU)
        xp, xs = _out_proj(xp, ma, mb.reshape(nb * t_len, d_lru), xs, mas, mbs, w_out[l], TM_OUT)
        xp, xs = _mlp(xp, xs, g_mlp[l], w_up_b, w_down_b, TM_MLP, TF_MLP)
        xp, xs = _ple(xp, p_prompt[l].reshape(nb * t_len, -1), xs, p_sample[l].reshape(ns, -1),
                      g_ple[l], w_ple_gate[l], w_ple[l], g_final, TM_PLE, final)
        hf = jnp.transpose(hf[:, :, :, :nb], (1, 3, 0, 2))
        outs[0].append(hf[0])
        outs[1].append(hf[1])
        outs[2].append(jnp.transpose(lru_h, (1, 0, 2)).reshape(nb, d_lru))
        outs[3].append(z4[:, t_len - (CONV_W - 1):, d_s5:d_s5 + d_lru])
        outs[4].append(hsr.reshape(ns, groups, S5_STATE))
        outs[5].append(hsi.reshape(ns, groups, S5_STATE))
        outs[6].append(lru_hs)
        outs[7].append(conv_s)
    return (xp.reshape(nb, t_len, d_model), xs.reshape(ns, 1, d_model),
            *(jnp.stack(o) for o in outs))
```
